```python
import math
import jax, jax.numpy as jnp
from jax import lax
import numpy as np

D_MODEL = 2048
BATCH = 8
SEQ = 8192
DEPTH = 1

MIX_WIDTH = D_MODEL
CONV_WIDTH = MIX_WIDTH // 2
CONV_GROUPS = 8
MLA_HEADS = 8
QK_NOPE_DIM = 128
QK_ROPE_DIM = 64
V_HEAD_DIM = 128
Q_LORA_RANK = 768
KV_LORA_RANK = 512
ROPE_THETA = 10000.0
Q_BLOCK = 128
D_FF = 5632
CONV_K = 3
RMS_EPS = 1e-6
N_MOD = 6

IN_SPLITS = (Q_LORA_RANK, KV_LORA_RANK, QK_ROPE_DIM, CONV_WIDTH, CONV_WIDTH, CONV_WIDTH)
IN_COLS = sum(IN_SPLITS)

kernel_name = "hybrid_mla_shortconv_convffn_adaln"


def rms_norm(x, g):
    xf = x.astype(jnp.float32)
    y = xf * lax.rsqrt(jnp.mean(xf * xf, axis=-1, keepdims=True) + RMS_EPS)
    return (y * g.astype(jnp.float32)).astype(x.dtype)


def rope(x, cos, sin):
    x1, x2 = jnp.split(x, 2, axis=-1)
    return jnp.concatenate([x1 * cos - x2 * sin, x2 * cos + x1 * sin], axis=-1)


def causal_dwconv3(u, w, b):
    s = u.shape[1]
    up = jnp.pad(u, ((0, 0), (CONV_K - 1, 0), (0, 0)))
    return up[:, :s] * w[0] + up[:, 1:s + 1] * w[1] + u * w[2] + b


def mla_attention(q_nope, q_rope, k_nope, k_rope, v):
    b, s, h, _ = q_nope.shape
    nb = s // Q_BLOCK
    scale = 1.0 / math.sqrt(QK_NOPE_DIM + QK_ROPE_DIM)
    k_idx = jnp.arange(s)
    neg = jnp.finfo(jnp.float32).min

    def blockify(t):
        return t.reshape(b, nb, Q_BLOCK, *t.shape[2:]).swapaxes(0, 1)

    def one_block(args):
        qn, qr, i = args
        sc = (jnp.einsum('bqhd,bkhd->bhqk', qn, k_nope)
              + jnp.einsum('bqhd,bkd->bhqk', qr, k_rope)).astype(jnp.float32) * scale
        q_idx = i * Q_BLOCK + jnp.arange(Q_BLOCK)
        mask = k_idx[None, :] <= q_idx[:, None]
        sc = jnp.where(mask, sc, neg)
        p = jax.nn.softmax(sc, axis=-1).astype(v.dtype)
        return jnp.einsum('bhqk,bkhd->bqhd', p, v)

    out = lax.map(one_block, (blockify(q_nope), blockify(q_rope), jnp.arange(nb)))
    return out.swapaxes(0, 1).reshape(b, s, h * V_HEAD_DIM)


def _fwd_setup_inputs(seed: int = 0) -> dict:
    key = jax.random.key(seed)
    ks = jax.random.split(key, 24)
    f32 = jnp.float32

    def nrm(k, shape, fan_in):
        return jax.random.normal(k, shape, f32) * (fan_in ** -0.5)

    def gain(k, n):
        return 1.0 + 0.02 * jax.random.normal(k, (DEPTH, n), f32)

    x = jax.random.normal(ks[0], (BATCH, SEQ, D_MODEL), f32)
    c = jax.random.normal(ks[1], (BATCH, D_MODEL), f32)
    offset = jax.random.randint(ks[2], (BATCH, 1), 0, 1024, dtype=jnp.int32)
    positions = offset + jnp.arange(SEQ, dtype=jnp.int32)[None, :]
    return {
        "x": x,
        "c": c,
        "positions": positions,
        "w_ada": nrm(ks[3], (DEPTH, D_MODEL, N_MOD * D_MODEL), D_MODEL),
        "b_ada": 0.02 * jax.random.normal(ks[4], (DEPTH, N_MOD * D_MODEL), f32),
        "g_pre_mix": gain(ks[5], D_MODEL),
        "g_post_mix": gain(ks[6], D_MODEL),
        "w_in": nrm(ks[7], (DEPTH, D_MODEL, IN_COLS), D_MODEL),
        "g_q": gain(ks[8], Q_LORA_RANK),
        "w_uq": nrm(ks[9], (DEPTH, Q_LORA_RANK, MLA_HEADS * (QK_NOPE_DIM + QK_ROPE_DIM)), Q_LORA_RANK),
        "g_kv": gain(ks[10], KV_LORA_RANK),
        "w_ukv": nrm(ks[11], (DEPTH, KV_LORA_RANK, MLA_HEADS * (QK_NOPE_DIM + V_HEAD_DIM)), KV_LORA_RANK),
        "conv_w_mix": nrm(ks[12], (DEPTH, CONV_K, CONV_WIDTH), CONV_K),
        "conv_b_mix": 0.02 * jax.random.normal(ks[13], (DEPTH, CONV_WIDTH), f32),
        "w_o": nrm(ks[14], (DEPTH, MIX_WIDTH, D_MODEL), MIX_WIDTH),
        "g_pre_ffn": gain(ks[15], D_MODEL),
        "g_post_ffn": gain(ks[16], D_MODEL),
        "w_up": nrm(ks[17], (DEPTH, D_MODEL, 2 * D_FF), D_MODEL),
        "conv_w_ffn": nrm(ks[18], (DEPTH, CONV_K, 2 * D_FF), CONV_K),
        "conv_b_ffn": 0.02 * jax.random.normal(ks[19], (DEPTH, 2 * D_FF), f32),
        "w_down": nrm(ks[20], (DEPTH, D_FF, D_MODEL), D_FF),
    }


def _fwd_reference(x, c, positions, w_ada, b_ada, g_pre_mix, g_post_mix, w_in, g_q, w_uq,
              g_kv, w_ukv, conv_w_mix, conv_b_mix, w_o, g_pre_ffn, g_post_ffn,
              w_up, conv_w_ffn, conv_b_ffn, w_down):
    b, s, _ = x.shape
    inv_freq = 1.0 / (ROPE_THETA ** (jnp.arange(0, QK_ROPE_DIM, 2, dtype=jnp.float32) / QK_ROPE_DIM))
    ang = positions.astype(jnp.float32)[..., None] * inv_freq
    cos = jnp.cos(ang).astype(x.dtype)
    sin = jnp.sin(ang).astype(x.dtype)
    c_act = jax.nn.silu(c)
    cut = np.cumsum(IN_SPLITS)[:-1].tolist()

    for l in range(DEPTH):
        mod = c_act @ w_ada[l] + b_ada[l]
        sh_m, sc_m, gt_m, sh_f, sc_f, gt_f = [m[:, None, :] for m in jnp.split(mod, N_MOD, axis=-1)]

        h = rms_norm(x, g_pre_mix[l]) * (1.0 + sc_m) + sh_m
        proj = h @ w_in[l]
        q_lat, kv_lat, k_rope, gate_b, gate_c, conv_in = jnp.split(proj, cut, axis=-1)

        q = (rms_norm(q_lat, g_q[l]) @ w_uq[l]).reshape(b, s, MLA_HEADS, QK_NOPE_DIM + QK_ROPE_DIM)
        q_nope, q_rope = q[..., :QK_NOPE_DIM], q[..., QK_NOPE_DIM:]
        q_rope = rope(q_rope, cos[:, :, None, :], sin[:, :, None, :])
        k_rope = rope(k_rope, cos, sin)
        kv = (rms_norm(kv_lat, g_kv[l]) @ w_ukv[l]).reshape(b, s, MLA_HEADS, QK_NOPE_DIM + V_HEAD_DIM)
        k_nope, v = kv[..., :QK_NOPE_DIM], kv[..., QK_NOPE_DIM:]
        attn_out = mla_attention(q_nope, q_rope, k_nope, k_rope, v)

        conv_out = gate_b * causal_dwconv3(gate_c * conv_in, conv_w_mix[l], conv_b_mix[l])

        mix = jnp.concatenate([attn_out, conv_out], axis=-1) @ w_o[l]
        x = x + gt_m * rms_norm(mix, g_post_mix[l])

        h = rms_norm(x, g_pre_ffn[l]) * (1.0 + sc_f) + sh_f
        u = causal_dwconv3(h @ w_up[l], conv_w_ffn[l], conv_b_ffn[l])
        a, g = jnp.split(u, 2, axis=-1)
        y = (jax.nn.silu(g) * a) @ w_down[l]
        x = x + gt_f * rms_norm(y, g_post_ffn[l])
    return x


import jax as _jax
import jax.numpy as _jnp

TWIN_FORMAT = 'train_step'
FWD_PARAMS = ['x', 'c', 'positions', 'w_ada', 'b_ada', 'g_pre_mix', 'g_post_mix', 'w_in', 'g_q', 'w_uq', 'g_kv', 'w_ukv', 'conv_w_mix', 'conv_b_mix', 'w_o', 'g_pre_ffn', 'g_post_ffn', 'w_up', 'conv_w_ffn', 'conv_b_ffn', 'w_down']
TWIN_WEIGHTS = ['w_ada', 'b_ada', 'g_pre_mix', 'g_post_mix', 'w_in', 'g_q', 'w_uq', 'g_kv', 'w_ukv', 'conv_w_mix', 'conv_b_mix', 'w_o', 'g_pre_ffn', 'g_post_ffn', 'w_up', 'conv_w_ffn', 'conv_b_ffn', 'w_down']
TWIN_DIFF_INPUT = 'x'
TWIN_INPUTS = ['x', 'c', 'positions', 'w_ada', 'b_ada', 'g_pre_mix', 'g_post_mix', 'w_in', 'g_q', 'w_uq', 'g_kv', 'w_ukv', 'conv_w_mix', 'conv_b_mix', 'w_o', 'g_pre_ffn', 'g_post_ffn', 'w_up', 'conv_w_ffn', 'conv_b_ffn', 'w_down', 'loss_target', 'm_w_ada', 'm_b_ada', 'm_g_pre_mix', 'm_g_post_mix', 'm_w_in', 'm_g_q', 'm_w_uq', 'm_g_kv', 'm_w_ukv', 'm_conv_w_mix', 'm_conv_b_mix', 'm_w_o', 'm_g_pre_ffn', 'm_g_post_ffn', 'm_w_up', 'm_conv_w_ffn', 'm_conv_b_ffn', 'm_w_down', 'v_w_ada', 'v_b_ada', 'v_g_pre_mix', 'v_g_post_mix', 'v_w_in', 'v_g_q', 'v_w_uq', 'v_g_kv', 'v_w_ukv', 'v_conv_w_mix', 'v_conv_b_mix', 'v_w_o', 'v_g_pre_ffn', 'v_g_post_ffn', 'v_w_up', 'v_conv_w_ffn', 'v_conv_b_ffn', 'v_w_down']
TWIN_OUTPUTS = ['loss', 'grad_x', 'grad_w_ada', 'grad_b_ada', 'grad_g_pre_mix', 'grad_g_post_mix', 'grad_w_in', 'grad_g_q', 'grad_w_uq', 'grad_g_kv', 'grad_w_ukv', 'grad_conv_w_mix', 'grad_conv_b_mix', 'grad_w_o', 'grad_g_pre_ffn', 'grad_g_post_ffn', 'grad_w_up', 'grad_conv_w_ffn', 'grad_conv_b_ffn', 'grad_w_down', 'delta_w_ada', 'delta_b_ada', 'delta_g_pre_mix', 'delta_g_post_mix', 'delta_w_in', 'delta_g_q', 'delta_w_uq', 'delta_g_kv', 'delta_w_ukv', 'delta_conv_w_mix', 'delta_conv_b_mix', 'delta_w_o', 'delta_g_pre_ffn', 'delta_g_post_ffn', 'delta_w_up', 'delta_conv_w_ffn', 'delta_conv_b_ffn', 'delta_w_down', 'new_m_w_ada', 'new_m_b_ada', 'new_m_g_pre_mix', 'new_m_g_post_mix', 'new_m_w_in', 'new_m_g_q', 'new_m_w_uq', 'new_m_g_kv', 'new_m_w_ukv', 'new_m_conv_w_mix', 'new_m_conv_b_mix', 'new_m_w_o', 'new_m_g_pre_ffn', 'new_m_g_post_ffn', 'new_m_w_up', 'new_m_conv_w_ffn', 'new_m_conv_b_ffn', 'new_m_w_down', 'new_v_w_ada', 'new_v_b_ada', 'new_v_g_pre_mix', 'new_v_g_post_mix', 'new_v_w_in', 'new_v_g_q', 'new_v_w_uq', 'new_v_g_kv', 'new_v_w_ukv', 'new_v_conv_w_mix', 'new_v_conv_b_mix', 'new_v_w_o', 'new_v_g_pre_ffn', 'new_v_g_post_ffn', 'new_v_w_up', 'new_v_conv_w_ffn', 'new_v_conv_b_ffn', 'new_v_w_down']
TWIN_LEAF_KINDS = {'loss': 'loss', 'grad_x': 'grad_x', 'grad_w_ada': 'grad_w', 'grad_b_ada': 'grad_w', 'grad_g_pre_mix': 'grad_w', 'grad_g_post_mix': 'grad_w', 'grad_w_in': 'grad_w', 'grad_g_q': 'grad_w', 'grad_w_uq': 'grad_w', 'grad_g_kv': 'grad_w', 'grad_w_ukv': 'grad_w', 'grad_conv_w_mix': 'grad_w', 'grad_conv_b_mix': 'grad_w', 'grad_w_o': 'grad_w', 'grad_g_pre_ffn': 'grad_w', 'grad_g_post_ffn': 'grad_w', 'grad_w_up': 'grad_w', 'grad_conv_w_ffn': 'grad_w', 'grad_conv_b_ffn': 'grad_w', 'grad_w_down': 'grad_w', 'delta_w_ada': 'delta_w', 'delta_b_ada': 'delta_w', 'delta_g_pre_mix': 'delta_w', 'delta_g_post_mix': 'delta_w', 'delta_w_in': 'delta_w', 'delta_g_q': 'delta_w', 'delta_w_uq': 'delta_w', 'delta_g_kv': 'delta_w', 'delta_w_ukv': 'delta_w', 'delta_conv_w_mix': 'delta_w', 'delta_conv_b_mix': 'delta_w', 'delta_w_o': 'delta_w', 'delta_g_pre_ffn': 'delta_w', 'delta_g_post_ffn': 'delta_w', 'delta_w_up': 'delta_w', 'delta_conv_w_ffn': 'delta_w', 'delta_conv_b_ffn': 'delta_w', 'delta_w_down': 'delta_w', 'new_m_w_ada': 'new_m', 'new_m_b_ada': 'new_m', 'new_m_g_pre_mix': 'new_m', 'new_m_g_post_mix': 'new_m', 'new_m_w_in': 'new_m', 'new_m_g_q': 'new_m', 'new_m_w_uq': 'new_m', 'new_m_g_kv': 'new_m', 'new_m_w_ukv': 'new_m', 'new_m_conv_w_mix': 'new_m', 'new_m_conv_b_mix': 'new_m', 'new_m_w_o': 'new_m', 'new_m_g_pre_ffn': 'new_m', 'new_m_g_post_ffn': 'new_m', 'new_m_w_up': 'new_m', 'new_m_conv_w_ffn': 'new_m', 'new_m_conv_b_ffn': 'new_m', 'new_m_w_down': 'new_m', 'new_v_w_ada': 'new_v', 'new_v_b_ada': 'new_v', 'new_v_g_pre_mix': 'new_v', 'new_v_g_post_mix': 'new_v', 'new_v_w_in': 'new_v', 'new_v_g_q': 'new_v', 'new_v_w_uq': 'new_v', 'new_v_g_kv': 'new_v', 'new_v_w_ukv': 'new_v', 'new_v_conv_w_mix': 'new_v', 'new_v_conv_b_mix': 'new_v', 'new_v_w_o': 'new_v', 'new_v_g_pre_ffn': 'new_v', 'new_v_g_post_ffn': 'new_v', 'new_v_w_up': 'new_v', 'new_v_conv_w_ffn': 'new_v', 'new_v_conv_b_ffn': 'new_v', 'new_v_w_down': 'new_v'}


def _forward(args):
    return _fwd_reference(*[args[k] for k in FWD_PARAMS])


def _output_shape():
    def fwd():
        inp = _fwd_setup_inputs(0)
        return _fwd_reference(*[inp[k] for k in FWD_PARAMS])
    out = _jax.eval_shape(fwd)
    return out.shape, out.dtype

N_MICROBATCH = 1
ADAM_LR = 0.001
ADAM_B1 = 0.9
ADAM_B2 = 0.999
ADAM_EPS = 1e-08
ADAM_WD = 0.01
ADAM_STEP = 10
PER_EXAMPLE_BATCH_AXIS = {'x': 0, 'c': 0, 'positions': 0, 'loss_target': 0}
SHARED_INPUTS = []
_WEIGHT_DTYPES = {'w_ada': _jnp.float32, 'b_ada': _jnp.float32, 'g_pre_mix': _jnp.float32, 'g_post_mix': _jnp.float32, 'w_in': _jnp.float32, 'g_q': _jnp.float32, 'w_uq': _jnp.float32, 'g_kv': _jnp.float32, 'w_ukv': _jnp.float32, 'conv_w_mix': _jnp.float32, 'conv_b_mix': _jnp.float32, 'w_o': _jnp.float32, 'g_pre_ffn': _jnp.float32, 'g_post_ffn': _jnp.float32, 'w_up': _jnp.float32, 'conv_w_ffn': _jnp.float32, 'conv_b_ffn': _jnp.float32, 'w_down': _jnp.float32}
MOMENT_SCALE = {'w_ada': 2.438075e+00, 'b_ada': 5.332402e+00, 'g_pre_mix': 2.467974e-01, 'g_post_mix': 1.325378e+01, 'w_in': 2.692026e-01, 'g_q': 2.040401e-02, 'w_uq': 1.466523e-02, 'g_kv': 7.532436e-01, 'w_ukv': 3.280334e-01, 'conv_w_mix': 3.011947e-01, 'conv_b_mix': 5.649441e-01, 'w_o': 3.442749e-01, 'g_pre_ffn': 1.811925e-01, 'g_post_ffn': 1.307251e+01, 'w_up': 1.708803e-01, 'conv_w_ffn': 2.069817e-01, 'conv_b_ffn': 3.345371e-01, 'w_down': 3.620489e-01}


def _to_microbatches(a, axis):
    t = _jnp.moveaxis(a, axis, 0)
    t = t.reshape((N_MICROBATCH, t.shape[0] // N_MICROBATCH) + t.shape[1:])
    return _jnp.moveaxis(t, 1, axis + 1)


def setup_inputs(seed: int = 0) -> dict:
    inp = _fwd_setup_inputs(seed)
    key = _jax.random.fold_in(_jax.random.key(seed), 7919)
    shape, _ = _output_shape()
    out = dict(inp)
    out["loss_target"] = _jax.random.normal(_jax.random.fold_in(key, 0), shape, _jnp.float32)
    for i, name in enumerate(TWIN_WEIGHTS):
        w = inp[name].astype(_jnp.float32)
        if MOMENT_SCALE is None:
            s = _jnp.sqrt(_jnp.mean(_jnp.square(w)) + 1e-30)
        else:
            s = MOMENT_SCALE[name]
        km, kv = _jax.random.split(_jax.random.fold_in(key, i + 1))
        out[name] = w
        out["m_" + name] = s * _jax.random.normal(km, w.shape, _jnp.float32)
        out["v_" + name] = (s * s) * _jax.random.uniform(kv, w.shape, _jnp.float32, 0.5, 1.5)
    if N_MICROBATCH > 1:
        for name, axis in PER_EXAMPLE_BATCH_AXIS.items():
            out[name] = _to_microbatches(out[name], axis)
    return {'x': out['x'], 'c': out['c'], 'positions': out['positions'], 'w_ada': out['w_ada'], 'b_ada': out['b_ada'], 'g_pre_mix': out['g_pre_mix'], 'g_post_mix': out['g_post_mix'], 'w_in': out['w_in'], 'g_q': out['g_q'], 'w_uq': out['w_uq'], 'g_kv': out['g_kv'], 'w_ukv': out['w_ukv'], 'conv_w_mix': out['conv_w_mix'], 'conv_b_mix': out['conv_b_mix'], 'w_o': out['w_o'], 'g_pre_ffn': out['g_pre_ffn'], 'g_post_ffn': out['g_post_ffn'], 'w_up': out['w_up'], 'conv_w_ffn': out['conv_w_ffn'], 'conv_b_ffn': out['conv_b_ffn'], 'w_down': out['w_down'], 'loss_target': out['loss_target'], 'm_w_ada': out['m_w_ada'], 'm_b_ada': out['m_b_ada'], 'm_g_pre_mix': out['m_g_pre_mix'], 'm_g_post_mix': out['m_g_post_mix'], 'm_w_in': out['m_w_in'], 'm_g_q': out['m_g_q'], 'm_w_uq': out['m_w_uq'], 'm_g_kv': out['m_g_kv'], 'm_w_ukv': out['m_w_ukv'], 'm_conv_w_mix': out['m_conv_w_mix'], 'm_conv_b_mix': out['m_conv_b_mix'], 'm_w_o': out['m_w_o'], 'm_g_pre_ffn': out['m_g_pre_ffn'], 'm_g_post_ffn': out['m_g_post_ffn'], 'm_w_up': out['m_w_up'], 'm_conv_w_ffn': out['m_conv_w_ffn'], 'm_conv_b_ffn': out['m_conv_b_ffn'], 'm_w_down': out['m_w_down'], 'v_w_ada': out['v_w_ada'], 'v_b_ada': out['v_b_ada'], 'v_g_pre_mix': out['v_g_pre_mix'], 'v_g_post_mix': out['v_g_post_mix'], 'v_w_in': out['v_w_in'], 'v_g_q': out['v_g_q'], 'v_w_uq': out['v_w_uq'], 'v_g_kv': out['v_g_kv'], 'v_w_ukv': out['v_w_ukv'], 'v_conv_w_mix': out['v_conv_w_mix'], 'v_conv_b_mix': out['v_conv_b_mix'], 'v_w_o': out['v_w_o'], 'v_g_pre_ffn': out['v_g_pre_ffn'], 'v_g_post_ffn': out['v_g_post_ffn'], 'v_w_up': out['v_w_up'], 'v_conv_w_ffn': out['v_conv_w_ffn'], 'v_conv_b_ffn': out['v_conv_b_ffn'], 'v_w_down': out['v_w_down']}


def _loss(weights, diff, rest, loss_target):
    with _jax.named_scope("forward"):
        args = {**rest, TWIN_DIFF_INPUT: diff, **{k: w.astype(_WEIGHT_DTYPES[k]) for k, w in weights.items()}}
        y = _forward(args)
    with _jax.named_scope("loss_head"):
        err = _jnp.square(y.astype(_jnp.float32) - loss_target)
        return 0.5 * _jnp.sum(_jnp.mean(err, axis=-1)) if err.ndim else 0.5 * err


def _adamw(w, g, m, v):
    m = ADAM_B1 * m + (1.0 - ADAM_B1) * g
    v = ADAM_B2 * v + (1.0 - ADAM_B2) * _jnp.square(g)
    m_hat = m / (1.0 - ADAM_B1 ** ADAM_STEP)
    v_hat = v / (1.0 - ADAM_B2 ** ADAM_STEP)
    delta = -ADAM_LR * (m_hat / (_jnp.sqrt(v_hat) + ADAM_EPS) + ADAM_WD * w)
    return delta, m, v


def reference(x, c, positions, w_ada, b_ada, g_pre_mix, g_post_mix, w_in, g_q, w_uq, g_kv, w_ukv, conv_w_mix, conv_b_mix, w_o, g_pre_ffn, g_post_ffn, w_up, conv_w_ffn, conv_b_ffn, w_down, loss_target, m_w_ada, m_b_ada, m_g_pre_mix, m_g_post_mix, m_w_in, m_g_q, m_w_uq, m_g_kv, m_w_ukv, m_conv_w_mix, m_conv_b_mix, m_w_o, m_g_pre_ffn, m_g_post_ffn, m_w_up, m_conv_w_ffn, m_conv_b_ffn, m_w_down, v_w_ada, v_b_ada, v_g_pre_mix, v_g_post_mix, v_w_in, v_g_q, v_w_uq, v_g_kv, v_w_ukv, v_conv_w_mix, v_conv_b_mix, v_w_o, v_g_pre_ffn, v_g_post_ffn, v_w_up, v_conv_w_ffn, v_conv_b_ffn, v_w_down):
    given = dict(x=x, c=c, positions=positions, w_ada=w_ada, b_ada=b_ada, g_pre_mix=g_pre_mix, g_post_mix=g_post_mix, w_in=w_in, g_q=g_q, w_uq=w_uq, g_kv=g_kv, w_ukv=w_ukv, conv_w_mix=conv_w_mix, conv_b_mix=conv_b_mix, w_o=w_o, g_pre_ffn=g_pre_ffn, g_post_ffn=g_post_ffn, w_up=w_up, conv_w_ffn=conv_w_ffn, conv_b_ffn=conv_b_ffn, w_down=w_down, loss_target=loss_target, m_w_ada=m_w_ada, m_b_ada=m_b_ada, m_g_pre_mix=m_g_pre_mix, m_g_post_mix=m_g_post_mix, m_w_in=m_w_in, m_g_q=m_g_q, m_w_uq=m_w_uq, m_g_kv=m_g_kv, m_w_ukv=m_w_ukv, m_conv_w_mix=m_conv_w_mix, m_conv_b_mix=m_conv_b_mix, m_w_o=m_w_o, m_g_pre_ffn=m_g_pre_ffn, m_g_post_ffn=m_g_post_ffn, m_w_up=m_w_up, m_conv_w_ffn=m_conv_w_ffn, m_conv_b_ffn=m_conv_b_ffn, m_w_down=m_w_down, v_w_ada=v_w_ada, v_b_ada=v_b_ada, v_g_pre_mix=v_g_pre_mix, v_g_post_mix=v_g_post_mix, v_w_in=v_w_in, v_g_q=v_g_q, v_w_uq=v_w_uq, v_g_kv=v_g_kv, v_w_ukv=v_w_ukv, v_conv_w_mix=v_conv_w_mix, v_conv_b_mix=v_conv_b_mix, v_w_o=v_w_o, v_g_pre_ffn=v_g_pre_ffn, v_g_post_ffn=v_g_post_ffn, v_w_up=v_w_up, v_conv_w_ffn=v_conv_w_ffn, v_conv_b_ffn=v_conv_b_ffn, v_w_down=v_w_down)
    weights = {n: given[n] for n in TWIN_WEIGHTS}
    shared = {n: given[n] for n in SHARED_INPUTS}
    per_example = {n: given[n] for n in ['x', 'c', 'positions']}
    grad_fn = _jax.value_and_grad(_loss, argnums=(0, 1))

    def one_microbatch(ex, loss_target):
        ex = dict(ex)
        diff = ex.pop(TWIN_DIFF_INPUT)
        return grad_fn(weights, diff, {**shared, **ex}, loss_target)

    if N_MICROBATCH == 1:
        loss, (grad_w, grad_x) = one_microbatch(per_example, given["loss_target"])
    else:
        def body(carry, xs):
            loss_sum, grad_sum = carry
            l_k, (gw_k, gx_k) = one_microbatch(xs[0], xs[1])
            with _jax.named_scope("update"):
                return (loss_sum + l_k, _jax.tree.map(_jnp.add, grad_sum, gw_k)), gx_k

        init = (_jnp.zeros((), _jnp.float32), _jax.tree.map(_jnp.zeros_like, weights))
        (loss, grad_w), grad_x = _jax.lax.scan(body, init, (per_example, given["loss_target"]))
    with _jax.named_scope("update"):
        delta_w, new_m, new_v = {}, {}, {}
        for n in TWIN_WEIGHTS:
            delta_w[n], new_m[n], new_v[n] = _adamw(weights[n], grad_w[n], given["m_" + n], given["v_" + n])
    return (loss, grad_x, *[grad_w[n] for n in TWIN_WEIGHTS], *[delta_w[n] for n in TWIN_WEIGHTS],
            *[new_m[n] for n in TWIN_WEIGHTS], *[new_v[n] for n in TWIN_WEIGHTS])
```

```python
import math

import jax
import jax.numpy as jnp
from jax import lax
from jax.experimental import pallas as pl
from jax.experimental.pallas import tpu as pltpu

F32 = jnp.float32
BF16 = jnp.bfloat16
MESH = pl.DeviceIdType.MESH

N_DEV = 8
N_CHIP = 4
LANES = 128
SUBLANES = 8
VMEM_LIMIT = 56 * 2**20

NOPE = 128
ROPE = 64
VDIM = 128
HEAD_PAD = 128
ROPE_THETA = 10000.0
RMS_EPS = 1e-6
N_MOD = 6
CONV_K = 3
ATT_BLOCK = 512
NEG = -1e30

ADAM_LR = 0.001
ADAM_B1 = 0.9
ADAM_B2 = 0.999
ADAM_EPS = 1e-08
ADAM_WD = 0.01
ADAM_STEP = 10


def _tile(n, pref, align):
    if n <= pref:
        return n
    t = (pref // align) * align
    while t >= align:
        if n % t == 0:
            return t
        t -= align
    return n


def _cp(*sem):
    return pltpu.CompilerParams(dimension_semantics=sem, vmem_limit_bytes=VMEM_LIMIT)


def _rsq(x):
    return lax.rsqrt(jnp.mean(x * x, axis=-1, keepdims=True) + RMS_EPS)


def _norm_bwd(dn, n, r):
    return r * (dn - n * jnp.mean(dn * n, axis=-1, keepdims=True))


def _colsum(a):
    return jnp.sum(a, axis=0, keepdims=True)


def _matmul(a, b, *, ta=False, tb=False, out_dtype, tm, tn, tk, name):
    (k_a, m) = a.shape if ta else a.shape[::-1]
    (n, k_b) = b.shape if tb else b.shape[::-1]
    assert k_a == k_b, (a.shape, b.shape, ta, tb)
    tm, tn, tk = _tile(m, tm, LANES), _tile(n, tn, LANES), _tile(k_a, tk, LANES)
    nk = k_a // tk
    a_spec = pl.BlockSpec((tk, tm), lambda i, j, k: (k, i)) if ta else pl.BlockSpec((tm, tk), lambda i, j, k: (i, k))
    b_spec = pl.BlockSpec((tn, tk), lambda i, j, k: (j, k)) if tb else pl.BlockSpec((tk, tn), lambda i, j, k: (k, j))
    dims = (((0 if ta else 1,), (1 if tb else 0,)), ((), ()))

    def body(a_ref, b_ref, o_ref, *acc):
        p = lax.dot_general(a_ref[...].astype(BF16), b_ref[...].astype(BF16), dims, preferred_element_type=F32)
        if nk == 1:
            o_ref[...] = p.astype(o_ref.dtype)
        else:
            k = pl.program_id(2)

            @pl.when(k == 0)
            def _():
                acc[0][...] = p

            @pl.when(k > 0)
            def _():
                acc[0][...] += p

            @pl.when(k == nk - 1)
            def _():
                o_ref[...] = acc[0][...].astype(o_ref.dtype)

    return pl.pallas_call(
        body,
        name=name,
        out_shape=jax.ShapeDtypeStruct((m, n), out_dtype),
        grid=(m // tm, n // tn, nk),
        in_specs=[a_spec, b_spec],
        out_specs=pl.BlockSpec((tm, tn), lambda i, j, k: (i, j)),
        scratch_shapes=[] if nk == 1 else [pltpu.VMEM((tm, tn), F32)],
        compiler_params=_cp("parallel", "parallel", "arbitrary"),
    )(a, b)


def _rope_tables(pos_col, invf):
    s = pos_col.shape[0]
    ts = _tile(s, 1024, SUBLANES)
    half = ROPE // 2

    def body(p_ref, f_ref, c_ref, sa_ref, sb_ref):
        ang = p_ref[...] * f_ref[...]
        lane = lax.broadcasted_iota(jnp.int32, ang.shape, 1)
        cs, sn = jnp.cos(ang), jnp.sin(ang)
        c_ref[...] = jnp.where(lane < ROPE, cs, 0.0)
        sa_ref[...] = jnp.where((lane >= half) & (lane < ROPE), sn, 0.0)
        sb_ref[...] = jnp.where(lane < half, -sn, 0.0)

    tab = jax.ShapeDtypeStruct((s, LANES), F32)
    return pl.pallas_call(
        body,
        name="rope_tables",
        out_shape=(tab, tab, tab),
        grid=(s // ts,),
        in_specs=[pl.BlockSpec((ts, 1), lambda i: (i, 0)), pl.BlockSpec((1, LANES), lambda i: (0, 0))],
        out_specs=[pl.BlockSpec((ts, LANES), lambda i: (i, 0))] * 3,
        compiler_params=_cp("parallel"),
    )(pos_col, invf)


def _widen(t, w):
    return t if w == LANES else jnp.tile(t, (1, w // LANES))


def _rope(x, c, sa, sb):
    w = x.shape[1]
    c, sa, sb = _widen(c, w), _widen(sa, w), _widen(sb, w)
    return x * c + pltpu.roll(x, ROPE // 2, 1) * sa + pltpu.roll(x, w - ROPE // 2, 1) * sb


def _rope_t(d, c, sa, sb):
    w = d.shape[1]
    c, sa, sb = _widen(c, w), _widen(sa, w), _widen(sb, w)
    return d * c + pltpu.roll(d * sa, w - ROPE // 2, 1) + pltpu.roll(d * sb, ROPE // 2, 1)


def _ada_fwd(c_all, w, b):
    d, nc = w.shape
    tn = _tile(nc, 512, LANES)

    def body(c_ref, w_ref, b_ref, o_ref, ca_ref):
        cv = c_ref[...]
        ca = cv * jax.nn.sigmoid(cv)
        ca_ref[...] = ca
        o_ref[...] = jnp.dot(ca.astype(BF16), w_ref[...].astype(BF16), preferred_element_type=F32) + b_ref[...]

    return pl.pallas_call(
        body,
        name="ada_fwd",
        out_shape=(jax.ShapeDtypeStruct((N_DEV, nc), F32), jax.ShapeDtypeStruct((N_DEV, d), F32)),
        grid=(nc // tn,),
        in_specs=[
            pl.BlockSpec((N_DEV, d), lambda j: (0, 0)),
            pl.BlockSpec((d, tn), lambda j: (0, j)),
            pl.BlockSpec((1, tn), lambda j: (0, j)),
        ],
        out_specs=[pl.BlockSpec((N_DEV, tn), lambda j: (0, j)), pl.BlockSpec((N_DEV, d), lambda j: (0, 0))],
        compiler_params=_cp("arbitrary"),
    )(c_all, w, b)


def _rows(ts, d):
    return pl.BlockSpec((ts, d), lambda i: (i, 0))


def _vec(d):
    return pl.BlockSpec((1, d), lambda i: (0, 0))


def _sums(d):
    return pl.BlockSpec((SUBLANES, d), lambda i: (0, 0))


def _acc_rows(ref, i, rows):
    @pl.when(i == 0)
    def _():
        ref[...] = jnp.zeros(ref.shape, ref.dtype)

    for k, r in enumerate(rows):
        ref[k : k + 1, :] += r


def _pre_fwd(x, g, sc, sh):
    s, d = x.shape
    ts = _tile(s, 512, SUBLANES)

    def body(x_ref, g_ref, sc_ref, sh_ref, h_ref):
        xv = x_ref[...]
        h_ref[...] = (((xv * _rsq(xv)) * g_ref[...]) * (1.0 + sc_ref[...]) + sh_ref[...]).astype(BF16)

    return pl.pallas_call(
        body,
        name="pre_mix_fwd",
        out_shape=jax.ShapeDtypeStruct((s, d), BF16),
        grid=(s // ts,),
        in_specs=[_rows(ts, d), _vec(d), _vec(d), _vec(d)],
        out_specs=_rows(ts, d),
        compiler_params=_cp("parallel"),
    )(x, g, sc, sh)


def _mid_fwd(x0, mix, g_post, gt, g_pre, sc, sh):
    s, d = x0.shape
    ts = _tile(s, 256, SUBLANES)

    def body(x_ref, m_ref, gp_ref, gt_ref, g_ref, sc_ref, sh_ref, x1_ref, h_ref):
        mv = m_ref[...]
        x1 = x_ref[...] + gt_ref[...] * ((mv * _rsq(mv)) * gp_ref[...])
        x1_ref[...] = x1
        h_ref[...] = (((x1 * _rsq(x1)) * g_ref[...]) * (1.0 + sc_ref[...]) + sh_ref[...]).astype(BF16)

    return pl.pallas_call(
        body,
        name="mid_fwd",
        out_shape=(jax.ShapeDtypeStruct((s, d), F32), jax.ShapeDtypeStruct((s, d), BF16)),
        grid=(s // ts,),
        in_specs=[_rows(ts, d), _rows(ts, d)] + [_vec(d)] * 5,
        out_specs=[_rows(ts, d), _rows(ts, d)],
        compiler_params=_cp("parallel"),
    )(x0, mix, g_post, gt, g_pre, sc, sh)


def _final(x1, y, tgt, g_post, gt):
    s, d = x1.shape
    ts = _tile(s, 256, SUBLANES)
    ni = s // ts

    def body(x_ref, y_ref, t_ref, gp_ref, gt_ref, dx_ref, dy_ref, s_ref):
        i = pl.program_id(0)
        yv, gp, gt_v = y_ref[...], gp_ref[...], gt_ref[...]
        r = _rsq(yv)
        n = yv * r
        err = (x_ref[...] + gt_v * (n * gp)) - t_ref[...]
        dx = err * (1.0 / d)
        dx_ref[...] = dx
        dy_ref[...] = _norm_bwd(dx * (gt_v * gp), n, r).astype(BF16)
        _acc_rows(s_ref, i, [_colsum(dx * (n * gp)), _colsum(dx * gt_v * n), _colsum(err * err)])

        @pl.when(i == ni - 1)
        def _():
            tot = jnp.sum(s_ref[2:3, :], axis=1, keepdims=True) * (0.5 / d)
            s_ref[3:4, :] = jnp.broadcast_to(tot, (1, d))

    return pl.pallas_call(
        body,
        name="final_fwd_bwd",
        out_shape=(
            jax.ShapeDtypeStruct((s, d), F32),
            jax.ShapeDtypeStruct((s, d), BF16),
            jax.ShapeDtypeStruct((SUBLANES, d), F32),
        ),
        grid=(ni,),
        in_specs=[_rows(ts, d)] * 3 + [_vec(d)] * 2,
        out_specs=[_rows(ts, d), _rows(ts, d), _sums(d)],
        compiler_params=_cp("arbitrary"),
    )(x1, y, tgt, g_post, gt)


def _mid_bwd(dh2, dx2, x1, mix, g_pre, sc, g_post, gt):
    s, d = x1.shape
    ts = _tile(s, 256, SUBLANES)

    def body(dh_ref, dx2_ref, x_ref, m_ref, g_ref, sc_ref, gp_ref, gt_ref, dx1_ref, dm_ref, s_ref):
        i = pl.program_id(0)
        dh, xv, mv = dh_ref[...], x_ref[...], m_ref[...]
        g, sc_v, gp, gt_v = g_ref[...], sc_ref[...], gp_ref[...], gt_ref[...]
        r1 = _rsq(xv)
        n1 = xv * r1
        dx1 = dx2_ref[...] + _norm_bwd(dh * (g * (1.0 + sc_v)), n1, r1)
        dx1_ref[...] = dx1
        rm = _rsq(mv)
        nm = mv * rm
        dm_ref[...] = _norm_bwd(dx1 * (gt_v * gp), nm, rm).astype(BF16)
        _acc_rows(
            s_ref,
            i,
            [
                _colsum(dh),
                _colsum(dh * (n1 * g)),
                _colsum(dh * (1.0 + sc_v) * n1),
                _colsum(dx1 * (nm * gp)),
                _colsum(dx1 * gt_v * nm),
            ],
        )

    return pl.pallas_call(
        body,
        name="mid_bwd",
        out_shape=(
            jax.ShapeDtypeStruct((s, d), F32),
            jax.ShapeDtypeStruct((s, d), BF16),
            jax.ShapeDtypeStruct((SUBLANES, d), F32),
        ),
        grid=(s // ts,),
        in_specs=[_rows(ts, d)] * 4 + [_vec(d)] * 4,
        out_specs=[_rows(ts, d), _rows(ts, d), _sums(d)],
        compiler_params=_cp("arbitrary"),
    )(dh2, dx2, x1, mix, g_pre, sc, g_post, gt)


def _first_bwd(dh1, dx1, x0, g, sc):
    s, d = x0.shape
    ts = _tile(s, 256, SUBLANES)

    def body(dh_ref, dx1_ref, x_ref, g_ref, sc_ref, dx_ref, s_ref):
        i = pl.program_id(0)
        dh, xv, gv, sc_v = dh_ref[...], x_ref[...], g_ref[...], sc_ref[...]
        r = _rsq(xv)
        n = xv * r
        dx_ref[...] = dx1_ref[...] + _norm_bwd(dh * (gv * (1.0 + sc_v)), n, r)
        _acc_rows(s_ref, i, [_colsum(dh), _colsum(dh * (n * gv)), _colsum(dh * (1.0 + sc_v) * n)])

    return pl.pallas_call(
        body,
        name="first_bwd",
        out_shape=(jax.ShapeDtypeStruct((s, d), F32), jax.ShapeDtypeStruct((SUBLANES, d), F32)),
        grid=(s // ts,),
        in_specs=[_rows(ts, d)] * 3 + [_vec(d)] * 2,
        out_specs=[_rows(ts, d), _sums(d)],
        compiler_params=_cp("arbitrary"),
    )(dh1, dx1, x0, g, sc)


def _latent_fwd(proj, g_q, g_kv, tabs, lb):
    s = proj.shape[0]
    ql, kl = g_q.shape[1], g_kv.shape[1]
    ts = _tile(s, 512, SUBLANES)

    def body(p_ref, gq_ref, gk_ref, c_ref, sa_ref, sb_ref, q_ref, kv_ref, kr_ref):
        pv = p_ref[...]
        q, kv, kr = pv[:, :ql], pv[:, ql : ql + kl], pv[:, ql + kl : ql + kl + HEAD_PAD]
        q_ref[...] = ((q * _rsq(q)) * gq_ref[...]).astype(BF16)
        kv_ref[...] = ((kv * _rsq(kv)) * gk_ref[...]).astype(BF16)
        kr_ref[...] = _rope(kr, c_ref[...], sa_ref[...], sb_ref[...]).astype(BF16)

    return pl.pallas_call(
        body,
        name="latent_fwd",
        out_shape=(
            jax.ShapeDtypeStruct((s, ql), BF16),
            jax.ShapeDtypeStruct((s, kl), BF16),
            jax.ShapeDtypeStruct((s, HEAD_PAD), BF16),
        ),
        grid=(s // ts,),
        in_specs=[_rows(ts, lb), _vec(ql), _vec(kl)] + [_rows(ts, LANES)] * 3,
        out_specs=[_rows(ts, ql), _rows(ts, kl), _rows(ts, HEAD_PAD)],
        compiler_params=_cp("parallel"),
    )(proj, g_q, g_kv, *tabs)


def _q_rope(q, tabs):
    s, w2 = q.shape
    w = w2 // 2
    ts = _tile(s, 512, SUBLANES)

    def body(q_ref, c_ref, sa_ref, sb_ref, o_ref):
        qv = q_ref[...]
        o_ref[:, :w] = qv[:, :w].astype(BF16)
        o_ref[:, w:] = _rope(qv[:, w:], c_ref[...], sa_ref[...], sb_ref[...]).astype(BF16)

    return pl.pallas_call(
        body,
        name="q_rope",
        out_shape=jax.ShapeDtypeStruct((s, w2), BF16),
        grid=(s // ts,),
        in_specs=[_rows(ts, w2)] + [_rows(ts, LANES)] * 3,
        out_specs=_rows(ts, w2),
        compiler_params=_cp("parallel"),
    )(q, *tabs)


def _latent_bwd(proj, dqn, dkvn, dkr_h, g_q, g_kv, tabs, lb):
    s = proj.shape[0]
    ql, kl = g_q.shape[1], g_kv.shape[1]
    hw = dkr_h.shape[1]
    ts = _tile(s, 256, SUBLANES)
    pad = lb - ql - kl - HEAD_PAD

    def body(p_ref, dq_ref, dkv_ref, dkr_ref, gq_ref, gk_ref, c_ref, sa_ref, sb_ref, o_ref, s_ref):
        i = pl.program_id(0)
        pv = p_ref[...]
        q, kv = pv[:, :ql], pv[:, ql : ql + kl]
        dqn_v, dkvn_v = dq_ref[...], dkv_ref[...]
        rq = _rsq(q)
        nq = q * rq
        rk = _rsq(kv)
        nk = kv * rk
        dkr = dkr_ref[:, :HEAD_PAD]
        for h in range(1, hw // HEAD_PAD):
            dkr = dkr + dkr_ref[:, h * HEAD_PAD : (h + 1) * HEAD_PAD]
        parts = [
            _norm_bwd(dqn_v * gq_ref[...], nq, rq).astype(BF16),
            _norm_bwd(dkvn_v * gk_ref[...], nk, rk).astype(BF16),
            _rope_t(dkr, c_ref[...], sa_ref[...], sb_ref[...]).astype(BF16),
        ]
        if pad:
            parts.append(jnp.zeros((ts, pad), BF16))
        o_ref[...] = jnp.concatenate(parts, axis=1)
        row = [_colsum(dqn_v * nq), _colsum(dkvn_v * nk), jnp.zeros((1, lb - ql - kl), F32)]
        _acc_rows(s_ref, i, [jnp.concatenate(row, axis=1)])

    return pl.pallas_call(
        body,
        name="latent_bwd",
        out_shape=(jax.ShapeDtypeStruct((s, lb), BF16), jax.ShapeDtypeStruct((SUBLANES, lb), F32)),
        grid=(s // ts,),
        in_specs=[_rows(ts, lb), _rows(ts, ql), _rows(ts, kl), _rows(ts, hw)]
        + [_vec(ql), _vec(kl)]
        + [_rows(ts, LANES)] * 3,
        out_specs=[_rows(ts, lb), _sums(lb)],
        compiler_params=_cp("arbitrary"),
    )(proj, dqn, dkvn, dkr_h, g_q, g_kv, *tabs)


def _conv3(ext, w, b):
    return (pltpu.roll(ext, 2, 0) * w[0:1] + pltpu.roll(ext, 1, 0) * w[1:2]) + ext * w[2:3] + b


def _conv3_t(du, w):
    n = du.shape[0]
    return du * w[2:3] + pltpu.roll(du, n - 1, 0) * w[1:2] + pltpu.roll(du, n - 2, 0) * w[0:1]


def _halo_maps(ts, s):
    r8, last = ts // SUBLANES, s // SUBLANES - 1
    prev = lambda i: jnp.maximum(i * r8 - 1, 0)
    nxt = lambda i: jnp.minimum((i + 1) * r8, last)
    return prev, nxt


def _mixer_fwd(cat, proj, cw, cb, lb, col0):
    s = proj.shape[0]
    cwid = cw.shape[1]
    ts = _tile(s, 512, SUBLANES)
    tc = _tile(cwid, 512, LANES)
    assert lb % tc == 0 and col0 % tc == 0
    nj, ob, oc = cwid // tc, lb // tc, col0 // tc
    prev, _ = _halo_maps(ts, s)

    def body(_, gb_ref, gc_ref, ci_ref, pgc_ref, pci_ref, w_ref, b_ref, o_ref):
        keep = jnp.where(pl.program_id(1) > 0, 1.0, 0.0)
        ext = jnp.concatenate([pgc_ref[...] * pci_ref[...] * keep, gc_ref[...] * ci_ref[...]], axis=0)
        o_ref[...] = (gb_ref[...] * _conv3(ext, w_ref[...], b_ref[...])[SUBLANES:]).astype(BF16)

    def col(k):
        return pl.BlockSpec((ts, tc), lambda j, i: (i, ob + k * nj + j))

    def halo(k):
        return pl.BlockSpec((SUBLANES, tc), lambda j, i: (prev(i), ob + k * nj + j))

    return pl.pallas_call(
        body,
        name="mixer_fwd",
        out_shape=jax.ShapeDtypeStruct(cat.shape, BF16),
        grid=(nj, s // ts),
        in_specs=[pl.BlockSpec(memory_space=pl.ANY), col(0), col(1), col(2), halo(1), halo(2)]
        + [pl.BlockSpec((CONV_K, tc), lambda j, i: (0, j)), pl.BlockSpec((1, tc), lambda j, i: (0, j))],
        out_specs=pl.BlockSpec((ts, tc), lambda j, i: (i, oc + j)),
        input_output_aliases={0: 0},
        compiler_params=_cp("parallel", "arbitrary"),
    )(cat, proj, proj, proj, proj, proj, cw, cb)


def _mixer_bwd(dcat, proj, cw, cb, lb, col0):
    s = proj.shape[0]
    cwid = cw.shape[1]
    ts = _tile(s, 256, SUBLANES)
    tc = _tile(cwid, 512, LANES)
    nj, ob, oc = cwid // tc, lb // tc, col0 // tc
    ni = s // ts
    prev, nxt = _halo_maps(ts, s)

    def body(d_ref, dn_ref, gb_ref, gbn_ref, gc_ref, gcp_ref, gcn_ref, ci_ref, cip_ref, cin_ref, w_ref, b_ref,
             dgb_ref, dgc_ref, dci_ref, s_ref):
        i = pl.program_id(1)
        keep_p = jnp.where(i > 0, 1.0, 0.0)
        keep_n = jnp.where(i < ni - 1, 1.0, 0.0)
        w = w_ref[...]
        gc = jnp.concatenate([gcp_ref[...], gc_ref[...], gcn_ref[...]], axis=0)
        ci = jnp.concatenate([cip_ref[...] * keep_p, ci_ref[...], cin_ref[...]], axis=0)
        u = gc * ci
        cv = _conv3(u, w, b_ref[...])[SUBLANES:]
        dco = jnp.concatenate([d_ref[...], dn_ref[...] * keep_n], axis=0)
        gb = jnp.concatenate([gb_ref[...], gbn_ref[...]], axis=0)
        dgb_ref[...] = (dco * cv)[:ts].astype(BF16)
        dcv = dco * gb
        du = _conv3_t(dcv, w)[:ts]
        dgc_ref[...] = (du * ci_ref[...]).astype(BF16)
        dci_ref[...] = (du * gc_ref[...]).astype(BF16)
        dt = dcv[:ts]
        u1, u2 = pltpu.roll(u, 1, 0), pltpu.roll(u, 2, 0)
        lo, hi = SUBLANES, SUBLANES + ts
        _acc_rows(s_ref, i, [_colsum(dt * u2[lo:hi]), _colsum(dt * u1[lo:hi]), _colsum(dt * u[lo:hi]), _colsum(dt)])

    def col(k):
        return pl.BlockSpec((ts, tc), lambda j, i: (i, ob + k * nj + j))

    def halo(k, which):
        return pl.BlockSpec((SUBLANES, tc), lambda j, i: (which(i), ob + k * nj + j))

    out_col = [pl.BlockSpec((ts, tc), lambda j, i: (i, j))] * 3
    grad = jax.ShapeDtypeStruct((s, cwid), BF16)
    return pl.pallas_call(
        body,
        name="mixer_bwd",
        out_shape=(grad, grad, grad, jax.ShapeDtypeStruct((SUBLANES, cwid), F32)),
        grid=(nj, ni),
        in_specs=[
            pl.BlockSpec((ts, tc), lambda j, i: (i, oc + j)),
            pl.BlockSpec((SUBLANES, tc), lambda j, i: (nxt(i), oc + j)),
            col(0), halo(0, nxt),
            col(1), halo(1, prev), halo(1, nxt),
            col(2), halo(2, prev), halo(2, nxt),
            pl.BlockSpec((CONV_K, tc), lambda j, i: (0, j)),
            pl.BlockSpec((1, tc), lambda j, i: (0, j)),
        ],
        out_specs=out_col + [pl.BlockSpec((SUBLANES, tc), lambda j, i: (0, j))],
        compiler_params=_cp("parallel", "arbitrary"),
    )(dcat, dcat, proj, proj, proj, proj, proj, proj, proj, proj, cw, cb)


def _ffn_act_fwd(up, cw, cb):
    s, f2 = up.shape
    f = f2 // 2
    ts = _tile(s, 512, SUBLANES)
    tc = _tile(f, 512, LANES)
    nj = f // tc
    prev, _ = _halo_maps(ts, s)

    def body(ua_ref, ug_ref, pa_ref, pg_ref, wa_ref, wg_ref, ba_ref, bg_ref, o_ref):
        keep = jnp.where(pl.program_id(1) > 0, 1.0, 0.0)

        def conv(u_ref, p_ref, w_ref, b_ref):
            ext = jnp.concatenate([p_ref[...] * keep, u_ref[...]], axis=0)
            return _conv3(ext, w_ref[...], b_ref[...])[SUBLANES:]

        a = conv(ua_ref, pa_ref, wa_ref, ba_ref)
        g = conv(ug_ref, pg_ref, wg_ref, bg_ref)
        o_ref[...] = ((g * jax.nn.sigmoid(g)) * a).astype(BF16)

    def col(k):
        return pl.BlockSpec((ts, tc), lambda j, i: (i, k * nj + j))

    def halo(k):
        return pl.BlockSpec((SUBLANES, tc), lambda j, i: (prev(i), k * nj + j))

    def wspec(rows, k):
        return pl.BlockSpec((rows, tc), lambda j, i: (0, k * nj + j))

    return pl.pallas_call(
        body,
        name="ffn_act_fwd",
        out_shape=jax.ShapeDtypeStruct((s, f), BF16),
        grid=(nj, s // ts),
        in_specs=[col(0), col(1), halo(0), halo(1), wspec(CONV_K, 0), wspec(CONV_K, 1), wspec(1, 0), wspec(1, 1)],
        out_specs=pl.BlockSpec((ts, tc), lambda j, i: (i, j)),
        compiler_params=_cp("parallel", "arbitrary"),
    )(up, up, up, up, cw, cw, cb, cb)


def _ffn_act_bwd(dact, up, cw, cb):
    s, f2 = up.shape
    f = f2 // 2
    ts = _tile(s, 256, SUBLANES)
    tc = _tile(f, 512, LANES)
    nj, ni = f // tc, s // ts
    prev, nxt = _halo_maps(ts, s)

    def body(d_ref, dn_ref, ua_ref, uap_ref, uan_ref, ug_ref, ugp_ref, ugn_ref, wa_ref, wg_ref, ba_ref, bg_ref,
             dua_ref, dug_ref, sa_ref, sg_ref):
        i = pl.program_id(1)
        keep_p = jnp.where(i > 0, 1.0, 0.0)
        keep_n = jnp.where(i < ni - 1, 1.0, 0.0)
        wa, wg = wa_ref[...], wg_ref[...]
        exta = jnp.concatenate([uap_ref[...] * keep_p, ua_ref[...], uan_ref[...]], axis=0)
        extg = jnp.concatenate([ugp_ref[...] * keep_p, ug_ref[...], ugn_ref[...]], axis=0)
        a = _conv3(exta, wa, ba_ref[...])[SUBLANES:]
        g = _conv3(extg, wg, bg_ref[...])[SUBLANES:]
        dact_v = jnp.concatenate([d_ref[...], dn_ref[...] * keep_n], axis=0)
        sg = jax.nn.sigmoid(g)
        da = dact_v * (g * sg)
        dg = dact_v * a * (sg * (1.0 + g * (1.0 - sg)))
        lo, hi = SUBLANES, SUBLANES + ts

        def back(du, ext, w, dup_ref, s_ref):
            dup_ref[...] = _conv3_t(du, w)[:ts].astype(BF16)
            dt = du[:ts]
            e1, e2 = pltpu.roll(ext, 1, 0), pltpu.roll(ext, 2, 0)
            _acc_rows(s_ref, i, [_colsum(dt * e2[lo:hi]), _colsum(dt * e1[lo:hi]), _colsum(dt * ext[lo:hi]), _colsum(dt)])

        back(da, exta, wa, dua_ref, sa_ref)
        back(dg, extg, wg, dug_ref, sg_ref)

    def col(k):
        return pl.BlockSpec((ts, tc), lambda j, i: (i, k * nj + j))

    def halo(k, which):
        return pl.BlockSpec((SUBLANES, tc), lambda j, i: (which(i), k * nj + j))

    def wspec(rows, k):
        return pl.BlockSpec((rows, tc), lambda j, i: (0, k * nj + j))

    half = pl.BlockSpec((ts, tc), lambda j, i: (i, j))
    half_sums = pl.BlockSpec((SUBLANES, tc), lambda j, i: (0, j))
    return pl.pallas_call(
        body,
        name="ffn_act_bwd",
        out_shape=(
            jax.ShapeDtypeStruct((s, f), BF16),
            jax.ShapeDtypeStruct((s, f), BF16),
            jax.ShapeDtypeStruct((SUBLANES, f), F32),
            jax.ShapeDtypeStruct((SUBLANES, f), F32),
        ),
        grid=(nj, ni),
        in_specs=[
            pl.BlockSpec((ts, tc), lambda j, i: (i, j)),
            pl.BlockSpec((SUBLANES, tc), lambda j, i: (nxt(i), j)),
            col(0), halo(0, prev), halo(0, nxt),
            col(1), halo(1, prev), halo(1, nxt),
            wspec(CONV_K, 0), wspec(CONV_K, 1), wspec(1, 0), wspec(1, 1),
        ],
        out_specs=[half, half, half_sums, half_sums],
        compiler_params=_cp("parallel", "arbitrary"),
    )(dact, dact, up, up, up, up, up, up, cw, cw, cb, cb)


ATT_SCALE = 1.0 / math.sqrt(NOPE + ROPE)
NT = (((1,), (1,)), ((), ()))
TN = (((0,), (0,)), ((), ()))


def _causal(sc):
    row = lax.broadcasted_iota(jnp.int32, sc.shape, 0)
    col = lax.broadcasted_iota(jnp.int32, sc.shape, 1)
    return jnp.where(col <= row, sc, NEG)


def _attn_fwd(q, kv, kr, n_heads, cat_cols):
    s = q.shape[0]
    t = _tile(s, ATT_BLOCK, LANES)
    nb = s // t
    hh = n_heads

    def body(qn_ref, qr_ref, kn_ref, kr_ref, v_ref, o_ref, lse_ref, q_s, m_s, l_s, acc_s):
        i, k = pl.program_id(1), pl.program_id(2)

        @pl.when(k == 0)
        def _():
            q_s[:, :NOPE] = qn_ref[...]
            q_s[:, NOPE:] = qr_ref[...]
            m_s[...] = jnp.full(m_s.shape, NEG, F32)
            l_s[...] = jnp.zeros(l_s.shape, F32)
            acc_s[...] = jnp.zeros(acc_s.shape, F32)

        def step(masked):
            kc = jnp.concatenate([kn_ref[...], kr_ref[...]], axis=1)
            sc = lax.dot_general(q_s[...], kc, NT, preferred_element_type=F32) * ATT_SCALE
            if masked:
                sc = _causal(sc)
            m_prev = m_s[...]
            m_new = jnp.maximum(m_prev, jnp.max(sc, axis=1, keepdims=True))
            alpha = jnp.exp(m_prev - m_new)
            p = jnp.exp(sc - m_new)
            l_s[...] = alpha * l_s[...] + jnp.sum(p, axis=1, keepdims=True)
            acc_s[...] = alpha * acc_s[...] + jnp.dot(p.astype(BF16), v_ref[...], preferred_element_type=F32)
            m_s[...] = m_new

        @pl.when(k < i)
        def _():
            step(False)

        @pl.when(k == i)
        def _():
            step(True)
            l = l_s[...]
            o_ref[...] = (acc_s[...] / l).astype(BF16)
            lse_ref[...] = jnp.broadcast_to(m_s[...] + jnp.log(l), lse_ref.shape)

    qmap = lambda off: (lambda h, i, k: (i, off + h))
    kmap = lambda off: (lambda h, i, k: (jnp.minimum(k, i), off + h))
    return pl.pallas_call(
        body,
        name="attn_fwd",
        out_shape=(jax.ShapeDtypeStruct((s, cat_cols), BF16), jax.ShapeDtypeStruct((s, hh * LANES), F32)),
        grid=(hh, nb, nb),
        in_specs=[
            pl.BlockSpec((t, NOPE), qmap(0)),
            pl.BlockSpec((t, HEAD_PAD), qmap(hh)),
            pl.BlockSpec((t, NOPE), kmap(0)),
            pl.BlockSpec((t, HEAD_PAD), lambda h, i, k: (jnp.minimum(k, i), 0)),
            pl.BlockSpec((t, VDIM), kmap(hh)),
        ],
        out_specs=[pl.BlockSpec((t, VDIM), qmap(0)), pl.BlockSpec((t, LANES), qmap(0))],
        scratch_shapes=[
            pltpu.VMEM((t, NOPE + HEAD_PAD), BF16),
            pltpu.VMEM((t, 1), F32),
            pltpu.VMEM((t, 1), F32),
            pltpu.VMEM((t, VDIM), F32),
        ],
        compiler_params=_cp("parallel", "parallel", "arbitrary"),
    )(q, q, kv, kr, kv)


def _attn_p_ds(qc, kc, v, do, dsum, lse, masked):
    sc = lax.dot_general(qc, kc, NT, preferred_element_type=F32) * ATT_SCALE
    if masked:
        sc = _causal(sc)
    p = jnp.exp(sc - lse)
    dp = lax.dot_general(do, v, NT, preferred_element_type=F32)
    return p, (p * (dp - dsum) * ATT_SCALE).astype(BF16)


def _attn_bwd_dkv(q, kv, kr, cat, dcat, lse, n_heads):
    s = q.shape[0]
    t = _tile(s, ATT_BLOCK, LANES)
    nb = s // t
    hh = n_heads

    def body(qn_ref, qr_ref, kn_ref, kr_ref, v_ref, o_ref, do_ref, lse_ref, dkv_k_ref, dkv_v_ref, dkr_ref, kc_s, dk_s, dv_s):
        j, i = pl.program_id(1), pl.program_id(2)

        @pl.when(i == 0)
        def _():
            kc_s[:, :NOPE] = kn_ref[...]
            kc_s[:, NOPE:] = kr_ref[...]
            dk_s[...] = jnp.zeros(dk_s.shape, F32)
            dv_s[...] = jnp.zeros(dv_s.shape, F32)

        def step(masked):
            qc = jnp.concatenate([qn_ref[...], qr_ref[...]], axis=1)
            do = do_ref[...]
            dsum = jnp.sum(do * o_ref[...].astype(F32), axis=1, keepdims=True)
            dob = do.astype(BF16)
            p, ds = _attn_p_ds(qc, kc_s[...], v_ref[...], dob, dsum, lse_ref[:, :1], masked)
            dv_s[...] += lax.dot_general(p.astype(BF16), dob, TN, preferred_element_type=F32)
            dk_s[...] += lax.dot_general(ds, qc, TN, preferred_element_type=F32)

        @pl.when(i > j)
        def _():
            step(False)

        @pl.when(i == j)
        def _():
            step(True)

        @pl.when(i == nb - 1)
        def _():
            dkv_k_ref[...] = dk_s[:, :NOPE].astype(BF16)
            dkv_v_ref[...] = dv_s[...].astype(BF16)
            dkr_ref[...] = dk_s[:, NOPE:]

    qmap = lambda off: (lambda h, j, i: (jnp.maximum(i, j), off + h))
    kmap = lambda off: (lambda h, j, i: (j, off + h))
    w = hh * LANES
    return pl.pallas_call(
        body,
        name="attn_bwd_dkv",
        out_shape=(
            jax.ShapeDtypeStruct((s, w), BF16),
            jax.ShapeDtypeStruct((s, w), BF16),
            jax.ShapeDtypeStruct((s, w), F32),
        ),
        grid=(hh, nb, nb),
        in_specs=[
            pl.BlockSpec((t, NOPE), qmap(0)),
            pl.BlockSpec((t, HEAD_PAD), qmap(hh)),
            pl.BlockSpec((t, NOPE), kmap(0)),
            pl.BlockSpec((t, HEAD_PAD), lambda h, j, i: (j, 0)),
            pl.BlockSpec((t, VDIM), kmap(hh)),
            pl.BlockSpec((t, VDIM), qmap(0)),
            pl.BlockSpec((t, VDIM), qmap(0)),
            pl.BlockSpec((t, LANES), qmap(0)),
        ],
        out_specs=[pl.BlockSpec((t, NOPE), kmap(0)), pl.BlockSpec((t, VDIM), kmap(0)), pl.BlockSpec((t, HEAD_PAD), kmap(0))],
        scratch_shapes=[
            pltpu.VMEM((t, NOPE + HEAD_PAD), BF16),
            pltpu.VMEM((t, NOPE + HEAD_PAD), F32),
            pltpu.VMEM((t, VDIM), F32),
        ],
        compiler_params=_cp("parallel", "parallel", "arbitrary"),
    )(q, q, kv, kr, kv, cat, dcat, lse)


def _attn_bwd_dq(q, kv, kr, cat, dcat, lse, tabs, n_heads):
    s = q.shape[0]
    t = _tile(s, ATT_BLOCK, LANES)
    nb = s // t
    hh = n_heads

    def body(qn_ref, qr_ref, kn_ref, kr_ref, v_ref, o_ref, do_ref, lse_ref, c_ref, sa_ref, sb_ref, dqn_ref, dqr_ref,
             q_s, dsum_s, dq_s):
        i, k = pl.program_id(1), pl.program_id(2)

        @pl.when(k == 0)
        def _():
            q_s[:, :NOPE] = qn_ref[...]
            q_s[:, NOPE:] = qr_ref[...]
            dsum_s[...] = jnp.sum(do_ref[...] * o_ref[...].astype(F32), axis=1, keepdims=True)
            dq_s[...] = jnp.zeros(dq_s.shape, F32)

        def step(masked):
            kc = jnp.concatenate([kn_ref[...], kr_ref[...]], axis=1)
            _, ds = _attn_p_ds(q_s[...], kc, v_ref[...], do_ref[...].astype(BF16), dsum_s[...], lse_ref[:, :1], masked)
            dq_s[...] += jnp.dot(ds, kc, preferred_element_type=F32)

        @pl.when(k < i)
        def _():
            step(False)

        @pl.when(k == i)
        def _():
            step(True)
            dqn_ref[...] = dq_s[:, :NOPE].astype(BF16)
            dqr_ref[...] = _rope_t(dq_s[:, NOPE:], c_ref[...], sa_ref[...], sb_ref[...]).astype(BF16)

    qmap = lambda off: (lambda h, i, k: (i, off + h))
    kmap = lambda off: (lambda h, i, k: (jnp.minimum(k, i), off + h))
    w = hh * LANES
    tab = pl.BlockSpec((t, LANES), lambda h, i, k: (i, 0))
    return pl.pallas_call(
        body,
        name="attn_bwd_dq",
        out_shape=(jax.ShapeDtypeStruct((s, w), BF16), jax.ShapeDtypeStruct((s, w), BF16)),
        grid=(hh, nb, nb),
        in_specs=[
            pl.BlockSpec((t, NOPE), qmap(0)),
            pl.BlockSpec((t, HEAD_PAD), qmap(hh)),
            pl.BlockSpec((t, NOPE), kmap(0)),
            pl.BlockSpec((t, HEAD_PAD), lambda h, i, k: (jnp.minimum(k, i), 0)),
            pl.BlockSpec((t, VDIM), kmap(hh)),
            pl.BlockSpec((t, VDIM), qmap(0)),
            pl.BlockSpec((t, VDIM), qmap(0)),
            pl.BlockSpec((t, LANES), qmap(0)),
            tab, tab, tab,
        ],
        out_specs=[pl.BlockSpec((t, NOPE), qmap(0)), pl.BlockSpec((t, HEAD_PAD), qmap(0))],
        scratch_shapes=[
            pltpu.VMEM((t, NOPE + HEAD_PAD), BF16),
            pltpu.VMEM((t, 1), F32),
            pltpu.VMEM((t, NOPE + HEAD_PAD), F32),
        ],
        compiler_params=_cp("parallel", "parallel", "arbitrary"),
    )(q, q, kv, kr, kv, cat, dcat, lse, *tabs)


def _adamw(w, m, v, grads, name):
    r, c = w.shape
    budget_rows = max(SUBLANES, (VMEM_LIMIT // 3) // (4 * c * 2 * (7 + len(grads))))
    tr = _tile(r, budget_rows, SUBLANES)
    ng = len(grads)
    c1 = 1.0 - ADAM_B1**ADAM_STEP
    c2 = 1.0 - ADAM_B2**ADAM_STEP

    def body(*refs):
        w_ref, m_ref, v_ref = refs[:3]
        g_ref, d_ref, nm_ref, nv_ref = refs[3 + ng :]
        g = refs[3][...]
        for extra in refs[4 : 3 + ng]:
            g = g + extra[...]
        mn = ADAM_B1 * m_ref[...] + (1.0 - ADAM_B1) * g
        vn = ADAM_B2 * v_ref[...] + (1.0 - ADAM_B2) * (g * g)
        g_ref[...] = g
        nm_ref[...] = mn
        nv_ref[...] = vn
        d_ref[...] = -ADAM_LR * ((mn / c1) / (jnp.sqrt(vn / c2) + ADAM_EPS) + ADAM_WD * w_ref[...])

    blk = pl.BlockSpec((tr, c), lambda i: (i, 0))
    out = jax.ShapeDtypeStruct((r, c), F32)
    return pl.pallas_call(
        body,
        name=name,
        out_shape=(out, out, out, out),
        grid=(r // tr,),
        in_specs=[blk] * (3 + ng),
        out_specs=[blk] * 4,
        compiler_params=_cp("parallel"),
    )(w, m, v, *grads)


def _ada_grad(ca_t, dm):
    d = ca_t.shape[0]
    nc = dm.shape[1]
    tn = _tile(nc, 512, LANES)

    def body(a_ref, b_ref, o_ref):
        o_ref[...] = jnp.dot(a_ref[...].astype(BF16), b_ref[...].astype(BF16), preferred_element_type=F32)

    return pl.pallas_call(
        body,
        name="ada_grad",
        out_shape=jax.ShapeDtypeStruct((d, nc), F32),
        grid=(nc // tn,),
        in_specs=[pl.BlockSpec((d, LANES), lambda j: (0, 0)), pl.BlockSpec((LANES, tn), lambda j: (0, j))],
        out_specs=pl.BlockSpec((d, tn), lambda j: (0, j)),
        compiler_params=_cp("parallel"),
    )(ca_t, dm)


def _sum_devices(g):
    n = g.shape[1]

    def body(g_ref, o_ref):
        acc = g_ref[0:SUBLANES, :]
        for dvc in range(1, N_DEV):
            acc = acc + g_ref[dvc * SUBLANES : (dvc + 1) * SUBLANES, :]
        o_ref[...] = acc

    return pl.pallas_call(
        body,
        name="sum_devices",
        out_shape=jax.ShapeDtypeStruct((SUBLANES, n), F32),
        in_specs=[pl.BlockSpec(memory_space=pltpu.VMEM)],
        out_specs=pl.BlockSpec(memory_space=pltpu.VMEM),
        compiler_params=pltpu.CompilerParams(vmem_limit_bytes=VMEM_LIMIT),
    )(g)


def _sum_chips(land, name):
    _, r, c = land.shape
    tr = _tile(r, max(SUBLANES * 2, (VMEM_LIMIT // 4) // (c * 2 * (2 * N_CHIP + 4 * 2))), SUBLANES * 2)

    def body(l_ref, o_ref):
        acc = l_ref[0].astype(F32)
        for k in range(1, N_CHIP):
            acc = acc + l_ref[k].astype(F32)
        o_ref[...] = acc

    return pl.pallas_call(
        body,
        name=name,
        out_shape=jax.ShapeDtypeStruct((r, c), F32),
        grid=(r // tr,),
        in_specs=[pl.BlockSpec((N_CHIP, tr, c), lambda i: (0, i, 0))],
        out_specs=pl.BlockSpec((tr, c), lambda i: (i, 0)),
        compiler_params=_cp("parallel"),
    )(land)


def _mesh_pos():
    return lax.axis_index("x"), lax.axis_index("y"), lax.axis_index("c")


def _other_chips(x, y):
    return [(1 - x, y), (x, 1 - y), (1 - x, 1 - y)]


def _all_gather8(x_shard, name):
    m_per, n = x_shard.shape

    def body(x_ref, out_ref, send_sems, recv_sems, local_sem):
        x, y, c = _mesh_pos()
        me, sibling = (x, y, c), (x, y, 1 - c)
        chips = _other_chips(x, y)

        def rows(px, py, pc):
            return out_ref.at[pl.ds((4 * px + 2 * py + pc) * m_per, m_per), :]

        def copy(k, block, to, src=None):
            return pltpu.make_async_remote_copy(
                src_ref=rows(*block) if src is None else src,
                dst_ref=rows(*block),
                send_sem=send_sems.at[k],
                recv_sem=recv_sems.at[k],
                device_id=to,
                device_id_type=MESH,
            )

        mine = pltpu.make_async_copy(x_ref, rows(*me), local_sem)
        mine.start()
        first = [copy(0, me, sibling, src=x_ref)]
        first += [copy(1 + j, me, (*chip, c), src=x_ref) for j, chip in enumerate(chips)]
        for cp in first:
            cp.start()
        passed = [copy(4 + j, (*chip, c), sibling) for j, chip in enumerate(chips)]
        for j, chip in enumerate(chips):
            copy(1 + j, (*chip, c), me).wait_recv()
            passed[j].start()
        copy(0, sibling, me).wait_recv()
        for j, chip in enumerate(chips):
            copy(4 + j, (*chip, 1 - c), me).wait_recv()
        for cp in first + passed:
            cp.wait_send()
        mine.wait()

    return pl.pallas_call(
        body,
        name=name,
        out_shape=jax.ShapeDtypeStruct((N_DEV * m_per, n), x_shard.dtype),
        in_specs=[pl.BlockSpec(memory_space=pltpu.VMEM)],
        out_specs=pl.BlockSpec(memory_space=pltpu.VMEM),
        scratch_shapes=[pltpu.SemaphoreType.DMA((7,)), pltpu.SemaphoreType.DMA((7,)), pltpu.SemaphoreType.DMA],
        compiler_params=pltpu.CompilerParams(vmem_limit_bytes=VMEM_LIMIT),
    )(x_shard)


def _chip_exchange(arrs, scatter, name):
    nt = len(arrs)

    def body(*refs):
        ins, outs = refs[:nt], refs[nt : 2 * nt]
        send_sems, recv_sems, local_sems = refs[2 * nt :]
        x, y, c = _mesh_pos()
        me = 2 * x + y
        chips = _other_chips(x, y)
        started = []
        for t in range(nt):
            src_me = ins[t].at[me] if scatter else ins[t]
            loc = pltpu.make_async_copy(src_me, outs[t].at[me], local_sems.at[t])
            loc.start()
            started.append(loc)
        sends = []
        for t in range(nt):
            for r, (px, py) in enumerate(chips):
                cp = pltpu.make_async_remote_copy(
                    src_ref=ins[t].at[2 * px + py] if scatter else ins[t],
                    dst_ref=outs[t].at[me],
                    send_sem=send_sems.at[3 * t + r],
                    recv_sem=recv_sems.at[3 * t + r],
                    device_id=(px, py, c),
                    device_id_type=MESH,
                )
                cp.start()
                sends.append(cp)
        for t in range(nt):
            for r, (px, py) in enumerate(chips):
                pltpu.make_async_remote_copy(
                    src_ref=ins[t].at[me] if scatter else ins[t],
                    dst_ref=outs[t].at[2 * px + py],
                    send_sem=send_sems.at[3 * t + r],
                    recv_sem=recv_sems.at[3 * t + r],
                    device_id=(px, py, c),
                    device_id_type=MESH,
                ).wait_recv()
        for cp in sends:
            cp.wait_send()
        for loc in started:
            loc.wait()

    def out_of(a):
        return jax.ShapeDtypeStruct(a.shape if scatter else (N_CHIP, *a.shape), a.dtype)

    return pl.pallas_call(
        body,
        name=name,
        out_shape=tuple(out_of(a) for a in arrs),
        in_specs=[pl.BlockSpec(memory_space=pl.ANY)] * nt,
        out_specs=[pl.BlockSpec(memory_space=pl.ANY)] * nt,
        scratch_shapes=[
            pltpu.SemaphoreType.DMA((3 * nt,)),
            pltpu.SemaphoreType.DMA((3 * nt,)),
            pltpu.SemaphoreType.DMA((nt,)),
        ],
    )(*arrs)


def _sibling_swap(arrs, name):
    nt = len(arrs)

    def body(*refs):
        ins, outs = refs[:nt], refs[nt : 2 * nt]
        send_sems, recv_sems = refs[2 * nt :]
        x, y, c = _mesh_pos()
        cps = [
            pltpu.make_async_remote_copy(
                src_ref=ins[t],
                dst_ref=outs[t],
                send_sem=send_sems.at[t],
                recv_sem=recv_sems.at[t],
                device_id=(x, y, 1 - c),
                device_id_type=MESH,
            )
            for t in range(nt)
        ]
        for cp in cps:
            cp.start()
        for cp in cps:
            cp.wait_recv()
        for cp in cps:
            cp.wait_send()

    return pl.pallas_call(
        body,
        name=name,
        out_shape=tuple(jax.ShapeDtypeStruct(a.shape, a.dtype) for a in arrs),
        in_specs=[pl.BlockSpec(memory_space=pl.ANY)] * nt,
        out_specs=[pl.BlockSpec(memory_space=pl.ANY)] * nt,
        scratch_shapes=[pltpu.SemaphoreType.DMA((nt,)), pltpu.SemaphoreType.DMA((nt,))],
    )(*arrs)


def _cols_from_shards(g):
    _, k, n = g.shape
    return jnp.transpose(g, (1, 0, 2)).reshape(k, N_CHIP * n)


def _cols_to_shards(a):
    k, n4 = a.shape
    return jnp.transpose(a.reshape(k, N_CHIP, n4 // N_CHIP), (1, 0, 2))


def _pad_to(vec, mult):
    n = vec.shape[0]
    return jnp.pad(vec, (0, (-n) % mult))


def kernel(x, c, positions, w_ada, b_ada, g_pre_mix, g_post_mix, w_in, g_q, w_uq, g_kv, w_ukv, conv_w_mix, conv_b_mix, w_o, g_pre_ffn, g_post_ffn, w_up, conv_w_ffn, conv_b_ffn, w_down, loss_target, m_w_ada, m_b_ada, m_g_pre_mix, m_g_post_mix, m_w_in, m_g_q, m_w_uq, m_g_kv, m_w_ukv, m_conv_w_mix, m_conv_b_mix, m_w_o, m_g_pre_ffn, m_g_post_ffn, m_w_up, m_conv_w_ffn, m_conv_b_ffn, m_w_down, v_w_ada, v_b_ada, v_g_pre_mix, v_g_post_mix, v_w_in, v_g_q, v_w_uq, v_g_kv, v_w_ukv, v_conv_w_mix, v_conv_b_mix, v_w_o, v_g_pre_ffn, v_g_post_ffn, v_w_up, v_conv_w_ffn, v_conv_b_ffn, v_w_down):
    xi, yi, ci = _mesh_pos()
    chip = 2 * xi + yi
    dev = 4 * xi + 2 * yi + ci

    s, d = x.shape[1], x.shape[2]
    ql, kl = g_q.shape[1], g_kv.shape[1]
    cwid = conv_b_mix.shape[1]
    f2 = conv_b_ffn.shape[1]
    hh = (w_uq.shape[2] * N_CHIP) // (NOPE + ROPE)
    w_att = hh * LANES
    nc_ada = w_ada.shape[2]
    lat = ql + kl + ROPE
    tc_mix = _tile(cwid, 512, LANES)
    lb = -(-(ql + kl + HEAD_PAD) // tc_mix) * tc_mix
    np_cols = lb + 3 * cwid
    assert cwid == hh * VDIM and w_att % tc_mix == 0

    x0 = x.reshape(s, d)
    tgt = loss_target.reshape(s, d)

    cwm_n, cwf_n = CONV_K * cwid // N_CHIP, CONV_K * f2 // N_CHIP
    pack_a = _pad_to(jnp.concatenate([c.reshape(-1), conv_w_mix.reshape(-1), conv_w_ffn.reshape(-1)]), SUBLANES * LANES)
    rows_a = _all_gather8(pack_a.reshape(SUBLANES, -1), "ag8_inputs").reshape(N_DEV, -1)
    c_all = rows_a[:, :d]
    south = rows_a[0::2]
    cw_mix = jnp.concatenate([south[j, d : d + cwm_n].reshape(CONV_K, -1) for j in range(N_CHIP)], axis=1)
    cw_ffn = jnp.concatenate([south[j, d + cwm_n : d + cwm_n + cwf_n].reshape(CONV_K, -1) for j in range(N_CHIP)], axis=1)

    b_cols = lax.dynamic_slice(b_ada, (0, chip * nc_ada), (1, nc_ada))
    mod_part, c_act = _ada_fwd(c_all, w_ada[0], b_cols)
    mod_rows = _all_gather8(mod_part, "ag8_mod")
    mod = jnp.concatenate(
        [lax.dynamic_slice_in_dim(mod_rows, 2 * N_DEV * j + dev, 1, axis=0) for j in range(N_CHIP)], axis=1
    )
    sh_m, sc_m, gt_m, sh_f, sc_f, gt_f = [mod[:, k * d : (k + 1) * d] for k in range(N_MOD)]

    shards = [w_in[0], w_uq[0], w_ukv[0], w_o[0], w_up[0], w_down[0]]
    g_in, g_uq, g_ukv, g_o, g_up, g_down = _chip_exchange([a.astype(BF16) for a in shards], False, "ag_weights")
    full_in = _cols_from_shards(g_in)
    w_in_p = jnp.concatenate([full_in[:, :lat], jnp.zeros((d, lb - lat), BF16), full_in[:, lat:]], axis=1)
    full_uq = _cols_from_shards(g_uq).reshape(ql, hh, NOPE + ROPE)
    w_uq_p = jnp.concatenate(
        [
            full_uq[:, :, :NOPE].reshape(ql, w_att),
            jnp.pad(full_uq[:, :, NOPE:], ((0, 0), (0, 0), (0, HEAD_PAD - ROPE))).reshape(ql, w_att),
        ],
        axis=1,
    )
    full_ukv = _cols_from_shards(g_ukv).reshape(kl, hh, NOPE + VDIM)
    w_ukv_p = jnp.concatenate([full_ukv[:, :, :NOPE].reshape(kl, w_att), full_ukv[:, :, NOPE:].reshape(kl, w_att)], axis=1)
    w_o_f = g_o.reshape(-1, d)
    w_up_f = _cols_from_shards(g_up)
    w_down_f = g_down.reshape(-1, d)

    inv_freq = 1.0 / (ROPE_THETA ** (jnp.arange(0, ROPE, 2, dtype=F32) / ROPE))
    invf = jnp.concatenate([inv_freq, inv_freq, jnp.zeros((LANES - ROPE,), F32)]).reshape(1, LANES)
    tabs = _rope_tables(positions.astype(F32).reshape(s, 1), invf)

    g_pre_mix2, g_post_mix2 = g_pre_mix, g_post_mix
    h1 = _pre_fwd(x0, g_pre_mix2, sc_m, sh_m)
    proj = _matmul(h1, w_in_p, out_dtype=F32, tm=1024, tn=768, tk=2048, name="mm_proj")
    qn, kvn, kr = _latent_fwd(proj, g_q, g_kv, tabs, lb)
    q_f = _matmul(qn, w_uq_p, out_dtype=F32, tm=1024, tn=1024, tk=2048, name="mm_q")
    q_p = _q_rope(q_f, tabs)
    kv_p = _matmul(kvn, w_ukv_p, out_dtype=BF16, tm=1024, tn=1024, tk=2048, name="mm_kv")
    cat, lse = _attn_fwd(q_p, kv_p, kr, hh, w_att + cwid)
    cb_mix = conv_b_mix
    cat = _mixer_fwd(cat, proj, cw_mix, cb_mix, lb, w_att)
    mix = _matmul(cat, w_o_f, out_dtype=F32, tm=1024, tn=1024, tk=2048, name="mm_mix")

    x1, h2 = _mid_fwd(x0, mix, g_post_mix2, gt_m, g_pre_ffn, sc_f, sh_f)
    up = _matmul(h2, w_up_f, out_dtype=F32, tm=1024, tn=1408, tk=2048, name="mm_up")
    act = _ffn_act_fwd(up, cw_ffn, conv_b_ffn)
    y = _matmul(act, w_down_f, out_dtype=F32, tm=1024, tn=1024, tk=1408, name="mm_down")
    dx2, dy, s_fin = _final(x1, y, tgt, g_post_ffn, gt_f)

    dw_down = _matmul(act, dy, ta=True, out_dtype=BF16, tm=1408, tn=1024, tk=1024, name="mm_dw_down")
    dact = _matmul(dy, w_down_f, tb=True, out_dtype=F32, tm=1024, tn=1408, tk=2048, name="mm_dact")
    dup_a, dup_g, s_fa, s_fg = _ffn_act_bwd(dact, up, cw_ffn, conv_b_ffn)
    dup = jnp.concatenate([dup_a, dup_g], axis=1)
    s_ffn = jnp.concatenate([s_fa, s_fg], axis=1)
    dw_up = _matmul(h2, dup, ta=True, out_dtype=BF16, tm=1024, tn=1408, tk=1024, name="mm_dw_up")
    dh2 = _matmul(dup, w_up_f, tb=True, out_dtype=F32, tm=1024, tn=1024, tk=1408, name="mm_dh2")
    dx1, dmix, s_mid = _mid_bwd(dh2, dx2, x1, mix, g_pre_ffn, sc_f, g_post_mix2, gt_m)

    dw_o = _matmul(cat, dmix, ta=True, out_dtype=BF16, tm=1024, tn=1024, tk=1024, name="mm_dw_o")
    dcat = _matmul(dmix, w_o_f, tb=True, out_dtype=F32, tm=1024, tn=1024, tk=2048, name="mm_dcat")
    dp_b, dp_c, dp_i, s_mix = _mixer_bwd(dcat, proj, cw_mix, cb_mix, lb, w_att)
    dkv_k, dkv_v, dkr_h = _attn_bwd_dkv(q_p, kv_p, kr, cat, dcat, lse, hh)
    dkv_p = jnp.concatenate([dkv_k, dkv_v], axis=1)
    dq_n, dq_r = _attn_bwd_dq(q_p, kv_p, kr, cat, dcat, lse, tabs, hh)
    dq_p = jnp.concatenate([dq_n, dq_r], axis=1)
    dw_uq_p = _matmul(qn, dq_p, ta=True, out_dtype=BF16, tm=1024, tn=1024, tk=1024, name="mm_dw_uq")
    dqn = _matmul(dq_p, w_uq_p, tb=True, out_dtype=F32, tm=1024, tn=1024, tk=2048, name="mm_dqn")
    dw_ukv_p = _matmul(kvn, dkv_p, ta=True, out_dtype=BF16, tm=1024, tn=1024, tk=1024, name="mm_dw_ukv")
    dkvn = _matmul(dkv_p, w_ukv_p, tb=True, out_dtype=F32, tm=1024, tn=1024, tk=2048, name="mm_dkvn")
    dp_lat, s_lat = _latent_bwd(proj, dqn, dkvn, dkr_h, g_q, g_kv, tabs, lb)
    dproj = jnp.concatenate([dp_lat, dp_b, dp_c, dp_i], axis=1)
    dw_in_p = _matmul(h1, dproj, ta=True, out_dtype=BF16, tm=1024, tn=1536, tk=1024, name="mm_dw_in")
    dh1 = _matmul(dproj, w_in_p, tb=True, out_dtype=F32, tm=1024, tn=1024, tk=1536, name="mm_dh1")
    grad_x, s_first = _first_bwd(dh1, dx1, x0, g_pre_mix2, sc_m)

    dw_in_f = jnp.concatenate([dw_in_p[:, :lat], dw_in_p[:, lb:]], axis=1)
    uq3 = dw_uq_p.reshape(ql, 2, hh, LANES)
    dw_uq_f = jnp.concatenate([uq3[:, 0], uq3[:, 1, :, :ROPE]], axis=2).reshape(ql, hh * (NOPE + ROPE))
    ukv3 = dw_ukv_p.reshape(kl, 2, hh, LANES)
    dw_ukv_f = jnp.concatenate([ukv3[:, 0], ukv3[:, 1]], axis=2).reshape(kl, hh * (NOPE + VDIM))
    to_send = [
        _cols_to_shards(dw_in_f),
        _cols_to_shards(dw_uq_f),
        _cols_to_shards(dw_ukv_f),
        dw_o.reshape(N_CHIP, -1, d),
        _cols_to_shards(dw_up),
        dw_down.reshape(N_CHIP, -1, d),
    ]
    landed = _chip_exchange(to_send, True, "rs_grads")
    names = ["w_in", "w_uq", "w_ukv", "w_o", "w_up", "w_down"]
    part = [_sum_chips(l, "sum_chips_" + n) for l, n in zip(landed, names)]
    other = _sibling_swap(part, "sibling_swap")

    dmod = jnp.concatenate([s_first[0:1], s_first[1:2], s_mid[3:4], s_mid[0:1], s_mid[1:2], s_fin[0:1]], axis=1)
    small = [
        dmod,
        s_first[2:3],
        s_mid[4:5],
        s_lat[0:1, :ql],
        s_lat[0:1, ql : ql + kl],
        s_mix[3:4],
        s_mid[2:3],
        s_fin[1:2],
        s_ffn[3:4],
        s_mix[0:3].reshape(1, -1),
        s_ffn[0:3].reshape(1, -1),
        s_fin[3:4, :LANES],
    ]
    sizes = [a.shape[1] for a in small]
    offs = [0]
    for n in sizes:
        offs.append(offs[-1] + n)
    pack_g = _pad_to(jnp.concatenate(small, axis=1).reshape(-1), SUBLANES * LANES * SUBLANES).reshape(SUBLANES, -1)
    gathered = _all_gather8(pack_g, "ag8_small_grads")
    tot = _sum_devices(gathered).reshape(-1)
    part_of = lambda k: tot[offs[k] : offs[k + 1]]
    dmod_all = gathered.reshape(N_DEV, -1)[:, : N_MOD * d]
    loss = part_of(11)[0]

    g_b_ada = part_of(0).reshape(1, -1)
    g_vecs = [part_of(k).reshape(1, -1) for k in range(1, 9)]
    g_cw_mix = lax.dynamic_slice(part_of(9).reshape(CONV_K, cwid), (0, chip * (cwid // N_CHIP)), (CONV_K, cwid // N_CHIP))
    g_cw_ffn = lax.dynamic_slice(part_of(10).reshape(CONV_K, f2), (0, chip * (f2 // N_CHIP)), (CONV_K, f2 // N_CHIP))

    dm_cols = lax.dynamic_slice(dmod_all, (0, chip * nc_ada), (N_DEV, nc_ada))
    g_w_ada = _ada_grad(
        jnp.pad(c_act.T, ((0, 0), (0, LANES - N_DEV))), jnp.pad(dm_cols, ((0, LANES - N_DEV), (0, 0)))
    )

    big_w = [w_in, w_uq, w_ukv, w_o, w_up, w_down]
    big_m = [m_w_in, m_w_uq, m_w_ukv, m_w_o, m_w_up, m_w_down]
    big_v = [v_w_in, v_w_uq, v_w_ukv, v_w_o, v_w_up, v_w_down]
    big = {}
    for n, w_, m_, v_, p_, o_ in zip(names, big_w, big_m, big_v, part, other):
        big[n] = [a[None] for a in _adamw(w_[0], m_[0], v_[0], [p_, o_], "adamw_" + n)]
    big["w_ada"] = [a[None] for a in _adamw(w_ada[0], m_w_ada[0], v_w_ada[0], [g_w_ada], "adamw_w_ada")]

    sm_names = ["b_ada", "g_pre_mix", "g_post_mix", "g_q", "g_kv", "conv_b_mix", "g_pre_ffn", "g_post_ffn", "conv_b_ffn",
                "conv_w_mix", "conv_w_ffn"]
    sm_w = [b_ada, g_pre_mix, g_post_mix, g_q, g_kv, conv_b_mix, g_pre_ffn, g_post_ffn, conv_b_ffn, conv_w_mix, conv_w_ffn]
    sm_m = [m_b_ada, m_g_pre_mix, m_g_post_mix, m_g_q, m_g_kv, m_conv_b_mix, m_g_pre_ffn, m_g_post_ffn, m_conv_b_ffn,
            m_conv_w_mix, m_conv_w_ffn]
    sm_v = [v_b_ada, v_g_pre_mix, v_g_post_mix, v_g_q, v_g_kv, v_conv_b_mix, v_g_pre_ffn, v_g_post_ffn, v_conv_b_ffn,
            v_conv_w_mix, v_conv_w_ffn]
    sm_g = [g_b_ada] + g_vecs + [g_cw_mix, g_cw_ffn]
    flat = lambda arrs: jnp.concatenate([a.reshape(1, -1) for a in arrs], axis=1)
    sm_out = _adamw(flat(sm_w), flat(sm_m), flat(sm_v), [flat(sm_g)], "adamw_small")
    sm = {}
    off = 0
    for n, w_ in zip(sm_names, sm_w):
        sm[n] = [o[:, off : off + w_.size].reshape(w_.shape) for o in sm_out]
        off += w_.size

    order = ["w_ada", "b_ada", "g_pre_mix", "g_post_mix", "w_in", "g_q", "w_uq", "g_kv", "w_ukv", "conv_w_mix", "conv_b_mix",
             "w_o", "g_pre_ffn", "g_post_ffn", "w_up", "conv_w_ffn", "conv_b_ffn", "w_down"]
    res = {**big, **sm}
    outs = [loss, grad_x.reshape(x.shape)]
    for k in range(4):
        outs += [res[n][k] for n in order]
    return tuple(outs)
```

```python
import math

import jax
import jax.numpy as jnp
from jax import lax
from jax.experimental import pallas as pl
from jax.experimental.pallas import tpu as pltpu

F32 = jnp.float32
BF16 = jnp.bfloat16
MESH = pl.DeviceIdType.MESH

N_DEV = 8
N_CHIP = 4
LANES = 128
SUBLANES = 8
VMEM_LIMIT = 56 * 2**20

NOPE = 128
ROPE = 64
VDIM = 128
HEAD_PAD = 128
ROPE_THETA = 10000.0
RMS_EPS = 1e-6
N_MOD = 6
CONV_K = 3
ATT_BLOCK = 512
NEG = -1e30

ADAM_LR = 0.001
ADAM_B1 = 0.9
ADAM_B2 = 0.999
ADAM_EPS = 1e-08
ADAM_WD = 0.01
ADAM_STEP = 10


def _tile(n, pref, align):
    if n <= pref:
        return n
    t = (pref // align) * align
    while t >= align:
        if n % t == 0:
            return t
        t -= align
    return n


def _cp(*sem):
    return pltpu.CompilerParams(dimension_semantics=sem, vmem_limit_bytes=VMEM_LIMIT)


def _rsq(x):
    return lax.rsqrt(jnp.mean(x * x, axis=-1, keepdims=True) + RMS_EPS)


def _norm_bwd(dn, n, r):
    return r * (dn - n * jnp.mean(dn * n, axis=-1, keepdims=True))


def _colsum(a):
    return jnp.sum(a, axis=0, keepdims=True)


def _matmul(a, b, *, ta=False, tb=False, out_dtype, tm, tn, tk, name):
    (k_a, m) = a.shape if ta else a.shape[::-1]
    (n, k_b) = b.shape if tb else b.shape[::-1]
    assert k_a == k_b, (a.shape, b.shape, ta, tb)
    tm, tn, tk = _tile(m, tm, LANES), _tile(n, tn, LANES), _tile(k_a, tk, LANES)
    nk = k_a // tk
    a_spec = pl.BlockSpec((tk, tm), lambda i, j, k: (k, i)) if ta else pl.BlockSpec((tm, tk), lambda i, j, k: (i, k))
    b_spec = pl.BlockSpec((tn, tk), lambda i, j, k: (j, k)) if tb else pl.BlockSpec((tk, tn), lambda i, j, k: (k, j))
    dims = (((0 if ta else 1,), (1 if tb else 0,)), ((), ()))

    def body(a_ref, b_ref, o_ref, *acc):
        p = lax.dot_general(a_ref[...].astype(BF16), b_ref[...].astype(BF16), dims, preferred_element_type=F32)
        if nk == 1:
            o_ref[...] = p.astype(o_ref.dtype)
        else:
            k = pl.program_id(2)

            @pl.when(k == 0)
            def _():
                acc[0][...] = p

            @pl.when(k > 0)
            def _():
                acc[0][...] += p

            @pl.when(k == nk - 1)
            def _():
                o_ref[...] = acc[0][...].astype(o_ref.dtype)

    return pl.pallas_call(
        body,
        name=name,
        out_shape=jax.ShapeDtypeStruct((m, n), out_dtype),
        grid=(m // tm, n // tn, nk),
        in_specs=[a_spec, b_spec],
        out_specs=pl.BlockSpec((tm, tn), lambda i, j, k: (i, j)),
        scratch_shapes=[] if nk == 1 else [pltpu.VMEM((tm, tn), F32)],
        compiler_params=_cp("parallel", "parallel", "arbitrary"),
    )(a, b)


def _rope_tables(pos_col, invf):
    s = pos_col.shape[0]
    ts = _tile(s, 1024, SUBLANES)
    half = ROPE // 2

    def body(p_ref, f_ref, c_ref, sa_ref, sb_ref):
        ang = p_ref[...] * f_ref[...]
        lane = lax.broadcasted_iota(jnp.int32, ang.shape, 1)
        cs, sn = jnp.cos(ang), jnp.sin(ang)
        c_ref[...] = jnp.where(lane < ROPE, cs, 0.0)
        sa_ref[...] = jnp.where((lane >= half) & (lane < ROPE), sn, 0.0)
        sb_ref[...] = jnp.where(lane < half, -sn, 0.0)

    tab = jax.ShapeDtypeStruct((s, LANES), F32)
    return pl.pallas_call(
        body,
        name="rope_tables",
        out_shape=(tab, tab, tab),
        grid=(s // ts,),
        in_specs=[pl.BlockSpec((ts, 1), lambda i: (i, 0)), pl.BlockSpec((1, LANES), lambda i: (0, 0))],
        out_specs=[pl.BlockSpec((ts, LANES), lambda i: (i, 0))] * 3,
        compiler_params=_cp("parallel"),
    )(pos_col, invf)


def _widen(t, w):
    return t if w == LANES else jnp.tile(t, (1, w // LANES))


def _rope(x, c, sa, sb):
    w = x.shape[1]
    c, sa, sb = _widen(c, w), _widen(sa, w), _widen(sb, w)
    return x * c + pltpu.roll(x, ROPE // 2, 1) * sa + pltpu.roll(x, w - ROPE // 2, 1) * sb


def _rope_t(d, c, sa, sb):
    w = d.shape[1]
    c, sa, sb = _widen(c, w), _widen(sa, w), _widen(sb, w)
    return d * c + pltpu.roll(d * sa, w - ROPE // 2, 1) + pltpu.roll(d * sb, ROPE // 2, 1)


def _ada_fwd(c_all, w, b):
    d, nc = w.shape
    tn = _tile(nc, 512, LANES)

    def body(c_ref, w_ref, b_ref, o_ref, ca_ref):
        cv = c_ref[...]
        ca = cv * jax.nn.sigmoid(cv)
        ca_ref[...] = ca
        o_ref[...] = jnp.dot(ca.astype(BF16), w_ref[...].astype(BF16), preferred_element_type=F32) + b_ref[...]

    return pl.pallas_call(
        body,
        name="ada_fwd",
        out_shape=(jax.ShapeDtypeStruct((N_DEV, nc), F32), jax.ShapeDtypeStruct((N_DEV, d), F32)),
        grid=(nc // tn,),
        in_specs=[
            pl.BlockSpec((N_DEV, d), lambda j: (0, 0)),
            pl.BlockSpec((d, tn), lambda j: (0, j)),
            pl.BlockSpec((1, tn), lambda j: (0, j)),
        ],
        out_specs=[pl.BlockSpec((N_DEV, tn), lambda j: (0, j)), pl.BlockSpec((N_DEV, d), lambda j: (0, 0))],
        compiler_params=_cp("arbitrary"),
    )(c_all, w, b)


def _rows(ts, d):
    return pl.BlockSpec((ts, d), lambda i: (i, 0))


def _vec(d):
    return pl.BlockSpec((1, d), lambda i: (0, 0))


def _sums(d):
    return pl.BlockSpec((SUBLANES, d), lambda i: (0, 0))


def _acc_rows(ref, i, rows):
    @pl.when(i == 0)
    def _():
        ref[...] = jnp.zeros(ref.shape, ref.dtype)

    for k, r in enumerate(rows):
        ref[k : k + 1, :] += r


def _pre_fwd(x, g, sc, sh):
    s, d = x.shape
    ts = _tile(s, 512, SUBLANES)

    def body(x_ref, g_ref, sc_ref, sh_ref, h_ref):
        xv = x_ref[...]
        h_ref[...] = (((xv * _rsq(xv)) * g_ref[...]) * (1.0 + sc_ref[...]) + sh_ref[...]).astype(BF16)

    return pl.pallas_call(
        body,
        name="pre_mix_fwd",
        out_shape=jax.ShapeDtypeStruct((s, d), BF16),
        grid=(s // ts,),
        in_specs=[_rows(ts, d), _vec(d), _vec(d), _vec(d)],
        out_specs=_rows(ts, d),
        compiler_params=_cp("parallel"),
    )(x, g, sc, sh)


def _mid_fwd(x0, mix, g_post, gt, g_pre, sc, sh):
    s, d = x0.shape
    ts = _tile(s, 256, SUBLANES)

    def body(x_ref, m_ref, gp_ref, gt_ref, g_ref, sc_ref, sh_ref, x1_ref, h_ref):
        mv = m_ref[...]
        x1 = x_ref[...] + gt_ref[...] * ((mv * _rsq(mv)) * gp_ref[...])
        x1_ref[...] = x1
        h_ref[...] = (((x1 * _rsq(x1)) * g_ref[...]) * (1.0 + sc_ref[...]) + sh_ref[...]).astype(BF16)

    return pl.pallas_call(
        body,
        name="mid_fwd",
        out_shape=(jax.ShapeDtypeStruct((s, d), F32), jax.ShapeDtypeStruct((s, d), BF16)),
        grid=(s // ts,),
        in_specs=[_rows(ts, d), _rows(ts, d)] + [_vec(d)] * 5,
        out_specs=[_rows(ts, d), _rows(ts, d)],
        compiler_params=_cp("parallel"),
    )(x0, mix, g_post, gt, g_pre, sc, sh)


def _final(x1, y, tgt, g_post, gt):
    s, d = x1.shape
    ts = _tile(s, 256, SUBLANES)
    ni = s // ts

    def body(x_ref, y_ref, t_ref, gp_ref, gt_ref, dx_ref, dy_ref, s_ref):
        i = pl.program_id(0)
        yv, gp, gt_v = y_ref[...], gp_ref[...], gt_ref[...]
        r = _rsq(yv)
        n = yv * r
        err = (x_ref[...] + gt_v * (n * gp)) - t_ref[...]
        dx = err * (1.0 / d)
        dx_ref[...] = dx
        dy_ref[...] = _norm_bwd(dx * (gt_v * gp), n, r).astype(BF16)
        _acc_rows(s_ref, i, [_colsum(dx * (n * gp)), _colsum(dx * gt_v * n), _colsum(err * err)])

        @pl.when(i == ni - 1)
        def _():
            tot = jnp.sum(s_ref[2:3, :], axis=1, keepdims=True) * (0.5 / d)
            s_ref[3:4, :] = jnp.broadcast_to(tot, (1, d))

    return pl.pallas_call(
        body,
        name="final_fwd_bwd",
        out_shape=(
            jax.ShapeDtypeStruct((s, d), F32),
            jax.ShapeDtypeStruct((s, d), BF16),
            jax.ShapeDtypeStruct((SUBLANES, d), F32),
        ),
        grid=(ni,),
        in_specs=[_rows(ts, d)] * 3 + [_vec(d)] * 2,
        out_specs=[_rows(ts, d), _rows(ts, d), _sums(d)],
        compiler_params=_cp("arbitrary"),
    )(x1, y, tgt, g_post, gt)


def _mid_bwd(dh2, dx2, x1, mix, g_pre, sc, g_post, gt):
    s, d = x1.shape
    ts = _tile(s, 256, SUBLANES)

    def body(dh_ref, dx2_ref, x_ref, m_ref, g_ref, sc_ref, gp_ref, gt_ref, dx1_ref, dm_ref, s_ref):
        i = pl.program_id(0)
        dh, xv, mv = dh_ref[...], x_ref[...], m_ref[...]
        g, sc_v, gp, gt_v = g_ref[...], sc_ref[...], gp_ref[...], gt_ref[...]
        r1 = _rsq(xv)
        n1 = xv * r1
        dx1 = dx2_ref[...] + _norm_bwd(dh * (g * (1.0 + sc_v)), n1, r1)
        dx1_ref[...] = dx1
        rm = _rsq(mv)
        nm = mv * rm
        dm_ref[...] = _norm_bwd(dx1 * (gt_v * gp), nm, rm).astype(BF16)
        _acc_rows(
            s_ref,
            i,
            [
                _colsum(dh),
                _colsum(dh * (n1 * g)),
                _colsum(dh * (1.0 + sc_v) * n1),
                _colsum(dx1 * (nm * gp)),
                _colsum(dx1 * gt_v * nm),
            ],
        )

    return pl.pallas_call(
        body,
        name="mid_bwd",
        out_shape=(
            jax.ShapeDtypeStruct((s, d), F32),
            jax.ShapeDtypeStruct((s, d), BF16),
            jax.ShapeDtypeStruct((SUBLANES, d), F32),
        ),
        grid=(s // ts,),
        in_specs=[_rows(ts, d)] * 4 + [_vec(d)] * 4,
        out_specs=[_rows(ts, d), _rows(ts, d), _sums(d)],
        compiler_params=_cp("arbitrary"),
    )(dh2, dx2, x1, mix, g_pre, sc, g_post, gt)


def _first_bwd(dh1, dx1, x0, g, sc):
    s, d = x0.shape
    ts = _tile(s, 256, SUBLANES)

    def body(dh_ref, dx1_ref, x_ref, g_ref, sc_ref, dx_ref, s_ref):
        i = pl.program_id(0)
        dh, xv, gv, sc_v = dh_ref[...], x_ref[...], g_ref[...], sc_ref[...]
        r = _rsq(xv)
        n = xv * r
        dx_ref[...] = dx1_ref[...] + _norm_bwd(dh * (gv * (1.0 + sc_v)), n, r)
        _acc_rows(s_ref, i, [_colsum(dh), _colsum(dh * (n * gv)), _colsum(dh * (1.0 + sc_v) * n)])

    return pl.pallas_call(
        body,
        name="first_bwd",
        out_shape=(jax.ShapeDtypeStruct((s, d), F32), jax.ShapeDtypeStruct((SUBLANES, d), F32)),
        grid=(s // ts,),
        in_specs=[_rows(ts, d)] * 3 + [_vec(d)] * 2,
        out_specs=[_rows(ts, d), _sums(d)],
        compiler_params=_cp("arbitrary"),
    )(dh1, dx1, x0, g, sc)


def _latent_fwd(proj, g_q, g_kv, tabs, lb):
    s = proj.shape[0]
    ql, kl = g_q.shape[1], g_kv.shape[1]
    ts = _tile(s, 512, SUBLANES)

    def body(p_ref, gq_ref, gk_ref, c_ref, sa_ref, sb_ref, q_ref, kv_ref, kr_ref):
        pv = p_ref[...]
        q, kv, kr = pv[:, :ql], pv[:, ql : ql + kl], pv[:, ql + kl : ql + kl + HEAD_PAD]
        q_ref[...] = ((q * _rsq(q)) * gq_ref[...]).astype(BF16)
        kv_ref[...] = ((kv * _rsq(kv)) * gk_ref[...]).astype(BF16)
        kr_ref[...] = _rope(kr, c_ref[...], sa_ref[...], sb_ref[...]).astype(BF16)

    return pl.pallas_call(
        body,
        name="latent_fwd",
        out_shape=(
            jax.ShapeDtypeStruct((s, ql), BF16),
            jax.ShapeDtypeStruct((s, kl), BF16),
            jax.ShapeDtypeStruct((s, HEAD_PAD), BF16),
        ),
        grid=(s // ts,),
        in_specs=[_rows(ts, lb), _vec(ql), _vec(kl)] + [_rows(ts, LANES)] * 3,
        out_specs=[_rows(ts, ql), _rows(ts, kl), _rows(ts, HEAD_PAD)],
        compiler_params=_cp("parallel"),
    )(proj, g_q, g_kv, *tabs)


def _latent_bwd(proj, dqn, dkvn, dkr_h, g_q, g_kv, tabs, lb):
    s = proj.shape[0]
    ql, kl = g_q.shape[1], g_kv.shape[1]
    hw = dkr_h.shape[1]
    ts = _tile(s, 256, SUBLANES)
    pad = lb - ql - kl - HEAD_PAD

    def body(p_ref, dq_ref, dkv_ref, dkr_ref, gq_ref, gk_ref, c_ref, sa_ref, sb_ref, o_ref, s_ref):
        i = pl.program_id(0)
        pv = p_ref[...]
        q, kv = pv[:, :ql], pv[:, ql : ql + kl]
        dqn_v, dkvn_v = dq_ref[...], dkv_ref[...]
        rq = _rsq(q)
        nq = q * rq
        rk = _rsq(kv)
        nk = kv * rk
        dkr = dkr_ref[:, :HEAD_PAD]
        for h in range(1, hw // HEAD_PAD):
            dkr = dkr + dkr_ref[:, h * HEAD_PAD : (h + 1) * HEAD_PAD]
        parts = [
            _norm_bwd(dqn_v * gq_ref[...], nq, rq).astype(BF16),
            _norm_bwd(dkvn_v * gk_ref[...], nk, rk).astype(BF16),
            _rope_t(dkr, c_ref[...], sa_ref[...], sb_ref[...]).astype(BF16),
        ]
        if pad:
            parts.append(jnp.zeros((ts, pad), BF16))
        o_ref[...] = jnp.concatenate(parts, axis=1)
        row = [_colsum(dqn_v * nq), _colsum(dkvn_v * nk), jnp.zeros((1, lb - ql - kl), F32)]
        _acc_rows(s_ref, i, [jnp.concatenate(row, axis=1)])

    return pl.pallas_call(
        body,
        name="latent_bwd",
        out_shape=(jax.ShapeDtypeStruct((s, lb), BF16), jax.ShapeDtypeStruct((SUBLANES, lb), F32)),
        grid=(s // ts,),
        in_specs=[_rows(ts, lb), _rows(ts, ql), _rows(ts, kl), _rows(ts, hw)]
        + [_vec(ql), _vec(kl)]
        + [_rows(ts, LANES)] * 3,
        out_specs=[_rows(ts, lb), _sums(lb)],
        compiler_params=_cp("arbitrary"),
    )(proj, dqn, dkvn, dkr_h, g_q, g_kv, *tabs)


def _conv3(ext, w, b):
    return (pltpu.roll(ext, 2, 0) * w[0:1] + pltpu.roll(ext, 1, 0) * w[1:2]) + ext * w[2:3] + b


def _conv3_t(du, w):
    n = du.shape[0]
    return du * w[2:3] + pltpu.roll(du, n - 1, 0) * w[1:2] + pltpu.roll(du, n - 2, 0) * w[0:1]


def _halo_maps(ts, s):
    r8, last = ts // SUBLANES, s // SUBLANES - 1
    prev = lambda i: jnp.maximum(i * r8 - 1, 0)
    nxt = lambda i: jnp.minimum((i + 1) * r8, last)
    return prev, nxt


def _mixer_fwd(cat, proj, cw, cb, lb, col0):
    s = proj.shape[0]
    cwid = cw.shape[1]
    ts = _tile(s, 512, SUBLANES)
    tc = _tile(cwid, 512, LANES)
    assert lb % tc == 0 and col0 % tc == 0
    nj, ob, oc = cwid // tc, lb // tc, col0 // tc
    prev, _ = _halo_maps(ts, s)

    def body(_, gb_ref, gc_ref, ci_ref, pgc_ref, pci_ref, w_ref, b_ref, o_ref):
        keep = jnp.where(pl.program_id(1) > 0, 1.0, 0.0)
        ext = jnp.concatenate([pgc_ref[...] * pci_ref[...] * keep, gc_ref[...] * ci_ref[...]], axis=0)
        o_ref[...] = (gb_ref[...] * _conv3(ext, w_ref[...], b_ref[...])[SUBLANES:]).astype(BF16)

    def col(k):
        return pl.BlockSpec((ts, tc), lambda j, i: (i, ob + k * nj + j))

    def halo(k):
        return pl.BlockSpec((SUBLANES, tc), lambda j, i: (prev(i), ob + k * nj + j))

    return pl.pallas_call(
        body,
        name="mixer_fwd",
        out_shape=jax.ShapeDtypeStruct(cat.shape, BF16),
        grid=(nj, s // ts),
        in_specs=[pl.BlockSpec(memory_space=pl.ANY), col(0), col(1), col(2), halo(1), halo(2)]
        + [pl.BlockSpec((CONV_K, tc), lambda j, i: (0, j)), pl.BlockSpec((1, tc), lambda j, i: (0, j))],
        out_specs=pl.BlockSpec((ts, tc), lambda j, i: (i, oc + j)),
        input_output_aliases={0: 0},
        compiler_params=_cp("parallel", "arbitrary"),
    )(cat, proj, proj, proj, proj, proj, cw, cb)


def _mixer_bwd(dcat, proj, cw, cb, lb, col0):
    s = proj.shape[0]
    cwid = cw.shape[1]
    ts = _tile(s, 256, SUBLANES)
    tc = _tile(cwid, 512, LANES)
    nj, ob, oc = cwid // tc, lb // tc, col0 // tc
    ni = s // ts
    prev, nxt = _halo_maps(ts, s)

    def body(d_ref, dn_ref, gb_ref, gbn_ref, gc_ref, gcp_ref, gcn_ref, ci_ref, cip_ref, cin_ref, w_ref, b_ref,
             dgb_ref, dgc_ref, dci_ref, s_ref):
        i = pl.program_id(1)
        keep_p = jnp.where(i > 0, 1.0, 0.0)
        keep_n = jnp.where(i < ni - 1, 1.0, 0.0)
        w = w_ref[...]
        gc = jnp.concatenate([gcp_ref[...], gc_ref[...], gcn_ref[...]], axis=0)
        ci = jnp.concatenate([cip_ref[...] * keep_p, ci_ref[...], cin_ref[...]], axis=0)
        u = gc * ci
        cv = _conv3(u, w, b_ref[...])[SUBLANES:]
        dco = jnp.concatenate([d_ref[...], dn_ref[...] * keep_n], axis=0)
        gb = jnp.concatenate([gb_ref[...], gbn_ref[...]], axis=0)
        dgb_ref[...] = (dco * cv)[:ts].astype(BF16)
        dcv = dco * gb
        du = _conv3_t(dcv, w)[:ts]
        dgc_ref[...] = (du * ci_ref[...]).astype(BF16)
        dci_ref[...] = (du * gc_ref[...]).astype(BF16)
        dt = dcv[:ts]
        u1, u2 = pltpu.roll(u, 1, 0), pltpu.roll(u, 2, 0)
        lo, hi = SUBLANES, SUBLANES + ts
        _acc_rows(s_ref, i, [_colsum(dt * u2[lo:hi]), _colsum(dt * u1[lo:hi]), _colsum(dt * u[lo:hi]), _colsum(dt)])

    def col(k):
        return pl.BlockSpec((ts, tc), lambda j, i: (i, ob + k * nj + j))

    def halo(k, which):
        return pl.BlockSpec((SUBLANES, tc), lambda j, i: (which(i), ob + k * nj + j))

    out_col = [pl.BlockSpec((ts, tc), lambda j, i: (i, j))] * 3
    grad = jax.ShapeDtypeStruct((s, cwid), BF16)
    return pl.pallas_call(
        body,
        name="mixer_bwd",
        out_shape=(grad, grad, grad, jax.ShapeDtypeStruct((SUBLANES, cwid), F32)),
        grid=(nj, ni),
        in_specs=[
            pl.BlockSpec((ts, tc), lambda j, i: (i, oc + j)),
            pl.BlockSpec((SUBLANES, tc), lambda j, i: (nxt(i), oc + j)),
            col(0), halo(0, nxt),
            col(1), halo(1, prev), halo(1, nxt),
            col(2), halo(2, prev), halo(2, nxt),
            pl.BlockSpec((CONV_K, tc), lambda j, i: (0, j)),
            pl.BlockSpec((1, tc), lambda j, i: (0, j)),
        ],
        out_specs=out_col + [pl.BlockSpec((SUBLANES, tc), lambda j, i: (0, j))],
        compiler_params=_cp("parallel", "arbitrary"),
    )(dcat, dcat, proj, proj, proj, proj, proj, proj, proj, proj, cw, cb)


def _ffn_act_fwd(up, cw, cb):
    s, f2 = up.shape
    f = f2 // 2
    ts = _tile(s, 512, SUBLANES)
    tc = _tile(f, 512, LANES)
    nj = f // tc
    prev, _ = _halo_maps(ts, s)

    def body(ua_ref, ug_ref, pa_ref, pg_ref, wa_ref, wg_ref, ba_ref, bg_ref, o_ref):
        keep = jnp.where(pl.program_id(1) > 0, 1.0, 0.0)

        def conv(u_ref, p_ref, w_ref, b_ref):
            ext = jnp.concatenate([p_ref[...] * keep, u_ref[...]], axis=0)
            return _conv3(ext, w_ref[...], b_ref[...])[SUBLANES:]

        a = conv(ua_ref, pa_ref, wa_ref, ba_ref)
        g = conv(ug_ref, pg_ref, wg_ref, bg_ref)
        o_ref[...] = ((g * jax.nn.sigmoid(g)) * a).astype(BF16)

    def col(k):
        return pl.BlockSpec((ts, tc), lambda j, i: (i, k * nj + j))

    def halo(k):
        return pl.BlockSpec((SUBLANES, tc), lambda j, i: (prev(i), k * nj + j))

    def wspec(rows, k):
        return pl.BlockSpec((rows, tc), lambda j, i: (0, k * nj + j))

    return pl.pallas_call(
        body,
        name="ffn_act_fwd",
        out_shape=jax.ShapeDtypeStruct((s, f), BF16),
        grid=(nj, s // ts),
        in_specs=[col(0), col(1), halo(0), halo(1), wspec(CONV_K, 0), wspec(CONV_K, 1), wspec(1, 0), wspec(1, 1)],
        out_specs=pl.BlockSpec((ts, tc), lambda j, i: (i, j)),
        compiler_params=_cp("parallel", "arbitrary"),
    )(up, up, up, up, cw, cw, cb, cb)


def _ffn_act_bwd(dact, up, cw, cb):
    s, f2 = up.shape
    f = f2 // 2
    ts = _tile(s, 256, SUBLANES)
    tc = _tile(f, 512, LANES)
    nj, ni = f // tc, s // ts
    prev, nxt = _halo_maps(ts, s)

    def body(d_ref, dn_ref, ua_ref, uap_ref, uan_ref, ug_ref, ugp_ref, ugn_ref, wa_ref, wg_ref, ba_ref, bg_ref,
             dua_ref, dug_ref, sa_ref, sg_ref):
        i = pl.program_id(1)
        keep_p = jnp.where(i > 0, 1.0, 0.0)
        keep_n = jnp.where(i < ni - 1, 1.0, 0.0)
        wa, wg = wa_ref[...], wg_ref[...]
        exta = jnp.concatenate([uap_ref[...] * keep_p, ua_ref[...], uan_ref[...]], axis=0)
        extg = jnp.concatenate([ugp_ref[...] * keep_p, ug_ref[...], ugn_ref[...]], axis=0)
        a = _conv3(exta, wa, ba_ref[...])[SUBLANES:]
        g = _conv3(extg, wg, bg_ref[...])[SUBLANES:]
        dact_v = jnp.concatenate([d_ref[...], dn_ref[...] * keep_n], axis=0)
        sg = jax.nn.sigmoid(g)
        da = dact_v * (g * sg)
        dg = dact_v * a * (sg * (1.0 + g * (1.0 - sg)))
        lo, hi = SUBLANES, SUBLANES + ts

        def back(du, ext, w, dup_ref, s_ref):
            dup_ref[...] = _conv3_t(du, w)[:ts].astype(BF16)
            dt = du[:ts]
            e1, e2 = pltpu.roll(ext, 1, 0), pltpu.roll(ext, 2, 0)
            _acc_rows(s_ref, i, [_colsum(dt * e2[lo:hi]), _colsum(dt * e1[lo:hi]), _colsum(dt * ext[lo:hi]), _colsum(dt)])

        back(da, exta, wa, dua_ref, sa_ref)
        back(dg, extg, wg, dug_ref, sg_ref)

    def col(k):
        return pl.BlockSpec((ts, tc), lambda j, i: (i, k * nj + j))

    def halo(k, which):
        return pl.BlockSpec((SUBLANES, tc), lambda j, i: (which(i), k * nj + j))

    def wspec(rows, k):
        return pl.BlockSpec((rows, tc), lambda j, i: (0, k * nj + j))

    half = pl.BlockSpec((ts, tc), lambda j, i: (i, j))
    half_sums = pl.BlockSpec((SUBLANES, tc), lambda j, i: (0, j))
    return pl.pallas_call(
        body,
        name="ffn_act_bwd",
        out_shape=(
            jax.ShapeDtypeStruct((s, f), BF16),
            jax.ShapeDtypeStruct((s, f), BF16),
            jax.ShapeDtypeStruct((SUBLANES, f), F32),
            jax.ShapeDtypeStruct((SUBLANES, f), F32),
        ),
        grid=(nj, ni),
        in_specs=[
            pl.BlockSpec((ts, tc), lambda j, i: (i, j)),
            pl.BlockSpec((SUBLANES, tc), lambda j, i: (nxt(i), j)),
            col(0), halo(0, prev), halo(0, nxt),
            col(1), halo(1, prev), halo(1, nxt),
            wspec(CONV_K, 0), wspec(CONV_K, 1), wspec(1, 0), wspec(1, 1),
        ],
        out_specs=[half, half, half_sums, half_sums],
        compiler_params=_cp("parallel", "arbitrary"),
    )(dact, dact, up, up, up, up, up, up, cw, cw, cb, cb)


ATT_SCALE = 1.0 / math.sqrt(NOPE + ROPE)
LOG2E = math.log2(math.e)
ATT_C2 = ATT_SCALE * LOG2E
ATT_SUB = 256
STAT_SPLIT = 64
NT = (((1,), (1,)), ((), ()))
TN = (((0,), (0,)), ((), ()))


def _head_cat(q, kv, kr, tabs, n_heads):
    s, w2 = q.shape
    w = w2 // 2
    ts = _tile(s, 512, SUBLANES)
    hd = NOPE + HEAD_PAD

    def body(q_ref, kv_ref, kr_ref, c_ref, sa_ref, sb_ref, qc_ref, kc_ref):
        qv = q_ref[...]
        qr = _rope(qv[:, w:], c_ref[...], sa_ref[...], sb_ref[...]).astype(BF16)
        krv = kr_ref[...]
        for h in range(n_heads):
            qc_ref[:, h * hd : h * hd + NOPE] = qv[:, h * NOPE : (h + 1) * NOPE].astype(BF16)
            qc_ref[:, h * hd + NOPE : (h + 1) * hd] = qr[:, h * HEAD_PAD : (h + 1) * HEAD_PAD]
            kc_ref[:, h * hd : h * hd + NOPE] = kv_ref[:, h * NOPE : (h + 1) * NOPE]
            kc_ref[:, h * hd + NOPE : (h + 1) * hd] = krv

    out = jax.ShapeDtypeStruct((s, n_heads * hd), BF16)
    return pl.pallas_call(
        body,
        name="head_cat",
        out_shape=(out, out),
        grid=(s // ts,),
        in_specs=[_rows(ts, w2), _rows(ts, w), _rows(ts, HEAD_PAD)] + [_rows(ts, LANES)] * 3,
        out_specs=[_rows(ts, n_heads * hd)] * 2,
        compiler_params=_cp("parallel"),
    )(q, kv, kr, *tabs)


def _attn_fwd(qc, kc, kv, n_heads, cat_cols):
    s = qc.shape[0]
    t = _tile(s, ATT_BLOCK, LANES)
    sub = _tile(t, ATT_SUB, LANES)
    hh = n_heads
    hd = NOPE + HEAD_PAD

    def body(q_ref, k_ref, v_ref, o_ref, lse_ref, m_s, l_s, acc_s):
        i = pl.program_id(1)
        m_s[...] = jnp.full(m_s.shape, NEG, F32)
        l_s[...] = jnp.zeros(l_s.shape, F32)
        acc_s[...] = jnp.zeros(acc_s.shape, F32)

        def chunk(k0, diag):
            for r0 in range(0, t, sub):
                ncol = r0 + sub if diag else t
                rows = pl.ds(r0, sub)
                kk = k_ref[pl.ds(k0, ncol), :]
                sc = lax.dot_general(q_ref[rows, :], kk, NT, preferred_element_type=F32)
                if diag:
                    row = lax.broadcasted_iota(jnp.int32, sc.shape, 0) + r0
                    col = lax.broadcasted_iota(jnp.int32, sc.shape, 1)
                    sc = jnp.where(col <= row, sc, NEG)
                m_prev = m_s[rows, :]
                m_new = jnp.maximum(m_prev, jnp.max(sc, axis=1, keepdims=True))
                alpha = jnp.exp2((m_prev - m_new) * ATT_C2)
                p = jnp.exp2((sc - m_new) * ATT_C2)
                l_s[rows, :] = alpha * l_s[rows, :] + jnp.sum(p, axis=1, keepdims=True)
                pv = jnp.dot(p.astype(BF16), v_ref[pl.ds(k0, ncol), :], preferred_element_type=F32)
                acc_s[rows, :] = alpha * acc_s[rows, :] + pv
                m_s[rows, :] = m_new

        def loop_body(k, carry):
            chunk(pl.multiple_of(k * t, t), False)
            return carry

        lax.fori_loop(0, i, loop_body, 0)
        chunk(pl.multiple_of(i * t, t), True)
        l = l_s[...]
        o_ref[...] = (acc_s[...] / l).astype(BF16)
        lse_ref[...] = jnp.broadcast_to(m_s[...] * ATT_C2 + jnp.log(l) * LOG2E, lse_ref.shape)

    return pl.pallas_call(
        body,
        name="attn_fwd",
        out_shape=(jax.ShapeDtypeStruct((s, cat_cols), BF16), jax.ShapeDtypeStruct((s, hh * LANES), F32)),
        grid=(hh, s // t),
        in_specs=[
            pl.BlockSpec((t, hd), lambda h, i: (i, h)),
            pl.BlockSpec((s, hd), lambda h, i: (0, h)),
            pl.BlockSpec((s, VDIM), lambda h, i: (0, hh + h)),
        ],
        out_specs=[pl.BlockSpec((t, VDIM), lambda h, i: (i, h)), pl.BlockSpec((t, LANES), lambda h, i: (i, h))],
        scratch_shapes=[pltpu.VMEM((t, 1), F32), pltpu.VMEM((t, 1), F32), pltpu.VMEM((t, VDIM), F32)],
        compiler_params=_cp("parallel", "parallel"),
    )(qc, kc, kv)


def _attn_bwd_prep(cat, dcat, lse2, n_heads):
    s, w = lse2.shape
    ts = _tile(s, 512, SUBLANES)

    def body(o_ref, do_ref, lse_ref, dob_ref, st_ref):
        do = do_ref[...]
        dob_ref[...] = do.astype(BF16)
        prod = do * o_ref[...].astype(F32)
        lane = lax.broadcasted_iota(jnp.int32, (ts, LANES), 1)
        for h in range(n_heads):
            cols = slice(h * LANES, (h + 1) * LANES)
            dsum = jnp.sum(prod[:, cols], axis=1, keepdims=True)
            st_ref[:, cols] = jnp.where(lane < STAT_SPLIT, lse_ref[:, cols], dsum)

    return pl.pallas_call(
        body,
        name="attn_bwd_prep",
        out_shape=(jax.ShapeDtypeStruct((s, w), BF16), jax.ShapeDtypeStruct((s, w), F32)),
        grid=(s // ts,),
        in_specs=[_rows(ts, w)] * 3,
        out_specs=[_rows(ts, w)] * 2,
        compiler_params=_cp("parallel"),
    )(cat, dcat, lse2)


def _attn_bwd(qc, kc, kv, dob, stats, n_heads):
    s = qc.shape[0]
    t = _tile(s, ATT_BLOCK, LANES)
    sub = _tile(t, ATT_SUB, LANES)
    nb = s // t
    hh = n_heads
    hd = NOPE + HEAD_PAD
    w = hh * LANES

    def body(q_ref, k_ref, v_ref, do_ref, st_ref, dq_ref, dkn_ref, dv_ref, dkr_ref, dk_s, dv_s):
        j = pl.program_id(1)

        @pl.when(j == 0)
        def _():
            dq_ref[...] = jnp.zeros(dq_ref.shape, F32)

        dk_s[...] = jnp.zeros(dk_s.shape, F32)
        dv_s[...] = jnp.zeros(dv_s.shape, F32)
        kk, vv = k_ref[...], v_ref[...]

        def pair(i0, diag):
            for r0 in range(0, t, sub):
                rows = pl.ds(i0 + r0, sub)
                qq, do, st = q_ref[rows, :], do_ref[rows, :], st_ref[rows, :]
                sc = lax.dot_general(qq, kk, NT, preferred_element_type=F32)
                if diag:
                    row = lax.broadcasted_iota(jnp.int32, sc.shape, 0) + r0
                    col = lax.broadcasted_iota(jnp.int32, sc.shape, 1)
                    sc = jnp.where(col <= row, sc, NEG)
                p = jnp.exp2(sc * ATT_C2 - st[:, 0:1])
                dv_s[...] += lax.dot_general(p.astype(BF16), do, TN, preferred_element_type=F32)
                dp = lax.dot_general(do, vv, NT, preferred_element_type=F32)
                ds = (p * (dp - st[:, STAT_SPLIT : STAT_SPLIT + 1]) * ATT_SCALE).astype(BF16)
                dk_s[...] += lax.dot_general(ds, qq, TN, preferred_element_type=F32)
                dq_ref[rows, :] += jnp.dot(ds, kk, preferred_element_type=F32)

        pair(pl.multiple_of(j * t, t), True)

        def loop_body(i, carry):
            pair(pl.multiple_of(i * t, t), False)
            return carry

        lax.fori_loop(j + 1, nb, loop_body, 0)
        dkn_ref[...] = dk_s[:, :NOPE].astype(BF16)
        dv_ref[...] = dv_s[...].astype(BF16)
        dkr_ref[...] = dk_s[:, NOPE:]

    whole = lambda width, off: pl.BlockSpec((s, width), lambda h, j: (0, off + h))
    blk = lambda width, off: pl.BlockSpec((t, width), lambda h, j: (j, off + h))
    return pl.pallas_call(
        body,
        name="attn_bwd",
        out_shape=(
            jax.ShapeDtypeStruct((s, hh * hd), F32),
            jax.ShapeDtypeStruct((s, w), BF16),
            jax.ShapeDtypeStruct((s, w), BF16),
            jax.ShapeDtypeStruct((s, w), F32),
        ),
        grid=(hh, nb),
        in_specs=[whole(hd, 0), blk(hd, 0), blk(VDIM, hh), whole(VDIM, 0), whole(LANES, 0)],
        out_specs=[whole(hd, 0), blk(NOPE, 0), blk(VDIM, 0), blk(HEAD_PAD, 0)],
        scratch_shapes=[pltpu.VMEM((t, hd), F32), pltpu.VMEM((t, VDIM), F32)],
        compiler_params=_cp("parallel", "arbitrary"),
    )(qc, kc, kv, dob, stats)


def _dq_unrope(dq, tabs, n_heads):
    s = dq.shape[0]
    hd = NOPE + HEAD_PAD
    w = n_heads * LANES
    ts = _tile(s, 512, SUBLANES)

    def body(d_ref, c_ref, sa_ref, sb_ref, o_ref):
        c, sa, sb = c_ref[...], sa_ref[...], sb_ref[...]
        for h in range(n_heads):
            o_ref[:, h * NOPE : (h + 1) * NOPE] = d_ref[:, h * hd : h * hd + NOPE].astype(BF16)
            rot = _rope_t(d_ref[:, h * hd + NOPE : (h + 1) * hd], c, sa, sb)
            o_ref[:, w + h * HEAD_PAD : w + (h + 1) * HEAD_PAD] = rot.astype(BF16)

    return pl.pallas_call(
        body,
        name="dq_unrope",
        out_shape=jax.ShapeDtypeStruct((s, 2 * w), BF16),
        grid=(s // ts,),
        in_specs=[_rows(ts, n_heads * hd)] + [_rows(ts, LANES)] * 3,
        out_specs=_rows(ts, 2 * w),
        compiler_params=_cp("parallel"),
    )(dq, *tabs)


def _adamw(w, m, v, grads, name):
    r, c = w.shape
    budget_rows = max(SUBLANES, (VMEM_LIMIT // 3) // (4 * c * 2 * (7 + len(grads))))
    tr = _tile(r, budget_rows, SUBLANES)
    ng = len(grads)
    c1 = 1.0 - ADAM_B1**ADAM_STEP
    c2 = 1.0 - ADAM_B2**ADAM_STEP

    def body(*refs):
        w_ref, m_ref, v_ref = refs[:3]
        g_ref, d_ref, nm_ref, nv_ref = refs[3 + ng :]
        g = refs[3][...]
        for extra in refs[4 : 3 + ng]:
            g = g + extra[...]
        mn = ADAM_B1 * m_ref[...] + (1.0 - ADAM_B1) * g
        vn = ADAM_B2 * v_ref[...] + (1.0 - ADAM_B2) * (g * g)
        g_ref[...] = g
        nm_ref[...] = mn
        nv_ref[...] = vn
        d_ref[...] = -ADAM_LR * ((mn / c1) / (jnp.sqrt(vn / c2) + ADAM_EPS) + ADAM_WD * w_ref[...])

    blk = pl.BlockSpec((tr, c), lambda i: (i, 0))
    out = jax.ShapeDtypeStruct((r, c), F32)
    return pl.pallas_call(
        body,
        name=name,
        out_shape=(out, out, out, out),
        grid=(r // tr,),
        in_specs=[blk] * (3 + ng),
        out_specs=[blk] * 4,
        compiler_params=_cp("parallel"),
    )(w, m, v, *grads)


def _ada_grad(ca_t, dm):
    d = ca_t.shape[0]
    nc = dm.shape[1]
    tn = _tile(nc, 512, LANES)

    def body(a_ref, b_ref, o_ref):
        o_ref[...] = jnp.dot(a_ref[...].astype(BF16), b_ref[...].astype(BF16), preferred_element_type=F32)

    return pl.pallas_call(
        body,
        name="ada_grad",
        out_shape=jax.ShapeDtypeStruct((d, nc), F32),
        grid=(nc // tn,),
        in_specs=[pl.BlockSpec((d, LANES), lambda j: (0, 0)), pl.BlockSpec((LANES, tn), lambda j: (0, j))],
        out_specs=pl.BlockSpec((d, tn), lambda j: (0, j)),
        compiler_params=_cp("parallel"),
    )(ca_t, dm)


def _sum_devices(g):
    n = g.shape[1]

    def body(g_ref, o_ref):
        acc = g_ref[0:SUBLANES, :]
        for dvc in range(1, N_DEV):
            acc = acc + g_ref[dvc * SUBLANES : (dvc + 1) * SUBLANES, :]
        o_ref[...] = acc

    return pl.pallas_call(
        body,
        name="sum_devices",
        out_shape=jax.ShapeDtypeStruct((SUBLANES, n), F32),
        in_specs=[pl.BlockSpec(memory_space=pltpu.VMEM)],
        out_specs=pl.BlockSpec(memory_space=pltpu.VMEM),
        compiler_params=pltpu.CompilerParams(vmem_limit_bytes=VMEM_LIMIT),
    )(g)


def _sum_chips(land, name):
    _, r, c = land.shape
    tr = _tile(r, max(SUBLANES * 2, (VMEM_LIMIT // 4) // (c * 2 * (2 * N_CHIP + 4 * 2))), SUBLANES * 2)

    def body(l_ref, o_ref):
        acc = l_ref[0].astype(F32)
        for k in range(1, N_CHIP):
            acc = acc + l_ref[k].astype(F32)
        o_ref[...] = acc

    return pl.pallas_call(
        body,
        name=name,
        out_shape=jax.ShapeDtypeStruct((r, c), F32),
        grid=(r // tr,),
        in_specs=[pl.BlockSpec((N_CHIP, tr, c), lambda i: (0, i, 0))],
        out_specs=pl.BlockSpec((tr, c), lambda i: (i, 0)),
        compiler_params=_cp("parallel"),
    )(land)


def _mesh_pos():
    return lax.axis_index("x"), lax.axis_index("y"), lax.axis_index("c")


def _other_chips(x, y):
    return [(1 - x, y), (x, 1 - y), (1 - x, 1 - y)]


def _all_gather8(x_shard, name):
    m_per, n = x_shard.shape

    def body(x_ref, out_ref, send_sems, recv_sems, local_sem):
        x, y, c = _mesh_pos()
        me, sibling = (x, y, c), (x, y, 1 - c)
        chips = _other_chips(x, y)

        def rows(px, py, pc):
            return out_ref.at[pl.ds((4 * px + 2 * py + pc) * m_per, m_per), :]

        def copy(k, block, to, src=None):
            return pltpu.make_async_remote_copy(
                src_ref=rows(*block) if src is None else src,
                dst_ref=rows(*block),
                send_sem=send_sems.at[k],
                recv_sem=recv_sems.at[k],
                device_id=to,
                device_id_type=MESH,
            )

        mine = pltpu.make_async_copy(x_ref, rows(*me), local_sem)
        mine.start()
        first = [copy(0, me, sibling, src=x_ref)]
        first += [copy(1 + j, me, (*chip, c), src=x_ref) for j, chip in enumerate(chips)]
        for cp in first:
            cp.start()
        passed = [copy(4 + j, (*chip, c), sibling) for j, chip in enumerate(chips)]
        for j, chip in enumerate(chips):
            copy(1 + j, (*chip, c), me).wait_recv()
            passed[j].start()
        copy(0, sibling, me).wait_recv()
        for j, chip in enumerate(chips):
            copy(4 + j, (*chip, 1 - c), me).wait_recv()
        for cp in first + passed:
            cp.wait_send()
        mine.wait()

    return pl.pallas_call(
        body,
        name=name,
        out_shape=jax.ShapeDtypeStruct((N_DEV * m_per, n), x_shard.dtype),
        in_specs=[pl.BlockSpec(memory_space=pltpu.VMEM)],
        out_specs=pl.BlockSpec(memory_space=pltpu.VMEM),
        scratch_shapes=[pltpu.SemaphoreType.DMA((7,)), pltpu.SemaphoreType.DMA((7,)), pltpu.SemaphoreType.DMA],
        compiler_params=pltpu.CompilerParams(vmem_limit_bytes=VMEM_LIMIT),
    )(x_shard)


def _chip_exchange(arrs, scatter, name):
    nt = len(arrs)

    def body(*refs):
        ins, outs = refs[:nt], refs[nt : 2 * nt]
        send_sems, recv_sems, local_sems = refs[2 * nt :]
        x, y, c = _mesh_pos()
        me = 2 * x + y
        chips = _other_chips(x, y)
        started = []
        for t in range(nt):
            src_me = ins[t].at[me] if scatter else ins[t]
            loc = pltpu.make_async_copy(src_me, outs[t].at[me], local_sems.at[t])
            loc.start()
            started.append(loc)
        sends = []
        for t in range(nt):
            for r, (px, py) in enumerate(chips):
                cp = pltpu.make_async_remote_copy(
                    src_ref=ins[t].at[2 * px + py] if scatter else ins[t],
                    dst_ref=outs[t].at[me],
                    send_sem=send_sems.at[3 * t + r],
                    recv_sem=recv_sems.at[3 * t + r],
                    device_id=(px, py, c),
                    device_id_type=MESH,
                )
                cp.start()
                sends.append(cp)
        for t in range(nt):
            for r, (px, py) in enumerate(chips):
                pltpu.make_async_remote_copy(
                    src_ref=ins[t].at[me] if scatter else ins[t],
                    dst_ref=outs[t].at[2 * px + py],
                    send_sem=send_sems.at[3 * t + r],
                    recv_sem=recv_sems.at[3 * t + r],
                    device_id=(px, py, c),
                    device_id_type=MESH,
                ).wait_recv()
        for cp in sends:
            cp.wait_send()
        for loc in started:
            loc.wait()

    def out_of(a):
        return jax.ShapeDtypeStruct(a.shape if scatter else (N_CHIP, *a.shape), a.dtype)

    return pl.pallas_call(
        body,
        name=name,
        out_shape=tuple(out_of(a) for a in arrs),
        in_specs=[pl.BlockSpec(memory_space=pl.ANY)] * nt,
        out_specs=[pl.BlockSpec(memory_space=pl.ANY)] * nt,
        scratch_shapes=[
            pltpu.SemaphoreType.DMA((3 * nt,)),
            pltpu.SemaphoreType.DMA((3 * nt,)),
            pltpu.SemaphoreType.DMA((nt,)),
        ],
    )(*arrs)


def _sibling_swap(arrs, name):
    nt = len(arrs)

    def body(*refs):
        ins, outs = refs[:nt], refs[nt : 2 * nt]
        send_sems, recv_sems = refs[2 * nt :]
        x, y, c = _mesh_pos()
        cps = [
            pltpu.make_async_remote_copy(
                src_ref=ins[t],
                dst_ref=outs[t],
                send_sem=send_sems.at[t],
                recv_sem=recv_sems.at[t],
                device_id=(x, y, 1 - c),
                device_id_type=MESH,
            )
            for t in range(nt)
        ]
        for cp in cps:
            cp.start()
        for cp in cps:
            cp.wait_recv()
        for cp in cps:
            cp.wait_send()

    return pl.pallas_call(
        body,
        name=name,
        out_shape=tuple(jax.ShapeDtypeStruct(a.shape, a.dtype) for a in arrs),
        in_specs=[pl.BlockSpec(memory_space=pl.ANY)] * nt,
        out_specs=[pl.BlockSpec(memory_space=pl.ANY)] * nt,
        scratch_shapes=[pltpu.SemaphoreType.DMA((nt,)), pltpu.SemaphoreType.DMA((nt,))],
    )(*arrs)


def _cols_from_shards(g):
    _, k, n = g.shape
    return jnp.transpose(g, (1, 0, 2)).reshape(k, N_CHIP * n)


def _cols_to_shards(a):
    k, n4 = a.shape
    return jnp.transpose(a.reshape(k, N_CHIP, n4 // N_CHIP), (1, 0, 2))


def _pad_to(vec, mult):
    n = vec.shape[0]
    return jnp.pad(vec, (0, (-n) % mult))


def kernel(x, c, positions, w_ada, b_ada, g_pre_mix, g_post_mix, w_in, g_q, w_uq, g_kv, w_ukv, conv_w_mix, conv_b_mix, w_o, g_pre_ffn, g_post_ffn, w_up, conv_w_ffn, conv_b_ffn, w_down, loss_target, m_w_ada, m_b_ada, m_g_pre_mix, m_g_post_mix, m_w_in, m_g_q, m_w_uq, m_g_kv, m_w_ukv, m_conv_w_mix, m_conv_b_mix, m_w_o, m_g_pre_ffn, m_g_post_ffn, m_w_up, m_conv_w_ffn, m_conv_b_ffn, m_w_down, v_w_ada, v_b_ada, v_g_pre_mix, v_g_post_mix, v_w_in, v_g_q, v_w_uq, v_g_kv, v_w_ukv, v_conv_w_mix, v_conv_b_mix, v_w_o, v_g_pre_ffn, v_g_post_ffn, v_w_up, v_conv_w_ffn, v_conv_b_ffn, v_w_down):
    xi, yi, ci = _mesh_pos()
    chip = 2 * xi + yi
    dev = 4 * xi + 2 * yi + ci

    s, d = x.shape[1], x.shape[2]
    ql, kl = g_q.shape[1], g_kv.shape[1]
    cwid = conv_b_mix.shape[1]
    f2 = conv_b_ffn.shape[1]
    hh = (w_uq.shape[2] * N_CHIP) // (NOPE + ROPE)
    w_att = hh * LANES
    nc_ada = w_ada.shape[2]
    lat = ql + kl + ROPE
    tc_mix = _tile(cwid, 512, LANES)
    lb = -(-(ql + kl + HEAD_PAD) // tc_mix) * tc_mix
    np_cols = lb + 3 * cwid
    assert cwid == hh * VDIM and w_att % tc_mix == 0

    x0 = x.reshape(s, d)
    tgt = loss_target.reshape(s, d)

    cwm_n, cwf_n = CONV_K * cwid // N_CHIP, CONV_K * f2 // N_CHIP
    pack_a = _pad_to(jnp.concatenate([c.reshape(-1), conv_w_mix.reshape(-1), conv_w_ffn.reshape(-1)]), SUBLANES * LANES)
    rows_a = _all_gather8(pack_a.reshape(SUBLANES, -1), "ag8_inputs").reshape(N_DEV, -1)
    c_all = rows_a[:, :d]
    south = rows_a[0::2]
    cw_mix = jnp.concatenate([south[j, d : d + cwm_n].reshape(CONV_K, -1) for j in range(N_CHIP)], axis=1)
    cw_ffn = jnp.concatenate([south[j, d + cwm_n : d + cwm_n + cwf_n].reshape(CONV_K, -1) for j in range(N_CHIP)], axis=1)

    b_cols = lax.dynamic_slice(b_ada, (0, chip * nc_ada), (1, nc_ada))
    mod_part, c_act = _ada_fwd(c_all, w_ada[0], b_cols)
    mod_rows = _all_gather8(mod_part, "ag8_mod")
    mod = jnp.concatenate(
        [lax.dynamic_slice_in_dim(mod_rows, 2 * N_DEV * j + dev, 1, axis=0) for j in range(N_CHIP)], axis=1
    )
    sh_m, sc_m, gt_m, sh_f, sc_f, gt_f = [mod[:, k * d : (k + 1) * d] for k in range(N_MOD)]

    shards = [w_in[0], w_uq[0], w_ukv[0], w_o[0], w_up[0], w_down[0]]
    g_in, g_uq, g_ukv, g_o, g_up, g_down = _chip_exchange([a.astype(BF16) for a in shards], False, "ag_weights")
    full_in = _cols_from_shards(g_in)
    w_in_p = jnp.concatenate([full_in[:, :lat], jnp.zeros((d, lb - lat), BF16), full_in[:, lat:]], axis=1)
    full_uq = _cols_from_shards(g_uq).reshape(ql, hh, NOPE + ROPE)
    w_uq_p = jnp.concatenate(
        [
            full_uq[:, :, :NOPE].reshape(ql, w_att),
            jnp.pad(full_uq[:, :, NOPE:], ((0, 0), (0, 0), (0, HEAD_PAD - ROPE))).reshape(ql, w_att),
        ],
        axis=1,
    )
    full_ukv = _cols_from_shards(g_ukv).reshape(kl, hh, NOPE + VDIM)
    w_ukv_p = jnp.concatenate([full_ukv[:, :, :NOPE].reshape(kl, w_att), full_ukv[:, :, NOPE:].reshape(kl, w_att)], axis=1)
    w_o_f = g_o.reshape(-1, d)
    w_up_f = _cols_from_shards(g_up)
    w_down_f = g_down.reshape(-1, d)

    inv_freq = 1.0 / (ROPE_THETA ** (jnp.arange(0, ROPE, 2, dtype=F32) / ROPE))
    invf = jnp.concatenate([inv_freq, inv_freq, jnp.zeros((LANES - ROPE,), F32)]).reshape(1, LANES)
    tabs = _rope_tables(positions.astype(F32).reshape(s, 1), invf)

    g_pre_mix2, g_post_mix2 = g_pre_mix, g_post_mix
    h1 = _pre_fwd(x0, g_pre_mix2, sc_m, sh_m)
    proj = _matmul(h1, w_in_p, out_dtype=F32, tm=1024, tn=768, tk=2048, name="mm_proj")
    qn, kvn, kr = _latent_fwd(proj, g_q, g_kv, tabs, lb)
    q_f = _matmul(qn, w_uq_p, out_dtype=F32, tm=1024, tn=1024, tk=2048, name="mm_q")
    kv_p = _matmul(kvn, w_ukv_p, out_dtype=BF16, tm=1024, tn=1024, tk=2048, name="mm_kv")
    q_c, k_c = _head_cat(q_f, kv_p, kr, tabs, hh)
    cat, lse2 = _attn_fwd(q_c, k_c, kv_p, hh, w_att + cwid)
    cb_mix = conv_b_mix
    cat = _mixer_fwd(cat, proj, cw_mix, cb_mix, lb, w_att)
    mix = _matmul(cat, w_o_f, out_dtype=F32, tm=1024, tn=1024, tk=2048, name="mm_mix")

    x1, h2 = _mid_fwd(x0, mix, g_post_mix2, gt_m, g_pre_ffn, sc_f, sh_f)
    up = _matmul(h2, w_up_f, out_dtype=F32, tm=1024, tn=1408, tk=2048, name="mm_up")
    act = _ffn_act_fwd(up, cw_ffn, conv_b_ffn)
    y = _matmul(act, w_down_f, out_dtype=F32, tm=1024, tn=1024, tk=1408, name="mm_down")
    dx2, dy, s_fin = _final(x1, y, tgt, g_post_ffn, gt_f)

    dw_down = _matmul(act, dy, ta=True, out_dtype=BF16, tm=1408, tn=1024, tk=1024, name="mm_dw_down")
    dact = _matmul(dy, w_down_f, tb=True, out_dtype=F32, tm=1024, tn=1408, tk=2048, name="mm_dact")
    dup_a, dup_g, s_fa, s_fg = _ffn_act_bwd(dact, up, cw_ffn, conv_b_ffn)
    dup = jnp.concatenate([dup_a, dup_g], axis=1)
    s_ffn = jnp.concatenate([s_fa, s_fg], axis=1)
    dw_up = _matmul(h2, dup, ta=True, out_dtype=BF16, tm=1024, tn=1408, tk=1024, name="mm_dw_up")
    dh2 = _matmul(dup, w_up_f, tb=True, out_dtype=F32, tm=1024, tn=1024, tk=1408, name="mm_dh2")
    dx1, dmix, s_mid = _mid_bwd(dh2, dx2, x1, mix, g_pre_ffn, sc_f, g_post_mix2, gt_m)

    dw_o = _matmul(cat, dmix, ta=True, out_dtype=BF16, tm=1024, tn=1024, tk=1024, name="mm_dw_o")
    dcat = _matmul(dmix, w_o_f, tb=True, out_dtype=F32, tm=1024, tn=1024, tk=2048, name="mm_dcat")
    dp_b, dp_c, dp_i, s_mix = _mixer_bwd(dcat, proj, cw_mix, cb_mix, lb, w_att)
    dob, stats = _attn_bwd_prep(cat, dcat, lse2, hh)
    dq_raw, dkv_k, dkv_v, dkr_h = _attn_bwd(q_c, k_c, kv_p, dob, stats, hh)
    dkv_p = jnp.concatenate([dkv_k, dkv_v], axis=1)
    dq_p = _dq_unrope(dq_raw, tabs, hh)
    dw_uq_p = _matmul(qn, dq_p, ta=True, out_dtype=BF16, tm=1024, tn=1024, tk=1024, name="mm_dw_uq")
    dqn = _matmul(dq_p, w_uq_p, tb=True, out_dtype=F32, tm=1024, tn=1024, tk=2048, name="mm_dqn")
    dw_ukv_p = _matmul(kvn, dkv_p, ta=True, out_dtype=BF16, tm=1024, tn=1024, tk=1024, name="mm_dw_ukv")
    dkvn = _matmul(dkv_p, w_ukv_p, tb=True, out_dtype=F32, tm=1024, tn=1024, tk=2048, name="mm_dkvn")
    dp_lat, s_lat = _latent_bwd(proj, dqn, dkvn, dkr_h, g_q, g_kv, tabs, lb)
    dproj = jnp.concatenate([dp_lat, dp_b, dp_c, dp_i], axis=1)
    dw_in_p = _matmul(h1, dproj, ta=True, out_dtype=BF16, tm=1024, tn=1536, tk=1024, name="mm_dw_in")
    dh1 = _matmul(dproj, w_in_p, tb=True, out_dtype=F32, tm=1024, tn=1024, tk=1536, name="mm_dh1")
    grad_x, s_first = _first_bwd(dh1, dx1, x0, g_pre_mix2, sc_m)

    dw_in_f = jnp.concatenate([dw_in_p[:, :lat], dw_in_p[:, lb:]], axis=1)
    uq3 = dw_uq_p.reshape(ql, 2, hh, LANES)
    dw_uq_f = jnp.concatenate([uq3[:, 0], uq3[:, 1, :, :ROPE]], axis=2).reshape(ql, hh * (NOPE + ROPE))
    ukv3 = dw_ukv_p.reshape(kl, 2, hh, LANES)
    dw_ukv_f = jnp.concatenate([ukv3[:, 0], ukv3[:, 1]], axis=2).reshape(kl, hh * (NOPE + VDIM))
    to_send = [
        _cols_to_shards(dw_in_f),
        _cols_to_shards(dw_uq_f),
        _cols_to_shards(dw_ukv_f),
        dw_o.reshape(N_CHIP, -1, d),
        _cols_to_shards(dw_up),
        dw_down.reshape(N_CHIP, -1, d),
    ]
    landed = _chip_exchange(to_send, True, "rs_grads")
    names = ["w_in", "w_uq", "w_ukv", "w_o", "w_up", "w_down"]
    part = [_sum_chips(l, "sum_chips_" + n) for l, n in zip(landed, names)]
    other = _sibling_swap(part, "sibling_swap")

    dmod = jnp.concatenate([s_first[0:1], s_first[1:2], s_mid[3:4], s_mid[0:1], s_mid[1:2], s_fin[0:1]], axis=1)
    small = [
        dmod,
        s_first[2:3],
        s_mid[4:5],
        s_lat[0:1, :ql],
        s_lat[0:1, ql : ql + kl],
        s_mix[3:4],
        s_mid[2:3],
        s_fin[1:2],
        s_ffn[3:4],
        s_mix[0:3].reshape(1, -1),
        s_ffn[0:3].reshape(1, -1),
        s_fin[3:4, :LANES],
    ]
    sizes = [a.shape[1] for a in small]
    offs = [0]
    for n in sizes:
        offs.append(offs[-1] + n)
    pack_g = _pad_to(jnp.concatenate(small, axis=1).reshape(-1), SUBLANES * LANES * SUBLANES).reshape(SUBLANES, -1)
    gathered = _all_gather8(pack_g, "ag8_small_grads")
    tot = _sum_devices(gathered).reshape(-1)
    part_of = lambda k: tot[offs[k] : offs[k + 1]]
    dmod_all = gathered.reshape(N_DEV, -1)[:, : N_MOD * d]
    loss = part_of(11)[0]

    g_b_ada = part_of(0).reshape(1, -1)
    g_vecs = [part_of(k).reshape(1, -1) for k in range(1, 9)]
    g_cw_mix = lax.dynamic_slice(part_of(9).reshape(CONV_K, cwid), (0, chip * (cwid // N_CHIP)), (CONV_K, cwid // N_CHIP))
    g_cw_ffn = lax.dynamic_slice(part_of(10).reshape(CONV_K, f2), (0, chip * (f2 // N_CHIP)), (CONV_K, f2 // N_CHIP))

    dm_cols = lax.dynamic_slice(dmod_all, (0, chip * nc_ada), (N_DEV, nc_ada))
    g_w_ada = _ada_grad(
        jnp.pad(c_act.T, ((0, 0), (0, LANES - N_DEV))), jnp.pad(dm_cols, ((0, LANES - N_DEV), (0, 0)))
    )

    big_w = [w_in, w_uq, w_ukv, w_o, w_up, w_down]
    big_m = [m_w_in, m_w_uq, m_w_ukv, m_w_o, m_w_up, m_w_down]
    big_v = [v_w_in, v_w_uq, v_w_ukv, v_w_o, v_w_up, v_w_down]
    big = {}
    for n, w_, m_, v_, p_, o_ in zip(names, big_w, big_m, big_v, part, other):
        big[n] = [a[None] for a in _adamw(w_[0], m_[0], v_[0], [p_, o_], "adamw_" + n)]
    big["w_ada"] = [a[None] for a in _adamw(w_ada[0], m_w_ada[0], v_w_ada[0], [g_w_ada], "adamw_w_ada")]

    sm_names = ["b_ada", "g_pre_mix", "g_post_mix", "g_q", "g_kv", "conv_b_mix", "g_pre_ffn", "g_post_ffn", "conv_b_ffn",
                "conv_w_mix", "conv_w_ffn"]
    sm_w = [b_ada, g_pre_mix, g_post_mix, g_q, g_kv, conv_b_mix, g_pre_ffn, g_post_ffn, conv_b_ffn, conv_w_mix, conv_w_ffn]
    sm_m = [m_b_ada, m_g_pre_mix, m_g_post_mix, m_g_q, m_g_kv, m_conv_b_mix, m_g_pre_ffn, m_g_post_ffn, m_conv_b_ffn,
            m_conv_w_mix, m_conv_w_ffn]
    sm_v = [v_b_ada, v_g_pre_mix, v_g_post_mix, v_g_q, v_g_kv, v_conv_b_mix, v_g_pre_ffn, v_g_post_ffn, v_conv_b_ffn,
            v_conv_w_mix, v_conv_w_ffn]
    sm_g = [g_b_ada] + g_vecs + [g_cw_mix, g_cw_ffn]
    flat = lambda arrs: jnp.concatenate([a.reshape(1, -1) for a in arrs], axis=1)
    sm_out = _adamw(flat(sm_w), flat(sm_m), flat(sm_v), [flat(sm_g)], "adamw_small")
    sm = {}
    off = 0
    for n, w_ in zip(sm_names, sm_w):
        sm[n] = [o[:, off : off + w_.size].reshape(w_.shape) for o in sm_out]
        off += w_.size

    order = ["w_ada", "b_ada", "g_pre_mix", "g_post_mix", "w_in", "g_q", "w_uq", "g_kv", "w_ukv", "conv_w_mix", "conv_b_mix",
             "w_o", "g_pre_ffn", "g_post_ffn", "w_up", "conv_w_ffn", "conv_b_ffn", "w_down"]
    res = {**big, **sm}
    outs = [loss, grad_x.reshape(x.shape)]
    for k in range(4):
        outs += [res[n][k] for n in order]
    return tuple(outs)
```

```python
import math

import jax
import jax.numpy as jnp
from jax import lax
from jax.experimental import pallas as pl
from jax.experimental.pallas import tpu as pltpu

F32 = jnp.float32
BF16 = jnp.bfloat16
MESH = pl.DeviceIdType.MESH

N_DEV = 8
N_CHIP = 4
LANES = 128
SUBLANES = 8
VMEM_LIMIT = 56 * 2**20

NOPE = 128
ROPE = 64
VDIM = 128
HEAD_PAD = 128
ROPE_THETA = 10000.0
RMS_EPS = 1e-6
N_MOD = 6
CONV_K = 3
ATT_BLOCK = 512
NEG = -1e30

ADAM_LR = 0.001
ADAM_B1 = 0.9
ADAM_B2 = 0.999
ADAM_EPS = 1e-08
ADAM_WD = 0.01
ADAM_STEP = 10


def _tile(n, pref, align):
    if n <= pref:
        return n
    t = (pref // align) * align
    while t >= align:
        if n % t == 0:
            return t
        t -= align
    return n


def _cp(*sem):
    return pltpu.CompilerParams(dimension_semantics=sem, vmem_limit_bytes=VMEM_LIMIT)


def _rsq(x):
    return lax.rsqrt(jnp.mean(x * x, axis=-1, keepdims=True) + RMS_EPS)


def _norm_bwd(dn, n, r):
    return r * (dn - n * jnp.mean(dn * n, axis=-1, keepdims=True))


def _colsum(a):
    return jnp.sum(a, axis=0, keepdims=True)


def _matmul(a, b, *, ta=False, tb=False, out_dtype, tm, tn, tk, name):
    (k_a, m) = a.shape if ta else a.shape[::-1]
    (n, k_b) = b.shape if tb else b.shape[::-1]
    assert k_a == k_b, (a.shape, b.shape, ta, tb)
    tm, tn, tk = _tile(m, tm, LANES), _tile(n, tn, LANES), _tile(k_a, tk, LANES)
    nk = k_a // tk
    a_spec = pl.BlockSpec((tk, tm), lambda i, j, k: (k, i)) if ta else pl.BlockSpec((tm, tk), lambda i, j, k: (i, k))
    b_spec = pl.BlockSpec((tn, tk), lambda i, j, k: (j, k)) if tb else pl.BlockSpec((tk, tn), lambda i, j, k: (k, j))
    dims = (((0 if ta else 1,), (1 if tb else 0,)), ((), ()))

    def body(a_ref, b_ref, o_ref, *acc):
        p = lax.dot_general(a_ref[...].astype(BF16), b_ref[...].astype(BF16), dims, preferred_element_type=F32)
        if nk == 1:
            o_ref[...] = p.astype(o_ref.dtype)
        else:
            k = pl.program_id(2)

            @pl.when(k == 0)
            def _():
                acc[0][...] = p

            @pl.when(k > 0)
            def _():
                acc[0][...] += p

            @pl.when(k == nk - 1)
            def _():
                o_ref[...] = acc[0][...].astype(o_ref.dtype)

    return pl.pallas_call(
        body,
        name=name,
        out_shape=jax.ShapeDtypeStruct((m, n), out_dtype),
        grid=(m // tm, n // tn, nk),
        in_specs=[a_spec, b_spec],
        out_specs=pl.BlockSpec((tm, tn), lambda i, j, k: (i, j)),
        scratch_shapes=[] if nk == 1 else [pltpu.VMEM((tm, tn), F32)],
        compiler_params=_cp("parallel", "parallel", "arbitrary"),
    )(a, b)


def _rope_tables(pos_col, invf):
    s = pos_col.shape[0]
    ts = _tile(s, 1024, SUBLANES)
    half = ROPE // 2

    def body(p_ref, f_ref, c_ref, sa_ref, sb_ref):
        ang = p_ref[...] * f_ref[...]
        lane = lax.broadcasted_iota(jnp.int32, ang.shape, 1)
        cs, sn = jnp.cos(ang), jnp.sin(ang)
        c_ref[...] = jnp.where(lane < ROPE, cs, 0.0)
        sa_ref[...] = jnp.where((lane >= half) & (lane < ROPE), sn, 0.0)
        sb_ref[...] = jnp.where(lane < half, -sn, 0.0)

    tab = jax.ShapeDtypeStruct((s, LANES), F32)
    return pl.pallas_call(
        body,
        name="rope_tables",
        out_shape=(tab, tab, tab),
        grid=(s // ts,),
        in_specs=[pl.BlockSpec((ts, 1), lambda i: (i, 0)), pl.BlockSpec((1, LANES), lambda i: (0, 0))],
        out_specs=[pl.BlockSpec((ts, LANES), lambda i: (i, 0))] * 3,
        compiler_params=_cp("parallel"),
    )(pos_col, invf)


def _widen(t, w):
    return t if w == LANES else jnp.tile(t, (1, w // LANES))


def _rope(x, c, sa, sb):
    w = x.shape[1]
    c, sa, sb = _widen(c, w), _widen(sa, w), _widen(sb, w)
    return x * c + pltpu.roll(x, ROPE // 2, 1) * sa + pltpu.roll(x, w - ROPE // 2, 1) * sb


def _rope_t(d, c, sa, sb):
    w = d.shape[1]
    c, sa, sb = _widen(c, w), _widen(sa, w), _widen(sb, w)
    return d * c + pltpu.roll(d * sa, w - ROPE // 2, 1) + pltpu.roll(d * sb, ROPE // 2, 1)


def _ada_fwd(c_all, w, b):
    d, nc = w.shape
    tn = _tile(nc, 512, LANES)

    def body(c_ref, w_ref, b_ref, o_ref, ca_ref):
        cv = c_ref[...]
        ca = cv * jax.nn.sigmoid(cv)
        ca_ref[...] = ca
        o_ref[...] = jnp.dot(ca.astype(BF16), w_ref[...].astype(BF16), preferred_element_type=F32) + b_ref[...]

    return pl.pallas_call(
        body,
        name="ada_fwd",
        out_shape=(jax.ShapeDtypeStruct((N_DEV, nc), F32), jax.ShapeDtypeStruct((N_DEV, d), F32)),
        grid=(nc // tn,),
        in_specs=[
            pl.BlockSpec((N_DEV, d), lambda j: (0, 0)),
            pl.BlockSpec((d, tn), lambda j: (0, j)),
            pl.BlockSpec((1, tn), lambda j: (0, j)),
        ],
        out_specs=[pl.BlockSpec((N_DEV, tn), lambda j: (0, j)), pl.BlockSpec((N_DEV, d), lambda j: (0, 0))],
        compiler_params=_cp("arbitrary"),
    )(c_all, w, b)


def _rows(ts, d):
    return pl.BlockSpec((ts, d), lambda i: (i, 0))


def _vec(d):
    return pl.BlockSpec((1, d), lambda i: (0, 0))


def _sums(d):
    return pl.BlockSpec((SUBLANES, d), lambda i: (0, 0))


def _acc_rows(ref, i, rows):
    @pl.when(i == 0)
    def _():
        ref[...] = jnp.zeros(ref.shape, ref.dtype)

    for k, r in enumerate(rows):
        ref[k : k + 1, :] += r


def _pre_fwd(x, g, sc, sh):
    s, d = x.shape
    ts = _tile(s, 512, SUBLANES)

    def body(x_ref, g_ref, sc_ref, sh_ref, h_ref):
        xv = x_ref[...]
        h_ref[...] = (((xv * _rsq(xv)) * g_ref[...]) * (1.0 + sc_ref[...]) + sh_ref[...]).astype(BF16)

    return pl.pallas_call(
        body,
        name="pre_mix_fwd",
        out_shape=jax.ShapeDtypeStruct((s, d), BF16),
        grid=(s // ts,),
        in_specs=[_rows(ts, d), _vec(d), _vec(d), _vec(d)],
        out_specs=_rows(ts, d),
        compiler_params=_cp("parallel"),
    )(x, g, sc, sh)


def _mid_fwd(x0, mix, g_post, gt, g_pre, sc, sh):
    s, d = x0.shape
    ts = _tile(s, 256, SUBLANES)

    def body(x_ref, m_ref, gp_ref, gt_ref, g_ref, sc_ref, sh_ref, x1_ref, h_ref):
        mv = m_ref[...]
        x1 = x_ref[...] + gt_ref[...] * ((mv * _rsq(mv)) * gp_ref[...])
        x1_ref[...] = x1
        h_ref[...] = (((x1 * _rsq(x1)) * g_ref[...]) * (1.0 + sc_ref[...]) + sh_ref[...]).astype(BF16)

    return pl.pallas_call(
        body,
        name="mid_fwd",
        out_shape=(jax.ShapeDtypeStruct((s, d), F32), jax.ShapeDtypeStruct((s, d), BF16)),
        grid=(s // ts,),
        in_specs=[_rows(ts, d), _rows(ts, d)] + [_vec(d)] * 5,
        out_specs=[_rows(ts, d), _rows(ts, d)],
        compiler_params=_cp("parallel"),
    )(x0, mix, g_post, gt, g_pre, sc, sh)


def _final(x1, y, tgt, g_post, gt):
    s, d = x1.shape
    ts = _tile(s, 256, SUBLANES)
    ni = s // ts

    def body(x_ref, y_ref, t_ref, gp_ref, gt_ref, dx_ref, dy_ref, s_ref):
        i = pl.program_id(0)
        yv, gp, gt_v = y_ref[...], gp_ref[...], gt_ref[...]
        r = _rsq(yv)
        n = yv * r
        err = (x_ref[...] + gt_v * (n * gp)) - t_ref[...]
        dx = err * (1.0 / d)
        dx_ref[...] = dx
        dy_ref[...] = _norm_bwd(dx * (gt_v * gp), n, r).astype(BF16)
        _acc_rows(s_ref, i, [_colsum(dx * (n * gp)), _colsum(dx * gt_v * n), _colsum(err * err)])

        @pl.when(i == ni - 1)
        def _():
            tot = jnp.sum(s_ref[2:3, :], axis=1, keepdims=True) * (0.5 / d)
            s_ref[3:4, :] = jnp.broadcast_to(tot, (1, d))

    return pl.pallas_call(
        body,
        name="final_fwd_bwd",
        out_shape=(
            jax.ShapeDtypeStruct((s, d), F32),
            jax.ShapeDtypeStruct((s, d), BF16),
            jax.ShapeDtypeStruct((SUBLANES, d), F32),
        ),
        grid=(ni,),
        in_specs=[_rows(ts, d)] * 3 + [_vec(d)] * 2,
        out_specs=[_rows(ts, d), _rows(ts, d), _sums(d)],
        compiler_params=_cp("arbitrary"),
    )(x1, y, tgt, g_post, gt)


def _mid_bwd(dh2, dx2, x1, mix, g_pre, sc, g_post, gt):
    s, d = x1.shape
    ts = _tile(s, 256, SUBLANES)

    def body(dh_ref, dx2_ref, x_ref, m_ref, g_ref, sc_ref, gp_ref, gt_ref, dx1_ref, dm_ref, s_ref):
        i = pl.program_id(0)
        dh, xv, mv = dh_ref[...], x_ref[...], m_ref[...]
        g, sc_v, gp, gt_v = g_ref[...], sc_ref[...], gp_ref[...], gt_ref[...]
        r1 = _rsq(xv)
        n1 = xv * r1
        dx1 = dx2_ref[...] + _norm_bwd(dh * (g * (1.0 + sc_v)), n1, r1)
        dx1_ref[...] = dx1
        rm = _rsq(mv)
        nm = mv * rm
        dm_ref[...] = _norm_bwd(dx1 * (gt_v * gp), nm, rm).astype(BF16)
        _acc_rows(
            s_ref,
            i,
            [
                _colsum(dh),
                _colsum(dh * (n1 * g)),
                _colsum(dh * (1.0 + sc_v) * n1),
                _colsum(dx1 * (nm * gp)),
                _colsum(dx1 * gt_v * nm),
            ],
        )

    return pl.pallas_call(
        body,
        name="mid_bwd",
        out_shape=(
            jax.ShapeDtypeStruct((s, d), F32),
            jax.ShapeDtypeStruct((s, d), BF16),
            jax.ShapeDtypeStruct((SUBLANES, d), F32),
        ),
        grid=(s // ts,),
        in_specs=[_rows(ts, d)] * 4 + [_vec(d)] * 4,
        out_specs=[_rows(ts, d), _rows(ts, d), _sums(d)],
        compiler_params=_cp("arbitrary"),
    )(dh2, dx2, x1, mix, g_pre, sc, g_post, gt)


def _first_bwd(dh1, dx1, x0, g, sc):
    s, d = x0.shape
    ts = _tile(s, 256, SUBLANES)

    def body(dh_ref, dx1_ref, x_ref, g_ref, sc_ref, dx_ref, s_ref):
        i = pl.program_id(0)
        dh, xv, gv, sc_v = dh_ref[...], x_ref[...], g_ref[...], sc_ref[...]
        r = _rsq(xv)
        n = xv * r
        dx_ref[...] = dx1_ref[...] + _norm_bwd(dh * (gv * (1.0 + sc_v)), n, r)
        _acc_rows(s_ref, i, [_colsum(dh), _colsum(dh * (n * gv)), _colsum(dh * (1.0 + sc_v) * n)])

    return pl.pallas_call(
        body,
        name="first_bwd",
        out_shape=(jax.ShapeDtypeStruct((s, d), F32), jax.ShapeDtypeStruct((SUBLANES, d), F32)),
        grid=(s // ts,),
        in_specs=[_rows(ts, d)] * 3 + [_vec(d)] * 2,
        out_specs=[_rows(ts, d), _sums(d)],
        compiler_params=_cp("arbitrary"),
    )(dh1, dx1, x0, g, sc)


def _latent_fwd(proj, g_q, g_kv, tabs, lb):
    s = proj.shape[0]
    ql, kl = g_q.shape[1], g_kv.shape[1]
    ts = _tile(s, 512, SUBLANES)

    def body(p_ref, gq_ref, gk_ref, c_ref, sa_ref, sb_ref, q_ref, kv_ref, kr_ref):
        pv = p_ref[...]
        q, kv, kr = pv[:, :ql], pv[:, ql : ql + kl], pv[:, ql + kl : ql + kl + HEAD_PAD]
        q_ref[...] = ((q * _rsq(q)) * gq_ref[...]).astype(BF16)
        kv_ref[...] = ((kv * _rsq(kv)) * gk_ref[...]).astype(BF16)
        kr_ref[...] = _rope(kr, c_ref[...], sa_ref[...], sb_ref[...]).astype(BF16)

    return pl.pallas_call(
        body,
        name="latent_fwd",
        out_shape=(
            jax.ShapeDtypeStruct((s, ql), BF16),
            jax.ShapeDtypeStruct((s, kl), BF16),
            jax.ShapeDtypeStruct((s, HEAD_PAD), BF16),
        ),
        grid=(s // ts,),
        in_specs=[_rows(ts, lb), _vec(ql), _vec(kl)] + [_rows(ts, LANES)] * 3,
        out_specs=[_rows(ts, ql), _rows(ts, kl), _rows(ts, HEAD_PAD)],
        compiler_params=_cp("parallel"),
    )(proj, g_q, g_kv, *tabs)


def _latent_bwd(proj, dqn, dkvn, dkr_h, g_q, g_kv, tabs, lb):
    s = proj.shape[0]
    ql, kl = g_q.shape[1], g_kv.shape[1]
    hw = dkr_h.shape[1]
    ts = _tile(s, 256, SUBLANES)
    pad = lb - ql - kl - HEAD_PAD

    def body(p_ref, dq_ref, dkv_ref, dkr_ref, gq_ref, gk_ref, c_ref, sa_ref, sb_ref, o_ref, s_ref):
        i = pl.program_id(0)
        pv = p_ref[...]
        q, kv = pv[:, :ql], pv[:, ql : ql + kl]
        dqn_v, dkvn_v = dq_ref[...], dkv_ref[...]
        rq = _rsq(q)
        nq = q * rq
        rk = _rsq(kv)
        nk = kv * rk
        dkr = dkr_ref[:, :HEAD_PAD]
        for h in range(1, hw // HEAD_PAD):
            dkr = dkr + dkr_ref[:, h * HEAD_PAD : (h + 1) * HEAD_PAD]
        parts = [
            _norm_bwd(dqn_v * gq_ref[...], nq, rq).astype(BF16),
            _norm_bwd(dkvn_v * gk_ref[...], nk, rk).astype(BF16),
            _rope_t(dkr, c_ref[...], sa_ref[...], sb_ref[...]).astype(BF16),
        ]
        if pad:
            parts.append(jnp.zeros((ts, pad), BF16))
        o_ref[...] = jnp.concatenate(parts, axis=1)
        row = [_colsum(dqn_v * nq), _colsum(dkvn_v * nk), jnp.zeros((1, lb - ql - kl), F32)]
        _acc_rows(s_ref, i, [jnp.concatenate(row, axis=1)])

    return pl.pallas_call(
        body,
        name="latent_bwd",
        out_shape=(jax.ShapeDtypeStruct((s, lb), BF16), jax.ShapeDtypeStruct((SUBLANES, lb), F32)),
        grid=(s // ts,),
        in_specs=[_rows(ts, lb), _rows(ts, ql), _rows(ts, kl), _rows(ts, hw)]
        + [_vec(ql), _vec(kl)]
        + [_rows(ts, LANES)] * 3,
        out_specs=[_rows(ts, lb), _sums(lb)],
        compiler_params=_cp("arbitrary"),
    )(proj, dqn, dkvn, dkr_h, g_q, g_kv, *tabs)


def _conv3(ext, w, b):
    return (pltpu.roll(ext, 2, 0) * w[0:1] + pltpu.roll(ext, 1, 0) * w[1:2]) + ext * w[2:3] + b


def _conv3_t(du, w):
    n = du.shape[0]
    return du * w[2:3] + pltpu.roll(du, n - 1, 0) * w[1:2] + pltpu.roll(du, n - 2, 0) * w[0:1]


def _halo_maps(ts, s):
    r8, last = ts // SUBLANES, s // SUBLANES - 1
    prev = lambda i: jnp.maximum(i * r8 - 1, 0)
    nxt = lambda i: jnp.minimum((i + 1) * r8, last)
    return prev, nxt


def _mixer_fwd(cat, proj, cw, cb, lb, col0):
    s = proj.shape[0]
    cwid = cw.shape[1]
    ts = _tile(s, 512, SUBLANES)
    tc = _tile(cwid, 512, LANES)
    assert lb % tc == 0 and col0 % tc == 0
    nj, ob, oc = cwid // tc, lb // tc, col0 // tc
    prev, _ = _halo_maps(ts, s)

    def body(_, gb_ref, gc_ref, ci_ref, pgc_ref, pci_ref, w_ref, b_ref, o_ref):
        keep = jnp.where(pl.program_id(1) > 0, 1.0, 0.0)
        ext = jnp.concatenate([pgc_ref[...] * pci_ref[...] * keep, gc_ref[...] * ci_ref[...]], axis=0)
        o_ref[...] = (gb_ref[...] * _conv3(ext, w_ref[...], b_ref[...])[SUBLANES:]).astype(BF16)

    def col(k):
        return pl.BlockSpec((ts, tc), lambda j, i: (i, ob + k * nj + j))

    def halo(k):
        return pl.BlockSpec((SUBLANES, tc), lambda j, i: (prev(i), ob + k * nj + j))

    return pl.pallas_call(
        body,
        name="mixer_fwd",
        out_shape=jax.ShapeDtypeStruct(cat.shape, BF16),
        grid=(nj, s // ts),
        in_specs=[pl.BlockSpec(memory_space=pl.ANY), col(0), col(1), col(2), halo(1), halo(2)]
        + [pl.BlockSpec((CONV_K, tc), lambda j, i: (0, j)), pl.BlockSpec((1, tc), lambda j, i: (0, j))],
        out_specs=pl.BlockSpec((ts, tc), lambda j, i: (i, oc + j)),
        input_output_aliases={0: 0},
        compiler_params=_cp("parallel", "arbitrary"),
    )(cat, proj, proj, proj, proj, proj, cw, cb)


def _mixer_bwd(dcat, proj, cw, cb, lb, col0):
    s = proj.shape[0]
    cwid = cw.shape[1]
    ts = _tile(s, 256, SUBLANES)
    tc = _tile(cwid, 512, LANES)
    nj, ob, oc = cwid // tc, lb // tc, col0 // tc
    ni = s // ts
    prev, nxt = _halo_maps(ts, s)

    def body(d_ref, dn_ref, gb_ref, gbn_ref, gc_ref, gcp_ref, gcn_ref, ci_ref, cip_ref, cin_ref, w_ref, b_ref,
             dgb_ref, dgc_ref, dci_ref, s_ref):
        i = pl.program_id(1)
        keep_p = jnp.where(i > 0, 1.0, 0.0)
        keep_n = jnp.where(i < ni - 1, 1.0, 0.0)
        w = w_ref[...]
        gc = jnp.concatenate([gcp_ref[...], gc_ref[...], gcn_ref[...]], axis=0)
        ci = jnp.concatenate([cip_ref[...] * keep_p, ci_ref[...], cin_ref[...]], axis=0)
        u = gc * ci
        cv = _conv3(u, w, b_ref[...])[SUBLANES:]
        dco = jnp.concatenate([d_ref[...], dn_ref[...] * keep_n], axis=0)
        gb = jnp.concatenate([gb_ref[...], gbn_ref[...]], axis=0)
        dgb_ref[...] = (dco * cv)[:ts].astype(BF16)
        dcv = dco * gb
        du = _conv3_t(dcv, w)[:ts]
        dgc_ref[...] = (du * ci_ref[...]).astype(BF16)
        dci_ref[...] = (du * gc_ref[...]).astype(BF16)
        dt = dcv[:ts]
        u1, u2 = pltpu.roll(u, 1, 0), pltpu.roll(u, 2, 0)
        lo, hi = SUBLANES, SUBLANES + ts
        _acc_rows(s_ref, i, [_colsum(dt * u2[lo:hi]), _colsum(dt * u1[lo:hi]), _colsum(dt * u[lo:hi]), _colsum(dt)])

    def col(k):
        return pl.BlockSpec((ts, tc), lambda j, i: (i, ob + k * nj + j))

    def halo(k, which):
        return pl.BlockSpec((SUBLANES, tc), lambda j, i: (which(i), ob + k * nj + j))

    out_col = [pl.BlockSpec((ts, tc), lambda j, i: (i, j))] * 3
    grad = jax.ShapeDtypeStruct((s, cwid), BF16)
    return pl.pallas_call(
        body,
        name="mixer_bwd",
        out_shape=(grad, grad, grad, jax.ShapeDtypeStruct((SUBLANES, cwid), F32)),
        grid=(nj, ni),
        in_specs=[
            pl.BlockSpec((ts, tc), lambda j, i: (i, oc + j)),
            pl.BlockSpec((SUBLANES, tc), lambda j, i: (nxt(i), oc + j)),
            col(0), halo(0, nxt),
            col(1), halo(1, prev), halo(1, nxt),
            col(2), halo(2, prev), halo(2, nxt),
            pl.BlockSpec((CONV_K, tc), lambda j, i: (0, j)),
            pl.BlockSpec((1, tc), lambda j, i: (0, j)),
        ],
        out_specs=out_col + [pl.BlockSpec((SUBLANES, tc), lambda j, i: (0, j))],
        compiler_params=_cp("parallel", "arbitrary"),
    )(dcat, dcat, proj, proj, proj, proj, proj, proj, proj, proj, cw, cb)


def _ffn_act_fwd(up, cw, cb):
    s, f2 = up.shape
    f = f2 // 2
    ts = _tile(s, 512, SUBLANES)
    tc = _tile(f, 512, LANES)
    nj = f // tc
    prev, _ = _halo_maps(ts, s)

    def body(ua_ref, ug_ref, pa_ref, pg_ref, wa_ref, wg_ref, ba_ref, bg_ref, o_ref):
        keep = jnp.where(pl.program_id(1) > 0, 1.0, 0.0)

        def conv(u_ref, p_ref, w_ref, b_ref):
            ext = jnp.concatenate([p_ref[...] * keep, u_ref[...]], axis=0)
            return _conv3(ext, w_ref[...], b_ref[...])[SUBLANES:]

        a = conv(ua_ref, pa_ref, wa_ref, ba_ref)
        g = conv(ug_ref, pg_ref, wg_ref, bg_ref)
        o_ref[...] = ((g * jax.nn.sigmoid(g)) * a).astype(BF16)

    def col(k):
        return pl.BlockSpec((ts, tc), lambda j, i: (i, k * nj + j))

    def halo(k):
        return pl.BlockSpec((SUBLANES, tc), lambda j, i: (prev(i), k * nj + j))

    def wspec(rows, k):
        return pl.BlockSpec((rows, tc), lambda j, i: (0, k * nj + j))

    return pl.pallas_call(
        body,
        name="ffn_act_fwd",
        out_shape=jax.ShapeDtypeStruct((s, f), BF16),
        grid=(nj, s // ts),
        in_specs=[col(0), col(1), halo(0), halo(1), wspec(CONV_K, 0), wspec(CONV_K, 1), wspec(1, 0), wspec(1, 1)],
        out_specs=pl.BlockSpec((ts, tc), lambda j, i: (i, j)),
        compiler_params=_cp("parallel", "arbitrary"),
    )(up, up, up, up, cw, cw, cb, cb)


def _ffn_act_bwd(dact, up, cw, cb):
    s, f2 = up.shape
    f = f2 // 2
    ts = _tile(s, 256, SUBLANES)
    tc = _tile(f, 512, LANES)
    nj, ni = f // tc, s // ts
    prev, nxt = _halo_maps(ts, s)

    def body(d_ref, dn_ref, ua_ref, uap_ref, uan_ref, ug_ref, ugp_ref, ugn_ref, wa_ref, wg_ref, ba_ref, bg_ref,
             dua_ref, dug_ref, sa_ref, sg_ref):
        i = pl.program_id(1)
        keep_p = jnp.where(i > 0, 1.0, 0.0)
        keep_n = jnp.where(i < ni - 1, 1.0, 0.0)
        wa, wg = wa_ref[...], wg_ref[...]
        exta = jnp.concatenate([uap_ref[...] * keep_p, ua_ref[...], uan_ref[...]], axis=0)
        extg = jnp.concatenate([ugp_ref[...] * keep_p, ug_ref[...], ugn_ref[...]], axis=0)
        a = _conv3(exta, wa, ba_ref[...])[SUBLANES:]
        g = _conv3(extg, wg, bg_ref[...])[SUBLANES:]
        dact_v = jnp.concatenate([d_ref[...], dn_ref[...] * keep_n], axis=0)
        sg = jax.nn.sigmoid(g)
        da = dact_v * (g * sg)
        dg = dact_v * a * (sg * (1.0 + g * (1.0 - sg)))
        lo, hi = SUBLANES, SUBLANES + ts

        def back(du, ext, w, dup_ref, s_ref):
            dup_ref[...] = _conv3_t(du, w)[:ts].astype(BF16)
            dt = du[:ts]
            e1, e2 = pltpu.roll(ext, 1, 0), pltpu.roll(ext, 2, 0)
            _acc_rows(s_ref, i, [_colsum(dt * e2[lo:hi]), _colsum(dt * e1[lo:hi]), _colsum(dt * ext[lo:hi]), _colsum(dt)])

        back(da, exta, wa, dua_ref, sa_ref)
        back(dg, extg, wg, dug_ref, sg_ref)

    def col(k):
        return pl.BlockSpec((ts, tc), lambda j, i: (i, k * nj + j))

    def halo(k, which):
        return pl.BlockSpec((SUBLANES, tc), lambda j, i: (which(i), k * nj + j))

    def wspec(rows, k):
        return pl.BlockSpec((rows, tc), lambda j, i: (0, k * nj + j))

    half = pl.BlockSpec((ts, tc), lambda j, i: (i, j))
    half_sums = pl.BlockSpec((SUBLANES, tc), lambda j, i: (0, j))
    return pl.pallas_call(
        body,
        name="ffn_act_bwd",
        out_shape=(
            jax.ShapeDtypeStruct((s, f), BF16),
            jax.ShapeDtypeStruct((s, f), BF16),
            jax.ShapeDtypeStruct((SUBLANES, f), F32),
            jax.ShapeDtypeStruct((SUBLANES, f), F32),
        ),
        grid=(nj, ni),
        in_specs=[
            pl.BlockSpec((ts, tc), lambda j, i: (i, j)),
            pl.BlockSpec((SUBLANES, tc), lambda j, i: (nxt(i), j)),
            col(0), halo(0, prev), halo(0, nxt),
            col(1), halo(1, prev), halo(1, nxt),
            wspec(CONV_K, 0), wspec(CONV_K, 1), wspec(1, 0), wspec(1, 1),
        ],
        out_specs=[half, half, half_sums, half_sums],
        compiler_params=_cp("parallel", "arbitrary"),
    )(dact, dact, up, up, up, up, up, up, cw, cw, cb, cb)


ATT_SCALE = 1.0 / math.sqrt(NOPE + ROPE)
LOG2E = math.log2(math.e)
ATT_C2 = ATT_SCALE * LOG2E
ATT_SUB = 256
STAT_SPLIT = 64
NT = (((1,), (1,)), ((), ()))
TN = (((0,), (0,)), ((), ()))


def _head_cat(q, kv, kr, tabs, n_heads):
    s, w2 = q.shape
    w = w2 // 2
    ts = _tile(s, 512, SUBLANES)
    hd = NOPE + HEAD_PAD

    def body(q_ref, kv_ref, kr_ref, c_ref, sa_ref, sb_ref, qc_ref, kc_ref):
        qv = q_ref[...]
        qr = _rope(qv[:, w:], c_ref[...], sa_ref[...], sb_ref[...]).astype(BF16)
        krv = kr_ref[...]
        for h in range(n_heads):
            qc_ref[:, h * hd : h * hd + NOPE] = qv[:, h * NOPE : (h + 1) * NOPE].astype(BF16)
            qc_ref[:, h * hd + NOPE : (h + 1) * hd] = qr[:, h * HEAD_PAD : (h + 1) * HEAD_PAD]
            kc_ref[:, h * hd : h * hd + NOPE] = kv_ref[:, h * NOPE : (h + 1) * NOPE]
            kc_ref[:, h * hd + NOPE : (h + 1) * hd] = krv

    out = jax.ShapeDtypeStruct((s, n_heads * hd), BF16)
    return pl.pallas_call(
        body,
        name="head_cat",
        out_shape=(out, out),
        grid=(s // ts,),
        in_specs=[_rows(ts, w2), _rows(ts, w), _rows(ts, HEAD_PAD)] + [_rows(ts, LANES)] * 3,
        out_specs=[_rows(ts, n_heads * hd)] * 2,
        compiler_params=_cp("parallel"),
    )(q, kv, kr, *tabs)


def _attn_fwd(qc, kc, kv, n_heads, cat_cols):
    s = qc.shape[0]
    t = _tile(s, ATT_BLOCK, LANES)
    sub = _tile(t, ATT_SUB, LANES)
    hh = n_heads
    hd = NOPE + HEAD_PAD

    def body(q_ref, k_ref, v_ref, o_ref, lse_ref, m_s, l_s, acc_s):
        i = pl.program_id(1)
        m_s[...] = jnp.full(m_s.shape, NEG, F32)
        l_s[...] = jnp.zeros(l_s.shape, F32)
        acc_s[...] = jnp.zeros(acc_s.shape, F32)

        def chunk(k0, diag):
            m_all, l_all, acc_all = m_s[...], l_s[...], acc_s[...]
            new_m, new_l, new_acc = [], [], []
            for r0 in range(0, t, sub):
                ncol = r0 + sub if diag else t
                kk = k_ref[pl.ds(k0, ncol), :]
                sc = lax.dot_general(q_ref[pl.ds(r0, sub), :], kk, NT, preferred_element_type=F32)
                if diag:
                    row = lax.broadcasted_iota(jnp.int32, sc.shape, 0) + r0
                    col = lax.broadcasted_iota(jnp.int32, sc.shape, 1)
                    sc = jnp.where(col <= row, sc, NEG)
                m_prev = m_all[r0 : r0 + sub]
                m_new = jnp.maximum(m_prev, jnp.max(sc, axis=1, keepdims=True))
                alpha = jnp.exp2((m_prev - m_new) * ATT_C2)
                p = jnp.exp2((sc - m_new) * ATT_C2)
                pv = jnp.dot(p.astype(BF16), v_ref[pl.ds(k0, ncol), :], preferred_element_type=F32)
                new_m.append(m_new)
                new_l.append(alpha * l_all[r0 : r0 + sub] + jnp.sum(p, axis=1, keepdims=True))
                new_acc.append(alpha * acc_all[r0 : r0 + sub] + pv)
            m_s[...] = jnp.concatenate(new_m, axis=0)
            l_s[...] = jnp.concatenate(new_l, axis=0)
            acc_s[...] = jnp.concatenate(new_acc, axis=0)

        def loop_body(k, carry):
            chunk(pl.multiple_of(k * t, t), False)
            return carry

        lax.fori_loop(0, i, loop_body, 0)
        chunk(pl.multiple_of(i * t, t), True)
        l = l_s[...]
        o_ref[...] = (acc_s[...] / l).astype(BF16)
        lse_ref[...] = jnp.broadcast_to(m_s[...] * ATT_C2 + jnp.log(l) * LOG2E, lse_ref.shape)

    return pl.pallas_call(
        body,
        name="attn_fwd",
        out_shape=(jax.ShapeDtypeStruct((s, cat_cols), BF16), jax.ShapeDtypeStruct((s, hh * LANES), F32)),
        grid=(hh, s // t),
        in_specs=[
            pl.BlockSpec((t, hd), lambda h, i: (i, h)),
            pl.BlockSpec((s, hd), lambda h, i: (0, h)),
            pl.BlockSpec((s, VDIM), lambda h, i: (0, hh + h)),
        ],
        out_specs=[pl.BlockSpec((t, VDIM), lambda h, i: (i, h)), pl.BlockSpec((t, LANES), lambda h, i: (i, h))],
        scratch_shapes=[pltpu.VMEM((t, 1), F32), pltpu.VMEM((t, 1), F32), pltpu.VMEM((t, VDIM), F32)],
        compiler_params=_cp("parallel", "parallel"),
    )(qc, kc, kv)


def _attn_bwd_prep(cat, dcat, lse2, n_heads):
    s, w = lse2.shape
    ts = _tile(s, 512, SUBLANES)

    def body(o_ref, do_ref, lse_ref, dob_ref, st_ref):
        do = do_ref[...]
        dob_ref[...] = do.astype(BF16)
        prod = do * o_ref[...].astype(F32)
        lane = lax.broadcasted_iota(jnp.int32, (ts, LANES), 1)
        for h in range(n_heads):
            cols = slice(h * LANES, (h + 1) * LANES)
            dsum = jnp.sum(prod[:, cols], axis=1, keepdims=True)
            st_ref[:, cols] = jnp.where(lane < STAT_SPLIT, lse_ref[:, cols], dsum)

    return pl.pallas_call(
        body,
        name="attn_bwd_prep",
        out_shape=(jax.ShapeDtypeStruct((s, w), BF16), jax.ShapeDtypeStruct((s, w), F32)),
        grid=(s // ts,),
        in_specs=[_rows(ts, w)] * 3,
        out_specs=[_rows(ts, w)] * 2,
        compiler_params=_cp("parallel"),
    )(cat, dcat, lse2)


def _attn_bwd(qc, kc, kv, dob, stats, n_heads):
    s = qc.shape[0]
    t = _tile(s, ATT_BLOCK, LANES)
    sub = _tile(t, ATT_SUB, LANES)
    nb = s // t
    hh = n_heads
    hd = NOPE + HEAD_PAD
    w = hh * LANES

    def body(q_ref, k_ref, v_ref, do_ref, st_ref, dq_ref, dkn_ref, dv_ref, dkr_ref, dk_s, dv_s):
        j = pl.program_id(1)

        @pl.when(j == 0)
        def _():
            dq_ref[...] = jnp.zeros(dq_ref.shape, F32)

        dk_s[...] = jnp.zeros(dk_s.shape, F32)
        dv_s[...] = jnp.zeros(dv_s.shape, F32)
        kk, vv = k_ref[...], v_ref[...]

        def pair(i0, diag):
            for r0 in range(0, t, sub):
                rows = pl.ds(i0 + r0, sub)
                qq, do, st = q_ref[rows, :], do_ref[rows, :], st_ref[rows, :]
                sc = lax.dot_general(qq, kk, NT, preferred_element_type=F32)
                if diag:
                    row = lax.broadcasted_iota(jnp.int32, sc.shape, 0) + r0
                    col = lax.broadcasted_iota(jnp.int32, sc.shape, 1)
                    sc = jnp.where(col <= row, sc, NEG)
                p = jnp.exp2(sc * ATT_C2 - st[:, 0:1])
                dv_s[...] += lax.dot_general(p.astype(BF16), do, TN, preferred_element_type=F32)
                dp = lax.dot_general(do, vv, NT, preferred_element_type=F32)
                ds = (p * (dp - st[:, STAT_SPLIT : STAT_SPLIT + 1]) * ATT_SCALE).astype(BF16)
                dk_s[...] += lax.dot_general(ds, qq, TN, preferred_element_type=F32)
                dq_ref[rows, :] += jnp.dot(ds, kk, preferred_element_type=F32)

        pair(pl.multiple_of(j * t, t), True)

        def loop_body(i, carry):
            pair(pl.multiple_of(i * t, t), False)
            return carry

        lax.fori_loop(j + 1, nb, loop_body, 0)
        dkn_ref[...] = dk_s[:, :NOPE].astype(BF16)
        dv_ref[...] = dv_s[...].astype(BF16)
        dkr_ref[...] = dk_s[:, NOPE:]

    whole = lambda width, off: pl.BlockSpec((s, width), lambda h, j: (0, off + h))
    blk = lambda width, off: pl.BlockSpec((t, width), lambda h, j: (j, off + h))
    return pl.pallas_call(
        body,
        name="attn_bwd",
        out_shape=(
            jax.ShapeDtypeStruct((s, hh * hd), F32),
            jax.ShapeDtypeStruct((s, w), BF16),
            jax.ShapeDtypeStruct((s, w), BF16),
            jax.ShapeDtypeStruct((s, w), F32),
        ),
        grid=(hh, nb),
        in_specs=[whole(hd, 0), blk(hd, 0), blk(VDIM, hh), whole(VDIM, 0), whole(LANES, 0)],
        out_specs=[whole(hd, 0), blk(NOPE, 0), blk(VDIM, 0), blk(HEAD_PAD, 0)],
        scratch_shapes=[pltpu.VMEM((t, hd), F32), pltpu.VMEM((t, VDIM), F32)],
        compiler_params=_cp("parallel", "arbitrary"),
    )(qc, kc, kv, dob, stats)


def _dq_unrope(dq, tabs, n_heads):
    s = dq.shape[0]
    hd = NOPE + HEAD_PAD
    w = n_heads * LANES
    ts = _tile(s, 512, SUBLANES)

    def body(d_ref, c_ref, sa_ref, sb_ref, o_ref):
        c, sa, sb = c_ref[...], sa_ref[...], sb_ref[...]
        for h in range(n_heads):
            o_ref[:, h * NOPE : (h + 1) * NOPE] = d_ref[:, h * hd : h * hd + NOPE].astype(BF16)
            rot = _rope_t(d_ref[:, h * hd + NOPE : (h + 1) * hd], c, sa, sb)
            o_ref[:, w + h * HEAD_PAD : w + (h + 1) * HEAD_PAD] = rot.astype(BF16)

    return pl.pallas_call(
        body,
        name="dq_unrope",
        out_shape=jax.ShapeDtypeStruct((s, 2 * w), BF16),
        grid=(s // ts,),
        in_specs=[_rows(ts, n_heads * hd)] + [_rows(ts, LANES)] * 3,
        out_specs=_rows(ts, 2 * w),
        compiler_params=_cp("parallel"),
    )(dq, *tabs)


def _adamw(w, m, v, grads, name):
    r, c = w.shape
    budget_rows = max(SUBLANES, (VMEM_LIMIT // 3) // (4 * c * 2 * (7 + len(grads))))
    tr = _tile(r, budget_rows, SUBLANES)
    ng = len(grads)
    c1 = 1.0 - ADAM_B1**ADAM_STEP
    c2 = 1.0 - ADAM_B2**ADAM_STEP

    def body(*refs):
        w_ref, m_ref, v_ref = refs[:3]
        g_ref, d_ref, nm_ref, nv_ref = refs[3 + ng :]
        g = refs[3][...]
        for extra in refs[4 : 3 + ng]:
            g = g + extra[...]
        mn = ADAM_B1 * m_ref[...] + (1.0 - ADAM_B1) * g
        vn = ADAM_B2 * v_ref[...] + (1.0 - ADAM_B2) * (g * g)
        g_ref[...] = g
        nm_ref[...] = mn
        nv_ref[...] = vn
        d_ref[...] = -ADAM_LR * ((mn / c1) / (jnp.sqrt(vn / c2) + ADAM_EPS) + ADAM_WD * w_ref[...])

    blk = pl.BlockSpec((tr, c), lambda i: (i, 0))
    out = jax.ShapeDtypeStruct((r, c), F32)
    return pl.pallas_call(
        body,
        name=name,
        out_shape=(out, out, out, out),
        grid=(r // tr,),
        in_specs=[blk] * (3 + ng),
        out_specs=[blk] * 4,
        compiler_params=_cp("parallel"),
    )(w, m, v, *grads)


def _ada_grad(ca_t, dm):
    d = ca_t.shape[0]
    nc = dm.shape[1]
    tn = _tile(nc, 512, LANES)

    def body(a_ref, b_ref, o_ref):
        o_ref[...] = jnp.dot(a_ref[...].astype(BF16), b_ref[...].astype(BF16), preferred_element_type=F32)

    return pl.pallas_call(
        body,
        name="ada_grad",
        out_shape=jax.ShapeDtypeStruct((d, nc), F32),
        grid=(nc // tn,),
        in_specs=[pl.BlockSpec((d, LANES), lambda j: (0, 0)), pl.BlockSpec((LANES, tn), lambda j: (0, j))],
        out_specs=pl.BlockSpec((d, tn), lambda j: (0, j)),
        compiler_params=_cp("parallel"),
    )(ca_t, dm)


def _sum_devices(g):
    n = g.shape[1]

    def body(g_ref, o_ref):
        acc = g_ref[0:SUBLANES, :]
        for dvc in range(1, N_DEV):
            acc = acc + g_ref[dvc * SUBLANES : (dvc + 1) * SUBLANES, :]
        o_ref[...] = acc

    return pl.pallas_call(
        body,
        name="sum_devices",
        out_shape=jax.ShapeDtypeStruct((SUBLANES, n), F32),
        in_specs=[pl.BlockSpec(memory_space=pltpu.VMEM)],
        out_specs=pl.BlockSpec(memory_space=pltpu.VMEM),
        compiler_params=pltpu.CompilerParams(vmem_limit_bytes=VMEM_LIMIT),
    )(g)


def _sum_chips(land, name):
    _, r, c = land.shape
    tr = _tile(r, max(SUBLANES * 2, (VMEM_LIMIT // 4) // (c * 2 * (2 * N_CHIP + 4 * 2))), SUBLANES * 2)

    def body(l_ref, o_ref):
        acc = l_ref[0].astype(F32)
        for k in range(1, N_CHIP):
            acc = acc + l_ref[k].astype(F32)
        o_ref[...] = acc

    return pl.pallas_call(
        body,
        name=name,
        out_shape=jax.ShapeDtypeStruct((r, c), F32),
        grid=(r // tr,),
        in_specs=[pl.BlockSpec((N_CHIP, tr, c), lambda i: (0, i, 0))],
        out_specs=pl.BlockSpec((tr, c), lambda i: (i, 0)),
        compiler_params=_cp("parallel"),
    )(land)


def _mesh_pos():
    return lax.axis_index("x"), lax.axis_index("y"), lax.axis_index("c")


def _other_chips(x, y):
    return [(1 - x, y), (x, 1 - y), (1 - x, 1 - y)]


def _all_gather8(x_shard, name):
    m_per, n = x_shard.shape

    def body(x_ref, out_ref, send_sems, recv_sems, local_sem):
        x, y, c = _mesh_pos()
        me, sibling = (x, y, c), (x, y, 1 - c)
        chips = _other_chips(x, y)

        def rows(px, py, pc):
            return out_ref.at[pl.ds((4 * px + 2 * py + pc) * m_per, m_per), :]

        def copy(k, block, to, src=None):
            return pltpu.make_async_remote_copy(
                src_ref=rows(*block) if src is None else src,
                dst_ref=rows(*block),
                send_sem=send_sems.at[k],
                recv_sem=recv_sems.at[k],
                device_id=to,
                device_id_type=MESH,
            )

        mine = pltpu.make_async_copy(x_ref, rows(*me), local_sem)
        mine.start()
        first = [copy(0, me, sibling, src=x_ref)]
        first += [copy(1 + j, me, (*chip, c), src=x_ref) for j, chip in enumerate(chips)]
        for cp in first:
            cp.start()
        passed = [copy(4 + j, (*chip, c), sibling) for j, chip in enumerate(chips)]
        for j, chip in enumerate(chips):
            copy(1 + j, (*chip, c), me).wait_recv()
            passed[j].start()
        copy(0, sibling, me).wait_recv()
        for j, chip in enumerate(chips):
            copy(4 + j, (*chip, 1 - c), me).wait_recv()
        for cp in first + passed:
            cp.wait_send()
        mine.wait()

    return pl.pallas_call(
        body,
        name=name,
        out_shape=jax.ShapeDtypeStruct((N_DEV * m_per, n), x_shard.dtype),
        in_specs=[pl.BlockSpec(memory_space=pltpu.VMEM)],
        out_specs=pl.BlockSpec(memory_space=pltpu.VMEM),
        scratch_shapes=[pltpu.SemaphoreType.DMA((7,)), pltpu.SemaphoreType.DMA((7,)), pltpu.SemaphoreType.DMA],
        compiler_params=pltpu.CompilerParams(vmem_limit_bytes=VMEM_LIMIT),
    )(x_shard)


def _chip_exchange(arrs, scatter, name):
    nt = len(arrs)

    def body(*refs):
        ins, outs = refs[:nt], refs[nt : 2 * nt]
        send_sems, recv_sems, local_sems = refs[2 * nt :]
        x, y, c = _mesh_pos()
        me = 2 * x + y
        chips = _other_chips(x, y)
        started = []
        for t in range(nt):
            src_me = ins[t].at[me] if scatter else ins[t]
            loc = pltpu.make_async_copy(src_me, outs[t].at[me], local_sems.at[t])
            loc.start()
            started.append(loc)
        sends = []
        for t in range(nt):
            for r, (px, py) in enumerate(chips):
                cp = pltpu.make_async_remote_copy(
                    src_ref=ins[t].at[2 * px + py] if scatter else ins[t],
                    dst_ref=outs[t].at[me],
                    send_sem=send_sems.at[3 * t + r],
                    recv_sem=recv_sems.at[3 * t + r],
                    device_id=(px, py, c),
                    device_id_type=MESH,
                )
                cp.start()
                sends.append(cp)
        for t in range(nt):
            for r, (px, py) in enumerate(chips):
                pltpu.make_async_remote_copy(
                    src_ref=ins[t].at[me] if scatter else ins[t],
                    dst_ref=outs[t].at[2 * px + py],
                    send_sem=send_sems.at[3 * t + r],
                    recv_sem=recv_sems.at[3 * t + r],
                    device_id=(px, py, c),
                    device_id_type=MESH,
                ).wait_recv()
        for cp in sends:
            cp.wait_send()
        for loc in started:
            loc.wait()

    def out_of(a):
        return jax.ShapeDtypeStruct(a.shape if scatter else (N_CHIP, *a.shape), a.dtype)

    return pl.pallas_call(
        body,
        name=name,
        out_shape=tuple(out_of(a) for a in arrs),
        in_specs=[pl.BlockSpec(memory_space=pl.ANY)] * nt,
        out_specs=[pl.BlockSpec(memory_space=pl.ANY)] * nt,
        scratch_shapes=[
            pltpu.SemaphoreType.DMA((3 * nt,)),
            pltpu.SemaphoreType.DMA((3 * nt,)),
            pltpu.SemaphoreType.DMA((nt,)),
        ],
    )(*arrs)


HBM_SPEC = pl.BlockSpec(memory_space=pltpu.HBM)
SEM_SPEC = pl.BlockSpec(memory_space=pltpu.SEMAPHORE)
DATAFLOW = pltpu.SideEffectType.DATAFLOW_SIDE_EFFECTING


def _own_slot(arrs, scatter, name):
    nt = len(arrs)

    def body(*refs):
        ins, outs, sems = refs[:nt], refs[nt : 2 * nt], refs[2 * nt]
        x, y, _ = _mesh_pos()
        me = 2 * x + y
        cps = [pltpu.make_async_copy(ins[t].at[me] if scatter else ins[t], outs[t].at[me], sems.at[t]) for t in range(nt)]
        for cp in cps:
            cp.start()
        for cp in cps:
            cp.wait()

    return pl.pallas_call(
        body,
        name=name,
        out_shape=tuple(jax.ShapeDtypeStruct(a.shape if scatter else (N_CHIP, *a.shape), a.dtype) for a in arrs),
        in_specs=[pl.BlockSpec(memory_space=pl.ANY)] * nt,
        out_specs=[pl.BlockSpec(memory_space=pl.ANY)] * nt,
        scratch_shapes=[pltpu.SemaphoreType.DMA((nt,))],
    )(*arrs)


def _exchange_copies(ins, lands, send_sems, recv_sems, scatter):
    x, y, c = _mesh_pos()
    me = 2 * x + y
    sends, recvs = [], []
    for t in range(len(ins)):
        for r, (px, py) in enumerate(_other_chips(x, y)):
            peer = 2 * px + py

            def copy(src, dst, k=3 * t + r, to=(px, py, c)):
                return pltpu.make_async_remote_copy(
                    src_ref=src, dst_ref=dst, send_sem=send_sems.at[k], recv_sem=recv_sems.at[k], device_id=to, device_id_type=MESH
                )

            sends.append(copy(ins[t].at[peer] if scatter else ins[t], lands[t].at[me]))
            recvs.append(copy(ins[t].at[me] if scatter else ins[t], lands[t].at[peer]))
    return sends, recvs


def _exchange_start(arrs, lands, scatter, name):
    nt = len(arrs)

    def body(*refs):
        ins, zones = refs[:nt], refs[nt : 2 * nt]
        send_sems, recv_sems, token = refs[2 * nt], refs[2 * nt + 1], refs[-1]
        sends, _ = _exchange_copies(ins, zones, send_sems, recv_sems, scatter)
        for cp in sends:
            cp.start()
        token[...] = jnp.zeros(token.shape, F32)

    bufs = list(arrs) + list(lands)
    return pl.pallas_call(
        body,
        name=name,
        out_shape=(
            pltpu.SemaphoreType.DMA((3 * nt,)),
            pltpu.SemaphoreType.DMA((3 * nt,)),
            *[pltpu.HBM(a.shape, a.dtype) for a in bufs],
            jax.ShapeDtypeStruct((SUBLANES, LANES), F32),
        ),
        in_specs=[HBM_SPEC] * (2 * nt),
        out_specs=(SEM_SPEC, SEM_SPEC, *[HBM_SPEC] * (2 * nt), pl.BlockSpec(memory_space=pltpu.VMEM)),
        input_output_aliases={k: 2 + k for k in range(2 * nt)},
        compiler_params=pltpu.CompilerParams(has_side_effects=DATAFLOW),
    )(*[pltpu.with_memory_space_constraint(a, pltpu.HBM) for a in bufs])


def _exchange_wait(state, after, scatter, name):
    send_sems, recv_sems, *bufs = state[:-1]
    nt = len(bufs) // 2

    def body(*refs):
        ins, zones = refs[:nt], refs[nt : 2 * nt]
        sends, recvs = _exchange_copies(ins, zones, refs[2 * nt], refs[2 * nt + 1], scatter)
        for cp in sends:
            cp.wait_send()
        for cp in recvs:
            cp.wait_recv()

    out = pl.pallas_call(
        body,
        name=name,
        out_shape=tuple(pltpu.HBM(a.shape, a.dtype) for a in bufs),
        in_specs=[HBM_SPEC] * (2 * nt) + [SEM_SPEC, SEM_SPEC, pl.BlockSpec(memory_space=pl.ANY)],
        out_specs=[HBM_SPEC] * (2 * nt),
        input_output_aliases={k: k for k in range(2 * nt)},
        compiler_params=pltpu.CompilerParams(has_side_effects=DATAFLOW),
    )(*bufs, send_sems, recv_sems, after)
    return out[nt:]


def _sibling_swap(arrs, name):
    nt = len(arrs)

    def body(*refs):
        ins, outs = refs[:nt], refs[nt : 2 * nt]
        send_sems, recv_sems = refs[2 * nt :]
        x, y, c = _mesh_pos()
        cps = [
            pltpu.make_async_remote_copy(
                src_ref=ins[t],
                dst_ref=outs[t],
                send_sem=send_sems.at[t],
                recv_sem=recv_sems.at[t],
                device_id=(x, y, 1 - c),
                device_id_type=MESH,
            )
            for t in range(nt)
        ]
        for cp in cps:
            cp.start()
        for cp in cps:
            cp.wait_recv()
        for cp in cps:
            cp.wait_send()

    return pl.pallas_call(
        body,
        name=name,
        out_shape=tuple(jax.ShapeDtypeStruct(a.shape, a.dtype) for a in arrs),
        in_specs=[pl.BlockSpec(memory_space=pl.ANY)] * nt,
        out_specs=[pl.BlockSpec(memory_space=pl.ANY)] * nt,
        scratch_shapes=[pltpu.SemaphoreType.DMA((nt,)), pltpu.SemaphoreType.DMA((nt,))],
    )(*arrs)


def _cols_from_shards(g):
    _, k, n = g.shape
    return jnp.transpose(g, (1, 0, 2)).reshape(k, N_CHIP * n)


def _cols_to_shards(a):
    k, n4 = a.shape
    return jnp.transpose(a.reshape(k, N_CHIP, n4 // N_CHIP), (1, 0, 2))


def _pad_to(vec, mult):
    n = vec.shape[0]
    return jnp.pad(vec, (0, (-n) % mult))


def kernel(x, c, positions, w_ada, b_ada, g_pre_mix, g_post_mix, w_in, g_q, w_uq, g_kv, w_ukv, conv_w_mix, conv_b_mix, w_o, g_pre_ffn, g_post_ffn, w_up, conv_w_ffn, conv_b_ffn, w_down, loss_target, m_w_ada, m_b_ada, m_g_pre_mix, m_g_post_mix, m_w_in, m_g_q, m_w_uq, m_g_kv, m_w_ukv, m_conv_w_mix, m_conv_b_mix, m_w_o, m_g_pre_ffn, m_g_post_ffn, m_w_up, m_conv_w_ffn, m_conv_b_ffn, m_w_down, v_w_ada, v_b_ada, v_g_pre_mix, v_g_post_mix, v_w_in, v_g_q, v_w_uq, v_g_kv, v_w_ukv, v_conv_w_mix, v_conv_b_mix, v_w_o, v_g_pre_ffn, v_g_post_ffn, v_w_up, v_conv_w_ffn, v_conv_b_ffn, v_w_down):
    xi, yi, ci = _mesh_pos()
    chip = 2 * xi + yi
    dev = 4 * xi + 2 * yi + ci

    s, d = x.shape[1], x.shape[2]
    ql, kl = g_q.shape[1], g_kv.shape[1]
    cwid = conv_b_mix.shape[1]
    f2 = conv_b_ffn.shape[1]
    hh = (w_uq.shape[2] * N_CHIP) // (NOPE + ROPE)
    w_att = hh * LANES
    nc_ada = w_ada.shape[2]
    lat = ql + kl + ROPE
    tc_mix = _tile(cwid, 512, LANES)
    lb = -(-(ql + kl + HEAD_PAD) // tc_mix) * tc_mix
    np_cols = lb + 3 * cwid
    assert cwid == hh * VDIM and w_att % tc_mix == 0

    x0 = x.reshape(s, d)
    tgt = loss_target.reshape(s, d)

    shards = [a[0].astype(BF16) for a in (w_in, w_uq, w_ukv, w_o, w_up, w_down)]
    ag_a = _exchange_start(shards[:3], _own_slot(shards[:3], False, "ag_a_own"), False, "ag_a_start")
    ag_b = _exchange_start(shards[3:], _own_slot(shards[3:], False, "ag_b_own"), False, "ag_b_start")
    started = ag_a[-1][0, 0] + ag_b[-1][0, 0]

    cwm_n, cwf_n = CONV_K * cwid // N_CHIP, CONV_K * f2 // N_CHIP
    pack_a = _pad_to(
        jnp.concatenate([c.reshape(-1) + started, conv_w_mix.reshape(-1), conv_w_ffn.reshape(-1)]), SUBLANES * LANES
    )
    rows_a = _all_gather8(pack_a.reshape(SUBLANES, -1), "ag8_inputs").reshape(N_DEV, -1)
    c_all = rows_a[:, :d]
    south = rows_a[0::2]
    cw_mix = jnp.concatenate([south[j, d : d + cwm_n].reshape(CONV_K, -1) for j in range(N_CHIP)], axis=1)
    cw_ffn = jnp.concatenate([south[j, d + cwm_n : d + cwm_n + cwf_n].reshape(CONV_K, -1) for j in range(N_CHIP)], axis=1)

    b_cols = lax.dynamic_slice(b_ada, (0, chip * nc_ada), (1, nc_ada))
    mod_part, c_act = _ada_fwd(c_all, w_ada[0], b_cols)
    mod_rows = _all_gather8(mod_part, "ag8_mod")
    mod = jnp.concatenate(
        [lax.dynamic_slice_in_dim(mod_rows, 2 * N_DEV * j + dev, 1, axis=0) for j in range(N_CHIP)], axis=1
    )
    sh_m, sc_m, gt_m, sh_f, sc_f, gt_f = [mod[:, k * d : (k + 1) * d] for k in range(N_MOD)]

    inv_freq = 1.0 / (ROPE_THETA ** (jnp.arange(0, ROPE, 2, dtype=F32) / ROPE))
    invf = jnp.concatenate([inv_freq, inv_freq, jnp.zeros((LANES - ROPE,), F32)]).reshape(1, LANES)
    tabs = _rope_tables(positions.astype(F32).reshape(s, 1), invf)
    h1 = _pre_fwd(x0, g_pre_mix, sc_m, sh_m)

    g_in, g_uq, g_ukv = _exchange_wait(ag_a, h1, False, "ag_a_wait")
    full_in = _cols_from_shards(g_in)
    w_in_p = jnp.concatenate([full_in[:, :lat], jnp.zeros((d, lb - lat), BF16), full_in[:, lat:]], axis=1)
    full_uq = _cols_from_shards(g_uq).reshape(ql, hh, NOPE + ROPE)
    w_uq_p = jnp.concatenate(
        [
            full_uq[:, :, :NOPE].reshape(ql, w_att),
            jnp.pad(full_uq[:, :, NOPE:], ((0, 0), (0, 0), (0, HEAD_PAD - ROPE))).reshape(ql, w_att),
        ],
        axis=1,
    )
    full_ukv = _cols_from_shards(g_ukv).reshape(kl, hh, NOPE + VDIM)
    w_ukv_p = jnp.concatenate([full_ukv[:, :, :NOPE].reshape(kl, w_att), full_ukv[:, :, NOPE:].reshape(kl, w_att)], axis=1)

    proj = _matmul(h1, w_in_p, out_dtype=F32, tm=1024, tn=768, tk=2048, name="mm_proj")
    qn, kvn, kr = _latent_fwd(proj, g_q, g_kv, tabs, lb)
    q_f = _matmul(qn, w_uq_p, out_dtype=F32, tm=1024, tn=1024, tk=2048, name="mm_q")
    kv_p = _matmul(kvn, w_ukv_p, out_dtype=BF16, tm=1024, tn=1024, tk=2048, name="mm_kv")
    q_c, k_c = _head_cat(q_f, kv_p, kr, tabs, hh)
    cat, lse2 = _attn_fwd(q_c, k_c, kv_p, hh, w_att + cwid)
    cat = _mixer_fwd(cat, proj, cw_mix, conv_b_mix, lb, w_att)
    g_o, g_up, g_down = _exchange_wait(ag_b, cat, False, "ag_b_wait")
    w_o_f = g_o.reshape(-1, d)
    w_up_f = _cols_from_shards(g_up)
    w_down_f = g_down.reshape(-1, d)
    mix = _matmul(cat, w_o_f, out_dtype=F32, tm=1024, tn=1024, tk=2048, name="mm_mix")

    x1, h2 = _mid_fwd(x0, mix, g_post_mix, gt_m, g_pre_ffn, sc_f, sh_f)
    up = _matmul(h2, w_up_f, out_dtype=F32, tm=1024, tn=1408, tk=2048, name="mm_up")
    act = _ffn_act_fwd(up, cw_ffn, conv_b_ffn)
    y = _matmul(act, w_down_f, out_dtype=F32, tm=1024, tn=1024, tk=1408, name="mm_down")
    dx2, dy, s_fin = _final(x1, y, tgt, g_post_ffn, gt_f)

    dw_down = _matmul(act, dy, ta=True, out_dtype=BF16, tm=1408, tn=1024, tk=1024, name="mm_dw_down")
    dact = _matmul(dy, w_down_f, tb=True, out_dtype=F32, tm=1024, tn=1408, tk=2048, name="mm_dact")
    dup_a, dup_g, s_fa, s_fg = _ffn_act_bwd(dact, up, cw_ffn, conv_b_ffn)
    dup = jnp.concatenate([dup_a, dup_g], axis=1)
    s_ffn = jnp.concatenate([s_fa, s_fg], axis=1)
    dw_up = _matmul(h2, dup, ta=True, out_dtype=BF16, tm=1024, tn=1408, tk=1024, name="mm_dw_up")
    dh2 = _matmul(dup, w_up_f, tb=True, out_dtype=F32, tm=1024, tn=1024, tk=1408, name="mm_dh2")
    dx1, dmix, s_mid = _mid_bwd(dh2, dx2, x1, mix, g_pre_ffn, sc_f, g_post_mix, gt_m)

    dw_o = _matmul(cat, dmix, ta=True, out_dtype=BF16, tm=1024, tn=1024, tk=1024, name="mm_dw_o")
    send_b = [dw_o.reshape(N_CHIP, -1, d), _cols_to_shards(dw_up), dw_down.reshape(N_CHIP, -1, d)]
    rs_b = _exchange_start(send_b, _own_slot(send_b, True, "rs_b_own"), True, "rs_b_start")
    dcat = _matmul(dmix, w_o_f, tb=True, out_dtype=F32, tm=1024, tn=1024, tk=2048, name="mm_dcat")
    dp_b, dp_c, dp_i, s_mix = _mixer_bwd(dcat, proj, cw_mix, conv_b_mix + rs_b[-1][0, 0], lb, w_att)
    dob, stats = _attn_bwd_prep(cat, dcat, lse2, hh)
    dq_raw, dkv_k, dkv_v, dkr_h = _attn_bwd(q_c, k_c, kv_p, dob, stats, hh)
    dkv_p = jnp.concatenate([dkv_k, dkv_v], axis=1)
    dq_p = _dq_unrope(dq_raw, tabs, hh)
    dw_uq_p = _matmul(qn, dq_p, ta=True, out_dtype=BF16, tm=1024, tn=1024, tk=1024, name="mm_dw_uq")
    dqn = _matmul(dq_p, w_uq_p, tb=True, out_dtype=F32, tm=1024, tn=1024, tk=2048, name="mm_dqn")
    dw_ukv_p = _matmul(kvn, dkv_p, ta=True, out_dtype=BF16, tm=1024, tn=1024, tk=1024, name="mm_dw_ukv")
    dkvn = _matmul(dkv_p, w_ukv_p, tb=True, out_dtype=F32, tm=1024, tn=1024, tk=2048, name="mm_dkvn")
    dp_lat, s_lat = _latent_bwd(proj, dqn, dkvn, dkr_h, g_q, g_kv, tabs, lb)
    dproj = jnp.concatenate([dp_lat, dp_b, dp_c, dp_i], axis=1)
    dw_in_p = _matmul(h1, dproj, ta=True, out_dtype=BF16, tm=1024, tn=1536, tk=1024, name="mm_dw_in")

    dw_in_f = jnp.concatenate([dw_in_p[:, :lat], dw_in_p[:, lb:]], axis=1)
    uq3 = dw_uq_p.reshape(ql, 2, hh, LANES)
    dw_uq_f = jnp.concatenate([uq3[:, 0], uq3[:, 1, :, :ROPE]], axis=2).reshape(ql, hh * (NOPE + ROPE))
    ukv3 = dw_ukv_p.reshape(kl, 2, hh, LANES)
    dw_ukv_f = jnp.concatenate([ukv3[:, 0], ukv3[:, 1]], axis=2).reshape(kl, hh * (NOPE + VDIM))
    send_a = [_cols_to_shards(dw_in_f), _cols_to_shards(dw_uq_f), _cols_to_shards(dw_ukv_f)]
    rs_a = _exchange_start(send_a, _own_slot(send_a, True, "rs_a_own"), True, "rs_a_start")

    dh1 = _matmul(dproj, w_in_p, tb=True, out_dtype=F32, tm=1024, tn=1024, tk=1536, name="mm_dh1")
    grad_x, s_first = _first_bwd(dh1, dx1, x0, g_pre_mix, sc_m + rs_a[-1][0, 0])

    names = ["w_in", "w_uq", "w_ukv", "w_o", "w_up", "w_down"]
    landed_b = _exchange_wait(rs_b, s_first, True, "rs_b_wait")
    part_b = [_sum_chips(l, "sum_chips_" + n) for l, n in zip(landed_b, names[3:])]
    other_b = _sibling_swap(part_b, "sibling_swap_b")
    landed_a = _exchange_wait(rs_a, other_b[0], True, "rs_a_wait")
    part_a = [_sum_chips(l, "sum_chips_" + n) for l, n in zip(landed_a, names[:3])]
    other_a = _sibling_swap(part_a, "sibling_swap_a")
    part, other = part_a + part_b, list(other_a) + list(other_b)

    dmod = jnp.concatenate([s_first[0:1], s_first[1:2], s_mid[3:4], s_mid[0:1], s_mid[1:2], s_fin[0:1]], axis=1)
    small = [
        dmod,
        s_first[2:3],
        s_mid[4:5],
        s_lat[0:1, :ql],
        s_lat[0:1, ql : ql + kl],
        s_mix[3:4],
        s_mid[2:3],
        s_fin[1:2],
        s_ffn[3:4],
        s_mix[0:3].reshape(1, -1),
        s_ffn[0:3].reshape(1, -1),
        s_fin[3:4, :LANES],
    ]
    sizes = [a.shape[1] for a in small]
    offs = [0]
    for n in sizes:
        offs.append(offs[-1] + n)
    pack_g = _pad_to(jnp.concatenate(small, axis=1).reshape(-1), SUBLANES * LANES * SUBLANES).reshape(SUBLANES, -1)
    gathered = _all_gather8(pack_g, "ag8_small_grads")
    tot = _sum_devices(gathered).reshape(-1)
    part_of = lambda k: tot[offs[k] : offs[k + 1]]
    dmod_all = gathered.reshape(N_DEV, -1)[:, : N_MOD * d]
    loss = part_of(11)[0]

    g_b_ada = part_of(0).reshape(1, -1)
    g_vecs = [part_of(k).reshape(1, -1) for k in range(1, 9)]
    g_cw_mix = lax.dynamic_slice(part_of(9).reshape(CONV_K, cwid), (0, chip * (cwid // N_CHIP)), (CONV_K, cwid // N_CHIP))
    g_cw_ffn = lax.dynamic_slice(part_of(10).reshape(CONV_K, f2), (0, chip * (f2 // N_CHIP)), (CONV_K, f2 // N_CHIP))

    dm_cols = lax.dynamic_slice(dmod_all, (0, chip * nc_ada), (N_DEV, nc_ada))
    g_w_ada = _ada_grad(
        jnp.pad(c_act.T, ((0, 0), (0, LANES - N_DEV))), jnp.pad(dm_cols, ((0, LANES - N_DEV), (0, 0)))
    )

    big_w = [w_in, w_uq, w_ukv, w_o, w_up, w_down]
    big_m = [m_w_in, m_w_uq, m_w_ukv, m_w_o, m_w_up, m_w_down]
    big_v = [v_w_in, v_w_uq, v_w_ukv, v_w_o, v_w_up, v_w_down]
    big = {}
    for n, w_, m_, v_, p_, o_ in zip(names, big_w, big_m, big_v, part, other):
        big[n] = [a[None] for a in _adamw(w_[0], m_[0], v_[0], [p_, o_], "adamw_" + n)]
    big["w_ada"] = [a[None] for a in _adamw(w_ada[0], m_w_ada[0], v_w_ada[0], [g_w_ada], "adamw_w_ada")]

    sm_names = ["b_ada", "g_pre_mix", "g_post_mix", "g_q", "g_kv", "conv_b_mix", "g_pre_ffn", "g_post_ffn", "conv_b_ffn",
                "conv_w_mix", "conv_w_ffn"]
    sm_w = [b_ada, g_pre_mix, g_post_mix, g_q, g_kv, conv_b_mix, g_pre_ffn, g_post_ffn, conv_b_ffn, conv_w_mix, conv_w_ffn]
    sm_m = [m_b_ada, m_g_pre_mix, m_g_post_mix, m_g_q, m_g_kv, m_conv_b_mix, m_g_pre_ffn, m_g_post_ffn, m_conv_b_ffn,
            m_conv_w_mix, m_conv_w_ffn]
    sm_v = [v_b_ada, v_g_pre_mix, v_g_post_mix, v_g_q, v_g_kv, v_conv_b_mix, v_g_pre_ffn, v_g_post_ffn, v_conv_b_ffn,
            v_conv_w_mix, v_conv_w_ffn]
    sm_g = [g_b_ada] + g_vecs + [g_cw_mix, g_cw_ffn]
    flat = lambda arrs: jnp.concatenate([a.reshape(1, -1) for a in arrs], axis=1)
    sm_out = _adamw(flat(sm_w), flat(sm_m), flat(sm_v), [flat(sm_g)], "adamw_small")
    sm = {}
    off = 0
    for n, w_ in zip(sm_names, sm_w):
        sm[n] = [o[:, off : off + w_.size].reshape(w_.shape) for o in sm_out]
        off += w_.size

    order = ["w_ada", "b_ada", "g_pre_mix", "g_post_mix", "w_in", "g_q", "w_uq", "g_kv", "w_ukv", "conv_w_mix", "conv_b_mix",
             "w_o", "g_pre_ffn", "g_post_ffn", "w_up", "conv_w_ffn", "conv_b_ffn", "w_down"]
    res = {**big, **sm}
    outs = [loss, grad_x.reshape(x.shape)]
    for k in range(4):
        outs += [res[n][k] for n in order]
    return tuple(outs)
```

```python
import math

import jax
import jax.numpy as jnp
from jax import lax
from jax.experimental import pallas as pl
from jax.experimental.pallas import tpu as pltpu

F32 = jnp.float32
BF16 = jnp.bfloat16
MESH = pl.DeviceIdType.MESH

N_DEV = 8
N_CHIP = 4
LANES = 128
SUBLANES = 8
VMEM_LIMIT = 56 * 2**20

NOPE = 128
ROPE = 64
VDIM = 128
HEAD_PAD = 128
ROPE_THETA = 10000.0
RMS_EPS = 1e-6
N_MOD = 6
CONV_K = 3
ATT_BLOCK = 512
NEG = -1e30

ADAM_LR = 0.001
ADAM_B1 = 0.9
ADAM_B2 = 0.999
ADAM_EPS = 1e-08
ADAM_WD = 0.01
ADAM_STEP = 10


def _tile(n, pref, align):
    if n <= pref:
        return n
    t = (pref // align) * align
    while t >= align:
        if n % t == 0:
            return t
        t -= align
    return n


def _cp(*sem):
    return pltpu.CompilerParams(dimension_semantics=sem, vmem_limit_bytes=VMEM_LIMIT)


def _rsq(x):
    return lax.rsqrt(jnp.mean(x * x, axis=-1, keepdims=True) + RMS_EPS)


def _norm_bwd(dn, n, r):
    return r * (dn - n * jnp.mean(dn * n, axis=-1, keepdims=True))


def _colsum(a):
    return jnp.sum(a, axis=0, keepdims=True)


def _matmul(a, b, *, ta=False, tb=False, out_dtype, tm, tn, tk, name):
    (k_a, m) = a.shape if ta else a.shape[::-1]
    (n, k_b) = b.shape if tb else b.shape[::-1]
    assert k_a == k_b, (a.shape, b.shape, ta, tb)
    tm, tn, tk = _tile(m, tm, LANES), _tile(n, tn, LANES), _tile(k_a, tk, LANES)
    nk = k_a // tk
    a_spec = pl.BlockSpec((tk, tm), lambda i, j, k: (k, i)) if ta else pl.BlockSpec((tm, tk), lambda i, j, k: (i, k))
    b_spec = pl.BlockSpec((tn, tk), lambda i, j, k: (j, k)) if tb else pl.BlockSpec((tk, tn), lambda i, j, k: (k, j))
    dims = (((0 if ta else 1,), (1 if tb else 0,)), ((), ()))

    def body(a_ref, b_ref, o_ref, *acc):
        p = lax.dot_general(a_ref[...].astype(BF16), b_ref[...].astype(BF16), dims, preferred_element_type=F32)
        if nk == 1:
            o_ref[...] = p.astype(o_ref.dtype)
        else:
            k = pl.program_id(2)

            @pl.when(k == 0)
            def _():
                acc[0][...] = p

            @pl.when(k > 0)
            def _():
                acc[0][...] += p

            @pl.when(k == nk - 1)
            def _():
                o_ref[...] = acc[0][...].astype(o_ref.dtype)

    return pl.pallas_call(
        body,
        name=name,
        out_shape=jax.ShapeDtypeStruct((m, n), out_dtype),
        grid=(m // tm, n // tn, nk),
        in_specs=[a_spec, b_spec],
        out_specs=pl.BlockSpec((tm, tn), lambda i, j, k: (i, j)),
        scratch_shapes=[] if nk == 1 else [pltpu.VMEM((tm, tn), F32)],
        compiler_params=_cp("parallel", "parallel", "arbitrary"),
    )(a, b)


def _rope_tables(pos_col, invf):
    s = pos_col.shape[0]
    ts = _tile(s, 1024, SUBLANES)
    half = ROPE // 2

    def body(p_ref, f_ref, c_ref, sa_ref, sb_ref):
        ang = p_ref[...] * f_ref[...]
        lane = lax.broadcasted_iota(jnp.int32, ang.shape, 1)
        cs, sn = jnp.cos(ang), jnp.sin(ang)
        c_ref[...] = jnp.where(lane < ROPE, cs, 0.0)
        sa_ref[...] = jnp.where((lane >= half) & (lane < ROPE), sn, 0.0)
        sb_ref[...] = jnp.where(lane < half, -sn, 0.0)

    tab = jax.ShapeDtypeStruct((s, LANES), F32)
    return pl.pallas_call(
        body,
        name="rope_tables",
        out_shape=(tab, tab, tab),
        grid=(s // ts,),
        in_specs=[pl.BlockSpec((ts, 1), lambda i: (i, 0)), pl.BlockSpec((1, LANES), lambda i: (0, 0))],
        out_specs=[pl.BlockSpec((ts, LANES), lambda i: (i, 0))] * 3,
        compiler_params=_cp("parallel"),
    )(pos_col, invf)


def _widen(t, w):
    return t if w == LANES else jnp.tile(t, (1, w // LANES))


def _rope(x, c, sa, sb):
    w = x.shape[1]
    c, sa, sb = _widen(c, w), _widen(sa, w), _widen(sb, w)
    return x * c + pltpu.roll(x, ROPE // 2, 1) * sa + pltpu.roll(x, w - ROPE // 2, 1) * sb


def _rope_t(d, c, sa, sb):
    w = d.shape[1]
    c, sa, sb = _widen(c, w), _widen(sa, w), _widen(sb, w)
    return d * c + pltpu.roll(d * sa, w - ROPE // 2, 1) + pltpu.roll(d * sb, ROPE // 2, 1)


def _ada_fwd(c_all, w, b):
    d, nc = w.shape
    tn = _tile(nc, 512, LANES)

    def body(c_ref, w_ref, b_ref, o_ref, ca_ref):
        cv = c_ref[...]
        ca = cv * jax.nn.sigmoid(cv)
        ca_ref[...] = ca
        o_ref[...] = jnp.dot(ca.astype(BF16), w_ref[...].astype(BF16), preferred_element_type=F32) + b_ref[...]

    return pl.pallas_call(
        body,
        name="ada_fwd",
        out_shape=(jax.ShapeDtypeStruct((N_DEV, nc), F32), jax.ShapeDtypeStruct((N_DEV, d), F32)),
        grid=(nc // tn,),
        in_specs=[
            pl.BlockSpec((N_DEV, d), lambda j: (0, 0)),
            pl.BlockSpec((d, tn), lambda j: (0, j)),
            pl.BlockSpec((1, tn), lambda j: (0, j)),
        ],
        out_specs=[pl.BlockSpec((N_DEV, tn), lambda j: (0, j)), pl.BlockSpec((N_DEV, d), lambda j: (0, 0))],
        compiler_params=_cp("arbitrary"),
    )(c_all, w, b)


def _rows(ts, d):
    return pl.BlockSpec((ts, d), lambda i: (i, 0))


def _vec(d):
    return pl.BlockSpec((1, d), lambda i: (0, 0))


def _sums(d):
    return pl.BlockSpec((SUBLANES, d), lambda i: (0, 0))


def _acc_rows(ref, i, rows):
    @pl.when(i == 0)
    def _():
        ref[...] = jnp.zeros(ref.shape, ref.dtype)

    for k, r in enumerate(rows):
        ref[k : k + 1, :] += r


def _pre_fwd(x, g, sc, sh):
    s, d = x.shape
    ts = _tile(s, 512, SUBLANES)

    def body(x_ref, g_ref, sc_ref, sh_ref, h_ref):
        xv = x_ref[...]
        h_ref[...] = (((xv * _rsq(xv)) * g_ref[...]) * (1.0 + sc_ref[...]) + sh_ref[...]).astype(BF16)

    return pl.pallas_call(
        body,
        name="pre_mix_fwd",
        out_shape=jax.ShapeDtypeStruct((s, d), BF16),
        grid=(s // ts,),
        in_specs=[_rows(ts, d), _vec(d), _vec(d), _vec(d)],
        out_specs=_rows(ts, d),
        compiler_params=_cp("parallel"),
    )(x, g, sc, sh)


def _mid_fwd(x0, mix, g_post, gt, g_pre, sc, sh):
    s, d = x0.shape
    ts = _tile(s, 256, SUBLANES)

    def body(x_ref, m_ref, gp_ref, gt_ref, g_ref, sc_ref, sh_ref, x1_ref, h_ref):
        mv = m_ref[...]
        x1 = x_ref[...] + gt_ref[...] * ((mv * _rsq(mv)) * gp_ref[...])
        x1_ref[...] = x1
        h_ref[...] = (((x1 * _rsq(x1)) * g_ref[...]) * (1.0 + sc_ref[...]) + sh_ref[...]).astype(BF16)

    return pl.pallas_call(
        body,
        name="mid_fwd",
        out_shape=(jax.ShapeDtypeStruct((s, d), F32), jax.ShapeDtypeStruct((s, d), BF16)),
        grid=(s // ts,),
        in_specs=[_rows(ts, d), _rows(ts, d)] + [_vec(d)] * 5,
        out_specs=[_rows(ts, d), _rows(ts, d)],
        compiler_params=_cp("parallel"),
    )(x0, mix, g_post, gt, g_pre, sc, sh)


def _final(x1, y, tgt, g_post, gt):
    s, d = x1.shape
    ts = _tile(s, 256, SUBLANES)
    ni = s // ts

    def body(x_ref, y_ref, t_ref, gp_ref, gt_ref, dx_ref, dy_ref, s_ref):
        i = pl.program_id(0)
        yv, gp, gt_v = y_ref[...], gp_ref[...], gt_ref[...]
        r = _rsq(yv)
        n = yv * r
        err = (x_ref[...] + gt_v * (n * gp)) - t_ref[...]
        dx = err * (1.0 / d)
        dx_ref[...] = dx
        dy_ref[...] = _norm_bwd(dx * (gt_v * gp), n, r).astype(BF16)
        _acc_rows(s_ref, i, [_colsum(dx * (n * gp)), _colsum(dx * gt_v * n), _colsum(err * err)])

        @pl.when(i == ni - 1)
        def _():
            tot = jnp.sum(s_ref[2:3, :], axis=1, keepdims=True) * (0.5 / d)
            s_ref[3:4, :] = jnp.broadcast_to(tot, (1, d))

    return pl.pallas_call(
        body,
        name="final_fwd_bwd",
        out_shape=(
            jax.ShapeDtypeStruct((s, d), F32),
            jax.ShapeDtypeStruct((s, d), BF16),
            jax.ShapeDtypeStruct((SUBLANES, d), F32),
        ),
        grid=(ni,),
        in_specs=[_rows(ts, d)] * 3 + [_vec(d)] * 2,
        out_specs=[_rows(ts, d), _rows(ts, d), _sums(d)],
        compiler_params=_cp("arbitrary"),
    )(x1, y, tgt, g_post, gt)


def _mid_bwd(dh2, dx2, x1, mix, g_pre, sc, g_post, gt):
    s, d = x1.shape
    ts = _tile(s, 256, SUBLANES)

    def body(dh_ref, dx2_ref, x_ref, m_ref, g_ref, sc_ref, gp_ref, gt_ref, dx1_ref, dm_ref, s_ref):
        i = pl.program_id(0)
        dh, xv, mv = dh_ref[...], x_ref[...], m_ref[...]
        g, sc_v, gp, gt_v = g_ref[...], sc_ref[...], gp_ref[...], gt_ref[...]
        r1 = _rsq(xv)
        n1 = xv * r1
        dx1 = dx2_ref[...] + _norm_bwd(dh * (g * (1.0 + sc_v)), n1, r1)
        dx1_ref[...] = dx1
        rm = _rsq(mv)
        nm = mv * rm
        dm_ref[...] = _norm_bwd(dx1 * (gt_v * gp), nm, rm).astype(BF16)
        _acc_rows(
            s_ref,
            i,
            [
                _colsum(dh),
                _colsum(dh * (n1 * g)),
                _colsum(dh * (1.0 + sc_v) * n1),
                _colsum(dx1 * (nm * gp)),
                _colsum(dx1 * gt_v * nm),
            ],
        )

    return pl.pallas_call(
        body,
        name="mid_bwd",
        out_shape=(
            jax.ShapeDtypeStruct((s, d), F32),
            jax.ShapeDtypeStruct((s, d), BF16),
            jax.ShapeDtypeStruct((SUBLANES, d), F32),
        ),
        grid=(s // ts,),
        in_specs=[_rows(ts, d)] * 4 + [_vec(d)] * 4,
        out_specs=[_rows(ts, d), _rows(ts, d), _sums(d)],
        compiler_params=_cp("arbitrary"),
    )(dh2, dx2, x1, mix, g_pre, sc, g_post, gt)


def _first_bwd(dh1, dx1, x0, g, sc):
    s, d = x0.shape
    ts = _tile(s, 256, SUBLANES)

    def body(dh_ref, dx1_ref, x_ref, g_ref, sc_ref, dx_ref, s_ref):
        i = pl.program_id(0)
        dh, xv, gv, sc_v = dh_ref[...], x_ref[...], g_ref[...], sc_ref[...]
        r = _rsq(xv)
        n = xv * r
        dx_ref[...] = dx1_ref[...] + _norm_bwd(dh * (gv * (1.0 + sc_v)), n, r)
        _acc_rows(s_ref, i, [_colsum(dh), _colsum(dh * (n * gv)), _colsum(dh * (1.0 + sc_v) * n)])

    return pl.pallas_call(
        body,
        name="first_bwd",
        out_shape=(jax.ShapeDtypeStruct((s, d), F32), jax.ShapeDtypeStruct((SUBLANES, d), F32)),
        grid=(s // ts,),
        in_specs=[_rows(ts, d)] * 3 + [_vec(d)] * 2,
        out_specs=[_rows(ts, d), _sums(d)],
        compiler_params=_cp("arbitrary"),
    )(dh1, dx1, x0, g, sc)


def _latent_fwd(proj, g_q, g_kv, tabs, lb):
    s = proj.shape[0]
    ql, kl = g_q.shape[1], g_kv.shape[1]
    ts = _tile(s, 512, SUBLANES)

    def body(p_ref, gq_ref, gk_ref, c_ref, sa_ref, sb_ref, q_ref, kv_ref, kr_ref):
        pv = p_ref[...]
        q, kv, kr = pv[:, :ql], pv[:, ql : ql + kl], pv[:, ql + kl : ql + kl + HEAD_PAD]
        q_ref[...] = ((q * _rsq(q)) * gq_ref[...]).astype(BF16)
        kv_ref[...] = ((kv * _rsq(kv)) * gk_ref[...]).astype(BF16)
        kr_ref[...] = _rope(kr, c_ref[...], sa_ref[...], sb_ref[...]).astype(BF16)

    return pl.pallas_call(
        body,
        name="latent_fwd",
        out_shape=(
            jax.ShapeDtypeStruct((s, ql), BF16),
            jax.ShapeDtypeStruct((s, kl), BF16),
            jax.ShapeDtypeStruct((s, HEAD_PAD), BF16),
        ),
        grid=(s // ts,),
        in_specs=[_rows(ts, lb), _vec(ql), _vec(kl)] + [_rows(ts, LANES)] * 3,
        out_specs=[_rows(ts, ql), _rows(ts, kl), _rows(ts, HEAD_PAD)],
        compiler_params=_cp("parallel"),
    )(proj, g_q, g_kv, *tabs)


def _latent_bwd(proj, dqn, dkvn, dkr_h, g_q, g_kv, tabs, lb):
    s = proj.shape[0]
    ql, kl = g_q.shape[1], g_kv.shape[1]
    hw = dkr_h.shape[1]
    ts = _tile(s, 256, SUBLANES)
    pad = lb - ql - kl - HEAD_PAD

    def body(p_ref, dq_ref, dkv_ref, dkr_ref, gq_ref, gk_ref, c_ref, sa_ref, sb_ref, o_ref, s_ref):
        i = pl.program_id(0)
        pv = p_ref[...]
        q, kv = pv[:, :ql], pv[:, ql : ql + kl]
        dqn_v, dkvn_v = dq_ref[...], dkv_ref[...]
        rq = _rsq(q)
        nq = q * rq
        rk = _rsq(kv)
        nk = kv * rk
        dkr = dkr_ref[:, :HEAD_PAD]
        for h in range(1, hw // HEAD_PAD):
            dkr = dkr + dkr_ref[:, h * HEAD_PAD : (h + 1) * HEAD_PAD]
        parts = [
            _norm_bwd(dqn_v * gq_ref[...], nq, rq).astype(BF16),
            _norm_bwd(dkvn_v * gk_ref[...], nk, rk).astype(BF16),
            _rope_t(dkr, c_ref[...], sa_ref[...], sb_ref[...]).astype(BF16),
        ]
        if pad:
            parts.append(jnp.zeros((ts, pad), BF16))
        o_ref[...] = jnp.concatenate(parts, axis=1)
        row = [_colsum(dqn_v * nq), _colsum(dkvn_v * nk), jnp.zeros((1, lb - ql - kl), F32)]
        _acc_rows(s_ref, i, [jnp.concatenate(row, axis=1)])

    return pl.pallas_call(
        body,
        name="latent_bwd",
        out_shape=(jax.ShapeDtypeStruct((s, lb), BF16), jax.ShapeDtypeStruct((SUBLANES, lb), F32)),
        grid=(s // ts,),
        in_specs=[_rows(ts, lb), _rows(ts, ql), _rows(ts, kl), _rows(ts, hw)]
        + [_vec(ql), _vec(kl)]
        + [_rows(ts, LANES)] * 3,
        out_specs=[_rows(ts, lb), _sums(lb)],
        compiler_params=_cp("arbitrary"),
    )(proj, dqn, dkvn, dkr_h, g_q, g_kv, *tabs)


def _conv3(ext, w, b):
    return (pltpu.roll(ext, 2, 0) * w[0:1] + pltpu.roll(ext, 1, 0) * w[1:2]) + ext * w[2:3] + b


def _conv3_t(du, w):
    n = du.shape[0]
    return du * w[2:3] + pltpu.roll(du, n - 1, 0) * w[1:2] + pltpu.roll(du, n - 2, 0) * w[0:1]


def _halo_maps(ts, s):
    r8, last = ts // SUBLANES, s // SUBLANES - 1
    prev = lambda i: jnp.maximum(i * r8 - 1, 0)
    nxt = lambda i: jnp.minimum((i + 1) * r8, last)
    return prev, nxt


def _mixer_fwd(cat, proj, cw, cb, lb, col0):
    s = proj.shape[0]
    cwid = cw.shape[1]
    ts = _tile(s, 512, SUBLANES)
    tc = _tile(cwid, 512, LANES)
    assert lb % tc == 0 and col0 % tc == 0
    nj, ob, oc = cwid // tc, lb // tc, col0 // tc
    prev, _ = _halo_maps(ts, s)

    def body(_, gb_ref, gc_ref, ci_ref, pgc_ref, pci_ref, w_ref, b_ref, o_ref):
        keep = jnp.where(pl.program_id(1) > 0, 1.0, 0.0)
        ext = jnp.concatenate([pgc_ref[...] * pci_ref[...] * keep, gc_ref[...] * ci_ref[...]], axis=0)
        o_ref[...] = (gb_ref[...] * _conv3(ext, w_ref[...], b_ref[...])[SUBLANES:]).astype(BF16)

    def col(k):
        return pl.BlockSpec((ts, tc), lambda j, i: (i, ob + k * nj + j))

    def halo(k):
        return pl.BlockSpec((SUBLANES, tc), lambda j, i: (prev(i), ob + k * nj + j))

    return pl.pallas_call(
        body,
        name="mixer_fwd",
        out_shape=jax.ShapeDtypeStruct(cat.shape, BF16),
        grid=(nj, s // ts),
        in_specs=[pl.BlockSpec(memory_space=pl.ANY), col(0), col(1), col(2), halo(1), halo(2)]
        + [pl.BlockSpec((CONV_K, tc), lambda j, i: (0, j)), pl.BlockSpec((1, tc), lambda j, i: (0, j))],
        out_specs=pl.BlockSpec((ts, tc), lambda j, i: (i, oc + j)),
        input_output_aliases={0: 0},
        compiler_params=_cp("parallel", "arbitrary"),
    )(cat, proj, proj, proj, proj, proj, cw, cb)


def _mixer_bwd(dcat, proj, cw, cb, lb, col0):
    s = proj.shape[0]
    cwid = cw.shape[1]
    ts = _tile(s, 256, SUBLANES)
    tc = _tile(cwid, 512, LANES)
    nj, ob, oc = cwid // tc, lb // tc, col0 // tc
    ni = s // ts
    prev, nxt = _halo_maps(ts, s)

    def body(d_ref, dn_ref, gb_ref, gbn_ref, gc_ref, gcp_ref, gcn_ref, ci_ref, cip_ref, cin_ref, w_ref, b_ref,
             dgb_ref, dgc_ref, dci_ref, s_ref):
        i = pl.program_id(1)
        keep_p = jnp.where(i > 0, 1.0, 0.0)
        keep_n = jnp.where(i < ni - 1, 1.0, 0.0)
        w = w_ref[...]
        gc = jnp.concatenate([gcp_ref[...], gc_ref[...], gcn_ref[...]], axis=0)
        ci = jnp.concatenate([cip_ref[...] * keep_p, ci_ref[...], cin_ref[...]], axis=0)
        u = gc * ci
        cv = _conv3(u, w, b_ref[...])[SUBLANES:]
        dco = jnp.concatenate([d_ref[...], dn_ref[...] * keep_n], axis=0)
        gb = jnp.concatenate([gb_ref[...], gbn_ref[...]], axis=0)
        dgb_ref[...] = (dco * cv)[:ts].astype(BF16)
        dcv = dco * gb
        du = _conv3_t(dcv, w)[:ts]
        dgc_ref[...] = (du * ci_ref[...]).astype(BF16)
        dci_ref[...] = (du * gc_ref[...]).astype(BF16)
        dt = dcv[:ts]
        u1, u2 = pltpu.roll(u, 1, 0), pltpu.roll(u, 2, 0)
        lo, hi = SUBLANES, SUBLANES + ts
        _acc_rows(s_ref, i, [_colsum(dt * u2[lo:hi]), _colsum(dt * u1[lo:hi]), _colsum(dt * u[lo:hi]), _colsum(dt)])

    def col(k):
        return pl.BlockSpec((ts, tc), lambda j, i: (i, ob + k * nj + j))

    def halo(k, which):
        return pl.BlockSpec((SUBLANES, tc), lambda j, i: (which(i), ob + k * nj + j))

    out_col = [pl.BlockSpec((ts, tc), lambda j, i: (i, j))] * 3
    grad = jax.ShapeDtypeStruct((s, cwid), BF16)
    return pl.pallas_call(
        body,
        name="mixer_bwd",
        out_shape=(grad, grad, grad, jax.ShapeDtypeStruct((SUBLANES, cwid), F32)),
        grid=(nj, ni),
        in_specs=[
            pl.BlockSpec((ts, tc), lambda j, i: (i, oc + j)),
            pl.BlockSpec((SUBLANES, tc), lambda j, i: (nxt(i), oc + j)),
            col(0), halo(0, nxt),
            col(1), halo(1, prev), halo(1, nxt),
            col(2), halo(2, prev), halo(2, nxt),
            pl.BlockSpec((CONV_K, tc), lambda j, i: (0, j)),
            pl.BlockSpec((1, tc), lambda j, i: (0, j)),
        ],
        out_specs=out_col + [pl.BlockSpec((SUBLANES, tc), lambda j, i: (0, j))],
        compiler_params=_cp("parallel", "arbitrary"),
    )(dcat, dcat, proj, proj, proj, proj, proj, proj, proj, proj, cw, cb)


def _ffn_act_fwd(up, cw, cb):
    s, f2 = up.shape
    f = f2 // 2
    ts = _tile(s, 512, SUBLANES)
    tc = _tile(f, 512, LANES)
    nj = f // tc
    prev, _ = _halo_maps(ts, s)

    def body(ua_ref, ug_ref, pa_ref, pg_ref, wa_ref, wg_ref, ba_ref, bg_ref, o_ref):
        keep = jnp.where(pl.program_id(1) > 0, 1.0, 0.0)

        def conv(u_ref, p_ref, w_ref, b_ref):
            ext = jnp.concatenate([p_ref[...] * keep, u_ref[...]], axis=0)
            return _conv3(ext, w_ref[...], b_ref[...])[SUBLANES:]

        a = conv(ua_ref, pa_ref, wa_ref, ba_ref)
        g = conv(ug_ref, pg_ref, wg_ref, bg_ref)
        o_ref[...] = ((g * jax.nn.sigmoid(g)) * a).astype(BF16)

    def col(k):
        return pl.BlockSpec((ts, tc), lambda j, i: (i, k * nj + j))

    def halo(k):
        return pl.BlockSpec((SUBLANES, tc), lambda j, i: (prev(i), k * nj + j))

    def wspec(rows, k):
        return pl.BlockSpec((rows, tc), lambda j, i: (0, k * nj + j))

    return pl.pallas_call(
        body,
        name="ffn_act_fwd",
        out_shape=jax.ShapeDtypeStruct((s, f), BF16),
        grid=(nj, s // ts),
        in_specs=[col(0), col(1), halo(0), halo(1), wspec(CONV_K, 0), wspec(CONV_K, 1), wspec(1, 0), wspec(1, 1)],
        out_specs=pl.BlockSpec((ts, tc), lambda j, i: (i, j)),
        compiler_params=_cp("parallel", "arbitrary"),
    )(up, up, up, up, cw, cw, cb, cb)


def _ffn_act_bwd(dact, up, cw, cb):
    s, f2 = up.shape
    f = f2 // 2
    ts = _tile(s, 256, SUBLANES)
    tc = _tile(f, 512, LANES)
    nj, ni = f // tc, s // ts
    prev, nxt = _halo_maps(ts, s)

    def body(d_ref, dn_ref, ua_ref, uap_ref, uan_ref, ug_ref, ugp_ref, ugn_ref, wa_ref, wg_ref, ba_ref, bg_ref,
             dua_ref, dug_ref, sa_ref, sg_ref):
        i = pl.program_id(1)
        keep_p = jnp.where(i > 0, 1.0, 0.0)
        keep_n = jnp.where(i < ni - 1, 1.0, 0.0)
        wa, wg = wa_ref[...], wg_ref[...]
        exta = jnp.concatenate([uap_ref[...] * keep_p, ua_ref[...], uan_ref[...]], axis=0)
        extg = jnp.concatenate([ugp_ref[...] * keep_p, ug_ref[...], ugn_ref[...]], axis=0)
        a = _conv3(exta, wa, ba_ref[...])[SUBLANES:]
        g = _conv3(extg, wg, bg_ref[...])[SUBLANES:]
        dact_v = jnp.concatenate([d_ref[...], dn_ref[...] * keep_n], axis=0)
        sg = jax.nn.sigmoid(g)
        da = dact_v * (g * sg)
        dg = dact_v * a * (sg * (1.0 + g * (1.0 - sg)))
        lo, hi = SUBLANES, SUBLANES + ts

        def back(du, ext, w, dup_ref, s_ref):
            dup_ref[...] = _conv3_t(du, w)[:ts].astype(BF16)
            dt = du[:ts]
            e1, e2 = pltpu.roll(ext, 1, 0), pltpu.roll(ext, 2, 0)
            _acc_rows(s_ref, i, [_colsum(dt * e2[lo:hi]), _colsum(dt * e1[lo:hi]), _colsum(dt * ext[lo:hi]), _colsum(dt)])

        back(da, exta, wa, dua_ref, sa_ref)
        back(dg, extg, wg, dug_ref, sg_ref)

    def col(k):
        return pl.BlockSpec((ts, tc), lambda j, i: (i, k * nj + j))

    def halo(k, which):
        return pl.BlockSpec((SUBLANES, tc), lambda j, i: (which(i), k * nj + j))

    def wspec(rows, k):
        return pl.BlockSpec((rows, tc), lambda j, i: (0, k * nj + j))

    half = pl.BlockSpec((ts, tc), lambda j, i: (i, j))
    half_sums = pl.BlockSpec((SUBLANES, tc), lambda j, i: (0, j))
    return pl.pallas_call(
        body,
        name="ffn_act_bwd",
        out_shape=(
            jax.ShapeDtypeStruct((s, f), BF16),
            jax.ShapeDtypeStruct((s, f), BF16),
            jax.ShapeDtypeStruct((SUBLANES, f), F32),
            jax.ShapeDtypeStruct((SUBLANES, f), F32),
        ),
        grid=(nj, ni),
        in_specs=[
            pl.BlockSpec((ts, tc), lambda j, i: (i, j)),
            pl.BlockSpec((SUBLANES, tc), lambda j, i: (nxt(i), j)),
            col(0), halo(0, prev), halo(0, nxt),
            col(1), halo(1, prev), halo(1, nxt),
            wspec(CONV_K, 0), wspec(CONV_K, 1), wspec(1, 0), wspec(1, 1),
        ],
        out_specs=[half, half, half_sums, half_sums],
        compiler_params=_cp("parallel", "arbitrary"),
    )(dact, dact, up, up, up, up, up, up, cw, cw, cb, cb)


ATT_SCALE = 1.0 / math.sqrt(NOPE + ROPE)
LOG2E = math.log2(math.e)
ATT_C2 = ATT_SCALE * LOG2E
ATT_SUB = 256
STAT_SPLIT = 64
NT = (((1,), (1,)), ((), ()))
TN = (((0,), (0,)), ((), ()))


def _head_cat(q, kv, kr, tabs, n_heads):
    s, w2 = q.shape
    w = w2 // 2
    ts = _tile(s, 512, SUBLANES)
    hd = NOPE + HEAD_PAD

    def body(q_ref, kv_ref, kr_ref, c_ref, sa_ref, sb_ref, qc_ref, kc_ref):
        qv = q_ref[...]
        qr = _rope(qv[:, w:], c_ref[...], sa_ref[...], sb_ref[...]).astype(BF16)
        krv = kr_ref[...]
        for h in range(n_heads):
            qc_ref[:, h * hd : h * hd + NOPE] = qv[:, h * NOPE : (h + 1) * NOPE].astype(BF16)
            qc_ref[:, h * hd + NOPE : (h + 1) * hd] = qr[:, h * HEAD_PAD : (h + 1) * HEAD_PAD]
            kc_ref[:, h * hd : h * hd + NOPE] = kv_ref[:, h * NOPE : (h + 1) * NOPE]
            kc_ref[:, h * hd + NOPE : (h + 1) * hd] = krv

    out = jax.ShapeDtypeStruct((s, n_heads * hd), BF16)
    return pl.pallas_call(
        body,
        name="head_cat",
        out_shape=(out, out),
        grid=(s // ts,),
        in_specs=[_rows(ts, w2), _rows(ts, w), _rows(ts, HEAD_PAD)] + [_rows(ts, LANES)] * 3,
        out_specs=[_rows(ts, n_heads * hd)] * 2,
        compiler_params=_cp("parallel"),
    )(q, kv, kr, *tabs)


def _attn_fwd(qc, kc, kv, n_heads, cat_cols):
    s = qc.shape[0]
    t = _tile(s, ATT_BLOCK, LANES)
    sub = _tile(t, ATT_SUB, LANES)
    hh = n_heads
    hd = NOPE + HEAD_PAD

    def body(q_ref, k_ref, v_ref, o_ref, lse_ref, m_s, l_s, acc_s):
        i = pl.program_id(1)
        m_s[...] = jnp.full(m_s.shape, NEG, F32)
        l_s[...] = jnp.zeros(l_s.shape, F32)
        acc_s[...] = jnp.zeros(acc_s.shape, F32)

        def chunk(k0, diag):
            m_all, l_all, acc_all = m_s[...], l_s[...], acc_s[...]
            new_m, new_l, new_acc = [], [], []
            for r0 in range(0, t, sub):
                ncol = r0 + sub if diag else t
                kk = k_ref[pl.ds(k0, ncol), :]
                sc = lax.dot_general(q_ref[pl.ds(r0, sub), :], kk, NT, preferred_element_type=F32)
                if diag:
                    row = lax.broadcasted_iota(jnp.int32, sc.shape, 0) + r0
                    col = lax.broadcasted_iota(jnp.int32, sc.shape, 1)
                    sc = jnp.where(col <= row, sc, NEG)
                m_prev = m_all[r0 : r0 + sub]
                m_new = jnp.maximum(m_prev, jnp.max(sc, axis=1, keepdims=True))
                alpha = jnp.exp2((m_prev - m_new) * ATT_C2)
                p = jnp.exp2((sc - m_new) * ATT_C2)
                pv = jnp.dot(p.astype(BF16), v_ref[pl.ds(k0, ncol), :], preferred_element_type=F32)
                new_m.append(m_new)
                new_l.append(alpha * l_all[r0 : r0 + sub] + jnp.sum(p, axis=1, keepdims=True))
                new_acc.append(alpha * acc_all[r0 : r0 + sub] + pv)
            m_s[...] = jnp.concatenate(new_m, axis=0)
            l_s[...] = jnp.concatenate(new_l, axis=0)
            acc_s[...] = jnp.concatenate(new_acc, axis=0)

        def loop_body(k, carry):
            chunk(pl.multiple_of(k * t, t), False)
            return carry

        lax.fori_loop(0, i, loop_body, 0)
        chunk(pl.multiple_of(i * t, t), True)
        l = l_s[...]
        o_ref[...] = (acc_s[...] / l).astype(BF16)
        lse_ref[...] = jnp.broadcast_to(m_s[...] * ATT_C2 + jnp.log(l) * LOG2E, lse_ref.shape)

    return pl.pallas_call(
        body,
        name="attn_fwd",
        out_shape=(jax.ShapeDtypeStruct((s, cat_cols), BF16), jax.ShapeDtypeStruct((s, hh * LANES), F32)),
        grid=(hh, s // t),
        in_specs=[
            pl.BlockSpec((t, hd), lambda h, i: (i, h)),
            pl.BlockSpec((s, hd), lambda h, i: (0, h)),
            pl.BlockSpec((s, VDIM), lambda h, i: (0, hh + h)),
        ],
        out_specs=[pl.BlockSpec((t, VDIM), lambda h, i: (i, h)), pl.BlockSpec((t, LANES), lambda h, i: (i, h))],
        scratch_shapes=[pltpu.VMEM((t, 1), F32), pltpu.VMEM((t, 1), F32), pltpu.VMEM((t, VDIM), F32)],
        compiler_params=_cp("parallel", "parallel"),
    )(qc, kc, kv)


def _attn_bwd_prep(cat, dcat, lse2, n_heads):
    s, w = lse2.shape
    ts = _tile(s, 512, SUBLANES)

    def body(o_ref, do_ref, lse_ref, dob_ref, st_ref):
        do = do_ref[...]
        dob_ref[...] = do.astype(BF16)
        prod = do * o_ref[...].astype(F32)
        lane = lax.broadcasted_iota(jnp.int32, (ts, LANES), 1)
        for h in range(n_heads):
            cols = slice(h * LANES, (h + 1) * LANES)
            dsum = jnp.sum(prod[:, cols], axis=1, keepdims=True)
            st_ref[:, cols] = jnp.where(lane < STAT_SPLIT, lse_ref[:, cols], dsum)

    return pl.pallas_call(
        body,
        name="attn_bwd_prep",
        out_shape=(jax.ShapeDtypeStruct((s, w), BF16), jax.ShapeDtypeStruct((s, w), F32)),
        grid=(s // ts,),
        in_specs=[_rows(ts, w)] * 3,
        out_specs=[_rows(ts, w)] * 2,
        compiler_params=_cp("parallel"),
    )(cat, dcat, lse2)


def _attn_bwd(qc, kc, kv, dob, stats, n_heads):
    s = qc.shape[0]
    t = _tile(s, ATT_BLOCK, LANES)
    sub = _tile(t, ATT_SUB, LANES)
    nb = s // t
    hh = n_heads
    hd = NOPE + HEAD_PAD
    w = hh * LANES

    def body(q_ref, k_ref, v_ref, do_ref, st_ref, dq_ref, dkn_ref, dv_ref, dkr_ref, dk_s, dv_s):
        j = pl.program_id(1)

        @pl.when(j == 0)
        def _():
            dq_ref[...] = jnp.zeros(dq_ref.shape, F32)

        dk_s[...] = jnp.zeros(dk_s.shape, F32)
        dv_s[...] = jnp.zeros(dv_s.shape, F32)
        kk, vv = k_ref[...], v_ref[...]

        def pair(i0, diag):
            for r0 in range(0, t, sub):
                rows = pl.ds(i0 + r0, sub)
                qq, do, st = q_ref[rows, :], do_ref[rows, :], st_ref[rows, :]
                sc = lax.dot_general(qq, kk, NT, preferred_element_type=F32)
                if diag:
                    row = lax.broadcasted_iota(jnp.int32, sc.shape, 0) + r0
                    col = lax.broadcasted_iota(jnp.int32, sc.shape, 1)
                    sc = jnp.where(col <= row, sc, NEG)
                p = jnp.exp2(sc * ATT_C2 - st[:, 0:1])
                dv_s[...] += lax.dot_general(p.astype(BF16), do, TN, preferred_element_type=F32)
                dp = lax.dot_general(do, vv, NT, preferred_element_type=F32)
                ds = (p * (dp - st[:, STAT_SPLIT : STAT_SPLIT + 1]) * ATT_SCALE).astype(BF16)
                dk_s[...] += lax.dot_general(ds, qq, TN, preferred_element_type=F32)
                dq_ref[rows, :] += jnp.dot(ds, kk, preferred_element_type=F32)

        pair(pl.multiple_of(j * t, t), True)

        def loop_body(i, carry):
            pair(pl.multiple_of(i * t, t), False)
            return carry

        lax.fori_loop(j + 1, nb, loop_body, 0)
        dkn_ref[...] = dk_s[:, :NOPE].astype(BF16)
        dv_ref[...] = dv_s[...].astype(BF16)
        dkr_ref[...] = dk_s[:, NOPE:]

    whole = lambda width, off: pl.BlockSpec((s, width), lambda h, j: (0, off + h))
    blk = lambda width, off: pl.BlockSpec((t, width), lambda h, j: (j, off + h))
    return pl.pallas_call(
        body,
        name="attn_bwd",
        out_shape=(
            jax.ShapeDtypeStruct((s, hh * hd), F32),
            jax.ShapeDtypeStruct((s, w), BF16),
            jax.ShapeDtypeStruct((s, w), BF16),
            jax.ShapeDtypeStruct((s, w), F32),
        ),
        grid=(hh, nb),
        in_specs=[whole(hd, 0), blk(hd, 0), blk(VDIM, hh), whole(VDIM, 0), whole(LANES, 0)],
        out_specs=[whole(hd, 0), blk(NOPE, 0), blk(VDIM, 0), blk(HEAD_PAD, 0)],
        scratch_shapes=[pltpu.VMEM((t, hd), F32), pltpu.VMEM((t, VDIM), F32)],
        compiler_params=_cp("parallel", "arbitrary"),
    )(qc, kc, kv, dob, stats)


def _dq_unrope(dq, tabs, n_heads):
    s = dq.shape[0]
    hd = NOPE + HEAD_PAD
    w = n_heads * LANES
    ts = _tile(s, 512, SUBLANES)

    def body(d_ref, c_ref, sa_ref, sb_ref, o_ref):
        c, sa, sb = c_ref[...], sa_ref[...], sb_ref[...]
        for h in range(n_heads):
            o_ref[:, h * NOPE : (h + 1) * NOPE] = d_ref[:, h * hd : h * hd + NOPE].astype(BF16)
            rot = _rope_t(d_ref[:, h * hd + NOPE : (h + 1) * hd], c, sa, sb)
            o_ref[:, w + h * HEAD_PAD : w + (h + 1) * HEAD_PAD] = rot.astype(BF16)

    return pl.pallas_call(
        body,
        name="dq_unrope",
        out_shape=jax.ShapeDtypeStruct((s, 2 * w), BF16),
        grid=(s // ts,),
        in_specs=[_rows(ts, n_heads * hd)] + [_rows(ts, LANES)] * 3,
        out_specs=_rows(ts, 2 * w),
        compiler_params=_cp("parallel"),
    )(dq, *tabs)


def _adamw(w, m, v, grads, name):
    r, c = w.shape
    budget_rows = max(SUBLANES, (VMEM_LIMIT // 3) // (4 * c * 2 * (7 + len(grads))))
    tr = _tile(r, budget_rows, SUBLANES)
    ng = len(grads)
    c1 = 1.0 - ADAM_B1**ADAM_STEP
    c2 = 1.0 - ADAM_B2**ADAM_STEP

    def body(*refs):
        w_ref, m_ref, v_ref = refs[:3]
        g_ref, d_ref, nm_ref, nv_ref = refs[3 + ng :]
        g = refs[3][...]
        for extra in refs[4 : 3 + ng]:
            g = g + extra[...]
        mn = ADAM_B1 * m_ref[...] + (1.0 - ADAM_B1) * g
        vn = ADAM_B2 * v_ref[...] + (1.0 - ADAM_B2) * (g * g)
        g_ref[...] = g
        nm_ref[...] = mn
        nv_ref[...] = vn
        d_ref[...] = -ADAM_LR * ((mn / c1) / (jnp.sqrt(vn / c2) + ADAM_EPS) + ADAM_WD * w_ref[...])

    blk = pl.BlockSpec((tr, c), lambda i: (i, 0))
    out = jax.ShapeDtypeStruct((r, c), F32)
    return pl.pallas_call(
        body,
        name=name,
        out_shape=(out, out, out, out),
        grid=(r // tr,),
        in_specs=[blk] * (3 + ng),
        out_specs=[blk] * 4,
        compiler_params=_cp("parallel"),
    )(w, m, v, *grads)


def _ada_grad(ca_t, dm):
    d = ca_t.shape[0]
    nc = dm.shape[1]
    tn = _tile(nc, 512, LANES)

    def body(a_ref, b_ref, o_ref):
        o_ref[...] = jnp.dot(a_ref[...].astype(BF16), b_ref[...].astype(BF16), preferred_element_type=F32)

    return pl.pallas_call(
        body,
        name="ada_grad",
        out_shape=jax.ShapeDtypeStruct((d, nc), F32),
        grid=(nc // tn,),
        in_specs=[pl.BlockSpec((d, LANES), lambda j: (0, 0)), pl.BlockSpec((LANES, tn), lambda j: (0, j))],
        out_specs=pl.BlockSpec((d, tn), lambda j: (0, j)),
        compiler_params=_cp("parallel"),
    )(ca_t, dm)


def _sum_devices(g):
    n = g.shape[1]

    def body(g_ref, o_ref):
        acc = g_ref[0:SUBLANES, :]
        for dvc in range(1, N_DEV):
            acc = acc + g_ref[dvc * SUBLANES : (dvc + 1) * SUBLANES, :]
        o_ref[...] = acc

    return pl.pallas_call(
        body,
        name="sum_devices",
        out_shape=jax.ShapeDtypeStruct((SUBLANES, n), F32),
        in_specs=[pl.BlockSpec(memory_space=pltpu.VMEM)],
        out_specs=pl.BlockSpec(memory_space=pltpu.VMEM),
        compiler_params=pltpu.CompilerParams(vmem_limit_bytes=VMEM_LIMIT),
    )(g)


def _sum_chips(land, sent, name):
    _, r, c = land.shape
    tr = _tile(r, max(SUBLANES * 2, (VMEM_LIMIT // 4) // (c * 2 * (4 * N_CHIP + 4 * 2))), SUBLANES * 2)

    def body(l_ref, s_ref, o_ref):
        x, y, _ = _mesh_pos()
        me = 2 * x + y
        acc = jnp.where(me == 0, s_ref[0], l_ref[0]).astype(F32)
        for k in range(1, N_CHIP):
            acc = acc + jnp.where(me == k, s_ref[k], l_ref[k]).astype(F32)
        o_ref[...] = acc

    slots = pl.BlockSpec((N_CHIP, tr, c), lambda i: (0, i, 0))
    return pl.pallas_call(
        body,
        name=name,
        out_shape=jax.ShapeDtypeStruct((r, c), F32),
        grid=(r // tr,),
        in_specs=[slots, slots],
        out_specs=pl.BlockSpec((tr, c), lambda i: (i, 0)),
        compiler_params=_cp("parallel"),
    )(land, sent)


def _mesh_pos():
    return lax.axis_index("x"), lax.axis_index("y"), lax.axis_index("c")


def _other_chips(x, y):
    return [(1 - x, y), (x, 1 - y), (1 - x, 1 - y)]


def _all_gather8(x_shard, name):
    m_per, n = x_shard.shape

    def body(x_ref, out_ref, send_sems, recv_sems, local_sem):
        x, y, c = _mesh_pos()
        me, sibling = (x, y, c), (x, y, 1 - c)
        chips = _other_chips(x, y)

        def rows(px, py, pc):
            return out_ref.at[pl.ds((4 * px + 2 * py + pc) * m_per, m_per), :]

        def copy(k, block, to, src=None):
            return pltpu.make_async_remote_copy(
                src_ref=rows(*block) if src is None else src,
                dst_ref=rows(*block),
                send_sem=send_sems.at[k],
                recv_sem=recv_sems.at[k],
                device_id=to,
                device_id_type=MESH,
            )

        mine = pltpu.make_async_copy(x_ref, rows(*me), local_sem)
        mine.start()
        first = [copy(0, me, sibling, src=x_ref)]
        first += [copy(1 + j, me, (*chip, c), src=x_ref) for j, chip in enumerate(chips)]
        for cp in first:
            cp.start()
        passed = [copy(4 + j, (*chip, c), sibling) for j, chip in enumerate(chips)]
        for j, chip in enumerate(chips):
            copy(1 + j, (*chip, c), me).wait_recv()
            passed[j].start()
        copy(0, sibling, me).wait_recv()
        for j, chip in enumerate(chips):
            copy(4 + j, (*chip, 1 - c), me).wait_recv()
        for cp in first + passed:
            cp.wait_send()
        mine.wait()

    return pl.pallas_call(
        body,
        name=name,
        out_shape=jax.ShapeDtypeStruct((N_DEV * m_per, n), x_shard.dtype),
        in_specs=[pl.BlockSpec(memory_space=pltpu.VMEM)],
        out_specs=pl.BlockSpec(memory_space=pltpu.VMEM),
        scratch_shapes=[pltpu.SemaphoreType.DMA((7,)), pltpu.SemaphoreType.DMA((7,)), pltpu.SemaphoreType.DMA],
        compiler_params=pltpu.CompilerParams(vmem_limit_bytes=VMEM_LIMIT),
    )(x_shard)


HBM_SPEC = pl.BlockSpec(memory_space=pltpu.HBM)
SEM_SPEC = pl.BlockSpec(memory_space=pltpu.SEMAPHORE)
DATAFLOW = pltpu.SideEffectType.DATAFLOW_SIDE_EFFECTING


def _exchange_copies(ins, lands, send_sems, recv_sems, scatter):
    x, y, c = _mesh_pos()
    me = 2 * x + y
    sends, recvs = [], []
    for t in range(len(ins)):
        for r, (px, py) in enumerate(_other_chips(x, y)):
            peer = 2 * px + py

            def copy(src, dst, k=3 * t + r, to=(px, py, c)):
                return pltpu.make_async_remote_copy(
                    src_ref=src, dst_ref=dst, send_sem=send_sems.at[k], recv_sem=recv_sems.at[k], device_id=to, device_id_type=MESH
                )

            sends.append(copy(ins[t].at[peer] if scatter else ins[t], lands[t].at[me]))
            recvs.append(copy(ins[t].at[me] if scatter else ins[t], lands[t].at[peer]))
    return sends, recvs


def _exchange_start(arrs, scatter, name):
    nt = len(arrs)
    lands = [lax.empty(a.shape if scatter else (N_CHIP, *a.shape), a.dtype) for a in arrs]

    def body(*refs):
        ins, zones = refs[:nt], refs[nt : 2 * nt]
        send_sems, recv_sems, token = refs[2 * nt], refs[2 * nt + 1], refs[-1]
        sends, _ = _exchange_copies(ins, zones, send_sems, recv_sems, scatter)
        for cp in sends:
            cp.start()
        token[...] = jnp.zeros(token.shape, F32)

    bufs = list(arrs) + list(lands)
    return pl.pallas_call(
        body,
        name=name,
        out_shape=(
            pltpu.SemaphoreType.DMA((3 * nt,)),
            pltpu.SemaphoreType.DMA((3 * nt,)),
            *[pltpu.HBM(a.shape, a.dtype) for a in bufs],
            jax.ShapeDtypeStruct((SUBLANES, LANES), F32),
        ),
        in_specs=[HBM_SPEC] * (2 * nt),
        out_specs=(SEM_SPEC, SEM_SPEC, *[HBM_SPEC] * (2 * nt), pl.BlockSpec(memory_space=pltpu.VMEM)),
        input_output_aliases={k: 2 + k for k in range(2 * nt)},
        compiler_params=pltpu.CompilerParams(has_side_effects=DATAFLOW),
    )(*[pltpu.with_memory_space_constraint(a, pltpu.HBM) for a in bufs])


def _exchange_wait(state, after, scatter, name):
    send_sems, recv_sems, *bufs = state[:-1]
    nt = len(bufs) // 2

    def body(*refs):
        ins, zones = refs[:nt], refs[nt : 2 * nt]
        sends, recvs = _exchange_copies(ins, zones, refs[2 * nt], refs[2 * nt + 1], scatter)
        for cp in sends:
            cp.wait_send()
        for cp in recvs:
            cp.wait_recv()

    out = pl.pallas_call(
        body,
        name=name,
        out_shape=tuple(pltpu.HBM(a.shape, a.dtype) for a in bufs),
        in_specs=[HBM_SPEC] * (2 * nt) + [SEM_SPEC, SEM_SPEC, pl.BlockSpec(memory_space=pl.ANY)],
        out_specs=[HBM_SPEC] * (2 * nt),
        input_output_aliases={k: k for k in range(2 * nt)},
        compiler_params=pltpu.CompilerParams(has_side_effects=DATAFLOW),
    )(*bufs, send_sems, recv_sems, after)
    return out[nt:]


def _sibling_swap(arrs, name):
    nt = len(arrs)

    def body(*refs):
        ins, outs = refs[:nt], refs[nt : 2 * nt]
        send_sems, recv_sems = refs[2 * nt :]
        x, y, c = _mesh_pos()
        cps = [
            pltpu.make_async_remote_copy(
                src_ref=ins[t],
                dst_ref=outs[t],
                send_sem=send_sems.at[t],
                recv_sem=recv_sems.at[t],
                device_id=(x, y, 1 - c),
                device_id_type=MESH,
            )
            for t in range(nt)
        ]
        for cp in cps:
            cp.start()
        for cp in cps:
            cp.wait_recv()
        for cp in cps:
            cp.wait_send()

    return pl.pallas_call(
        body,
        name=name,
        out_shape=tuple(jax.ShapeDtypeStruct(a.shape, a.dtype) for a in arrs),
        in_specs=[pl.BlockSpec(memory_space=pl.ANY)] * nt,
        out_specs=[pl.BlockSpec(memory_space=pl.ANY)] * nt,
        scratch_shapes=[pltpu.SemaphoreType.DMA((nt,)), pltpu.SemaphoreType.DMA((nt,))],
    )(*arrs)


def _cols_from_shards(g):
    _, k, n = g.shape
    return jnp.transpose(g, (1, 0, 2)).reshape(k, N_CHIP * n)


def _cols_to_shards(a):
    k, n4 = a.shape
    return jnp.transpose(a.reshape(k, N_CHIP, n4 // N_CHIP), (1, 0, 2))


def _pad_to(vec, mult):
    n = vec.shape[0]
    return jnp.pad(vec, (0, (-n) % mult))


def kernel(x, c, positions, w_ada, b_ada, g_pre_mix, g_post_mix, w_in, g_q, w_uq, g_kv, w_ukv, conv_w_mix, conv_b_mix, w_o, g_pre_ffn, g_post_ffn, w_up, conv_w_ffn, conv_b_ffn, w_down, loss_target, m_w_ada, m_b_ada, m_g_pre_mix, m_g_post_mix, m_w_in, m_g_q, m_w_uq, m_g_kv, m_w_ukv, m_conv_w_mix, m_conv_b_mix, m_w_o, m_g_pre_ffn, m_g_post_ffn, m_w_up, m_conv_w_ffn, m_conv_b_ffn, m_w_down, v_w_ada, v_b_ada, v_g_pre_mix, v_g_post_mix, v_w_in, v_g_q, v_w_uq, v_g_kv, v_w_ukv, v_conv_w_mix, v_conv_b_mix, v_w_o, v_g_pre_ffn, v_g_post_ffn, v_w_up, v_conv_w_ffn, v_conv_b_ffn, v_w_down):
    xi, yi, ci = _mesh_pos()
    chip = 2 * xi + yi
    dev = 4 * xi + 2 * yi + ci

    s, d = x.shape[1], x.shape[2]
    ql, kl = g_q.shape[1], g_kv.shape[1]
    cwid = conv_b_mix.shape[1]
    f2 = conv_b_ffn.shape[1]
    hh = (w_uq.shape[2] * N_CHIP) // (NOPE + ROPE)
    w_att = hh * LANES
    nc_ada = w_ada.shape[2]
    lat = ql + kl + ROPE
    tc_mix = _tile(cwid, 512, LANES)
    lb = -(-(ql + kl + HEAD_PAD) // tc_mix) * tc_mix
    np_cols = lb + 3 * cwid
    assert cwid == hh * VDIM and w_att % tc_mix == 0

    x0 = x.reshape(s, d)
    tgt = loss_target.reshape(s, d)

    cwm_n, cwf_n = CONV_K * cwid // N_CHIP, CONV_K * f2 // N_CHIP
    pack_a = _pad_to(jnp.concatenate([c.reshape(-1), conv_w_mix.reshape(-1), conv_w_ffn.reshape(-1)]), SUBLANES * LANES)
    rows_a = _all_gather8(pack_a.reshape(SUBLANES, -1), "ag8_inputs").reshape(N_DEV, -1)
    c_all = rows_a[:, :d]
    south = rows_a[0::2]
    cw_mix = jnp.concatenate([south[j, d : d + cwm_n].reshape(CONV_K, -1) for j in range(N_CHIP)], axis=1)
    cw_ffn = jnp.concatenate([south[j, d + cwm_n : d + cwm_n + cwf_n].reshape(CONV_K, -1) for j in range(N_CHIP)], axis=1)

    b_cols = lax.dynamic_slice(b_ada, (0, chip * nc_ada), (1, nc_ada))
    mod_part, c_act = _ada_fwd(c_all, w_ada[0], b_cols)
    mod_rows = _all_gather8(mod_part, "ag8_mod")
    mod = jnp.concatenate(
        [lax.dynamic_slice_in_dim(mod_rows, 2 * N_DEV * j + dev, 1, axis=0) for j in range(N_CHIP)], axis=1
    )

    shards = [a[0].astype(BF16) for a in (w_in, w_uq, w_ukv, w_o, w_up, w_down)]
    shards, mod = lax.optimization_barrier((shards, mod))
    ag_a = _exchange_start(shards[:3], False, "ag_a_start")
    ag_b = _exchange_start(shards[3:], False, "ag_b_start")
    started = ag_a[-1][0:1, 0:1] + ag_b[-1][0:1, 0:1]
    sh_m, sc_m, gt_m, sh_f, sc_f, gt_f = [mod[:, k * d : (k + 1) * d] for k in range(N_MOD)]
    sh_m = sh_m + started

    inv_freq = 1.0 / (ROPE_THETA ** (jnp.arange(0, ROPE, 2, dtype=F32) / ROPE))
    invf = jnp.concatenate([inv_freq, inv_freq, jnp.zeros((LANES - ROPE,), F32)]).reshape(1, LANES)
    tabs = _rope_tables(positions.astype(F32).reshape(s, 1), invf)
    h1 = _pre_fwd(x0, g_pre_mix, sc_m, sh_m)

    def with_own(landed, own):
        return [lax.dynamic_update_slice_in_dim(g, a[None], chip, axis=0) for g, a in zip(landed, own)]

    g_in, g_uq, g_ukv = with_own(_exchange_wait(ag_a, h1, False, "ag_a_wait"), shards[:3])
    full_in = _cols_from_shards(g_in)
    w_in_p = jnp.concatenate([full_in[:, :lat], jnp.zeros((d, lb - lat), BF16), full_in[:, lat:]], axis=1)
    full_uq = _cols_from_shards(g_uq).reshape(ql, hh, NOPE + ROPE)
    w_uq_p = jnp.concatenate(
        [
            full_uq[:, :, :NOPE].reshape(ql, w_att),
            jnp.pad(full_uq[:, :, NOPE:], ((0, 0), (0, 0), (0, HEAD_PAD - ROPE))).reshape(ql, w_att),
        ],
        axis=1,
    )
    full_ukv = _cols_from_shards(g_ukv).reshape(kl, hh, NOPE + VDIM)
    w_ukv_p = jnp.concatenate([full_ukv[:, :, :NOPE].reshape(kl, w_att), full_ukv[:, :, NOPE:].reshape(kl, w_att)], axis=1)

    proj = _matmul(h1, w_in_p, out_dtype=F32, tm=1024, tn=768, tk=2048, name="mm_proj")
    qn, kvn, kr = _latent_fwd(proj, g_q, g_kv, tabs, lb)
    q_f = _matmul(qn, w_uq_p, out_dtype=F32, tm=1024, tn=1024, tk=2048, name="mm_q")
    kv_p = _matmul(kvn, w_ukv_p, out_dtype=BF16, tm=1024, tn=1024, tk=2048, name="mm_kv")
    q_c, k_c = _head_cat(q_f, kv_p, kr, tabs, hh)
    cat, lse2 = _attn_fwd(q_c, k_c, kv_p, hh, w_att + cwid)
    cat = _mixer_fwd(cat, proj, cw_mix, conv_b_mix, lb, w_att)
    g_o, g_up, g_down = with_own(_exchange_wait(ag_b, cat, False, "ag_b_wait"), shards[3:])
    w_o_f = g_o.reshape(-1, d)
    w_up_f = _cols_from_shards(g_up)
    w_down_f = g_down.reshape(-1, d)
    mix = _matmul(cat, w_o_f, out_dtype=F32, tm=1024, tn=1024, tk=2048, name="mm_mix")

    x1, h2 = _mid_fwd(x0, mix, g_post_mix, gt_m, g_pre_ffn, sc_f, sh_f)
    up = _matmul(h2, w_up_f, out_dtype=F32, tm=1024, tn=1408, tk=2048, name="mm_up")
    act = _ffn_act_fwd(up, cw_ffn, conv_b_ffn)
    y = _matmul(act, w_down_f, out_dtype=F32, tm=1024, tn=1024, tk=1408, name="mm_down")
    dx2, dy, s_fin = _final(x1, y, tgt, g_post_ffn, gt_f)

    dw_down = _matmul(act, dy, ta=True, out_dtype=BF16, tm=1408, tn=1024, tk=1024, name="mm_dw_down")
    dact = _matmul(dy, w_down_f, tb=True, out_dtype=F32, tm=1024, tn=1408, tk=2048, name="mm_dact")
    dup_a, dup_g, s_fa, s_fg = _ffn_act_bwd(dact, up, cw_ffn, conv_b_ffn)
    dup = jnp.concatenate([dup_a, dup_g], axis=1)
    s_ffn = jnp.concatenate([s_fa, s_fg], axis=1)
    dw_up = _matmul(h2, dup, ta=True, out_dtype=BF16, tm=1024, tn=1408, tk=1024, name="mm_dw_up")
    dh2 = _matmul(dup, w_up_f, tb=True, out_dtype=F32, tm=1024, tn=1024, tk=1408, name="mm_dh2")
    dx1, dmix, s_mid = _mid_bwd(dh2, dx2, x1, mix, g_pre_ffn, sc_f, g_post_mix, gt_m)

    dw_o = _matmul(cat, dmix, ta=True, out_dtype=BF16, tm=1024, tn=1024, tk=1024, name="mm_dw_o")
    send_b = [dw_o.reshape(N_CHIP, -1, d), _cols_to_shards(dw_up), dw_down.reshape(N_CHIP, -1, d)]
    rs_b = _exchange_start(send_b, True, "rs_b_start")
    dcat = _matmul(dmix, w_o_f, tb=True, out_dtype=F32, tm=1024, tn=1024, tk=2048, name="mm_dcat")
    dp_b, dp_c, dp_i, s_mix = _mixer_bwd(dcat, proj, cw_mix, conv_b_mix + rs_b[-1][0, 0], lb, w_att)
    dob, stats = _attn_bwd_prep(cat, dcat, lse2, hh)
    dq_raw, dkv_k, dkv_v, dkr_h = _attn_bwd(q_c, k_c, kv_p, dob, stats, hh)
    dkv_p = jnp.concatenate([dkv_k, dkv_v], axis=1)
    dq_p = _dq_unrope(dq_raw, tabs, hh)
    dw_uq_p = _matmul(qn, dq_p, ta=True, out_dtype=BF16, tm=1024, tn=1024, tk=1024, name="mm_dw_uq")
    dqn = _matmul(dq_p, w_uq_p, tb=True, out_dtype=F32, tm=1024, tn=1024, tk=2048, name="mm_dqn")
    dw_ukv_p = _matmul(kvn, dkv_p, ta=True, out_dtype=BF16, tm=1024, tn=1024, tk=1024, name="mm_dw_ukv")
    dkvn = _matmul(dkv_p, w_ukv_p, tb=True, out_dtype=F32, tm=1024, tn=1024, tk=2048, name="mm_dkvn")
    dp_lat, s_lat = _latent_bwd(proj, dqn, dkvn, dkr_h, g_q, g_kv, tabs, lb)
    dproj = jnp.concatenate([dp_lat, dp_b, dp_c, dp_i], axis=1)
    dw_in_p = _matmul(h1, dproj, ta=True, out_dtype=BF16, tm=1024, tn=1536, tk=1024, name="mm_dw_in")

    dw_in_f = jnp.concatenate([dw_in_p[:, :lat], dw_in_p[:, lb:]], axis=1)
    uq3 = dw_uq_p.reshape(ql, 2, hh, LANES)
    dw_uq_f = jnp.concatenate([uq3[:, 0], uq3[:, 1, :, :ROPE]], axis=2).reshape(ql, hh * (NOPE + ROPE))
    ukv3 = dw_ukv_p.reshape(kl, 2, hh, LANES)
    dw_ukv_f = jnp.concatenate([ukv3[:, 0], ukv3[:, 1]], axis=2).reshape(kl, hh * (NOPE + VDIM))
    send_a = [_cols_to_shards(dw_in_f), _cols_to_shards(dw_uq_f), _cols_to_shards(dw_ukv_f)]
    rs_a = _exchange_start(send_a, True, "rs_a_start")

    dh1 = _matmul(dproj, w_in_p, tb=True, out_dtype=F32, tm=1024, tn=1024, tk=1536, name="mm_dh1")
    grad_x, s_first = _first_bwd(dh1, dx1, x0, g_pre_mix, sc_m + rs_a[-1][0:1, 0:1])

    names = ["w_in", "w_uq", "w_ukv", "w_o", "w_up", "w_down"]
    landed_b = _exchange_wait(rs_b, s_first, True, "rs_b_wait")
    landed_a = _exchange_wait(rs_a, landed_b[0], True, "rs_a_wait")
    landed_a, s_first = lax.optimization_barrier((list(landed_a), s_first))
    part = [_sum_chips(l, a, "sum_chips_" + n) for l, a, n in zip(list(landed_a) + list(landed_b), send_a + send_b, names)]
    other = _sibling_swap(part, "sibling_swap")

    dmod = jnp.concatenate([s_first[0:1], s_first[1:2], s_mid[3:4], s_mid[0:1], s_mid[1:2], s_fin[0:1]], axis=1)
    small = [
        dmod,
        s_first[2:3],
        s_mid[4:5],
        s_lat[0:1, :ql],
        s_lat[0:1, ql : ql + kl],
        s_mix[3:4],
        s_mid[2:3],
        s_fin[1:2],
        s_ffn[3:4],
        s_mix[0:3].reshape(1, -1),
        s_ffn[0:3].reshape(1, -1),
        s_fin[3:4, :LANES],
    ]
    sizes = [a.shape[1] for a in small]
    offs = [0]
    for n in sizes:
        offs.append(offs[-1] + n)
    pack_g = _pad_to(jnp.concatenate(small, axis=1).reshape(-1), SUBLANES * LANES * SUBLANES).reshape(SUBLANES, -1)
    gathered = _all_gather8(pack_g, "ag8_small_grads")
    tot = _sum_devices(gathered).reshape(-1)
    part_of = lambda k: tot[offs[k] : offs[k + 1]]
    dmod_all = gathered.reshape(N_DEV, -1)[:, : N_MOD * d]
    loss = part_of(11)[0]

    g_b_ada = part_of(0).reshape(1, -1)
    g_vecs = [part_of(k).reshape(1, -1) for k in range(1, 9)]
    g_cw_mix = lax.dynamic_slice(part_of(9).reshape(CONV_K, cwid), (0, chip * (cwid // N_CHIP)), (CONV_K, cwid // N_CHIP))
    g_cw_ffn = lax.dynamic_slice(part_of(10).reshape(CONV_K, f2), (0, chip * (f2 // N_CHIP)), (CONV_K, f2 // N_CHIP))

    dm_cols = lax.dynamic_slice(dmod_all, (0, chip * nc_ada), (N_DEV, nc_ada))
    g_w_ada = _ada_grad(
        jnp.pad(c_act.T, ((0, 0), (0, LANES - N_DEV))), jnp.pad(dm_cols, ((0, LANES - N_DEV), (0, 0)))
    )

    big_w = [w_in, w_uq, w_ukv, w_o, w_up, w_down]
    big_m = [m_w_in, m_w_uq, m_w_ukv, m_w_o, m_w_up, m_w_down]
    big_v = [v_w_in, v_w_uq, v_w_ukv, v_w_o, v_w_up, v_w_down]
    big = {}
    for n, w_, m_, v_, p_, o_ in zip(names, big_w, big_m, big_v, part, other):
        big[n] = [a[None] for a in _adamw(w_[0], m_[0], v_[0], [p_, o_], "adamw_" + n)]
    big["w_ada"] = [a[None] for a in _adamw(w_ada[0], m_w_ada[0], v_w_ada[0], [g_w_ada], "adamw_w_ada")]

    sm_names = ["b_ada", "g_pre_mix", "g_post_mix", "g_q", "g_kv", "conv_b_mix", "g_pre_ffn", "g_post_ffn", "conv_b_ffn",
                "conv_w_mix", "conv_w_ffn"]
    sm_w = [b_ada, g_pre_mix, g_post_mix, g_q, g_kv, conv_b_mix, g_pre_ffn, g_post_ffn, conv_b_ffn, conv_w_mix, conv_w_ffn]
    sm_m = [m_b_ada, m_g_pre_mix, m_g_post_mix, m_g_q, m_g_kv, m_conv_b_mix, m_g_pre_ffn, m_g_post_ffn, m_conv_b_ffn,
            m_conv_w_mix, m_conv_w_ffn]
    sm_v = [v_b_ada, v_g_pre_mix, v_g_post_mix, v_g_q, v_g_kv, v_conv_b_mix, v_g_pre_ffn, v_g_post_ffn, v_conv_b_ffn,
            v_conv_w_mix, v_conv_w_ffn]
    sm_g = [g_b_ada] + g_vecs + [g_cw_mix, g_cw_ffn]
    flat = lambda arrs: jnp.concatenate([a.reshape(1, -1) for a in arrs], axis=1)
    sm_out = _adamw(flat(sm_w), flat(sm_m), flat(sm_v), [flat(sm_g)], "adamw_small")
    sm = {}
    off = 0
    for n, w_ in zip(sm_names, sm_w):
        sm[n] = [o[:, off : off + w_.size].reshape(w_.shape) for o in sm_out]
        off += w_.size

    order = ["w_ada", "b_ada", "g_pre_mix", "g_post_mix", "w_in", "g_q", "w_uq", "g_kv", "w_ukv", "conv_w_mix", "conv_b_mix",
             "w_o", "g_pre_ffn", "g_post_ffn", "w_up", "conv_w_ffn", "conv_b_ffn", "w_down"]
    res = {**big, **sm}
    outs = [loss, grad_x.reshape(x.shape)]
    for k in range(4):
        outs += [res[n][k] for n in order]
    return tuple(outs)
```

```python
import math

import jax
import jax.numpy as jnp
from jax import lax
from jax.experimental import pallas as pl
from jax.experimental.pallas import tpu as pltpu

F32 = jnp.float32
BF16 = jnp.bfloat16
MESH = pl.DeviceIdType.MESH

N_DEV = 8
N_CHIP = 4
LANES = 128
SUBLANES = 8
VMEM_LIMIT = 56 * 2**20

NOPE = 128
ROPE = 64
VDIM = 128
HEAD_PAD = 128
ROPE_THETA = 10000.0
RMS_EPS = 1e-6
N_MOD = 6
CONV_K = 3
ATT_BLOCK = 512
NEG = -1e30

ADAM_LR = 0.001
ADAM_B1 = 0.9
ADAM_B2 = 0.999
ADAM_EPS = 1e-08
ADAM_WD = 0.01
ADAM_STEP = 10


def _tile(n, pref, align):
    if n <= pref:
        return n
    t = (pref // align) * align
    while t >= align:
        if n % t == 0:
            return t
        t -= align
    return n


def _cp(*sem):
    return pltpu.CompilerParams(dimension_semantics=sem, vmem_limit_bytes=VMEM_LIMIT)


def _rsq(x):
    return lax.rsqrt(jnp.mean(x * x, axis=-1, keepdims=True) + RMS_EPS)


def _norm_bwd(dn, n, r):
    return r * (dn - n * jnp.mean(dn * n, axis=-1, keepdims=True))


def _colsum(a):
    return jnp.sum(a, axis=0, keepdims=True)


def _matmul(a, b, *, ta=False, tb=False, out_dtype, tm, tn, tk, name):
    (k_a, m) = a.shape if ta else a.shape[::-1]
    (n, k_b) = b.shape if tb else b.shape[::-1]
    assert k_a == k_b, (a.shape, b.shape, ta, tb)
    tm, tn, tk = _tile(m, tm, LANES), _tile(n, tn, LANES), _tile(k_a, tk, LANES)
    nk = k_a // tk
    a_spec = pl.BlockSpec((tk, tm), lambda i, j, k: (k, i)) if ta else pl.BlockSpec((tm, tk), lambda i, j, k: (i, k))
    b_spec = pl.BlockSpec((tn, tk), lambda i, j, k: (j, k)) if tb else pl.BlockSpec((tk, tn), lambda i, j, k: (k, j))
    dims = (((0 if ta else 1,), (1 if tb else 0,)), ((), ()))

    def body(a_ref, b_ref, o_ref, *acc):
        p = lax.dot_general(a_ref[...].astype(BF16), b_ref[...].astype(BF16), dims, preferred_element_type=F32)
        if nk == 1:
            o_ref[...] = p.astype(o_ref.dtype)
        else:
            k = pl.program_id(2)

            @pl.when(k == 0)
            def _():
                acc[0][...] = p

            @pl.when(k > 0)
            def _():
                acc[0][...] += p

            @pl.when(k == nk - 1)
            def _():
                o_ref[...] = acc[0][...].astype(o_ref.dtype)

    return pl.pallas_call(
        body,
        name=name,
        out_shape=jax.ShapeDtypeStruct((m, n), out_dtype),
        grid=(m // tm, n // tn, nk),
        in_specs=[a_spec, b_spec],
        out_specs=pl.BlockSpec((tm, tn), lambda i, j, k: (i, j)),
        scratch_shapes=[] if nk == 1 else [pltpu.VMEM((tm, tn), F32)],
        compiler_params=_cp("parallel", "parallel", "arbitrary"),
    )(a, b)


def _rope_tables(pos_col, invf):
    s = pos_col.shape[0]
    ts = _tile(s, 1024, SUBLANES)
    half = ROPE // 2

    def body(p_ref, f_ref, c_ref, sa_ref, sb_ref):
        ang = p_ref[...] * f_ref[...]
        lane = lax.broadcasted_iota(jnp.int32, ang.shape, 1)
        cs, sn = jnp.cos(ang), jnp.sin(ang)
        c_ref[...] = jnp.where(lane < ROPE, cs, 0.0)
        sa_ref[...] = jnp.where((lane >= half) & (lane < ROPE), sn, 0.0)
        sb_ref[...] = jnp.where(lane < half, -sn, 0.0)

    tab = jax.ShapeDtypeStruct((s, LANES), F32)
    return pl.pallas_call(
        body,
        name="rope_tables",
        out_shape=(tab, tab, tab),
        grid=(s // ts,),
        in_specs=[pl.BlockSpec((ts, 1), lambda i: (i, 0)), pl.BlockSpec((1, LANES), lambda i: (0, 0))],
        out_specs=[pl.BlockSpec((ts, LANES), lambda i: (i, 0))] * 3,
        compiler_params=_cp("parallel"),
    )(pos_col, invf)


def _widen(t, w):
    return t if w == LANES else jnp.tile(t, (1, w // LANES))


def _rope(x, c, sa, sb):
    w = x.shape[1]
    c, sa, sb = _widen(c, w), _widen(sa, w), _widen(sb, w)
    return x * c + pltpu.roll(x, ROPE // 2, 1) * sa + pltpu.roll(x, w - ROPE // 2, 1) * sb


def _rope_t(d, c, sa, sb):
    w = d.shape[1]
    c, sa, sb = _widen(c, w), _widen(sa, w), _widen(sb, w)
    return d * c + pltpu.roll(d * sa, w - ROPE // 2, 1) + pltpu.roll(d * sb, ROPE // 2, 1)


def _ada_fwd(c_all, w, b):
    d, nc = w.shape
    tn = _tile(nc, 512, LANES)

    def body(c_ref, w_ref, b_ref, o_ref, ca_ref):
        cv = c_ref[...]
        ca = cv * jax.nn.sigmoid(cv)
        ca_ref[...] = ca
        o_ref[...] = jnp.dot(ca.astype(BF16), w_ref[...].astype(BF16), preferred_element_type=F32) + b_ref[...]

    return pl.pallas_call(
        body,
        name="ada_fwd",
        out_shape=(jax.ShapeDtypeStruct((N_DEV, nc), F32), jax.ShapeDtypeStruct((N_DEV, d), F32)),
        grid=(nc // tn,),
        in_specs=[
            pl.BlockSpec((N_DEV, d), lambda j: (0, 0)),
            pl.BlockSpec((d, tn), lambda j: (0, j)),
            pl.BlockSpec((1, tn), lambda j: (0, j)),
        ],
        out_specs=[pl.BlockSpec((N_DEV, tn), lambda j: (0, j)), pl.BlockSpec((N_DEV, d), lambda j: (0, 0))],
        compiler_params=_cp("arbitrary"),
    )(c_all, w, b)


def _rows(ts, d):
    return pl.BlockSpec((ts, d), lambda i: (i, 0))


def _vec(d):
    return pl.BlockSpec((1, d), lambda i: (0, 0))


def _sums(d):
    return pl.BlockSpec((SUBLANES, d), lambda i: (0, 0))


def _acc_rows(ref, i, rows):
    @pl.when(i == 0)
    def _():
        ref[...] = jnp.zeros(ref.shape, ref.dtype)

    for k, r in enumerate(rows):
        ref[k : k + 1, :] += r


def _pre_fwd(x, g, sc, sh):
    s, d = x.shape
    ts = _tile(s, 512, SUBLANES)

    def body(x_ref, g_ref, sc_ref, sh_ref, h_ref):
        xv = x_ref[...]
        h_ref[...] = (((xv * _rsq(xv)) * g_ref[...]) * (1.0 + sc_ref[...]) + sh_ref[...]).astype(BF16)

    return pl.pallas_call(
        body,
        name="pre_mix_fwd",
        out_shape=jax.ShapeDtypeStruct((s, d), BF16),
        grid=(s // ts,),
        in_specs=[_rows(ts, d), _vec(d), _vec(d), _vec(d)],
        out_specs=_rows(ts, d),
        compiler_params=_cp("parallel"),
    )(x, g, sc, sh)


def _mid_fwd(x0, mix, g_post, gt, g_pre, sc, sh):
    s, d = x0.shape
    ts = _tile(s, 256, SUBLANES)

    def body(x_ref, m_ref, gp_ref, gt_ref, g_ref, sc_ref, sh_ref, x1_ref, h_ref):
        mv = m_ref[...]
        x1 = x_ref[...] + gt_ref[...] * ((mv * _rsq(mv)) * gp_ref[...])
        x1_ref[...] = x1
        h_ref[...] = (((x1 * _rsq(x1)) * g_ref[...]) * (1.0 + sc_ref[...]) + sh_ref[...]).astype(BF16)

    return pl.pallas_call(
        body,
        name="mid_fwd",
        out_shape=(jax.ShapeDtypeStruct((s, d), F32), jax.ShapeDtypeStruct((s, d), BF16)),
        grid=(s // ts,),
        in_specs=[_rows(ts, d), _rows(ts, d)] + [_vec(d)] * 5,
        out_specs=[_rows(ts, d), _rows(ts, d)],
        compiler_params=_cp("parallel"),
    )(x0, mix, g_post, gt, g_pre, sc, sh)


def _final(x1, y, tgt, g_post, gt):
    s, d = x1.shape
    ts = _tile(s, 256, SUBLANES)
    ni = s // ts

    def body(x_ref, y_ref, t_ref, gp_ref, gt_ref, dx_ref, dy_ref, s_ref):
        i = pl.program_id(0)
        yv, gp, gt_v = y_ref[...], gp_ref[...], gt_ref[...]
        r = _rsq(yv)
        n = yv * r
        err = (x_ref[...] + gt_v * (n * gp)) - t_ref[...]
        dx = err * (1.0 / d)
        dx_ref[...] = dx
        dy_ref[...] = _norm_bwd(dx * (gt_v * gp), n, r).astype(BF16)
        _acc_rows(s_ref, i, [_colsum(dx * (n * gp)), _colsum(dx * gt_v * n), _colsum(err * err)])

        @pl.when(i == ni - 1)
        def _():
            tot = jnp.sum(s_ref[2:3, :], axis=1, keepdims=True) * (0.5 / d)
            s_ref[3:4, :] = jnp.broadcast_to(tot, (1, d))

    return pl.pallas_call(
        body,
        name="final_fwd_bwd",
        out_shape=(
            jax.ShapeDtypeStruct((s, d), F32),
            jax.ShapeDtypeStruct((s, d), BF16),
            jax.ShapeDtypeStruct((SUBLANES, d), F32),
        ),
        grid=(ni,),
        in_specs=[_rows(ts, d)] * 3 + [_vec(d)] * 2,
        out_specs=[_rows(ts, d), _rows(ts, d), _sums(d)],
        compiler_params=_cp("arbitrary"),
    )(x1, y, tgt, g_post, gt)


def _mid_bwd(dh2, dx2, x1, mix, g_pre, sc, g_post, gt):
    s, d = x1.shape
    ts = _tile(s, 256, SUBLANES)

    def body(dh_ref, dx2_ref, x_ref, m_ref, g_ref, sc_ref, gp_ref, gt_ref, dx1_ref, dm_ref, s_ref):
        i = pl.program_id(0)
        dh, xv, mv = dh_ref[...], x_ref[...], m_ref[...]
        g, sc_v, gp, gt_v = g_ref[...], sc_ref[...], gp_ref[...], gt_ref[...]
        r1 = _rsq(xv)
        n1 = xv * r1
        dx1 = dx2_ref[...] + _norm_bwd(dh * (g * (1.0 + sc_v)), n1, r1)
        dx1_ref[...] = dx1
        rm = _rsq(mv)
        nm = mv * rm
        dm_ref[...] = _norm_bwd(dx1 * (gt_v * gp), nm, rm).astype(BF16)
        _acc_rows(
            s_ref,
            i,
            [
                _colsum(dh),
                _colsum(dh * (n1 * g)),
                _colsum(dh * (1.0 + sc_v) * n1),
                _colsum(dx1 * (nm * gp)),
                _colsum(dx1 * gt_v * nm),
            ],
        )

    return pl.pallas_call(
        body,
        name="mid_bwd",
        out_shape=(
            jax.ShapeDtypeStruct((s, d), F32),
            jax.ShapeDtypeStruct((s, d), BF16),
            jax.ShapeDtypeStruct((SUBLANES, d), F32),
        ),
        grid=(s // ts,),
        in_specs=[_rows(ts, d)] * 4 + [_vec(d)] * 4,
        out_specs=[_rows(ts, d), _rows(ts, d), _sums(d)],
        compiler_params=_cp("arbitrary"),
    )(dh2, dx2, x1, mix, g_pre, sc, g_post, gt)


def _first_bwd(dh1, dx1, x0, g, sc):
    s, d = x0.shape
    ts = _tile(s, 256, SUBLANES)

    def body(dh_ref, dx1_ref, x_ref, g_ref, sc_ref, dx_ref, s_ref):
        i = pl.program_id(0)
        dh, xv, gv, sc_v = dh_ref[...], x_ref[...], g_ref[...], sc_ref[...]
        r = _rsq(xv)
        n = xv * r
        dx_ref[...] = dx1_ref[...] + _norm_bwd(dh * (gv * (1.0 + sc_v)), n, r)
        _acc_rows(s_ref, i, [_colsum(dh), _colsum(dh * (n * gv)), _colsum(dh * (1.0 + sc_v) * n)])

    return pl.pallas_call(
        body,
        name="first_bwd",
        out_shape=(jax.ShapeDtypeStruct((s, d), F32), jax.ShapeDtypeStruct((SUBLANES, d), F32)),
        grid=(s // ts,),
        in_specs=[_rows(ts, d)] * 3 + [_vec(d)] * 2,
        out_specs=[_rows(ts, d), _sums(d)],
        compiler_params=_cp("arbitrary"),
    )(dh1, dx1, x0, g, sc)


def _latent_fwd(proj, g_q, g_kv, tabs, lb):
    s = proj.shape[0]
    ql, kl = g_q.shape[1], g_kv.shape[1]
    ts = _tile(s, 512, SUBLANES)

    def body(p_ref, gq_ref, gk_ref, c_ref, sa_ref, sb_ref, q_ref, kv_ref, kr_ref):
        pv = p_ref[...]
        q, kv, kr = pv[:, :ql], pv[:, ql : ql + kl], pv[:, ql + kl : ql + kl + HEAD_PAD]
        q_ref[...] = ((q * _rsq(q)) * gq_ref[...]).astype(BF16)
        kv_ref[...] = ((kv * _rsq(kv)) * gk_ref[...]).astype(BF16)
        kr_ref[...] = _rope(kr, c_ref[...], sa_ref[...], sb_ref[...]).astype(BF16)

    return pl.pallas_call(
        body,
        name="latent_fwd",
        out_shape=(
            jax.ShapeDtypeStruct((s, ql), BF16),
            jax.ShapeDtypeStruct((s, kl), BF16),
            jax.ShapeDtypeStruct((s, HEAD_PAD), BF16),
        ),
        grid=(s // ts,),
        in_specs=[_rows(ts, lb), _vec(ql), _vec(kl)] + [_rows(ts, LANES)] * 3,
        out_specs=[_rows(ts, ql), _rows(ts, kl), _rows(ts, HEAD_PAD)],
        compiler_params=_cp("parallel"),
    )(proj, g_q, g_kv, *tabs)


def _latent_bwd(proj, dqn, dkvn, dkr_h, g_q, g_kv, tabs, lb):
    s = proj.shape[0]
    ql, kl = g_q.shape[1], g_kv.shape[1]
    hw = dkr_h.shape[1]
    ts = _tile(s, 256, SUBLANES)
    pad = lb - ql - kl - HEAD_PAD

    def body(p_ref, dq_ref, dkv_ref, dkr_ref, gq_ref, gk_ref, c_ref, sa_ref, sb_ref, o_ref, s_ref):
        i = pl.program_id(0)
        pv = p_ref[...]
        q, kv = pv[:, :ql], pv[:, ql : ql + kl]
        dqn_v, dkvn_v = dq_ref[...], dkv_ref[...]
        rq = _rsq(q)
        nq = q * rq
        rk = _rsq(kv)
        nk = kv * rk
        dkr = dkr_ref[:, :HEAD_PAD]
        for h in range(1, hw // HEAD_PAD):
            dkr = dkr + dkr_ref[:, h * HEAD_PAD : (h + 1) * HEAD_PAD]
        parts = [
            _norm_bwd(dqn_v * gq_ref[...], nq, rq).astype(BF16),
            _norm_bwd(dkvn_v * gk_ref[...], nk, rk).astype(BF16),
            _rope_t(dkr, c_ref[...], sa_ref[...], sb_ref[...]).astype(BF16),
        ]
        if pad:
            parts.append(jnp.zeros((ts, pad), BF16))
        o_ref[...] = jnp.concatenate(parts, axis=1)
        row = [_colsum(dqn_v * nq), _colsum(dkvn_v * nk), jnp.zeros((1, lb - ql - kl), F32)]
        _acc_rows(s_ref, i, [jnp.concatenate(row, axis=1)])

    return pl.pallas_call(
        body,
        name="latent_bwd",
        out_shape=(jax.ShapeDtypeStruct((s, lb), BF16), jax.ShapeDtypeStruct((SUBLANES, lb), F32)),
        grid=(s // ts,),
        in_specs=[_rows(ts, lb), _rows(ts, ql), _rows(ts, kl), _rows(ts, hw)]
        + [_vec(ql), _vec(kl)]
        + [_rows(ts, LANES)] * 3,
        out_specs=[_rows(ts, lb), _sums(lb)],
        compiler_params=_cp("arbitrary"),
    )(proj, dqn, dkvn, dkr_h, g_q, g_kv, *tabs)


def _conv3(ext, w, b):
    return (pltpu.roll(ext, 2, 0) * w[0:1] + pltpu.roll(ext, 1, 0) * w[1:2]) + ext * w[2:3] + b


def _conv3_t(du, w):
    n = du.shape[0]
    return du * w[2:3] + pltpu.roll(du, n - 1, 0) * w[1:2] + pltpu.roll(du, n - 2, 0) * w[0:1]


def _halo_maps(ts, s):
    r8, last = ts // SUBLANES, s // SUBLANES - 1
    prev = lambda i: jnp.maximum(i * r8 - 1, 0)
    nxt = lambda i: jnp.minimum((i + 1) * r8, last)
    return prev, nxt


def _mixer_fwd(cat, proj, cw, cb, lb, col0):
    s = proj.shape[0]
    cwid = cw.shape[1]
    ts = _tile(s, 512, SUBLANES)
    tc = _tile(cwid, 512, LANES)
    assert lb % tc == 0 and col0 % tc == 0
    nj, ob, oc = cwid // tc, lb // tc, col0 // tc
    prev, _ = _halo_maps(ts, s)

    def body(_, gb_ref, gc_ref, ci_ref, pgc_ref, pci_ref, w_ref, b_ref, o_ref):
        keep = jnp.where(pl.program_id(1) > 0, 1.0, 0.0)
        ext = jnp.concatenate([pgc_ref[...] * pci_ref[...] * keep, gc_ref[...] * ci_ref[...]], axis=0)
        o_ref[...] = (gb_ref[...] * _conv3(ext, w_ref[...], b_ref[...])[SUBLANES:]).astype(BF16)

    def col(k):
        return pl.BlockSpec((ts, tc), lambda j, i: (i, ob + k * nj + j))

    def halo(k):
        return pl.BlockSpec((SUBLANES, tc), lambda j, i: (prev(i), ob + k * nj + j))

    return pl.pallas_call(
        body,
        name="mixer_fwd",
        out_shape=jax.ShapeDtypeStruct(cat.shape, BF16),
        grid=(nj, s // ts),
        in_specs=[pl.BlockSpec(memory_space=pl.ANY), col(0), col(1), col(2), halo(1), halo(2)]
        + [pl.BlockSpec((CONV_K, tc), lambda j, i: (0, j)), pl.BlockSpec((1, tc), lambda j, i: (0, j))],
        out_specs=pl.BlockSpec((ts, tc), lambda j, i: (i, oc + j)),
        input_output_aliases={0: 0},
        compiler_params=_cp("parallel", "arbitrary"),
    )(cat, proj, proj, proj, proj, proj, cw, cb)


def _mixer_bwd(dcat, proj, cw, cb, lb, col0):
    s = proj.shape[0]
    cwid = cw.shape[1]
    ts = _tile(s, 256, SUBLANES)
    tc = _tile(cwid, 512, LANES)
    nj, ob, oc = cwid // tc, lb // tc, col0 // tc
    ni = s // ts
    prev, nxt = _halo_maps(ts, s)

    def body(d_ref, dn_ref, gb_ref, gbn_ref, gc_ref, gcp_ref, gcn_ref, ci_ref, cip_ref, cin_ref, w_ref, b_ref,
             dgb_ref, dgc_ref, dci_ref, s_ref):
        i = pl.program_id(1)
        keep_p = jnp.where(i > 0, 1.0, 0.0)
        keep_n = jnp.where(i < ni - 1, 1.0, 0.0)
        w = w_ref[...]
        gc = jnp.concatenate([gcp_ref[...], gc_ref[...], gcn_ref[...]], axis=0)
        ci = jnp.concatenate([cip_ref[...] * keep_p, ci_ref[...], cin_ref[...]], axis=0)
        u = gc * ci
        cv = _conv3(u, w, b_ref[...])[SUBLANES:]
        dco = jnp.concatenate([d_ref[...], dn_ref[...] * keep_n], axis=0)
        gb = jnp.concatenate([gb_ref[...], gbn_ref[...]], axis=0)
        dgb_ref[...] = (dco * cv)[:ts].astype(BF16)
        dcv = dco * gb
        du = _conv3_t(dcv, w)[:ts]
        dgc_ref[...] = (du * ci_ref[...]).astype(BF16)
        dci_ref[...] = (du * gc_ref[...]).astype(BF16)
        dt = dcv[:ts]
        u1, u2 = pltpu.roll(u, 1, 0), pltpu.roll(u, 2, 0)
        lo, hi = SUBLANES, SUBLANES + ts
        _acc_rows(s_ref, i, [_colsum(dt * u2[lo:hi]), _colsum(dt * u1[lo:hi]), _colsum(dt * u[lo:hi]), _colsum(dt)])

    def col(k):
        return pl.BlockSpec((ts, tc), lambda j, i: (i, ob + k * nj + j))

    def halo(k, which):
        return pl.BlockSpec((SUBLANES, tc), lambda j, i: (which(i), ob + k * nj + j))

    out_col = [pl.BlockSpec((ts, tc), lambda j, i: (i, j))] * 3
    grad = jax.ShapeDtypeStruct((s, cwid), BF16)
    return pl.pallas_call(
        body,
        name="mixer_bwd",
        out_shape=(grad, grad, grad, jax.ShapeDtypeStruct((SUBLANES, cwid), F32)),
        grid=(nj, ni),
        in_specs=[
            pl.BlockSpec((ts, tc), lambda j, i: (i, oc + j)),
            pl.BlockSpec((SUBLANES, tc), lambda j, i: (nxt(i), oc + j)),
            col(0), halo(0, nxt),
            col(1), halo(1, prev), halo(1, nxt),
            col(2), halo(2, prev), halo(2, nxt),
            pl.BlockSpec((CONV_K, tc), lambda j, i: (0, j)),
            pl.BlockSpec((1, tc), lambda j, i: (0, j)),
        ],
        out_specs=out_col + [pl.BlockSpec((SUBLANES, tc), lambda j, i: (0, j))],
        compiler_params=_cp("parallel", "arbitrary"),
    )(dcat, dcat, proj, proj, proj, proj, proj, proj, proj, proj, cw, cb)


def _ffn_act_fwd(up, cw, cb):
    s, f2 = up.shape
    f = f2 // 2
    ts = _tile(s, 512, SUBLANES)
    tc = _tile(f, 512, LANES)
    nj = f // tc
    prev, _ = _halo_maps(ts, s)

    def body(ua_ref, ug_ref, pa_ref, pg_ref, wa_ref, wg_ref, ba_ref, bg_ref, o_ref):
        keep = jnp.where(pl.program_id(1) > 0, 1.0, 0.0)

        def conv(u_ref, p_ref, w_ref, b_ref):
            ext = jnp.concatenate([p_ref[...] * keep, u_ref[...]], axis=0)
            return _conv3(ext, w_ref[...], b_ref[...])[SUBLANES:]

        a = conv(ua_ref, pa_ref, wa_ref, ba_ref)
        g = conv(ug_ref, pg_ref, wg_ref, bg_ref)
        o_ref[...] = ((g * jax.nn.sigmoid(g)) * a).astype(BF16)

    def col(k):
        return pl.BlockSpec((ts, tc), lambda j, i: (i, k * nj + j))

    def halo(k):
        return pl.BlockSpec((SUBLANES, tc), lambda j, i: (prev(i), k * nj + j))

    def wspec(rows, k):
        return pl.BlockSpec((rows, tc), lambda j, i: (0, k * nj + j))

    return pl.pallas_call(
        body,
        name="ffn_act_fwd",
        out_shape=jax.ShapeDtypeStruct((s, f), BF16),
        grid=(nj, s // ts),
        in_specs=[col(0), col(1), halo(0), halo(1), wspec(CONV_K, 0), wspec(CONV_K, 1), wspec(1, 0), wspec(1, 1)],
        out_specs=pl.BlockSpec((ts, tc), lambda j, i: (i, j)),
        compiler_params=_cp("parallel", "arbitrary"),
    )(up, up, up, up, cw, cw, cb, cb)


def _ffn_act_bwd(dact, up, cw, cb):
    s, f2 = up.shape
    f = f2 // 2
    ts = _tile(s, 256, SUBLANES)
    tc = _tile(f, 512, LANES)
    nj, ni = f // tc, s // ts
    prev, nxt = _halo_maps(ts, s)

    def body(d_ref, dn_ref, ua_ref, uap_ref, uan_ref, ug_ref, ugp_ref, ugn_ref, wa_ref, wg_ref, ba_ref, bg_ref,
             dua_ref, dug_ref, sa_ref, sg_ref):
        i = pl.program_id(1)
        keep_p = jnp.where(i > 0, 1.0, 0.0)
        keep_n = jnp.where(i < ni - 1, 1.0, 0.0)
        wa, wg = wa_ref[...], wg_ref[...]
        exta = jnp.concatenate([uap_ref[...] * keep_p, ua_ref[...], uan_ref[...]], axis=0)
        extg = jnp.concatenate([ugp_ref[...] * keep_p, ug_ref[...], ugn_ref[...]], axis=0)
        a = _conv3(exta, wa, ba_ref[...])[SUBLANES:]
        g = _conv3(extg, wg, bg_ref[...])[SUBLANES:]
        dact_v = jnp.concatenate([d_ref[...], dn_ref[...] * keep_n], axis=0)
        sg = jax.nn.sigmoid(g)
        da = dact_v * (g * sg)
        dg = dact_v * a * (sg * (1.0 + g * (1.0 - sg)))
        lo, hi = SUBLANES, SUBLANES + ts

        def back(du, ext, w, dup_ref, s_ref):
            dup_ref[...] = _conv3_t(du, w)[:ts].astype(BF16)
            dt = du[:ts]
            e1, e2 = pltpu.roll(ext, 1, 0), pltpu.roll(ext, 2, 0)
            _acc_rows(s_ref, i, [_colsum(dt * e2[lo:hi]), _colsum(dt * e1[lo:hi]), _colsum(dt * ext[lo:hi]), _colsum(dt)])

        back(da, exta, wa, dua_ref, sa_ref)
        back(dg, extg, wg, dug_ref, sg_ref)

    def col(k):
        return pl.BlockSpec((ts, tc), lambda j, i: (i, k * nj + j))

    def halo(k, which):
        return pl.BlockSpec((SUBLANES, tc), lambda j, i: (which(i), k * nj + j))

    def wspec(rows, k):
        return pl.BlockSpec((rows, tc), lambda j, i: (0, k * nj + j))

    half = pl.BlockSpec((ts, tc), lambda j, i: (i, j))
    half_sums = pl.BlockSpec((SUBLANES, tc), lambda j, i: (0, j))
    return pl.pallas_call(
        body,
        name="ffn_act_bwd",
        out_shape=(
            jax.ShapeDtypeStruct((s, f), BF16),
            jax.ShapeDtypeStruct((s, f), BF16),
            jax.ShapeDtypeStruct((SUBLANES, f), F32),
            jax.ShapeDtypeStruct((SUBLANES, f), F32),
        ),
        grid=(nj, ni),
        in_specs=[
            pl.BlockSpec((ts, tc), lambda j, i: (i, j)),
            pl.BlockSpec((SUBLANES, tc), lambda j, i: (nxt(i), j)),
            col(0), halo(0, prev), halo(0, nxt),
            col(1), halo(1, prev), halo(1, nxt),
            wspec(CONV_K, 0), wspec(CONV_K, 1), wspec(1, 0), wspec(1, 1),
        ],
        out_specs=[half, half, half_sums, half_sums],
        compiler_params=_cp("parallel", "arbitrary"),
    )(dact, dact, up, up, up, up, up, up, cw, cw, cb, cb)


ATT_SCALE = 1.0 / math.sqrt(NOPE + ROPE)
LOG2E = math.log2(math.e)
ATT_C2 = ATT_SCALE * LOG2E
ATT_SUB = 256
STAT_SPLIT = 64
NT = (((1,), (1,)), ((), ()))
TN = (((0,), (0,)), ((), ()))


def _head_cat(q, kv, kr, tabs, n_heads):
    s, w2 = q.shape
    w = w2 // 2
    ts = _tile(s, 512, SUBLANES)
    hd = NOPE + HEAD_PAD

    def body(q_ref, kv_ref, kr_ref, c_ref, sa_ref, sb_ref, qc_ref, kc_ref):
        qv = q_ref[...]
        qr = _rope(qv[:, w:], c_ref[...], sa_ref[...], sb_ref[...]).astype(BF16)
        krv = kr_ref[...]
        for h in range(n_heads):
            qc_ref[:, h * hd : h * hd + NOPE] = qv[:, h * NOPE : (h + 1) * NOPE].astype(BF16)
            qc_ref[:, h * hd + NOPE : (h + 1) * hd] = qr[:, h * HEAD_PAD : (h + 1) * HEAD_PAD]
            kc_ref[:, h * hd : h * hd + NOPE] = kv_ref[:, h * NOPE : (h + 1) * NOPE]
            kc_ref[:, h * hd + NOPE : (h + 1) * hd] = krv

    out = jax.ShapeDtypeStruct((s, n_heads * hd), BF16)
    return pl.pallas_call(
        body,
        name="head_cat",
        out_shape=(out, out),
        grid=(s // ts,),
        in_specs=[_rows(ts, w2), _rows(ts, w), _rows(ts, HEAD_PAD)] + [_rows(ts, LANES)] * 3,
        out_specs=[_rows(ts, n_heads * hd)] * 2,
        compiler_params=_cp("parallel"),
    )(q, kv, kr, *tabs)


def _attn_fwd(qc, kc, kv, n_heads, cat_cols):
    s = qc.shape[0]
    t = _tile(s, ATT_BLOCK, LANES)
    sub = _tile(t, ATT_SUB, LANES)
    hh = n_heads
    hd = NOPE + HEAD_PAD

    def body(q_ref, k_ref, v_ref, o_ref, lse_ref, m_s, l_s, acc_s):
        i = pl.program_id(1)
        m_s[...] = jnp.full(m_s.shape, NEG, F32)
        l_s[...] = jnp.zeros(l_s.shape, F32)
        acc_s[...] = jnp.zeros(acc_s.shape, F32)

        def chunk(k0, diag):
            m_all, l_all, acc_all = m_s[...], l_s[...], acc_s[...]
            new_m, new_l, new_acc = [], [], []
            for r0 in range(0, t, sub):
                ncol = r0 + sub if diag else t
                kk = k_ref[pl.ds(k0, ncol), :]
                sc = lax.dot_general(q_ref[pl.ds(r0, sub), :], kk, NT, preferred_element_type=F32)
                if diag:
                    row = lax.broadcasted_iota(jnp.int32, sc.shape, 0) + r0
                    col = lax.broadcasted_iota(jnp.int32, sc.shape, 1)
                    sc = jnp.where(col <= row, sc, NEG)
                m_prev = m_all[r0 : r0 + sub]
                m_new = jnp.maximum(m_prev, jnp.max(sc, axis=1, keepdims=True))
                alpha = jnp.exp2((m_prev - m_new) * ATT_C2)
                p = jnp.exp2((sc - m_new) * ATT_C2)
                pv = jnp.dot(p.astype(BF16), v_ref[pl.ds(k0, ncol), :], preferred_element_type=F32)
                new_m.append(m_new)
                new_l.append(alpha * l_all[r0 : r0 + sub] + jnp.sum(p, axis=1, keepdims=True))
                new_acc.append(alpha * acc_all[r0 : r0 + sub] + pv)
            m_s[...] = jnp.concatenate(new_m, axis=0)
            l_s[...] = jnp.concatenate(new_l, axis=0)
            acc_s[...] = jnp.concatenate(new_acc, axis=0)

        def loop_body(k, carry):
            chunk(pl.multiple_of(k * t, t), False)
            return carry

        lax.fori_loop(0, i, loop_body, 0)
        chunk(pl.multiple_of(i * t, t), True)
        l = l_s[...]
        o_ref[...] = (acc_s[...] / l).astype(BF16)
        lse_ref[...] = jnp.broadcast_to(m_s[...] * ATT_C2 + jnp.log(l) * LOG2E, lse_ref.shape)

    return pl.pallas_call(
        body,
        name="attn_fwd",
        out_shape=(jax.ShapeDtypeStruct((s, cat_cols), BF16), jax.ShapeDtypeStruct((s, hh * LANES), F32)),
        grid=(hh, s // t),
        in_specs=[
            pl.BlockSpec((t, hd), lambda h, i: (i, h)),
            pl.BlockSpec((s, hd), lambda h, i: (0, h)),
            pl.BlockSpec((s, VDIM), lambda h, i: (0, hh + h)),
        ],
        out_specs=[pl.BlockSpec((t, VDIM), lambda h, i: (i, h)), pl.BlockSpec((t, LANES), lambda h, i: (i, h))],
        scratch_shapes=[pltpu.VMEM((t, 1), F32), pltpu.VMEM((t, 1), F32), pltpu.VMEM((t, VDIM), F32)],
        compiler_params=_cp("parallel", "parallel"),
    )(qc, kc, kv)


def _attn_bwd_prep(cat, dcat, lse2, n_heads):
    s, w = lse2.shape
    ts = _tile(s, 512, SUBLANES)

    def body(o_ref, do_ref, lse_ref, dob_ref, st_ref):
        do = do_ref[...]
        dob_ref[...] = do.astype(BF16)
        prod = do * o_ref[...].astype(F32)
        lane = lax.broadcasted_iota(jnp.int32, (ts, LANES), 1)
        for h in range(n_heads):
            cols = slice(h * LANES, (h + 1) * LANES)
            dsum = jnp.sum(prod[:, cols], axis=1, keepdims=True)
            st_ref[:, cols] = jnp.where(lane < STAT_SPLIT, lse_ref[:, cols], dsum)

    return pl.pallas_call(
        body,
        name="attn_bwd_prep",
        out_shape=(jax.ShapeDtypeStruct((s, w), BF16), jax.ShapeDtypeStruct((s, w), F32)),
        grid=(s // ts,),
        in_specs=[_rows(ts, w)] * 3,
        out_specs=[_rows(ts, w)] * 2,
        compiler_params=_cp("parallel"),
    )(cat, dcat, lse2)


def _attn_bwd(qc, kc, kv, dob, stats, n_heads):
    s = qc.shape[0]
    t = _tile(s, ATT_BLOCK, LANES)
    sub = _tile(t, ATT_SUB, LANES)
    nb = s // t
    hh = n_heads
    hd = NOPE + HEAD_PAD
    w = hh * LANES

    def body(q_ref, k_ref, v_ref, do_ref, st_ref, dq_ref, dkn_ref, dv_ref, dkr_ref, dk_s, dv_s):
        j = pl.program_id(1)

        @pl.when(j == 0)
        def _():
            dq_ref[...] = jnp.zeros(dq_ref.shape, F32)

        dk_s[...] = jnp.zeros(dk_s.shape, F32)
        dv_s[...] = jnp.zeros(dv_s.shape, F32)
        kk, vv = k_ref[...], v_ref[...]

        def pair(i0, diag):
            for r0 in range(0, t, sub):
                rows = pl.ds(i0 + r0, sub)
                qq, do, st = q_ref[rows, :], do_ref[rows, :], st_ref[rows, :]
                sc = lax.dot_general(qq, kk, NT, preferred_element_type=F32)
                if diag:
                    row = lax.broadcasted_iota(jnp.int32, sc.shape, 0) + r0
                    col = lax.broadcasted_iota(jnp.int32, sc.shape, 1)
                    sc = jnp.where(col <= row, sc, NEG)
                p = jnp.exp2(sc * ATT_C2 - st[:, 0:1])
                dv_s[...] += lax.dot_general(p.astype(BF16), do, TN, preferred_element_type=F32)
                dp = lax.dot_general(do, vv, NT, preferred_element_type=F32)
                ds = (p * (dp - st[:, STAT_SPLIT : STAT_SPLIT + 1]) * ATT_SCALE).astype(BF16)
                dk_s[...] += lax.dot_general(ds, qq, TN, preferred_element_type=F32)
                dq_ref[rows, :] += jnp.dot(ds, kk, preferred_element_type=F32)

        pair(pl.multiple_of(j * t, t), True)

        def loop_body(i, carry):
            pair(pl.multiple_of(i * t, t), False)
            return carry

        lax.fori_loop(j + 1, nb, loop_body, 0)
        dkn_ref[...] = dk_s[:, :NOPE].astype(BF16)
        dv_ref[...] = dv_s[...].astype(BF16)
        dkr_ref[...] = dk_s[:, NOPE:]

    whole = lambda width, off: pl.BlockSpec((s, width), lambda h, j: (0, off + h))
    blk = lambda width, off: pl.BlockSpec((t, width), lambda h, j: (j, off + h))
    return pl.pallas_call(
        body,
        name="attn_bwd",
        out_shape=(
            jax.ShapeDtypeStruct((s, hh * hd), F32),
            jax.ShapeDtypeStruct((s, w), BF16),
            jax.ShapeDtypeStruct((s, w), BF16),
            jax.ShapeDtypeStruct((s, w), F32),
        ),
        grid=(hh, nb),
        in_specs=[whole(hd, 0), blk(hd, 0), blk(VDIM, hh), whole(VDIM, 0), whole(LANES, 0)],
        out_specs=[whole(hd, 0), blk(NOPE, 0), blk(VDIM, 0), blk(HEAD_PAD, 0)],
        scratch_shapes=[pltpu.VMEM((t, hd), F32), pltpu.VMEM((t, VDIM), F32)],
        compiler_params=_cp("parallel", "arbitrary"),
    )(qc, kc, kv, dob, stats)


def _dq_unrope(dq, tabs, n_heads):
    s = dq.shape[0]
    hd = NOPE + HEAD_PAD
    w = n_heads * LANES
    ts = _tile(s, 512, SUBLANES)

    def body(d_ref, c_ref, sa_ref, sb_ref, o_ref):
        c, sa, sb = c_ref[...], sa_ref[...], sb_ref[...]
        for h in range(n_heads):
            o_ref[:, h * NOPE : (h + 1) * NOPE] = d_ref[:, h * hd : h * hd + NOPE].astype(BF16)
            rot = _rope_t(d_ref[:, h * hd + NOPE : (h + 1) * hd], c, sa, sb)
            o_ref[:, w + h * HEAD_PAD : w + (h + 1) * HEAD_PAD] = rot.astype(BF16)

    return pl.pallas_call(
        body,
        name="dq_unrope",
        out_shape=jax.ShapeDtypeStruct((s, 2 * w), BF16),
        grid=(s // ts,),
        in_specs=[_rows(ts, n_heads * hd)] + [_rows(ts, LANES)] * 3,
        out_specs=_rows(ts, 2 * w),
        compiler_params=_cp("parallel"),
    )(dq, *tabs)


def _adamw(w, m, v, grads, name):
    r, c = w.shape
    budget_rows = max(SUBLANES, (VMEM_LIMIT // 3) // (4 * c * 2 * (7 + len(grads))))
    tr = _tile(r, budget_rows, SUBLANES)
    ng = len(grads)
    c1 = 1.0 - ADAM_B1**ADAM_STEP
    c2 = 1.0 - ADAM_B2**ADAM_STEP

    def body(*refs):
        w_ref, m_ref, v_ref = refs[:3]
        g_ref, d_ref, nm_ref, nv_ref = refs[3 + ng :]
        g = refs[3][...]
        for extra in refs[4 : 3 + ng]:
            g = g + extra[...]
        mn = ADAM_B1 * m_ref[...] + (1.0 - ADAM_B1) * g
        vn = ADAM_B2 * v_ref[...] + (1.0 - ADAM_B2) * (g * g)
        g_ref[...] = g
        nm_ref[...] = mn
        nv_ref[...] = vn
        d_ref[...] = -ADAM_LR * ((mn / c1) / (jnp.sqrt(vn / c2) + ADAM_EPS) + ADAM_WD * w_ref[...])

    blk = pl.BlockSpec((tr, c), lambda i: (i, 0))
    out = jax.ShapeDtypeStruct((r, c), F32)
    return pl.pallas_call(
        body,
        name=name,
        out_shape=(out, out, out, out),
        grid=(r // tr,),
        in_specs=[blk] * (3 + ng),
        out_specs=[blk] * 4,
        compiler_params=_cp("parallel"),
    )(w, m, v, *grads)


def _ada_grad(ca_t, dm):
    d = ca_t.shape[0]
    nc = dm.shape[1]
    tn = _tile(nc, 512, LANES)

    def body(a_ref, b_ref, o_ref):
        o_ref[...] = jnp.dot(a_ref[...].astype(BF16), b_ref[...].astype(BF16), preferred_element_type=F32)

    return pl.pallas_call(
        body,
        name="ada_grad",
        out_shape=jax.ShapeDtypeStruct((d, nc), F32),
        grid=(nc // tn,),
        in_specs=[pl.BlockSpec((d, LANES), lambda j: (0, 0)), pl.BlockSpec((LANES, tn), lambda j: (0, j))],
        out_specs=pl.BlockSpec((d, tn), lambda j: (0, j)),
        compiler_params=_cp("parallel"),
    )(ca_t, dm)


def _sum_devices(g):
    n = g.shape[1]

    def body(g_ref, o_ref):
        acc = g_ref[0:SUBLANES, :]
        for dvc in range(1, N_DEV):
            acc = acc + g_ref[dvc * SUBLANES : (dvc + 1) * SUBLANES, :]
        o_ref[...] = acc

    return pl.pallas_call(
        body,
        name="sum_devices",
        out_shape=jax.ShapeDtypeStruct((SUBLANES, n), F32),
        in_specs=[pl.BlockSpec(memory_space=pltpu.VMEM)],
        out_specs=pl.BlockSpec(memory_space=pltpu.VMEM),
        compiler_params=pltpu.CompilerParams(vmem_limit_bytes=VMEM_LIMIT),
    )(g)


def _sum_chips(land, sent, name):
    _, r, c = land.shape
    tr = _tile(r, max(SUBLANES * 2, (VMEM_LIMIT // 4) // (c * 2 * (4 * N_CHIP + 4 * 2))), SUBLANES * 2)

    def body(l_ref, s_ref, o_ref):
        x, y, _ = _mesh_pos()
        me = 2 * x + y
        acc = jnp.where(me == 0, s_ref[0], l_ref[0]).astype(F32)
        for k in range(1, N_CHIP):
            acc = acc + jnp.where(me == k, s_ref[k], l_ref[k]).astype(F32)
        o_ref[...] = acc

    slots = pl.BlockSpec((N_CHIP, tr, c), lambda i: (0, i, 0))
    return pl.pallas_call(
        body,
        name=name,
        out_shape=jax.ShapeDtypeStruct((r, c), F32),
        grid=(r // tr,),
        in_specs=[slots, slots],
        out_specs=pl.BlockSpec((tr, c), lambda i: (i, 0)),
        compiler_params=_cp("parallel"),
    )(land, sent)


def _mesh_pos():
    return lax.axis_index("x"), lax.axis_index("y"), lax.axis_index("c")


def _other_chips(x, y):
    return [(1 - x, y), (x, 1 - y), (1 - x, 1 - y)]


def _all_gather8(x_shard, name):
    m_per, n = x_shard.shape

    def body(x_ref, out_ref, send_sems, recv_sems, local_sem):
        x, y, c = _mesh_pos()
        me, sibling = (x, y, c), (x, y, 1 - c)
        chips = _other_chips(x, y)

        def rows(px, py, pc):
            return out_ref.at[pl.ds((4 * px + 2 * py + pc) * m_per, m_per), :]

        def copy(k, block, to, src=None):
            return pltpu.make_async_remote_copy(
                src_ref=rows(*block) if src is None else src,
                dst_ref=rows(*block),
                send_sem=send_sems.at[k],
                recv_sem=recv_sems.at[k],
                device_id=to,
                device_id_type=MESH,
            )

        mine = pltpu.make_async_copy(x_ref, rows(*me), local_sem)
        mine.start()
        first = [copy(0, me, sibling, src=x_ref)]
        first += [copy(1 + j, me, (*chip, c), src=x_ref) for j, chip in enumerate(chips)]
        for cp in first:
            cp.start()
        passed = [copy(4 + j, (*chip, c), sibling) for j, chip in enumerate(chips)]
        for j, chip in enumerate(chips):
            copy(1 + j, (*chip, c), me).wait_recv()
            passed[j].start()
        copy(0, sibling, me).wait_recv()
        for j, chip in enumerate(chips):
            copy(4 + j, (*chip, 1 - c), me).wait_recv()
        for cp in first + passed:
            cp.wait_send()
        mine.wait()

    return pl.pallas_call(
        body,
        name=name,
        out_shape=jax.ShapeDtypeStruct((N_DEV * m_per, n), x_shard.dtype),
        in_specs=[pl.BlockSpec(memory_space=pltpu.VMEM)],
        out_specs=pl.BlockSpec(memory_space=pltpu.VMEM),
        scratch_shapes=[pltpu.SemaphoreType.DMA((7,)), pltpu.SemaphoreType.DMA((7,)), pltpu.SemaphoreType.DMA],
        compiler_params=pltpu.CompilerParams(vmem_limit_bytes=VMEM_LIMIT),
    )(x_shard)


HBM_SPEC = pl.BlockSpec(memory_space=pltpu.HBM)
SEM_SPEC = pl.BlockSpec(memory_space=pltpu.SEMAPHORE)
DATAFLOW = pltpu.SideEffectType.DATAFLOW_SIDE_EFFECTING


def _exchange_copies(ins, lands, send_sems, recv_sems, scatter):
    x, y, c = _mesh_pos()
    me = 2 * x + y
    sends, recvs = [], []
    for t in range(len(ins)):
        for r, (px, py) in enumerate(_other_chips(x, y)):
            peer = 2 * px + py

            def copy(src, dst, k=3 * t + r, to=(px, py, c)):
                return pltpu.make_async_remote_copy(
                    src_ref=src, dst_ref=dst, send_sem=send_sems.at[k], recv_sem=recv_sems.at[k], device_id=to, device_id_type=MESH
                )

            sends.append(copy(ins[t].at[peer] if scatter else ins[t], lands[t].at[me]))
            recvs.append(copy(ins[t].at[me] if scatter else ins[t], lands[t].at[peer]))
    return sends, recvs


def _exchange_start(arrs, scatter, name):
    nt = len(arrs)
    lands = [lax.empty(a.shape if scatter else (N_CHIP, *a.shape), a.dtype) for a in arrs]

    def body(*refs):
        ins, zones = refs[:nt], refs[nt : 2 * nt]
        send_sems, recv_sems, token = refs[2 * nt], refs[2 * nt + 1], refs[-1]
        sends, _ = _exchange_copies(ins, zones, send_sems, recv_sems, scatter)
        for cp in sends:
            cp.start()
        token[...] = jnp.zeros(token.shape, F32)

    bufs = list(arrs) + list(lands)
    return pl.pallas_call(
        body,
        name=name,
        out_shape=(
            pltpu.SemaphoreType.DMA((3 * nt,)),
            pltpu.SemaphoreType.DMA((3 * nt,)),
            *[pltpu.HBM(a.shape, a.dtype) for a in bufs],
            jax.ShapeDtypeStruct((SUBLANES, LANES), F32),
        ),
        in_specs=[HBM_SPEC] * (2 * nt),
        out_specs=(SEM_SPEC, SEM_SPEC, *[HBM_SPEC] * (2 * nt), pl.BlockSpec(memory_space=pltpu.VMEM)),
        input_output_aliases={k: 2 + k for k in range(2 * nt)},
        compiler_params=pltpu.CompilerParams(has_side_effects=DATAFLOW),
    )(*[pltpu.with_memory_space_constraint(a, pltpu.HBM) for a in bufs])


def _exchange_wait(state, after, scatter, name):
    send_sems, recv_sems, *bufs = state[:-1]
    nt = len(bufs) // 2

    def body(*refs):
        ins, zones = refs[:nt], refs[nt : 2 * nt]
        sends, recvs = _exchange_copies(ins, zones, refs[2 * nt], refs[2 * nt + 1], scatter)
        for cp in sends:
            cp.wait_send()
        for cp in recvs:
            cp.wait_recv()

    out = pl.pallas_call(
        body,
        name=name,
        out_shape=tuple(pltpu.HBM(a.shape, a.dtype) for a in bufs),
        in_specs=[HBM_SPEC] * (2 * nt) + [SEM_SPEC, SEM_SPEC, pl.BlockSpec(memory_space=pl.ANY)],
        out_specs=[HBM_SPEC] * (2 * nt),
        input_output_aliases={k: k for k in range(2 * nt)},
        compiler_params=pltpu.CompilerParams(has_side_effects=DATAFLOW),
    )(*bufs, send_sems, recv_sems, after)
    return out[nt:]


def _sibling_swap(arrs, name):
    nt = len(arrs)

    def body(*refs):
        ins, outs = refs[:nt], refs[nt : 2 * nt]
        send_sems, recv_sems = refs[2 * nt :]
        x, y, c = _mesh_pos()
        cps = [
            pltpu.make_async_remote_copy(
                src_ref=ins[t],
                dst_ref=outs[t],
                send_sem=send_sems.at[t],
                recv_sem=recv_sems.at[t],
                device_id=(x, y, 1 - c),
                device_id_type=MESH,
            )
            for t in range(nt)
        ]
        for cp in cps:
            cp.start()
        for cp in cps:
            cp.wait_recv()
        for cp in cps:
            cp.wait_send()

    return pl.pallas_call(
        body,
        name=name,
        out_shape=tuple(jax.ShapeDtypeStruct(a.shape, a.dtype) for a in arrs),
        in_specs=[pl.BlockSpec(memory_space=pl.ANY)] * nt,
        out_specs=[pl.BlockSpec(memory_space=pl.ANY)] * nt,
        scratch_shapes=[pltpu.SemaphoreType.DMA((nt,)), pltpu.SemaphoreType.DMA((nt,))],
    )(*arrs)


def _cols_from_shards(g):
    _, k, n = g.shape
    return jnp.transpose(g, (1, 0, 2)).reshape(k, N_CHIP * n)


def _cols_to_shards(a):
    k, n4 = a.shape
    return jnp.transpose(a.reshape(k, N_CHIP, n4 // N_CHIP), (1, 0, 2))


def _pad_to(vec, mult):
    n = vec.shape[0]
    return jnp.pad(vec, (0, (-n) % mult))


def kernel(x, c, positions, w_ada, b_ada, g_pre_mix, g_post_mix, w_in, g_q, w_uq, g_kv, w_ukv, conv_w_mix, conv_b_mix, w_o, g_pre_ffn, g_post_ffn, w_up, conv_w_ffn, conv_b_ffn, w_down, loss_target, m_w_ada, m_b_ada, m_g_pre_mix, m_g_post_mix, m_w_in, m_g_q, m_w_uq, m_g_kv, m_w_ukv, m_conv_w_mix, m_conv_b_mix, m_w_o, m_g_pre_ffn, m_g_post_ffn, m_w_up, m_conv_w_ffn, m_conv_b_ffn, m_w_down, v_w_ada, v_b_ada, v_g_pre_mix, v_g_post_mix, v_w_in, v_g_q, v_w_uq, v_g_kv, v_w_ukv, v_conv_w_mix, v_conv_b_mix, v_w_o, v_g_pre_ffn, v_g_post_ffn, v_w_up, v_conv_w_ffn, v_conv_b_ffn, v_w_down):
    xi, yi, ci = _mesh_pos()
    chip = 2 * xi + yi
    dev = 4 * xi + 2 * yi + ci

    s, d = x.shape[1], x.shape[2]
    ql, kl = g_q.shape[1], g_kv.shape[1]
    cwid = conv_b_mix.shape[1]
    f2 = conv_b_ffn.shape[1]
    hh = (w_uq.shape[2] * N_CHIP) // (NOPE + ROPE)
    w_att = hh * LANES
    nc_ada = w_ada.shape[2]
    lat = ql + kl + ROPE
    tc_mix = _tile(cwid, 512, LANES)
    lb = -(-(ql + kl + HEAD_PAD) // tc_mix) * tc_mix
    np_cols = lb + 3 * cwid
    assert cwid == hh * VDIM and w_att % tc_mix == 0

    x0 = x.reshape(s, d)
    tgt = loss_target.reshape(s, d)

    cwm_n, cwf_n = CONV_K * cwid // N_CHIP, CONV_K * f2 // N_CHIP
    pack_a = _pad_to(jnp.concatenate([c.reshape(-1), conv_w_mix.reshape(-1), conv_w_ffn.reshape(-1)]), SUBLANES * LANES)
    rows_a = _all_gather8(pack_a.reshape(SUBLANES, -1), "ag8_inputs").reshape(N_DEV, -1)
    c_all = rows_a[:, :d]
    south = rows_a[0::2]
    cw_mix = jnp.concatenate([south[j, d : d + cwm_n].reshape(CONV_K, -1) for j in range(N_CHIP)], axis=1)
    cw_ffn = jnp.concatenate([south[j, d + cwm_n : d + cwm_n + cwf_n].reshape(CONV_K, -1) for j in range(N_CHIP)], axis=1)

    b_cols = lax.dynamic_slice(b_ada, (0, chip * nc_ada), (1, nc_ada))
    mod_part, c_act = _ada_fwd(c_all, w_ada[0], b_cols)
    mod_rows = _all_gather8(mod_part, "ag8_mod")
    mod = jnp.concatenate(
        [lax.dynamic_slice_in_dim(mod_rows, 2 * N_DEV * j + dev, 1, axis=0) for j in range(N_CHIP)], axis=1
    )

    shards = [a[0].astype(BF16) for a in (w_in, w_uq, w_ukv, w_o, w_up, w_down)]
    shards, mod = lax.optimization_barrier((shards, mod))
    ag_a = _exchange_start(shards[:3], False, "ag_a_start")
    mod, _ = lax.optimization_barrier((mod, ag_a[-1]))
    sh_m, sc_m, gt_m, sh_f, sc_f, gt_f = [mod[:, k * d : (k + 1) * d] for k in range(N_MOD)]

    inv_freq = 1.0 / (ROPE_THETA ** (jnp.arange(0, ROPE, 2, dtype=F32) / ROPE))
    invf = jnp.concatenate([inv_freq, inv_freq, jnp.zeros((LANES - ROPE,), F32)]).reshape(1, LANES)
    tabs = _rope_tables(positions.astype(F32).reshape(s, 1), invf)
    h1 = _pre_fwd(x0, g_pre_mix, sc_m, sh_m)

    def with_own(landed, own):
        return [lax.dynamic_update_slice_in_dim(g, a[None], chip, axis=0) for g, a in zip(landed, own)]

    landed_w = _exchange_wait(ag_a, h1, False, "ag_a_wait")
    rest, landed_w = lax.optimization_barrier((shards[3:], list(landed_w)))
    ag_b = _exchange_start(rest, False, "ag_b_start")
    landed_w, _ = lax.optimization_barrier((landed_w, ag_b[-1]))
    g_in, g_uq, g_ukv = with_own(landed_w, shards[:3])
    full_in = _cols_from_shards(g_in)
    w_in_p = jnp.concatenate([full_in[:, :lat], jnp.zeros((d, lb - lat), BF16), full_in[:, lat:]], axis=1)
    full_uq = _cols_from_shards(g_uq).reshape(ql, hh, NOPE + ROPE)
    w_uq_p = jnp.concatenate(
        [
            full_uq[:, :, :NOPE].reshape(ql, w_att),
            jnp.pad(full_uq[:, :, NOPE:], ((0, 0), (0, 0), (0, HEAD_PAD - ROPE))).reshape(ql, w_att),
        ],
        axis=1,
    )
    full_ukv = _cols_from_shards(g_ukv).reshape(kl, hh, NOPE + VDIM)
    w_ukv_p = jnp.concatenate([full_ukv[:, :, :NOPE].reshape(kl, w_att), full_ukv[:, :, NOPE:].reshape(kl, w_att)], axis=1)

    proj = _matmul(h1, w_in_p, out_dtype=F32, tm=1024, tn=768, tk=2048, name="mm_proj")
    qn, kvn, kr = _latent_fwd(proj, g_q, g_kv, tabs, lb)
    q_f = _matmul(qn, w_uq_p, out_dtype=F32, tm=1024, tn=1024, tk=2048, name="mm_q")
    kv_p = _matmul(kvn, w_ukv_p, out_dtype=BF16, tm=1024, tn=1024, tk=2048, name="mm_kv")
    q_c, k_c = _head_cat(q_f, kv_p, kr, tabs, hh)
    cat, lse2 = _attn_fwd(q_c, k_c, kv_p, hh, w_att + cwid)
    cat = _mixer_fwd(cat, proj, cw_mix, conv_b_mix, lb, w_att)
    g_o, g_up, g_down = with_own(_exchange_wait(ag_b, cat, False, "ag_b_wait"), rest)
    w_o_f = g_o.reshape(-1, d)
    w_up_f = _cols_from_shards(g_up)
    w_down_f = g_down.reshape(-1, d)
    mix = _matmul(cat, w_o_f, out_dtype=F32, tm=1024, tn=1024, tk=2048, name="mm_mix")

    x1, h2 = _mid_fwd(x0, mix, g_post_mix, gt_m, g_pre_ffn, sc_f, sh_f)
    up = _matmul(h2, w_up_f, out_dtype=F32, tm=1024, tn=1408, tk=2048, name="mm_up")
    act = _ffn_act_fwd(up, cw_ffn, conv_b_ffn)
    y = _matmul(act, w_down_f, out_dtype=F32, tm=1024, tn=1024, tk=1408, name="mm_down")
    dx2, dy, s_fin = _final(x1, y, tgt, g_post_ffn, gt_f)

    dw_down = _matmul(act, dy, ta=True, out_dtype=BF16, tm=1408, tn=1024, tk=1024, name="mm_dw_down")
    dact = _matmul(dy, w_down_f, tb=True, out_dtype=F32, tm=1024, tn=1408, tk=2048, name="mm_dact")
    dup_a, dup_g, s_fa, s_fg = _ffn_act_bwd(dact, up, cw_ffn, conv_b_ffn)
    dup = jnp.concatenate([dup_a, dup_g], axis=1)
    s_ffn = jnp.concatenate([s_fa, s_fg], axis=1)
    dw_up = _matmul(h2, dup, ta=True, out_dtype=BF16, tm=1024, tn=1408, tk=1024, name="mm_dw_up")
    dh2 = _matmul(dup, w_up_f, tb=True, out_dtype=F32, tm=1024, tn=1024, tk=1408, name="mm_dh2")
    dx1, dmix, s_mid = _mid_bwd(dh2, dx2, x1, mix, g_pre_ffn, sc_f, g_post_mix, gt_m)

    dw_o = _matmul(cat, dmix, ta=True, out_dtype=BF16, tm=1024, tn=1024, tk=1024, name="mm_dw_o")
    send_b = [dw_o.reshape(N_CHIP, -1, d), _cols_to_shards(dw_up), dw_down.reshape(N_CHIP, -1, d)]
    rs_b = _exchange_start(send_b, True, "rs_b_start")
    dmix, _ = lax.optimization_barrier((dmix, rs_b[-1]))
    dcat = _matmul(dmix, w_o_f, tb=True, out_dtype=F32, tm=1024, tn=1024, tk=2048, name="mm_dcat")
    dp_b, dp_c, dp_i, s_mix = _mixer_bwd(dcat, proj, cw_mix, conv_b_mix, lb, w_att)
    dob, stats = _attn_bwd_prep(cat, dcat, lse2, hh)
    dq_raw, dkv_k, dkv_v, dkr_h = _attn_bwd(q_c, k_c, kv_p, dob, stats, hh)
    dkv_p = jnp.concatenate([dkv_k, dkv_v], axis=1)
    dq_p = _dq_unrope(dq_raw, tabs, hh)
    dw_uq_p = _matmul(qn, dq_p, ta=True, out_dtype=BF16, tm=1024, tn=1024, tk=1024, name="mm_dw_uq")
    dqn = _matmul(dq_p, w_uq_p, tb=True, out_dtype=F32, tm=1024, tn=1024, tk=2048, name="mm_dqn")
    dw_ukv_p = _matmul(kvn, dkv_p, ta=True, out_dtype=BF16, tm=1024, tn=1024, tk=1024, name="mm_dw_ukv")
    dkvn = _matmul(dkv_p, w_ukv_p, tb=True, out_dtype=F32, tm=1024, tn=1024, tk=2048, name="mm_dkvn")
    dp_lat, s_lat = _latent_bwd(proj, dqn, dkvn, dkr_h, g_q, g_kv, tabs, lb)
    dproj = jnp.concatenate([dp_lat, dp_b, dp_c, dp_i], axis=1)
    dw_in_p = _matmul(h1, dproj, ta=True, out_dtype=BF16, tm=1024, tn=1536, tk=1024, name="mm_dw_in")

    dw_in_f = jnp.concatenate([dw_in_p[:, :lat], dw_in_p[:, lb:]], axis=1)
    uq3 = dw_uq_p.reshape(ql, 2, hh, LANES)
    dw_uq_f = jnp.concatenate([uq3[:, 0], uq3[:, 1, :, :ROPE]], axis=2).reshape(ql, hh * (NOPE + ROPE))
    ukv3 = dw_ukv_p.reshape(kl, 2, hh, LANES)
    dw_ukv_f = jnp.concatenate([ukv3[:, 0], ukv3[:, 1]], axis=2).reshape(kl, hh * (NOPE + VDIM))
    send_a = [_cols_to_shards(dw_in_f), _cols_to_shards(dw_uq_f), _cols_to_shards(dw_ukv_f)]
    rs_a = _exchange_start(send_a, True, "rs_a_start")
    dproj, _ = lax.optimization_barrier((dproj, rs_a[-1]))

    dh1 = _matmul(dproj, w_in_p, tb=True, out_dtype=F32, tm=1024, tn=1024, tk=1536, name="mm_dh1")
    grad_x, s_first = _first_bwd(dh1, dx1, x0, g_pre_mix, sc_m)

    names = ["w_in", "w_uq", "w_ukv", "w_o", "w_up", "w_down"]
    landed_b = _exchange_wait(rs_b, s_first, True, "rs_b_wait")
    landed_a = _exchange_wait(rs_a, landed_b[0], True, "rs_a_wait")
    landed_a, s_first = lax.optimization_barrier((list(landed_a), s_first))
    part = [_sum_chips(l, a, "sum_chips_" + n) for l, a, n in zip(list(landed_a) + list(landed_b), send_a + send_b, names)]
    other = _sibling_swap(part, "sibling_swap")

    dmod = jnp.concatenate([s_first[0:1], s_first[1:2], s_mid[3:4], s_mid[0:1], s_mid[1:2], s_fin[0:1]], axis=1)
    small = [
        dmod,
        s_first[2:3],
        s_mid[4:5],
        s_lat[0:1, :ql],
        s_lat[0:1, ql : ql + kl],
        s_mix[3:4],
        s_mid[2:3],
        s_fin[1:2],
        s_ffn[3:4],
        s_mix[0:3].reshape(1, -1),
        s_ffn[0:3].reshape(1, -1),
        s_fin[3:4, :LANES],
    ]
    sizes = [a.shape[1] for a in small]
    offs = [0]
    for n in sizes:
        offs.append(offs[-1] + n)
    pack_g = _pad_to(jnp.concatenate(small, axis=1).reshape(-1), SUBLANES * LANES * SUBLANES).reshape(SUBLANES, -1)
    gathered = _all_gather8(pack_g, "ag8_small_grads")
    tot = _sum_devices(gathered).reshape(-1)
    part_of = lambda k: tot[offs[k] : offs[k + 1]]
    dmod_all = gathered.reshape(N_DEV, -1)[:, : N_MOD * d]
    loss = part_of(11)[0]

    g_b_ada = part_of(0).reshape(1, -1)
    g_vecs = [part_of(k).reshape(1, -1) for k in range(1, 9)]
    g_cw_mix = lax.dynamic_slice(part_of(9).reshape(CONV_K, cwid), (0, chip * (cwid // N_CHIP)), (CONV_K, cwid // N_CHIP))
    g_cw_ffn = lax.dynamic_slice(part_of(10).reshape(CONV_K, f2), (0, chip * (f2 // N_CHIP)), (CONV_K, f2 // N_CHIP))

    dm_cols = lax.dynamic_slice(dmod_all, (0, chip * nc_ada), (N_DEV, nc_ada))
    g_w_ada = _ada_grad(
        jnp.pad(c_act.T, ((0, 0), (0, LANES - N_DEV))), jnp.pad(dm_cols, ((0, LANES - N_DEV), (0, 0)))
    )

    big_w = [w_in, w_uq, w_ukv, w_o, w_up, w_down]
    big_m = [m_w_in, m_w_uq, m_w_ukv, m_w_o, m_w_up, m_w_down]
    big_v = [v_w_in, v_w_uq, v_w_ukv, v_w_o, v_w_up, v_w_down]
    big = {}
    for n, w_, m_, v_, p_, o_ in zip(names, big_w, big_m, big_v, part, other):
        big[n] = [a[None] for a in _adamw(w_[0], m_[0], v_[0], [p_, o_], "adamw_" + n)]
    big["w_ada"] = [a[None] for a in _adamw(w_ada[0], m_w_ada[0], v_w_ada[0], [g_w_ada], "adamw_w_ada")]

    sm_names = ["b_ada", "g_pre_mix", "g_post_mix", "g_q", "g_kv", "conv_b_mix", "g_pre_ffn", "g_post_ffn", "conv_b_ffn",
                "conv_w_mix", "conv_w_ffn"]
    sm_w = [b_ada, g_pre_mix, g_post_mix, g_q, g_kv, conv_b_mix, g_pre_ffn, g_post_ffn, conv_b_ffn, conv_w_mix, conv_w_ffn]
    sm_m = [m_b_ada, m_g_pre_mix, m_g_post_mix, m_g_q, m_g_kv, m_conv_b_mix, m_g_pre_ffn, m_g_post_ffn, m_conv_b_ffn,
            m_conv_w_mix, m_conv_w_ffn]
    sm_v = [v_b_ada, v_g_pre_mix, v_g_post_mix, v_g_q, v_g_kv, v_conv_b_mix, v_g_pre_ffn, v_g_post_ffn, v_conv_b_ffn,
            v_conv_w_mix, v_conv_w_ffn]
    sm_g = [g_b_ada] + g_vecs + [g_cw_mix, g_cw_ffn]
    flat = lambda arrs: jnp.concatenate([a.reshape(1, -1) for a in arrs], axis=1)
    sm_out = _adamw(flat(sm_w), flat(sm_m), flat(sm_v), [flat(sm_g)], "adamw_small")
    sm = {}
    off = 0
    for n, w_ in zip(sm_names, sm_w):
        sm[n] = [o[:, off : off + w_.size].reshape(w_.shape) for o in sm_out]
        off += w_.size

    order = ["w_ada", "b_ada", "g_pre_mix", "g_post_mix", "w_in", "g_q", "w_uq", "g_kv", "w_ukv", "conv_w_mix", "conv_b_mix",
             "w_o", "g_pre_ffn", "g_post_ffn", "w_up", "conv_w_ffn", "conv_b_ffn", "w_down"]
    res = {**big, **sm}
    outs = [loss, grad_x.reshape(x.shape)]
    for k in range(4):
        outs += [res[n][k] for n in order]
    return tuple(outs)
```

```python
import math

import jax
import jax.numpy as jnp
from jax import lax
from jax.experimental import pallas as pl
from jax.experimental.pallas import tpu as pltpu

F32 = jnp.float32
BF16 = jnp.bfloat16
MESH = pl.DeviceIdType.MESH

N_DEV = 8
N_CHIP = 4
LANES = 128
SUBLANES = 8
VMEM_LIMIT = 56 * 2**20

NOPE = 128
ROPE = 64
VDIM = 128
HEAD_PAD = 128
ROPE_THETA = 10000.0
RMS_EPS = 1e-6
N_MOD = 6
CONV_K = 3
ATT_BLOCK = 512
NEG = -1e30

ADAM_LR = 0.001
ADAM_B1 = 0.9
ADAM_B2 = 0.999
ADAM_EPS = 1e-08
ADAM_WD = 0.01
ADAM_STEP = 10


def _tile(n, pref, align):
    if n <= pref:
        return n
    t = (pref // align) * align
    while t >= align:
        if n % t == 0:
            return t
        t -= align
    return n


def _cp(*sem):
    return pltpu.CompilerParams(dimension_semantics=sem, vmem_limit_bytes=VMEM_LIMIT)


def _rsq(x):
    return lax.rsqrt(jnp.mean(x * x, axis=-1, keepdims=True) + RMS_EPS)


def _norm_bwd(dn, n, r):
    return r * (dn - n * jnp.mean(dn * n, axis=-1, keepdims=True))


def _colsum(a):
    return jnp.sum(a, axis=0, keepdims=True)


def _matmul(a, b, *, ta=False, tb=False, out_dtype, tm, tn, tk, name):
    (k_a, m) = a.shape if ta else a.shape[::-1]
    (n, k_b) = b.shape if tb else b.shape[::-1]
    assert k_a == k_b, (a.shape, b.shape, ta, tb)
    tm, tn, tk = _tile(m, tm, LANES), _tile(n, tn, LANES), _tile(k_a, tk, LANES)
    nk = k_a // tk
    a_spec = pl.BlockSpec((tk, tm), lambda i, j, k: (k, i)) if ta else pl.BlockSpec((tm, tk), lambda i, j, k: (i, k))
    b_spec = pl.BlockSpec((tn, tk), lambda i, j, k: (j, k)) if tb else pl.BlockSpec((tk, tn), lambda i, j, k: (k, j))
    dims = (((0 if ta else 1,), (1 if tb else 0,)), ((), ()))

    def body(a_ref, b_ref, o_ref, *acc):
        p = lax.dot_general(a_ref[...].astype(BF16), b_ref[...].astype(BF16), dims, preferred_element_type=F32)
        if nk == 1:
            o_ref[...] = p.astype(o_ref.dtype)
        else:
            k = pl.program_id(2)

            @pl.when(k == 0)
            def _():
                acc[0][...] = p

            @pl.when(k > 0)
            def _():
                acc[0][...] += p

            @pl.when(k == nk - 1)
            def _():
                o_ref[...] = acc[0][...].astype(o_ref.dtype)

    return pl.pallas_call(
        body,
        name=name,
        out_shape=jax.ShapeDtypeStruct((m, n), out_dtype),
        grid=(m // tm, n // tn, nk),
        in_specs=[a_spec, b_spec],
        out_specs=pl.BlockSpec((tm, tn), lambda i, j, k: (i, j)),
        scratch_shapes=[] if nk == 1 else [pltpu.VMEM((tm, tn), F32)],
        compiler_params=_cp("parallel", "parallel", "arbitrary"),
    )(a, b)


def _rope_tables(pos_col, invf):
    s = pos_col.shape[0]
    ts = _tile(s, 1024, SUBLANES)
    half = ROPE // 2

    def body(p_ref, f_ref, c_ref, sa_ref, sb_ref):
        ang = p_ref[...] * f_ref[...]
        lane = lax.broadcasted_iota(jnp.int32, ang.shape, 1)
        cs, sn = jnp.cos(ang), jnp.sin(ang)
        c_ref[...] = jnp.where(lane < ROPE, cs, 0.0)
        sa_ref[...] = jnp.where((lane >= half) & (lane < ROPE), sn, 0.0)
        sb_ref[...] = jnp.where(lane < half, -sn, 0.0)

    tab = jax.ShapeDtypeStruct((s, LANES), F32)
    return pl.pallas_call(
        body,
        name="rope_tables",
        out_shape=(tab, tab, tab),
        grid=(s // ts,),
        in_specs=[pl.BlockSpec((ts, 1), lambda i: (i, 0)), pl.BlockSpec((1, LANES), lambda i: (0, 0))],
        out_specs=[pl.BlockSpec((ts, LANES), lambda i: (i, 0))] * 3,
        compiler_params=_cp("parallel"),
    )(pos_col, invf)


def _widen(t, w):
    return t if w == LANES else jnp.tile(t, (1, w // LANES))


def _rope(x, c, sa, sb):
    w = x.shape[1]
    c, sa, sb = _widen(c, w), _widen(sa, w), _widen(sb, w)
    return x * c + pltpu.roll(x, ROPE // 2, 1) * sa + pltpu.roll(x, w - ROPE // 2, 1) * sb


def _rope_t(d, c, sa, sb):
    w = d.shape[1]
    c, sa, sb = _widen(c, w), _widen(sa, w), _widen(sb, w)
    return d * c + pltpu.roll(d * sa, w - ROPE // 2, 1) + pltpu.roll(d * sb, ROPE // 2, 1)


def _ada_fwd(c_all, w, b):
    d, nc = w.shape
    tn = _tile(nc, 512, LANES)

    def body(c_ref, w_ref, b_ref, o_ref, ca_ref):
        cv = c_ref[...]
        ca = cv * jax.nn.sigmoid(cv)
        ca_ref[...] = ca
        o_ref[...] = jnp.dot(ca.astype(BF16), w_ref[...].astype(BF16), preferred_element_type=F32) + b_ref[...]

    return pl.pallas_call(
        body,
        name="ada_fwd",
        out_shape=(jax.ShapeDtypeStruct((N_DEV, nc), F32), jax.ShapeDtypeStruct((N_DEV, d), F32)),
        grid=(nc // tn,),
        in_specs=[
            pl.BlockSpec((N_DEV, d), lambda j: (0, 0)),
            pl.BlockSpec((d, tn), lambda j: (0, j)),
            pl.BlockSpec((1, tn), lambda j: (0, j)),
        ],
        out_specs=[pl.BlockSpec((N_DEV, tn), lambda j: (0, j)), pl.BlockSpec((N_DEV, d), lambda j: (0, 0))],
        compiler_params=_cp("arbitrary"),
    )(c_all, w, b)


def _rows(ts, d):
    return pl.BlockSpec((ts, d), lambda i: (i, 0))


def _vec(d):
    return pl.BlockSpec((1, d), lambda i: (0, 0))


def _sums(d):
    return pl.BlockSpec((SUBLANES, d), lambda i: (0, 0))


def _acc_rows(ref, i, rows):
    @pl.when(i == 0)
    def _():
        ref[...] = jnp.zeros(ref.shape, ref.dtype)

    for k, r in enumerate(rows):
        ref[k : k + 1, :] += r


def _pre_fwd(x, g, sc, sh):
    s, d = x.shape
    ts = _tile(s, 512, SUBLANES)

    def body(x_ref, g_ref, sc_ref, sh_ref, h_ref):
        xv = x_ref[...]
        h_ref[...] = (((xv * _rsq(xv)) * g_ref[...]) * (1.0 + sc_ref[...]) + sh_ref[...]).astype(BF16)

    return pl.pallas_call(
        body,
        name="pre_mix_fwd",
        out_shape=jax.ShapeDtypeStruct((s, d), BF16),
        grid=(s // ts,),
        in_specs=[_rows(ts, d), _vec(d), _vec(d), _vec(d)],
        out_specs=_rows(ts, d),
        compiler_params=_cp("parallel"),
    )(x, g, sc, sh)


def _mid_fwd(x0, mix, g_post, gt, g_pre, sc, sh):
    s, d = x0.shape
    ts = _tile(s, 256, SUBLANES)

    def body(x_ref, m_ref, gp_ref, gt_ref, g_ref, sc_ref, sh_ref, x1_ref, h_ref):
        mv = m_ref[...]
        x1 = x_ref[...] + gt_ref[...] * ((mv * _rsq(mv)) * gp_ref[...])
        x1_ref[...] = x1
        h_ref[...] = (((x1 * _rsq(x1)) * g_ref[...]) * (1.0 + sc_ref[...]) + sh_ref[...]).astype(BF16)

    return pl.pallas_call(
        body,
        name="mid_fwd",
        out_shape=(jax.ShapeDtypeStruct((s, d), F32), jax.ShapeDtypeStruct((s, d), BF16)),
        grid=(s // ts,),
        in_specs=[_rows(ts, d), _rows(ts, d)] + [_vec(d)] * 5,
        out_specs=[_rows(ts, d), _rows(ts, d)],
        compiler_params=_cp("parallel"),
    )(x0, mix, g_post, gt, g_pre, sc, sh)


def _final(x1, y, tgt, g_post, gt):
    s, d = x1.shape
    ts = _tile(s, 256, SUBLANES)
    ni = s // ts

    def body(x_ref, y_ref, t_ref, gp_ref, gt_ref, dx_ref, dy_ref, s_ref):
        i = pl.program_id(0)
        yv, gp, gt_v = y_ref[...], gp_ref[...], gt_ref[...]
        r = _rsq(yv)
        n = yv * r
        err = (x_ref[...] + gt_v * (n * gp)) - t_ref[...]
        dx = err * (1.0 / d)
        dx_ref[...] = dx
        dy_ref[...] = _norm_bwd(dx * (gt_v * gp), n, r).astype(BF16)
        _acc_rows(s_ref, i, [_colsum(dx * (n * gp)), _colsum(dx * gt_v * n), _colsum(err * err)])

        @pl.when(i == ni - 1)
        def _():
            tot = jnp.sum(s_ref[2:3, :], axis=1, keepdims=True) * (0.5 / d)
            s_ref[3:4, :] = jnp.broadcast_to(tot, (1, d))

    return pl.pallas_call(
        body,
        name="final_fwd_bwd",
        out_shape=(
            jax.ShapeDtypeStruct((s, d), F32),
            jax.ShapeDtypeStruct((s, d), BF16),
            jax.ShapeDtypeStruct((SUBLANES, d), F32),
        ),
        grid=(ni,),
        in_specs=[_rows(ts, d)] * 3 + [_vec(d)] * 2,
        out_specs=[_rows(ts, d), _rows(ts, d), _sums(d)],
        compiler_params=_cp("arbitrary"),
    )(x1, y, tgt, g_post, gt)


def _mid_bwd(dh2, dx2, x1, mix, g_pre, sc, g_post, gt):
    s, d = x1.shape
    ts = _tile(s, 256, SUBLANES)

    def body(dh_ref, dx2_ref, x_ref, m_ref, g_ref, sc_ref, gp_ref, gt_ref, dx1_ref, dm_ref, s_ref):
        i = pl.program_id(0)
        dh, xv, mv = dh_ref[...], x_ref[...], m_ref[...]
        g, sc_v, gp, gt_v = g_ref[...], sc_ref[...], gp_ref[...], gt_ref[...]
        r1 = _rsq(xv)
        n1 = xv * r1
        dx1 = dx2_ref[...] + _norm_bwd(dh * (g * (1.0 + sc_v)), n1, r1)
        dx1_ref[...] = dx1
        rm = _rsq(mv)
        nm = mv * rm
        dm_ref[...] = _norm_bwd(dx1 * (gt_v * gp), nm, rm).astype(BF16)
        _acc_rows(
            s_ref,
            i,
            [
                _colsum(dh),
                _colsum(dh * (n1 * g)),
                _colsum(dh * (1.0 + sc_v) * n1),
                _colsum(dx1 * (nm * gp)),
                _colsum(dx1 * gt_v * nm),
            ],
        )

    return pl.pallas_call(
        body,
        name="mid_bwd",
        out_shape=(
            jax.ShapeDtypeStruct((s, d), F32),
            jax.ShapeDtypeStruct((s, d), BF16),
            jax.ShapeDtypeStruct((SUBLANES, d), F32),
        ),
        grid=(s // ts,),
        in_specs=[_rows(ts, d)] * 4 + [_vec(d)] * 4,
        out_specs=[_rows(ts, d), _rows(ts, d), _sums(d)],
        compiler_params=_cp("arbitrary"),
    )(dh2, dx2, x1, mix, g_pre, sc, g_post, gt)


def _first_bwd(dh1, dx1, x0, g, sc):
    s, d = x0.shape
    ts = _tile(s, 256, SUBLANES)

    def body(dh_ref, dx1_ref, x_ref, g_ref, sc_ref, dx_ref, s_ref):
        i = pl.program_id(0)
        dh, xv, gv, sc_v = dh_ref[...], x_ref[...], g_ref[...], sc_ref[...]
        r = _rsq(xv)
        n = xv * r
        dx_ref[...] = dx1_ref[...] + _norm_bwd(dh * (gv * (1.0 + sc_v)), n, r)
        _acc_rows(s_ref, i, [_colsum(dh), _colsum(dh * (n * gv)), _colsum(dh * (1.0 + sc_v) * n)])

    return pl.pallas_call(
        body,
        name="first_bwd",
        out_shape=(jax.ShapeDtypeStruct((s, d), F32), jax.ShapeDtypeStruct((SUBLANES, d), F32)),
        grid=(s // ts,),
        in_specs=[_rows(ts, d)] * 3 + [_vec(d)] * 2,
        out_specs=[_rows(ts, d), _sums(d)],
        compiler_params=_cp("arbitrary"),
    )(dh1, dx1, x0, g, sc)


def _latent_fwd(proj, g_q, g_kv, tabs, lb):
    s = proj.shape[0]
    ql, kl = g_q.shape[1], g_kv.shape[1]
    ts = _tile(s, 512, SUBLANES)

    def body(p_ref, gq_ref, gk_ref, c_ref, sa_ref, sb_ref, q_ref, kv_ref, kr_ref):
        pv = p_ref[...]
        q, kv, kr = pv[:, :ql], pv[:, ql : ql + kl], pv[:, ql + kl : ql + kl + HEAD_PAD]
        q_ref[...] = ((q * _rsq(q)) * gq_ref[...]).astype(BF16)
        kv_ref[...] = ((kv * _rsq(kv)) * gk_ref[...]).astype(BF16)
        kr_ref[...] = _rope(kr, c_ref[...], sa_ref[...], sb_ref[...]).astype(BF16)

    return pl.pallas_call(
        body,
        name="latent_fwd",
        out_shape=(
            jax.ShapeDtypeStruct((s, ql), BF16),
            jax.ShapeDtypeStruct((s, kl), BF16),
            jax.ShapeDtypeStruct((s, HEAD_PAD), BF16),
        ),
        grid=(s // ts,),
        in_specs=[_rows(ts, lb), _vec(ql), _vec(kl)] + [_rows(ts, LANES)] * 3,
        out_specs=[_rows(ts, ql), _rows(ts, kl), _rows(ts, HEAD_PAD)],
        compiler_params=_cp("parallel"),
    )(proj, g_q, g_kv, *tabs)


def _latent_bwd(proj, dqn, dkvn, dkr_h, g_q, g_kv, tabs, lb):
    s = proj.shape[0]
    ql, kl = g_q.shape[1], g_kv.shape[1]
    hw = dkr_h.shape[1]
    ts = _tile(s, 256, SUBLANES)
    pad = lb - ql - kl - HEAD_PAD

    def body(p_ref, dq_ref, dkv_ref, dkr_ref, gq_ref, gk_ref, c_ref, sa_ref, sb_ref, o_ref, s_ref):
        i = pl.program_id(0)
        pv = p_ref[...]
        q, kv = pv[:, :ql], pv[:, ql : ql + kl]
        dqn_v, dkvn_v = dq_ref[...], dkv_ref[...]
        rq = _rsq(q)
        nq = q * rq
        rk = _rsq(kv)
        nk = kv * rk
        dkr = dkr_ref[:, :HEAD_PAD]
        for h in range(1, hw // HEAD_PAD):
            dkr = dkr + dkr_ref[:, h * HEAD_PAD : (h + 1) * HEAD_PAD]
        parts = [
            _norm_bwd(dqn_v * gq_ref[...], nq, rq).astype(BF16),
            _norm_bwd(dkvn_v * gk_ref[...], nk, rk).astype(BF16),
            _rope_t(dkr, c_ref[...], sa_ref[...], sb_ref[...]).astype(BF16),
        ]
        if pad:
            parts.append(jnp.zeros((ts, pad), BF16))
        o_ref[...] = jnp.concatenate(parts, axis=1)
        row = [_colsum(dqn_v * nq), _colsum(dkvn_v * nk), jnp.zeros((1, lb - ql - kl), F32)]
        _acc_rows(s_ref, i, [jnp.concatenate(row, axis=1)])

    return pl.pallas_call(
        body,
        name="latent_bwd",
        out_shape=(jax.ShapeDtypeStruct((s, lb), BF16), jax.ShapeDtypeStruct((SUBLANES, lb), F32)),
        grid=(s // ts,),
        in_specs=[_rows(ts, lb), _rows(ts, ql), _rows(ts, kl), _rows(ts, hw)]
        + [_vec(ql), _vec(kl)]
        + [_rows(ts, LANES)] * 3,
        out_specs=[_rows(ts, lb), _sums(lb)],
        compiler_params=_cp("arbitrary"),
    )(proj, dqn, dkvn, dkr_h, g_q, g_kv, *tabs)


def _conv3(ext, w, b):
    return (pltpu.roll(ext, 2, 0) * w[0:1] + pltpu.roll(ext, 1, 0) * w[1:2]) + ext * w[2:3] + b


def _conv3_t(du, w):
    n = du.shape[0]
    return du * w[2:3] + pltpu.roll(du, n - 1, 0) * w[1:2] + pltpu.roll(du, n - 2, 0) * w[0:1]


def _halo_maps(ts, s):
    r8, last = ts // SUBLANES, s // SUBLANES - 1
    prev = lambda i: jnp.maximum(i * r8 - 1, 0)
    nxt = lambda i: jnp.minimum((i + 1) * r8, last)
    return prev, nxt


def _mixer_fwd(cat, proj, cw, cb, lb, col0):
    s = proj.shape[0]
    cwid = cw.shape[1]
    ts = _tile(s, 512, SUBLANES)
    tc = _tile(cwid, 512, LANES)
    assert lb % tc == 0 and col0 % tc == 0
    nj, ob, oc = cwid // tc, lb // tc, col0 // tc
    prev, _ = _halo_maps(ts, s)

    def body(_, gb_ref, gc_ref, ci_ref, pgc_ref, pci_ref, w_ref, b_ref, o_ref):
        keep = jnp.where(pl.program_id(1) > 0, 1.0, 0.0)
        ext = jnp.concatenate([pgc_ref[...] * pci_ref[...] * keep, gc_ref[...] * ci_ref[...]], axis=0)
        o_ref[...] = (gb_ref[...] * _conv3(ext, w_ref[...], b_ref[...])[SUBLANES:]).astype(BF16)

    def col(k):
        return pl.BlockSpec((ts, tc), lambda j, i: (i, ob + k * nj + j))

    def halo(k):
        return pl.BlockSpec((SUBLANES, tc), lambda j, i: (prev(i), ob + k * nj + j))

    return pl.pallas_call(
        body,
        name="mixer_fwd",
        out_shape=jax.ShapeDtypeStruct(cat.shape, BF16),
        grid=(nj, s // ts),
        in_specs=[pl.BlockSpec(memory_space=pl.ANY), col(0), col(1), col(2), halo(1), halo(2)]
        + [pl.BlockSpec((CONV_K, tc), lambda j, i: (0, j)), pl.BlockSpec((1, tc), lambda j, i: (0, j))],
        out_specs=pl.BlockSpec((ts, tc), lambda j, i: (i, oc + j)),
        input_output_aliases={0: 0},
        compiler_params=_cp("parallel", "arbitrary"),
    )(cat, proj, proj, proj, proj, proj, cw, cb)


def _mixer_bwd(dcat, proj, cw, cb, lb, col0):
    s = proj.shape[0]
    cwid = cw.shape[1]
    ts = _tile(s, 256, SUBLANES)
    tc = _tile(cwid, 512, LANES)
    nj, ob, oc = cwid // tc, lb // tc, col0 // tc
    ni = s // ts
    prev, nxt = _halo_maps(ts, s)

    def body(d_ref, dn_ref, gb_ref, gbn_ref, gc_ref, gcp_ref, gcn_ref, ci_ref, cip_ref, cin_ref, w_ref, b_ref,
             dgb_ref, dgc_ref, dci_ref, s_ref):
        i = pl.program_id(1)
        keep_p = jnp.where(i > 0, 1.0, 0.0)
        keep_n = jnp.where(i < ni - 1, 1.0, 0.0)
        w = w_ref[...]
        gc = jnp.concatenate([gcp_ref[...], gc_ref[...], gcn_ref[...]], axis=0)
        ci = jnp.concatenate([cip_ref[...] * keep_p, ci_ref[...], cin_ref[...]], axis=0)
        u = gc * ci
        cv = _conv3(u, w, b_ref[...])[SUBLANES:]
        dco = jnp.concatenate([d_ref[...], dn_ref[...] * keep_n], axis=0)
        gb = jnp.concatenate([gb_ref[...], gbn_ref[...]], axis=0)
        dgb_ref[...] = (dco * cv)[:ts].astype(BF16)
        dcv = dco * gb
        du = _conv3_t(dcv, w)[:ts]
        dgc_ref[...] = (du * ci_ref[...]).astype(BF16)
        dci_ref[...] = (du * gc_ref[...]).astype(BF16)
        dt = dcv[:ts]
        u1, u2 = pltpu.roll(u, 1, 0), pltpu.roll(u, 2, 0)
        lo, hi = SUBLANES, SUBLANES + ts
        _acc_rows(s_ref, i, [_colsum(dt * u2[lo:hi]), _colsum(dt * u1[lo:hi]), _colsum(dt * u[lo:hi]), _colsum(dt)])

    def col(k):
        return pl.BlockSpec((ts, tc), lambda j, i: (i, ob + k * nj + j))

    def halo(k, which):
        return pl.BlockSpec((SUBLANES, tc), lambda j, i: (which(i), ob + k * nj + j))

    out_col = [pl.BlockSpec((ts, tc), lambda j, i: (i, j))] * 3
    grad = jax.ShapeDtypeStruct((s, cwid), BF16)
    return pl.pallas_call(
        body,
        name="mixer_bwd",
        out_shape=(grad, grad, grad, jax.ShapeDtypeStruct((SUBLANES, cwid), F32)),
        grid=(nj, ni),
        in_specs=[
            pl.BlockSpec((ts, tc), lambda j, i: (i, oc + j)),
            pl.BlockSpec((SUBLANES, tc), lambda j, i: (nxt(i), oc + j)),
            col(0), halo(0, nxt),
            col(1), halo(1, prev), halo(1, nxt),
            col(2), halo(2, prev), halo(2, nxt),
            pl.BlockSpec((CONV_K, tc), lambda j, i: (0, j)),
            pl.BlockSpec((1, tc), lambda j, i: (0, j)),
        ],
        out_specs=out_col + [pl.BlockSpec((SUBLANES, tc), lambda j, i: (0, j))],
        compiler_params=_cp("parallel", "arbitrary"),
    )(dcat, dcat, proj, proj, proj, proj, proj, proj, proj, proj, cw, cb)


def _pair_tile(f):
    return _tile(f, 512, LANES)


def _pair_cols(a):
    r, f2 = a.shape
    tc = _pair_tile(f2 // 2)
    return a.reshape(r, 2, f2 // (2 * tc), tc).transpose(0, 2, 1, 3).reshape(r, f2)


def _unpair_cols(a):
    r, f2 = a.shape
    tc = _pair_tile(f2 // 2)
    return a.reshape(r, f2 // (2 * tc), 2, tc).transpose(0, 2, 1, 3).reshape(r, f2)


def _ffn_act_fwd(up, cw, cb):
    s, f2 = up.shape
    f = f2 // 2
    ts = _tile(s, 512, SUBLANES)
    tc = _pair_tile(f)
    prev, _ = _halo_maps(ts, s)

    def body(u_ref, p_ref, w_ref, b_ref, o_ref):
        keep = jnp.where(pl.program_id(1) > 0, 1.0, 0.0)
        ext = jnp.concatenate([p_ref[...] * keep, u_ref[...]], axis=0)
        u = _conv3(ext, w_ref[...], b_ref[...])[SUBLANES:]
        a, g = u[:, :tc], u[:, tc:]
        o_ref[...] = ((g * jax.nn.sigmoid(g)) * a).astype(BF16)

    def pair(rows, which):
        return pl.BlockSpec((rows, 2 * tc), lambda j, i: (which(i), j))

    return pl.pallas_call(
        body,
        name="ffn_act_fwd",
        out_shape=jax.ShapeDtypeStruct((s, f), BF16),
        grid=(f // tc, s // ts),
        in_specs=[pair(ts, lambda i: i), pair(SUBLANES, prev), pair(CONV_K, lambda i: 0), pair(1, lambda i: 0)],
        out_specs=pl.BlockSpec((ts, tc), lambda j, i: (i, j)),
        compiler_params=_cp("parallel", "arbitrary"),
    )(up, up, cw, cb)


def _ffn_act_bwd(dact, up, cw, cb):
    s, f2 = up.shape
    f = f2 // 2
    ts = _tile(s, 256, SUBLANES)
    tc = _pair_tile(f)
    nj, ni = f // tc, s // ts
    prev, nxt = _halo_maps(ts, s)

    def body(d_ref, dn_ref, u_ref, up_ref, un_ref, w_ref, b_ref, dup_ref, s_ref):
        i = pl.program_id(1)
        keep_p = jnp.where(i > 0, 1.0, 0.0)
        keep_n = jnp.where(i < ni - 1, 1.0, 0.0)
        w = w_ref[...]
        ext = jnp.concatenate([up_ref[...] * keep_p, u_ref[...], un_ref[...]], axis=0)
        u = _conv3(ext, w, b_ref[...])[SUBLANES:]
        a, g = u[:, :tc], u[:, tc:]
        dact_v = jnp.concatenate([d_ref[...], dn_ref[...] * keep_n], axis=0)
        sg = jax.nn.sigmoid(g)
        du = jnp.concatenate([dact_v * (g * sg), dact_v * a * (sg * (1.0 + g * (1.0 - sg)))], axis=1)
        dup_ref[...] = _conv3_t(du, w)[:ts].astype(BF16)
        dt = du[:ts]
        lo, hi = SUBLANES, SUBLANES + ts
        e1, e2 = pltpu.roll(ext, 1, 0), pltpu.roll(ext, 2, 0)
        _acc_rows(s_ref, i, [_colsum(dt * e2[lo:hi]), _colsum(dt * e1[lo:hi]), _colsum(dt * ext[lo:hi]), _colsum(dt)])

    def pair(rows, which):
        return pl.BlockSpec((rows, 2 * tc), lambda j, i: (which(i), j))

    return pl.pallas_call(
        body,
        name="ffn_act_bwd",
        out_shape=(jax.ShapeDtypeStruct((s, f2), BF16), jax.ShapeDtypeStruct((SUBLANES, f2), F32)),
        grid=(nj, ni),
        in_specs=[
            pl.BlockSpec((ts, tc), lambda j, i: (i, j)),
            pl.BlockSpec((SUBLANES, tc), lambda j, i: (nxt(i), j)),
            pair(ts, lambda i: i), pair(SUBLANES, prev), pair(SUBLANES, nxt),
            pair(CONV_K, lambda i: 0), pair(1, lambda i: 0),
        ],
        out_specs=[pair(ts, lambda i: i), pair(SUBLANES, lambda i: 0)],
        compiler_params=_cp("parallel", "arbitrary"),
    )(dact, dact, up, up, up, cw, cb)


ATT_SCALE = 1.0 / math.sqrt(NOPE + ROPE)
LOG2E = math.log2(math.e)
ATT_C2 = ATT_SCALE * LOG2E
ATT_SUB = 256
STAT_SPLIT = 64
NT = (((1,), (1,)), ((), ()))
TN = (((0,), (0,)), ((), ()))


def _head_cat(q, kv, kr, tabs, n_heads):
    s, w2 = q.shape
    w = w2 // 2
    ts = _tile(s, 512, SUBLANES)
    hd = NOPE + HEAD_PAD

    def body(q_ref, kv_ref, kr_ref, c_ref, sa_ref, sb_ref, qc_ref, kc_ref):
        qv = q_ref[...]
        qr = _rope(qv[:, w:], c_ref[...], sa_ref[...], sb_ref[...]).astype(BF16)
        krv = kr_ref[...]
        for h in range(n_heads):
            qc_ref[:, h * hd : h * hd + NOPE] = qv[:, h * NOPE : (h + 1) * NOPE].astype(BF16)
            qc_ref[:, h * hd + NOPE : (h + 1) * hd] = qr[:, h * HEAD_PAD : (h + 1) * HEAD_PAD]
            kc_ref[:, h * hd : h * hd + NOPE] = kv_ref[:, h * NOPE : (h + 1) * NOPE]
            kc_ref[:, h * hd + NOPE : (h + 1) * hd] = krv

    out = jax.ShapeDtypeStruct((s, n_heads * hd), BF16)
    return pl.pallas_call(
        body,
        name="head_cat",
        out_shape=(out, out),
        grid=(s // ts,),
        in_specs=[_rows(ts, w2), _rows(ts, w), _rows(ts, HEAD_PAD)] + [_rows(ts, LANES)] * 3,
        out_specs=[_rows(ts, n_heads * hd)] * 2,
        compiler_params=_cp("parallel"),
    )(q, kv, kr, *tabs)


def _attn_fwd(qc, kc, kv, n_heads, cat_cols):
    s = qc.shape[0]
    t = _tile(s, ATT_BLOCK, LANES)
    sub = _tile(t, ATT_SUB, LANES)
    hh = n_heads
    hd = NOPE + HEAD_PAD

    def body(q_ref, k_ref, v_ref, o_ref, lse_ref, m_s, l_s, acc_s):
        i = pl.program_id(1)
        m_s[...] = jnp.full(m_s.shape, NEG, F32)
        l_s[...] = jnp.zeros(l_s.shape, F32)
        acc_s[...] = jnp.zeros(acc_s.shape, F32)

        def chunk(k0, diag):
            m_all, l_all, acc_all = m_s[...], l_s[...], acc_s[...]
            new_m, new_l, new_acc = [], [], []
            for r0 in range(0, t, sub):
                ncol = r0 + sub if diag else t
                kk = k_ref[pl.ds(k0, ncol), :]
                sc = lax.dot_general(q_ref[pl.ds(r0, sub), :], kk, NT, preferred_element_type=F32)
                if diag:
                    row = lax.broadcasted_iota(jnp.int32, sc.shape, 0) + r0
                    col = lax.broadcasted_iota(jnp.int32, sc.shape, 1)
                    sc = jnp.where(col <= row, sc, NEG)
                m_prev = m_all[r0 : r0 + sub]
                m_new = jnp.maximum(m_prev, jnp.max(sc, axis=1, keepdims=True))
                alpha = jnp.exp2((m_prev - m_new) * ATT_C2)
                p = jnp.exp2((sc - m_new) * ATT_C2)
                pv = jnp.dot(p.astype(BF16), v_ref[pl.ds(k0, ncol), :], preferred_element_type=F32)
                new_m.append(m_new)
                new_l.append(alpha * l_all[r0 : r0 + sub] + jnp.sum(p, axis=1, keepdims=True))
                new_acc.append(alpha * acc_all[r0 : r0 + sub] + pv)
            m_s[...] = jnp.concatenate(new_m, axis=0)
            l_s[...] = jnp.concatenate(new_l, axis=0)
            acc_s[...] = jnp.concatenate(new_acc, axis=0)

        def loop_body(k, carry):
            chunk(pl.multiple_of(k * t, t), False)
            return carry

        lax.fori_loop(0, i, loop_body, 0)
        chunk(pl.multiple_of(i * t, t), True)
        l = l_s[...]
        o_ref[...] = (acc_s[...] / l).astype(BF16)
        lse_ref[...] = jnp.broadcast_to(m_s[...] * ATT_C2 + jnp.log(l) * LOG2E, lse_ref.shape)

    return pl.pallas_call(
        body,
        name="attn_fwd",
        out_shape=(jax.ShapeDtypeStruct((s, cat_cols), BF16), jax.ShapeDtypeStruct((s, hh * LANES), F32)),
        grid=(hh, s // t),
        in_specs=[
            pl.BlockSpec((t, hd), lambda h, i: (i, h)),
            pl.BlockSpec((s, hd), lambda h, i: (0, h)),
            pl.BlockSpec((s, VDIM), lambda h, i: (0, hh + h)),
        ],
        out_specs=[pl.BlockSpec((t, VDIM), lambda h, i: (i, h)), pl.BlockSpec((t, LANES), lambda h, i: (i, h))],
        scratch_shapes=[pltpu.VMEM((t, 1), F32), pltpu.VMEM((t, 1), F32), pltpu.VMEM((t, VDIM), F32)],
        compiler_params=_cp("parallel", "parallel"),
    )(qc, kc, kv)


def _attn_bwd_prep(cat, dcat, lse2, n_heads):
    s, w = lse2.shape
    ts = _tile(s, 512, SUBLANES)

    def body(o_ref, do_ref, lse_ref, dob_ref, st_ref):
        do = do_ref[...]
        dob_ref[...] = do.astype(BF16)
        prod = do * o_ref[...].astype(F32)
        lane = lax.broadcasted_iota(jnp.int32, (ts, LANES), 1)
        for h in range(n_heads):
            cols = slice(h * LANES, (h + 1) * LANES)
            dsum = jnp.sum(prod[:, cols], axis=1, keepdims=True)
            st_ref[:, cols] = jnp.where(lane < STAT_SPLIT, lse_ref[:, cols], dsum)

    return pl.pallas_call(
        body,
        name="attn_bwd_prep",
        out_shape=(jax.ShapeDtypeStruct((s, w), BF16), jax.ShapeDtypeStruct((s, w), F32)),
        grid=(s // ts,),
        in_specs=[_rows(ts, w)] * 3,
        out_specs=[_rows(ts, w)] * 2,
        compiler_params=_cp("parallel"),
    )(cat, dcat, lse2)


def _attn_bwd(qc, kc, kv, dob, stats, n_heads):
    s = qc.shape[0]
    t = _tile(s, ATT_BLOCK, LANES)
    sub = _tile(t, ATT_SUB, LANES)
    nb = s // t
    hh = n_heads
    hd = NOPE + HEAD_PAD
    w = hh * LANES

    def body(q_ref, k_ref, v_ref, do_ref, st_ref, dq_ref, dkn_ref, dv_ref, dkr_ref, dk_s, dv_s):
        j = pl.program_id(1)

        @pl.when(j == 0)
        def _():
            dq_ref[...] = jnp.zeros(dq_ref.shape, F32)

        dk_s[...] = jnp.zeros(dk_s.shape, F32)
        dv_s[...] = jnp.zeros(dv_s.shape, F32)
        kk, vv = k_ref[...], v_ref[...]

        def pair(i0, diag):
            for r0 in range(0, t, sub):
                rows = pl.ds(i0 + r0, sub)
                qq, do, st = q_ref[rows, :], do_ref[rows, :], st_ref[rows, :]
                sc = lax.dot_general(qq, kk, NT, preferred_element_type=F32)
                if diag:
                    row = lax.broadcasted_iota(jnp.int32, sc.shape, 0) + r0
                    col = lax.broadcasted_iota(jnp.int32, sc.shape, 1)
                    sc = jnp.where(col <= row, sc, NEG)
                p = jnp.exp2(sc * ATT_C2 - st[:, 0:1])
                dv_s[...] += lax.dot_general(p.astype(BF16), do, TN, preferred_element_type=F32)
                dp = lax.dot_general(do, vv, NT, preferred_element_type=F32)
                ds = (p * (dp - st[:, STAT_SPLIT : STAT_SPLIT + 1]) * ATT_SCALE).astype(BF16)
                dk_s[...] += lax.dot_general(ds, qq, TN, preferred_element_type=F32)
                dq_ref[rows, :] += jnp.dot(ds, kk, preferred_element_type=F32)

        pair(pl.multiple_of(j * t, t), True)

        def loop_body(i, carry):
            pair(pl.multiple_of(i * t, t), False)
            return carry

        lax.fori_loop(j + 1, nb, loop_body, 0)
        dkn_ref[...] = dk_s[:, :NOPE].astype(BF16)
        dv_ref[...] = dv_s[...].astype(BF16)
        dkr_ref[...] = dk_s[:, NOPE:]

    whole = lambda width, off: pl.BlockSpec((s, width), lambda h, j: (0, off + h))
    blk = lambda width, off: pl.BlockSpec((t, width), lambda h, j: (j, off + h))
    return pl.pallas_call(
        body,
        name="attn_bwd",
        out_shape=(
            jax.ShapeDtypeStruct((s, hh * hd), F32),
            jax.ShapeDtypeStruct((s, w), BF16),
            jax.ShapeDtypeStruct((s, w), BF16),
            jax.ShapeDtypeStruct((s, w), F32),
        ),
        grid=(hh, nb),
        in_specs=[whole(hd, 0), blk(hd, 0), blk(VDIM, hh), whole(VDIM, 0), whole(LANES, 0)],
        out_specs=[whole(hd, 0), blk(NOPE, 0), blk(VDIM, 0), blk(HEAD_PAD, 0)],
        scratch_shapes=[pltpu.VMEM((t, hd), F32), pltpu.VMEM((t, VDIM), F32)],
        compiler_params=_cp("parallel", "arbitrary"),
    )(qc, kc, kv, dob, stats)


def _dq_unrope(dq, tabs, n_heads):
    s = dq.shape[0]
    hd = NOPE + HEAD_PAD
    w = n_heads * LANES
    ts = _tile(s, 512, SUBLANES)

    def body(d_ref, c_ref, sa_ref, sb_ref, o_ref):
        c, sa, sb = c_ref[...], sa_ref[...], sb_ref[...]
        for h in range(n_heads):
            o_ref[:, h * NOPE : (h + 1) * NOPE] = d_ref[:, h * hd : h * hd + NOPE].astype(BF16)
            rot = _rope_t(d_ref[:, h * hd + NOPE : (h + 1) * hd], c, sa, sb)
            o_ref[:, w + h * HEAD_PAD : w + (h + 1) * HEAD_PAD] = rot.astype(BF16)

    return pl.pallas_call(
        body,
        name="dq_unrope",
        out_shape=jax.ShapeDtypeStruct((s, 2 * w), BF16),
        grid=(s // ts,),
        in_specs=[_rows(ts, n_heads * hd)] + [_rows(ts, LANES)] * 3,
        out_specs=_rows(ts, 2 * w),
        compiler_params=_cp("parallel"),
    )(dq, *tabs)


def _adamw(w, m, v, grads, name):
    r, c = w.shape
    budget_rows = max(SUBLANES, (VMEM_LIMIT // 3) // (4 * c * 2 * (7 + len(grads))))
    tr = _tile(r, budget_rows, SUBLANES)
    ng = len(grads)
    c1 = 1.0 - ADAM_B1**ADAM_STEP
    c2 = 1.0 - ADAM_B2**ADAM_STEP

    def body(*refs):
        w_ref, m_ref, v_ref = refs[:3]
        g_ref, d_ref, nm_ref, nv_ref = refs[3 + ng :]
        g = refs[3][...]
        for extra in refs[4 : 3 + ng]:
            g = g + extra[...]
        mn = ADAM_B1 * m_ref[...] + (1.0 - ADAM_B1) * g
        vn = ADAM_B2 * v_ref[...] + (1.0 - ADAM_B2) * (g * g)
        g_ref[...] = g
        nm_ref[...] = mn
        nv_ref[...] = vn
        d_ref[...] = -ADAM_LR * ((mn / c1) / (jnp.sqrt(vn / c2) + ADAM_EPS) + ADAM_WD * w_ref[...])

    blk = pl.BlockSpec((tr, c), lambda i: (i, 0))
    out = jax.ShapeDtypeStruct((r, c), F32)
    return pl.pallas_call(
        body,
        name=name,
        out_shape=(out, out, out, out),
        grid=(r // tr,),
        in_specs=[blk] * (3 + ng),
        out_specs=[blk] * 4,
        compiler_params=_cp("parallel"),
    )(w, m, v, *grads)


def _ada_grad(ca_t, dm):
    d = ca_t.shape[0]
    nc = dm.shape[1]
    tn = _tile(nc, 512, LANES)

    def body(a_ref, b_ref, o_ref):
        o_ref[...] = jnp.dot(a_ref[...].astype(BF16), b_ref[...].astype(BF16), preferred_element_type=F32)

    return pl.pallas_call(
        body,
        name="ada_grad",
        out_shape=jax.ShapeDtypeStruct((d, nc), F32),
        grid=(nc // tn,),
        in_specs=[pl.BlockSpec((d, LANES), lambda j: (0, 0)), pl.BlockSpec((LANES, tn), lambda j: (0, j))],
        out_specs=pl.BlockSpec((d, tn), lambda j: (0, j)),
        compiler_params=_cp("parallel"),
    )(ca_t, dm)


def _sum_devices(g):
    n = g.shape[1]

    def body(g_ref, o_ref):
        acc = g_ref[0:SUBLANES, :]
        for dvc in range(1, N_DEV):
            acc = acc + g_ref[dvc * SUBLANES : (dvc + 1) * SUBLANES, :]
        o_ref[...] = acc

    return pl.pallas_call(
        body,
        name="sum_devices",
        out_shape=jax.ShapeDtypeStruct((SUBLANES, n), F32),
        in_specs=[pl.BlockSpec(memory_space=pltpu.VMEM)],
        out_specs=pl.BlockSpec(memory_space=pltpu.VMEM),
        compiler_params=pltpu.CompilerParams(vmem_limit_bytes=VMEM_LIMIT),
    )(g)


def _sum_chips(land, sent, name):
    _, r, c = land.shape
    tr = _tile(r, max(SUBLANES * 2, (VMEM_LIMIT // 4) // (c * 2 * (4 * N_CHIP + 4 * 2))), SUBLANES * 2)

    def body(l_ref, s_ref, o_ref):
        x, y, _ = _mesh_pos()
        me = 2 * x + y
        acc = jnp.where(me == 0, s_ref[0], l_ref[0]).astype(F32)
        for k in range(1, N_CHIP):
            acc = acc + jnp.where(me == k, s_ref[k], l_ref[k]).astype(F32)
        o_ref[...] = acc

    slots = pl.BlockSpec((N_CHIP, tr, c), lambda i: (0, i, 0))
    return pl.pallas_call(
        body,
        name=name,
        out_shape=jax.ShapeDtypeStruct((r, c), F32),
        grid=(r // tr,),
        in_specs=[slots, slots],
        out_specs=pl.BlockSpec((tr, c), lambda i: (i, 0)),
        compiler_params=_cp("parallel"),
    )(land, sent)


def _mesh_pos():
    return lax.axis_index("x"), lax.axis_index("y"), lax.axis_index("c")


def _other_chips(x, y):
    return [(1 - x, y), (x, 1 - y), (1 - x, 1 - y)]


def _all_gather8(x_shard, name):
    m_per, n = x_shard.shape

    def body(x_ref, out_ref, send_sems, recv_sems, local_sem):
        x, y, c = _mesh_pos()
        me, sibling = (x, y, c), (x, y, 1 - c)
        chips = _other_chips(x, y)

        def rows(px, py, pc):
            return out_ref.at[pl.ds((4 * px + 2 * py + pc) * m_per, m_per), :]

        def copy(k, block, to, src=None):
            return pltpu.make_async_remote_copy(
                src_ref=rows(*block) if src is None else src,
                dst_ref=rows(*block),
                send_sem=send_sems.at[k],
                recv_sem=recv_sems.at[k],
                device_id=to,
                device_id_type=MESH,
            )

        mine = pltpu.make_async_copy(x_ref, rows(*me), local_sem)
        mine.start()
        first = [copy(0, me, sibling, src=x_ref)]
        first += [copy(1 + j, me, (*chip, c), src=x_ref) for j, chip in enumerate(chips)]
        for cp in first:
            cp.start()
        passed = [copy(4 + j, (*chip, c), sibling) for j, chip in enumerate(chips)]
        for j, chip in enumerate(chips):
            copy(1 + j, (*chip, c), me).wait_recv()
            passed[j].start()
        copy(0, sibling, me).wait_recv()
        for j, chip in enumerate(chips):
            copy(4 + j, (*chip, 1 - c), me).wait_recv()
        for cp in first + passed:
            cp.wait_send()
        mine.wait()

    return pl.pallas_call(
        body,
        name=name,
        out_shape=jax.ShapeDtypeStruct((N_DEV * m_per, n), x_shard.dtype),
        in_specs=[pl.BlockSpec(memory_space=pltpu.VMEM)],
        out_specs=pl.BlockSpec(memory_space=pltpu.VMEM),
        scratch_shapes=[pltpu.SemaphoreType.DMA((7,)), pltpu.SemaphoreType.DMA((7,)), pltpu.SemaphoreType.DMA],
        compiler_params=pltpu.CompilerParams(vmem_limit_bytes=VMEM_LIMIT),
    )(x_shard)


HBM_SPEC = pl.BlockSpec(memory_space=pltpu.HBM)
SEM_SPEC = pl.BlockSpec(memory_space=pltpu.SEMAPHORE)
DATAFLOW = pltpu.SideEffectType.DATAFLOW_SIDE_EFFECTING


def _exchange_copies(ins, lands, send_sems, recv_sems, scatter):
    x, y, c = _mesh_pos()
    me = 2 * x + y
    sends, recvs = [], []
    for t in range(len(ins)):
        for r, (px, py) in enumerate(_other_chips(x, y)):
            peer = 2 * px + py

            def copy(src, dst, k=3 * t + r, to=(px, py, c)):
                return pltpu.make_async_remote_copy(
                    src_ref=src, dst_ref=dst, send_sem=send_sems.at[k], recv_sem=recv_sems.at[k], device_id=to, device_id_type=MESH
                )

            sends.append(copy(ins[t].at[peer] if scatter else ins[t], lands[t].at[me]))
            recvs.append(copy(ins[t].at[me] if scatter else ins[t], lands[t].at[peer]))
    return sends, recvs


def _exchange_start(arrs, scatter, name):
    nt = len(arrs)
    lands = [lax.empty(a.shape if scatter else (N_CHIP, *a.shape), a.dtype) for a in arrs]

    def body(*refs):
        ins, zones = refs[:nt], refs[nt : 2 * nt]
        send_sems, recv_sems, token = refs[2 * nt], refs[2 * nt + 1], refs[-1]
        sends, _ = _exchange_copies(ins, zones, send_sems, recv_sems, scatter)
        for cp in sends:
            cp.start()
        token[...] = jnp.zeros(token.shape, F32)

    bufs = list(arrs) + list(lands)
    return pl.pallas_call(
        body,
        name=name,
        out_shape=(
            pltpu.SemaphoreType.DMA((3 * nt,)),
            pltpu.SemaphoreType.DMA((3 * nt,)),
            *[pltpu.HBM(a.shape, a.dtype) for a in bufs],
            jax.ShapeDtypeStruct((SUBLANES, LANES), F32),
        ),
        in_specs=[HBM_SPEC] * (2 * nt),
        out_specs=(SEM_SPEC, SEM_SPEC, *[HBM_SPEC] * (2 * nt), pl.BlockSpec(memory_space=pltpu.VMEM)),
        input_output_aliases={k: 2 + k for k in range(2 * nt)},
        compiler_params=pltpu.CompilerParams(has_side_effects=DATAFLOW),
    )(*[pltpu.with_memory_space_constraint(a, pltpu.HBM) for a in bufs])


def _exchange_wait(state, after, scatter, name):
    send_sems, recv_sems, *bufs = state[:-1]
    nt = len(bufs) // 2

    def body(*refs):
        ins, zones = refs[:nt], refs[nt : 2 * nt]
        sends, recvs = _exchange_copies(ins, zones, refs[2 * nt], refs[2 * nt + 1], scatter)
        for cp in sends:
            cp.wait_send()
        for cp in recvs:
            cp.wait_recv()

    out = pl.pallas_call(
        body,
        name=name,
        out_shape=tuple(pltpu.HBM(a.shape, a.dtype) for a in bufs),
        in_specs=[HBM_SPEC] * (2 * nt) + [SEM_SPEC, SEM_SPEC, pl.BlockSpec(memory_space=pl.ANY)],
        out_specs=[HBM_SPEC] * (2 * nt),
        input_output_aliases={k: k for k in range(2 * nt)},
        compiler_params=pltpu.CompilerParams(has_side_effects=DATAFLOW),
    )(*bufs, send_sems, recv_sems, after)
    return list(out[:nt]), list(out[nt:])


def _sibling_swap(arrs, name):
    nt = len(arrs)

    def body(*refs):
        ins, outs = refs[:nt], refs[nt : 2 * nt]
        send_sems, recv_sems = refs[2 * nt :]
        x, y, c = _mesh_pos()
        cps = [
            pltpu.make_async_remote_copy(
                src_ref=ins[t],
                dst_ref=outs[t],
                send_sem=send_sems.at[t],
                recv_sem=recv_sems.at[t],
                device_id=(x, y, 1 - c),
                device_id_type=MESH,
            )
            for t in range(nt)
        ]
        for cp in cps:
            cp.start()
        for cp in cps:
            cp.wait_recv()
        for cp in cps:
            cp.wait_send()

    return pl.pallas_call(
        body,
        name=name,
        out_shape=tuple(jax.ShapeDtypeStruct(a.shape, a.dtype) for a in arrs),
        in_specs=[pl.BlockSpec(memory_space=pl.ANY)] * nt,
        out_specs=[pl.BlockSpec(memory_space=pl.ANY)] * nt,
        scratch_shapes=[pltpu.SemaphoreType.DMA((nt,)), pltpu.SemaphoreType.DMA((nt,))],
    )(*arrs)


def _cols_from_shards(g):
    _, k, n = g.shape
    return jnp.transpose(g, (1, 0, 2)).reshape(k, N_CHIP * n)


def _cols_to_shards(a):
    k, n4 = a.shape
    return jnp.transpose(a.reshape(k, N_CHIP, n4 // N_CHIP), (1, 0, 2))


def _pad_to(vec, mult):
    n = vec.shape[0]
    return jnp.pad(vec, (0, (-n) % mult))


def kernel(x, c, positions, w_ada, b_ada, g_pre_mix, g_post_mix, w_in, g_q, w_uq, g_kv, w_ukv, conv_w_mix, conv_b_mix, w_o, g_pre_ffn, g_post_ffn, w_up, conv_w_ffn, conv_b_ffn, w_down, loss_target, m_w_ada, m_b_ada, m_g_pre_mix, m_g_post_mix, m_w_in, m_g_q, m_w_uq, m_g_kv, m_w_ukv, m_conv_w_mix, m_conv_b_mix, m_w_o, m_g_pre_ffn, m_g_post_ffn, m_w_up, m_conv_w_ffn, m_conv_b_ffn, m_w_down, v_w_ada, v_b_ada, v_g_pre_mix, v_g_post_mix, v_w_in, v_g_q, v_w_uq, v_g_kv, v_w_ukv, v_conv_w_mix, v_conv_b_mix, v_w_o, v_g_pre_ffn, v_g_post_ffn, v_w_up, v_conv_w_ffn, v_conv_b_ffn, v_w_down):
    xi, yi, ci = _mesh_pos()
    chip = 2 * xi + yi
    dev = 4 * xi + 2 * yi + ci

    s, d = x.shape[1], x.shape[2]
    ql, kl = g_q.shape[1], g_kv.shape[1]
    cwid = conv_b_mix.shape[1]
    f2 = conv_b_ffn.shape[1]
    hh = (w_uq.shape[2] * N_CHIP) // (NOPE + ROPE)
    w_att = hh * LANES
    nc_ada = w_ada.shape[2]
    lat = ql + kl + ROPE
    tc_mix = _tile(cwid, 512, LANES)
    lb = -(-(ql + kl + HEAD_PAD) // tc_mix) * tc_mix
    np_cols = lb + 3 * cwid
    assert cwid == hh * VDIM and w_att % tc_mix == 0

    x0 = x.reshape(s, d)
    tgt = loss_target.reshape(s, d)

    anchors = []

    def _behind(val, state):
        val, tok = lax.optimization_barrier((val, state[-1]))
        anchors.append(tok[0, 0])
        return val

    cwm_n, cwf_n = CONV_K * cwid // N_CHIP, CONV_K * f2 // N_CHIP
    pack_a = _pad_to(jnp.concatenate([c.reshape(-1), conv_w_mix.reshape(-1), conv_w_ffn.reshape(-1)]), SUBLANES * LANES)
    rows_a = _all_gather8(pack_a.reshape(SUBLANES, -1), "ag8_inputs").reshape(N_DEV, -1)
    c_all = rows_a[:, :d]
    south = rows_a[0::2]
    cw_mix = jnp.concatenate([south[j, d : d + cwm_n].reshape(CONV_K, -1) for j in range(N_CHIP)], axis=1)
    cw_ffn = jnp.concatenate([south[j, d + cwm_n : d + cwm_n + cwf_n].reshape(CONV_K, -1) for j in range(N_CHIP)], axis=1)

    b_cols = lax.dynamic_slice(b_ada, (0, chip * nc_ada), (1, nc_ada))
    mod_part, c_act = _ada_fwd(c_all, w_ada[0], b_cols)
    mod_rows = _all_gather8(mod_part, "ag8_mod")
    mod = jnp.concatenate(
        [lax.dynamic_slice_in_dim(mod_rows, 2 * N_DEV * j + dev, 1, axis=0) for j in range(N_CHIP)], axis=1
    )

    shards = [a[0].astype(BF16) for a in (w_in, w_uq, w_ukv, w_o, w_up, w_down)]
    shards, mod = lax.optimization_barrier((shards, mod))
    ag_a = _exchange_start(shards[:3], False, "ag_a_start")
    mod = _behind(mod, ag_a)
    sh_m, sc_m, gt_m, sh_f, sc_f, gt_f = [mod[:, k * d : (k + 1) * d] for k in range(N_MOD)]

    inv_freq = 1.0 / (ROPE_THETA ** (jnp.arange(0, ROPE, 2, dtype=F32) / ROPE))
    invf = jnp.concatenate([inv_freq, inv_freq, jnp.zeros((LANES - ROPE,), F32)]).reshape(1, LANES)
    tabs = _rope_tables(positions.astype(F32).reshape(s, 1), invf)
    h1 = _pre_fwd(x0, g_pre_mix, sc_m, sh_m)

    def with_own(landed, own):
        return [lax.dynamic_update_slice_in_dim(g, a[None], chip, axis=0) for g, a in zip(landed, own)]

    own_w, landed_w = _exchange_wait(ag_a, h1, False, "ag_a_wait")
    rest, landed_w = lax.optimization_barrier((shards[3:], landed_w))
    ag_b = _exchange_start(rest, False, "ag_b_start")
    h1 = _behind(h1, ag_b)
    g_in, g_uq, g_ukv = with_own(landed_w, own_w)
    full_in = _cols_from_shards(g_in)
    w_in_p = jnp.concatenate([full_in[:, :lat], jnp.zeros((d, lb - lat), BF16), full_in[:, lat:]], axis=1)
    full_uq = _cols_from_shards(g_uq).reshape(ql, hh, NOPE + ROPE)
    w_uq_p = jnp.concatenate(
        [
            full_uq[:, :, :NOPE].reshape(ql, w_att),
            jnp.pad(full_uq[:, :, NOPE:], ((0, 0), (0, 0), (0, HEAD_PAD - ROPE))).reshape(ql, w_att),
        ],
        axis=1,
    )
    full_ukv = _cols_from_shards(g_ukv).reshape(kl, hh, NOPE + VDIM)
    w_ukv_p = jnp.concatenate([full_ukv[:, :, :NOPE].reshape(kl, w_att), full_ukv[:, :, NOPE:].reshape(kl, w_att)], axis=1)

    proj = _matmul(h1, w_in_p, out_dtype=F32, tm=1024, tn=768, tk=2048, name="mm_proj")
    qn, kvn, kr = _latent_fwd(proj, g_q, g_kv, tabs, lb)
    q_f = _matmul(qn, w_uq_p, out_dtype=F32, tm=1024, tn=1024, tk=2048, name="mm_q")
    kv_p = _matmul(kvn, w_ukv_p, out_dtype=BF16, tm=1024, tn=1024, tk=2048, name="mm_kv")
    q_c, k_c = _head_cat(q_f, kv_p, kr, tabs, hh)
    cat, lse2 = _attn_fwd(q_c, k_c, kv_p, hh, w_att + cwid)
    cat = _mixer_fwd(cat, proj, cw_mix, conv_b_mix, lb, w_att)
    own_w, landed_w = _exchange_wait(ag_b, cat, False, "ag_b_wait")
    g_o, g_up, g_down = with_own(landed_w, own_w)
    w_o_f = g_o.reshape(-1, d)
    w_up_f = _pair_cols(_cols_from_shards(g_up))
    cw_ffn_p, cb_ffn_p = _pair_cols(cw_ffn), _pair_cols(conv_b_ffn)
    w_down_f = g_down.reshape(-1, d)
    mix = _matmul(cat, w_o_f, out_dtype=F32, tm=1024, tn=1024, tk=2048, name="mm_mix")

    x1, h2 = _mid_fwd(x0, mix, g_post_mix, gt_m, g_pre_ffn, sc_f, sh_f)
    up = _matmul(h2, w_up_f, out_dtype=F32, tm=1024, tn=1408, tk=2048, name="mm_up")
    act = _ffn_act_fwd(up, cw_ffn_p, cb_ffn_p)
    y = _matmul(act, w_down_f, out_dtype=F32, tm=1024, tn=1024, tk=1408, name="mm_down")
    dx2, dy, s_fin = _final(x1, y, tgt, g_post_ffn, gt_f)

    dw_down = _matmul(act, dy, ta=True, out_dtype=BF16, tm=1408, tn=1024, tk=1024, name="mm_dw_down")
    dact = _matmul(dy, w_down_f, tb=True, out_dtype=F32, tm=1024, tn=1408, tk=2048, name="mm_dact")
    dup, s_ffn_p = _ffn_act_bwd(dact, up, cw_ffn_p, cb_ffn_p)
    s_ffn = _unpair_cols(s_ffn_p)
    dw_up = _matmul(h2, dup, ta=True, out_dtype=BF16, tm=1024, tn=1408, tk=1024, name="mm_dw_up")
    dh2 = _matmul(dup, w_up_f, tb=True, out_dtype=F32, tm=1024, tn=1024, tk=1408, name="mm_dh2")
    dx1, dmix, s_mid = _mid_bwd(dh2, dx2, x1, mix, g_pre_ffn, sc_f, g_post_mix, gt_m)

    dw_o = _matmul(cat, dmix, ta=True, out_dtype=BF16, tm=1024, tn=1024, tk=1024, name="mm_dw_o")
    send_b = [dw_o.reshape(N_CHIP, -1, d), _cols_to_shards(_unpair_cols(dw_up)), dw_down.reshape(N_CHIP, -1, d)]
    rs_b = _exchange_start(send_b, True, "rs_b_start")
    dmix = _behind(dmix, rs_b)
    dcat = _matmul(dmix, w_o_f, tb=True, out_dtype=F32, tm=1024, tn=1024, tk=2048, name="mm_dcat")
    dp_b, dp_c, dp_i, s_mix = _mixer_bwd(dcat, proj, cw_mix, conv_b_mix, lb, w_att)
    dob, stats = _attn_bwd_prep(cat, dcat, lse2, hh)
    dq_raw, dkv_k, dkv_v, dkr_h = _attn_bwd(q_c, k_c, kv_p, dob, stats, hh)
    dkv_p = jnp.concatenate([dkv_k, dkv_v], axis=1)
    dq_p = _dq_unrope(dq_raw, tabs, hh)
    dw_uq_p = _matmul(qn, dq_p, ta=True, out_dtype=BF16, tm=1024, tn=1024, tk=1024, name="mm_dw_uq")
    dqn = _matmul(dq_p, w_uq_p, tb=True, out_dtype=F32, tm=1024, tn=1024, tk=2048, name="mm_dqn")
    dw_ukv_p = _matmul(kvn, dkv_p, ta=True, out_dtype=BF16, tm=1024, tn=1024, tk=1024, name="mm_dw_ukv")
    dkvn = _matmul(dkv_p, w_ukv_p, tb=True, out_dtype=F32, tm=1024, tn=1024, tk=2048, name="mm_dkvn")
    dp_lat, s_lat = _latent_bwd(proj, dqn, dkvn, dkr_h, g_q, g_kv, tabs, lb)
    dproj = jnp.concatenate([dp_lat, dp_b, dp_c, dp_i], axis=1)
    dw_in_p = _matmul(h1, dproj, ta=True, out_dtype=BF16, tm=1024, tn=1536, tk=1024, name="mm_dw_in")

    dw_in_f = jnp.concatenate([dw_in_p[:, :lat], dw_in_p[:, lb:]], axis=1)
    uq3 = dw_uq_p.reshape(ql, 2, hh, LANES)
    dw_uq_f = jnp.concatenate([uq3[:, 0], uq3[:, 1, :, :ROPE]], axis=2).reshape(ql, hh * (NOPE + ROPE))
    ukv3 = dw_ukv_p.reshape(kl, 2, hh, LANES)
    dw_ukv_f = jnp.concatenate([ukv3[:, 0], ukv3[:, 1]], axis=2).reshape(kl, hh * (NOPE + VDIM))
    send_a = [_cols_to_shards(dw_in_f), _cols_to_shards(dw_uq_f), _cols_to_shards(dw_ukv_f)]
    rs_a = _exchange_start(send_a, True, "rs_a_start")
    dproj = _behind(dproj, rs_a)

    dh1 = _matmul(dproj, w_in_p, tb=True, out_dtype=F32, tm=1024, tn=1024, tk=1536, name="mm_dh1")
    grad_x, s_first = _first_bwd(dh1, dx1, x0, g_pre_mix, sc_m)

    names = ["w_in", "w_uq", "w_ukv", "w_o", "w_up", "w_down"]
    sent_b, landed_b = _exchange_wait(rs_b, s_first, True, "rs_b_wait")
    sent_a, landed_a = _exchange_wait(rs_a, landed_b[0], True, "rs_a_wait")
    landed_a, s_first = lax.optimization_barrier((landed_a, s_first))
    part = [_sum_chips(l, a, "sum_chips_" + n) for l, a, n in zip(landed_a + landed_b, sent_a + sent_b, names)]
    other = _sibling_swap(part, "sibling_swap")

    dmod = jnp.concatenate([s_first[0:1], s_first[1:2], s_mid[3:4], s_mid[0:1], s_mid[1:2], s_fin[0:1]], axis=1)
    small = [
        dmod,
        s_first[2:3],
        s_mid[4:5],
        s_lat[0:1, :ql],
        s_lat[0:1, ql : ql + kl],
        s_mix[3:4],
        s_mid[2:3],
        s_fin[1:2],
        s_ffn[3:4],
        s_mix[0:3].reshape(1, -1),
        s_ffn[0:3].reshape(1, -1),
        s_fin[3:4, :LANES],
    ]
    sizes = [a.shape[1] for a in small]
    offs = [0]
    for n in sizes:
        offs.append(offs[-1] + n)
    pack_g = _pad_to(jnp.concatenate(small, axis=1).reshape(-1), SUBLANES * LANES * SUBLANES).reshape(SUBLANES, -1)
    gathered = _all_gather8(pack_g, "ag8_small_grads")
    tot = _sum_devices(gathered).reshape(-1)
    part_of = lambda k: tot[offs[k] : offs[k + 1]]
    dmod_all = gathered.reshape(N_DEV, -1)[:, : N_MOD * d]
    loss = part_of(11)[0]

    g_b_ada = part_of(0).reshape(1, -1)
    g_vecs = [part_of(k).reshape(1, -1) for k in range(1, 9)]
    g_cw_mix = lax.dynamic_slice(part_of(9).reshape(CONV_K, cwid), (0, chip * (cwid // N_CHIP)), (CONV_K, cwid // N_CHIP))
    g_cw_ffn = lax.dynamic_slice(part_of(10).reshape(CONV_K, f2), (0, chip * (f2 // N_CHIP)), (CONV_K, f2 // N_CHIP))

    dm_cols = lax.dynamic_slice(dmod_all, (0, chip * nc_ada), (N_DEV, nc_ada))
    g_w_ada = _ada_grad(
        jnp.pad(c_act.T, ((0, 0), (0, LANES - N_DEV))), jnp.pad(dm_cols, ((0, LANES - N_DEV), (0, 0)))
    )

    big_w = [w_in, w_uq, w_ukv, w_o, w_up, w_down]
    big_m = [m_w_in, m_w_uq, m_w_ukv, m_w_o, m_w_up, m_w_down]
    big_v = [v_w_in, v_w_uq, v_w_ukv, v_w_o, v_w_up, v_w_down]
    big = {}
    for n, w_, m_, v_, p_, o_ in zip(names, big_w, big_m, big_v, part, other):
        big[n] = [a[None] for a in _adamw(w_[0], m_[0], v_[0], [p_, o_], "adamw_" + n)]
    big["w_ada"] = [a[None] for a in _adamw(w_ada[0], m_w_ada[0], v_w_ada[0], [g_w_ada], "adamw_w_ada")]

    sm_names = ["b_ada", "g_pre_mix", "g_post_mix", "g_q", "g_kv", "conv_b_mix", "g_pre_ffn", "g_post_ffn", "conv_b_ffn",
                "conv_w_mix", "conv_w_ffn"]
    sm_w = [b_ada, g_pre_mix, g_post_mix, g_q, g_kv, conv_b_mix, g_pre_ffn, g_post_ffn, conv_b_ffn, conv_w_mix, conv_w_ffn]
    sm_m = [m_b_ada, m_g_pre_mix, m_g_post_mix, m_g_q, m_g_kv, m_conv_b_mix, m_g_pre_ffn, m_g_post_ffn, m_conv_b_ffn,
            m_conv_w_mix, m_conv_w_ffn]
    sm_v = [v_b_ada, v_g_pre_mix, v_g_post_mix, v_g_q, v_g_kv, v_conv_b_mix, v_g_pre_ffn, v_g_post_ffn, v_conv_b_ffn,
            v_conv_w_mix, v_conv_w_ffn]
    sm_g = [g_b_ada] + g_vecs + [g_cw_mix, g_cw_ffn]
    flat = lambda arrs: jnp.concatenate([a.reshape(1, -1) for a in arrs], axis=1)
    sm_out = _adamw(flat(sm_w), flat(sm_m), flat(sm_v), [flat(sm_g)], "adamw_small")
    sm = {}
    off = 0
    for n, w_ in zip(sm_names, sm_w):
        sm[n] = [o[:, off : off + w_.size].reshape(w_.shape) for o in sm_out]
        off += w_.size

    order = ["w_ada", "b_ada", "g_pre_mix", "g_post_mix", "w_in", "g_q", "w_uq", "g_kv", "w_ukv", "conv_w_mix", "conv_b_mix",
             "w_o", "g_pre_ffn", "g_post_ffn", "w_up", "conv_w_ffn", "conv_b_ffn", "w_down"]
    res = {**big, **sm}
    outs = [loss + sum(anchors), grad_x.reshape(x.shape)]
    for k in range(4):
        outs += [res[n][k] for n in order]
    return tuple(outs)
```

```python
import math

import jax
import jax.numpy as jnp
from jax import lax
from jax.experimental import pallas as pl
from jax.experimental.pallas import tpu as pltpu

F32 = jnp.float32
BF16 = jnp.bfloat16
MESH = pl.DeviceIdType.MESH

N_DEV = 8
N_CHIP = 4
LANES = 128
SUBLANES = 8
VMEM_LIMIT = 56 * 2**20

NOPE = 128
ROPE = 64
VDIM = 128
HEAD_PAD = 128
ROPE_THETA = 10000.0
RMS_EPS = 1e-6
N_MOD = 6
CONV_K = 3
ATT_BLOCK = 512
NEG = -1e30

ADAM_LR = 0.001
ADAM_B1 = 0.9
ADAM_B2 = 0.999
ADAM_EPS = 1e-08
ADAM_WD = 0.01
ADAM_STEP = 10


def _tile(n, pref, align):
    if n <= pref:
        return n
    t = (pref // align) * align
    while t >= align:
        if n % t == 0:
            return t
        t -= align
    return n


def _cp(*sem):
    return pltpu.CompilerParams(dimension_semantics=sem, vmem_limit_bytes=VMEM_LIMIT)


def _rsq(x):
    return lax.rsqrt(jnp.mean(x * x, axis=-1, keepdims=True) + RMS_EPS)


def _norm_bwd(dn, n, r):
    return r * (dn - n * jnp.mean(dn * n, axis=-1, keepdims=True))


def _colsum(a):
    return jnp.sum(a, axis=0, keepdims=True)


def _matmul(a, b, *, ta=False, tb=False, out_dtype, tm, tn, tk, name, b_n_perm=None, b_k_perm=None, out_n_perm=None):
    (k_a, m) = a.shape if ta else a.shape[::-1]
    (n, k_b) = b.shape if tb else b.shape[::-1]
    assert k_a == k_b, (a.shape, b.shape, ta, tb)
    tm, tn, tk = _tile(m, tm, LANES), _tile(n, tn, LANES), _tile(k_a, tk, LANES)
    nk = k_a // tk
    same = lambda t: t
    bn, bk, on = b_n_perm or same, b_k_perm or same, out_n_perm or same
    a_spec = pl.BlockSpec((tk, tm), lambda i, j, k: (k, i)) if ta else pl.BlockSpec((tm, tk), lambda i, j, k: (i, k))
    if tb:
        b_spec = pl.BlockSpec((tn, tk), lambda i, j, k: (bn(j), bk(k)))
    else:
        b_spec = pl.BlockSpec((tk, tn), lambda i, j, k: (bk(k), bn(j)))
    dims = (((0 if ta else 1,), (1 if tb else 0,)), ((), ()))

    def body(a_ref, b_ref, o_ref, *acc):
        p = lax.dot_general(a_ref[...].astype(BF16), b_ref[...].astype(BF16), dims, preferred_element_type=F32)
        if nk == 1:
            o_ref[...] = p.astype(o_ref.dtype)
        else:
            k = pl.program_id(2)

            @pl.when(k == 0)
            def _():
                acc[0][...] = p

            @pl.when(k > 0)
            def _():
                acc[0][...] += p

            @pl.when(k == nk - 1)
            def _():
                o_ref[...] = acc[0][...].astype(o_ref.dtype)

    return pl.pallas_call(
        body,
        name=name,
        out_shape=jax.ShapeDtypeStruct((m, n), out_dtype),
        grid=(m // tm, n // tn, nk),
        in_specs=[a_spec, b_spec],
        out_specs=pl.BlockSpec((tm, tn), lambda i, j, k: (i, on(j))),
        scratch_shapes=[] if nk == 1 else [pltpu.VMEM((tm, tn), F32)],
        compiler_params=_cp("parallel", "parallel", "arbitrary"),
    )(a, b)


def _rope_tables(pos_col, invf):
    s = pos_col.shape[0]
    ts = _tile(s, 1024, SUBLANES)
    half = ROPE // 2

    def body(p_ref, f_ref, c_ref, sa_ref, sb_ref):
        ang = p_ref[...] * f_ref[...]
        lane = lax.broadcasted_iota(jnp.int32, ang.shape, 1)
        cs, sn = jnp.cos(ang), jnp.sin(ang)
        c_ref[...] = jnp.where(lane < ROPE, cs, 0.0)
        sa_ref[...] = jnp.where((lane >= half) & (lane < ROPE), sn, 0.0)
        sb_ref[...] = jnp.where(lane < half, -sn, 0.0)

    tab = jax.ShapeDtypeStruct((s, LANES), F32)
    return pl.pallas_call(
        body,
        name="rope_tables",
        out_shape=(tab, tab, tab),
        grid=(s // ts,),
        in_specs=[pl.BlockSpec((ts, 1), lambda i: (i, 0)), pl.BlockSpec((1, LANES), lambda i: (0, 0))],
        out_specs=[pl.BlockSpec((ts, LANES), lambda i: (i, 0))] * 3,
        compiler_params=_cp("parallel"),
    )(pos_col, invf)


def _widen(t, w):
    return t if w == LANES else jnp.tile(t, (1, w // LANES))


def _rope(x, c, sa, sb):
    w = x.shape[1]
    c, sa, sb = _widen(c, w), _widen(sa, w), _widen(sb, w)
    return x * c + pltpu.roll(x, ROPE // 2, 1) * sa + pltpu.roll(x, w - ROPE // 2, 1) * sb


def _rope_t(d, c, sa, sb):
    w = d.shape[1]
    c, sa, sb = _widen(c, w), _widen(sa, w), _widen(sb, w)
    return d * c + pltpu.roll(d * sa, w - ROPE // 2, 1) + pltpu.roll(d * sb, ROPE // 2, 1)


def _ada_fwd(c_all, w, b):
    d, nc = w.shape
    tn = _tile(nc, 512, LANES)

    def body(c_ref, w_ref, b_ref, o_ref, ca_ref):
        cv = c_ref[...]
        ca = cv * jax.nn.sigmoid(cv)
        ca_ref[...] = ca
        o_ref[...] = jnp.dot(ca.astype(BF16), w_ref[...].astype(BF16), preferred_element_type=F32) + b_ref[...]

    return pl.pallas_call(
        body,
        name="ada_fwd",
        out_shape=(jax.ShapeDtypeStruct((N_DEV, nc), F32), jax.ShapeDtypeStruct((N_DEV, d), F32)),
        grid=(nc // tn,),
        in_specs=[
            pl.BlockSpec((N_DEV, d), lambda j: (0, 0)),
            pl.BlockSpec((d, tn), lambda j: (0, j)),
            pl.BlockSpec((1, tn), lambda j: (0, j)),
        ],
        out_specs=[pl.BlockSpec((N_DEV, tn), lambda j: (0, j)), pl.BlockSpec((N_DEV, d), lambda j: (0, 0))],
        compiler_params=_cp("arbitrary"),
    )(c_all, w, b)


def _rows(ts, d):
    return pl.BlockSpec((ts, d), lambda i: (i, 0))


def _vec(d):
    return pl.BlockSpec((1, d), lambda i: (0, 0))


def _sums(d):
    return pl.BlockSpec((SUBLANES, d), lambda i: (0, 0))


def _acc_rows(ref, i, rows):
    @pl.when(i == 0)
    def _():
        ref[...] = jnp.zeros(ref.shape, ref.dtype)

    for k, r in enumerate(rows):
        ref[k : k + 1, :] += r


def _pre_fwd(x, g, sc, sh):
    s, d = x.shape
    ts = _tile(s, 512, SUBLANES)

    def body(x_ref, g_ref, sc_ref, sh_ref, h_ref):
        xv = x_ref[...]
        h_ref[...] = (((xv * _rsq(xv)) * g_ref[...]) * (1.0 + sc_ref[...]) + sh_ref[...]).astype(BF16)

    return pl.pallas_call(
        body,
        name="pre_mix_fwd",
        out_shape=jax.ShapeDtypeStruct((s, d), BF16),
        grid=(s // ts,),
        in_specs=[_rows(ts, d), _vec(d), _vec(d), _vec(d)],
        out_specs=_rows(ts, d),
        compiler_params=_cp("parallel"),
    )(x, g, sc, sh)


def _mid_fwd(x0, mix, g_post, gt, g_pre, sc, sh):
    s, d = x0.shape
    ts = _tile(s, 256, SUBLANES)

    def body(x_ref, m_ref, gp_ref, gt_ref, g_ref, sc_ref, sh_ref, x1_ref, h_ref):
        mv = m_ref[...]
        x1 = x_ref[...] + gt_ref[...] * ((mv * _rsq(mv)) * gp_ref[...])
        x1_ref[...] = x1
        h_ref[...] = (((x1 * _rsq(x1)) * g_ref[...]) * (1.0 + sc_ref[...]) + sh_ref[...]).astype(BF16)

    return pl.pallas_call(
        body,
        name="mid_fwd",
        out_shape=(jax.ShapeDtypeStruct((s, d), F32), jax.ShapeDtypeStruct((s, d), BF16)),
        grid=(s // ts,),
        in_specs=[_rows(ts, d), _rows(ts, d)] + [_vec(d)] * 5,
        out_specs=[_rows(ts, d), _rows(ts, d)],
        compiler_params=_cp("parallel"),
    )(x0, mix, g_post, gt, g_pre, sc, sh)


def _final(x1, y, tgt, g_post, gt):
    s, d = x1.shape
    ts = _tile(s, 256, SUBLANES)
    ni = s // ts

    def body(x_ref, y_ref, t_ref, gp_ref, gt_ref, dx_ref, dy_ref, s_ref):
        i = pl.program_id(0)
        yv, gp, gt_v = y_ref[...], gp_ref[...], gt_ref[...]
        r = _rsq(yv)
        n = yv * r
        err = (x_ref[...] + gt_v * (n * gp)) - t_ref[...]
        dx = err * (1.0 / d)
        dx_ref[...] = dx
        dy_ref[...] = _norm_bwd(dx * (gt_v * gp), n, r).astype(BF16)
        _acc_rows(s_ref, i, [_colsum(dx * (n * gp)), _colsum(dx * gt_v * n), _colsum(err * err)])

        @pl.when(i == ni - 1)
        def _():
            tot = jnp.sum(s_ref[2:3, :], axis=1, keepdims=True) * (0.5 / d)
            s_ref[3:4, :] = jnp.broadcast_to(tot, (1, d))

    return pl.pallas_call(
        body,
        name="final_fwd_bwd",
        out_shape=(
            jax.ShapeDtypeStruct((s, d), F32),
            jax.ShapeDtypeStruct((s, d), BF16),
            jax.ShapeDtypeStruct((SUBLANES, d), F32),
        ),
        grid=(ni,),
        in_specs=[_rows(ts, d)] * 3 + [_vec(d)] * 2,
        out_specs=[_rows(ts, d), _rows(ts, d), _sums(d)],
        compiler_params=_cp("arbitrary"),
    )(x1, y, tgt, g_post, gt)


def _mid_bwd(dh2, dx2, x1, mix, g_pre, sc, g_post, gt):
    s, d = x1.shape
    ts = _tile(s, 256, SUBLANES)

    def body(dh_ref, dx2_ref, x_ref, m_ref, g_ref, sc_ref, gp_ref, gt_ref, dx1_ref, dm_ref, s_ref):
        i = pl.program_id(0)
        dh, xv, mv = dh_ref[...], x_ref[...], m_ref[...]
        g, sc_v, gp, gt_v = g_ref[...], sc_ref[...], gp_ref[...], gt_ref[...]
        r1 = _rsq(xv)
        n1 = xv * r1
        dx1 = dx2_ref[...] + _norm_bwd(dh * (g * (1.0 + sc_v)), n1, r1)
        dx1_ref[...] = dx1
        rm = _rsq(mv)
        nm = mv * rm
        dm_ref[...] = _norm_bwd(dx1 * (gt_v * gp), nm, rm).astype(BF16)
        _acc_rows(
            s_ref,
            i,
            [
                _colsum(dh),
                _colsum(dh * (n1 * g)),
                _colsum(dh * (1.0 + sc_v) * n1),
                _colsum(dx1 * (nm * gp)),
                _colsum(dx1 * gt_v * nm),
            ],
        )

    return pl.pallas_call(
        body,
        name="mid_bwd",
        out_shape=(
            jax.ShapeDtypeStruct((s, d), F32),
            jax.ShapeDtypeStruct((s, d), BF16),
            jax.ShapeDtypeStruct((SUBLANES, d), F32),
        ),
        grid=(s // ts,),
        in_specs=[_rows(ts, d)] * 4 + [_vec(d)] * 4,
        out_specs=[_rows(ts, d), _rows(ts, d), _sums(d)],
        compiler_params=_cp("arbitrary"),
    )(dh2, dx2, x1, mix, g_pre, sc, g_post, gt)


def _first_bwd(dh1, dx1, x0, g, sc):
    s, d = x0.shape
    ts = _tile(s, 256, SUBLANES)

    def body(dh_ref, dx1_ref, x_ref, g_ref, sc_ref, dx_ref, s_ref):
        i = pl.program_id(0)
        dh, xv, gv, sc_v = dh_ref[...], x_ref[...], g_ref[...], sc_ref[...]
        r = _rsq(xv)
        n = xv * r
        dx_ref[...] = dx1_ref[...] + _norm_bwd(dh * (gv * (1.0 + sc_v)), n, r)
        _acc_rows(s_ref, i, [_colsum(dh), _colsum(dh * (n * gv)), _colsum(dh * (1.0 + sc_v) * n)])

    return pl.pallas_call(
        body,
        name="first_bwd",
        out_shape=(jax.ShapeDtypeStruct((s, d), F32), jax.ShapeDtypeStruct((SUBLANES, d), F32)),
        grid=(s // ts,),
        in_specs=[_rows(ts, d)] * 3 + [_vec(d)] * 2,
        out_specs=[_rows(ts, d), _sums(d)],
        compiler_params=_cp("arbitrary"),
    )(dh1, dx1, x0, g, sc)


def _latent_fwd(proj, g_q, g_kv, tabs, lb):
    s = proj.shape[0]
    ql, kl = g_q.shape[1], g_kv.shape[1]
    ts = _tile(s, 512, SUBLANES)

    def body(p_ref, gq_ref, gk_ref, c_ref, sa_ref, sb_ref, q_ref, kv_ref, kr_ref):
        pv = p_ref[...]
        q, kv, kr = pv[:, :ql], pv[:, ql : ql + kl], pv[:, ql + kl : ql + kl + HEAD_PAD]
        q_ref[...] = ((q * _rsq(q)) * gq_ref[...]).astype(BF16)
        kv_ref[...] = ((kv * _rsq(kv)) * gk_ref[...]).astype(BF16)
        kr_ref[...] = _rope(kr, c_ref[...], sa_ref[...], sb_ref[...]).astype(BF16)

    return pl.pallas_call(
        body,
        name="latent_fwd",
        out_shape=(
            jax.ShapeDtypeStruct((s, ql), BF16),
            jax.ShapeDtypeStruct((s, kl), BF16),
            jax.ShapeDtypeStruct((s, HEAD_PAD), BF16),
        ),
        grid=(s // ts,),
        in_specs=[_rows(ts, lb), _vec(ql), _vec(kl)] + [_rows(ts, LANES)] * 3,
        out_specs=[_rows(ts, ql), _rows(ts, kl), _rows(ts, HEAD_PAD)],
        compiler_params=_cp("parallel"),
    )(proj, g_q, g_kv, *tabs)


def _latent_bwd(proj, dqn, dkvn, dkr_h, g_q, g_kv, tabs, lb):
    s = proj.shape[0]
    ql, kl = g_q.shape[1], g_kv.shape[1]
    hw = dkr_h.shape[1]
    ts = _tile(s, 256, SUBLANES)
    pad = lb - ql - kl - HEAD_PAD

    def body(p_ref, dq_ref, dkv_ref, dkr_ref, gq_ref, gk_ref, c_ref, sa_ref, sb_ref, o_ref, s_ref):
        i = pl.program_id(0)
        pv = p_ref[...]
        q, kv = pv[:, :ql], pv[:, ql : ql + kl]
        dqn_v, dkvn_v = dq_ref[...], dkv_ref[...]
        rq = _rsq(q)
        nq = q * rq
        rk = _rsq(kv)
        nk = kv * rk
        dkr = dkr_ref[:, :HEAD_PAD]
        for h in range(1, hw // HEAD_PAD):
            dkr = dkr + dkr_ref[:, h * HEAD_PAD : (h + 1) * HEAD_PAD]
        parts = [
            _norm_bwd(dqn_v * gq_ref[...], nq, rq).astype(BF16),
            _norm_bwd(dkvn_v * gk_ref[...], nk, rk).astype(BF16),
            _rope_t(dkr, c_ref[...], sa_ref[...], sb_ref[...]).astype(BF16),
        ]
        if pad:
            parts.append(jnp.zeros((ts, pad), BF16))
        o_ref[...] = jnp.concatenate(parts, axis=1)
        row = [_colsum(dqn_v * nq), _colsum(dkvn_v * nk), jnp.zeros((1, lb - ql - kl), F32)]
        _acc_rows(s_ref, i, [jnp.concatenate(row, axis=1)])

    return pl.pallas_call(
        body,
        name="latent_bwd",
        out_shape=(jax.ShapeDtypeStruct((s, lb), BF16), jax.ShapeDtypeStruct((SUBLANES, lb), F32)),
        grid=(s // ts,),
        in_specs=[_rows(ts, lb), _rows(ts, ql), _rows(ts, kl), _rows(ts, hw)]
        + [_vec(ql), _vec(kl)]
        + [_rows(ts, LANES)] * 3,
        out_specs=[_rows(ts, lb), _sums(lb)],
        compiler_params=_cp("arbitrary"),
    )(proj, dqn, dkvn, dkr_h, g_q, g_kv, *tabs)


def _conv3(ext, w, b):
    return (pltpu.roll(ext, 2, 0) * w[0:1] + pltpu.roll(ext, 1, 0) * w[1:2]) + ext * w[2:3] + b


def _conv3_t(du, w):
    n = du.shape[0]
    return du * w[2:3] + pltpu.roll(du, n - 1, 0) * w[1:2] + pltpu.roll(du, n - 2, 0) * w[0:1]


def _halo_maps(ts, s):
    r8, last = ts // SUBLANES, s // SUBLANES - 1
    prev = lambda i: jnp.maximum(i * r8 - 1, 0)
    nxt = lambda i: jnp.minimum((i + 1) * r8, last)
    return prev, nxt


def _mixer_fwd(cat, proj, cw, cb, lb, col0):
    s = proj.shape[0]
    cwid = cw.shape[1]
    ts = _tile(s, 512, SUBLANES)
    tc = _tile(cwid, 512, LANES)
    assert lb % tc == 0 and col0 % tc == 0
    nj, ob, oc = cwid // tc, lb // tc, col0 // tc
    prev, _ = _halo_maps(ts, s)

    def body(_, gb_ref, gc_ref, ci_ref, pgc_ref, pci_ref, w_ref, b_ref, o_ref):
        keep = jnp.where(pl.program_id(1) > 0, 1.0, 0.0)
        ext = jnp.concatenate([pgc_ref[...] * pci_ref[...] * keep, gc_ref[...] * ci_ref[...]], axis=0)
        o_ref[...] = (gb_ref[...] * _conv3(ext, w_ref[...], b_ref[...])[SUBLANES:]).astype(BF16)

    def col(k):
        return pl.BlockSpec((ts, tc), lambda j, i: (i, ob + k * nj + j))

    def halo(k):
        return pl.BlockSpec((SUBLANES, tc), lambda j, i: (prev(i), ob + k * nj + j))

    return pl.pallas_call(
        body,
        name="mixer_fwd",
        out_shape=jax.ShapeDtypeStruct(cat.shape, BF16),
        grid=(nj, s // ts),
        in_specs=[pl.BlockSpec(memory_space=pl.ANY), col(0), col(1), col(2), halo(1), halo(2)]
        + [pl.BlockSpec((CONV_K, tc), lambda j, i: (0, j)), pl.BlockSpec((1, tc), lambda j, i: (0, j))],
        out_specs=pl.BlockSpec((ts, tc), lambda j, i: (i, oc + j)),
        input_output_aliases={0: 0},
        compiler_params=_cp("parallel", "arbitrary"),
    )(cat, proj, proj, proj, proj, proj, cw, cb)


def _mixer_bwd(dcat, proj, cw, cb, lb, col0):
    s = proj.shape[0]
    cwid = cw.shape[1]
    ts = _tile(s, 256, SUBLANES)
    tc = _tile(cwid, 512, LANES)
    nj, ob, oc = cwid // tc, lb // tc, col0 // tc
    ni = s // ts
    prev, nxt = _halo_maps(ts, s)

    def body(d_ref, dn_ref, gb_ref, gbn_ref, gc_ref, gcp_ref, gcn_ref, ci_ref, cip_ref, cin_ref, w_ref, b_ref,
             dgb_ref, dgc_ref, dci_ref, s_ref):
        i = pl.program_id(1)
        keep_p = jnp.where(i > 0, 1.0, 0.0)
        keep_n = jnp.where(i < ni - 1, 1.0, 0.0)
        w = w_ref[...]
        gc = jnp.concatenate([gcp_ref[...], gc_ref[...], gcn_ref[...]], axis=0)
        ci = jnp.concatenate([cip_ref[...] * keep_p, ci_ref[...], cin_ref[...]], axis=0)
        u = gc * ci
        cv = _conv3(u, w, b_ref[...])[SUBLANES:]
        dco = jnp.concatenate([d_ref[...], dn_ref[...] * keep_n], axis=0)
        gb = jnp.concatenate([gb_ref[...], gbn_ref[...]], axis=0)
        dgb_ref[...] = (dco * cv)[:ts].astype(BF16)
        dcv = dco * gb
        du = _conv3_t(dcv, w)[:ts]
        dgc_ref[...] = (du * ci_ref[...]).astype(BF16)
        dci_ref[...] = (du * gc_ref[...]).astype(BF16)
        dt = dcv[:ts]
        u1, u2 = pltpu.roll(u, 1, 0), pltpu.roll(u, 2, 0)
        lo, hi = SUBLANES, SUBLANES + ts
        _acc_rows(s_ref, i, [_colsum(dt * u2[lo:hi]), _colsum(dt * u1[lo:hi]), _colsum(dt * u[lo:hi]), _colsum(dt)])

    def col(k):
        return pl.BlockSpec((ts, tc), lambda j, i: (i, ob + k * nj + j))

    def halo(k, which):
        return pl.BlockSpec((SUBLANES, tc), lambda j, i: (which(i), ob + k * nj + j))

    out_col = [pl.BlockSpec((ts, tc), lambda j, i: (i, j))] * 3
    grad = jax.ShapeDtypeStruct((s, cwid), BF16)
    return pl.pallas_call(
        body,
        name="mixer_bwd",
        out_shape=(grad, grad, grad, jax.ShapeDtypeStruct((SUBLANES, cwid), F32)),
        grid=(nj, ni),
        in_specs=[
            pl.BlockSpec((ts, tc), lambda j, i: (i, oc + j)),
            pl.BlockSpec((SUBLANES, tc), lambda j, i: (nxt(i), oc + j)),
            col(0), halo(0, nxt),
            col(1), halo(1, prev), halo(1, nxt),
            col(2), halo(2, prev), halo(2, nxt),
            pl.BlockSpec((CONV_K, tc), lambda j, i: (0, j)),
            pl.BlockSpec((1, tc), lambda j, i: (0, j)),
        ],
        out_specs=out_col + [pl.BlockSpec((SUBLANES, tc), lambda j, i: (0, j))],
        compiler_params=_cp("parallel", "arbitrary"),
    )(dcat, dcat, proj, proj, proj, proj, proj, proj, proj, proj, cw, cb)


def _pair_tile(f):
    return _tile(f, 1408, LANES)


def _pair_perm(f):
    nj = f // _pair_tile(f)
    return lambda p: (p % 2) * nj + p // 2


def _pair_cols(a):
    r, f2 = a.shape
    tc = _pair_tile(f2 // 2)
    return a.reshape(r, 2, f2 // (2 * tc), tc).transpose(0, 2, 1, 3).reshape(r, f2)


def _unpair_cols(a):
    r, f2 = a.shape
    tc = _pair_tile(f2 // 2)
    return a.reshape(r, f2 // (2 * tc), 2, tc).transpose(0, 2, 1, 3).reshape(r, f2)


def _ffn_act_fwd(up, cw, cb):
    s, f2 = up.shape
    f = f2 // 2
    ts = _tile(s, 256, SUBLANES)
    tc = _pair_tile(f)
    prev, _ = _halo_maps(ts, s)

    def body(u_ref, p_ref, w_ref, b_ref, o_ref):
        keep = jnp.where(pl.program_id(1) > 0, 1.0, 0.0)
        ext = jnp.concatenate([p_ref[...] * keep, u_ref[...]], axis=0)
        u = _conv3(ext, w_ref[...], b_ref[...])[SUBLANES:]
        a, g = u[:, :tc], u[:, tc:]
        o_ref[...] = ((g * jax.nn.sigmoid(g)) * a).astype(BF16)

    def pair(rows, which):
        return pl.BlockSpec((rows, 2 * tc), lambda j, i: (which(i), j))

    return pl.pallas_call(
        body,
        name="ffn_act_fwd",
        out_shape=jax.ShapeDtypeStruct((s, f), BF16),
        grid=(f // tc, s // ts),
        in_specs=[pair(ts, lambda i: i), pair(SUBLANES, prev), pair(CONV_K, lambda i: 0), pair(1, lambda i: 0)],
        out_specs=pl.BlockSpec((ts, tc), lambda j, i: (i, j)),
        compiler_params=_cp("parallel", "arbitrary"),
    )(up, up, cw, cb)


def _ffn_act_bwd(dact, up, cw, cb):
    s, f2 = up.shape
    f = f2 // 2
    ts = _tile(s, 128, SUBLANES)
    tc = _pair_tile(f)
    nj, ni = f // tc, s // ts
    prev, nxt = _halo_maps(ts, s)

    def body(d_ref, dn_ref, u_ref, up_ref, un_ref, w_ref, b_ref, dup_ref, s_ref):
        i = pl.program_id(1)
        keep_p = jnp.where(i > 0, 1.0, 0.0)
        keep_n = jnp.where(i < ni - 1, 1.0, 0.0)
        w = w_ref[...]
        ext = jnp.concatenate([up_ref[...] * keep_p, u_ref[...], un_ref[...]], axis=0)
        u = _conv3(ext, w, b_ref[...])[SUBLANES:]
        a, g = u[:, :tc], u[:, tc:]
        dact_v = jnp.concatenate([d_ref[...], dn_ref[...] * keep_n], axis=0)
        sg = jax.nn.sigmoid(g)
        du = jnp.concatenate([dact_v * (g * sg), dact_v * a * (sg * (1.0 + g * (1.0 - sg)))], axis=1)
        dup_ref[...] = _conv3_t(du, w)[:ts].astype(BF16)
        dt = du[:ts]
        lo, hi = SUBLANES, SUBLANES + ts
        e1, e2 = pltpu.roll(ext, 1, 0), pltpu.roll(ext, 2, 0)
        _acc_rows(s_ref, i, [_colsum(dt * e2[lo:hi]), _colsum(dt * e1[lo:hi]), _colsum(dt * ext[lo:hi]), _colsum(dt)])

    def pair(rows, which):
        return pl.BlockSpec((rows, 2 * tc), lambda j, i: (which(i), j))

    return pl.pallas_call(
        body,
        name="ffn_act_bwd",
        out_shape=(jax.ShapeDtypeStruct((s, f2), BF16), jax.ShapeDtypeStruct((SUBLANES, f2), F32)),
        grid=(nj, ni),
        in_specs=[
            pl.BlockSpec((ts, tc), lambda j, i: (i, j)),
            pl.BlockSpec((SUBLANES, tc), lambda j, i: (nxt(i), j)),
            pair(ts, lambda i: i), pair(SUBLANES, prev), pair(SUBLANES, nxt),
            pair(CONV_K, lambda i: 0), pair(1, lambda i: 0),
        ],
        out_specs=[pair(ts, lambda i: i), pair(SUBLANES, lambda i: 0)],
        compiler_params=_cp("parallel", "arbitrary"),
    )(dact, dact, up, up, up, cw, cb)


ATT_SCALE = 1.0 / math.sqrt(NOPE + ROPE)
LOG2E = math.log2(math.e)
ATT_C2 = ATT_SCALE * LOG2E
ATT_SUB = 256
STAT_SPLIT = 64
NT = (((1,), (1,)), ((), ()))
TN = (((0,), (0,)), ((), ()))


def _head_cat(q, kv, kr, tabs, n_heads):
    s, w2 = q.shape
    w = w2 // 2
    ts = _tile(s, 512, SUBLANES)
    hd = NOPE + HEAD_PAD

    def body(q_ref, kv_ref, kr_ref, c_ref, sa_ref, sb_ref, qc_ref, kc_ref):
        qv = q_ref[...]
        qr = _rope(qv[:, w:], c_ref[...], sa_ref[...], sb_ref[...]).astype(BF16)
        krv = kr_ref[...]
        for h in range(n_heads):
            qc_ref[:, h * hd : h * hd + NOPE] = qv[:, h * NOPE : (h + 1) * NOPE].astype(BF16)
            qc_ref[:, h * hd + NOPE : (h + 1) * hd] = qr[:, h * HEAD_PAD : (h + 1) * HEAD_PAD]
            kc_ref[:, h * hd : h * hd + NOPE] = kv_ref[:, h * NOPE : (h + 1) * NOPE]
            kc_ref[:, h * hd + NOPE : (h + 1) * hd] = krv

    out = jax.ShapeDtypeStruct((s, n_heads * hd), BF16)
    return pl.pallas_call(
        body,
        name="head_cat",
        out_shape=(out, out),
        grid=(s // ts,),
        in_specs=[_rows(ts, w2), _rows(ts, w), _rows(ts, HEAD_PAD)] + [_rows(ts, LANES)] * 3,
        out_specs=[_rows(ts, n_heads * hd)] * 2,
        compiler_params=_cp("parallel"),
    )(q, kv, kr, *tabs)


def _attn_fwd(qc, kc, kv, n_heads, cat_cols):
    s = qc.shape[0]
    t = _tile(s, ATT_BLOCK, LANES)
    sub = _tile(t, ATT_SUB, LANES)
    hh = n_heads
    hd = NOPE + HEAD_PAD

    def body(q_ref, k_ref, v_ref, o_ref, lse_ref, m_s, l_s, acc_s):
        i = pl.program_id(1)
        m_s[...] = jnp.full(m_s.shape, NEG, F32)
        l_s[...] = jnp.zeros(l_s.shape, F32)
        acc_s[...] = jnp.zeros(acc_s.shape, F32)

        def chunk(k0, diag):
            m_all, l_all, acc_all = m_s[...], l_s[...], acc_s[...]
            new_m, new_l, new_acc = [], [], []
            for r0 in range(0, t, sub):
                ncol = r0 + sub if diag else t
                kk = k_ref[pl.ds(k0, ncol), :]
                sc = lax.dot_general(q_ref[pl.ds(r0, sub), :], kk, NT, preferred_element_type=F32)
                if diag:
                    row = lax.broadcasted_iota(jnp.int32, sc.shape, 0) + r0
                    col = lax.broadcasted_iota(jnp.int32, sc.shape, 1)
                    sc = jnp.where(col <= row, sc, NEG)
                m_prev = m_all[r0 : r0 + sub]
                m_new = jnp.maximum(m_prev, jnp.max(sc, axis=1, keepdims=True))
                alpha = jnp.exp2((m_prev - m_new) * ATT_C2)
                p = jnp.exp2((sc - m_new) * ATT_C2)
                pv = jnp.dot(p.astype(BF16), v_ref[pl.ds(k0, ncol), :], preferred_element_type=F32)
                new_m.append(m_new)
                new_l.append(alpha * l_all[r0 : r0 + sub] + jnp.sum(p, axis=1, keepdims=True))
                new_acc.append(alpha * acc_all[r0 : r0 + sub] + pv)
            m_s[...] = jnp.concatenate(new_m, axis=0)
            l_s[...] = jnp.concatenate(new_l, axis=0)
            acc_s[...] = jnp.concatenate(new_acc, axis=0)

        def loop_body(k, carry):
            chunk(pl.multiple_of(k * t, t), False)
            return carry

        lax.fori_loop(0, i, loop_body, 0)
        chunk(pl.multiple_of(i * t, t), True)
        l = l_s[...]
        o_ref[...] = (acc_s[...] / l).astype(BF16)
        lse_ref[...] = jnp.broadcast_to(m_s[...] * ATT_C2 + jnp.log(l) * LOG2E, lse_ref.shape)

    return pl.pallas_call(
        body,
        name="attn_fwd",
        out_shape=(jax.ShapeDtypeStruct((s, cat_cols), BF16), jax.ShapeDtypeStruct((s, hh * LANES), F32)),
        grid=(hh, s // t),
        in_specs=[
            pl.BlockSpec((t, hd), lambda h, i: (i, h)),
            pl.BlockSpec((s, hd), lambda h, i: (0, h)),
            pl.BlockSpec((s, VDIM), lambda h, i: (0, hh + h)),
        ],
        out_specs=[pl.BlockSpec((t, VDIM), lambda h, i: (i, h)), pl.BlockSpec((t, LANES), lambda h, i: (i, h))],
        scratch_shapes=[pltpu.VMEM((t, 1), F32), pltpu.VMEM((t, 1), F32), pltpu.VMEM((t, VDIM), F32)],
        compiler_params=_cp("parallel", "parallel"),
    )(qc, kc, kv)


def _attn_bwd_prep(cat, dcat, lse2, n_heads):
    s, w = lse2.shape
    ts = _tile(s, 512, SUBLANES)

    def body(o_ref, do_ref, lse_ref, dob_ref, st_ref):
        do = do_ref[...]
        dob_ref[...] = do.astype(BF16)
        prod = do * o_ref[...].astype(F32)
        lane = lax.broadcasted_iota(jnp.int32, (ts, LANES), 1)
        for h in range(n_heads):
            cols = slice(h * LANES, (h + 1) * LANES)
            dsum = jnp.sum(prod[:, cols], axis=1, keepdims=True)
            st_ref[:, cols] = jnp.where(lane < STAT_SPLIT, lse_ref[:, cols], dsum)

    return pl.pallas_call(
        body,
        name="attn_bwd_prep",
        out_shape=(jax.ShapeDtypeStruct((s, w), BF16), jax.ShapeDtypeStruct((s, w), F32)),
        grid=(s // ts,),
        in_specs=[_rows(ts, w)] * 3,
        out_specs=[_rows(ts, w)] * 2,
        compiler_params=_cp("parallel"),
    )(cat, dcat, lse2)


def _attn_bwd(qc, kc, kv, dob, stats, n_heads):
    s = qc.shape[0]
    t = _tile(s, ATT_BLOCK, LANES)
    sub = _tile(t, ATT_SUB, LANES)
    nb = s // t
    hh = n_heads
    hd = NOPE + HEAD_PAD
    w = hh * LANES

    def body(q_ref, k_ref, v_ref, do_ref, st_ref, dq_ref, dkn_ref, dv_ref, dkr_ref, dk_s, dv_s):
        j = pl.program_id(1)

        @pl.when(j == 0)
        def _():
            dq_ref[...] = jnp.zeros(dq_ref.shape, F32)

        dk_s[...] = jnp.zeros(dk_s.shape, F32)
        dv_s[...] = jnp.zeros(dv_s.shape, F32)
        kk, vv = k_ref[...], v_ref[...]

        def pair(i0, diag):
            for r0 in range(0, t, sub):
                rows = pl.ds(i0 + r0, sub)
                qq, do, st = q_ref[rows, :], do_ref[rows, :], st_ref[rows, :]
                sc = lax.dot_general(qq, kk, NT, preferred_element_type=F32)
                if diag:
                    row = lax.broadcasted_iota(jnp.int32, sc.shape, 0) + r0
                    col = lax.broadcasted_iota(jnp.int32, sc.shape, 1)
                    sc = jnp.where(col <= row, sc, NEG)
                p = jnp.exp2(sc * ATT_C2 - st[:, 0:1])
                dv_s[...] += lax.dot_general(p.astype(BF16), do, TN, preferred_element_type=F32)
                dp = lax.dot_general(do, vv, NT, preferred_element_type=F32)
                ds = (p * (dp - st[:, STAT_SPLIT : STAT_SPLIT + 1]) * ATT_SCALE).astype(BF16)
                dk_s[...] += lax.dot_general(ds, qq, TN, preferred_element_type=F32)
                dq_ref[rows, :] += jnp.dot(ds, kk, preferred_element_type=F32)

        pair(pl.multiple_of(j * t, t), True)

        def loop_body(i, carry):
            pair(pl.multiple_of(i * t, t), False)
            return carry

        lax.fori_loop(j + 1, nb, loop_body, 0)
        dkn_ref[...] = dk_s[:, :NOPE].astype(BF16)
        dv_ref[...] = dv_s[...].astype(BF16)
        dkr_ref[...] = dk_s[:, NOPE:]

    whole = lambda width, off: pl.BlockSpec((s, width), lambda h, j: (0, off + h))
    blk = lambda width, off: pl.BlockSpec((t, width), lambda h, j: (j, off + h))
    return pl.pallas_call(
        body,
        name="attn_bwd",
        out_shape=(
            jax.ShapeDtypeStruct((s, hh * hd), F32),
            jax.ShapeDtypeStruct((s, w), BF16),
            jax.ShapeDtypeStruct((s, w), BF16),
            jax.ShapeDtypeStruct((s, w), F32),
        ),
        grid=(hh, nb),
        in_specs=[whole(hd, 0), blk(hd, 0), blk(VDIM, hh), whole(VDIM, 0), whole(LANES, 0)],
        out_specs=[whole(hd, 0), blk(NOPE, 0), blk(VDIM, 0), blk(HEAD_PAD, 0)],
        scratch_shapes=[pltpu.VMEM((t, hd), F32), pltpu.VMEM((t, VDIM), F32)],
        compiler_params=_cp("parallel", "arbitrary"),
    )(qc, kc, kv, dob, stats)


def _dq_unrope(dq, tabs, n_heads):
    s = dq.shape[0]
    hd = NOPE + HEAD_PAD
    w = n_heads * LANES
    ts = _tile(s, 512, SUBLANES)

    def body(d_ref, c_ref, sa_ref, sb_ref, o_ref):
        c, sa, sb = c_ref[...], sa_ref[...], sb_ref[...]
        for h in range(n_heads):
            o_ref[:, h * NOPE : (h + 1) * NOPE] = d_ref[:, h * hd : h * hd + NOPE].astype(BF16)
            rot = _rope_t(d_ref[:, h * hd + NOPE : (h + 1) * hd], c, sa, sb)
            o_ref[:, w + h * HEAD_PAD : w + (h + 1) * HEAD_PAD] = rot.astype(BF16)

    return pl.pallas_call(
        body,
        name="dq_unrope",
        out_shape=jax.ShapeDtypeStruct((s, 2 * w), BF16),
        grid=(s // ts,),
        in_specs=[_rows(ts, n_heads * hd)] + [_rows(ts, LANES)] * 3,
        out_specs=_rows(ts, 2 * w),
        compiler_params=_cp("parallel"),
    )(dq, *tabs)


def _adamw(w, m, v, grads, name):
    r, c = w.shape
    budget_rows = max(SUBLANES, (VMEM_LIMIT // 3) // (4 * c * 2 * (7 + len(grads))))
    tr = _tile(r, budget_rows, SUBLANES)
    ng = len(grads)
    c1 = 1.0 - ADAM_B1**ADAM_STEP
    c2 = 1.0 - ADAM_B2**ADAM_STEP

    def body(*refs):
        w_ref, m_ref, v_ref = refs[:3]
        g_ref, d_ref, nm_ref, nv_ref = refs[3 + ng :]
        g = refs[3][...]
        for extra in refs[4 : 3 + ng]:
            g = g + extra[...]
        mn = ADAM_B1 * m_ref[...] + (1.0 - ADAM_B1) * g
        vn = ADAM_B2 * v_ref[...] + (1.0 - ADAM_B2) * (g * g)
        g_ref[...] = g
        nm_ref[...] = mn
        nv_ref[...] = vn
        d_ref[...] = -ADAM_LR * ((mn / c1) / (jnp.sqrt(vn / c2) + ADAM_EPS) + ADAM_WD * w_ref[...])

    blk = pl.BlockSpec((tr, c), lambda i: (i, 0))
    out = jax.ShapeDtypeStruct((r, c), F32)
    return pl.pallas_call(
        body,
        name=name,
        out_shape=(out, out, out, out),
        grid=(r // tr,),
        in_specs=[blk] * (3 + ng),
        out_specs=[blk] * 4,
        compiler_params=_cp("parallel"),
    )(w, m, v, *grads)


def _ada_grad(ca_t, dm):
    d = ca_t.shape[0]
    nc = dm.shape[1]
    tn = _tile(nc, 512, LANES)

    def body(a_ref, b_ref, o_ref):
        o_ref[...] = jnp.dot(a_ref[...].astype(BF16), b_ref[...].astype(BF16), preferred_element_type=F32)

    return pl.pallas_call(
        body,
        name="ada_grad",
        out_shape=jax.ShapeDtypeStruct((d, nc), F32),
        grid=(nc // tn,),
        in_specs=[pl.BlockSpec((d, LANES), lambda j: (0, 0)), pl.BlockSpec((LANES, tn), lambda j: (0, j))],
        out_specs=pl.BlockSpec((d, tn), lambda j: (0, j)),
        compiler_params=_cp("parallel"),
    )(ca_t, dm)


def _sum_devices(g):
    n = g.shape[1]

    def body(g_ref, o_ref):
        acc = g_ref[0:SUBLANES, :]
        for dvc in range(1, N_DEV):
            acc = acc + g_ref[dvc * SUBLANES : (dvc + 1) * SUBLANES, :]
        o_ref[...] = acc

    return pl.pallas_call(
        body,
        name="sum_devices",
        out_shape=jax.ShapeDtypeStruct((SUBLANES, n), F32),
        in_specs=[pl.BlockSpec(memory_space=pltpu.VMEM)],
        out_specs=pl.BlockSpec(memory_space=pltpu.VMEM),
        compiler_params=pltpu.CompilerParams(vmem_limit_bytes=VMEM_LIMIT),
    )(g)


def _sum_chips(land, sent, name):
    _, r, c = land.shape
    tr = _tile(r, max(SUBLANES * 2, (VMEM_LIMIT // 4) // (c * 2 * (4 * N_CHIP + 4 * 2))), SUBLANES * 2)

    def body(l_ref, s_ref, o_ref):
        x, y, _ = _mesh_pos()
        me = 2 * x + y
        acc = jnp.where(me == 0, s_ref[0], l_ref[0]).astype(F32)
        for k in range(1, N_CHIP):
            acc = acc + jnp.where(me == k, s_ref[k], l_ref[k]).astype(F32)
        o_ref[...] = acc

    slots = pl.BlockSpec((N_CHIP, tr, c), lambda i: (0, i, 0))
    return pl.pallas_call(
        body,
        name=name,
        out_shape=jax.ShapeDtypeStruct((r, c), F32),
        grid=(r // tr,),
        in_specs=[slots, slots],
        out_specs=pl.BlockSpec((tr, c), lambda i: (i, 0)),
        compiler_params=_cp("parallel"),
    )(land, sent)


def _mesh_pos():
    return lax.axis_index("x"), lax.axis_index("y"), lax.axis_index("c")


def _other_chips(x, y):
    return [(1 - x, y), (x, 1 - y), (1 - x, 1 - y)]


def _all_gather8(x_shard, name):
    m_per, n = x_shard.shape

    def body(x_ref, out_ref, send_sems, recv_sems, local_sem):
        x, y, c = _mesh_pos()
        me, sibling = (x, y, c), (x, y, 1 - c)
        chips = _other_chips(x, y)

        def rows(px, py, pc):
            return out_ref.at[pl.ds((4 * px + 2 * py + pc) * m_per, m_per), :]

        def copy(k, block, to, src=None):
            return pltpu.make_async_remote_copy(
                src_ref=rows(*block) if src is None else src,
                dst_ref=rows(*block),
                send_sem=send_sems.at[k],
                recv_sem=recv_sems.at[k],
                device_id=to,
                device_id_type=MESH,
            )

        mine = pltpu.make_async_copy(x_ref, rows(*me), local_sem)
        mine.start()
        first = [copy(0, me, sibling, src=x_ref)]
        first += [copy(1 + j, me, (*chip, c), src=x_ref) for j, chip in enumerate(chips)]
        for cp in first:
            cp.start()
        passed = [copy(4 + j, (*chip, c), sibling) for j, chip in enumerate(chips)]
        for j, chip in enumerate(chips):
            copy(1 + j, (*chip, c), me).wait_recv()
            passed[j].start()
        copy(0, sibling, me).wait_recv()
        for j, chip in enumerate(chips):
            copy(4 + j, (*chip, 1 - c), me).wait_recv()
        for cp in first + passed:
            cp.wait_send()
        mine.wait()

    return pl.pallas_call(
        body,
        name=name,
        out_shape=jax.ShapeDtypeStruct((N_DEV * m_per, n), x_shard.dtype),
        in_specs=[pl.BlockSpec(memory_space=pltpu.VMEM)],
        out_specs=pl.BlockSpec(memory_space=pltpu.VMEM),
        scratch_shapes=[pltpu.SemaphoreType.DMA((7,)), pltpu.SemaphoreType.DMA((7,)), pltpu.SemaphoreType.DMA],
        compiler_params=pltpu.CompilerParams(vmem_limit_bytes=VMEM_LIMIT),
    )(x_shard)


HBM_SPEC = pl.BlockSpec(memory_space=pltpu.HBM)
SEM_SPEC = pl.BlockSpec(memory_space=pltpu.SEMAPHORE)
DATAFLOW = pltpu.SideEffectType.DATAFLOW_SIDE_EFFECTING


def _exchange_copies(ins, lands, send_sems, recv_sems, scatter):
    x, y, c = _mesh_pos()
    me = 2 * x + y
    sends, recvs = [], []
    for t in range(len(ins)):
        for r, (px, py) in enumerate(_other_chips(x, y)):
            peer = 2 * px + py

            def copy(src, dst, k=3 * t + r, to=(px, py, c)):
                return pltpu.make_async_remote_copy(
                    src_ref=src, dst_ref=dst, send_sem=send_sems.at[k], recv_sem=recv_sems.at[k], device_id=to, device_id_type=MESH
                )

            sends.append(copy(ins[t].at[peer] if scatter else ins[t], lands[t].at[me]))
            recvs.append(copy(ins[t].at[me] if scatter else ins[t], lands[t].at[peer]))
    return sends, recvs


def _exchange_start(arrs, scatter, name):
    nt = len(arrs)
    lands = [lax.empty(a.shape if scatter else (N_CHIP, *a.shape), a.dtype) for a in arrs]

    def body(*refs):
        ins, zones = refs[:nt], refs[nt : 2 * nt]
        send_sems, recv_sems, token = refs[2 * nt], refs[2 * nt + 1], refs[-1]
        sends, _ = _exchange_copies(ins, zones, send_sems, recv_sems, scatter)
        for cp in sends:
            cp.start()
        token[...] = jnp.zeros(token.shape, F32)

    bufs = list(arrs) + list(lands)
    return pl.pallas_call(
        body,
        name=name,
        out_shape=(
            pltpu.SemaphoreType.DMA((3 * nt,)),
            pltpu.SemaphoreType.DMA((3 * nt,)),
            *[pltpu.HBM(a.shape, a.dtype) for a in bufs],
            jax.ShapeDtypeStruct((SUBLANES, LANES), F32),
        ),
        in_specs=[HBM_SPEC] * (2 * nt),
        out_specs=(SEM_SPEC, SEM_SPEC, *[HBM_SPEC] * (2 * nt), pl.BlockSpec(memory_space=pltpu.VMEM)),
        input_output_aliases={k: 2 + k for k in range(2 * nt)},
        compiler_params=pltpu.CompilerParams(has_side_effects=DATAFLOW),
    )(*[pltpu.with_memory_space_constraint(a, pltpu.HBM) for a in bufs])


def _exchange_wait(state, after, scatter, name):
    send_sems, recv_sems, *bufs = state[:-1]
    nt = len(bufs) // 2

    def body(*refs):
        ins, zones = refs[:nt], refs[nt : 2 * nt]
        sends, recvs = _exchange_copies(ins, zones, refs[2 * nt], refs[2 * nt + 1], scatter)
        for cp in sends:
            cp.wait_send()
        for cp in recvs:
            cp.wait_recv()

    out = pl.pallas_call(
        body,
        name=name,
        out_shape=tuple(pltpu.HBM(a.shape, a.dtype) for a in bufs),
        in_specs=[HBM_SPEC] * (2 * nt) + [SEM_SPEC, SEM_SPEC, pl.BlockSpec(memory_space=pl.ANY)],
        out_specs=[HBM_SPEC] * (2 * nt),
        input_output_aliases={k: k for k in range(2 * nt)},
        compiler_params=pltpu.CompilerParams(has_side_effects=DATAFLOW),
    )(*bufs, send_sems, recv_sems, after)
    return list(out[:nt]), list(out[nt:])


def _sibling_swap(arrs, name):
    nt = len(arrs)

    def body(*refs):
        ins, outs = refs[:nt], refs[nt : 2 * nt]
        send_sems, recv_sems = refs[2 * nt :]
        x, y, c = _mesh_pos()
        cps = [
            pltpu.make_async_remote_copy(
                src_ref=ins[t],
                dst_ref=outs[t],
                send_sem=send_sems.at[t],
                recv_sem=recv_sems.at[t],
                device_id=(x, y, 1 - c),
                device_id_type=MESH,
            )
            for t in range(nt)
        ]
        for cp in cps:
            cp.start()
        for cp in cps:
            cp.wait_recv()
        for cp in cps:
            cp.wait_send()

    return pl.pallas_call(
        body,
        name=name,
        out_shape=tuple(jax.ShapeDtypeStruct(a.shape, a.dtype) for a in arrs),
        in_specs=[pl.BlockSpec(memory_space=pl.ANY)] * nt,
        out_specs=[pl.BlockSpec(memory_space=pl.ANY)] * nt,
        scratch_shapes=[pltpu.SemaphoreType.DMA((nt,)), pltpu.SemaphoreType.DMA((nt,))],
    )(*arrs)


def _cols_from_shards(g):
    _, k, n = g.shape
    return jnp.transpose(g, (1, 0, 2)).reshape(k, N_CHIP * n)


def _cols_to_shards(a):
    k, n4 = a.shape
    return jnp.transpose(a.reshape(k, N_CHIP, n4 // N_CHIP), (1, 0, 2))


def _pad_to(vec, mult):
    n = vec.shape[0]
    return jnp.pad(vec, (0, (-n) % mult))


def kernel(x, c, positions, w_ada, b_ada, g_pre_mix, g_post_mix, w_in, g_q, w_uq, g_kv, w_ukv, conv_w_mix, conv_b_mix, w_o, g_pre_ffn, g_post_ffn, w_up, conv_w_ffn, conv_b_ffn, w_down, loss_target, m_w_ada, m_b_ada, m_g_pre_mix, m_g_post_mix, m_w_in, m_g_q, m_w_uq, m_g_kv, m_w_ukv, m_conv_w_mix, m_conv_b_mix, m_w_o, m_g_pre_ffn, m_g_post_ffn, m_w_up, m_conv_w_ffn, m_conv_b_ffn, m_w_down, v_w_ada, v_b_ada, v_g_pre_mix, v_g_post_mix, v_w_in, v_g_q, v_w_uq, v_g_kv, v_w_ukv, v_conv_w_mix, v_conv_b_mix, v_w_o, v_g_pre_ffn, v_g_post_ffn, v_w_up, v_conv_w_ffn, v_conv_b_ffn, v_w_down):
    xi, yi, ci = _mesh_pos()
    chip = 2 * xi + yi
    dev = 4 * xi + 2 * yi + ci

    s, d = x.shape[1], x.shape[2]
    ql, kl = g_q.shape[1], g_kv.shape[1]
    cwid = conv_b_mix.shape[1]
    f2 = conv_b_ffn.shape[1]
    hh = (w_uq.shape[2] * N_CHIP) // (NOPE + ROPE)
    w_att = hh * LANES
    nc_ada = w_ada.shape[2]
    lat = ql + kl + ROPE
    tc_mix = _tile(cwid, 512, LANES)
    lb = -(-(ql + kl + HEAD_PAD) // tc_mix) * tc_mix
    np_cols = lb + 3 * cwid
    assert cwid == hh * VDIM and w_att % tc_mix == 0

    x0 = x.reshape(s, d)
    tgt = loss_target.reshape(s, d)

    anchors = []

    def _behind(val, state):
        val, tok = lax.optimization_barrier((val, state[-1]))
        anchors.append(tok[0, 0])
        return val

    cwm_n, cwf_n = CONV_K * cwid // N_CHIP, CONV_K * f2 // N_CHIP
    pack_a = _pad_to(jnp.concatenate([c.reshape(-1), conv_w_mix.reshape(-1), conv_w_ffn.reshape(-1)]), SUBLANES * LANES)
    rows_a = _all_gather8(pack_a.reshape(SUBLANES, -1), "ag8_inputs").reshape(N_DEV, -1)
    c_all = rows_a[:, :d]
    south = rows_a[0::2]
    cw_mix = jnp.concatenate([south[j, d : d + cwm_n].reshape(CONV_K, -1) for j in range(N_CHIP)], axis=1)
    cw_ffn = jnp.concatenate([south[j, d + cwm_n : d + cwm_n + cwf_n].reshape(CONV_K, -1) for j in range(N_CHIP)], axis=1)

    b_cols = lax.dynamic_slice(b_ada, (0, chip * nc_ada), (1, nc_ada))
    mod_part, c_act = _ada_fwd(c_all, w_ada[0], b_cols)
    mod_rows = _all_gather8(mod_part, "ag8_mod")
    mod = jnp.concatenate(
        [lax.dynamic_slice_in_dim(mod_rows, 2 * N_DEV * j + dev, 1, axis=0) for j in range(N_CHIP)], axis=1
    )

    shards = [a[0].astype(BF16) for a in (w_in, w_uq, w_ukv, w_o, w_up, w_down)]
    shards, mod = lax.optimization_barrier((shards, mod))
    ag_a = _exchange_start(shards[:3], False, "ag_a_start")
    mod = _behind(mod, ag_a)
    sh_m, sc_m, gt_m, sh_f, sc_f, gt_f = [mod[:, k * d : (k + 1) * d] for k in range(N_MOD)]

    inv_freq = 1.0 / (ROPE_THETA ** (jnp.arange(0, ROPE, 2, dtype=F32) / ROPE))
    invf = jnp.concatenate([inv_freq, inv_freq, jnp.zeros((LANES - ROPE,), F32)]).reshape(1, LANES)
    tabs = _rope_tables(positions.astype(F32).reshape(s, 1), invf)
    h1 = _pre_fwd(x0, g_pre_mix, sc_m, sh_m)

    def with_own(landed, own):
        return [lax.dynamic_update_slice_in_dim(g, a[None], chip, axis=0) for g, a in zip(landed, own)]

    own_w, landed_w = _exchange_wait(ag_a, h1, False, "ag_a_wait")
    rest, landed_w = lax.optimization_barrier((shards[3:], landed_w))
    ag_b = _exchange_start(rest, False, "ag_b_start")
    h1 = _behind(h1, ag_b)
    g_in, g_uq, g_ukv = with_own(landed_w, own_w)
    full_in = _cols_from_shards(g_in)
    w_in_p = jnp.concatenate([full_in[:, :lat], jnp.zeros((d, lb - lat), BF16), full_in[:, lat:]], axis=1)
    full_uq = _cols_from_shards(g_uq).reshape(ql, hh, NOPE + ROPE)
    w_uq_p = jnp.concatenate(
        [
            full_uq[:, :, :NOPE].reshape(ql, w_att),
            jnp.pad(full_uq[:, :, NOPE:], ((0, 0), (0, 0), (0, HEAD_PAD - ROPE))).reshape(ql, w_att),
        ],
        axis=1,
    )
    full_ukv = _cols_from_shards(g_ukv).reshape(kl, hh, NOPE + VDIM)
    w_ukv_p = jnp.concatenate([full_ukv[:, :, :NOPE].reshape(kl, w_att), full_ukv[:, :, NOPE:].reshape(kl, w_att)], axis=1)

    proj = _matmul(h1, w_in_p, out_dtype=F32, tm=1024, tn=768, tk=2048, name="mm_proj")
    qn, kvn, kr = _latent_fwd(proj, g_q, g_kv, tabs, lb)
    q_f = _matmul(qn, w_uq_p, out_dtype=F32, tm=1024, tn=1024, tk=2048, name="mm_q")
    kv_p = _matmul(kvn, w_ukv_p, out_dtype=BF16, tm=1024, tn=1024, tk=2048, name="mm_kv")
    q_c, k_c = _head_cat(q_f, kv_p, kr, tabs, hh)
    cat, lse2 = _attn_fwd(q_c, k_c, kv_p, hh, w_att + cwid)
    cat = _mixer_fwd(cat, proj, cw_mix, conv_b_mix, lb, w_att)
    own_w, landed_w = _exchange_wait(ag_b, cat, False, "ag_b_wait")
    g_o, g_up, g_down = with_own(landed_w, own_w)
    w_o_f = g_o.reshape(-1, d)
    w_up_f = _cols_from_shards(g_up)
    cw_ffn_p, cb_ffn_p = _pair_cols(cw_ffn), _pair_cols(conv_b_ffn)
    tcp, pair_perm = _pair_tile(f2 // 2), _pair_perm(f2 // 2)
    w_down_f = g_down.reshape(-1, d)
    mix = _matmul(cat, w_o_f, out_dtype=F32, tm=1024, tn=1024, tk=2048, name="mm_mix")

    x1, h2 = _mid_fwd(x0, mix, g_post_mix, gt_m, g_pre_ffn, sc_f, sh_f)
    up = _matmul(h2, w_up_f, out_dtype=F32, tm=1024, tn=tcp, tk=2048, name="mm_up", b_n_perm=pair_perm)
    act = _ffn_act_fwd(up, cw_ffn_p, cb_ffn_p)
    y = _matmul(act, w_down_f, out_dtype=F32, tm=1024, tn=1024, tk=1408, name="mm_down")
    dx2, dy, s_fin = _final(x1, y, tgt, g_post_ffn, gt_f)

    dw_down = _matmul(act, dy, ta=True, out_dtype=BF16, tm=1408, tn=1024, tk=1024, name="mm_dw_down")
    dact = _matmul(dy, w_down_f, tb=True, out_dtype=F32, tm=1024, tn=1408, tk=2048, name="mm_dact")
    dup, s_ffn_p = _ffn_act_bwd(dact, up, cw_ffn_p, cb_ffn_p)
    s_ffn = _unpair_cols(s_ffn_p)
    dw_up = _matmul(h2, dup, ta=True, out_dtype=BF16, tm=1024, tn=tcp, tk=1024, name="mm_dw_up", out_n_perm=pair_perm)
    dh2 = _matmul(dup, w_up_f, tb=True, out_dtype=F32, tm=1024, tn=1024, tk=tcp, name="mm_dh2", b_k_perm=pair_perm)
    dx1, dmix, s_mid = _mid_bwd(dh2, dx2, x1, mix, g_pre_ffn, sc_f, g_post_mix, gt_m)

    dw_o = _matmul(cat, dmix, ta=True, out_dtype=BF16, tm=1024, tn=1024, tk=1024, name="mm_dw_o")
    send_b = [dw_o.reshape(N_CHIP, -1, d), _cols_to_shards(dw_up), dw_down.reshape(N_CHIP, -1, d)]
    rs_b = _exchange_start(send_b, True, "rs_b_start")
    dmix = _behind(dmix, rs_b)
    dcat = _matmul(dmix, w_o_f, tb=True, out_dtype=F32, tm=1024, tn=1024, tk=2048, name="mm_dcat")
    dp_b, dp_c, dp_i, s_mix = _mixer_bwd(dcat, proj, cw_mix, conv_b_mix, lb, w_att)
    dob, stats = _attn_bwd_prep(cat, dcat, lse2, hh)
    dq_raw, dkv_k, dkv_v, dkr_h = _attn_bwd(q_c, k_c, kv_p, dob, stats, hh)
    dkv_p = jnp.concatenate([dkv_k, dkv_v], axis=1)
    dq_p = _dq_unrope(dq_raw, tabs, hh)
    dw_uq_p = _matmul(qn, dq_p, ta=True, out_dtype=BF16, tm=1024, tn=1024, tk=1024, name="mm_dw_uq")
    dqn = _matmul(dq_p, w_uq_p, tb=True, out_dtype=F32, tm=1024, tn=1024, tk=2048, name="mm_dqn")
    dw_ukv_p = _matmul(kvn, dkv_p, ta=True, out_dtype=BF16, tm=1024, tn=1024, tk=1024, name="mm_dw_ukv")
    dkvn = _matmul(dkv_p, w_ukv_p, tb=True, out_dtype=F32, tm=1024, tn=1024, tk=2048, name="mm_dkvn")
    dp_lat, s_lat = _latent_bwd(proj, dqn, dkvn, dkr_h, g_q, g_kv, tabs, lb)
    dproj = jnp.concatenate([dp_lat, dp_b, dp_c, dp_i], axis=1)
    dw_in_p = _matmul(h1, dproj, ta=True, out_dtype=BF16, tm=1024, tn=1536, tk=1024, name="mm_dw_in")

    dw_in_f = jnp.concatenate([dw_in_p[:, :lat], dw_in_p[:, lb:]], axis=1)
    uq3 = dw_uq_p.reshape(ql, 2, hh, LANES)
    dw_uq_f = jnp.concatenate([uq3[:, 0], uq3[:, 1, :, :ROPE]], axis=2).reshape(ql, hh * (NOPE + ROPE))
    ukv3 = dw_ukv_p.reshape(kl, 2, hh, LANES)
    dw_ukv_f = jnp.concatenate([ukv3[:, 0], ukv3[:, 1]], axis=2).reshape(kl, hh * (NOPE + VDIM))
    send_a = [_cols_to_shards(dw_in_f), _cols_to_shards(dw_uq_f), _cols_to_shards(dw_ukv_f)]
    rs_a = _exchange_start(send_a, True, "rs_a_start")
    dproj = _behind(dproj, rs_a)

    dh1 = _matmul(dproj, w_in_p, tb=True, out_dtype=F32, tm=1024, tn=1024, tk=1536, name="mm_dh1")
    grad_x, s_first = _first_bwd(dh1, dx1, x0, g_pre_mix, sc_m)

    names = ["w_in", "w_uq", "w_ukv", "w_o", "w_up", "w_down"]
    sent_b, landed_b = _exchange_wait(rs_b, s_first, True, "rs_b_wait")
    sent_a, landed_a = _exchange_wait(rs_a, landed_b[0], True, "rs_a_wait")
    landed_a, s_first = lax.optimization_barrier((landed_a, s_first))
    part = [_sum_chips(l, a, "sum_chips_" + n) for l, a, n in zip(landed_a + landed_b, sent_a + sent_b, names)]
    other = _sibling_swap(part, "sibling_swap")

    dmod = jnp.concatenate([s_first[0:1], s_first[1:2], s_mid[3:4], s_mid[0:1], s_mid[1:2], s_fin[0:1]], axis=1)
    small = [
        dmod,
        s_first[2:3],
        s_mid[4:5],
        s_lat[0:1, :ql],
        s_lat[0:1, ql : ql + kl],
        s_mix[3:4],
        s_mid[2:3],
        s_fin[1:2],
        s_ffn[3:4],
        s_mix[0:3].reshape(1, -1),
        s_ffn[0:3].reshape(1, -1),
        s_fin[3:4, :LANES],
    ]
    sizes = [a.shape[1] for a in small]
    offs = [0]
    for n in sizes:
        offs.append(offs[-1] + n)
    pack_g = _pad_to(jnp.concatenate(small, axis=1).reshape(-1), SUBLANES * LANES * SUBLANES).reshape(SUBLANES, -1)
    gathered = _all_gather8(pack_g, "ag8_small_grads")
    tot = _sum_devices(gathered).reshape(-1)
    part_of = lambda k: tot[offs[k] : offs[k + 1]]
    dmod_all = gathered.reshape(N_DEV, -1)[:, : N_MOD * d]
    loss = part_of(11)[0]

    g_b_ada = part_of(0).reshape(1, -1)
    g_vecs = [part_of(k).reshape(1, -1) for k in range(1, 9)]
    g_cw_mix = lax.dynamic_slice(part_of(9).reshape(CONV_K, cwid), (0, chip * (cwid // N_CHIP)), (CONV_K, cwid // N_CHIP))
    g_cw_ffn = lax.dynamic_slice(part_of(10).reshape(CONV_K, f2), (0, chip * (f2 // N_CHIP)), (CONV_K, f2 // N_CHIP))

    dm_cols = lax.dynamic_slice(dmod_all, (0, chip * nc_ada), (N_DEV, nc_ada))
    g_w_ada = _ada_grad(
        jnp.pad(c_act.T, ((0, 0), (0, LANES - N_DEV))), jnp.pad(dm_cols, ((0, LANES - N_DEV), (0, 0)))
    )

    big_w = [w_in, w_uq, w_ukv, w_o, w_up, w_down]
    big_m = [m_w_in, m_w_uq, m_w_ukv, m_w_o, m_w_up, m_w_down]
    big_v = [v_w_in, v_w_uq, v_w_ukv, v_w_o, v_w_up, v_w_down]
    big = {}
    for n, w_, m_, v_, p_, o_ in zip(names, big_w, big_m, big_v, part, other):
        big[n] = [a[None] for a in _adamw(w_[0], m_[0], v_[0], [p_, o_], "adamw_" + n)]
    big["w_ada"] = [a[None] for a in _adamw(w_ada[0], m_w_ada[0], v_w_ada[0], [g_w_ada], "adamw_w_ada")]

    sm_names = ["b_ada", "g_pre_mix", "g_post_mix", "g_q", "g_kv", "conv_b_mix", "g_pre_ffn", "g_post_ffn", "conv_b_ffn",
                "conv_w_mix", "conv_w_ffn"]
    sm_w = [b_ada, g_pre_mix, g_post_mix, g_q, g_kv, conv_b_mix, g_pre_ffn, g_post_ffn, conv_b_ffn, conv_w_mix, conv_w_ffn]
    sm_m = [m_b_ada, m_g_pre_mix, m_g_post_mix, m_g_q, m_g_kv, m_conv_b_mix, m_g_pre_ffn, m_g_post_ffn, m_conv_b_ffn,
            m_conv_w_mix, m_conv_w_ffn]
    sm_v = [v_b_ada, v_g_pre_mix, v_g_post_mix, v_g_q, v_g_kv, v_conv_b_mix, v_g_pre_ffn, v_g_post_ffn, v_conv_b_ffn,
            v_conv_w_mix, v_conv_w_ffn]
    sm_g = [g_b_ada] + g_vecs + [g_cw_mix, g_cw_ffn]
    flat = lambda arrs: jnp.concatenate([a.reshape(1, -1) for a in arrs], axis=1)
    sm_out = _adamw(flat(sm_w), flat(sm_m), flat(sm_v), [flat(sm_g)], "adamw_small")
    sm = {}
    off = 0
    for n, w_ in zip(sm_names, sm_w):
        sm[n] = [o[:, off : off + w_.size].reshape(w_.shape) for o in sm_out]
        off += w_.size

    order = ["w_ada", "b_ada", "g_pre_mix", "g_post_mix", "w_in", "g_q", "w_uq", "g_kv", "w_ukv", "conv_w_mix", "conv_b_mix",
             "w_o", "g_pre_ffn", "g_post_ffn", "w_up", "conv_w_ffn", "conv_b_ffn", "w_down"]
    res = {**big, **sm}
    outs = [loss + sum(anchors), grad_x.reshape(x.shape)]
    for k in range(4):
        outs += [res[n][k] for n in order]
    return tuple(outs)
```

```python
import math

import jax
import jax.numpy as jnp
from jax import lax
from jax.experimental import pallas as pl
from jax.experimental.pallas import tpu as pltpu

F32 = jnp.float32
BF16 = jnp.bfloat16
MESH = pl.DeviceIdType.MESH

N_DEV = 8
N_CHIP = 4
LANES = 128
SUBLANES = 8
VMEM_LIMIT = 56 * 2**20

NOPE = 128
ROPE = 64
VDIM = 128
HEAD_PAD = 128
ROPE_THETA = 10000.0
RMS_EPS = 1e-6
N_MOD = 6
CONV_K = 3
ATT_FWD_BLOCK, ATT_FWD_SUB = 2048, 256
ATT_BWD_BLOCK, ATT_BWD_SUB = 1024, 512
NEG = -1e30

ADAM_LR = 0.001
ADAM_B1 = 0.9
ADAM_B2 = 0.999
ADAM_EPS = 1e-08
ADAM_WD = 0.01
ADAM_STEP = 10


def _tile(n, pref, align):
    if n <= pref:
        return n
    t = (pref // align) * align
    while t >= align:
        if n % t == 0:
            return t
        t -= align
    return n


def _cp(*sem):
    return pltpu.CompilerParams(dimension_semantics=sem, vmem_limit_bytes=VMEM_LIMIT)


def _rsq(x):
    return lax.rsqrt(jnp.mean(x * x, axis=-1, keepdims=True) + RMS_EPS)


def _norm_bwd(dn, n, r):
    return r * (dn - n * jnp.mean(dn * n, axis=-1, keepdims=True))


def _colsum(a):
    return jnp.sum(a, axis=0, keepdims=True)


def _matmul(a, b, *, ta=False, tb=False, out_dtype, tm, tn, tk, name, b_n_perm=None, b_k_perm=None, out_n_perm=None):
    (k_a, m) = a.shape if ta else a.shape[::-1]
    (n, k_b) = b.shape if tb else b.shape[::-1]
    assert k_a == k_b, (a.shape, b.shape, ta, tb)
    tm, tn, tk = _tile(m, tm, LANES), _tile(n, tn, LANES), _tile(k_a, tk, LANES)
    nk = k_a // tk
    same = lambda t: t
    bn, bk, on = b_n_perm or same, b_k_perm or same, out_n_perm or same
    a_spec = pl.BlockSpec((tk, tm), lambda i, j, k: (k, i)) if ta else pl.BlockSpec((tm, tk), lambda i, j, k: (i, k))
    if tb:
        b_spec = pl.BlockSpec((tn, tk), lambda i, j, k: (bn(j), bk(k)))
    else:
        b_spec = pl.BlockSpec((tk, tn), lambda i, j, k: (bk(k), bn(j)))
    dims = (((0 if ta else 1,), (1 if tb else 0,)), ((), ()))

    def body(a_ref, b_ref, o_ref, *acc):
        p = lax.dot_general(a_ref[...].astype(BF16), b_ref[...].astype(BF16), dims, preferred_element_type=F32)
        if nk == 1:
            o_ref[...] = p.astype(o_ref.dtype)
        else:
            k = pl.program_id(2)

            @pl.when(k == 0)
            def _():
                acc[0][...] = p

            @pl.when(k > 0)
            def _():
                acc[0][...] += p

            @pl.when(k == nk - 1)
            def _():
                o_ref[...] = acc[0][...].astype(o_ref.dtype)

    return pl.pallas_call(
        body,
        name=name,
        out_shape=jax.ShapeDtypeStruct((m, n), out_dtype),
        grid=(m // tm, n // tn, nk),
        in_specs=[a_spec, b_spec],
        out_specs=pl.BlockSpec((tm, tn), lambda i, j, k: (i, on(j))),
        scratch_shapes=[] if nk == 1 else [pltpu.VMEM((tm, tn), F32)],
        compiler_params=_cp("parallel", "parallel", "arbitrary"),
    )(a, b)


def _rope_tables(pos_col, invf):
    s = pos_col.shape[0]
    ts = _tile(s, 1024, SUBLANES)
    half = ROPE // 2

    def body(p_ref, f_ref, c_ref, sa_ref, sb_ref):
        ang = p_ref[...] * f_ref[...]
        lane = lax.broadcasted_iota(jnp.int32, ang.shape, 1)
        cs, sn = jnp.cos(ang), jnp.sin(ang)
        c_ref[...] = jnp.where(lane < ROPE, cs, 0.0)
        sa_ref[...] = jnp.where((lane >= half) & (lane < ROPE), sn, 0.0)
        sb_ref[...] = jnp.where(lane < half, -sn, 0.0)

    tab = jax.ShapeDtypeStruct((s, LANES), F32)
    return pl.pallas_call(
        body,
        name="rope_tables",
        out_shape=(tab, tab, tab),
        grid=(s // ts,),
        in_specs=[pl.BlockSpec((ts, 1), lambda i: (i, 0)), pl.BlockSpec((1, LANES), lambda i: (0, 0))],
        out_specs=[pl.BlockSpec((ts, LANES), lambda i: (i, 0))] * 3,
        compiler_params=_cp("parallel"),
    )(pos_col, invf)


def _widen(t, w):
    return t if w == LANES else jnp.tile(t, (1, w // LANES))


def _rope(x, c, sa, sb):
    w = x.shape[1]
    c, sa, sb = _widen(c, w), _widen(sa, w), _widen(sb, w)
    return x * c + pltpu.roll(x, ROPE // 2, 1) * sa + pltpu.roll(x, w - ROPE // 2, 1) * sb


def _rope_t(d, c, sa, sb):
    w = d.shape[1]
    c, sa, sb = _widen(c, w), _widen(sa, w), _widen(sb, w)
    return d * c + pltpu.roll(d * sa, w - ROPE // 2, 1) + pltpu.roll(d * sb, ROPE // 2, 1)


def _ada_fwd(c_all, w, b):
    d, nc = w.shape
    tn = _tile(nc, 512, LANES)

    def body(c_ref, w_ref, b_ref, o_ref, ca_ref):
        cv = c_ref[...]
        ca = cv * jax.nn.sigmoid(cv)
        ca_ref[...] = ca
        o_ref[...] = jnp.dot(ca.astype(BF16), w_ref[...].astype(BF16), preferred_element_type=F32) + b_ref[...]

    return pl.pallas_call(
        body,
        name="ada_fwd",
        out_shape=(jax.ShapeDtypeStruct((N_DEV, nc), F32), jax.ShapeDtypeStruct((N_DEV, d), F32)),
        grid=(nc // tn,),
        in_specs=[
            pl.BlockSpec((N_DEV, d), lambda j: (0, 0)),
            pl.BlockSpec((d, tn), lambda j: (0, j)),
            pl.BlockSpec((1, tn), lambda j: (0, j)),
        ],
        out_specs=[pl.BlockSpec((N_DEV, tn), lambda j: (0, j)), pl.BlockSpec((N_DEV, d), lambda j: (0, 0))],
        compiler_params=_cp("arbitrary"),
    )(c_all, w, b)


def _rows(ts, d):
    return pl.BlockSpec((ts, d), lambda i: (i, 0))


def _vec(d):
    return pl.BlockSpec((1, d), lambda i: (0, 0))


def _sums(d):
    return pl.BlockSpec((SUBLANES, d), lambda i: (0, 0))


def _acc_rows(ref, i, rows):
    @pl.when(i == 0)
    def _():
        ref[...] = jnp.zeros(ref.shape, ref.dtype)

    for k, r in enumerate(rows):
        ref[k : k + 1, :] += r


def _pre_fwd(x, g, sc, sh):
    s, d = x.shape
    ts = _tile(s, 512, SUBLANES)

    def body(x_ref, g_ref, sc_ref, sh_ref, h_ref):
        xv = x_ref[...]
        h_ref[...] = (((xv * _rsq(xv)) * g_ref[...]) * (1.0 + sc_ref[...]) + sh_ref[...]).astype(BF16)

    return pl.pallas_call(
        body,
        name="pre_mix_fwd",
        out_shape=jax.ShapeDtypeStruct((s, d), BF16),
        grid=(s // ts,),
        in_specs=[_rows(ts, d), _vec(d), _vec(d), _vec(d)],
        out_specs=_rows(ts, d),
        compiler_params=_cp("parallel"),
    )(x, g, sc, sh)


def _mid_fwd(x0, mix, g_post, gt, g_pre, sc, sh):
    s, d = x0.shape
    ts = _tile(s, 256, SUBLANES)

    def body(x_ref, m_ref, gp_ref, gt_ref, g_ref, sc_ref, sh_ref, x1_ref, h_ref):
        mv = m_ref[...]
        x1 = x_ref[...] + gt_ref[...] * ((mv * _rsq(mv)) * gp_ref[...])
        x1_ref[...] = x1
        h_ref[...] = (((x1 * _rsq(x1)) * g_ref[...]) * (1.0 + sc_ref[...]) + sh_ref[...]).astype(BF16)

    return pl.pallas_call(
        body,
        name="mid_fwd",
        out_shape=(jax.ShapeDtypeStruct((s, d), F32), jax.ShapeDtypeStruct((s, d), BF16)),
        grid=(s // ts,),
        in_specs=[_rows(ts, d), _rows(ts, d)] + [_vec(d)] * 5,
        out_specs=[_rows(ts, d), _rows(ts, d)],
        compiler_params=_cp("parallel"),
    )(x0, mix, g_post, gt, g_pre, sc, sh)


def _final(x1, y, tgt, g_post, gt):
    s, d = x1.shape
    ts = _tile(s, 256, SUBLANES)
    ni = s // ts

    def body(x_ref, y_ref, t_ref, gp_ref, gt_ref, dx_ref, dy_ref, s_ref):
        i = pl.program_id(0)
        yv, gp, gt_v = y_ref[...], gp_ref[...], gt_ref[...]
        r = _rsq(yv)
        n = yv * r
        err = (x_ref[...] + gt_v * (n * gp)) - t_ref[...]
        dx = err * (1.0 / d)
        dx_ref[...] = dx
        dy_ref[...] = _norm_bwd(dx * (gt_v * gp), n, r).astype(BF16)
        _acc_rows(s_ref, i, [_colsum(dx * (n * gp)), _colsum(dx * gt_v * n), _colsum(err * err)])

        @pl.when(i == ni - 1)
        def _():
            tot = jnp.sum(s_ref[2:3, :], axis=1, keepdims=True) * (0.5 / d)
            s_ref[3:4, :] = jnp.broadcast_to(tot, (1, d))

    return pl.pallas_call(
        body,
        name="final_fwd_bwd",
        out_shape=(
            jax.ShapeDtypeStruct((s, d), F32),
            jax.ShapeDtypeStruct((s, d), BF16),
            jax.ShapeDtypeStruct((SUBLANES, d), F32),
        ),
        grid=(ni,),
        in_specs=[_rows(ts, d)] * 3 + [_vec(d)] * 2,
        out_specs=[_rows(ts, d), _rows(ts, d), _sums(d)],
        compiler_params=_cp("arbitrary"),
    )(x1, y, tgt, g_post, gt)


def _mid_bwd(dh2, dx2, x1, mix, g_pre, sc, g_post, gt):
    s, d = x1.shape
    ts = _tile(s, 256, SUBLANES)

    def body(dh_ref, dx2_ref, x_ref, m_ref, g_ref, sc_ref, gp_ref, gt_ref, dx1_ref, dm_ref, s_ref):
        i = pl.program_id(0)
        dh, xv, mv = dh_ref[...], x_ref[...], m_ref[...]
        g, sc_v, gp, gt_v = g_ref[...], sc_ref[...], gp_ref[...], gt_ref[...]
        r1 = _rsq(xv)
        n1 = xv * r1
        dx1 = dx2_ref[...] + _norm_bwd(dh * (g * (1.0 + sc_v)), n1, r1)
        dx1_ref[...] = dx1
        rm = _rsq(mv)
        nm = mv * rm
        dm_ref[...] = _norm_bwd(dx1 * (gt_v * gp), nm, rm).astype(BF16)
        _acc_rows(
            s_ref,
            i,
            [
                _colsum(dh),
                _colsum(dh * (n1 * g)),
                _colsum(dh * (1.0 + sc_v) * n1),
                _colsum(dx1 * (nm * gp)),
                _colsum(dx1 * gt_v * nm),
            ],
        )

    return pl.pallas_call(
        body,
        name="mid_bwd",
        out_shape=(
            jax.ShapeDtypeStruct((s, d), F32),
            jax.ShapeDtypeStruct((s, d), BF16),
            jax.ShapeDtypeStruct((SUBLANES, d), F32),
        ),
        grid=(s // ts,),
        in_specs=[_rows(ts, d)] * 4 + [_vec(d)] * 4,
        out_specs=[_rows(ts, d), _rows(ts, d), _sums(d)],
        compiler_params=_cp("arbitrary"),
    )(dh2, dx2, x1, mix, g_pre, sc, g_post, gt)


def _first_bwd(dh1, dx1, x0, g, sc):
    s, d = x0.shape
    ts = _tile(s, 256, SUBLANES)

    def body(dh_ref, dx1_ref, x_ref, g_ref, sc_ref, dx_ref, s_ref):
        i = pl.program_id(0)
        dh, xv, gv, sc_v = dh_ref[...], x_ref[...], g_ref[...], sc_ref[...]
        r = _rsq(xv)
        n = xv * r
        dx_ref[...] = dx1_ref[...] + _norm_bwd(dh * (gv * (1.0 + sc_v)), n, r)
        _acc_rows(s_ref, i, [_colsum(dh), _colsum(dh * (n * gv)), _colsum(dh * (1.0 + sc_v) * n)])

    return pl.pallas_call(
        body,
        name="first_bwd",
        out_shape=(jax.ShapeDtypeStruct((s, d), F32), jax.ShapeDtypeStruct((SUBLANES, d), F32)),
        grid=(s // ts,),
        in_specs=[_rows(ts, d)] * 3 + [_vec(d)] * 2,
        out_specs=[_rows(ts, d), _sums(d)],
        compiler_params=_cp("arbitrary"),
    )(dh1, dx1, x0, g, sc)


def _latent_fwd(proj, g_q, g_kv, tabs, lb):
    s = proj.shape[0]
    ql, kl = g_q.shape[1], g_kv.shape[1]
    ts = _tile(s, 512, SUBLANES)

    def body(p_ref, gq_ref, gk_ref, c_ref, sa_ref, sb_ref, q_ref, kv_ref, kr_ref):
        pv = p_ref[...]
        q, kv, kr = pv[:, :ql], pv[:, ql : ql + kl], pv[:, ql + kl : ql + kl + HEAD_PAD]
        q_ref[...] = ((q * _rsq(q)) * gq_ref[...]).astype(BF16)
        kv_ref[...] = ((kv * _rsq(kv)) * gk_ref[...]).astype(BF16)
        kr_ref[...] = _rope(kr, c_ref[...], sa_ref[...], sb_ref[...]).astype(BF16)

    return pl.pallas_call(
        body,
        name="latent_fwd",
        out_shape=(
            jax.ShapeDtypeStruct((s, ql), BF16),
            jax.ShapeDtypeStruct((s, kl), BF16),
            jax.ShapeDtypeStruct((s, HEAD_PAD), BF16),
        ),
        grid=(s // ts,),
        in_specs=[_rows(ts, lb), _vec(ql), _vec(kl)] + [_rows(ts, LANES)] * 3,
        out_specs=[_rows(ts, ql), _rows(ts, kl), _rows(ts, HEAD_PAD)],
        compiler_params=_cp("parallel"),
    )(proj, g_q, g_kv, *tabs)


def _latent_bwd(proj, dqn, dkvn, dkr_h, g_q, g_kv, tabs, lb):
    s = proj.shape[0]
    ql, kl = g_q.shape[1], g_kv.shape[1]
    hw = dkr_h.shape[1]
    ts = _tile(s, 256, SUBLANES)
    pad = lb - ql - kl - HEAD_PAD

    def body(p_ref, dq_ref, dkv_ref, dkr_ref, gq_ref, gk_ref, c_ref, sa_ref, sb_ref, o_ref, s_ref):
        i = pl.program_id(0)
        pv = p_ref[...]
        q, kv = pv[:, :ql], pv[:, ql : ql + kl]
        dqn_v, dkvn_v = dq_ref[...], dkv_ref[...]
        rq = _rsq(q)
        nq = q * rq
        rk = _rsq(kv)
        nk = kv * rk
        dkr = dkr_ref[:, :HEAD_PAD]
        for h in range(1, hw // HEAD_PAD):
            dkr = dkr + dkr_ref[:, h * HEAD_PAD : (h + 1) * HEAD_PAD]
        parts = [
            _norm_bwd(dqn_v * gq_ref[...], nq, rq).astype(BF16),
            _norm_bwd(dkvn_v * gk_ref[...], nk, rk).astype(BF16),
            _rope_t(dkr, c_ref[...], sa_ref[...], sb_ref[...]).astype(BF16),
        ]
        if pad:
            parts.append(jnp.zeros((ts, pad), BF16))
        o_ref[...] = jnp.concatenate(parts, axis=1)
        row = [_colsum(dqn_v * nq), _colsum(dkvn_v * nk), jnp.zeros((1, lb - ql - kl), F32)]
        _acc_rows(s_ref, i, [jnp.concatenate(row, axis=1)])

    return pl.pallas_call(
        body,
        name="latent_bwd",
        out_shape=(jax.ShapeDtypeStruct((s, lb), BF16), jax.ShapeDtypeStruct((SUBLANES, lb), F32)),
        grid=(s // ts,),
        in_specs=[_rows(ts, lb), _rows(ts, ql), _rows(ts, kl), _rows(ts, hw)]
        + [_vec(ql), _vec(kl)]
        + [_rows(ts, LANES)] * 3,
        out_specs=[_rows(ts, lb), _sums(lb)],
        compiler_params=_cp("arbitrary"),
    )(proj, dqn, dkvn, dkr_h, g_q, g_kv, *tabs)


def _conv3(ext, w, b):
    return (pltpu.roll(ext, 2, 0) * w[0:1] + pltpu.roll(ext, 1, 0) * w[1:2]) + ext * w[2:3] + b


def _conv3_t(du, w):
    n = du.shape[0]
    return du * w[2:3] + pltpu.roll(du, n - 1, 0) * w[1:2] + pltpu.roll(du, n - 2, 0) * w[0:1]


def _halo_maps(ts, s):
    r8, last = ts // SUBLANES, s // SUBLANES - 1
    prev = lambda i: jnp.maximum(i * r8 - 1, 0)
    nxt = lambda i: jnp.minimum((i + 1) * r8, last)
    return prev, nxt


def _mixer_fwd(cat, proj, cw, cb, lb, col0):
    s = proj.shape[0]
    cwid = cw.shape[1]
    ts = _tile(s, 512, SUBLANES)
    tc = _tile(cwid, 512, LANES)
    assert lb % tc == 0 and col0 % tc == 0
    nj, ob, oc = cwid // tc, lb // tc, col0 // tc
    prev, _ = _halo_maps(ts, s)

    def body(_, gb_ref, gc_ref, ci_ref, pgc_ref, pci_ref, w_ref, b_ref, o_ref):
        keep = jnp.where(pl.program_id(1) > 0, 1.0, 0.0)
        ext = jnp.concatenate([pgc_ref[...] * pci_ref[...] * keep, gc_ref[...] * ci_ref[...]], axis=0)
        o_ref[...] = (gb_ref[...] * _conv3(ext, w_ref[...], b_ref[...])[SUBLANES:]).astype(BF16)

    def col(k):
        return pl.BlockSpec((ts, tc), lambda j, i: (i, ob + k * nj + j))

    def halo(k):
        return pl.BlockSpec((SUBLANES, tc), lambda j, i: (prev(i), ob + k * nj + j))

    return pl.pallas_call(
        body,
        name="mixer_fwd",
        out_shape=jax.ShapeDtypeStruct(cat.shape, BF16),
        grid=(nj, s // ts),
        in_specs=[pl.BlockSpec(memory_space=pl.ANY), col(0), col(1), col(2), halo(1), halo(2)]
        + [pl.BlockSpec((CONV_K, tc), lambda j, i: (0, j)), pl.BlockSpec((1, tc), lambda j, i: (0, j))],
        out_specs=pl.BlockSpec((ts, tc), lambda j, i: (i, oc + j)),
        input_output_aliases={0: 0},
        compiler_params=_cp("parallel", "arbitrary"),
    )(cat, proj, proj, proj, proj, proj, cw, cb)


def _mixer_bwd(dcat, proj, cw, cb, lb, col0):
    s = proj.shape[0]
    cwid = cw.shape[1]
    ts = _tile(s, 256, SUBLANES)
    tc = _tile(cwid, 512, LANES)
    nj, ob, oc = cwid // tc, lb // tc, col0 // tc
    ni = s // ts
    prev, nxt = _halo_maps(ts, s)

    def body(d_ref, dn_ref, gb_ref, gbn_ref, gc_ref, gcp_ref, gcn_ref, ci_ref, cip_ref, cin_ref, w_ref, b_ref,
             dgb_ref, dgc_ref, dci_ref, s_ref):
        i = pl.program_id(1)
        keep_p = jnp.where(i > 0, 1.0, 0.0)
        keep_n = jnp.where(i < ni - 1, 1.0, 0.0)
        w = w_ref[...]
        gc = jnp.concatenate([gcp_ref[...], gc_ref[...], gcn_ref[...]], axis=0)
        ci = jnp.concatenate([cip_ref[...] * keep_p, ci_ref[...], cin_ref[...]], axis=0)
        u = gc * ci
        cv = _conv3(u, w, b_ref[...])[SUBLANES:]
        dco = jnp.concatenate([d_ref[...], dn_ref[...] * keep_n], axis=0)
        gb = jnp.concatenate([gb_ref[...], gbn_ref[...]], axis=0)
        dgb_ref[...] = (dco * cv)[:ts].astype(BF16)
        dcv = dco * gb
        du = _conv3_t(dcv, w)[:ts]
        dgc_ref[...] = (du * ci_ref[...]).astype(BF16)
        dci_ref[...] = (du * gc_ref[...]).astype(BF16)
        dt = dcv[:ts]
        u1, u2 = pltpu.roll(u, 1, 0), pltpu.roll(u, 2, 0)
        lo, hi = SUBLANES, SUBLANES + ts
        _acc_rows(s_ref, i, [_colsum(dt * u2[lo:hi]), _colsum(dt * u1[lo:hi]), _colsum(dt * u[lo:hi]), _colsum(dt)])

    def col(k):
        return pl.BlockSpec((ts, tc), lambda j, i: (i, ob + k * nj + j))

    def halo(k, which):
        return pl.BlockSpec((SUBLANES, tc), lambda j, i: (which(i), ob + k * nj + j))

    out_col = [pl.BlockSpec((ts, tc), lambda j, i: (i, j))] * 3
    grad = jax.ShapeDtypeStruct((s, cwid), BF16)
    return pl.pallas_call(
        body,
        name="mixer_bwd",
        out_shape=(grad, grad, grad, jax.ShapeDtypeStruct((SUBLANES, cwid), F32)),
        grid=(nj, ni),
        in_specs=[
            pl.BlockSpec((ts, tc), lambda j, i: (i, oc + j)),
            pl.BlockSpec((SUBLANES, tc), lambda j, i: (nxt(i), oc + j)),
            col(0), halo(0, nxt),
            col(1), halo(1, prev), halo(1, nxt),
            col(2), halo(2, prev), halo(2, nxt),
            pl.BlockSpec((CONV_K, tc), lambda j, i: (0, j)),
            pl.BlockSpec((1, tc), lambda j, i: (0, j)),
        ],
        out_specs=out_col + [pl.BlockSpec((SUBLANES, tc), lambda j, i: (0, j))],
        compiler_params=_cp("parallel", "arbitrary"),
    )(dcat, dcat, proj, proj, proj, proj, proj, proj, proj, proj, cw, cb)


def _pair_tile(f):
    return _tile(f, 1408, LANES)


def _pair_perm(f):
    nj = f // _pair_tile(f)
    return lambda p: (p % 2) * nj + p // 2


def _pair_cols(a):
    r, f2 = a.shape
    tc = _pair_tile(f2 // 2)
    return a.reshape(r, 2, f2 // (2 * tc), tc).transpose(0, 2, 1, 3).reshape(r, f2)


def _unpair_cols(a):
    r, f2 = a.shape
    tc = _pair_tile(f2 // 2)
    return a.reshape(r, f2 // (2 * tc), 2, tc).transpose(0, 2, 1, 3).reshape(r, f2)


def _ffn_act_fwd(up, cw, cb):
    s, f2 = up.shape
    f = f2 // 2
    ts = _tile(s, 256, SUBLANES)
    tc = _pair_tile(f)
    prev, _ = _halo_maps(ts, s)

    def body(u_ref, p_ref, w_ref, b_ref, o_ref):
        keep = jnp.where(pl.program_id(1) > 0, 1.0, 0.0)
        ext = jnp.concatenate([p_ref[...] * keep, u_ref[...]], axis=0)
        u = _conv3(ext, w_ref[...], b_ref[...])[SUBLANES:]
        a, g = u[:, :tc], u[:, tc:]
        o_ref[...] = ((g * jax.nn.sigmoid(g)) * a).astype(BF16)

    def pair(rows, which):
        return pl.BlockSpec((rows, 2 * tc), lambda j, i: (which(i), j))

    return pl.pallas_call(
        body,
        name="ffn_act_fwd",
        out_shape=jax.ShapeDtypeStruct((s, f), BF16),
        grid=(f // tc, s // ts),
        in_specs=[pair(ts, lambda i: i), pair(SUBLANES, prev), pair(CONV_K, lambda i: 0), pair(1, lambda i: 0)],
        out_specs=pl.BlockSpec((ts, tc), lambda j, i: (i, j)),
        compiler_params=_cp("parallel", "arbitrary"),
    )(up, up, cw, cb)


def _ffn_act_bwd(dact, up, cw, cb):
    s, f2 = up.shape
    f = f2 // 2
    ts = _tile(s, 128, SUBLANES)
    tc = _pair_tile(f)
    nj, ni = f // tc, s // ts
    prev, nxt = _halo_maps(ts, s)

    def body(d_ref, dn_ref, u_ref, up_ref, un_ref, w_ref, b_ref, dup_ref, s_ref):
        i = pl.program_id(1)
        keep_p = jnp.where(i > 0, 1.0, 0.0)
        keep_n = jnp.where(i < ni - 1, 1.0, 0.0)
        w = w_ref[...]
        ext = jnp.concatenate([up_ref[...] * keep_p, u_ref[...], un_ref[...]], axis=0)
        u = _conv3(ext, w, b_ref[...])[SUBLANES:]
        a, g = u[:, :tc], u[:, tc:]
        dact_v = jnp.concatenate([d_ref[...], dn_ref[...] * keep_n], axis=0)
        sg = jax.nn.sigmoid(g)
        du = jnp.concatenate([dact_v * (g * sg), dact_v * a * (sg * (1.0 + g * (1.0 - sg)))], axis=1)
        dup_ref[...] = _conv3_t(du, w)[:ts].astype(BF16)
        dt = du[:ts]
        lo, hi = SUBLANES, SUBLANES + ts
        e1, e2 = pltpu.roll(ext, 1, 0), pltpu.roll(ext, 2, 0)
        _acc_rows(s_ref, i, [_colsum(dt * e2[lo:hi]), _colsum(dt * e1[lo:hi]), _colsum(dt * ext[lo:hi]), _colsum(dt)])

    def pair(rows, which):
        return pl.BlockSpec((rows, 2 * tc), lambda j, i: (which(i), j))

    return pl.pallas_call(
        body,
        name="ffn_act_bwd",
        out_shape=(jax.ShapeDtypeStruct((s, f2), BF16), jax.ShapeDtypeStruct((SUBLANES, f2), F32)),
        grid=(nj, ni),
        in_specs=[
            pl.BlockSpec((ts, tc), lambda j, i: (i, j)),
            pl.BlockSpec((SUBLANES, tc), lambda j, i: (nxt(i), j)),
            pair(ts, lambda i: i), pair(SUBLANES, prev), pair(SUBLANES, nxt),
            pair(CONV_K, lambda i: 0), pair(1, lambda i: 0),
        ],
        out_specs=[pair(ts, lambda i: i), pair(SUBLANES, lambda i: 0)],
        compiler_params=_cp("parallel", "arbitrary"),
    )(dact, dact, up, up, up, cw, cb)


ATT_SCALE = 1.0 / math.sqrt(NOPE + ROPE)
LOG2E = math.log2(math.e)
ATT_C2 = ATT_SCALE * LOG2E
STAT_SPLIT = 64
NT = (((1,), (1,)), ((), ()))
TN = (((0,), (0,)), ((), ()))


def _head_cat(q, kv, kr, tabs, n_heads):
    s, w2 = q.shape
    w = w2 // 2
    ts = _tile(s, 512, SUBLANES)
    hd = NOPE + HEAD_PAD

    def body(q_ref, kv_ref, kr_ref, c_ref, sa_ref, sb_ref, qc_ref, kc_ref):
        qv = q_ref[...]
        qr = _rope(qv[:, w:], c_ref[...], sa_ref[...], sb_ref[...]).astype(BF16)
        krv = kr_ref[...]
        for h in range(n_heads):
            qc_ref[:, h * hd : h * hd + NOPE] = qv[:, h * NOPE : (h + 1) * NOPE].astype(BF16)
            qc_ref[:, h * hd + NOPE : (h + 1) * hd] = qr[:, h * HEAD_PAD : (h + 1) * HEAD_PAD]
            kc_ref[:, h * hd : h * hd + NOPE] = kv_ref[:, h * NOPE : (h + 1) * NOPE]
            kc_ref[:, h * hd + NOPE : (h + 1) * hd] = krv

    out = jax.ShapeDtypeStruct((s, n_heads * hd), BF16)
    return pl.pallas_call(
        body,
        name="head_cat",
        out_shape=(out, out),
        grid=(s // ts,),
        in_specs=[_rows(ts, w2), _rows(ts, w), _rows(ts, HEAD_PAD)] + [_rows(ts, LANES)] * 3,
        out_specs=[_rows(ts, n_heads * hd)] * 2,
        compiler_params=_cp("parallel"),
    )(q, kv, kr, *tabs)


def _attn_fwd(qc, kc, kv, n_heads, cat_cols):
    s = qc.shape[0]
    t = _tile(s, ATT_FWD_BLOCK, LANES)
    sub = _tile(t, ATT_FWD_SUB, LANES)
    hh = n_heads
    hd = NOPE + HEAD_PAD

    def body(q_ref, k_ref, v_ref, o_ref, lse_ref, m_s, l_s, acc_s):
        i = pl.program_id(1)
        m_s[...] = jnp.full(m_s.shape, NEG, F32)
        l_s[...] = jnp.zeros(l_s.shape, F32)
        acc_s[...] = jnp.zeros(acc_s.shape, F32)

        def chunk(k0, diag):
            m_all, l_all, acc_all = m_s[...], l_s[...], acc_s[...]
            new_m, new_l, new_acc = [], [], []
            for r0 in range(0, t, sub):
                ncol = r0 + sub if diag else t
                kk = k_ref[pl.ds(k0, ncol), :]
                sc = lax.dot_general(q_ref[pl.ds(r0, sub), :], kk, NT, preferred_element_type=F32)
                if diag:
                    row = lax.broadcasted_iota(jnp.int32, sc.shape, 0) + r0
                    col = lax.broadcasted_iota(jnp.int32, sc.shape, 1)
                    sc = jnp.where(col <= row, sc, NEG)
                m_prev = m_all[r0 : r0 + sub]
                m_new = jnp.maximum(m_prev, jnp.max(sc, axis=1, keepdims=True))
                alpha = jnp.exp2((m_prev - m_new) * ATT_C2)
                p = jnp.exp2((sc - m_new) * ATT_C2)
                pv = jnp.dot(p.astype(BF16), v_ref[pl.ds(k0, ncol), :], preferred_element_type=F32)
                new_m.append(m_new)
                new_l.append(alpha * l_all[r0 : r0 + sub] + jnp.sum(p, axis=1, keepdims=True))
                new_acc.append(alpha * acc_all[r0 : r0 + sub] + pv)
            m_s[...] = jnp.concatenate(new_m, axis=0)
            l_s[...] = jnp.concatenate(new_l, axis=0)
            acc_s[...] = jnp.concatenate(new_acc, axis=0)

        def loop_body(k, carry):
            chunk(pl.multiple_of(k * t, t), False)
            return carry

        lax.fori_loop(0, i, loop_body, 0)
        chunk(pl.multiple_of(i * t, t), True)
        l = l_s[...]
        o_ref[...] = (acc_s[...] / l).astype(BF16)
        lse_ref[...] = jnp.broadcast_to(m_s[...] * ATT_C2 + jnp.log(l) * LOG2E, lse_ref.shape)

    return pl.pallas_call(
        body,
        name="attn_fwd",
        out_shape=(jax.ShapeDtypeStruct((s, cat_cols), BF16), jax.ShapeDtypeStruct((s, hh * LANES), F32)),
        grid=(hh, s // t),
        in_specs=[
            pl.BlockSpec((t, hd), lambda h, i: (i, h)),
            pl.BlockSpec((s, hd), lambda h, i: (0, h)),
            pl.BlockSpec((s, VDIM), lambda h, i: (0, hh + h)),
        ],
        out_specs=[pl.BlockSpec((t, VDIM), lambda h, i: (i, h)), pl.BlockSpec((t, LANES), lambda h, i: (i, h))],
        scratch_shapes=[pltpu.VMEM((t, 1), F32), pltpu.VMEM((t, 1), F32), pltpu.VMEM((t, VDIM), F32)],
        compiler_params=_cp("parallel", "parallel"),
    )(qc, kc, kv)


def _attn_bwd_prep(cat, dcat, lse2, n_heads):
    s, w = lse2.shape
    ts = _tile(s, 512, SUBLANES)

    def body(o_ref, do_ref, lse_ref, dob_ref, st_ref):
        do = do_ref[...]
        dob_ref[...] = do.astype(BF16)
        prod = do * o_ref[...].astype(F32)
        lane = lax.broadcasted_iota(jnp.int32, (ts, LANES), 1)
        for h in range(n_heads):
            cols = slice(h * LANES, (h + 1) * LANES)
            dsum = jnp.sum(prod[:, cols], axis=1, keepdims=True)
            st_ref[:, cols] = jnp.where(lane < STAT_SPLIT, lse_ref[:, cols], dsum)

    return pl.pallas_call(
        body,
        name="attn_bwd_prep",
        out_shape=(jax.ShapeDtypeStruct((s, w), BF16), jax.ShapeDtypeStruct((s, w), F32)),
        grid=(s // ts,),
        in_specs=[_rows(ts, w)] * 3,
        out_specs=[_rows(ts, w)] * 2,
        compiler_params=_cp("parallel"),
    )(cat, dcat, lse2)


def _attn_bwd(qc, kc, kv, dob, stats, n_heads):
    s = qc.shape[0]
    t = _tile(s, ATT_BWD_BLOCK, LANES)
    sub = _tile(t, ATT_BWD_SUB, LANES)
    nb = s // t
    hh = n_heads
    hd = NOPE + HEAD_PAD
    w = hh * LANES

    def body(q_ref, k_ref, v_ref, do_ref, st_ref, dq_ref, dkn_ref, dv_ref, dkr_ref, dk_s, dv_s):
        j = pl.program_id(1)

        @pl.when(j == 0)
        def _():
            dq_ref[...] = jnp.zeros(dq_ref.shape, F32)

        dk_s[...] = jnp.zeros(dk_s.shape, F32)
        dv_s[...] = jnp.zeros(dv_s.shape, F32)

        def pair(i0, diag):
            for r0 in range(0, t, sub):
                ncol = r0 + sub if diag else t
                rows = pl.ds(i0 + r0, sub)
                kk, vv = k_ref[0:ncol, :], v_ref[0:ncol, :]
                qq, do, st = q_ref[rows, :], do_ref[rows, :], st_ref[rows, :]
                sc = lax.dot_general(qq, kk, NT, preferred_element_type=F32)
                if diag:
                    row = lax.broadcasted_iota(jnp.int32, sc.shape, 0) + r0
                    col = lax.broadcasted_iota(jnp.int32, sc.shape, 1)
                    sc = jnp.where(col <= row, sc, NEG)
                p = jnp.exp2(sc * ATT_C2 - st[:, 0:1])
                dv_s[0:ncol, :] += lax.dot_general(p.astype(BF16), do, TN, preferred_element_type=F32)
                dp = lax.dot_general(do, vv, NT, preferred_element_type=F32)
                ds = (p * (dp - st[:, STAT_SPLIT : STAT_SPLIT + 1]) * ATT_SCALE).astype(BF16)
                dk_s[0:ncol, :] += lax.dot_general(ds, qq, TN, preferred_element_type=F32)
                dq_ref[rows, :] += jnp.dot(ds, kk, preferred_element_type=F32)

        pair(pl.multiple_of(j * t, t), True)

        def loop_body(i, carry):
            pair(pl.multiple_of(i * t, t), False)
            return carry

        lax.fori_loop(j + 1, nb, loop_body, 0)
        dkn_ref[...] = dk_s[:, :NOPE].astype(BF16)
        dv_ref[...] = dv_s[...].astype(BF16)
        dkr_ref[...] = dk_s[:, NOPE:]

    whole = lambda width, off: pl.BlockSpec((s, width), lambda h, j: (0, off + h))
    blk = lambda width, off: pl.BlockSpec((t, width), lambda h, j: (j, off + h))
    return pl.pallas_call(
        body,
        name="attn_bwd",
        out_shape=(
            jax.ShapeDtypeStruct((s, hh * hd), F32),
            jax.ShapeDtypeStruct((s, w), BF16),
            jax.ShapeDtypeStruct((s, w), BF16),
            jax.ShapeDtypeStruct((s, w), F32),
        ),
        grid=(hh, nb),
        in_specs=[whole(hd, 0), blk(hd, 0), blk(VDIM, hh), whole(VDIM, 0), whole(LANES, 0)],
        out_specs=[whole(hd, 0), blk(NOPE, 0), blk(VDIM, 0), blk(HEAD_PAD, 0)],
        scratch_shapes=[pltpu.VMEM((t, hd), F32), pltpu.VMEM((t, VDIM), F32)],
        compiler_params=_cp("parallel", "arbitrary"),
    )(qc, kc, kv, dob, stats)


def _dq_unrope(dq, tabs, n_heads):
    s = dq.shape[0]
    hd = NOPE + HEAD_PAD
    w = n_heads * LANES
    ts = _tile(s, 512, SUBLANES)

    def body(d_ref, c_ref, sa_ref, sb_ref, o_ref):
        c, sa, sb = c_ref[...], sa_ref[...], sb_ref[...]
        for h in range(n_heads):
            o_ref[:, h * NOPE : (h + 1) * NOPE] = d_ref[:, h * hd : h * hd + NOPE].astype(BF16)
            rot = _rope_t(d_ref[:, h * hd + NOPE : (h + 1) * hd], c, sa, sb)
            o_ref[:, w + h * HEAD_PAD : w + (h + 1) * HEAD_PAD] = rot.astype(BF16)

    return pl.pallas_call(
        body,
        name="dq_unrope",
        out_shape=jax.ShapeDtypeStruct((s, 2 * w), BF16),
        grid=(s // ts,),
        in_specs=[_rows(ts, n_heads * hd)] + [_rows(ts, LANES)] * 3,
        out_specs=_rows(ts, 2 * w),
        compiler_params=_cp("parallel"),
    )(dq, *tabs)


def _adamw(w, m, v, grads, name):
    r, c = w.shape
    budget_rows = max(SUBLANES, (VMEM_LIMIT // 3) // (4 * c * 2 * (7 + len(grads))))
    tr = _tile(r, budget_rows, SUBLANES)
    ng = len(grads)
    c1 = 1.0 - ADAM_B1**ADAM_STEP
    c2 = 1.0 - ADAM_B2**ADAM_STEP

    def body(*refs):
        w_ref, m_ref, v_ref = refs[:3]
        g_ref, d_ref, nm_ref, nv_ref = refs[3 + ng :]
        g = refs[3][...]
        for extra in refs[4 : 3 + ng]:
            g = g + extra[...]
        mn = ADAM_B1 * m_ref[...] + (1.0 - ADAM_B1) * g
        vn = ADAM_B2 * v_ref[...] + (1.0 - ADAM_B2) * (g * g)
        g_ref[...] = g
        nm_ref[...] = mn
        nv_ref[...] = vn
        d_ref[...] = -ADAM_LR * ((mn / c1) / (jnp.sqrt(vn / c2) + ADAM_EPS) + ADAM_WD * w_ref[...])

    blk = pl.BlockSpec((tr, c), lambda i: (i, 0))
    out = jax.ShapeDtypeStruct((r, c), F32)
    return pl.pallas_call(
        body,
        name=name,
        out_shape=(out, out, out, out),
        grid=(r // tr,),
        in_specs=[blk] * (3 + ng),
        out_specs=[blk] * 4,
        compiler_params=_cp("parallel"),
    )(w, m, v, *grads)


def _ada_grad(ca_t, dm):
    d = ca_t.shape[0]
    nc = dm.shape[1]
    tn = _tile(nc, 512, LANES)

    def body(a_ref, b_ref, o_ref):
        o_ref[...] = jnp.dot(a_ref[...].astype(BF16), b_ref[...].astype(BF16), preferred_element_type=F32)

    return pl.pallas_call(
        body,
        name="ada_grad",
        out_shape=jax.ShapeDtypeStruct((d, nc), F32),
        grid=(nc // tn,),
        in_specs=[pl.BlockSpec((d, LANES), lambda j: (0, 0)), pl.BlockSpec((LANES, tn), lambda j: (0, j))],
        out_specs=pl.BlockSpec((d, tn), lambda j: (0, j)),
        compiler_params=_cp("parallel"),
    )(ca_t, dm)


def _sum_devices(g):
    n = g.shape[1]

    def body(g_ref, o_ref):
        acc = g_ref[0:SUBLANES, :]
        for dvc in range(1, N_DEV):
            acc = acc + g_ref[dvc * SUBLANES : (dvc + 1) * SUBLANES, :]
        o_ref[...] = acc

    return pl.pallas_call(
        body,
        name="sum_devices",
        out_shape=jax.ShapeDtypeStruct((SUBLANES, n), F32),
        in_specs=[pl.BlockSpec(memory_space=pltpu.VMEM)],
        out_specs=pl.BlockSpec(memory_space=pltpu.VMEM),
        compiler_params=pltpu.CompilerParams(vmem_limit_bytes=VMEM_LIMIT),
    )(g)


def _sum_chips(land, sent, name):
    _, r, c = land.shape
    tr = _tile(r, max(SUBLANES * 2, (VMEM_LIMIT // 4) // (c * 2 * (4 * N_CHIP + 4 * 2))), SUBLANES * 2)

    def body(l_ref, s_ref, o_ref):
        x, y, _ = _mesh_pos()
        me = 2 * x + y
        acc = jnp.where(me == 0, s_ref[0], l_ref[0]).astype(F32)
        for k in range(1, N_CHIP):
            acc = acc + jnp.where(me == k, s_ref[k], l_ref[k]).astype(F32)
        o_ref[...] = acc

    slots = pl.BlockSpec((N_CHIP, tr, c), lambda i: (0, i, 0))
    return pl.pallas_call(
        body,
        name=name,
        out_shape=jax.ShapeDtypeStruct((r, c), F32),
        grid=(r // tr,),
        in_specs=[slots, slots],
        out_specs=pl.BlockSpec((tr, c), lambda i: (i, 0)),
        compiler_params=_cp("parallel"),
    )(land, sent)


def _mesh_pos():
    return lax.axis_index("x"), lax.axis_index("y"), lax.axis_index("c")


def _other_chips(x, y):
    return [(1 - x, y), (x, 1 - y), (1 - x, 1 - y)]


def _all_gather8(x_shard, name):
    m_per, n = x_shard.shape

    def body(x_ref, out_ref, send_sems, recv_sems, local_sem):
        x, y, c = _mesh_pos()
        me, sibling = (x, y, c), (x, y, 1 - c)
        chips = _other_chips(x, y)

        def rows(px, py, pc):
            return out_ref.at[pl.ds((4 * px + 2 * py + pc) * m_per, m_per), :]

        def copy(k, block, to, src=None):
            return pltpu.make_async_remote_copy(
                src_ref=rows(*block) if src is None else src,
                dst_ref=rows(*block),
                send_sem=send_sems.at[k],
                recv_sem=recv_sems.at[k],
                device_id=to,
                device_id_type=MESH,
            )

        mine = pltpu.make_async_copy(x_ref, rows(*me), local_sem)
        mine.start()
        first = [copy(0, me, sibling, src=x_ref)]
        first += [copy(1 + j, me, (*chip, c), src=x_ref) for j, chip in enumerate(chips)]
        for cp in first:
            cp.start()
        passed = [copy(4 + j, (*chip, c), sibling) for j, chip in enumerate(chips)]
        for j, chip in enumerate(chips):
            copy(1 + j, (*chip, c), me).wait_recv()
            passed[j].start()
        copy(0, sibling, me).wait_recv()
        for j, chip in enumerate(chips):
            copy(4 + j, (*chip, 1 - c), me).wait_recv()
        for cp in first + passed:
            cp.wait_send()
        mine.wait()

    return pl.pallas_call(
        body,
        name=name,
        out_shape=jax.ShapeDtypeStruct((N_DEV * m_per, n), x_shard.dtype),
        in_specs=[pl.BlockSpec(memory_space=pltpu.VMEM)],
        out_specs=pl.BlockSpec(memory_space=pltpu.VMEM),
        scratch_shapes=[pltpu.SemaphoreType.DMA((7,)), pltpu.SemaphoreType.DMA((7,)), pltpu.SemaphoreType.DMA],
        compiler_params=pltpu.CompilerParams(vmem_limit_bytes=VMEM_LIMIT),
    )(x_shard)


HBM_SPEC = pl.BlockSpec(memory_space=pltpu.HBM)
SEM_SPEC = pl.BlockSpec(memory_space=pltpu.SEMAPHORE)
DATAFLOW = pltpu.SideEffectType.DATAFLOW_SIDE_EFFECTING


def _exchange_copies(ins, lands, send_sems, recv_sems, scatter):
    x, y, c = _mesh_pos()
    me = 2 * x + y
    sends, recvs = [], []
    for t in range(len(ins)):
        for r, (px, py) in enumerate(_other_chips(x, y)):
            peer = 2 * px + py

            def copy(src, dst, k=3 * t + r, to=(px, py, c)):
                return pltpu.make_async_remote_copy(
                    src_ref=src, dst_ref=dst, send_sem=send_sems.at[k], recv_sem=recv_sems.at[k], device_id=to, device_id_type=MESH
                )

            sends.append(copy(ins[t].at[peer] if scatter else ins[t], lands[t].at[me]))
            recvs.append(copy(ins[t].at[me] if scatter else ins[t], lands[t].at[peer]))
    return sends, recvs


def _exchange_start(arrs, scatter, name):
    nt = len(arrs)
    lands = [lax.empty(a.shape if scatter else (N_CHIP, *a.shape), a.dtype) for a in arrs]

    def body(*refs):
        ins, zones = refs[:nt], refs[nt : 2 * nt]
        send_sems, recv_sems, token = refs[2 * nt], refs[2 * nt + 1], refs[-1]
        sends, _ = _exchange_copies(ins, zones, send_sems, recv_sems, scatter)
        for cp in sends:
            cp.start()
        token[...] = jnp.zeros(token.shape, F32)

    bufs = list(arrs) + list(lands)
    return pl.pallas_call(
        body,
        name=name,
        out_shape=(
            pltpu.SemaphoreType.DMA((3 * nt,)),
            pltpu.SemaphoreType.DMA((3 * nt,)),
            *[pltpu.HBM(a.shape, a.dtype) for a in bufs],
            jax.ShapeDtypeStruct((SUBLANES, LANES), F32),
        ),
        in_specs=[HBM_SPEC] * (2 * nt),
        out_specs=(SEM_SPEC, SEM_SPEC, *[HBM_SPEC] * (2 * nt), pl.BlockSpec(memory_space=pltpu.VMEM)),
        input_output_aliases={k: 2 + k for k in range(2 * nt)},
        compiler_params=pltpu.CompilerParams(has_side_effects=DATAFLOW),
    )(*[pltpu.with_memory_space_constraint(a, pltpu.HBM) for a in bufs])


def _exchange_wait(state, after, scatter, name):
    send_sems, recv_sems, *bufs = state[:-1]
    nt = len(bufs) // 2

    def body(*refs):
        ins, zones = refs[:nt], refs[nt : 2 * nt]
        sends, recvs = _exchange_copies(ins, zones, refs[2 * nt], refs[2 * nt + 1], scatter)
        for cp in sends:
            cp.wait_send()
        for cp in recvs:
            cp.wait_recv()

    out = pl.pallas_call(
        body,
        name=name,
        out_shape=tuple(pltpu.HBM(a.shape, a.dtype) for a in bufs),
        in_specs=[HBM_SPEC] * (2 * nt) + [SEM_SPEC, SEM_SPEC, pl.BlockSpec(memory_space=pl.ANY)],
        out_specs=[HBM_SPEC] * (2 * nt),
        input_output_aliases={k: k for k in range(2 * nt)},
        compiler_params=pltpu.CompilerParams(has_side_effects=DATAFLOW),
    )(*bufs, send_sems, recv_sems, after)
    return list(out[:nt]), list(out[nt:])


def _sibling_swap(arrs, name):
    nt = len(arrs)

    def body(*refs):
        ins, outs = refs[:nt], refs[nt : 2 * nt]
        send_sems, recv_sems = refs[2 * nt :]
        x, y, c = _mesh_pos()
        cps = [
            pltpu.make_async_remote_copy(
                src_ref=ins[t],
                dst_ref=outs[t],
                send_sem=send_sems.at[t],
                recv_sem=recv_sems.at[t],
                device_id=(x, y, 1 - c),
                device_id_type=MESH,
            )
            for t in range(nt)
        ]
        for cp in cps:
            cp.start()
        for cp in cps:
            cp.wait_recv()
        for cp in cps:
            cp.wait_send()

    return pl.pallas_call(
        body,
        name=name,
        out_shape=tuple(jax.ShapeDtypeStruct(a.shape, a.dtype) for a in arrs),
        in_specs=[pl.BlockSpec(memory_space=pl.ANY)] * nt,
        out_specs=[pl.BlockSpec(memory_space=pl.ANY)] * nt,
        scratch_shapes=[pltpu.SemaphoreType.DMA((nt,)), pltpu.SemaphoreType.DMA((nt,))],
    )(*arrs)


def _cols_from_shards(g):
    _, k, n = g.shape
    return jnp.transpose(g, (1, 0, 2)).reshape(k, N_CHIP * n)


def _cols_to_shards(a):
    k, n4 = a.shape
    return jnp.transpose(a.reshape(k, N_CHIP, n4 // N_CHIP), (1, 0, 2))


def _pad_to(vec, mult):
    n = vec.shape[0]
    return jnp.pad(vec, (0, (-n) % mult))


def kernel(x, c, positions, w_ada, b_ada, g_pre_mix, g_post_mix, w_in, g_q, w_uq, g_kv, w_ukv, conv_w_mix, conv_b_mix, w_o, g_pre_ffn, g_post_ffn, w_up, conv_w_ffn, conv_b_ffn, w_down, loss_target, m_w_ada, m_b_ada, m_g_pre_mix, m_g_post_mix, m_w_in, m_g_q, m_w_uq, m_g_kv, m_w_ukv, m_conv_w_mix, m_conv_b_mix, m_w_o, m_g_pre_ffn, m_g_post_ffn, m_w_up, m_conv_w_ffn, m_conv_b_ffn, m_w_down, v_w_ada, v_b_ada, v_g_pre_mix, v_g_post_mix, v_w_in, v_g_q, v_w_uq, v_g_kv, v_w_ukv, v_conv_w_mix, v_conv_b_mix, v_w_o, v_g_pre_ffn, v_g_post_ffn, v_w_up, v_conv_w_ffn, v_conv_b_ffn, v_w_down):
    xi, yi, ci = _mesh_pos()
    chip = 2 * xi + yi
    dev = 4 * xi + 2 * yi + ci

    s, d = x.shape[1], x.shape[2]
    ql, kl = g_q.shape[1], g_kv.shape[1]
    cwid = conv_b_mix.shape[1]
    f2 = conv_b_ffn.shape[1]
    hh = (w_uq.shape[2] * N_CHIP) // (NOPE + ROPE)
    w_att = hh * LANES
    nc_ada = w_ada.shape[2]
    lat = ql + kl + ROPE
    tc_mix = _tile(cwid, 512, LANES)
    lb = -(-(ql + kl + HEAD_PAD) // tc_mix) * tc_mix
    np_cols = lb + 3 * cwid
    assert cwid == hh * VDIM and w_att % tc_mix == 0

    x0 = x.reshape(s, d)
    tgt = loss_target.reshape(s, d)

    anchors = []

    def _behind(val, state):
        val, tok = lax.optimization_barrier((val, state[-1]))
        anchors.append(tok[0, 0])
        return val

    cwm_n, cwf_n = CONV_K * cwid // N_CHIP, CONV_K * f2 // N_CHIP
    pack_a = _pad_to(jnp.concatenate([c.reshape(-1), conv_w_mix.reshape(-1), conv_w_ffn.reshape(-1)]), SUBLANES * LANES)
    rows_a = _all_gather8(pack_a.reshape(SUBLANES, -1), "ag8_inputs").reshape(N_DEV, -1)
    c_all = rows_a[:, :d]
    south = rows_a[0::2]
    cw_mix = jnp.concatenate([south[j, d : d + cwm_n].reshape(CONV_K, -1) for j in range(N_CHIP)], axis=1)
    cw_ffn = jnp.concatenate([south[j, d + cwm_n : d + cwm_n + cwf_n].reshape(CONV_K, -1) for j in range(N_CHIP)], axis=1)

    b_cols = lax.dynamic_slice(b_ada, (0, chip * nc_ada), (1, nc_ada))
    mod_part, c_act = _ada_fwd(c_all, w_ada[0], b_cols)
    mod_rows = _all_gather8(mod_part, "ag8_mod")
    mod = jnp.concatenate(
        [lax.dynamic_slice_in_dim(mod_rows, 2 * N_DEV * j + dev, 1, axis=0) for j in range(N_CHIP)], axis=1
    )

    shards = [a[0].astype(BF16) for a in (w_in, w_uq, w_ukv, w_o, w_up, w_down)]
    shards, mod = lax.optimization_barrier((shards, mod))
    ag_a = _exchange_start(shards[:3], False, "ag_a_start")
    mod = _behind(mod, ag_a)
    sh_m, sc_m, gt_m, sh_f, sc_f, gt_f = [mod[:, k * d : (k + 1) * d] for k in range(N_MOD)]

    inv_freq = 1.0 / (ROPE_THETA ** (jnp.arange(0, ROPE, 2, dtype=F32) / ROPE))
    invf = jnp.concatenate([inv_freq, inv_freq, jnp.zeros((LANES - ROPE,), F32)]).reshape(1, LANES)
    tabs = _rope_tables(positions.astype(F32).reshape(s, 1), invf)
    h1 = _pre_fwd(x0, g_pre_mix, sc_m, sh_m)

    def with_own(landed, own):
        return [lax.dynamic_update_slice_in_dim(g, a[None], chip, axis=0) for g, a in zip(landed, own)]

    own_w, landed_w = _exchange_wait(ag_a, h1, False, "ag_a_wait")
    rest, landed_w = lax.optimization_barrier((shards[3:], landed_w))
    ag_b = _exchange_start(rest, False, "ag_b_start")
    h1 = _behind(h1, ag_b)
    g_in, g_uq, g_ukv = with_own(landed_w, own_w)
    full_in = _cols_from_shards(g_in)
    w_in_p = jnp.concatenate([full_in[:, :lat], jnp.zeros((d, lb - lat), BF16), full_in[:, lat:]], axis=1)
    full_uq = _cols_from_shards(g_uq).reshape(ql, hh, NOPE + ROPE)
    w_uq_p = jnp.concatenate(
        [
            full_uq[:, :, :NOPE].reshape(ql, w_att),
            jnp.pad(full_uq[:, :, NOPE:], ((0, 0), (0, 0), (0, HEAD_PAD - ROPE))).reshape(ql, w_att),
        ],
        axis=1,
    )
    full_ukv = _cols_from_shards(g_ukv).reshape(kl, hh, NOPE + VDIM)
    w_ukv_p = jnp.concatenate([full_ukv[:, :, :NOPE].reshape(kl, w_att), full_ukv[:, :, NOPE:].reshape(kl, w_att)], axis=1)

    proj = _matmul(h1, w_in_p, out_dtype=F32, tm=1024, tn=768, tk=2048, name="mm_proj")
    qn, kvn, kr = _latent_fwd(proj, g_q, g_kv, tabs, lb)
    q_f = _matmul(qn, w_uq_p, out_dtype=F32, tm=1024, tn=1024, tk=2048, name="mm_q")
    kv_p = _matmul(kvn, w_ukv_p, out_dtype=BF16, tm=1024, tn=1024, tk=2048, name="mm_kv")
    q_c, k_c = _head_cat(q_f, kv_p, kr, tabs, hh)
    cat, lse2 = _attn_fwd(q_c, k_c, kv_p, hh, w_att + cwid)
    cat = _mixer_fwd(cat, proj, cw_mix, conv_b_mix, lb, w_att)
    own_w, landed_w = _exchange_wait(ag_b, cat, False, "ag_b_wait")
    g_o, g_up, g_down = with_own(landed_w, own_w)
    w_o_f = g_o.reshape(-1, d)
    w_up_f = _cols_from_shards(g_up)
    cw_ffn_p, cb_ffn_p = _pair_cols(cw_ffn), _pair_cols(conv_b_ffn)
    tcp, pair_perm = _pair_tile(f2 // 2), _pair_perm(f2 // 2)
    w_down_f = g_down.reshape(-1, d)
    mix = _matmul(cat, w_o_f, out_dtype=F32, tm=1024, tn=1024, tk=2048, name="mm_mix")

    x1, h2 = _mid_fwd(x0, mix, g_post_mix, gt_m, g_pre_ffn, sc_f, sh_f)
    up = _matmul(h2, w_up_f, out_dtype=F32, tm=1024, tn=tcp, tk=2048, name="mm_up", b_n_perm=pair_perm)
    act = _ffn_act_fwd(up, cw_ffn_p, cb_ffn_p)
    y = _matmul(act, w_down_f, out_dtype=F32, tm=1024, tn=1024, tk=1408, name="mm_down")
    dx2, dy, s_fin = _final(x1, y, tgt, g_post_ffn, gt_f)

    dw_down = _matmul(act, dy, ta=True, out_dtype=BF16, tm=1408, tn=1024, tk=1024, name="mm_dw_down")
    dact = _matmul(dy, w_down_f, tb=True, out_dtype=F32, tm=1024, tn=1408, tk=2048, name="mm_dact")
    dup, s_ffn_p = _ffn_act_bwd(dact, up, cw_ffn_p, cb_ffn_p)
    s_ffn = _unpair_cols(s_ffn_p)
    dw_up = _matmul(h2, dup, ta=True, out_dtype=BF16, tm=1024, tn=tcp, tk=1024, name="mm_dw_up", out_n_perm=pair_perm)
    dh2 = _matmul(dup, w_up_f, tb=True, out_dtype=F32, tm=1024, tn=1024, tk=tcp, name="mm_dh2", b_k_perm=pair_perm)
    dx1, dmix, s_mid = _mid_bwd(dh2, dx2, x1, mix, g_pre_ffn, sc_f, g_post_mix, gt_m)

    dw_o = _matmul(cat, dmix, ta=True, out_dtype=BF16, tm=1024, tn=1024, tk=1024, name="mm_dw_o")
    send_b = [dw_o.reshape(N_CHIP, -1, d), _cols_to_shards(dw_up), dw_down.reshape(N_CHIP, -1, d)]
    rs_b = _exchange_start(send_b, True, "rs_b_start")
    dmix = _behind(dmix, rs_b)
    dcat = _matmul(dmix, w_o_f, tb=True, out_dtype=F32, tm=1024, tn=1024, tk=2048, name="mm_dcat")
    dp_b, dp_c, dp_i, s_mix = _mixer_bwd(dcat, proj, cw_mix, conv_b_mix, lb, w_att)
    dob, stats = _attn_bwd_prep(cat, dcat, lse2, hh)
    dq_raw, dkv_k, dkv_v, dkr_h = _attn_bwd(q_c, k_c, kv_p, dob, stats, hh)
    dkv_p = jnp.concatenate([dkv_k, dkv_v], axis=1)
    dq_p = _dq_unrope(dq_raw, tabs, hh)
    dw_uq_p = _matmul(qn, dq_p, ta=True, out_dtype=BF16, tm=1024, tn=1024, tk=1024, name="mm_dw_uq")
    dqn = _matmul(dq_p, w_uq_p, tb=True, out_dtype=F32, tm=1024, tn=1024, tk=2048, name="mm_dqn")
    dw_ukv_p = _matmul(kvn, dkv_p, ta=True, out_dtype=BF16, tm=1024, tn=1024, tk=1024, name="mm_dw_ukv")
    dkvn = _matmul(dkv_p, w_ukv_p, tb=True, out_dtype=F32, tm=1024, tn=1024, tk=2048, name="mm_dkvn")
    dp_lat, s_lat = _latent_bwd(proj, dqn, dkvn, dkr_h, g_q, g_kv, tabs, lb)
    dproj = jnp.concatenate([dp_lat, dp_b, dp_c, dp_i], axis=1)
    dw_in_p = _matmul(h1, dproj, ta=True, out_dtype=BF16, tm=1024, tn=1536, tk=1024, name="mm_dw_in")

    dw_in_f = jnp.concatenate([dw_in_p[:, :lat], dw_in_p[:, lb:]], axis=1)
    uq3 = dw_uq_p.reshape(ql, 2, hh, LANES)
    dw_uq_f = jnp.concatenate([uq3[:, 0], uq3[:, 1, :, :ROPE]], axis=2).reshape(ql, hh * (NOPE + ROPE))
    ukv3 = dw_ukv_p.reshape(kl, 2, hh, LANES)
    dw_ukv_f = jnp.concatenate([ukv3[:, 0], ukv3[:, 1]], axis=2).reshape(kl, hh * (NOPE + VDIM))
    send_a = [_cols_to_shards(dw_in_f), _cols_to_shards(dw_uq_f), _cols_to_shards(dw_ukv_f)]
    rs_a = _exchange_start(send_a, True, "rs_a_start")
    dproj = _behind(dproj, rs_a)

    dh1 = _matmul(dproj, w_in_p, tb=True, out_dtype=F32, tm=1024, tn=1024, tk=1536, name="mm_dh1")
    grad_x, s_first = _first_bwd(dh1, dx1, x0, g_pre_mix, sc_m)

    names = ["w_in", "w_uq", "w_ukv", "w_o", "w_up", "w_down"]
    sent_b, landed_b = _exchange_wait(rs_b, s_first, True, "rs_b_wait")
    sent_a, landed_a = _exchange_wait(rs_a, landed_b[0], True, "rs_a_wait")
    landed_a, s_first = lax.optimization_barrier((landed_a, s_first))
    part = [_sum_chips(l, a, "sum_chips_" + n) for l, a, n in zip(landed_a + landed_b, sent_a + sent_b, names)]
    other = _sibling_swap(part, "sibling_swap")

    dmod = jnp.concatenate([s_first[0:1], s_first[1:2], s_mid[3:4], s_mid[0:1], s_mid[1:2], s_fin[0:1]], axis=1)
    small = [
        dmod,
        s_first[2:3],
        s_mid[4:5],
        s_lat[0:1, :ql],
        s_lat[0:1, ql : ql + kl],
        s_mix[3:4],
        s_mid[2:3],
        s_fin[1:2],
        s_ffn[3:4],
        s_mix[0:3].reshape(1, -1),
        s_ffn[0:3].reshape(1, -1),
        s_fin[3:4, :LANES],
    ]
    sizes = [a.shape[1] for a in small]
    offs = [0]
    for n in sizes:
        offs.append(offs[-1] + n)
    pack_g = _pad_to(jnp.concatenate(small, axis=1).reshape(-1), SUBLANES * LANES * SUBLANES).reshape(SUBLANES, -1)
    gathered = _all_gather8(pack_g, "ag8_small_grads")
    tot = _sum_devices(gathered).reshape(-1)
    part_of = lambda k: tot[offs[k] : offs[k + 1]]
    dmod_all = gathered.reshape(N_DEV, -1)[:, : N_MOD * d]
    loss = part_of(11)[0]

    g_b_ada = part_of(0).reshape(1, -1)
    g_vecs = [part_of(k).reshape(1, -1) for k in range(1, 9)]
    g_cw_mix = lax.dynamic_slice(part_of(9).reshape(CONV_K, cwid), (0, chip * (cwid // N_CHIP)), (CONV_K, cwid // N_CHIP))
    g_cw_ffn = lax.dynamic_slice(part_of(10).reshape(CONV_K, f2), (0, chip * (f2 // N_CHIP)), (CONV_K, f2 // N_CHIP))

    dm_cols = lax.dynamic_slice(dmod_all, (0, chip * nc_ada), (N_DEV, nc_ada))
    g_w_ada = _ada_grad(
        jnp.pad(c_act.T, ((0, 0), (0, LANES - N_DEV))), jnp.pad(dm_cols, ((0, LANES - N_DEV), (0, 0)))
    )

    big_w = [w_in, w_uq, w_ukv, w_o, w_up, w_down]
    big_m = [m_w_in, m_w_uq, m_w_ukv, m_w_o, m_w_up, m_w_down]
    big_v = [v_w_in, v_w_uq, v_w_ukv, v_w_o, v_w_up, v_w_down]
    big = {}
    for n, w_, m_, v_, p_, o_ in zip(names, big_w, big_m, big_v, part, other):
        big[n] = [a[None] for a in _adamw(w_[0], m_[0], v_[0], [p_, o_], "adamw_" + n)]
    big["w_ada"] = [a[None] for a in _adamw(w_ada[0], m_w_ada[0], v_w_ada[0], [g_w_ada], "adamw_w_ada")]

    sm_names = ["b_ada", "g_pre_mix", "g_post_mix", "g_q", "g_kv", "conv_b_mix", "g_pre_ffn", "g_post_ffn", "conv_b_ffn",
                "conv_w_mix", "conv_w_ffn"]
    sm_w = [b_ada, g_pre_mix, g_post_mix, g_q, g_kv, conv_b_mix, g_pre_ffn, g_post_ffn, conv_b_ffn, conv_w_mix, conv_w_ffn]
    sm_m = [m_b_ada, m_g_pre_mix, m_g_post_mix, m_g_q, m_g_kv, m_conv_b_mix, m_g_pre_ffn, m_g_post_ffn, m_conv_b_ffn,
            m_conv_w_mix, m_conv_w_ffn]
    sm_v = [v_b_ada, v_g_pre_mix, v_g_post_mix, v_g_q, v_g_kv, v_conv_b_mix, v_g_pre_ffn, v_g_post_ffn, v_conv_b_ffn,
            v_conv_w_mix, v_conv_w_ffn]
    sm_g = [g_b_ada] + g_vecs + [g_cw_mix, g_cw_ffn]
    flat = lambda arrs: jnp.concatenate([a.reshape(1, -1) for a in arrs], axis=1)
    sm_out = _adamw(flat(sm_w), flat(sm_m), flat(sm_v), [flat(sm_g)], "adamw_small")
    sm = {}
    off = 0
    for n, w_ in zip(sm_names, sm_w):
        sm[n] = [o[:, off : off + w_.size].reshape(w_.shape) for o in sm_out]
        off += w_.size

    order = ["w_ada", "b_ada", "g_pre_mix", "g_post_mix", "w_in", "g_q", "w_uq", "g_kv", "w_ukv", "conv_w_mix", "conv_b_mix",
             "w_o", "g_pre_ffn", "g_post_ffn", "w_up", "conv_w_ffn", "conv_b_ffn", "w_down"]
    res = {**big, **sm}
    outs = [loss + sum(anchors), grad_x.reshape(x.shape)]
    for k in range(4):
        outs += [res[n][k] for n in order]
    return tuple(outs)
```

```python
import math

import jax
import jax.numpy as jnp
from jax import lax
from jax.experimental import pallas as pl
from jax.experimental.pallas import tpu as pltpu

F32 = jnp.float32
BF16 = jnp.bfloat16
MESH = pl.DeviceIdType.MESH

N_DEV = 8
N_CHIP = 4
LANES = 128
SUBLANES = 8
VMEM_LIMIT = 56 * 2**20

NOPE = 128
ROPE = 64
VDIM = 128
HEAD_PAD = 128
ROPE_THETA = 10000.0
RMS_EPS = 1e-6
N_MOD = 6
CONV_K = 3
ATT_FWD_BLOCK, ATT_FWD_SUB = 2048, 256
ATT_BWD_BLOCK, ATT_BWD_SUB = 1024, 512
NEG = -1e30

ADAM_LR = 0.001
ADAM_B1 = 0.9
ADAM_B2 = 0.999
ADAM_EPS = 1e-08
ADAM_WD = 0.01
ADAM_STEP = 10


def _tile(n, pref, align):
    if n <= pref:
        return n
    t = (pref // align) * align
    while t >= align:
        if n % t == 0:
            return t
        t -= align
    return n


def _cp(*sem):
    return pltpu.CompilerParams(dimension_semantics=sem, vmem_limit_bytes=VMEM_LIMIT)


def _rsq(x):
    return lax.rsqrt(jnp.mean(x * x, axis=-1, keepdims=True) + RMS_EPS)


def _norm_bwd(dn, n, r):
    return r * (dn - n * jnp.mean(dn * n, axis=-1, keepdims=True))


def _colsum(a):
    return jnp.sum(a, axis=0, keepdims=True)


def _matmul(a, b, *, ta=False, tb=False, out_dtype, tm, tn, tk, name, b_n_perm=None, b_k_perm=None, out_n_perm=None,
            b_col_shards=False, out_col_shards=False):
    if b_col_shards:
        b_rows, b_cols = b.shape[1], N_CHIP * b.shape[2]
    else:
        b_rows, b_cols = b.shape
    (k_a, m) = a.shape if ta else a.shape[::-1]
    (n, k_b) = (b_rows, b_cols) if tb else (b_cols, b_rows)
    assert k_a == k_b, (a.shape, b.shape, ta, tb)
    tm, tn, tk = _tile(m, tm, LANES), _tile(n, tn, LANES), _tile(k_a, tk, LANES)
    nk = k_a // tk
    same = lambda t: t
    bn, bk, on = b_n_perm or same, b_k_perm or same, out_n_perm or same
    a_spec = pl.BlockSpec((tk, tm), lambda i, j, k: (k, i)) if ta else pl.BlockSpec((tm, tk), lambda i, j, k: (i, k))
    if b_col_shards and tb:
        per = (b_cols // N_CHIP) // tk
        b_spec = pl.BlockSpec((None, tn, tk), lambda i, j, k: (bk(k) // per, bn(j), bk(k) % per))
    elif b_col_shards:
        per = (b_cols // N_CHIP) // tn
        b_spec = pl.BlockSpec((None, tk, tn), lambda i, j, k: (bn(j) // per, bk(k), bn(j) % per))
    elif tb:
        b_spec = pl.BlockSpec((tn, tk), lambda i, j, k: (bn(j), bk(k)))
    else:
        b_spec = pl.BlockSpec((tk, tn), lambda i, j, k: (bk(k), bn(j)))
    if out_col_shards:
        per_o = (n // N_CHIP) // tn
        out_shape = jax.ShapeDtypeStruct((N_CHIP, m, n // N_CHIP), out_dtype)
        out_spec = pl.BlockSpec((None, tm, tn), lambda i, j, k: (on(j) // per_o, i, on(j) % per_o))
    else:
        out_shape = jax.ShapeDtypeStruct((m, n), out_dtype)
        out_spec = pl.BlockSpec((tm, tn), lambda i, j, k: (i, on(j)))
    dims = (((0 if ta else 1,), (1 if tb else 0,)), ((), ()))

    def body(a_ref, b_ref, o_ref, *acc):
        p = lax.dot_general(a_ref[...].astype(BF16), b_ref[...].astype(BF16), dims, preferred_element_type=F32)
        if nk == 1:
            o_ref[...] = p.astype(o_ref.dtype)
        else:
            k = pl.program_id(2)

            @pl.when(k == 0)
            def _():
                acc[0][...] = p

            @pl.when(k > 0)
            def _():
                acc[0][...] += p

            @pl.when(k == nk - 1)
            def _():
                o_ref[...] = acc[0][...].astype(o_ref.dtype)

    return pl.pallas_call(
        body,
        name=name,
        out_shape=out_shape,
        grid=(m // tm, n // tn, nk),
        in_specs=[a_spec, b_spec],
        out_specs=out_spec,
        scratch_shapes=[] if nk == 1 else [pltpu.VMEM((tm, tn), F32)],
        compiler_params=_cp("parallel", "parallel", "arbitrary"),
    )(a, b)


def _rope_tables(pos_col, invf):
    s = pos_col.shape[0]
    ts = _tile(s, 1024, SUBLANES)
    half = ROPE // 2

    def body(p_ref, f_ref, c_ref, sa_ref, sb_ref):
        ang = p_ref[...] * f_ref[...]
        lane = lax.broadcasted_iota(jnp.int32, ang.shape, 1)
        cs, sn = jnp.cos(ang), jnp.sin(ang)
        c_ref[...] = jnp.where(lane < ROPE, cs, 0.0)
        sa_ref[...] = jnp.where((lane >= half) & (lane < ROPE), sn, 0.0)
        sb_ref[...] = jnp.where(lane < half, -sn, 0.0)

    tab = jax.ShapeDtypeStruct((s, LANES), F32)
    return pl.pallas_call(
        body,
        name="rope_tables",
        out_shape=(tab, tab, tab),
        grid=(s // ts,),
        in_specs=[pl.BlockSpec((ts, 1), lambda i: (i, 0)), pl.BlockSpec((1, LANES), lambda i: (0, 0))],
        out_specs=[pl.BlockSpec((ts, LANES), lambda i: (i, 0))] * 3,
        compiler_params=_cp("parallel"),
    )(pos_col, invf)


def _widen(t, w):
    return t if w == LANES else jnp.tile(t, (1, w // LANES))


def _rope(x, c, sa, sb):
    w = x.shape[1]
    c, sa, sb = _widen(c, w), _widen(sa, w), _widen(sb, w)
    return x * c + pltpu.roll(x, ROPE // 2, 1) * sa + pltpu.roll(x, w - ROPE // 2, 1) * sb


def _rope_t(d, c, sa, sb):
    w = d.shape[1]
    c, sa, sb = _widen(c, w), _widen(sa, w), _widen(sb, w)
    return d * c + pltpu.roll(d * sa, w - ROPE // 2, 1) + pltpu.roll(d * sb, ROPE // 2, 1)


def _ada_fwd(c_all, w, b):
    d, nc = w.shape
    tn = _tile(nc, 512, LANES)

    def body(c_ref, w_ref, b_ref, o_ref, ca_ref):
        cv = c_ref[...]
        ca = cv * jax.nn.sigmoid(cv)
        ca_ref[...] = ca
        o_ref[...] = jnp.dot(ca.astype(BF16), w_ref[...].astype(BF16), preferred_element_type=F32) + b_ref[...]

    return pl.pallas_call(
        body,
        name="ada_fwd",
        out_shape=(jax.ShapeDtypeStruct((N_DEV, nc), F32), jax.ShapeDtypeStruct((N_DEV, d), F32)),
        grid=(nc // tn,),
        in_specs=[
            pl.BlockSpec((N_DEV, d), lambda j: (0, 0)),
            pl.BlockSpec((d, tn), lambda j: (0, j)),
            pl.BlockSpec((1, tn), lambda j: (0, j)),
        ],
        out_specs=[pl.BlockSpec((N_DEV, tn), lambda j: (0, j)), pl.BlockSpec((N_DEV, d), lambda j: (0, 0))],
        compiler_params=_cp("arbitrary"),
    )(c_all, w, b)


def _rows(ts, d):
    return pl.BlockSpec((ts, d), lambda i: (i, 0))


def _vec(d):
    return pl.BlockSpec((1, d), lambda i: (0, 0))


def _sums(d):
    return pl.BlockSpec((SUBLANES, d), lambda i: (0, 0))


def _acc_rows(ref, i, rows):
    @pl.when(i == 0)
    def _():
        ref[...] = jnp.zeros(ref.shape, ref.dtype)

    for k, r in enumerate(rows):
        ref[k : k + 1, :] += r


def _pre_fwd(x, g, sc, sh):
    s, d = x.shape
    ts = _tile(s, 512, SUBLANES)

    def body(x_ref, g_ref, sc_ref, sh_ref, h_ref):
        xv = x_ref[...]
        h_ref[...] = (((xv * _rsq(xv)) * g_ref[...]) * (1.0 + sc_ref[...]) + sh_ref[...]).astype(BF16)

    return pl.pallas_call(
        body,
        name="pre_mix_fwd",
        out_shape=jax.ShapeDtypeStruct((s, d), BF16),
        grid=(s // ts,),
        in_specs=[_rows(ts, d), _vec(d), _vec(d), _vec(d)],
        out_specs=_rows(ts, d),
        compiler_params=_cp("parallel"),
    )(x, g, sc, sh)


def _mid_fwd(x0, mix, g_post, gt, g_pre, sc, sh):
    s, d = x0.shape
    ts = _tile(s, 256, SUBLANES)

    def body(x_ref, m_ref, gp_ref, gt_ref, g_ref, sc_ref, sh_ref, x1_ref, h_ref):
        mv = m_ref[...]
        x1 = x_ref[...] + gt_ref[...] * ((mv * _rsq(mv)) * gp_ref[...])
        x1_ref[...] = x1
        h_ref[...] = (((x1 * _rsq(x1)) * g_ref[...]) * (1.0 + sc_ref[...]) + sh_ref[...]).astype(BF16)

    return pl.pallas_call(
        body,
        name="mid_fwd",
        out_shape=(jax.ShapeDtypeStruct((s, d), F32), jax.ShapeDtypeStruct((s, d), BF16)),
        grid=(s // ts,),
        in_specs=[_rows(ts, d), _rows(ts, d)] + [_vec(d)] * 5,
        out_specs=[_rows(ts, d), _rows(ts, d)],
        compiler_params=_cp("parallel"),
    )(x0, mix, g_post, gt, g_pre, sc, sh)


def _final(x1, y, tgt, g_post, gt):
    s, d = x1.shape
    ts = _tile(s, 256, SUBLANES)
    ni = s // ts

    def body(x_ref, y_ref, t_ref, gp_ref, gt_ref, dx_ref, dy_ref, s_ref):
        i = pl.program_id(0)
        yv, gp, gt_v = y_ref[...], gp_ref[...], gt_ref[...]
        r = _rsq(yv)
        n = yv * r
        err = (x_ref[...] + gt_v * (n * gp)) - t_ref[...]
        dx = err * (1.0 / d)
        dx_ref[...] = dx
        dy_ref[...] = _norm_bwd(dx * (gt_v * gp), n, r).astype(BF16)
        _acc_rows(s_ref, i, [_colsum(dx * (n * gp)), _colsum(dx * gt_v * n), _colsum(err * err)])

        @pl.when(i == ni - 1)
        def _():
            tot = jnp.sum(s_ref[2:3, :], axis=1, keepdims=True) * (0.5 / d)
            s_ref[3:4, :] = jnp.broadcast_to(tot, (1, d))

    return pl.pallas_call(
        body,
        name="final_fwd_bwd",
        out_shape=(
            jax.ShapeDtypeStruct((s, d), F32),
            jax.ShapeDtypeStruct((s, d), BF16),
            jax.ShapeDtypeStruct((SUBLANES, d), F32),
        ),
        grid=(ni,),
        in_specs=[_rows(ts, d)] * 3 + [_vec(d)] * 2,
        out_specs=[_rows(ts, d), _rows(ts, d), _sums(d)],
        compiler_params=_cp("arbitrary"),
    )(x1, y, tgt, g_post, gt)


def _mid_bwd(dh2, dx2, x1, mix, g_pre, sc, g_post, gt):
    s, d = x1.shape
    ts = _tile(s, 256, SUBLANES)

    def body(dh_ref, dx2_ref, x_ref, m_ref, g_ref, sc_ref, gp_ref, gt_ref, dx1_ref, dm_ref, s_ref):
        i = pl.program_id(0)
        dh, xv, mv = dh_ref[...], x_ref[...], m_ref[...]
        g, sc_v, gp, gt_v = g_ref[...], sc_ref[...], gp_ref[...], gt_ref[...]
        r1 = _rsq(xv)
        n1 = xv * r1
        dx1 = dx2_ref[...] + _norm_bwd(dh * (g * (1.0 + sc_v)), n1, r1)
        dx1_ref[...] = dx1
        rm = _rsq(mv)
        nm = mv * rm
        dm_ref[...] = _norm_bwd(dx1 * (gt_v * gp), nm, rm).astype(BF16)
        _acc_rows(
            s_ref,
            i,
            [
                _colsum(dh),
                _colsum(dh * (n1 * g)),
                _colsum(dh * (1.0 + sc_v) * n1),
                _colsum(dx1 * (nm * gp)),
                _colsum(dx1 * gt_v * nm),
            ],
        )

    return pl.pallas_call(
        body,
        name="mid_bwd",
        out_shape=(
            jax.ShapeDtypeStruct((s, d), F32),
            jax.ShapeDtypeStruct((s, d), BF16),
            jax.ShapeDtypeStruct((SUBLANES, d), F32),
        ),
        grid=(s // ts,),
        in_specs=[_rows(ts, d)] * 4 + [_vec(d)] * 4,
        out_specs=[_rows(ts, d), _rows(ts, d), _sums(d)],
        compiler_params=_cp("arbitrary"),
    )(dh2, dx2, x1, mix, g_pre, sc, g_post, gt)


def _first_bwd(dh1, dx1, x0, g, sc):
    s, d = x0.shape
    ts = _tile(s, 256, SUBLANES)

    def body(dh_ref, dx1_ref, x_ref, g_ref, sc_ref, dx_ref, s_ref):
        i = pl.program_id(0)
        dh, xv, gv, sc_v = dh_ref[...], x_ref[...], g_ref[...], sc_ref[...]
        r = _rsq(xv)
        n = xv * r
        dx_ref[...] = dx1_ref[...] + _norm_bwd(dh * (gv * (1.0 + sc_v)), n, r)
        _acc_rows(s_ref, i, [_colsum(dh), _colsum(dh * (n * gv)), _colsum(dh * (1.0 + sc_v) * n)])

    return pl.pallas_call(
        body,
        name="first_bwd",
        out_shape=(jax.ShapeDtypeStruct((s, d), F32), jax.ShapeDtypeStruct((SUBLANES, d), F32)),
        grid=(s // ts,),
        in_specs=[_rows(ts, d)] * 3 + [_vec(d)] * 2,
        out_specs=[_rows(ts, d), _sums(d)],
        compiler_params=_cp("arbitrary"),
    )(dh1, dx1, x0, g, sc)


def _latent_fwd(proj, g_q, g_kv, tabs, lb):
    s = proj.shape[0]
    ql, kl = g_q.shape[1], g_kv.shape[1]
    ts = _tile(s, 512, SUBLANES)

    def body(p_ref, gq_ref, gk_ref, c_ref, sa_ref, sb_ref, q_ref, kv_ref, kr_ref):
        pv = p_ref[...]
        q, kv, kr = pv[:, :ql], pv[:, ql : ql + kl], pv[:, ql + kl : ql + kl + HEAD_PAD]
        q_ref[...] = ((q * _rsq(q)) * gq_ref[...]).astype(BF16)
        kv_ref[...] = ((kv * _rsq(kv)) * gk_ref[...]).astype(BF16)
        kr_ref[...] = _rope(kr, c_ref[...], sa_ref[...], sb_ref[...]).astype(BF16)

    return pl.pallas_call(
        body,
        name="latent_fwd",
        out_shape=(
            jax.ShapeDtypeStruct((s, ql), BF16),
            jax.ShapeDtypeStruct((s, kl), BF16),
            jax.ShapeDtypeStruct((s, HEAD_PAD), BF16),
        ),
        grid=(s // ts,),
        in_specs=[_rows(ts, lb), _vec(ql), _vec(kl)] + [_rows(ts, LANES)] * 3,
        out_specs=[_rows(ts, ql), _rows(ts, kl), _rows(ts, HEAD_PAD)],
        compiler_params=_cp("parallel"),
    )(proj, g_q, g_kv, *tabs)


def _latent_bwd(proj, dqn, dkvn, dkr_h, g_q, g_kv, tabs, lb):
    s = proj.shape[0]
    ql, kl = g_q.shape[1], g_kv.shape[1]
    hw = dkr_h.shape[1]
    ts = _tile(s, 256, SUBLANES)
    pad = lb - ql - kl - HEAD_PAD

    def body(p_ref, dq_ref, dkv_ref, dkr_ref, gq_ref, gk_ref, c_ref, sa_ref, sb_ref, o_ref, s_ref):
        i = pl.program_id(0)
        pv = p_ref[...]
        q, kv = pv[:, :ql], pv[:, ql : ql + kl]
        dqn_v, dkvn_v = dq_ref[...], dkv_ref[...]
        rq = _rsq(q)
        nq = q * rq
        rk = _rsq(kv)
        nk = kv * rk
        dkr = dkr_ref[:, :HEAD_PAD]
        for h in range(1, hw // HEAD_PAD):
            dkr = dkr + dkr_ref[:, h * HEAD_PAD : (h + 1) * HEAD_PAD]
        parts = [
            _norm_bwd(dqn_v * gq_ref[...], nq, rq).astype(BF16),
            _norm_bwd(dkvn_v * gk_ref[...], nk, rk).astype(BF16),
            _rope_t(dkr, c_ref[...], sa_ref[...], sb_ref[...]).astype(BF16),
        ]
        if pad:
            parts.append(jnp.zeros((ts, pad), BF16))
        o_ref[...] = jnp.concatenate(parts, axis=1)
        row = [_colsum(dqn_v * nq), _colsum(dkvn_v * nk), jnp.zeros((1, lb - ql - kl), F32)]
        _acc_rows(s_ref, i, [jnp.concatenate(row, axis=1)])

    return pl.pallas_call(
        body,
        name="latent_bwd",
        out_shape=(jax.ShapeDtypeStruct((s, lb), BF16), jax.ShapeDtypeStruct((SUBLANES, lb), F32)),
        grid=(s // ts,),
        in_specs=[_rows(ts, lb), _rows(ts, ql), _rows(ts, kl), _rows(ts, hw)]
        + [_vec(ql), _vec(kl)]
        + [_rows(ts, LANES)] * 3,
        out_specs=[_rows(ts, lb), _sums(lb)],
        compiler_params=_cp("arbitrary"),
    )(proj, dqn, dkvn, dkr_h, g_q, g_kv, *tabs)


def _conv3(ext, w, b):
    return (pltpu.roll(ext, 2, 0) * w[0:1] + pltpu.roll(ext, 1, 0) * w[1:2]) + ext * w[2:3] + b


def _conv3_t(du, w):
    n = du.shape[0]
    return du * w[2:3] + pltpu.roll(du, n - 1, 0) * w[1:2] + pltpu.roll(du, n - 2, 0) * w[0:1]


def _halo_maps(ts, s):
    r8, last = ts // SUBLANES, s // SUBLANES - 1
    prev = lambda i: jnp.maximum(i * r8 - 1, 0)
    nxt = lambda i: jnp.minimum((i + 1) * r8, last)
    return prev, nxt


def _mixer_fwd(cat, proj, cw, cb, lb, col0):
    s = proj.shape[0]
    cwid = cw.shape[1]
    ts = _tile(s, 512, SUBLANES)
    tc = _tile(cwid, 512, LANES)
    assert lb % tc == 0 and col0 % tc == 0
    nj, ob, oc = cwid // tc, lb // tc, col0 // tc
    prev, _ = _halo_maps(ts, s)

    def body(_, gb_ref, gc_ref, ci_ref, pgc_ref, pci_ref, w_ref, b_ref, o_ref):
        keep = jnp.where(pl.program_id(1) > 0, 1.0, 0.0)
        ext = jnp.concatenate([pgc_ref[...] * pci_ref[...] * keep, gc_ref[...] * ci_ref[...]], axis=0)
        o_ref[...] = (gb_ref[...] * _conv3(ext, w_ref[...], b_ref[...])[SUBLANES:]).astype(BF16)

    def col(k):
        return pl.BlockSpec((ts, tc), lambda j, i: (i, ob + k * nj + j))

    def halo(k):
        return pl.BlockSpec((SUBLANES, tc), lambda j, i: (prev(i), ob + k * nj + j))

    return pl.pallas_call(
        body,
        name="mixer_fwd",
        out_shape=jax.ShapeDtypeStruct(cat.shape, BF16),
        grid=(nj, s // ts),
        in_specs=[pl.BlockSpec(memory_space=pl.ANY), col(0), col(1), col(2), halo(1), halo(2)]
        + [pl.BlockSpec((CONV_K, tc), lambda j, i: (0, j)), pl.BlockSpec((1, tc), lambda j, i: (0, j))],
        out_specs=pl.BlockSpec((ts, tc), lambda j, i: (i, oc + j)),
        input_output_aliases={0: 0},
        compiler_params=_cp("parallel", "arbitrary"),
    )(cat, proj, proj, proj, proj, proj, cw, cb)


def _mixer_bwd(dcat, proj, cw, cb, lb, col0):
    s = proj.shape[0]
    cwid = cw.shape[1]
    ts = _tile(s, 256, SUBLANES)
    tc = _tile(cwid, 512, LANES)
    nj, ob, oc = cwid // tc, lb // tc, col0 // tc
    ni = s // ts
    prev, nxt = _halo_maps(ts, s)

    def body(d_ref, dn_ref, gb_ref, gbn_ref, gc_ref, gcp_ref, gcn_ref, ci_ref, cip_ref, cin_ref, w_ref, b_ref,
             dgb_ref, dgc_ref, dci_ref, s_ref):
        i = pl.program_id(1)
        keep_p = jnp.where(i > 0, 1.0, 0.0)
        keep_n = jnp.where(i < ni - 1, 1.0, 0.0)
        w = w_ref[...]
        gc = jnp.concatenate([gcp_ref[...], gc_ref[...], gcn_ref[...]], axis=0)
        ci = jnp.concatenate([cip_ref[...] * keep_p, ci_ref[...], cin_ref[...]], axis=0)
        u = gc * ci
        cv = _conv3(u, w, b_ref[...])[SUBLANES:]
        dco = jnp.concatenate([d_ref[...], dn_ref[...] * keep_n], axis=0)
        gb = jnp.concatenate([gb_ref[...], gbn_ref[...]], axis=0)
        dgb_ref[...] = (dco * cv)[:ts].astype(BF16)
        dcv = dco * gb
        du = _conv3_t(dcv, w)[:ts]
        dgc_ref[...] = (du * ci_ref[...]).astype(BF16)
        dci_ref[...] = (du * gc_ref[...]).astype(BF16)
        dt = dcv[:ts]
        u1, u2 = pltpu.roll(u, 1, 0), pltpu.roll(u, 2, 0)
        lo, hi = SUBLANES, SUBLANES + ts
        _acc_rows(s_ref, i, [_colsum(dt * u2[lo:hi]), _colsum(dt * u1[lo:hi]), _colsum(dt * u[lo:hi]), _colsum(dt)])

    def col(k):
        return pl.BlockSpec((ts, tc), lambda j, i: (i, ob + k * nj + j))

    def halo(k, which):
        return pl.BlockSpec((SUBLANES, tc), lambda j, i: (which(i), ob + k * nj + j))

    out_col = [pl.BlockSpec((ts, tc), lambda j, i: (i, j))] * 3
    grad = jax.ShapeDtypeStruct((s, cwid), BF16)
    return pl.pallas_call(
        body,
        name="mixer_bwd",
        out_shape=(grad, grad, grad, jax.ShapeDtypeStruct((SUBLANES, cwid), F32)),
        grid=(nj, ni),
        in_specs=[
            pl.BlockSpec((ts, tc), lambda j, i: (i, oc + j)),
            pl.BlockSpec((SUBLANES, tc), lambda j, i: (nxt(i), oc + j)),
            col(0), halo(0, nxt),
            col(1), halo(1, prev), halo(1, nxt),
            col(2), halo(2, prev), halo(2, nxt),
            pl.BlockSpec((CONV_K, tc), lambda j, i: (0, j)),
            pl.BlockSpec((1, tc), lambda j, i: (0, j)),
        ],
        out_specs=out_col + [pl.BlockSpec((SUBLANES, tc), lambda j, i: (0, j))],
        compiler_params=_cp("parallel", "arbitrary"),
    )(dcat, dcat, proj, proj, proj, proj, proj, proj, proj, proj, cw, cb)


def _pair_tile(f):
    return _tile(f, 1408, LANES)


def _pair_perm(f):
    nj = f // _pair_tile(f)
    return lambda p: (p % 2) * nj + p // 2


def _pair_cols(a):
    r, f2 = a.shape
    tc = _pair_tile(f2 // 2)
    return a.reshape(r, 2, f2 // (2 * tc), tc).transpose(0, 2, 1, 3).reshape(r, f2)


def _unpair_cols(a):
    r, f2 = a.shape
    tc = _pair_tile(f2 // 2)
    return a.reshape(r, f2 // (2 * tc), 2, tc).transpose(0, 2, 1, 3).reshape(r, f2)


def _ffn_act_fwd(up, cw, cb):
    s, f2 = up.shape
    f = f2 // 2
    ts = _tile(s, 256, SUBLANES)
    tc = _pair_tile(f)
    prev, _ = _halo_maps(ts, s)

    def body(u_ref, p_ref, w_ref, b_ref, o_ref):
        keep = jnp.where(pl.program_id(1) > 0, 1.0, 0.0)
        ext = jnp.concatenate([p_ref[...] * keep, u_ref[...]], axis=0)
        u = _conv3(ext, w_ref[...], b_ref[...])[SUBLANES:]
        a, g = u[:, :tc], u[:, tc:]
        o_ref[...] = ((g * jax.nn.sigmoid(g)) * a).astype(BF16)

    def pair(rows, which):
        return pl.BlockSpec((rows, 2 * tc), lambda j, i: (which(i), j))

    return pl.pallas_call(
        body,
        name="ffn_act_fwd",
        out_shape=jax.ShapeDtypeStruct((s, f), BF16),
        grid=(f // tc, s // ts),
        in_specs=[pair(ts, lambda i: i), pair(SUBLANES, prev), pair(CONV_K, lambda i: 0), pair(1, lambda i: 0)],
        out_specs=pl.BlockSpec((ts, tc), lambda j, i: (i, j)),
        compiler_params=_cp("parallel", "arbitrary"),
    )(up, up, cw, cb)


def _ffn_act_bwd(dact, up, cw, cb):
    s, f2 = up.shape
    f = f2 // 2
    ts = _tile(s, 128, SUBLANES)
    tc = _pair_tile(f)
    nj, ni = f // tc, s // ts
    prev, nxt = _halo_maps(ts, s)

    def body(d_ref, dn_ref, u_ref, up_ref, un_ref, w_ref, b_ref, dup_ref, s_ref):
        i = pl.program_id(1)
        keep_p = jnp.where(i > 0, 1.0, 0.0)
        keep_n = jnp.where(i < ni - 1, 1.0, 0.0)
        w = w_ref[...]
        ext = jnp.concatenate([up_ref[...] * keep_p, u_ref[...], un_ref[...]], axis=0)
        u = _conv3(ext, w, b_ref[...])[SUBLANES:]
        a, g = u[:, :tc], u[:, tc:]
        dact_v = jnp.concatenate([d_ref[...], dn_ref[...] * keep_n], axis=0)
        sg = jax.nn.sigmoid(g)
        du = jnp.concatenate([dact_v * (g * sg), dact_v * a * (sg * (1.0 + g * (1.0 - sg)))], axis=1)
        dup_ref[...] = _conv3_t(du, w)[:ts].astype(BF16)
        dt = du[:ts]
        lo, hi = SUBLANES, SUBLANES + ts
        e1, e2 = pltpu.roll(ext, 1, 0), pltpu.roll(ext, 2, 0)
        _acc_rows(s_ref, i, [_colsum(dt * e2[lo:hi]), _colsum(dt * e1[lo:hi]), _colsum(dt * ext[lo:hi]), _colsum(dt)])

    def pair(rows, which):
        return pl.BlockSpec((rows, 2 * tc), lambda j, i: (which(i), j))

    return pl.pallas_call(
        body,
        name="ffn_act_bwd",
        out_shape=(jax.ShapeDtypeStruct((s, f2), BF16), jax.ShapeDtypeStruct((SUBLANES, f2), F32)),
        grid=(nj, ni),
        in_specs=[
            pl.BlockSpec((ts, tc), lambda j, i: (i, j)),
            pl.BlockSpec((SUBLANES, tc), lambda j, i: (nxt(i), j)),
            pair(ts, lambda i: i), pair(SUBLANES, prev), pair(SUBLANES, nxt),
            pair(CONV_K, lambda i: 0), pair(1, lambda i: 0),
        ],
        out_specs=[pair(ts, lambda i: i), pair(SUBLANES, lambda i: 0)],
        compiler_params=_cp("parallel", "arbitrary"),
    )(dact, dact, up, up, up, cw, cb)


ATT_SCALE = 1.0 / math.sqrt(NOPE + ROPE)
LOG2E = math.log2(math.e)
ATT_C2 = ATT_SCALE * LOG2E
STAT_SPLIT = 64
NT = (((1,), (1,)), ((), ()))
TN = (((0,), (0,)), ((), ()))


def _head_cat(q, kv, kr, tabs, n_heads):
    s, w2 = q.shape
    w = w2 // 2
    ts = _tile(s, 512, SUBLANES)
    hd = NOPE + HEAD_PAD

    def body(q_ref, kv_ref, kr_ref, c_ref, sa_ref, sb_ref, qc_ref, kc_ref):
        qv = q_ref[...]
        qr = _rope(qv[:, w:], c_ref[...], sa_ref[...], sb_ref[...]).astype(BF16)
        krv = kr_ref[...]
        for h in range(n_heads):
            qc_ref[:, h * hd : h * hd + NOPE] = qv[:, h * NOPE : (h + 1) * NOPE].astype(BF16)
            qc_ref[:, h * hd + NOPE : (h + 1) * hd] = qr[:, h * HEAD_PAD : (h + 1) * HEAD_PAD]
            kc_ref[:, h * hd : h * hd + NOPE] = kv_ref[:, h * NOPE : (h + 1) * NOPE]
            kc_ref[:, h * hd + NOPE : (h + 1) * hd] = krv

    out = jax.ShapeDtypeStruct((s, n_heads * hd), BF16)
    return pl.pallas_call(
        body,
        name="head_cat",
        out_shape=(out, out),
        grid=(s // ts,),
        in_specs=[_rows(ts, w2), _rows(ts, w), _rows(ts, HEAD_PAD)] + [_rows(ts, LANES)] * 3,
        out_specs=[_rows(ts, n_heads * hd)] * 2,
        compiler_params=_cp("parallel"),
    )(q, kv, kr, *tabs)


def _attn_fwd(qc, kc, kv, n_heads, cat_cols):
    s = qc.shape[0]
    t = _tile(s, ATT_FWD_BLOCK, LANES)
    sub = _tile(t, ATT_FWD_SUB, LANES)
    hh = n_heads
    hd = NOPE + HEAD_PAD

    def body(q_ref, k_ref, v_ref, o_ref, lse_ref, m_s, l_s, acc_s):
        i = pl.program_id(1)
        m_s[...] = jnp.full(m_s.shape, NEG, F32)
        l_s[...] = jnp.zeros(l_s.shape, F32)
        acc_s[...] = jnp.zeros(acc_s.shape, F32)

        def chunk(k0, diag):
            m_all, l_all, acc_all = m_s[...], l_s[...], acc_s[...]
            new_m, new_l, new_acc = [], [], []
            for r0 in range(0, t, sub):
                ncol = r0 + sub if diag else t
                kk = k_ref[pl.ds(k0, ncol), :]
                sc = lax.dot_general(q_ref[pl.ds(r0, sub), :], kk, NT, preferred_element_type=F32)
                if diag:
                    row = lax.broadcasted_iota(jnp.int32, sc.shape, 0) + r0
                    col = lax.broadcasted_iota(jnp.int32, sc.shape, 1)
                    sc = jnp.where(col <= row, sc, NEG)
                m_prev = m_all[r0 : r0 + sub]
                m_new = jnp.maximum(m_prev, jnp.max(sc, axis=1, keepdims=True))
                alpha = jnp.exp2((m_prev - m_new) * ATT_C2)
                p = jnp.exp2((sc - m_new) * ATT_C2)
                pv = jnp.dot(p.astype(BF16), v_ref[pl.ds(k0, ncol), :], preferred_element_type=F32)
                new_m.append(m_new)
                new_l.append(alpha * l_all[r0 : r0 + sub] + jnp.sum(p, axis=1, keepdims=True))
                new_acc.append(alpha * acc_all[r0 : r0 + sub] + pv)
            m_s[...] = jnp.concatenate(new_m, axis=0)
            l_s[...] = jnp.concatenate(new_l, axis=0)
            acc_s[...] = jnp.concatenate(new_acc, axis=0)

        def loop_body(k, carry):
            chunk(pl.multiple_of(k * t, t), False)
            return carry

        lax.fori_loop(0, i, loop_body, 0)
        chunk(pl.multiple_of(i * t, t), True)
        l = l_s[...]
        o_ref[...] = (acc_s[...] / l).astype(BF16)
        lse_ref[...] = jnp.broadcast_to(m_s[...] * ATT_C2 + jnp.log(l) * LOG2E, lse_ref.shape)

    return pl.pallas_call(
        body,
        name="attn_fwd",
        out_shape=(jax.ShapeDtypeStruct((s, cat_cols), BF16), jax.ShapeDtypeStruct((s, hh * LANES), F32)),
        grid=(hh, s // t),
        in_specs=[
            pl.BlockSpec((t, hd), lambda h, i: (i, h)),
            pl.BlockSpec((s, hd), lambda h, i: (0, h)),
            pl.BlockSpec((s, VDIM), lambda h, i: (0, hh + h)),
        ],
        out_specs=[pl.BlockSpec((t, VDIM), lambda h, i: (i, h)), pl.BlockSpec((t, LANES), lambda h, i: (i, h))],
        scratch_shapes=[pltpu.VMEM((t, 1), F32), pltpu.VMEM((t, 1), F32), pltpu.VMEM((t, VDIM), F32)],
        compiler_params=_cp("parallel", "parallel"),
    )(qc, kc, kv)


def _attn_bwd_prep(cat, dcat, lse2, n_heads):
    s, w = lse2.shape
    ts = _tile(s, 512, SUBLANES)

    def body(o_ref, do_ref, lse_ref, dob_ref, st_ref):
        do = do_ref[...]
        dob_ref[...] = do.astype(BF16)
        prod = do * o_ref[...].astype(F32)
        lane = lax.broadcasted_iota(jnp.int32, (ts, LANES), 1)
        for h in range(n_heads):
            cols = slice(h * LANES, (h + 1) * LANES)
            dsum = jnp.sum(prod[:, cols], axis=1, keepdims=True)
            st_ref[:, cols] = jnp.where(lane < STAT_SPLIT, lse_ref[:, cols], dsum)

    return pl.pallas_call(
        body,
        name="attn_bwd_prep",
        out_shape=(jax.ShapeDtypeStruct((s, w), BF16), jax.ShapeDtypeStruct((s, w), F32)),
        grid=(s // ts,),
        in_specs=[_rows(ts, w)] * 3,
        out_specs=[_rows(ts, w)] * 2,
        compiler_params=_cp("parallel"),
    )(cat, dcat, lse2)


def _attn_bwd(qc, kc, kv, dob, stats, n_heads):
    s = qc.shape[0]
    t = _tile(s, ATT_BWD_BLOCK, LANES)
    sub = _tile(t, ATT_BWD_SUB, LANES)
    nb = s // t
    hh = n_heads
    hd = NOPE + HEAD_PAD
    w = hh * LANES

    def body(q_ref, k_ref, v_ref, do_ref, st_ref, dq_ref, dkn_ref, dv_ref, dkr_ref, dk_s, dv_s):
        j = pl.program_id(1)

        @pl.when(j == 0)
        def _():
            dq_ref[...] = jnp.zeros(dq_ref.shape, F32)

        dk_s[...] = jnp.zeros(dk_s.shape, F32)
        dv_s[...] = jnp.zeros(dv_s.shape, F32)

        def pair(i0, diag):
            for r0 in range(0, t, sub):
                ncol = r0 + sub if diag else t
                rows = pl.ds(i0 + r0, sub)
                kk, vv = k_ref[0:ncol, :], v_ref[0:ncol, :]
                qq, do, st = q_ref[rows, :], do_ref[rows, :], st_ref[rows, :]
                sc = lax.dot_general(qq, kk, NT, preferred_element_type=F32)
                if diag:
                    row = lax.broadcasted_iota(jnp.int32, sc.shape, 0) + r0
                    col = lax.broadcasted_iota(jnp.int32, sc.shape, 1)
                    sc = jnp.where(col <= row, sc, NEG)
                p = jnp.exp2(sc * ATT_C2 - st[:, 0:1])
                dv_s[0:ncol, :] += lax.dot_general(p.astype(BF16), do, TN, preferred_element_type=F32)
                dp = lax.dot_general(do, vv, NT, preferred_element_type=F32)
                ds = (p * (dp - st[:, STAT_SPLIT : STAT_SPLIT + 1]) * ATT_SCALE).astype(BF16)
                dk_s[0:ncol, :] += lax.dot_general(ds, qq, TN, preferred_element_type=F32)
                dq_ref[rows, :] += jnp.dot(ds, kk, preferred_element_type=F32)

        pair(pl.multiple_of(j * t, t), True)

        def loop_body(i, carry):
            pair(pl.multiple_of(i * t, t), False)
            return carry

        lax.fori_loop(j + 1, nb, loop_body, 0)
        dkn_ref[...] = dk_s[:, :NOPE].astype(BF16)
        dv_ref[...] = dv_s[...].astype(BF16)
        dkr_ref[...] = dk_s[:, NOPE:]

    whole = lambda width, off: pl.BlockSpec((s, width), lambda h, j: (0, off + h))
    blk = lambda width, off: pl.BlockSpec((t, width), lambda h, j: (j, off + h))
    return pl.pallas_call(
        body,
        name="attn_bwd",
        out_shape=(
            jax.ShapeDtypeStruct((s, hh * hd), F32),
            jax.ShapeDtypeStruct((s, w), BF16),
            jax.ShapeDtypeStruct((s, w), BF16),
            jax.ShapeDtypeStruct((s, w), F32),
        ),
        grid=(hh, nb),
        in_specs=[whole(hd, 0), blk(hd, 0), blk(VDIM, hh), whole(VDIM, 0), whole(LANES, 0)],
        out_specs=[whole(hd, 0), blk(NOPE, 0), blk(VDIM, 0), blk(HEAD_PAD, 0)],
        scratch_shapes=[pltpu.VMEM((t, hd), F32), pltpu.VMEM((t, VDIM), F32)],
        compiler_params=_cp("parallel", "arbitrary"),
    )(qc, kc, kv, dob, stats)


def _dq_unrope(dq, tabs, n_heads):
    s = dq.shape[0]
    hd = NOPE + HEAD_PAD
    w = n_heads * LANES
    ts = _tile(s, 512, SUBLANES)

    def body(d_ref, c_ref, sa_ref, sb_ref, o_ref):
        c, sa, sb = c_ref[...], sa_ref[...], sb_ref[...]
        for h in range(n_heads):
            o_ref[:, h * NOPE : (h + 1) * NOPE] = d_ref[:, h * hd : h * hd + NOPE].astype(BF16)
            rot = _rope_t(d_ref[:, h * hd + NOPE : (h + 1) * hd], c, sa, sb)
            o_ref[:, w + h * HEAD_PAD : w + (h + 1) * HEAD_PAD] = rot.astype(BF16)

    return pl.pallas_call(
        body,
        name="dq_unrope",
        out_shape=jax.ShapeDtypeStruct((s, 2 * w), BF16),
        grid=(s // ts,),
        in_specs=[_rows(ts, n_heads * hd)] + [_rows(ts, LANES)] * 3,
        out_specs=_rows(ts, 2 * w),
        compiler_params=_cp("parallel"),
    )(dq, *tabs)


def _adamw(w, m, v, grads, name):
    r, c = w.shape
    budget_rows = max(SUBLANES, (VMEM_LIMIT // 3) // (4 * c * 2 * (7 + len(grads))))
    tr = _tile(r, budget_rows, SUBLANES)
    ng = len(grads)
    c1 = 1.0 - ADAM_B1**ADAM_STEP
    c2 = 1.0 - ADAM_B2**ADAM_STEP

    def body(*refs):
        w_ref, m_ref, v_ref = refs[:3]
        g_ref, d_ref, nm_ref, nv_ref = refs[3 + ng :]
        g = refs[3][...]
        for extra in refs[4 : 3 + ng]:
            g = g + extra[...]
        mn = ADAM_B1 * m_ref[...] + (1.0 - ADAM_B1) * g
        vn = ADAM_B2 * v_ref[...] + (1.0 - ADAM_B2) * (g * g)
        g_ref[...] = g
        nm_ref[...] = mn
        nv_ref[...] = vn
        d_ref[...] = -ADAM_LR * ((mn / c1) / (jnp.sqrt(vn / c2) + ADAM_EPS) + ADAM_WD * w_ref[...])

    blk = pl.BlockSpec((tr, c), lambda i: (i, 0))
    out = jax.ShapeDtypeStruct((r, c), F32)
    return pl.pallas_call(
        body,
        name=name,
        out_shape=(out, out, out, out),
        grid=(r // tr,),
        in_specs=[blk] * (3 + ng),
        out_specs=[blk] * 4,
        compiler_params=_cp("parallel"),
    )(w, m, v, *grads)


def _ada_grad(ca_t, dm):
    d = ca_t.shape[0]
    nc = dm.shape[1]
    tn = _tile(nc, 512, LANES)

    def body(a_ref, b_ref, o_ref):
        o_ref[...] = jnp.dot(a_ref[...].astype(BF16), b_ref[...].astype(BF16), preferred_element_type=F32)

    return pl.pallas_call(
        body,
        name="ada_grad",
        out_shape=jax.ShapeDtypeStruct((d, nc), F32),
        grid=(nc // tn,),
        in_specs=[pl.BlockSpec((d, LANES), lambda j: (0, 0)), pl.BlockSpec((LANES, tn), lambda j: (0, j))],
        out_specs=pl.BlockSpec((d, tn), lambda j: (0, j)),
        compiler_params=_cp("parallel"),
    )(ca_t, dm)


def _sum_devices(g):
    n = g.shape[1]

    def body(g_ref, o_ref):
        acc = g_ref[0:SUBLANES, :]
        for dvc in range(1, N_DEV):
            acc = acc + g_ref[dvc * SUBLANES : (dvc + 1) * SUBLANES, :]
        o_ref[...] = acc

    return pl.pallas_call(
        body,
        name="sum_devices",
        out_shape=jax.ShapeDtypeStruct((SUBLANES, n), F32),
        in_specs=[pl.BlockSpec(memory_space=pltpu.VMEM)],
        out_specs=pl.BlockSpec(memory_space=pltpu.VMEM),
        compiler_params=pltpu.CompilerParams(vmem_limit_bytes=VMEM_LIMIT),
    )(g)


def _sum_chips(land, sent, name):
    _, r, c = land.shape
    tr = _tile(r, max(SUBLANES * 2, (VMEM_LIMIT // 4) // (c * 2 * (4 * N_CHIP + 4 * 2))), SUBLANES * 2)

    def body(l_ref, s_ref, o_ref):
        x, y, _ = _mesh_pos()
        me = 2 * x + y
        acc = jnp.where(me == 0, s_ref[0], l_ref[0]).astype(F32)
        for k in range(1, N_CHIP):
            acc = acc + jnp.where(me == k, s_ref[k], l_ref[k]).astype(F32)
        o_ref[...] = acc

    slots = pl.BlockSpec((N_CHIP, tr, c), lambda i: (0, i, 0))
    return pl.pallas_call(
        body,
        name=name,
        out_shape=jax.ShapeDtypeStruct((r, c), F32),
        grid=(r // tr,),
        in_specs=[slots, slots],
        out_specs=pl.BlockSpec((tr, c), lambda i: (i, 0)),
        compiler_params=_cp("parallel"),
    )(land, sent)


def _mesh_pos():
    return lax.axis_index("x"), lax.axis_index("y"), lax.axis_index("c")


def _other_chips(x, y):
    return [(1 - x, y), (x, 1 - y), (1 - x, 1 - y)]


def _all_gather8(x_shard, name):
    m_per, n = x_shard.shape

    def body(x_ref, out_ref, send_sems, recv_sems, local_sem):
        x, y, c = _mesh_pos()
        me, sibling = (x, y, c), (x, y, 1 - c)
        chips = _other_chips(x, y)

        def rows(px, py, pc):
            return out_ref.at[pl.ds((4 * px + 2 * py + pc) * m_per, m_per), :]

        def copy(k, block, to, src=None):
            return pltpu.make_async_remote_copy(
                src_ref=rows(*block) if src is None else src,
                dst_ref=rows(*block),
                send_sem=send_sems.at[k],
                recv_sem=recv_sems.at[k],
                device_id=to,
                device_id_type=MESH,
            )

        mine = pltpu.make_async_copy(x_ref, rows(*me), local_sem)
        mine.start()
        first = [copy(0, me, sibling, src=x_ref)]
        first += [copy(1 + j, me, (*chip, c), src=x_ref) for j, chip in enumerate(chips)]
        for cp in first:
            cp.start()
        passed = [copy(4 + j, (*chip, c), sibling) for j, chip in enumerate(chips)]
        for j, chip in enumerate(chips):
            copy(1 + j, (*chip, c), me).wait_recv()
            passed[j].start()
        copy(0, sibling, me).wait_recv()
        for j, chip in enumerate(chips):
            copy(4 + j, (*chip, 1 - c), me).wait_recv()
        for cp in first + passed:
            cp.wait_send()
        mine.wait()

    return pl.pallas_call(
        body,
        name=name,
        out_shape=jax.ShapeDtypeStruct((N_DEV * m_per, n), x_shard.dtype),
        in_specs=[pl.BlockSpec(memory_space=pltpu.VMEM)],
        out_specs=pl.BlockSpec(memory_space=pltpu.VMEM),
        scratch_shapes=[pltpu.SemaphoreType.DMA((7,)), pltpu.SemaphoreType.DMA((7,)), pltpu.SemaphoreType.DMA],
        compiler_params=pltpu.CompilerParams(vmem_limit_bytes=VMEM_LIMIT),
    )(x_shard)


HBM_SPEC = pl.BlockSpec(memory_space=pltpu.HBM)
SEM_SPEC = pl.BlockSpec(memory_space=pltpu.SEMAPHORE)
DATAFLOW = pltpu.SideEffectType.DATAFLOW_SIDE_EFFECTING


def _exchange_copies(ins, lands, send_sems, recv_sems, scatter):
    x, y, c = _mesh_pos()
    me = 2 * x + y
    sends, recvs = [], []
    for t in range(len(ins)):
        for r, (px, py) in enumerate(_other_chips(x, y)):
            peer = 2 * px + py

            def copy(src, dst, k=3 * t + r, to=(px, py, c)):
                return pltpu.make_async_remote_copy(
                    src_ref=src, dst_ref=dst, send_sem=send_sems.at[k], recv_sem=recv_sems.at[k], device_id=to, device_id_type=MESH
                )

            sends.append(copy(ins[t].at[peer] if scatter else ins[t], lands[t].at[me]))
            recvs.append(copy(ins[t].at[me] if scatter else ins[t], lands[t].at[peer]))
    return sends, recvs


def _exchange_start(arrs, scatter, name):
    nt = len(arrs)
    lands = [lax.empty(a.shape if scatter else (N_CHIP, *a.shape), a.dtype) for a in arrs]

    def body(*refs):
        ins, zones = refs[:nt], refs[nt : 2 * nt]
        send_sems, recv_sems, token = refs[2 * nt], refs[2 * nt + 1], refs[-1]
        sends, _ = _exchange_copies(ins, zones, send_sems, recv_sems, scatter)
        for cp in sends:
            cp.start()
        token[...] = jnp.zeros(token.shape, F32)

    bufs = list(arrs) + list(lands)
    return pl.pallas_call(
        body,
        name=name,
        out_shape=(
            pltpu.SemaphoreType.DMA((3 * nt,)),
            pltpu.SemaphoreType.DMA((3 * nt,)),
            *[pltpu.HBM(a.shape, a.dtype) for a in bufs],
            jax.ShapeDtypeStruct((SUBLANES, LANES), F32),
        ),
        in_specs=[HBM_SPEC] * (2 * nt),
        out_specs=(SEM_SPEC, SEM_SPEC, *[HBM_SPEC] * (2 * nt), pl.BlockSpec(memory_space=pltpu.VMEM)),
        input_output_aliases={k: 2 + k for k in range(2 * nt)},
        compiler_params=pltpu.CompilerParams(has_side_effects=DATAFLOW),
    )(*[pltpu.with_memory_space_constraint(a, pltpu.HBM) for a in bufs])


def _exchange_wait(state, after, scatter, name):
    send_sems, recv_sems, *bufs = state[:-1]
    nt = len(bufs) // 2

    def body(*refs):
        ins, zones = refs[:nt], refs[nt : 2 * nt]
        sends, recvs = _exchange_copies(ins, zones, refs[2 * nt], refs[2 * nt + 1], scatter)
        for cp in sends:
            cp.wait_send()
        for cp in recvs:
            cp.wait_recv()

    out = pl.pallas_call(
        body,
        name=name,
        out_shape=tuple(pltpu.HBM(a.shape, a.dtype) for a in bufs),
        in_specs=[HBM_SPEC] * (2 * nt) + [SEM_SPEC, SEM_SPEC, pl.BlockSpec(memory_space=pl.ANY)],
        out_specs=[HBM_SPEC] * (2 * nt),
        input_output_aliases={k: k for k in range(2 * nt)},
        compiler_params=pltpu.CompilerParams(has_side_effects=DATAFLOW),
    )(*bufs, send_sems, recv_sems, after)
    return list(out[:nt]), list(out[nt:])


def _sibling_swap(arrs, name):
    nt = len(arrs)

    def body(*refs):
        ins, outs = refs[:nt], refs[nt : 2 * nt]
        send_sems, recv_sems = refs[2 * nt :]
        x, y, c = _mesh_pos()
        cps = [
            pltpu.make_async_remote_copy(
                src_ref=ins[t],
                dst_ref=outs[t],
                send_sem=send_sems.at[t],
                recv_sem=recv_sems.at[t],
                device_id=(x, y, 1 - c),
                device_id_type=MESH,
            )
            for t in range(nt)
        ]
        for cp in cps:
            cp.start()
        for cp in cps:
            cp.wait_recv()
        for cp in cps:
            cp.wait_send()

    return pl.pallas_call(
        body,
        name=name,
        out_shape=tuple(jax.ShapeDtypeStruct(a.shape, a.dtype) for a in arrs),
        in_specs=[pl.BlockSpec(memory_space=pl.ANY)] * nt,
        out_specs=[pl.BlockSpec(memory_space=pl.ANY)] * nt,
        scratch_shapes=[pltpu.SemaphoreType.DMA((nt,)), pltpu.SemaphoreType.DMA((nt,))],
    )(*arrs)


def _cols_from_shards(g):
    _, k, n = g.shape
    return jnp.transpose(g, (1, 0, 2)).reshape(k, N_CHIP * n)


def _cols_to_shards(a):
    k, n4 = a.shape
    return jnp.transpose(a.reshape(k, N_CHIP, n4 // N_CHIP), (1, 0, 2))


def _pad_to(vec, mult):
    n = vec.shape[0]
    return jnp.pad(vec, (0, (-n) % mult))


def kernel(x, c, positions, w_ada, b_ada, g_pre_mix, g_post_mix, w_in, g_q, w_uq, g_kv, w_ukv, conv_w_mix, conv_b_mix, w_o, g_pre_ffn, g_post_ffn, w_up, conv_w_ffn, conv_b_ffn, w_down, loss_target, m_w_ada, m_b_ada, m_g_pre_mix, m_g_post_mix, m_w_in, m_g_q, m_w_uq, m_g_kv, m_w_ukv, m_conv_w_mix, m_conv_b_mix, m_w_o, m_g_pre_ffn, m_g_post_ffn, m_w_up, m_conv_w_ffn, m_conv_b_ffn, m_w_down, v_w_ada, v_b_ada, v_g_pre_mix, v_g_post_mix, v_w_in, v_g_q, v_w_uq, v_g_kv, v_w_ukv, v_conv_w_mix, v_conv_b_mix, v_w_o, v_g_pre_ffn, v_g_post_ffn, v_w_up, v_conv_w_ffn, v_conv_b_ffn, v_w_down):
    xi, yi, ci = _mesh_pos()
    chip = 2 * xi + yi
    dev = 4 * xi + 2 * yi + ci

    s, d = x.shape[1], x.shape[2]
    ql, kl = g_q.shape[1], g_kv.shape[1]
    cwid = conv_b_mix.shape[1]
    f2 = conv_b_ffn.shape[1]
    hh = (w_uq.shape[2] * N_CHIP) // (NOPE + ROPE)
    w_att = hh * LANES
    nc_ada = w_ada.shape[2]
    lat = ql + kl + ROPE
    tc_mix = _tile(cwid, 512, LANES)
    lb = -(-(ql + kl + HEAD_PAD) // tc_mix) * tc_mix
    np_cols = lb + 3 * cwid
    assert cwid == hh * VDIM and w_att % tc_mix == 0

    x0 = x.reshape(s, d)
    tgt = loss_target.reshape(s, d)

    anchors = []

    def _behind(val, state):
        val, tok = lax.optimization_barrier((val, state[-1]))
        anchors.append(tok[0, 0])
        return val

    cwm_n, cwf_n = CONV_K * cwid // N_CHIP, CONV_K * f2 // N_CHIP
    pack_a = _pad_to(jnp.concatenate([c.reshape(-1), conv_w_mix.reshape(-1), conv_w_ffn.reshape(-1)]), SUBLANES * LANES)
    rows_a = _all_gather8(pack_a.reshape(SUBLANES, -1), "ag8_inputs").reshape(N_DEV, -1)
    c_all = rows_a[:, :d]
    south = rows_a[0::2]
    cw_mix = jnp.concatenate([south[j, d : d + cwm_n].reshape(CONV_K, -1) for j in range(N_CHIP)], axis=1)
    cw_ffn = jnp.concatenate([south[j, d + cwm_n : d + cwm_n + cwf_n].reshape(CONV_K, -1) for j in range(N_CHIP)], axis=1)

    b_cols = lax.dynamic_slice(b_ada, (0, chip * nc_ada), (1, nc_ada))
    mod_part, c_act = _ada_fwd(c_all, w_ada[0], b_cols)
    mod_rows = _all_gather8(mod_part, "ag8_mod")
    mod = jnp.concatenate(
        [lax.dynamic_slice_in_dim(mod_rows, 2 * N_DEV * j + dev, 1, axis=0) for j in range(N_CHIP)], axis=1
    )

    shards = [a[0].astype(BF16) for a in (w_in, w_uq, w_ukv, w_o, w_up, w_down)]
    shards, mod = lax.optimization_barrier((shards, mod))
    ag_a = _exchange_start(shards[:3], False, "ag_a_start")
    mod = _behind(mod, ag_a)
    sh_m, sc_m, gt_m, sh_f, sc_f, gt_f = [mod[:, k * d : (k + 1) * d] for k in range(N_MOD)]

    inv_freq = 1.0 / (ROPE_THETA ** (jnp.arange(0, ROPE, 2, dtype=F32) / ROPE))
    invf = jnp.concatenate([inv_freq, inv_freq, jnp.zeros((LANES - ROPE,), F32)]).reshape(1, LANES)
    tabs = _rope_tables(positions.astype(F32).reshape(s, 1), invf)
    h1 = _pre_fwd(x0, g_pre_mix, sc_m, sh_m)

    def with_own(landed, own):
        return [lax.dynamic_update_slice_in_dim(g, a[None], chip, axis=0) for g, a in zip(landed, own)]

    own_w, landed_w = _exchange_wait(ag_a, h1, False, "ag_a_wait")
    rest, landed_w = lax.optimization_barrier((shards[3:], landed_w))
    ag_b = _exchange_start(rest, False, "ag_b_start")
    h1 = _behind(h1, ag_b)
    g_in, g_uq, g_ukv = with_own(landed_w, own_w)
    full_in = _cols_from_shards(g_in)
    w_in_p = jnp.concatenate([full_in[:, :lat], jnp.zeros((d, lb - lat), BF16), full_in[:, lat:]], axis=1)
    full_uq = _cols_from_shards(g_uq).reshape(ql, hh, NOPE + ROPE)
    w_uq_p = jnp.concatenate(
        [
            full_uq[:, :, :NOPE].reshape(ql, w_att),
            jnp.pad(full_uq[:, :, NOPE:], ((0, 0), (0, 0), (0, HEAD_PAD - ROPE))).reshape(ql, w_att),
        ],
        axis=1,
    )
    full_ukv = _cols_from_shards(g_ukv).reshape(kl, hh, NOPE + VDIM)
    w_ukv_p = jnp.concatenate([full_ukv[:, :, :NOPE].reshape(kl, w_att), full_ukv[:, :, NOPE:].reshape(kl, w_att)], axis=1)

    proj = _matmul(h1, w_in_p, out_dtype=F32, tm=1024, tn=768, tk=2048, name="mm_proj")
    qn, kvn, kr = _latent_fwd(proj, g_q, g_kv, tabs, lb)
    q_f = _matmul(qn, w_uq_p, out_dtype=F32, tm=1024, tn=1024, tk=2048, name="mm_q")
    kv_p = _matmul(kvn, w_ukv_p, out_dtype=BF16, tm=1024, tn=1024, tk=2048, name="mm_kv")
    q_c, k_c = _head_cat(q_f, kv_p, kr, tabs, hh)
    cat, lse2 = _attn_fwd(q_c, k_c, kv_p, hh, w_att + cwid)
    cat = _mixer_fwd(cat, proj, cw_mix, conv_b_mix, lb, w_att)
    own_w, landed_w = _exchange_wait(ag_b, cat, False, "ag_b_wait")
    g_o, g_up, g_down = with_own(landed_w, own_w)
    w_o_f = g_o.reshape(-1, d)
    cw_ffn_p, cb_ffn_p = _pair_cols(cw_ffn), _pair_cols(conv_b_ffn)
    tcp, pair_perm = _pair_tile(f2 // 2), _pair_perm(f2 // 2)
    w_down_f = g_down.reshape(-1, d)
    mix = _matmul(cat, w_o_f, out_dtype=F32, tm=1024, tn=1024, tk=2048, name="mm_mix")

    x1, h2 = _mid_fwd(x0, mix, g_post_mix, gt_m, g_pre_ffn, sc_f, sh_f)
    up = _matmul(h2, g_up, out_dtype=F32, tm=1024, tn=tcp, tk=2048, name="mm_up", b_n_perm=pair_perm, b_col_shards=True)
    act = _ffn_act_fwd(up, cw_ffn_p, cb_ffn_p)
    y = _matmul(act, w_down_f, out_dtype=F32, tm=512, tn=1024, tk=5632, name="mm_down")
    dx2, dy, s_fin = _final(x1, y, tgt, g_post_ffn, gt_f)

    dw_down = _matmul(act, dy, ta=True, out_dtype=BF16, tm=1408, tn=1024, tk=2048, name="mm_dw_down")
    dact = _matmul(dy, w_down_f, tb=True, out_dtype=F32, tm=1024, tn=1408, tk=2048, name="mm_dact")
    dup, s_ffn_p = _ffn_act_bwd(dact, up, cw_ffn_p, cb_ffn_p)
    s_ffn = _unpair_cols(s_ffn_p)
    dw_up = _matmul(
        h2, dup, ta=True, out_dtype=BF16, tm=1024, tn=tcp, tk=2048, name="mm_dw_up", out_n_perm=pair_perm, out_col_shards=True
    )
    dh2 = _matmul(
        dup, g_up, tb=True, out_dtype=F32, tm=1024, tn=1024, tk=tcp, name="mm_dh2", b_k_perm=pair_perm, b_col_shards=True
    )
    dx1, dmix, s_mid = _mid_bwd(dh2, dx2, x1, mix, g_pre_ffn, sc_f, g_post_mix, gt_m)

    dw_o = _matmul(cat, dmix, ta=True, out_dtype=BF16, tm=1024, tn=1024, tk=2048, name="mm_dw_o")
    send_b = [dw_o.reshape(N_CHIP, -1, d), dw_up, dw_down.reshape(N_CHIP, -1, d)]
    rs_b = _exchange_start(send_b, True, "rs_b_start")
    dmix = _behind(dmix, rs_b)
    dcat = _matmul(dmix, w_o_f, tb=True, out_dtype=F32, tm=1024, tn=1024, tk=2048, name="mm_dcat")
    dp_b, dp_c, dp_i, s_mix = _mixer_bwd(dcat, proj, cw_mix, conv_b_mix, lb, w_att)
    dob, stats = _attn_bwd_prep(cat, dcat, lse2, hh)
    dq_raw, dkv_k, dkv_v, dkr_h = _attn_bwd(q_c, k_c, kv_p, dob, stats, hh)
    dkv_p = jnp.concatenate([dkv_k, dkv_v], axis=1)
    dq_p = _dq_unrope(dq_raw, tabs, hh)
    dw_uq_p = _matmul(qn, dq_p, ta=True, out_dtype=BF16, tm=1024, tn=1024, tk=1024, name="mm_dw_uq")
    dqn = _matmul(dq_p, w_uq_p, tb=True, out_dtype=F32, tm=1024, tn=1024, tk=2048, name="mm_dqn")
    dw_ukv_p = _matmul(kvn, dkv_p, ta=True, out_dtype=BF16, tm=1024, tn=1024, tk=1024, name="mm_dw_ukv")
    dkvn = _matmul(dkv_p, w_ukv_p, tb=True, out_dtype=F32, tm=1024, tn=1024, tk=2048, name="mm_dkvn")
    dp_lat, s_lat = _latent_bwd(proj, dqn, dkvn, dkr_h, g_q, g_kv, tabs, lb)
    dproj = jnp.concatenate([dp_lat, dp_b, dp_c, dp_i], axis=1)
    dw_in_p = _matmul(h1, dproj, ta=True, out_dtype=BF16, tm=1024, tn=1536, tk=2048, name="mm_dw_in")

    dw_in_f = jnp.concatenate([dw_in_p[:, :lat], dw_in_p[:, lb:]], axis=1)
    uq3 = dw_uq_p.reshape(ql, 2, hh, LANES)
    dw_uq_f = jnp.concatenate([uq3[:, 0], uq3[:, 1, :, :ROPE]], axis=2).reshape(ql, hh * (NOPE + ROPE))
    ukv3 = dw_ukv_p.reshape(kl, 2, hh, LANES)
    dw_ukv_f = jnp.concatenate([ukv3[:, 0], ukv3[:, 1]], axis=2).reshape(kl, hh * (NOPE + VDIM))
    send_a = [_cols_to_shards(dw_in_f), _cols_to_shards(dw_uq_f), _cols_to_shards(dw_ukv_f)]
    rs_a = _exchange_start(send_a, True, "rs_a_start")
    dproj = _behind(dproj, rs_a)

    dh1 = _matmul(dproj, w_in_p, tb=True, out_dtype=F32, tm=512, tn=1024, tk=4608, name="mm_dh1")
    grad_x, s_first = _first_bwd(dh1, dx1, x0, g_pre_mix, sc_m)

    names = ["w_in", "w_uq", "w_ukv", "w_o", "w_up", "w_down"]
    sent_b, landed_b = _exchange_wait(rs_b, s_first, True, "rs_b_wait")
    sent_a, landed_a = _exchange_wait(rs_a, landed_b[0], True, "rs_a_wait")
    landed_a, s_first = lax.optimization_barrier((landed_a, s_first))
    part = [_sum_chips(l, a, "sum_chips_" + n) for l, a, n in zip(landed_a + landed_b, sent_a + sent_b, names)]
    other = _sibling_swap(part, "sibling_swap")

    dmod = jnp.concatenate([s_first[0:1], s_first[1:2], s_mid[3:4], s_mid[0:1], s_mid[1:2], s_fin[0:1]], axis=1)
    small = [
        dmod,
        s_first[2:3],
        s_mid[4:5],
        s_lat[0:1, :ql],
        s_lat[0:1, ql : ql + kl],
        s_mix[3:4],
        s_mid[2:3],
        s_fin[1:2],
        s_ffn[3:4],
        s_mix[0:3].reshape(1, -1),
        s_ffn[0:3].reshape(1, -1),
        s_fin[3:4, :LANES],
    ]
    sizes = [a.shape[1] for a in small]
    offs = [0]
    for n in sizes:
        offs.append(offs[-1] + n)
    pack_g = _pad_to(jnp.concatenate(small, axis=1).reshape(-1), SUBLANES * LANES * SUBLANES).reshape(SUBLANES, -1)
    gathered = _all_gather8(pack_g, "ag8_small_grads")
    tot = _sum_devices(gathered).reshape(-1)
    part_of = lambda k: tot[offs[k] : offs[k + 1]]
    dmod_all = gathered.reshape(N_DEV, -1)[:, : N_MOD * d]
    loss = part_of(11)[0]

    g_b_ada = part_of(0).reshape(1, -1)
    g_vecs = [part_of(k).reshape(1, -1) for k in range(1, 9)]
    g_cw_mix = lax.dynamic_slice(part_of(9).reshape(CONV_K, cwid), (0, chip * (cwid // N_CHIP)), (CONV_K, cwid // N_CHIP))
    g_cw_ffn = lax.dynamic_slice(part_of(10).reshape(CONV_K, f2), (0, chip * (f2 // N_CHIP)), (CONV_K, f2 // N_CHIP))

    dm_cols = lax.dynamic_slice(dmod_all, (0, chip * nc_ada), (N_DEV, nc_ada))
    g_w_ada = _ada_grad(
        jnp.pad(c_act.T, ((0, 0), (0, LANES - N_DEV))), jnp.pad(dm_cols, ((0, LANES - N_DEV), (0, 0)))
    )

    big_w = [w_in, w_uq, w_ukv, w_o, w_up, w_down]
    big_m = [m_w_in, m_w_uq, m_w_ukv, m_w_o, m_w_up, m_w_down]
    big_v = [v_w_in, v_w_uq, v_w_ukv, v_w_o, v_w_up, v_w_down]
    big = {}
    for n, w_, m_, v_, p_, o_ in zip(names, big_w, big_m, big_v, part, other):
        big[n] = [a[None] for a in _adamw(w_[0], m_[0], v_[0], [p_, o_], "adamw_" + n)]
    big["w_ada"] = [a[None] for a in _adamw(w_ada[0], m_w_ada[0], v_w_ada[0], [g_w_ada], "adamw_w_ada")]

    sm_names = ["b_ada", "g_pre_mix", "g_post_mix", "g_q", "g_kv", "conv_b_mix", "g_pre_ffn", "g_post_ffn", "conv_b_ffn",
                "conv_w_mix", "conv_w_ffn"]
    sm_w = [b_ada, g_pre_mix, g_post_mix, g_q, g_kv, conv_b_mix, g_pre_ffn, g_post_ffn, conv_b_ffn, conv_w_mix, conv_w_ffn]
    sm_m = [m_b_ada, m_g_pre_mix, m_g_post_mix, m_g_q, m_g_kv, m_conv_b_mix, m_g_pre_ffn, m_g_post_ffn, m_conv_b_ffn,
            m_conv_w_mix, m_conv_w_ffn]
    sm_v = [v_b_ada, v_g_pre_mix, v_g_post_mix, v_g_q, v_g_kv, v_conv_b_mix, v_g_pre_ffn, v_g_post_ffn, v_conv_b_ffn,
            v_conv_w_mix, v_conv_w_ffn]
    sm_g = [g_b_ada] + g_vecs + [g_cw_mix, g_cw_ffn]
    flat = lambda arrs: jnp.concatenate([a.reshape(1, -1) for a in arrs], axis=1)
    sm_out = _adamw(flat(sm_w), flat(sm_m), flat(sm_v), [flat(sm_g)], "adamw_small")
    sm = {}
    off = 0
    for n, w_ in zip(sm_names, sm_w):
        sm[n] = [o[:, off : off + w_.size].reshape(w_.shape) for o in sm_out]
        off += w_.size

    order = ["w_ada", "b_ada", "g_pre_mix", "g_post_mix", "w_in", "g_q", "w_uq", "g_kv", "w_ukv", "conv_w_mix", "conv_b_mix",
             "w_o", "g_pre_ffn", "g_post_ffn", "w_up", "conv_w_ffn", "conv_b_ffn", "w_down"]
    res = {**big, **sm}
    outs = [loss + sum(anchors), grad_x.reshape(x.shape)]
    for k in range(4):
        outs += [res[n][k] for n in order]
    return tuple(outs)
```

```python
import math

import jax
import jax.numpy as jnp
from jax import lax
from jax.experimental import pallas as pl
from jax.experimental.pallas import tpu as pltpu

F32 = jnp.float32
BF16 = jnp.bfloat16
MESH = pl.DeviceIdType.MESH

N_DEV = 8
N_CHIP = 4
LANES = 128
SUBLANES = 8
VMEM_LIMIT = 56 * 2**20

NOPE = 128
ROPE = 64
VDIM = 128
HEAD_PAD = 128
ROPE_THETA = 10000.0
RMS_EPS = 1e-6
N_MOD = 6
CONV_K = 3
ATT_FWD_BLOCK, ATT_FWD_SUB = 2048, 256
ATT_BWD_BLOCK, ATT_BWD_SUB = 1024, 256
NEG = -1e30

ADAM_LR = 0.001
ADAM_B1 = 0.9
ADAM_B2 = 0.999
ADAM_EPS = 1e-08
ADAM_WD = 0.01
ADAM_STEP = 10


def _tile(n, pref, align):
    if n <= pref:
        return n
    t = (pref // align) * align
    while t >= align:
        if n % t == 0:
            return t
        t -= align
    return n


def _cp(*sem):
    return pltpu.CompilerParams(dimension_semantics=sem, vmem_limit_bytes=VMEM_LIMIT)


def _rsq(x):
    return lax.rsqrt(jnp.mean(x * x, axis=-1, keepdims=True) + RMS_EPS)


def _norm_bwd(dn, n, r):
    return r * (dn - n * jnp.mean(dn * n, axis=-1, keepdims=True))


def _colsum(a):
    return jnp.sum(a, axis=0, keepdims=True)


def _matmul(a, b, *, ta=False, tb=False, out_dtype, tm, tn, tk, name, b_n_perm=None, b_k_perm=None, out_n_perm=None,
            b_col_shards=False, out_col_shards=False):
    if b_col_shards:
        b_rows, b_cols = b.shape[1], N_CHIP * b.shape[2]
    else:
        b_rows, b_cols = b.shape
    (k_a, m) = a.shape if ta else a.shape[::-1]
    (n, k_b) = (b_rows, b_cols) if tb else (b_cols, b_rows)
    assert k_a == k_b, (a.shape, b.shape, ta, tb)
    tm, tn, tk = _tile(m, tm, LANES), _tile(n, tn, LANES), _tile(k_a, tk, LANES)
    nk = k_a // tk
    same = lambda t: t
    bn, bk, on = b_n_perm or same, b_k_perm or same, out_n_perm or same
    a_spec = pl.BlockSpec((tk, tm), lambda i, j, k: (k, i)) if ta else pl.BlockSpec((tm, tk), lambda i, j, k: (i, k))
    if b_col_shards and tb:
        per = (b_cols // N_CHIP) // tk
        b_spec = pl.BlockSpec((None, tn, tk), lambda i, j, k: (bk(k) // per, bn(j), bk(k) % per))
    elif b_col_shards:
        per = (b_cols // N_CHIP) // tn
        b_spec = pl.BlockSpec((None, tk, tn), lambda i, j, k: (bn(j) // per, bk(k), bn(j) % per))
    elif tb:
        b_spec = pl.BlockSpec((tn, tk), lambda i, j, k: (bn(j), bk(k)))
    else:
        b_spec = pl.BlockSpec((tk, tn), lambda i, j, k: (bk(k), bn(j)))
    if out_col_shards:
        per_o = (n // N_CHIP) // tn
        out_shape = jax.ShapeDtypeStruct((N_CHIP, m, n // N_CHIP), out_dtype)
        out_spec = pl.BlockSpec((None, tm, tn), lambda i, j, k: (on(j) // per_o, i, on(j) % per_o))
    else:
        out_shape = jax.ShapeDtypeStruct((m, n), out_dtype)
        out_spec = pl.BlockSpec((tm, tn), lambda i, j, k: (i, on(j)))
    dims = (((0 if ta else 1,), (1 if tb else 0,)), ((), ()))

    def body(a_ref, b_ref, o_ref, *acc):
        p = lax.dot_general(a_ref[...].astype(BF16), b_ref[...].astype(BF16), dims, preferred_element_type=F32)
        if nk == 1:
            o_ref[...] = p.astype(o_ref.dtype)
        else:
            k = pl.program_id(2)

            @pl.when(k == 0)
            def _():
                acc[0][...] = p

            @pl.when(k > 0)
            def _():
                acc[0][...] += p

            @pl.when(k == nk - 1)
            def _():
                o_ref[...] = acc[0][...].astype(o_ref.dtype)

    return pl.pallas_call(
        body,
        name=name,
        out_shape=out_shape,
        grid=(m // tm, n // tn, nk),
        in_specs=[a_spec, b_spec],
        out_specs=out_spec,
        scratch_shapes=[] if nk == 1 else [pltpu.VMEM((tm, tn), F32)],
        compiler_params=_cp("parallel", "parallel", "arbitrary"),
    )(a, b)


def _rope_tables(pos_col, invf):
    s = pos_col.shape[0]
    ts = _tile(s, 1024, SUBLANES)
    half = ROPE // 2

    def body(p_ref, f_ref, c_ref, sa_ref, sb_ref):
        ang = p_ref[...] * f_ref[...]
        lane = lax.broadcasted_iota(jnp.int32, ang.shape, 1)
        cs, sn = jnp.cos(ang), jnp.sin(ang)
        c_ref[...] = jnp.where(lane < ROPE, cs, 0.0)
        sa_ref[...] = jnp.where((lane >= half) & (lane < ROPE), sn, 0.0)
        sb_ref[...] = jnp.where(lane < half, -sn, 0.0)

    tab = jax.ShapeDtypeStruct((s, LANES), F32)
    return pl.pallas_call(
        body,
        name="rope_tables",
        out_shape=(tab, tab, tab),
        grid=(s // ts,),
        in_specs=[pl.BlockSpec((ts, 1), lambda i: (i, 0)), pl.BlockSpec((1, LANES), lambda i: (0, 0))],
        out_specs=[pl.BlockSpec((ts, LANES), lambda i: (i, 0))] * 3,
        compiler_params=_cp("parallel"),
    )(pos_col, invf)


def _widen(t, w):
    return t if w == LANES else jnp.tile(t, (1, w // LANES))


def _rope(x, c, sa, sb):
    w = x.shape[1]
    c, sa, sb = _widen(c, w), _widen(sa, w), _widen(sb, w)
    return x * c + pltpu.roll(x, ROPE // 2, 1) * sa + pltpu.roll(x, w - ROPE // 2, 1) * sb


def _rope_t(d, c, sa, sb):
    w = d.shape[1]
    c, sa, sb = _widen(c, w), _widen(sa, w), _widen(sb, w)
    return d * c + pltpu.roll(d * sa, w - ROPE // 2, 1) + pltpu.roll(d * sb, ROPE // 2, 1)


def _ada_fwd(c_all, w, b):
    d, nc = w.shape
    tn = _tile(nc, 512, LANES)

    def body(c_ref, w_ref, b_ref, o_ref, ca_ref):
        cv = c_ref[...]
        ca = cv * jax.nn.sigmoid(cv)
        ca_ref[...] = ca
        o_ref[...] = jnp.dot(ca.astype(BF16), w_ref[...].astype(BF16), preferred_element_type=F32) + b_ref[...]

    return pl.pallas_call(
        body,
        name="ada_fwd",
        out_shape=(jax.ShapeDtypeStruct((N_DEV, nc), F32), jax.ShapeDtypeStruct((N_DEV, d), F32)),
        grid=(nc // tn,),
        in_specs=[
            pl.BlockSpec((N_DEV, d), lambda j: (0, 0)),
            pl.BlockSpec((d, tn), lambda j: (0, j)),
            pl.BlockSpec((1, tn), lambda j: (0, j)),
        ],
        out_specs=[pl.BlockSpec((N_DEV, tn), lambda j: (0, j)), pl.BlockSpec((N_DEV, d), lambda j: (0, 0))],
        compiler_params=_cp("arbitrary"),
    )(c_all, w, b)


def _rows(ts, d):
    return pl.BlockSpec((ts, d), lambda i: (i, 0))


def _vec(d):
    return pl.BlockSpec((1, d), lambda i: (0, 0))


def _sums(d):
    return pl.BlockSpec((SUBLANES, d), lambda i: (0, 0))


def _acc_rows(ref, i, rows):
    @pl.when(i == 0)
    def _():
        ref[...] = jnp.zeros(ref.shape, ref.dtype)

    for k, r in enumerate(rows):
        ref[k : k + 1, :] += r


def _pre_fwd(x, g, sc, sh):
    s, d = x.shape
    ts = _tile(s, 512, SUBLANES)

    def body(x_ref, g_ref, sc_ref, sh_ref, h_ref):
        xv = x_ref[...]
        h_ref[...] = (((xv * _rsq(xv)) * g_ref[...]) * (1.0 + sc_ref[...]) + sh_ref[...]).astype(BF16)

    return pl.pallas_call(
        body,
        name="pre_mix_fwd",
        out_shape=jax.ShapeDtypeStruct((s, d), BF16),
        grid=(s // ts,),
        in_specs=[_rows(ts, d), _vec(d), _vec(d), _vec(d)],
        out_specs=_rows(ts, d),
        compiler_params=_cp("parallel"),
    )(x, g, sc, sh)


def _mid_fwd(x0, mix, g_post, gt, g_pre, sc, sh):
    s, d = x0.shape
    ts = _tile(s, 256, SUBLANES)

    def body(x_ref, m_ref, gp_ref, gt_ref, g_ref, sc_ref, sh_ref, x1_ref, h_ref):
        mv = m_ref[...]
        x1 = x_ref[...] + gt_ref[...] * ((mv * _rsq(mv)) * gp_ref[...])
        x1_ref[...] = x1
        h_ref[...] = (((x1 * _rsq(x1)) * g_ref[...]) * (1.0 + sc_ref[...]) + sh_ref[...]).astype(BF16)

    return pl.pallas_call(
        body,
        name="mid_fwd",
        out_shape=(jax.ShapeDtypeStruct((s, d), F32), jax.ShapeDtypeStruct((s, d), BF16)),
        grid=(s // ts,),
        in_specs=[_rows(ts, d), _rows(ts, d)] + [_vec(d)] * 5,
        out_specs=[_rows(ts, d), _rows(ts, d)],
        compiler_params=_cp("parallel"),
    )(x0, mix, g_post, gt, g_pre, sc, sh)


def _final(x1, y, tgt, g_post, gt):
    s, d = x1.shape
    ts = _tile(s, 256, SUBLANES)
    ni = s // ts

    def body(x_ref, y_ref, t_ref, gp_ref, gt_ref, dx_ref, dy_ref, s_ref):
        i = pl.program_id(0)
        yv, gp, gt_v = y_ref[...], gp_ref[...], gt_ref[...]
        r = _rsq(yv)
        n = yv * r
        err = (x_ref[...] + gt_v * (n * gp)) - t_ref[...]
        dx = err * (1.0 / d)
        dx_ref[...] = dx
        dy_ref[...] = _norm_bwd(dx * (gt_v * gp), n, r).astype(BF16)
        _acc_rows(s_ref, i, [_colsum(dx * (n * gp)), _colsum(dx * gt_v * n), _colsum(err * err)])

        @pl.when(i == ni - 1)
        def _():
            tot = jnp.sum(s_ref[2:3, :], axis=1, keepdims=True) * (0.5 / d)
            s_ref[3:4, :] = jnp.broadcast_to(tot, (1, d))

    return pl.pallas_call(
        body,
        name="final_fwd_bwd",
        out_shape=(
            jax.ShapeDtypeStruct((s, d), F32),
            jax.ShapeDtypeStruct((s, d), BF16),
            jax.ShapeDtypeStruct((SUBLANES, d), F32),
        ),
        grid=(ni,),
        in_specs=[_rows(ts, d)] * 3 + [_vec(d)] * 2,
        out_specs=[_rows(ts, d), _rows(ts, d), _sums(d)],
        compiler_params=_cp("arbitrary"),
    )(x1, y, tgt, g_post, gt)


def _mid_bwd(dh2, dx2, x1, mix, g_pre, sc, g_post, gt):
    s, d = x1.shape
    ts = _tile(s, 256, SUBLANES)

    def body(dh_ref, dx2_ref, x_ref, m_ref, g_ref, sc_ref, gp_ref, gt_ref, dx1_ref, dm_ref, s_ref):
        i = pl.program_id(0)
        dh, xv, mv = dh_ref[...], x_ref[...], m_ref[...]
        g, sc_v, gp, gt_v = g_ref[...], sc_ref[...], gp_ref[...], gt_ref[...]
        r1 = _rsq(xv)
        n1 = xv * r1
        dx1 = dx2_ref[...] + _norm_bwd(dh * (g * (1.0 + sc_v)), n1, r1)
        dx1_ref[...] = dx1
        rm = _rsq(mv)
        nm = mv * rm
        dm_ref[...] = _norm_bwd(dx1 * (gt_v * gp), nm, rm).astype(BF16)
        _acc_rows(
            s_ref,
            i,
            [
                _colsum(dh),
                _colsum(dh * (n1 * g)),
                _colsum(dh * (1.0 + sc_v) * n1),
                _colsum(dx1 * (nm * gp)),
                _colsum(dx1 * gt_v * nm),
            ],
        )

    return pl.pallas_call(
        body,
        name="mid_bwd",
        out_shape=(
            jax.ShapeDtypeStruct((s, d), F32),
            jax.ShapeDtypeStruct((s, d), BF16),
            jax.ShapeDtypeStruct((SUBLANES, d), F32),
        ),
        grid=(s // ts,),
        in_specs=[_rows(ts, d)] * 4 + [_vec(d)] * 4,
        out_specs=[_rows(ts, d), _rows(ts, d), _sums(d)],
        compiler_params=_cp("arbitrary"),
    )(dh2, dx2, x1, mix, g_pre, sc, g_post, gt)


def _first_bwd(dh1, dx1, x0, g, sc):
    s, d = x0.shape
    ts = _tile(s, 256, SUBLANES)

    def body(dh_ref, dx1_ref, x_ref, g_ref, sc_ref, dx_ref, s_ref):
        i = pl.program_id(0)
        dh, xv, gv, sc_v = dh_ref[...], x_ref[...], g_ref[...], sc_ref[...]
        r = _rsq(xv)
        n = xv * r
        dx_ref[...] = dx1_ref[...] + _norm_bwd(dh * (gv * (1.0 + sc_v)), n, r)
        _acc_rows(s_ref, i, [_colsum(dh), _colsum(dh * (n * gv)), _colsum(dh * (1.0 + sc_v) * n)])

    return pl.pallas_call(
        body,
        name="first_bwd",
        out_shape=(jax.ShapeDtypeStruct((s, d), F32), jax.ShapeDtypeStruct((SUBLANES, d), F32)),
        grid=(s // ts,),
        in_specs=[_rows(ts, d)] * 3 + [_vec(d)] * 2,
        out_specs=[_rows(ts, d), _sums(d)],
        compiler_params=_cp("arbitrary"),
    )(dh1, dx1, x0, g, sc)


def _latent_fwd(proj, g_q, g_kv, tabs, lb):
    s = proj.shape[0]
    ql, kl = g_q.shape[1], g_kv.shape[1]
    ts = _tile(s, 512, SUBLANES)

    def body(p_ref, gq_ref, gk_ref, c_ref, sa_ref, sb_ref, q_ref, kv_ref, kr_ref):
        pv = p_ref[...]
        q, kv, kr = pv[:, :ql], pv[:, ql : ql + kl], pv[:, ql + kl : ql + kl + HEAD_PAD]
        q_ref[...] = ((q * _rsq(q)) * gq_ref[...]).astype(BF16)
        kv_ref[...] = ((kv * _rsq(kv)) * gk_ref[...]).astype(BF16)
        kr_ref[...] = _rope(kr, c_ref[...], sa_ref[...], sb_ref[...]).astype(BF16)

    return pl.pallas_call(
        body,
        name="latent_fwd",
        out_shape=(
            jax.ShapeDtypeStruct((s, ql), BF16),
            jax.ShapeDtypeStruct((s, kl), BF16),
            jax.ShapeDtypeStruct((s, HEAD_PAD), BF16),
        ),
        grid=(s // ts,),
        in_specs=[_rows(ts, lb), _vec(ql), _vec(kl)] + [_rows(ts, LANES)] * 3,
        out_specs=[_rows(ts, ql), _rows(ts, kl), _rows(ts, HEAD_PAD)],
        compiler_params=_cp("parallel"),
    )(proj, g_q, g_kv, *tabs)


def _latent_bwd(proj, dqn, dkvn, dkr_h, g_q, g_kv, tabs, lb):
    s = proj.shape[0]
    ql, kl = g_q.shape[1], g_kv.shape[1]
    hw = dkr_h.shape[1]
    ts = _tile(s, 256, SUBLANES)
    pad = lb - ql - kl - HEAD_PAD

    def body(p_ref, dq_ref, dkv_ref, dkr_ref, gq_ref, gk_ref, c_ref, sa_ref, sb_ref, o_ref, s_ref):
        i = pl.program_id(0)
        pv = p_ref[...]
        q, kv = pv[:, :ql], pv[:, ql : ql + kl]
        dqn_v, dkvn_v = dq_ref[...], dkv_ref[...]
        rq = _rsq(q)
        nq = q * rq
        rk = _rsq(kv)
        nk = kv * rk
        dkr = dkr_ref[:, :HEAD_PAD]
        for h in range(1, hw // HEAD_PAD):
            dkr = dkr + dkr_ref[:, h * HEAD_PAD : (h + 1) * HEAD_PAD]
        parts = [
            _norm_bwd(dqn_v * gq_ref[...], nq, rq).astype(BF16),
            _norm_bwd(dkvn_v * gk_ref[...], nk, rk).astype(BF16),
            _rope_t(dkr, c_ref[...], sa_ref[...], sb_ref[...]).astype(BF16),
        ]
        if pad:
            parts.append(jnp.zeros((ts, pad), BF16))
        o_ref[...] = jnp.concatenate(parts, axis=1)
        row = [_colsum(dqn_v * nq), _colsum(dkvn_v * nk), jnp.zeros((1, lb - ql - kl), F32)]
        _acc_rows(s_ref, i, [jnp.concatenate(row, axis=1)])

    return pl.pallas_call(
        body,
        name="latent_bwd",
        out_shape=(jax.ShapeDtypeStruct((s, lb), BF16), jax.ShapeDtypeStruct((SUBLANES, lb), F32)),
        grid=(s // ts,),
        in_specs=[_rows(ts, lb), _rows(ts, ql), _rows(ts, kl), _rows(ts, hw)]
        + [_vec(ql), _vec(kl)]
        + [_rows(ts, LANES)] * 3,
        out_specs=[_rows(ts, lb), _sums(lb)],
        compiler_params=_cp("arbitrary"),
    )(proj, dqn, dkvn, dkr_h, g_q, g_kv, *tabs)


def _conv3(ext, w, b):
    return (pltpu.roll(ext, 2, 0) * w[0:1] + pltpu.roll(ext, 1, 0) * w[1:2]) + ext * w[2:3] + b


def _conv3_t(du, w):
    n = du.shape[0]
    return du * w[2:3] + pltpu.roll(du, n - 1, 0) * w[1:2] + pltpu.roll(du, n - 2, 0) * w[0:1]


def _halo_maps(ts, s):
    r8, last = ts // SUBLANES, s // SUBLANES - 1
    prev = lambda i: jnp.maximum(i * r8 - 1, 0)
    nxt = lambda i: jnp.minimum((i + 1) * r8, last)
    return prev, nxt


def _mixer_fwd(cat, proj, cw, cb, lb, col0):
    s = proj.shape[0]
    cwid = cw.shape[1]
    ts = _tile(s, 512, SUBLANES)
    tc = _tile(cwid, 512, LANES)
    assert lb % tc == 0 and col0 % tc == 0
    nj, ob, oc = cwid // tc, lb // tc, col0 // tc
    prev, _ = _halo_maps(ts, s)

    def body(_, gb_ref, gc_ref, ci_ref, pgc_ref, pci_ref, w_ref, b_ref, o_ref):
        keep = jnp.where(pl.program_id(1) > 0, 1.0, 0.0)
        ext = jnp.concatenate([pgc_ref[...] * pci_ref[...] * keep, gc_ref[...] * ci_ref[...]], axis=0)
        o_ref[...] = (gb_ref[...] * _conv3(ext, w_ref[...], b_ref[...])[SUBLANES:]).astype(BF16)

    def col(k):
        return pl.BlockSpec((ts, tc), lambda j, i: (i, ob + k * nj + j))

    def halo(k):
        return pl.BlockSpec((SUBLANES, tc), lambda j, i: (prev(i), ob + k * nj + j))

    return pl.pallas_call(
        body,
        name="mixer_fwd",
        out_shape=jax.ShapeDtypeStruct(cat.shape, BF16),
        grid=(nj, s // ts),
        in_specs=[pl.BlockSpec(memory_space=pl.ANY), col(0), col(1), col(2), halo(1), halo(2)]
        + [pl.BlockSpec((CONV_K, tc), lambda j, i: (0, j)), pl.BlockSpec((1, tc), lambda j, i: (0, j))],
        out_specs=pl.BlockSpec((ts, tc), lambda j, i: (i, oc + j)),
        input_output_aliases={0: 0},
        compiler_params=_cp("parallel", "arbitrary"),
    )(cat, proj, proj, proj, proj, proj, cw, cb)


def _mixer_bwd(dcat, proj, cw, cb, lb, col0):
    s = proj.shape[0]
    cwid = cw.shape[1]
    ts = _tile(s, 256, SUBLANES)
    tc = _tile(cwid, 512, LANES)
    nj, ob, oc = cwid // tc, lb // tc, col0 // tc
    ni = s // ts
    prev, nxt = _halo_maps(ts, s)

    def body(d_ref, dn_ref, gb_ref, gbn_ref, gc_ref, gcp_ref, gcn_ref, ci_ref, cip_ref, cin_ref, w_ref, b_ref,
             dgb_ref, dgc_ref, dci_ref, s_ref):
        i = pl.program_id(1)
        keep_p = jnp.where(i > 0, 1.0, 0.0)
        keep_n = jnp.where(i < ni - 1, 1.0, 0.0)
        w = w_ref[...]
        gc = jnp.concatenate([gcp_ref[...], gc_ref[...], gcn_ref[...]], axis=0)
        ci = jnp.concatenate([cip_ref[...] * keep_p, ci_ref[...], cin_ref[...]], axis=0)
        u = gc * ci
        cv = _conv3(u, w, b_ref[...])[SUBLANES:]
        dco = jnp.concatenate([d_ref[...], dn_ref[...] * keep_n], axis=0)
        gb = jnp.concatenate([gb_ref[...], gbn_ref[...]], axis=0)
        dgb_ref[...] = (dco * cv)[:ts].astype(BF16)
        dcv = dco * gb
        du = _conv3_t(dcv, w)[:ts]
        dgc_ref[...] = (du * ci_ref[...]).astype(BF16)
        dci_ref[...] = (du * gc_ref[...]).astype(BF16)
        dt = dcv[:ts]
        u1, u2 = pltpu.roll(u, 1, 0), pltpu.roll(u, 2, 0)
        lo, hi = SUBLANES, SUBLANES + ts
        _acc_rows(s_ref, i, [_colsum(dt * u2[lo:hi]), _colsum(dt * u1[lo:hi]), _colsum(dt * u[lo:hi]), _colsum(dt)])

    def col(k):
        return pl.BlockSpec((ts, tc), lambda j, i: (i, ob + k * nj + j))

    def halo(k, which):
        return pl.BlockSpec((SUBLANES, tc), lambda j, i: (which(i), ob + k * nj + j))

    out_col = [pl.BlockSpec((ts, tc), lambda j, i: (i, j))] * 3
    grad = jax.ShapeDtypeStruct((s, cwid), BF16)
    return pl.pallas_call(
        body,
        name="mixer_bwd",
        out_shape=(grad, grad, grad, jax.ShapeDtypeStruct((SUBLANES, cwid), F32)),
        grid=(nj, ni),
        in_specs=[
            pl.BlockSpec((ts, tc), lambda j, i: (i, oc + j)),
            pl.BlockSpec((SUBLANES, tc), lambda j, i: (nxt(i), oc + j)),
            col(0), halo(0, nxt),
            col(1), halo(1, prev), halo(1, nxt),
            col(2), halo(2, prev), halo(2, nxt),
            pl.BlockSpec((CONV_K, tc), lambda j, i: (0, j)),
            pl.BlockSpec((1, tc), lambda j, i: (0, j)),
        ],
        out_specs=out_col + [pl.BlockSpec((SUBLANES, tc), lambda j, i: (0, j))],
        compiler_params=_cp("parallel", "arbitrary"),
    )(dcat, dcat, proj, proj, proj, proj, proj, proj, proj, proj, cw, cb)


def _pair_tile(f):
    return _tile(f, 1408, LANES)


def _pair_perm(f):
    nj = f // _pair_tile(f)
    return lambda p: (p % 2) * nj + p // 2


def _pair_cols(a):
    r, f2 = a.shape
    tc = _pair_tile(f2 // 2)
    return a.reshape(r, 2, f2 // (2 * tc), tc).transpose(0, 2, 1, 3).reshape(r, f2)


def _unpair_cols(a):
    r, f2 = a.shape
    tc = _pair_tile(f2 // 2)
    return a.reshape(r, f2 // (2 * tc), 2, tc).transpose(0, 2, 1, 3).reshape(r, f2)


def _ffn_act_fwd(up, cw, cb):
    s, f2 = up.shape
    f = f2 // 2
    ts = _tile(s, 256, SUBLANES)
    tc = _pair_tile(f)
    prev, _ = _halo_maps(ts, s)

    def body(u_ref, p_ref, w_ref, b_ref, o_ref):
        keep = jnp.where(pl.program_id(1) > 0, 1.0, 0.0)
        ext = jnp.concatenate([p_ref[...] * keep, u_ref[...]], axis=0)
        u = _conv3(ext, w_ref[...], b_ref[...])[SUBLANES:]
        a, g = u[:, :tc], u[:, tc:]
        o_ref[...] = ((g * jax.nn.sigmoid(g)) * a).astype(BF16)

    def pair(rows, which):
        return pl.BlockSpec((rows, 2 * tc), lambda j, i: (which(i), j))

    return pl.pallas_call(
        body,
        name="ffn_act_fwd",
        out_shape=jax.ShapeDtypeStruct((s, f), BF16),
        grid=(f // tc, s // ts),
        in_specs=[pair(ts, lambda i: i), pair(SUBLANES, prev), pair(CONV_K, lambda i: 0), pair(1, lambda i: 0)],
        out_specs=pl.BlockSpec((ts, tc), lambda j, i: (i, j)),
        compiler_params=_cp("parallel", "arbitrary"),
    )(up, up, cw, cb)


def _ffn_act_bwd(dact, up, cw, cb):
    s, f2 = up.shape
    f = f2 // 2
    ts = _tile(s, 128, SUBLANES)
    tc = _pair_tile(f)
    nj, ni = f // tc, s // ts
    prev, nxt = _halo_maps(ts, s)

    def body(d_ref, dn_ref, u_ref, up_ref, un_ref, w_ref, b_ref, dup_ref, s_ref):
        i = pl.program_id(1)
        keep_p = jnp.where(i > 0, 1.0, 0.0)
        keep_n = jnp.where(i < ni - 1, 1.0, 0.0)
        w = w_ref[...]
        ext = jnp.concatenate([up_ref[...] * keep_p, u_ref[...], un_ref[...]], axis=0)
        u = _conv3(ext, w, b_ref[...])[SUBLANES:]
        a, g = u[:, :tc], u[:, tc:]
        dact_v = jnp.concatenate([d_ref[...], dn_ref[...] * keep_n], axis=0)
        sg = jax.nn.sigmoid(g)
        du = jnp.concatenate([dact_v * (g * sg), dact_v * a * (sg * (1.0 + g * (1.0 - sg)))], axis=1)
        dup_ref[...] = _conv3_t(du, w)[:ts].astype(BF16)
        dt = du[:ts]
        lo, hi = SUBLANES, SUBLANES + ts
        e1, e2 = pltpu.roll(ext, 1, 0), pltpu.roll(ext, 2, 0)
        _acc_rows(s_ref, i, [_colsum(dt * e2[lo:hi]), _colsum(dt * e1[lo:hi]), _colsum(dt * ext[lo:hi]), _colsum(dt)])

    def pair(rows, which):
        return pl.BlockSpec((rows, 2 * tc), lambda j, i: (which(i), j))

    return pl.pallas_call(
        body,
        name="ffn_act_bwd",
        out_shape=(jax.ShapeDtypeStruct((s, f2), BF16), jax.ShapeDtypeStruct((SUBLANES, f2), F32)),
        grid=(nj, ni),
        in_specs=[
            pl.BlockSpec((ts, tc), lambda j, i: (i, j)),
            pl.BlockSpec((SUBLANES, tc), lambda j, i: (nxt(i), j)),
            pair(ts, lambda i: i), pair(SUBLANES, prev), pair(SUBLANES, nxt),
            pair(CONV_K, lambda i: 0), pair(1, lambda i: 0),
        ],
        out_specs=[pair(ts, lambda i: i), pair(SUBLANES, lambda i: 0)],
        compiler_params=_cp("parallel", "arbitrary"),
    )(dact, dact, up, up, up, cw, cb)


ATT_SCALE = 1.0 / math.sqrt(NOPE + ROPE)
LOG2E = math.log2(math.e)
ATT_C2 = ATT_SCALE * LOG2E
STAT_SPLIT = 64
NT = (((1,), (1,)), ((), ()))
TN = (((0,), (0,)), ((), ()))


def _head_cat(q, kv, kr, tabs, n_heads):
    s, w2 = q.shape
    w = w2 // 2
    ts = _tile(s, 512, SUBLANES)
    hd = NOPE + HEAD_PAD

    def body(q_ref, kv_ref, kr_ref, c_ref, sa_ref, sb_ref, qc_ref, kc_ref):
        qv = q_ref[...]
        qr = _rope(qv[:, w:], c_ref[...], sa_ref[...], sb_ref[...]).astype(BF16)
        krv = kr_ref[...]
        for h in range(n_heads):
            qc_ref[:, h * hd : h * hd + NOPE] = qv[:, h * NOPE : (h + 1) * NOPE].astype(BF16)
            qc_ref[:, h * hd + NOPE : (h + 1) * hd] = qr[:, h * HEAD_PAD : (h + 1) * HEAD_PAD]
            kc_ref[:, h * hd : h * hd + NOPE] = kv_ref[:, h * NOPE : (h + 1) * NOPE]
            kc_ref[:, h * hd + NOPE : (h + 1) * hd] = krv

    out = jax.ShapeDtypeStruct((s, n_heads * hd), BF16)
    return pl.pallas_call(
        body,
        name="head_cat",
        out_shape=(out, out),
        grid=(s // ts,),
        in_specs=[_rows(ts, w2), _rows(ts, w), _rows(ts, HEAD_PAD)] + [_rows(ts, LANES)] * 3,
        out_specs=[_rows(ts, n_heads * hd)] * 2,
        compiler_params=_cp("parallel"),
    )(q, kv, kr, *tabs)


def _attn_fwd(qc, kc, kv, n_heads, cat_cols):
    s = qc.shape[0]
    t = _tile(s, ATT_FWD_BLOCK, LANES)
    sub = _tile(t, ATT_FWD_SUB, LANES)
    hh = n_heads
    hd = NOPE + HEAD_PAD

    def body(q_ref, k_ref, v_ref, o_ref, lse_ref, m_s, l_s, acc_s):
        i = pl.program_id(1)
        m_s[...] = jnp.full(m_s.shape, NEG, F32)
        l_s[...] = jnp.zeros(l_s.shape, F32)
        acc_s[...] = jnp.zeros(acc_s.shape, F32)

        def chunk(k0, diag):
            m_all, l_all, acc_all = m_s[...], l_s[...], acc_s[...]
            new_m, new_l, new_acc = [], [], []

            def scores(r0):
                ncol = r0 + sub if diag else t
                return lax.dot_general(q_ref[pl.ds(r0, sub), :], k_ref[pl.ds(k0, ncol), :], NT, preferred_element_type=F32)

            sc_next = scores(0)
            for r0 in range(0, t, sub):
                ncol = r0 + sub if diag else t
                sc = sc_next
                if r0 + sub < t:
                    sc_next = scores(r0 + sub)
                if diag:
                    row = lax.broadcasted_iota(jnp.int32, sc.shape, 0) + r0
                    col = lax.broadcasted_iota(jnp.int32, sc.shape, 1)
                    sc = jnp.where(col <= row, sc, NEG)
                m_prev = m_all[r0 : r0 + sub]
                m_new = jnp.maximum(m_prev, jnp.max(sc, axis=1, keepdims=True))
                alpha = jnp.exp2((m_prev - m_new) * ATT_C2)
                p = jnp.exp2((sc - m_new) * ATT_C2)
                pv = jnp.dot(p.astype(BF16), v_ref[pl.ds(k0, ncol), :], preferred_element_type=F32)
                new_m.append(m_new)
                new_l.append(alpha * l_all[r0 : r0 + sub] + jnp.sum(p, axis=1, keepdims=True))
                new_acc.append(alpha * acc_all[r0 : r0 + sub] + pv)
            m_s[...] = jnp.concatenate(new_m, axis=0)
            l_s[...] = jnp.concatenate(new_l, axis=0)
            acc_s[...] = jnp.concatenate(new_acc, axis=0)

        def loop_body(k, carry):
            chunk(pl.multiple_of(k * t, t), False)
            return carry

        lax.fori_loop(0, i, loop_body, 0)
        chunk(pl.multiple_of(i * t, t), True)
        l = l_s[...]
        o_ref[...] = (acc_s[...] / l).astype(BF16)
        lse_ref[...] = jnp.broadcast_to(m_s[...] * ATT_C2 + jnp.log(l) * LOG2E, lse_ref.shape)

    return pl.pallas_call(
        body,
        name="attn_fwd",
        out_shape=(jax.ShapeDtypeStruct((s, cat_cols), BF16), jax.ShapeDtypeStruct((s, hh * LANES), F32)),
        grid=(hh, s // t),
        in_specs=[
            pl.BlockSpec((t, hd), lambda h, i: (i, h)),
            pl.BlockSpec((s, hd), lambda h, i: (0, h)),
            pl.BlockSpec((s, VDIM), lambda h, i: (0, hh + h)),
        ],
        out_specs=[pl.BlockSpec((t, VDIM), lambda h, i: (i, h)), pl.BlockSpec((t, LANES), lambda h, i: (i, h))],
        scratch_shapes=[pltpu.VMEM((t, 1), F32), pltpu.VMEM((t, 1), F32), pltpu.VMEM((t, VDIM), F32)],
        compiler_params=_cp("parallel", "parallel"),
    )(qc, kc, kv)


def _attn_bwd_prep(cat, dcat, lse2, n_heads):
    s, w = lse2.shape
    ts = _tile(s, 512, SUBLANES)

    def body(o_ref, do_ref, lse_ref, dob_ref, st_ref):
        do = do_ref[...]
        dob_ref[...] = do.astype(BF16)
        prod = do * o_ref[...].astype(F32)
        lane = lax.broadcasted_iota(jnp.int32, (ts, LANES), 1)
        for h in range(n_heads):
            cols = slice(h * LANES, (h + 1) * LANES)
            dsum = jnp.sum(prod[:, cols], axis=1, keepdims=True)
            st_ref[:, cols] = jnp.where(lane < STAT_SPLIT, lse_ref[:, cols], dsum)

    return pl.pallas_call(
        body,
        name="attn_bwd_prep",
        out_shape=(jax.ShapeDtypeStruct((s, w), BF16), jax.ShapeDtypeStruct((s, w), F32)),
        grid=(s // ts,),
        in_specs=[_rows(ts, w)] * 3,
        out_specs=[_rows(ts, w)] * 2,
        compiler_params=_cp("parallel"),
    )(cat, dcat, lse2)


def _attn_bwd(qc, kc, kv, dob, stats, n_heads):
    s = qc.shape[0]
    t = _tile(s, ATT_BWD_BLOCK, LANES)
    sub = _tile(t, ATT_BWD_SUB, LANES)
    nb = s // t
    hh = n_heads
    hd = NOPE + HEAD_PAD
    w = hh * LANES

    def body(q_ref, k_ref, v_ref, do_ref, st_ref, dq_ref, dkn_ref, dv_ref, dkr_ref, dk_s, dv_s):
        j = pl.program_id(1)

        @pl.when(j == 0)
        def _():
            dq_ref[...] = jnp.zeros(dq_ref.shape, F32)

        dk_s[...] = jnp.zeros(dk_s.shape, F32)
        dv_s[...] = jnp.zeros(dv_s.shape, F32)

        def pair(i0, diag):
            def width(r0):
                return r0 + sub if diag else t

            def products(r0):
                rows = pl.ds(i0 + r0, sub)
                sc_ = lax.dot_general(q_ref[rows, :], k_ref[0 : width(r0), :], NT, preferred_element_type=F32)
                dp_ = lax.dot_general(do_ref[rows, :], v_ref[0 : width(r0), :], NT, preferred_element_type=F32)
                return sc_, dp_

            nxt = products(0)
            for r0 in range(0, t, sub):
                ncol = width(r0)
                rows = pl.ds(i0 + r0, sub)
                kk = k_ref[0:ncol, :]
                qq, do, st = q_ref[rows, :], do_ref[rows, :], st_ref[rows, :]
                sc, dp = nxt
                if r0 + sub < t:
                    nxt = products(r0 + sub)
                if diag:
                    row = lax.broadcasted_iota(jnp.int32, sc.shape, 0) + r0
                    col = lax.broadcasted_iota(jnp.int32, sc.shape, 1)
                    sc = jnp.where(col <= row, sc, NEG)
                p = jnp.exp2(sc * ATT_C2 - st[:, 0:1])
                dv_s[0:ncol, :] += lax.dot_general(p.astype(BF16), do, TN, preferred_element_type=F32)
                ds = (p * (dp - st[:, STAT_SPLIT : STAT_SPLIT + 1]) * ATT_SCALE).astype(BF16)
                dk_s[0:ncol, :] += lax.dot_general(ds, qq, TN, preferred_element_type=F32)
                dq_ref[rows, :] += jnp.dot(ds, kk, preferred_element_type=F32)

        pair(pl.multiple_of(j * t, t), True)

        def loop_body(i, carry):
            pair(pl.multiple_of(i * t, t), False)
            return carry

        lax.fori_loop(j + 1, nb, loop_body, 0)
        dkn_ref[...] = dk_s[:, :NOPE].astype(BF16)
        dv_ref[...] = dv_s[...].astype(BF16)
        dkr_ref[...] = dk_s[:, NOPE:]

    whole = lambda width, off: pl.BlockSpec((s, width), lambda h, j: (0, off + h))
    blk = lambda width, off: pl.BlockSpec((t, width), lambda h, j: (j, off + h))
    return pl.pallas_call(
        body,
        name="attn_bwd",
        out_shape=(
            jax.ShapeDtypeStruct((s, hh * hd), F32),
            jax.ShapeDtypeStruct((s, w), BF16),
            jax.ShapeDtypeStruct((s, w), BF16),
            jax.ShapeDtypeStruct((s, w), F32),
        ),
        grid=(hh, nb),
        in_specs=[whole(hd, 0), blk(hd, 0), blk(VDIM, hh), whole(VDIM, 0), whole(LANES, 0)],
        out_specs=[whole(hd, 0), blk(NOPE, 0), blk(VDIM, 0), blk(HEAD_PAD, 0)],
        scratch_shapes=[pltpu.VMEM((t, hd), F32), pltpu.VMEM((t, VDIM), F32)],
        compiler_params=_cp("parallel", "arbitrary"),
    )(qc, kc, kv, dob, stats)


def _dq_unrope(dq, tabs, n_heads):
    s = dq.shape[0]
    hd = NOPE + HEAD_PAD
    w = n_heads * LANES
    ts = _tile(s, 512, SUBLANES)

    def body(d_ref, c_ref, sa_ref, sb_ref, o_ref):
        c, sa, sb = c_ref[...], sa_ref[...], sb_ref[...]
        for h in range(n_heads):
            o_ref[:, h * NOPE : (h + 1) * NOPE] = d_ref[:, h * hd : h * hd + NOPE].astype(BF16)
            rot = _rope_t(d_ref[:, h * hd + NOPE : (h + 1) * hd], c, sa, sb)
            o_ref[:, w + h * HEAD_PAD : w + (h + 1) * HEAD_PAD] = rot.astype(BF16)

    return pl.pallas_call(
        body,
        name="dq_unrope",
        out_shape=jax.ShapeDtypeStruct((s, 2 * w), BF16),
        grid=(s // ts,),
        in_specs=[_rows(ts, n_heads * hd)] + [_rows(ts, LANES)] * 3,
        out_specs=_rows(ts, 2 * w),
        compiler_params=_cp("parallel"),
    )(dq, *tabs)


def _adamw(w, m, v, grads, name):
    r, c = w.shape
    budget_rows = max(SUBLANES, (VMEM_LIMIT // 3) // (4 * c * 2 * (7 + len(grads))))
    tr = _tile(r, budget_rows, SUBLANES)
    ng = len(grads)
    c1 = 1.0 - ADAM_B1**ADAM_STEP
    c2 = 1.0 - ADAM_B2**ADAM_STEP

    def body(*refs):
        w_ref, m_ref, v_ref = refs[:3]
        g_ref, d_ref, nm_ref, nv_ref = refs[3 + ng :]
        g = refs[3][...]
        for extra in refs[4 : 3 + ng]:
            g = g + extra[...]
        mn = ADAM_B1 * m_ref[...] + (1.0 - ADAM_B1) * g
        vn = ADAM_B2 * v_ref[...] + (1.0 - ADAM_B2) * (g * g)
        g_ref[...] = g
        nm_ref[...] = mn
        nv_ref[...] = vn
        d_ref[...] = -ADAM_LR * ((mn / c1) / (jnp.sqrt(vn / c2) + ADAM_EPS) + ADAM_WD * w_ref[...])

    blk = pl.BlockSpec((tr, c), lambda i: (i, 0))
    out = jax.ShapeDtypeStruct((r, c), F32)
    return pl.pallas_call(
        body,
        name=name,
        out_shape=(out, out, out, out),
        grid=(r // tr,),
        in_specs=[blk] * (3 + ng),
        out_specs=[blk] * 4,
        compiler_params=_cp("parallel"),
    )(w, m, v, *grads)


def _ada_grad(ca_t, dm):
    d = ca_t.shape[0]
    nc = dm.shape[1]
    tn = _tile(nc, 512, LANES)

    def body(a_ref, b_ref, o_ref):
        o_ref[...] = jnp.dot(a_ref[...].astype(BF16), b_ref[...].astype(BF16), preferred_element_type=F32)

    return pl.pallas_call(
        body,
        name="ada_grad",
        out_shape=jax.ShapeDtypeStruct((d, nc), F32),
        grid=(nc // tn,),
        in_specs=[pl.BlockSpec((d, LANES), lambda j: (0, 0)), pl.BlockSpec((LANES, tn), lambda j: (0, j))],
        out_specs=pl.BlockSpec((d, tn), lambda j: (0, j)),
        compiler_params=_cp("parallel"),
    )(ca_t, dm)


def _sum_devices(g):
    n = g.shape[1]

    def body(g_ref, o_ref):
        acc = g_ref[0:SUBLANES, :]
        for dvc in range(1, N_DEV):
            acc = acc + g_ref[dvc * SUBLANES : (dvc + 1) * SUBLANES, :]
        o_ref[...] = acc

    return pl.pallas_call(
        body,
        name="sum_devices",
        out_shape=jax.ShapeDtypeStruct((SUBLANES, n), F32),
        in_specs=[pl.BlockSpec(memory_space=pltpu.VMEM)],
        out_specs=pl.BlockSpec(memory_space=pltpu.VMEM),
        compiler_params=pltpu.CompilerParams(vmem_limit_bytes=VMEM_LIMIT),
    )(g)


def _sum_chips(land, sent, name):
    _, r, c = land.shape
    tr = _tile(r, max(SUBLANES * 2, (VMEM_LIMIT // 4) // (c * 2 * (4 * N_CHIP + 4 * 2))), SUBLANES * 2)

    def body(l_ref, s_ref, o_ref):
        x, y, _ = _mesh_pos()
        me = 2 * x + y
        acc = jnp.where(me == 0, s_ref[0], l_ref[0]).astype(F32)
        for k in range(1, N_CHIP):
            acc = acc + jnp.where(me == k, s_ref[k], l_ref[k]).astype(F32)
        o_ref[...] = acc

    slots = pl.BlockSpec((N_CHIP, tr, c), lambda i: (0, i, 0))
    return pl.pallas_call(
        body,
        name=name,
        out_shape=jax.ShapeDtypeStruct((r, c), F32),
        grid=(r // tr,),
        in_specs=[slots, slots],
        out_specs=pl.BlockSpec((tr, c), lambda i: (i, 0)),
        compiler_params=_cp("parallel"),
    )(land, sent)


def _mesh_pos():
    return lax.axis_index("x"), lax.axis_index("y"), lax.axis_index("c")


def _other_chips(x, y):
    return [(1 - x, y), (x, 1 - y), (1 - x, 1 - y)]


def _all_gather8(x_shard, name):
    m_per, n = x_shard.shape

    def body(x_ref, out_ref, send_sems, recv_sems, local_sem):
        x, y, c = _mesh_pos()
        me, sibling = (x, y, c), (x, y, 1 - c)
        chips = _other_chips(x, y)

        def rows(px, py, pc):
            return out_ref.at[pl.ds((4 * px + 2 * py + pc) * m_per, m_per), :]

        def copy(k, block, to, src=None):
            return pltpu.make_async_remote_copy(
                src_ref=rows(*block) if src is None else src,
                dst_ref=rows(*block),
                send_sem=send_sems.at[k],
                recv_sem=recv_sems.at[k],
                device_id=to,
                device_id_type=MESH,
            )

        mine = pltpu.make_async_copy(x_ref, rows(*me), local_sem)
        mine.start()
        first = [copy(0, me, sibling, src=x_ref)]
        first += [copy(1 + j, me, (*chip, c), src=x_ref) for j, chip in enumerate(chips)]
        for cp in first:
            cp.start()
        passed = [copy(4 + j, (*chip, c), sibling) for j, chip in enumerate(chips)]
        for j, chip in enumerate(chips):
            copy(1 + j, (*chip, c), me).wait_recv()
            passed[j].start()
        copy(0, sibling, me).wait_recv()
        for j, chip in enumerate(chips):
            copy(4 + j, (*chip, 1 - c), me).wait_recv()
        for cp in first + passed:
            cp.wait_send()
        mine.wait()

    return pl.pallas_call(
        body,
        name=name,
        out_shape=jax.ShapeDtypeStruct((N_DEV * m_per, n), x_shard.dtype),
        in_specs=[pl.BlockSpec(memory_space=pltpu.VMEM)],
        out_specs=pl.BlockSpec(memory_space=pltpu.VMEM),
        scratch_shapes=[pltpu.SemaphoreType.DMA((7,)), pltpu.SemaphoreType.DMA((7,)), pltpu.SemaphoreType.DMA],
        compiler_params=pltpu.CompilerParams(vmem_limit_bytes=VMEM_LIMIT),
    )(x_shard)


HBM_SPEC = pl.BlockSpec(memory_space=pltpu.HBM)
SEM_SPEC = pl.BlockSpec(memory_space=pltpu.SEMAPHORE)
DATAFLOW = pltpu.SideEffectType.DATAFLOW_SIDE_EFFECTING


def _exchange_copies(ins, lands, send_sems, recv_sems, scatter):
    x, y, c = _mesh_pos()
    me = 2 * x + y
    sends, recvs = [], []
    for t in range(len(ins)):
        for r, (px, py) in enumerate(_other_chips(x, y)):
            peer = 2 * px + py

            def copy(src, dst, k=3 * t + r, to=(px, py, c)):
                return pltpu.make_async_remote_copy(
                    src_ref=src, dst_ref=dst, send_sem=send_sems.at[k], recv_sem=recv_sems.at[k], device_id=to, device_id_type=MESH
                )

            sends.append(copy(ins[t].at[peer] if scatter else ins[t], lands[t].at[me]))
            recvs.append(copy(ins[t].at[me] if scatter else ins[t], lands[t].at[peer]))
    return sends, recvs


def _exchange_start(arrs, scatter, name):
    nt = len(arrs)
    lands = [lax.empty(a.shape if scatter else (N_CHIP, *a.shape), a.dtype) for a in arrs]

    def body(*refs):
        ins, zones = refs[:nt], refs[nt : 2 * nt]
        send_sems, recv_sems, token = refs[2 * nt], refs[2 * nt + 1], refs[-1]
        sends, _ = _exchange_copies(ins, zones, send_sems, recv_sems, scatter)
        for cp in sends:
            cp.start()
        token[...] = jnp.zeros(token.shape, F32)

    bufs = list(arrs) + list(lands)
    return pl.pallas_call(
        body,
        name=name,
        out_shape=(
            pltpu.SemaphoreType.DMA((3 * nt,)),
            pltpu.SemaphoreType.DMA((3 * nt,)),
            *[pltpu.HBM(a.shape, a.dtype) for a in bufs],
            jax.ShapeDtypeStruct((SUBLANES, LANES), F32),
        ),
        in_specs=[HBM_SPEC] * (2 * nt),
        out_specs=(SEM_SPEC, SEM_SPEC, *[HBM_SPEC] * (2 * nt), pl.BlockSpec(memory_space=pltpu.VMEM)),
        input_output_aliases={k: 2 + k for k in range(2 * nt)},
        compiler_params=pltpu.CompilerParams(has_side_effects=DATAFLOW),
    )(*[pltpu.with_memory_space_constraint(a, pltpu.HBM) for a in bufs])


def _exchange_wait(state, after, scatter, name):
    send_sems, recv_sems, *bufs = state[:-1]
    nt = len(bufs) // 2

    def body(*refs):
        ins, zones = refs[:nt], refs[nt : 2 * nt]
        sends, recvs = _exchange_copies(ins, zones, refs[2 * nt], refs[2 * nt + 1], scatter)
        for cp in sends:
            cp.wait_send()
        for cp in recvs:
            cp.wait_recv()

    out = pl.pallas_call(
        body,
        name=name,
        out_shape=tuple(pltpu.HBM(a.shape, a.dtype) for a in bufs),
        in_specs=[HBM_SPEC] * (2 * nt) + [SEM_SPEC, SEM_SPEC, pl.BlockSpec(memory_space=pl.ANY)],
        out_specs=[HBM_SPEC] * (2 * nt),
        input_output_aliases={k: k for k in range(2 * nt)},
        compiler_params=pltpu.CompilerParams(has_side_effects=DATAFLOW),
    )(*bufs, send_sems, recv_sems, after)
    return list(out[:nt]), list(out[nt:])


def _sibling_swap(arrs, name):
    nt = len(arrs)

    def body(*refs):
        ins, outs = refs[:nt], refs[nt : 2 * nt]
        send_sems, recv_sems = refs[2 * nt :]
        x, y, c = _mesh_pos()
        cps = [
            pltpu.make_async_remote_copy(
                src_ref=ins[t],
                dst_ref=outs[t],
                send_sem=send_sems.at[t],
                recv_sem=recv_sems.at[t],
                device_id=(x, y, 1 - c),
                device_id_type=MESH,
            )
            for t in range(nt)
        ]
        for cp in cps:
            cp.start()
        for cp in cps:
            cp.wait_recv()
        for cp in cps:
            cp.wait_send()

    return pl.pallas_call(
        body,
        name=name,
        out_shape=tuple(jax.ShapeDtypeStruct(a.shape, a.dtype) for a in arrs),
        in_specs=[pl.BlockSpec(memory_space=pl.ANY)] * nt,
        out_specs=[pl.BlockSpec(memory_space=pl.ANY)] * nt,
        scratch_shapes=[pltpu.SemaphoreType.DMA((nt,)), pltpu.SemaphoreType.DMA((nt,))],
    )(*arrs)


def _cols_from_shards(g):
    _, k, n = g.shape
    return jnp.transpose(g, (1, 0, 2)).reshape(k, N_CHIP * n)


def _cols_to_shards(a):
    k, n4 = a.shape
    return jnp.transpose(a.reshape(k, N_CHIP, n4 // N_CHIP), (1, 0, 2))


def _pad_to(vec, mult):
    n = vec.shape[0]
    return jnp.pad(vec, (0, (-n) % mult))


def kernel(x, c, positions, w_ada, b_ada, g_pre_mix, g_post_mix, w_in, g_q, w_uq, g_kv, w_ukv, conv_w_mix, conv_b_mix, w_o, g_pre_ffn, g_post_ffn, w_up, conv_w_ffn, conv_b_ffn, w_down, loss_target, m_w_ada, m_b_ada, m_g_pre_mix, m_g_post_mix, m_w_in, m_g_q, m_w_uq, m_g_kv, m_w_ukv, m_conv_w_mix, m_conv_b_mix, m_w_o, m_g_pre_ffn, m_g_post_ffn, m_w_up, m_conv_w_ffn, m_conv_b_ffn, m_w_down, v_w_ada, v_b_ada, v_g_pre_mix, v_g_post_mix, v_w_in, v_g_q, v_w_uq, v_g_kv, v_w_ukv, v_conv_w_mix, v_conv_b_mix, v_w_o, v_g_pre_ffn, v_g_post_ffn, v_w_up, v_conv_w_ffn, v_conv_b_ffn, v_w_down):
    xi, yi, ci = _mesh_pos()
    chip = 2 * xi + yi
    dev = 4 * xi + 2 * yi + ci

    s, d = x.shape[1], x.shape[2]
    ql, kl = g_q.shape[1], g_kv.shape[1]
    cwid = conv_b_mix.shape[1]
    f2 = conv_b_ffn.shape[1]
    hh = (w_uq.shape[2] * N_CHIP) // (NOPE + ROPE)
    w_att = hh * LANES
    nc_ada = w_ada.shape[2]
    lat = ql + kl + ROPE
    tc_mix = _tile(cwid, 512, LANES)
    lb = -(-(ql + kl + HEAD_PAD) // tc_mix) * tc_mix
    np_cols = lb + 3 * cwid
    assert cwid == hh * VDIM and w_att % tc_mix == 0

    x0 = x.reshape(s, d)
    tgt = loss_target.reshape(s, d)

    anchors = []

    def _behind(val, state):
        val, tok = lax.optimization_barrier((val, state[-1]))
        anchors.append(tok[0, 0])
        return val

    cwm_n, cwf_n = CONV_K * cwid // N_CHIP, CONV_K * f2 // N_CHIP
    pack_a = _pad_to(jnp.concatenate([c.reshape(-1), conv_w_mix.reshape(-1), conv_w_ffn.reshape(-1)]), SUBLANES * LANES)
    rows_a = _all_gather8(pack_a.reshape(SUBLANES, -1), "ag8_inputs").reshape(N_DEV, -1)
    c_all = rows_a[:, :d]
    south = rows_a[0::2]
    cw_mix = jnp.concatenate([south[j, d : d + cwm_n].reshape(CONV_K, -1) for j in range(N_CHIP)], axis=1)
    cw_ffn = jnp.concatenate([south[j, d + cwm_n : d + cwm_n + cwf_n].reshape(CONV_K, -1) for j in range(N_CHIP)], axis=1)

    b_cols = lax.dynamic_slice(b_ada, (0, chip * nc_ada), (1, nc_ada))
    mod_part, c_act = _ada_fwd(c_all, w_ada[0], b_cols)
    mod_rows = _all_gather8(mod_part, "ag8_mod")
    mod = jnp.concatenate(
        [lax.dynamic_slice_in_dim(mod_rows, 2 * N_DEV * j + dev, 1, axis=0) for j in range(N_CHIP)], axis=1
    )

    shards = [a[0].astype(BF16) for a in (w_in, w_uq, w_ukv, w_o, w_up, w_down)]
    first, mod = lax.optimization_barrier((shards[:3], mod))
    ag_a = _exchange_start(first, False, "ag_a_start")
    mod = _behind(mod, ag_a)
    sh_m, sc_m, gt_m, sh_f, sc_f, gt_f = [mod[:, k * d : (k + 1) * d] for k in range(N_MOD)]

    inv_freq = 1.0 / (ROPE_THETA ** (jnp.arange(0, ROPE, 2, dtype=F32) / ROPE))
    invf = jnp.concatenate([inv_freq, inv_freq, jnp.zeros((LANES - ROPE,), F32)]).reshape(1, LANES)
    tabs = _rope_tables(positions.astype(F32).reshape(s, 1), invf)
    h1 = _pre_fwd(x0, g_pre_mix, sc_m, sh_m)

    def with_own(landed, own):
        return [lax.dynamic_update_slice_in_dim(g, a[None], chip, axis=0) for g, a in zip(landed, own)]

    own_w, landed_w = _exchange_wait(ag_a, h1, False, "ag_a_wait")
    rest, landed_w = lax.optimization_barrier((shards[3:], landed_w))
    ag_b = _exchange_start(rest, False, "ag_b_start")
    h1 = _behind(h1, ag_b)
    g_in, g_uq, g_ukv = with_own(landed_w, own_w)
    full_in = _cols_from_shards(g_in)
    w_in_p = jnp.concatenate([full_in[:, :lat], jnp.zeros((d, lb - lat), BF16), full_in[:, lat:]], axis=1)
    full_uq = _cols_from_shards(g_uq).reshape(ql, hh, NOPE + ROPE)
    w_uq_p = jnp.concatenate(
        [
            full_uq[:, :, :NOPE].reshape(ql, w_att),
            jnp.pad(full_uq[:, :, NOPE:], ((0, 0), (0, 0), (0, HEAD_PAD - ROPE))).reshape(ql, w_att),
        ],
        axis=1,
    )
    full_ukv = _cols_from_shards(g_ukv).reshape(kl, hh, NOPE + VDIM)
    w_ukv_p = jnp.concatenate([full_ukv[:, :, :NOPE].reshape(kl, w_att), full_ukv[:, :, NOPE:].reshape(kl, w_att)], axis=1)

    proj = _matmul(h1, w_in_p, out_dtype=F32, tm=1024, tn=768, tk=2048, name="mm_proj")
    qn, kvn, kr = _latent_fwd(proj, g_q, g_kv, tabs, lb)
    q_f = _matmul(qn, w_uq_p, out_dtype=F32, tm=1024, tn=1024, tk=2048, name="mm_q")
    kv_p = _matmul(kvn, w_ukv_p, out_dtype=BF16, tm=1024, tn=1024, tk=2048, name="mm_kv")
    q_c, k_c = _head_cat(q_f, kv_p, kr, tabs, hh)
    cat, lse2 = _attn_fwd(q_c, k_c, kv_p, hh, w_att + cwid)
    cat = _mixer_fwd(cat, proj, cw_mix, conv_b_mix, lb, w_att)
    own_w, landed_w = _exchange_wait(ag_b, cat, False, "ag_b_wait")
    g_o, g_up, g_down = with_own(landed_w, own_w)
    w_o_f = g_o.reshape(-1, d)
    cw_ffn_p, cb_ffn_p = _pair_cols(cw_ffn), _pair_cols(conv_b_ffn)
    tcp, pair_perm = _pair_tile(f2 // 2), _pair_perm(f2 // 2)
    w_down_f = g_down.reshape(-1, d)
    mix = _matmul(cat, w_o_f, out_dtype=F32, tm=1024, tn=1024, tk=2048, name="mm_mix")

    x1, h2 = _mid_fwd(x0, mix, g_post_mix, gt_m, g_pre_ffn, sc_f, sh_f)
    up = _matmul(h2, g_up, out_dtype=F32, tm=1024, tn=tcp, tk=2048, name="mm_up", b_n_perm=pair_perm, b_col_shards=True)
    act = _ffn_act_fwd(up, cw_ffn_p, cb_ffn_p)
    y = _matmul(act, w_down_f, out_dtype=F32, tm=512, tn=1024, tk=5632, name="mm_down")
    dx2, dy, s_fin = _final(x1, y, tgt, g_post_ffn, gt_f)

    dw_down = _matmul(act, dy, ta=True, out_dtype=BF16, tm=1408, tn=1024, tk=2048, name="mm_dw_down")
    dact = _matmul(dy, w_down_f, tb=True, out_dtype=F32, tm=1024, tn=1408, tk=2048, name="mm_dact")
    dup, s_ffn_p = _ffn_act_bwd(dact, up, cw_ffn_p, cb_ffn_p)
    s_ffn = _unpair_cols(s_ffn_p)
    dw_up = _matmul(
        h2, dup, ta=True, out_dtype=BF16, tm=1024, tn=tcp, tk=2048, name="mm_dw_up", out_n_perm=pair_perm, out_col_shards=True
    )
    dh2 = _matmul(
        dup, g_up, tb=True, out_dtype=F32, tm=1024, tn=1024, tk=tcp, name="mm_dh2", b_k_perm=pair_perm, b_col_shards=True
    )
    dx1, dmix, s_mid = _mid_bwd(dh2, dx2, x1, mix, g_pre_ffn, sc_f, g_post_mix, gt_m)

    dw_o = _matmul(cat, dmix, ta=True, out_dtype=BF16, tm=1024, tn=1024, tk=2048, name="mm_dw_o")
    send_b = [dw_o.reshape(N_CHIP, -1, d), dw_up, dw_down.reshape(N_CHIP, -1, d)]
    rs_b = _exchange_start(send_b, True, "rs_b_start")
    dmix = _behind(dmix, rs_b)
    dcat = _matmul(dmix, w_o_f, tb=True, out_dtype=F32, tm=1024, tn=1024, tk=2048, name="mm_dcat")
    dp_b, dp_c, dp_i, s_mix = _mixer_bwd(dcat, proj, cw_mix, conv_b_mix, lb, w_att)
    dob, stats = _attn_bwd_prep(cat, dcat, lse2, hh)
    dq_raw, dkv_k, dkv_v, dkr_h = _attn_bwd(q_c, k_c, kv_p, dob, stats, hh)
    dkv_p = jnp.concatenate([dkv_k, dkv_v], axis=1)
    dq_p = _dq_unrope(dq_raw, tabs, hh)
    dw_uq_p = _matmul(qn, dq_p, ta=True, out_dtype=BF16, tm=1024, tn=1024, tk=1024, name="mm_dw_uq")
    dqn = _matmul(dq_p, w_uq_p, tb=True, out_dtype=F32, tm=1024, tn=1024, tk=2048, name="mm_dqn")
    dw_ukv_p = _matmul(kvn, dkv_p, ta=True, out_dtype=BF16, tm=1024, tn=1024, tk=1024, name="mm_dw_ukv")
    dkvn = _matmul(dkv_p, w_ukv_p, tb=True, out_dtype=F32, tm=1024, tn=1024, tk=2048, name="mm_dkvn")
    dp_lat, s_lat = _latent_bwd(proj, dqn, dkvn, dkr_h, g_q, g_kv, tabs, lb)
    dproj = jnp.concatenate([dp_lat, dp_b, dp_c, dp_i], axis=1)
    dw_in_p = _matmul(h1, dproj, ta=True, out_dtype=BF16, tm=1024, tn=1536, tk=2048, name="mm_dw_in")

    dw_in_f = jnp.concatenate([dw_in_p[:, :lat], dw_in_p[:, lb:]], axis=1)
    uq3 = dw_uq_p.reshape(ql, 2, hh, LANES)
    dw_uq_f = jnp.concatenate([uq3[:, 0], uq3[:, 1, :, :ROPE]], axis=2).reshape(ql, hh * (NOPE + ROPE))
    ukv3 = dw_ukv_p.reshape(kl, 2, hh, LANES)
    dw_ukv_f = jnp.concatenate([ukv3[:, 0], ukv3[:, 1]], axis=2).reshape(kl, hh * (NOPE + VDIM))
    send_a = [_cols_to_shards(dw_in_f), _cols_to_shards(dw_uq_f), _cols_to_shards(dw_ukv_f)]
    rs_a = _exchange_start(send_a, True, "rs_a_start")
    dproj = _behind(dproj, rs_a)

    dh1 = _matmul(dproj, w_in_p, tb=True, out_dtype=F32, tm=512, tn=1024, tk=4608, name="mm_dh1")
    grad_x, s_first = _first_bwd(dh1, dx1, x0, g_pre_mix, sc_m)

    names = ["w_in", "w_uq", "w_ukv", "w_o", "w_up", "w_down"]
    sent_b, landed_b = _exchange_wait(rs_b, s_first, True, "rs_b_wait")
    sent_a, landed_a = _exchange_wait(rs_a, landed_b[0], True, "rs_a_wait")
    landed_a, s_first = lax.optimization_barrier((landed_a, s_first))
    part = [_sum_chips(l, a, "sum_chips_" + n) for l, a, n in zip(landed_a + landed_b, sent_a + sent_b, names)]
    other = _sibling_swap(part, "sibling_swap")

    dmod = jnp.concatenate([s_first[0:1], s_first[1:2], s_mid[3:4], s_mid[0:1], s_mid[1:2], s_fin[0:1]], axis=1)
    small = [
        dmod,
        s_first[2:3],
        s_mid[4:5],
        s_lat[0:1, :ql],
        s_lat[0:1, ql : ql + kl],
        s_mix[3:4],
        s_mid[2:3],
        s_fin[1:2],
        s_ffn[3:4],
        s_mix[0:3].reshape(1, -1),
        s_ffn[0:3].reshape(1, -1),
        s_fin[3:4, :LANES],
    ]
    sizes = [a.shape[1] for a in small]
    offs = [0]
    for n in sizes:
        offs.append(offs[-1] + n)
    pack_g = _pad_to(jnp.concatenate(small, axis=1).reshape(-1), SUBLANES * LANES * SUBLANES).reshape(SUBLANES, -1)
    gathered = _all_gather8(pack_g, "ag8_small_grads")
    tot = _sum_devices(gathered).reshape(-1)
    part_of = lambda k: tot[offs[k] : offs[k + 1]]
    dmod_all = gathered.reshape(N_DEV, -1)[:, : N_MOD * d]
    loss = part_of(11)[0]

    g_b_ada = part_of(0).reshape(1, -1)
    g_vecs = [part_of(k).reshape(1, -1) for k in range(1, 9)]
    g_cw_mix = lax.dynamic_slice(part_of(9).reshape(CONV_K, cwid), (0, chip * (cwid // N_CHIP)), (CONV_K, cwid // N_CHIP))
    g_cw_ffn = lax.dynamic_slice(part_of(10).reshape(CONV_K, f2), (0, chip * (f2 // N_CHIP)), (CONV_K, f2 // N_CHIP))

    dm_cols = lax.dynamic_slice(dmod_all, (0, chip * nc_ada), (N_DEV, nc_ada))
    g_w_ada = _ada_grad(
        jnp.pad(c_act.T, ((0, 0), (0, LANES - N_DEV))), jnp.pad(dm_cols, ((0, LANES - N_DEV), (0, 0)))
    )

    big_w = [w_in, w_uq, w_ukv, w_o, w_up, w_down]
    big_m = [m_w_in, m_w_uq, m_w_ukv, m_w_o, m_w_up, m_w_down]
    big_v = [v_w_in, v_w_uq, v_w_ukv, v_w_o, v_w_up, v_w_down]
    big = {}
    for n, w_, m_, v_, p_, o_ in zip(names, big_w, big_m, big_v, part, other):
        big[n] = [a[None] for a in _adamw(w_[0], m_[0], v_[0], [p_, o_], "adamw_" + n)]
    big["w_ada"] = [a[None] for a in _adamw(w_ada[0], m_w_ada[0], v_w_ada[0], [g_w_ada], "adamw_w_ada")]

    sm_names = ["b_ada", "g_pre_mix", "g_post_mix", "g_q", "g_kv", "conv_b_mix", "g_pre_ffn", "g_post_ffn", "conv_b_ffn",
                "conv_w_mix", "conv_w_ffn"]
    sm_w = [b_ada, g_pre_mix, g_post_mix, g_q, g_kv, conv_b_mix, g_pre_ffn, g_post_ffn, conv_b_ffn, conv_w_mix, conv_w_ffn]
    sm_m = [m_b_ada, m_g_pre_mix, m_g_post_mix, m_g_q, m_g_kv, m_conv_b_mix, m_g_pre_ffn, m_g_post_ffn, m_conv_b_ffn,
            m_conv_w_mix, m_conv_w_ffn]
    sm_v = [v_b_ada, v_g_pre_mix, v_g_post_mix, v_g_q, v_g_kv, v_conv_b_mix, v_g_pre_ffn, v_g_post_ffn, v_conv_b_ffn,
            v_conv_w_mix, v_conv_w_ffn]
    sm_g = [g_b_ada] + g_vecs + [g_cw_mix, g_cw_ffn]
    flat = lambda arrs: jnp.concatenate([a.reshape(1, -1) for a in arrs], axis=1)
    sm_out = _adamw(flat(sm_w), flat(sm_m), flat(sm_v), [flat(sm_g)], "adamw_small")
    sm = {}
    off = 0
    for n, w_ in zip(sm_names, sm_w):
        sm[n] = [o[:, off : off + w_.size].reshape(w_.shape) for o in sm_out]
        off += w_.size

    order = ["w_ada", "b_ada", "g_pre_mix", "g_post_mix", "w_in", "g_q", "w_uq", "g_kv", "w_ukv", "conv_w_mix", "conv_b_mix",
             "w_o", "g_pre_ffn", "g_post_ffn", "w_up", "conv_w_ffn", "conv_b_ffn", "w_down"]
    res = {**big, **sm}
    outs = [loss + sum(anchors), grad_x.reshape(x.shape)]
    for k in range(4):
        outs += [res[n][k] for n in order]
    return tuple(outs)
```

```python
import math

import jax
import jax.numpy as jnp
from jax import lax
from jax.experimental import pallas as pl
from jax.experimental.pallas import tpu as pltpu

F32 = jnp.float32
BF16 = jnp.bfloat16
MESH = pl.DeviceIdType.MESH

N_DEV = 8
N_CHIP = 4
LANES = 128
SUBLANES = 8
VMEM_LIMIT = 56 * 2**20

NOPE = 128
ROPE = 64
VDIM = 128
HEAD_PAD = 128
ROPE_THETA = 10000.0
RMS_EPS = 1e-6
N_MOD = 6
CONV_K = 3
ATT_FWD_BLOCK, ATT_FWD_SUB = 2048, 256
ATT_BWD_BLOCK, ATT_BWD_SUB = 1024, 256
NEG = -1e30

ADAM_LR = 0.001
ADAM_B1 = 0.9
ADAM_B2 = 0.999
ADAM_EPS = 1e-08
ADAM_WD = 0.01
ADAM_STEP = 10


def _tile(n, pref, align):
    if n <= pref:
        return n
    t = (pref // align) * align
    while t >= align:
        if n % t == 0:
            return t
        t -= align
    return n


def _cp(*sem):
    return pltpu.CompilerParams(dimension_semantics=sem, vmem_limit_bytes=VMEM_LIMIT)


def _rsq(x):
    return lax.rsqrt(jnp.mean(x * x, axis=-1, keepdims=True) + RMS_EPS)


def _norm_bwd(dn, n, r):
    return r * (dn - n * jnp.mean(dn * n, axis=-1, keepdims=True))


def _colsum(a):
    return jnp.sum(a, axis=0, keepdims=True)


def _matmul(a, b, *, ta=False, tb=False, out_dtype, tm, tn, tk, name, b_n_perm=None, b_k_perm=None, out_n_perm=None,
            b_col_shards=False, out_col_shards=False):
    if b_col_shards:
        b_rows, b_cols = b.shape[1], N_CHIP * b.shape[2]
    else:
        b_rows, b_cols = b.shape
    (k_a, m) = a.shape if ta else a.shape[::-1]
    (n, k_b) = (b_rows, b_cols) if tb else (b_cols, b_rows)
    assert k_a == k_b, (a.shape, b.shape, ta, tb)
    tm, tn, tk = _tile(m, tm, LANES), _tile(n, tn, LANES), _tile(k_a, tk, LANES)
    nk = k_a // tk
    same = lambda t: t
    bn, bk, on = b_n_perm or same, b_k_perm or same, out_n_perm or same
    a_spec = pl.BlockSpec((tk, tm), lambda i, j, k: (k, i)) if ta else pl.BlockSpec((tm, tk), lambda i, j, k: (i, k))
    if b_col_shards and tb:
        per = (b_cols // N_CHIP) // tk
        b_spec = pl.BlockSpec((None, tn, tk), lambda i, j, k: (bk(k) // per, bn(j), bk(k) % per))
    elif b_col_shards:
        per = (b_cols // N_CHIP) // tn
        b_spec = pl.BlockSpec((None, tk, tn), lambda i, j, k: (bn(j) // per, bk(k), bn(j) % per))
    elif tb:
        b_spec = pl.BlockSpec((tn, tk), lambda i, j, k: (bn(j), bk(k)))
    else:
        b_spec = pl.BlockSpec((tk, tn), lambda i, j, k: (bk(k), bn(j)))
    if out_col_shards:
        per_o = (n // N_CHIP) // tn
        out_shape = jax.ShapeDtypeStruct((N_CHIP, m, n // N_CHIP), out_dtype)
        out_spec = pl.BlockSpec((None, tm, tn), lambda i, j, k: (on(j) // per_o, i, on(j) % per_o))
    else:
        out_shape = jax.ShapeDtypeStruct((m, n), out_dtype)
        out_spec = pl.BlockSpec((tm, tn), lambda i, j, k: (i, on(j)))
    dims = (((0 if ta else 1,), (1 if tb else 0,)), ((), ()))

    def body(a_ref, b_ref, o_ref, *acc):
        p = lax.dot_general(a_ref[...].astype(BF16), b_ref[...].astype(BF16), dims, preferred_element_type=F32)
        if nk == 1:
            o_ref[...] = p.astype(o_ref.dtype)
        else:
            k = pl.program_id(2)

            @pl.when(k == 0)
            def _():
                acc[0][...] = p

            @pl.when(k > 0)
            def _():
                acc[0][...] += p

            @pl.when(k == nk - 1)
            def _():
                o_ref[...] = acc[0][...].astype(o_ref.dtype)

    return pl.pallas_call(
        body,
        name=name,
        out_shape=out_shape,
        grid=(m // tm, n // tn, nk),
        in_specs=[a_spec, b_spec],
        out_specs=out_spec,
        scratch_shapes=[] if nk == 1 else [pltpu.VMEM((tm, tn), F32)],
        compiler_params=_cp("parallel", "parallel", "arbitrary"),
    )(a, b)


def _rope_tables(pos_col, invf):
    s = pos_col.shape[0]
    ts = _tile(s, 1024, SUBLANES)
    half = ROPE // 2

    def body(p_ref, f_ref, c_ref, sa_ref, sb_ref):
        ang = p_ref[...] * f_ref[...]
        lane = lax.broadcasted_iota(jnp.int32, ang.shape, 1)
        cs, sn = jnp.cos(ang), jnp.sin(ang)
        c_ref[...] = jnp.where(lane < ROPE, cs, 0.0)
        sa_ref[...] = jnp.where((lane >= half) & (lane < ROPE), sn, 0.0)
        sb_ref[...] = jnp.where(lane < half, -sn, 0.0)

    tab = jax.ShapeDtypeStruct((s, LANES), F32)
    return pl.pallas_call(
        body,
        name="rope_tables",
        out_shape=(tab, tab, tab),
        grid=(s // ts,),
        in_specs=[pl.BlockSpec((ts, 1), lambda i: (i, 0)), pl.BlockSpec((1, LANES), lambda i: (0, 0))],
        out_specs=[pl.BlockSpec((ts, LANES), lambda i: (i, 0))] * 3,
        compiler_params=_cp("parallel"),
    )(pos_col, invf)


def _widen(t, w):
    return t if w == LANES else jnp.tile(t, (1, w // LANES))


def _rope(x, c, sa, sb):
    w = x.shape[1]
    c, sa, sb = _widen(c, w), _widen(sa, w), _widen(sb, w)
    return x * c + pltpu.roll(x, ROPE // 2, 1) * sa + pltpu.roll(x, w - ROPE // 2, 1) * sb


def _rope_t(d, c, sa, sb):
    w = d.shape[1]
    c, sa, sb = _widen(c, w), _widen(sa, w), _widen(sb, w)
    return d * c + pltpu.roll(d * sa, w - ROPE // 2, 1) + pltpu.roll(d * sb, ROPE // 2, 1)


def _ada_fwd(c_all, w, b):
    d, nc = w.shape
    tn = _tile(nc, 512, LANES)

    def body(c_ref, w_ref, b_ref, o_ref, ca_ref):
        cv = c_ref[...]
        ca = cv * jax.nn.sigmoid(cv)
        ca_ref[...] = ca
        o_ref[...] = jnp.dot(ca.astype(BF16), w_ref[...].astype(BF16), preferred_element_type=F32) + b_ref[...]

    return pl.pallas_call(
        body,
        name="ada_fwd",
        out_shape=(jax.ShapeDtypeStruct((N_DEV, nc), F32), jax.ShapeDtypeStruct((N_DEV, d), F32)),
        grid=(nc // tn,),
        in_specs=[
            pl.BlockSpec((N_DEV, d), lambda j: (0, 0)),
            pl.BlockSpec((d, tn), lambda j: (0, j)),
            pl.BlockSpec((1, tn), lambda j: (0, j)),
        ],
        out_specs=[pl.BlockSpec((N_DEV, tn), lambda j: (0, j)), pl.BlockSpec((N_DEV, d), lambda j: (0, 0))],
        compiler_params=_cp("arbitrary"),
    )(c_all, w, b)


def _rows(ts, d):
    return pl.BlockSpec((ts, d), lambda i: (i, 0))


def _vec(d):
    return pl.BlockSpec((1, d), lambda i: (0, 0))


def _sums(d):
    return pl.BlockSpec((SUBLANES, d), lambda i: (0, 0))


def _acc_rows(ref, i, rows):
    @pl.when(i == 0)
    def _():
        ref[...] = jnp.zeros(ref.shape, ref.dtype)

    for k, r in enumerate(rows):
        ref[k : k + 1, :] += r


def _pre_fwd(x, g, sc, sh):
    s, d = x.shape
    ts = _tile(s, 512, SUBLANES)

    def body(x_ref, g_ref, sc_ref, sh_ref, h_ref):
        xv = x_ref[...]
        h_ref[...] = (((xv * _rsq(xv)) * g_ref[...]) * (1.0 + sc_ref[...]) + sh_ref[...]).astype(BF16)

    return pl.pallas_call(
        body,
        name="pre_mix_fwd",
        out_shape=jax.ShapeDtypeStruct((s, d), BF16),
        grid=(s // ts,),
        in_specs=[_rows(ts, d), _vec(d), _vec(d), _vec(d)],
        out_specs=_rows(ts, d),
        compiler_params=_cp("parallel"),
    )(x, g, sc, sh)


def _mid_fwd(x0, mix, g_post, gt, g_pre, sc, sh):
    s, d = x0.shape
    ts = _tile(s, 256, SUBLANES)

    def body(x_ref, m_ref, gp_ref, gt_ref, g_ref, sc_ref, sh_ref, x1_ref, h_ref):
        mv = m_ref[...]
        x1 = x_ref[...] + gt_ref[...] * ((mv * _rsq(mv)) * gp_ref[...])
        x1_ref[...] = x1
        h_ref[...] = (((x1 * _rsq(x1)) * g_ref[...]) * (1.0 + sc_ref[...]) + sh_ref[...]).astype(BF16)

    return pl.pallas_call(
        body,
        name="mid_fwd",
        out_shape=(jax.ShapeDtypeStruct((s, d), F32), jax.ShapeDtypeStruct((s, d), BF16)),
        grid=(s // ts,),
        in_specs=[_rows(ts, d), _rows(ts, d)] + [_vec(d)] * 5,
        out_specs=[_rows(ts, d), _rows(ts, d)],
        compiler_params=_cp("parallel"),
    )(x0, mix, g_post, gt, g_pre, sc, sh)


def _final(x1, y, tgt, g_post, gt):
    s, d = x1.shape
    ts = _tile(s, 256, SUBLANES)
    ni = s // ts

    def body(x_ref, y_ref, t_ref, gp_ref, gt_ref, dx_ref, dy_ref, s_ref):
        i = pl.program_id(0)
        yv, gp, gt_v = y_ref[...], gp_ref[...], gt_ref[...]
        r = _rsq(yv)
        n = yv * r
        err = (x_ref[...] + gt_v * (n * gp)) - t_ref[...]
        dx = err * (1.0 / d)
        dx_ref[...] = dx
        dy_ref[...] = _norm_bwd(dx * (gt_v * gp), n, r).astype(BF16)
        _acc_rows(s_ref, i, [_colsum(dx * (n * gp)), _colsum(dx * gt_v * n), _colsum(err * err)])

        @pl.when(i == ni - 1)
        def _():
            tot = jnp.sum(s_ref[2:3, :], axis=1, keepdims=True) * (0.5 / d)
            s_ref[3:4, :] = jnp.broadcast_to(tot, (1, d))

    return pl.pallas_call(
        body,
        name="final_fwd_bwd",
        out_shape=(
            jax.ShapeDtypeStruct((s, d), F32),
            jax.ShapeDtypeStruct((s, d), BF16),
            jax.ShapeDtypeStruct((SUBLANES, d), F32),
        ),
        grid=(ni,),
        in_specs=[_rows(ts, d)] * 3 + [_vec(d)] * 2,
        out_specs=[_rows(ts, d), _rows(ts, d), _sums(d)],
        compiler_params=_cp("arbitrary"),
    )(x1, y, tgt, g_post, gt)


def _mid_bwd(dh2, dx2, x1, mix, g_pre, sc, g_post, gt):
    s, d = x1.shape
    ts = _tile(s, 256, SUBLANES)

    def body(dh_ref, dx2_ref, x_ref, m_ref, g_ref, sc_ref, gp_ref, gt_ref, dx1_ref, dm_ref, s_ref):
        i = pl.program_id(0)
        dh, xv, mv = dh_ref[...], x_ref[...], m_ref[...]
        g, sc_v, gp, gt_v = g_ref[...], sc_ref[...], gp_ref[...], gt_ref[...]
        r1 = _rsq(xv)
        n1 = xv * r1
        dx1 = dx2_ref[...] + _norm_bwd(dh * (g * (1.0 + sc_v)), n1, r1)
        dx1_ref[...] = dx1
        rm = _rsq(mv)
        nm = mv * rm
        dm_ref[...] = _norm_bwd(dx1 * (gt_v * gp), nm, rm).astype(BF16)
        _acc_rows(
            s_ref,
            i,
            [
                _colsum(dh),
                _colsum(dh * (n1 * g)),
                _colsum(dh * (1.0 + sc_v) * n1),
                _colsum(dx1 * (nm * gp)),
                _colsum(dx1 * gt_v * nm),
            ],
        )

    return pl.pallas_call(
        body,
        name="mid_bwd",
        out_shape=(
            jax.ShapeDtypeStruct((s, d), F32),
            jax.ShapeDtypeStruct((s, d), BF16),
            jax.ShapeDtypeStruct((SUBLANES, d), F32),
        ),
        grid=(s // ts,),
        in_specs=[_rows(ts, d)] * 4 + [_vec(d)] * 4,
        out_specs=[_rows(ts, d), _rows(ts, d), _sums(d)],
        compiler_params=_cp("arbitrary"),
    )(dh2, dx2, x1, mix, g_pre, sc, g_post, gt)


def _first_bwd(dh1, dx1, x0, g, sc):
    s, d = x0.shape
    ts = _tile(s, 256, SUBLANES)

    def body(dh_ref, dx1_ref, x_ref, g_ref, sc_ref, dx_ref, s_ref):
        i = pl.program_id(0)
        dh, xv, gv, sc_v = dh_ref[...], x_ref[...], g_ref[...], sc_ref[...]
        r = _rsq(xv)
        n = xv * r
        dx_ref[...] = dx1_ref[...] + _norm_bwd(dh * (gv * (1.0 + sc_v)), n, r)
        _acc_rows(s_ref, i, [_colsum(dh), _colsum(dh * (n * gv)), _colsum(dh * (1.0 + sc_v) * n)])

    return pl.pallas_call(
        body,
        name="first_bwd",
        out_shape=(jax.ShapeDtypeStruct((s, d), F32), jax.ShapeDtypeStruct((SUBLANES, d), F32)),
        grid=(s // ts,),
        in_specs=[_rows(ts, d)] * 3 + [_vec(d)] * 2,
        out_specs=[_rows(ts, d), _sums(d)],
        compiler_params=_cp("arbitrary"),
    )(dh1, dx1, x0, g, sc)


def _latent_fwd(proj, g_q, g_kv, tabs, lb):
    s = proj.shape[0]
    ql, kl = g_q.shape[1], g_kv.shape[1]
    ts = _tile(s, 512, SUBLANES)

    def body(p_ref, gq_ref, gk_ref, c_ref, sa_ref, sb_ref, q_ref, kv_ref, kr_ref):
        pv = p_ref[...]
        q, kv, kr = pv[:, :ql], pv[:, ql : ql + kl], pv[:, ql + kl : ql + kl + HEAD_PAD]
        q_ref[...] = ((q * _rsq(q)) * gq_ref[...]).astype(BF16)
        kv_ref[...] = ((kv * _rsq(kv)) * gk_ref[...]).astype(BF16)
        kr_ref[...] = _rope(kr, c_ref[...], sa_ref[...], sb_ref[...]).astype(BF16)

    return pl.pallas_call(
        body,
        name="latent_fwd",
        out_shape=(
            jax.ShapeDtypeStruct((s, ql), BF16),
            jax.ShapeDtypeStruct((s, kl), BF16),
            jax.ShapeDtypeStruct((s, HEAD_PAD), BF16),
        ),
        grid=(s // ts,),
        in_specs=[_rows(ts, lb), _vec(ql), _vec(kl)] + [_rows(ts, LANES)] * 3,
        out_specs=[_rows(ts, ql), _rows(ts, kl), _rows(ts, HEAD_PAD)],
        compiler_params=_cp("parallel"),
    )(proj, g_q, g_kv, *tabs)


def _latent_bwd(proj, dqn, dkvn, dkr_h, g_q, g_kv, tabs, lb):
    s = proj.shape[0]
    ql, kl = g_q.shape[1], g_kv.shape[1]
    hw = dkr_h.shape[1]
    ts = _tile(s, 256, SUBLANES)
    pad = lb - ql - kl - HEAD_PAD

    def body(p_ref, dq_ref, dkv_ref, dkr_ref, gq_ref, gk_ref, c_ref, sa_ref, sb_ref, o_ref, s_ref):
        i = pl.program_id(0)
        pv = p_ref[...]
        q, kv = pv[:, :ql], pv[:, ql : ql + kl]
        dqn_v, dkvn_v = dq_ref[...], dkv_ref[...]
        rq = _rsq(q)
        nq = q * rq
        rk = _rsq(kv)
        nk = kv * rk
        dkr = dkr_ref[:, :HEAD_PAD]
        for h in range(1, hw // HEAD_PAD):
            dkr = dkr + dkr_ref[:, h * HEAD_PAD : (h + 1) * HEAD_PAD]
        parts = [
            _norm_bwd(dqn_v * gq_ref[...], nq, rq).astype(BF16),
            _norm_bwd(dkvn_v * gk_ref[...], nk, rk).astype(BF16),
            _rope_t(dkr, c_ref[...], sa_ref[...], sb_ref[...]).astype(BF16),
        ]
        if pad:
            parts.append(jnp.zeros((ts, pad), BF16))
        o_ref[...] = jnp.concatenate(parts, axis=1)
        row = [_colsum(dqn_v * nq), _colsum(dkvn_v * nk), jnp.zeros((1, lb - ql - kl), F32)]
        _acc_rows(s_ref, i, [jnp.concatenate(row, axis=1)])

    return pl.pallas_call(
        body,
        name="latent_bwd",
        out_shape=(jax.ShapeDtypeStruct((s, lb), BF16), jax.ShapeDtypeStruct((SUBLANES, lb), F32)),
        grid=(s // ts,),
        in_specs=[_rows(ts, lb), _rows(ts, ql), _rows(ts, kl), _rows(ts, hw)]
        + [_vec(ql), _vec(kl)]
        + [_rows(ts, LANES)] * 3,
        out_specs=[_rows(ts, lb), _sums(lb)],
        compiler_params=_cp("arbitrary"),
    )(proj, dqn, dkvn, dkr_h, g_q, g_kv, *tabs)


def _conv3(ext, w, b):
    return (pltpu.roll(ext, 2, 0) * w[0:1] + pltpu.roll(ext, 1, 0) * w[1:2]) + ext * w[2:3] + b


def _conv3_t(du, w):
    n = du.shape[0]
    return du * w[2:3] + pltpu.roll(du, n - 1, 0) * w[1:2] + pltpu.roll(du, n - 2, 0) * w[0:1]


def _halo_maps(ts, s):
    r8, last = ts // SUBLANES, s // SUBLANES - 1
    prev = lambda i: jnp.maximum(i * r8 - 1, 0)
    nxt = lambda i: jnp.minimum((i + 1) * r8, last)
    return prev, nxt


def _mixer_fwd(cat, proj, cw, cb, lb, col0):
    s = proj.shape[0]
    cwid = cw.shape[1]
    ts = _tile(s, 512, SUBLANES)
    tc = _tile(cwid, 512, LANES)
    assert lb % tc == 0 and col0 % tc == 0
    nj, ob, oc = cwid // tc, lb // tc, col0 // tc
    prev, _ = _halo_maps(ts, s)

    def body(_, gb_ref, gc_ref, ci_ref, pgc_ref, pci_ref, w_ref, b_ref, o_ref):
        keep = jnp.where(pl.program_id(1) > 0, 1.0, 0.0)
        ext = jnp.concatenate([pgc_ref[...] * pci_ref[...] * keep, gc_ref[...] * ci_ref[...]], axis=0)
        o_ref[...] = (gb_ref[...] * _conv3(ext, w_ref[...], b_ref[...])[SUBLANES:]).astype(BF16)

    def col(k):
        return pl.BlockSpec((ts, tc), lambda j, i: (i, ob + k * nj + j))

    def halo(k):
        return pl.BlockSpec((SUBLANES, tc), lambda j, i: (prev(i), ob + k * nj + j))

    return pl.pallas_call(
        body,
        name="mixer_fwd",
        out_shape=jax.ShapeDtypeStruct(cat.shape, BF16),
        grid=(nj, s // ts),
        in_specs=[pl.BlockSpec(memory_space=pl.ANY), col(0), col(1), col(2), halo(1), halo(2)]
        + [pl.BlockSpec((CONV_K, tc), lambda j, i: (0, j)), pl.BlockSpec((1, tc), lambda j, i: (0, j))],
        out_specs=pl.BlockSpec((ts, tc), lambda j, i: (i, oc + j)),
        input_output_aliases={0: 0},
        compiler_params=_cp("parallel", "arbitrary"),
    )(cat, proj, proj, proj, proj, proj, cw, cb)


def _mixer_bwd(dcat, proj, cw, cb, lb, col0):
    s = proj.shape[0]
    cwid = cw.shape[1]
    ts = _tile(s, 256, SUBLANES)
    tc = _tile(cwid, 512, LANES)
    nj, ob, oc = cwid // tc, lb // tc, col0 // tc
    ni = s // ts
    prev, nxt = _halo_maps(ts, s)

    def body(d_ref, dn_ref, gb_ref, gbn_ref, gc_ref, gcp_ref, gcn_ref, ci_ref, cip_ref, cin_ref, w_ref, b_ref,
             dgb_ref, dgc_ref, dci_ref, s_ref):
        i = pl.program_id(1)
        keep_p = jnp.where(i > 0, 1.0, 0.0)
        keep_n = jnp.where(i < ni - 1, 1.0, 0.0)
        w = w_ref[...]
        gc = jnp.concatenate([gcp_ref[...], gc_ref[...], gcn_ref[...]], axis=0)
        ci = jnp.concatenate([cip_ref[...] * keep_p, ci_ref[...], cin_ref[...]], axis=0)
        u = gc * ci
        cv = _conv3(u, w, b_ref[...])[SUBLANES:]
        dco = jnp.concatenate([d_ref[...], dn_ref[...] * keep_n], axis=0)
        gb = jnp.concatenate([gb_ref[...], gbn_ref[...]], axis=0)
        dgb_ref[...] = (dco * cv)[:ts].astype(BF16)
        dcv = dco * gb
        du = _conv3_t(dcv, w)[:ts]
        dgc_ref[...] = (du * ci_ref[...]).astype(BF16)
        dci_ref[...] = (du * gc_ref[...]).astype(BF16)
        dt = dcv[:ts]
        u1, u2 = pltpu.roll(u, 1, 0), pltpu.roll(u, 2, 0)
        lo, hi = SUBLANES, SUBLANES + ts
        _acc_rows(s_ref, i, [_colsum(dt * u2[lo:hi]), _colsum(dt * u1[lo:hi]), _colsum(dt * u[lo:hi]), _colsum(dt)])

    def col(k):
        return pl.BlockSpec((ts, tc), lambda j, i: (i, ob + k * nj + j))

    def halo(k, which):
        return pl.BlockSpec((SUBLANES, tc), lambda j, i: (which(i), ob + k * nj + j))

    out_col = [pl.BlockSpec((ts, tc), lambda j, i: (i, j))] * 3
    grad = jax.ShapeDtypeStruct((s, cwid), BF16)
    return pl.pallas_call(
        body,
        name="mixer_bwd",
        out_shape=(grad, grad, grad, jax.ShapeDtypeStruct((SUBLANES, cwid), F32)),
        grid=(nj, ni),
        in_specs=[
            pl.BlockSpec((ts, tc), lambda j, i: (i, oc + j)),
            pl.BlockSpec((SUBLANES, tc), lambda j, i: (nxt(i), oc + j)),
            col(0), halo(0, nxt),
            col(1), halo(1, prev), halo(1, nxt),
            col(2), halo(2, prev), halo(2, nxt),
            pl.BlockSpec((CONV_K, tc), lambda j, i: (0, j)),
            pl.BlockSpec((1, tc), lambda j, i: (0, j)),
        ],
        out_specs=out_col + [pl.BlockSpec((SUBLANES, tc), lambda j, i: (0, j))],
        compiler_params=_cp("parallel", "arbitrary"),
    )(dcat, dcat, proj, proj, proj, proj, proj, proj, proj, proj, cw, cb)


def _pair_tile(f):
    return _tile(f, 1408, LANES)


def _pair_perm(f):
    nj = f // _pair_tile(f)
    return lambda p: (p % 2) * nj + p // 2


def _pair_cols(a):
    r, f2 = a.shape
    tc = _pair_tile(f2 // 2)
    return a.reshape(r, 2, f2 // (2 * tc), tc).transpose(0, 2, 1, 3).reshape(r, f2)


def _unpair_cols(a):
    r, f2 = a.shape
    tc = _pair_tile(f2 // 2)
    return a.reshape(r, f2 // (2 * tc), 2, tc).transpose(0, 2, 1, 3).reshape(r, f2)


def _ffn_act_fwd(up, cw, cb):
    s, f2 = up.shape
    f = f2 // 2
    ts = _tile(s, 256, SUBLANES)
    tc = _pair_tile(f)
    prev, _ = _halo_maps(ts, s)

    def body(u_ref, p_ref, w_ref, b_ref, o_ref, ub_ref):
        keep = jnp.where(pl.program_id(1) > 0, 1.0, 0.0)
        ext = jnp.concatenate([p_ref[...] * keep, u_ref[...]], axis=0)
        u = _conv3(ext, w_ref[...], b_ref[...])[SUBLANES:]
        ub_ref[...] = u.astype(BF16)
        a, g = u[:, :tc], u[:, tc:]
        o_ref[...] = ((g * jax.nn.sigmoid(g)) * a).astype(BF16)

    def pair(rows, which):
        return pl.BlockSpec((rows, 2 * tc), lambda j, i: (which(i), j))

    return pl.pallas_call(
        body,
        name="ffn_act_fwd",
        out_shape=(jax.ShapeDtypeStruct((s, f), BF16), jax.ShapeDtypeStruct((s, f2), BF16)),
        grid=(f // tc, s // ts),
        in_specs=[pair(ts, lambda i: i), pair(SUBLANES, prev), pair(CONV_K, lambda i: 0), pair(1, lambda i: 0)],
        out_specs=[pl.BlockSpec((ts, tc), lambda j, i: (i, j)), pair(ts, lambda i: i)],
        compiler_params=_cp("parallel", "arbitrary"),
    )(up, up, cw, cb)


def _ffn_act_bwd(dact, u_b, up, cw):
    s, f2 = up.shape
    f = f2 // 2
    ts = _tile(s, 128, SUBLANES)
    tc = _pair_tile(f)
    nj, ni = f // tc, s // ts
    prev, nxt = _halo_maps(ts, s)
    halo_b = 2 * SUBLANES
    nxt_b = lambda i: jnp.minimum((i + 1) * (ts // halo_b), s // halo_b - 1)

    def body(d_ref, dn_ref, ub_ref, ubn_ref, x_ref, xp_ref, w_ref, dup_ref, s_ref):
        i = pl.program_id(1)
        keep_p = jnp.where(i > 0, 1.0, 0.0)
        keep_n = jnp.where(i < ni - 1, 1.0, 0.0)
        w = w_ref[...]
        u = jnp.concatenate([ub_ref[...], ubn_ref[0:SUBLANES, :]], axis=0).astype(F32)
        a, g = u[:, :tc], u[:, tc:]
        dact_v = jnp.concatenate([d_ref[...], dn_ref[...] * keep_n], axis=0)
        sg = jax.nn.sigmoid(g)
        du = jnp.concatenate([dact_v * (g * sg), dact_v * a * (sg * (1.0 + g * (1.0 - sg)))], axis=1)
        dup_ref[...] = _conv3_t(du, w)[:ts].astype(BF16)
        dt = du[:ts]
        ext = jnp.concatenate([xp_ref[...] * keep_p, x_ref[...]], axis=0)
        e1, e2 = pltpu.roll(ext, 1, 0), pltpu.roll(ext, 2, 0)
        lo = SUBLANES
        _acc_rows(s_ref, i, [_colsum(dt * e2[lo:]), _colsum(dt * e1[lo:]), _colsum(dt * ext[lo:]), _colsum(dt)])

    def pair(rows, which):
        return pl.BlockSpec((rows, 2 * tc), lambda j, i: (which(i), j))

    return pl.pallas_call(
        body,
        name="ffn_act_bwd",
        out_shape=(jax.ShapeDtypeStruct((s, f2), BF16), jax.ShapeDtypeStruct((SUBLANES, f2), F32)),
        grid=(nj, ni),
        in_specs=[
            pl.BlockSpec((ts, tc), lambda j, i: (i, j)),
            pl.BlockSpec((SUBLANES, tc), lambda j, i: (nxt(i), j)),
            pair(ts, lambda i: i), pair(halo_b, nxt_b),
            pair(ts, lambda i: i), pair(SUBLANES, prev),
            pair(CONV_K, lambda i: 0),
        ],
        out_specs=[pair(ts, lambda i: i), pair(SUBLANES, lambda i: 0)],
        compiler_params=_cp("parallel", "arbitrary"),
    )(dact, dact, u_b, u_b, up, up, cw)


ATT_SCALE = 1.0 / math.sqrt(NOPE + ROPE)
LOG2E = math.log2(math.e)
ATT_C2 = ATT_SCALE * LOG2E
STAT_SPLIT = 64
NT = (((1,), (1,)), ((), ()))
TN = (((0,), (0,)), ((), ()))


def _head_cat(q, kv, kr, tabs, n_heads):
    s, w2 = q.shape
    w = w2 // 2
    ts = _tile(s, 512, SUBLANES)
    hd = NOPE + HEAD_PAD

    def body(q_ref, kv_ref, kr_ref, c_ref, sa_ref, sb_ref, qc_ref, kc_ref):
        qv = q_ref[...]
        qr = _rope(qv[:, w:], c_ref[...], sa_ref[...], sb_ref[...]).astype(BF16)
        krv = kr_ref[...]
        for h in range(n_heads):
            qc_ref[:, h * hd : h * hd + NOPE] = qv[:, h * NOPE : (h + 1) * NOPE].astype(BF16)
            qc_ref[:, h * hd + NOPE : (h + 1) * hd] = qr[:, h * HEAD_PAD : (h + 1) * HEAD_PAD]
            kc_ref[:, h * hd : h * hd + NOPE] = kv_ref[:, h * NOPE : (h + 1) * NOPE]
            kc_ref[:, h * hd + NOPE : (h + 1) * hd] = krv

    out = jax.ShapeDtypeStruct((s, n_heads * hd), BF16)
    return pl.pallas_call(
        body,
        name="head_cat",
        out_shape=(out, out),
        grid=(s // ts,),
        in_specs=[_rows(ts, w2), _rows(ts, w), _rows(ts, HEAD_PAD)] + [_rows(ts, LANES)] * 3,
        out_specs=[_rows(ts, n_heads * hd)] * 2,
        compiler_params=_cp("parallel"),
    )(q, kv, kr, *tabs)


def _attn_fwd(qc, kc, kv, n_heads, cat_cols):
    s = qc.shape[0]
    t = _tile(s, ATT_FWD_BLOCK, LANES)
    sub = _tile(t, ATT_FWD_SUB, LANES)
    hh = n_heads
    hd = NOPE + HEAD_PAD

    def body(q_ref, k_ref, v_ref, o_ref, lse_ref, m_s, l_s, acc_s):
        i = pl.program_id(1)
        m_s[...] = jnp.full(m_s.shape, NEG, F32)
        l_s[...] = jnp.zeros(l_s.shape, F32)
        acc_s[...] = jnp.zeros(acc_s.shape, F32)

        def chunk(k0, diag):
            m_all, l_all, acc_all = m_s[...], l_s[...], acc_s[...]
            new_m, new_l, new_acc = [], [], []

            def scores(r0):
                ncol = r0 + sub if diag else t
                return lax.dot_general(q_ref[pl.ds(r0, sub), :], k_ref[pl.ds(k0, ncol), :], NT, preferred_element_type=F32)

            sc_next = scores(0)
            for r0 in range(0, t, sub):
                ncol = r0 + sub if diag else t
                sc = sc_next
                if r0 + sub < t:
                    sc_next = scores(r0 + sub)
                if diag:
                    row = lax.broadcasted_iota(jnp.int32, sc.shape, 0) + r0
                    col = lax.broadcasted_iota(jnp.int32, sc.shape, 1)
                    sc = jnp.where(col <= row, sc, NEG)
                m_prev = m_all[r0 : r0 + sub]
                m_new = jnp.maximum(m_prev, jnp.max(sc, axis=1, keepdims=True))
                alpha = jnp.exp2((m_prev - m_new) * ATT_C2)
                p = jnp.exp2((sc - m_new) * ATT_C2)
                pv = jnp.dot(p.astype(BF16), v_ref[pl.ds(k0, ncol), :], preferred_element_type=F32)
                new_m.append(m_new)
                new_l.append(alpha * l_all[r0 : r0 + sub] + jnp.sum(p, axis=1, keepdims=True))
                new_acc.append(alpha * acc_all[r0 : r0 + sub] + pv)
            m_s[...] = jnp.concatenate(new_m, axis=0)
            l_s[...] = jnp.concatenate(new_l, axis=0)
            acc_s[...] = jnp.concatenate(new_acc, axis=0)

        def loop_body(k, carry):
            chunk(pl.multiple_of(k * t, t), False)
            return carry

        lax.fori_loop(0, i, loop_body, 0)
        chunk(pl.multiple_of(i * t, t), True)
        l = l_s[...]
        o_ref[...] = (acc_s[...] / l).astype(BF16)
        lse_ref[...] = jnp.broadcast_to(m_s[...] * ATT_C2 + jnp.log(l) * LOG2E, lse_ref.shape)

    return pl.pallas_call(
        body,
        name="attn_fwd",
        out_shape=(jax.ShapeDtypeStruct((s, cat_cols), BF16), jax.ShapeDtypeStruct((s, hh * LANES), F32)),
        grid=(hh, s // t),
        in_specs=[
            pl.BlockSpec((t, hd), lambda h, i: (i, h)),
            pl.BlockSpec((s, hd), lambda h, i: (0, h)),
            pl.BlockSpec((s, VDIM), lambda h, i: (0, hh + h)),
        ],
        out_specs=[pl.BlockSpec((t, VDIM), lambda h, i: (i, h)), pl.BlockSpec((t, LANES), lambda h, i: (i, h))],
        scratch_shapes=[pltpu.VMEM((t, 1), F32), pltpu.VMEM((t, 1), F32), pltpu.VMEM((t, VDIM), F32)],
        compiler_params=_cp("parallel", "parallel"),
    )(qc, kc, kv)


def _attn_bwd_prep(cat, dcat, lse2, n_heads):
    s, w = lse2.shape
    ts = _tile(s, 512, SUBLANES)

    def body(o_ref, do_ref, lse_ref, dob_ref, st_ref):
        do = do_ref[...]
        dob_ref[...] = do.astype(BF16)
        prod = do * o_ref[...].astype(F32)
        lane = lax.broadcasted_iota(jnp.int32, (ts, LANES), 1)
        for h in range(n_heads):
            cols = slice(h * LANES, (h + 1) * LANES)
            dsum = jnp.sum(prod[:, cols], axis=1, keepdims=True)
            st_ref[:, cols] = jnp.where(lane < STAT_SPLIT, lse_ref[:, cols], dsum)

    return pl.pallas_call(
        body,
        name="attn_bwd_prep",
        out_shape=(jax.ShapeDtypeStruct((s, w), BF16), jax.ShapeDtypeStruct((s, w), F32)),
        grid=(s // ts,),
        in_specs=[_rows(ts, w)] * 3,
        out_specs=[_rows(ts, w)] * 2,
        compiler_params=_cp("parallel"),
    )(cat, dcat, lse2)


def _attn_bwd(qc, kc, kv, dob, stats, n_heads):
    s = qc.shape[0]
    t = _tile(s, ATT_BWD_BLOCK, LANES)
    sub = _tile(t, ATT_BWD_SUB, LANES)
    nb = s // t
    hh = n_heads
    hd = NOPE + HEAD_PAD
    w = hh * LANES

    def body(q_ref, k_ref, v_ref, do_ref, st_ref, dq_ref, dkn_ref, dv_ref, dkr_ref, dk_s, dv_s):
        j = pl.program_id(1)

        @pl.when(j == 0)
        def _():
            dq_ref[...] = jnp.zeros(dq_ref.shape, F32)

        dk_s[...] = jnp.zeros(dk_s.shape, F32)
        dv_s[...] = jnp.zeros(dv_s.shape, F32)

        def pair(i0, diag):
            def width(r0):
                return r0 + sub if diag else t

            def products(r0):
                rows = pl.ds(i0 + r0, sub)
                sc_ = lax.dot_general(q_ref[rows, :], k_ref[0 : width(r0), :], NT, preferred_element_type=F32)
                dp_ = lax.dot_general(do_ref[rows, :], v_ref[0 : width(r0), :], NT, preferred_element_type=F32)
                return sc_, dp_

            nxt = products(0)
            for r0 in range(0, t, sub):
                ncol = width(r0)
                rows = pl.ds(i0 + r0, sub)
                kk = k_ref[0:ncol, :]
                qq, do, st = q_ref[rows, :], do_ref[rows, :], st_ref[rows, :]
                sc, dp = nxt
                if r0 + sub < t:
                    nxt = products(r0 + sub)
                if diag:
                    row = lax.broadcasted_iota(jnp.int32, sc.shape, 0) + r0
                    col = lax.broadcasted_iota(jnp.int32, sc.shape, 1)
                    sc = jnp.where(col <= row, sc, NEG)
                p = jnp.exp2(sc * ATT_C2 - st[:, 0:1])
                dv_s[0:ncol, :] += lax.dot_general(p.astype(BF16), do, TN, preferred_element_type=F32)
                ds = (p * (dp - st[:, STAT_SPLIT : STAT_SPLIT + 1]) * ATT_SCALE).astype(BF16)
                dk_s[0:ncol, :] += lax.dot_general(ds, qq, TN, preferred_element_type=F32)
                dq_ref[rows, :] += jnp.dot(ds, kk, preferred_element_type=F32)

        pair(pl.multiple_of(j * t, t), True)

        def loop_body(i, carry):
            pair(pl.multiple_of(i * t, t), False)
            return carry

        lax.fori_loop(j + 1, nb, loop_body, 0)
        dkn_ref[...] = dk_s[:, :NOPE].astype(BF16)
        dv_ref[...] = dv_s[...].astype(BF16)
        dkr_ref[...] = dk_s[:, NOPE:]

    whole = lambda width, off: pl.BlockSpec((s, width), lambda h, j: (0, off + h))
    blk = lambda width, off: pl.BlockSpec((t, width), lambda h, j: (j, off + h))
    return pl.pallas_call(
        body,
        name="attn_bwd",
        out_shape=(
            jax.ShapeDtypeStruct((s, hh * hd), F32),
            jax.ShapeDtypeStruct((s, w), BF16),
            jax.ShapeDtypeStruct((s, w), BF16),
            jax.ShapeDtypeStruct((s, w), F32),
        ),
        grid=(hh, nb),
        in_specs=[whole(hd, 0), blk(hd, 0), blk(VDIM, hh), whole(VDIM, 0), whole(LANES, 0)],
        out_specs=[whole(hd, 0), blk(NOPE, 0), blk(VDIM, 0), blk(HEAD_PAD, 0)],
        scratch_shapes=[pltpu.VMEM((t, hd), F32), pltpu.VMEM((t, VDIM), F32)],
        compiler_params=_cp("parallel", "arbitrary"),
    )(qc, kc, kv, dob, stats)


def _dq_unrope(dq, tabs, n_heads):
    s = dq.shape[0]
    hd = NOPE + HEAD_PAD
    w = n_heads * LANES
    ts = _tile(s, 512, SUBLANES)

    def body(d_ref, c_ref, sa_ref, sb_ref, o_ref):
        c, sa, sb = c_ref[...], sa_ref[...], sb_ref[...]
        for h in range(n_heads):
            o_ref[:, h * NOPE : (h + 1) * NOPE] = d_ref[:, h * hd : h * hd + NOPE].astype(BF16)
            rot = _rope_t(d_ref[:, h * hd + NOPE : (h + 1) * hd], c, sa, sb)
            o_ref[:, w + h * HEAD_PAD : w + (h + 1) * HEAD_PAD] = rot.astype(BF16)

    return pl.pallas_call(
        body,
        name="dq_unrope",
        out_shape=jax.ShapeDtypeStruct((s, 2 * w), BF16),
        grid=(s // ts,),
        in_specs=[_rows(ts, n_heads * hd)] + [_rows(ts, LANES)] * 3,
        out_specs=_rows(ts, 2 * w),
        compiler_params=_cp("parallel"),
    )(dq, *tabs)


def _adamw(w, m, v, grads, name):
    r, c = w.shape
    budget_rows = max(SUBLANES, (VMEM_LIMIT // 3) // (4 * c * 2 * (7 + len(grads))))
    tr = _tile(r, budget_rows, SUBLANES)
    ng = len(grads)
    c1 = 1.0 - ADAM_B1**ADAM_STEP
    c2 = 1.0 - ADAM_B2**ADAM_STEP

    def body(*refs):
        w_ref, m_ref, v_ref = refs[:3]
        g_ref, d_ref, nm_ref, nv_ref = refs[3 + ng :]
        g = refs[3][...]
        for extra in refs[4 : 3 + ng]:
            g = g + extra[...]
        mn = ADAM_B1 * m_ref[...] + (1.0 - ADAM_B1) * g
        vn = ADAM_B2 * v_ref[...] + (1.0 - ADAM_B2) * (g * g)
        g_ref[...] = g
        nm_ref[...] = mn
        nv_ref[...] = vn
        d_ref[...] = -ADAM_LR * ((mn / c1) / (jnp.sqrt(vn / c2) + ADAM_EPS) + ADAM_WD * w_ref[...])

    blk = pl.BlockSpec((tr, c), lambda i: (i, 0))
    out = jax.ShapeDtypeStruct((r, c), F32)
    return pl.pallas_call(
        body,
        name=name,
        out_shape=(out, out, out, out),
        grid=(r // tr,),
        in_specs=[blk] * (3 + ng),
        out_specs=[blk] * 4,
        compiler_params=_cp("parallel"),
    )(w, m, v, *grads)


def _ada_grad(ca_t, dm):
    d = ca_t.shape[0]
    nc = dm.shape[1]
    tn = _tile(nc, 512, LANES)

    def body(a_ref, b_ref, o_ref):
        o_ref[...] = jnp.dot(a_ref[...].astype(BF16), b_ref[...].astype(BF16), preferred_element_type=F32)

    return pl.pallas_call(
        body,
        name="ada_grad",
        out_shape=jax.ShapeDtypeStruct((d, nc), F32),
        grid=(nc // tn,),
        in_specs=[pl.BlockSpec((d, LANES), lambda j: (0, 0)), pl.BlockSpec((LANES, tn), lambda j: (0, j))],
        out_specs=pl.BlockSpec((d, tn), lambda j: (0, j)),
        compiler_params=_cp("parallel"),
    )(ca_t, dm)


def _sum_devices(g):
    n = g.shape[1]

    def body(g_ref, o_ref):
        acc = g_ref[0:SUBLANES, :]
        for dvc in range(1, N_DEV):
            acc = acc + g_ref[dvc * SUBLANES : (dvc + 1) * SUBLANES, :]
        o_ref[...] = acc

    return pl.pallas_call(
        body,
        name="sum_devices",
        out_shape=jax.ShapeDtypeStruct((SUBLANES, n), F32),
        in_specs=[pl.BlockSpec(memory_space=pltpu.VMEM)],
        out_specs=pl.BlockSpec(memory_space=pltpu.VMEM),
        compiler_params=pltpu.CompilerParams(vmem_limit_bytes=VMEM_LIMIT),
    )(g)


def _sum_chips(land, sent, name):
    _, r, c = land.shape
    tr = _tile(r, max(SUBLANES * 2, (VMEM_LIMIT // 4) // (c * 2 * (4 * N_CHIP + 4 * 2))), SUBLANES * 2)

    def body(l_ref, s_ref, o_ref):
        x, y, _ = _mesh_pos()
        me = 2 * x + y
        acc = jnp.where(me == 0, s_ref[0], l_ref[0]).astype(F32)
        for k in range(1, N_CHIP):
            acc = acc + jnp.where(me == k, s_ref[k], l_ref[k]).astype(F32)
        o_ref[...] = acc

    slots = pl.BlockSpec((N_CHIP, tr, c), lambda i: (0, i, 0))
    return pl.pallas_call(
        body,
        name=name,
        out_shape=jax.ShapeDtypeStruct((r, c), F32),
        grid=(r // tr,),
        in_specs=[slots, slots],
        out_specs=pl.BlockSpec((tr, c), lambda i: (i, 0)),
        compiler_params=_cp("parallel"),
    )(land, sent)


def _mesh_pos():
    return lax.axis_index("x"), lax.axis_index("y"), lax.axis_index("c")


def _other_chips(x, y):
    return [(1 - x, y), (x, 1 - y), (1 - x, 1 - y)]


def _all_gather8(x_shard, name):
    m_per, n = x_shard.shape

    def body(x_ref, out_ref, send_sems, recv_sems, local_sem):
        x, y, c = _mesh_pos()
        me, sibling = (x, y, c), (x, y, 1 - c)
        chips = _other_chips(x, y)

        def rows(px, py, pc):
            return out_ref.at[pl.ds((4 * px + 2 * py + pc) * m_per, m_per), :]

        def copy(k, block, to, src=None):
            return pltpu.make_async_remote_copy(
                src_ref=rows(*block) if src is None else src,
                dst_ref=rows(*block),
                send_sem=send_sems.at[k],
                recv_sem=recv_sems.at[k],
                device_id=to,
                device_id_type=MESH,
            )

        mine = pltpu.make_async_copy(x_ref, rows(*me), local_sem)
        mine.start()
        first = [copy(0, me, sibling, src=x_ref)]
        first += [copy(1 + j, me, (*chip, c), src=x_ref) for j, chip in enumerate(chips)]
        for cp in first:
            cp.start()
        passed = [copy(4 + j, (*chip, c), sibling) for j, chip in enumerate(chips)]
        for j, chip in enumerate(chips):
            copy(1 + j, (*chip, c), me).wait_recv()
            passed[j].start()
        copy(0, sibling, me).wait_recv()
        for j, chip in enumerate(chips):
            copy(4 + j, (*chip, 1 - c), me).wait_recv()
        for cp in first + passed:
            cp.wait_send()
        mine.wait()

    return pl.pallas_call(
        body,
        name=name,
        out_shape=jax.ShapeDtypeStruct((N_DEV * m_per, n), x_shard.dtype),
        in_specs=[pl.BlockSpec(memory_space=pltpu.VMEM)],
        out_specs=pl.BlockSpec(memory_space=pltpu.VMEM),
        scratch_shapes=[pltpu.SemaphoreType.DMA((7,)), pltpu.SemaphoreType.DMA((7,)), pltpu.SemaphoreType.DMA],
        compiler_params=pltpu.CompilerParams(vmem_limit_bytes=VMEM_LIMIT),
    )(x_shard)


HBM_SPEC = pl.BlockSpec(memory_space=pltpu.HBM)
SEM_SPEC = pl.BlockSpec(memory_space=pltpu.SEMAPHORE)
DATAFLOW = pltpu.SideEffectType.DATAFLOW_SIDE_EFFECTING


def _exchange_copies(ins, lands, send_sems, recv_sems, scatter):
    x, y, c = _mesh_pos()
    me = 2 * x + y
    sends, recvs = [], []
    for t in range(len(ins)):
        for r, (px, py) in enumerate(_other_chips(x, y)):
            peer = 2 * px + py

            def copy(src, dst, k=3 * t + r, to=(px, py, c)):
                return pltpu.make_async_remote_copy(
                    src_ref=src, dst_ref=dst, send_sem=send_sems.at[k], recv_sem=recv_sems.at[k], device_id=to, device_id_type=MESH
                )

            sends.append(copy(ins[t].at[peer] if scatter else ins[t], lands[t].at[me]))
            recvs.append(copy(ins[t].at[me] if scatter else ins[t], lands[t].at[peer]))
    return sends, recvs


def _exchange_start(arrs, scatter, name):
    nt = len(arrs)
    lands = [lax.empty(a.shape if scatter else (N_CHIP, *a.shape), a.dtype) for a in arrs]

    def body(*refs):
        ins, zones = refs[:nt], refs[nt : 2 * nt]
        send_sems, recv_sems, token = refs[2 * nt], refs[2 * nt + 1], refs[-1]
        sends, _ = _exchange_copies(ins, zones, send_sems, recv_sems, scatter)
        for cp in sends:
            cp.start()
        token[...] = jnp.zeros(token.shape, F32)

    bufs = list(arrs) + list(lands)
    return pl.pallas_call(
        body,
        name=name,
        out_shape=(
            pltpu.SemaphoreType.DMA((3 * nt,)),
            pltpu.SemaphoreType.DMA((3 * nt,)),
            *[pltpu.HBM(a.shape, a.dtype) for a in bufs],
            jax.ShapeDtypeStruct((SUBLANES, LANES), F32),
        ),
        in_specs=[HBM_SPEC] * (2 * nt),
        out_specs=(SEM_SPEC, SEM_SPEC, *[HBM_SPEC] * (2 * nt), pl.BlockSpec(memory_space=pltpu.VMEM)),
        input_output_aliases={k: 2 + k for k in range(2 * nt)},
        compiler_params=pltpu.CompilerParams(has_side_effects=DATAFLOW),
    )(*[pltpu.with_memory_space_constraint(a, pltpu.HBM) for a in bufs])


def _exchange_wait(state, after, scatter, name):
    send_sems, recv_sems, *bufs = state[:-1]
    nt = len(bufs) // 2

    def body(*refs):
        ins, zones = refs[:nt], refs[nt : 2 * nt]
        sends, recvs = _exchange_copies(ins, zones, refs[2 * nt], refs[2 * nt + 1], scatter)
        for cp in sends:
            cp.wait_send()
        for cp in recvs:
            cp.wait_recv()

    out = pl.pallas_call(
        body,
        name=name,
        out_shape=tuple(pltpu.HBM(a.shape, a.dtype) for a in bufs),
        in_specs=[HBM_SPEC] * (2 * nt) + [SEM_SPEC, SEM_SPEC, pl.BlockSpec(memory_space=pl.ANY)],
        out_specs=[HBM_SPEC] * (2 * nt),
        input_output_aliases={k: k for k in range(2 * nt)},
        compiler_params=pltpu.CompilerParams(has_side_effects=DATAFLOW),
    )(*bufs, send_sems, recv_sems, after)
    return list(out[:nt]), list(out[nt:])


def _swap_copies(ins, lands, send_sems, recv_sems):
    x, y, c = _mesh_pos()
    return [
        pltpu.make_async_remote_copy(
            src_ref=ins[t], dst_ref=lands[t], send_sem=send_sems.at[t], recv_sem=recv_sems.at[t],
            device_id=(x, y, 1 - c), device_id_type=MESH,
        )
        for t in range(len(ins))
    ]


def _swap_start(arrs, name):
    nt = len(arrs)
    lands = [lax.empty(a.shape, a.dtype) for a in arrs]

    def body(*refs):
        ins, zones = refs[:nt], refs[nt : 2 * nt]
        send_sems, recv_sems, token = refs[2 * nt], refs[2 * nt + 1], refs[-1]
        for cp in _swap_copies(ins, zones, send_sems, recv_sems):
            cp.start()
        token[...] = jnp.zeros(token.shape, F32)

    bufs = list(arrs) + lands
    return pl.pallas_call(
        body,
        name=name,
        out_shape=(
            pltpu.SemaphoreType.DMA((nt,)),
            pltpu.SemaphoreType.DMA((nt,)),
            *[pltpu.HBM(a.shape, a.dtype) for a in bufs],
            jax.ShapeDtypeStruct((SUBLANES, LANES), F32),
        ),
        in_specs=[HBM_SPEC] * (2 * nt),
        out_specs=(SEM_SPEC, SEM_SPEC, *[HBM_SPEC] * (2 * nt), pl.BlockSpec(memory_space=pltpu.VMEM)),
        input_output_aliases={k: 2 + k for k in range(2 * nt)},
        compiler_params=pltpu.CompilerParams(has_side_effects=DATAFLOW),
    )(*[pltpu.with_memory_space_constraint(a, pltpu.HBM) for a in bufs])


def _swap_wait(state, after, name):
    send_sems, recv_sems, *bufs = state[:-1]
    nt = len(bufs) // 2

    def body(*refs):
        cps = _swap_copies(refs[:nt], refs[nt : 2 * nt], refs[2 * nt], refs[2 * nt + 1])
        for cp in cps:
            cp.wait_send()
        for cp in cps:
            cp.wait_recv()

    out = pl.pallas_call(
        body,
        name=name,
        out_shape=tuple(pltpu.HBM(a.shape, a.dtype) for a in bufs),
        in_specs=[HBM_SPEC] * (2 * nt) + [SEM_SPEC, SEM_SPEC, pl.BlockSpec(memory_space=pl.ANY)],
        out_specs=[HBM_SPEC] * (2 * nt),
        input_output_aliases={k: k for k in range(2 * nt)},
        compiler_params=pltpu.CompilerParams(has_side_effects=DATAFLOW),
    )(*bufs, send_sems, recv_sems, after)
    return list(out[:nt]), list(out[nt:])


def _cols_from_shards(g):
    _, k, n = g.shape
    return jnp.transpose(g, (1, 0, 2)).reshape(k, N_CHIP * n)


def _cols_to_shards(a):
    k, n4 = a.shape
    return jnp.transpose(a.reshape(k, N_CHIP, n4 // N_CHIP), (1, 0, 2))


def _pad_to(vec, mult):
    n = vec.shape[0]
    return jnp.pad(vec, (0, (-n) % mult))


def kernel(x, c, positions, w_ada, b_ada, g_pre_mix, g_post_mix, w_in, g_q, w_uq, g_kv, w_ukv, conv_w_mix, conv_b_mix, w_o, g_pre_ffn, g_post_ffn, w_up, conv_w_ffn, conv_b_ffn, w_down, loss_target, m_w_ada, m_b_ada, m_g_pre_mix, m_g_post_mix, m_w_in, m_g_q, m_w_uq, m_g_kv, m_w_ukv, m_conv_w_mix, m_conv_b_mix, m_w_o, m_g_pre_ffn, m_g_post_ffn, m_w_up, m_conv_w_ffn, m_conv_b_ffn, m_w_down, v_w_ada, v_b_ada, v_g_pre_mix, v_g_post_mix, v_w_in, v_g_q, v_w_uq, v_g_kv, v_w_ukv, v_conv_w_mix, v_conv_b_mix, v_w_o, v_g_pre_ffn, v_g_post_ffn, v_w_up, v_conv_w_ffn, v_conv_b_ffn, v_w_down):
    xi, yi, ci = _mesh_pos()
    chip = 2 * xi + yi
    dev = 4 * xi + 2 * yi + ci

    s, d = x.shape[1], x.shape[2]
    ql, kl = g_q.shape[1], g_kv.shape[1]
    cwid = conv_b_mix.shape[1]
    f2 = conv_b_ffn.shape[1]
    hh = (w_uq.shape[2] * N_CHIP) // (NOPE + ROPE)
    w_att = hh * LANES
    nc_ada = w_ada.shape[2]
    lat = ql + kl + ROPE
    tc_mix = _tile(cwid, 512, LANES)
    lb = -(-(ql + kl + HEAD_PAD) // tc_mix) * tc_mix
    np_cols = lb + 3 * cwid
    assert cwid == hh * VDIM and w_att % tc_mix == 0

    x0 = x.reshape(s, d)
    tgt = loss_target.reshape(s, d)

    anchors = []

    def _behind(val, state):
        val, tok = lax.optimization_barrier((val, state[-1]))
        anchors.append(tok[0, 0])
        return val

    cwm_n, cwf_n = CONV_K * cwid // N_CHIP, CONV_K * f2 // N_CHIP
    pack_a = _pad_to(jnp.concatenate([c.reshape(-1), conv_w_mix.reshape(-1), conv_w_ffn.reshape(-1)]), SUBLANES * LANES)
    rows_a = _all_gather8(pack_a.reshape(SUBLANES, -1), "ag8_inputs").reshape(N_DEV, -1)
    c_all = rows_a[:, :d]
    south = rows_a[0::2]
    cw_mix = jnp.concatenate([south[j, d : d + cwm_n].reshape(CONV_K, -1) for j in range(N_CHIP)], axis=1)
    cw_ffn = jnp.concatenate([south[j, d + cwm_n : d + cwm_n + cwf_n].reshape(CONV_K, -1) for j in range(N_CHIP)], axis=1)

    b_cols = lax.dynamic_slice(b_ada, (0, chip * nc_ada), (1, nc_ada))
    mod_part, c_act = _ada_fwd(c_all, w_ada[0], b_cols)
    mod_rows = _all_gather8(mod_part, "ag8_mod")
    mod = jnp.concatenate(
        [lax.dynamic_slice_in_dim(mod_rows, 2 * N_DEV * j + dev, 1, axis=0) for j in range(N_CHIP)], axis=1
    )

    shards = [a[0].astype(BF16) for a in (w_in, w_uq, w_ukv, w_o, w_up, w_down)]
    first, mod = lax.optimization_barrier((shards[:3], mod))
    ag_a = _exchange_start(first, False, "ag_a_start")
    mod = _behind(mod, ag_a)
    sh_m, sc_m, gt_m, sh_f, sc_f, gt_f = [mod[:, k * d : (k + 1) * d] for k in range(N_MOD)]

    inv_freq = 1.0 / (ROPE_THETA ** (jnp.arange(0, ROPE, 2, dtype=F32) / ROPE))
    invf = jnp.concatenate([inv_freq, inv_freq, jnp.zeros((LANES - ROPE,), F32)]).reshape(1, LANES)
    tabs = _rope_tables(positions.astype(F32).reshape(s, 1), invf)
    h1 = _pre_fwd(x0, g_pre_mix, sc_m, sh_m)

    def with_own(landed, own):
        return [lax.dynamic_update_slice_in_dim(g, a[None], chip, axis=0) for g, a in zip(landed, own)]

    own_w, landed_w = _exchange_wait(ag_a, h1, False, "ag_a_wait")
    rest, landed_w = lax.optimization_barrier((shards[3:], landed_w))
    ag_b = _exchange_start(rest, False, "ag_b_start")
    h1 = _behind(h1, ag_b)
    g_in, g_uq, g_ukv = with_own(landed_w, own_w)
    full_in = _cols_from_shards(g_in)
    w_in_p = jnp.concatenate([full_in[:, :lat], jnp.zeros((d, lb - lat), BF16), full_in[:, lat:]], axis=1)
    full_uq = _cols_from_shards(g_uq).reshape(ql, hh, NOPE + ROPE)
    w_uq_p = jnp.concatenate(
        [
            full_uq[:, :, :NOPE].reshape(ql, w_att),
            jnp.pad(full_uq[:, :, NOPE:], ((0, 0), (0, 0), (0, HEAD_PAD - ROPE))).reshape(ql, w_att),
        ],
        axis=1,
    )
    full_ukv = _cols_from_shards(g_ukv).reshape(kl, hh, NOPE + VDIM)
    w_ukv_p = jnp.concatenate([full_ukv[:, :, :NOPE].reshape(kl, w_att), full_ukv[:, :, NOPE:].reshape(kl, w_att)], axis=1)

    proj = _matmul(h1, w_in_p, out_dtype=F32, tm=1024, tn=768, tk=2048, name="mm_proj")
    qn, kvn, kr = _latent_fwd(proj, g_q, g_kv, tabs, lb)
    q_f = _matmul(qn, w_uq_p, out_dtype=F32, tm=1024, tn=1024, tk=2048, name="mm_q")
    kv_p = _matmul(kvn, w_ukv_p, out_dtype=BF16, tm=1024, tn=1024, tk=2048, name="mm_kv")
    q_c, k_c = _head_cat(q_f, kv_p, kr, tabs, hh)
    cat, lse2 = _attn_fwd(q_c, k_c, kv_p, hh, w_att + cwid)
    cat = _mixer_fwd(cat, proj, cw_mix, conv_b_mix, lb, w_att)
    own_w, landed_w = _exchange_wait(ag_b, cat, False, "ag_b_wait")
    g_o, g_up, g_down = with_own(landed_w, own_w)
    w_o_f = g_o.reshape(-1, d)
    cw_ffn_p, cb_ffn_p = _pair_cols(cw_ffn), _pair_cols(conv_b_ffn)
    tcp, pair_perm = _pair_tile(f2 // 2), _pair_perm(f2 // 2)
    w_down_f = g_down.reshape(-1, d)
    mix = _matmul(cat, w_o_f, out_dtype=F32, tm=1024, tn=1024, tk=2048, name="mm_mix")

    x1, h2 = _mid_fwd(x0, mix, g_post_mix, gt_m, g_pre_ffn, sc_f, sh_f)
    up = _matmul(h2, g_up, out_dtype=F32, tm=1024, tn=tcp, tk=2048, name="mm_up", b_n_perm=pair_perm, b_col_shards=True)
    act, u_b = _ffn_act_fwd(up, cw_ffn_p, cb_ffn_p)
    y = _matmul(act, w_down_f, out_dtype=F32, tm=512, tn=1024, tk=5632, name="mm_down")
    dx2, dy, s_fin = _final(x1, y, tgt, g_post_ffn, gt_f)

    dw_down = _matmul(act, dy, ta=True, out_dtype=BF16, tm=1408, tn=1024, tk=2048, name="mm_dw_down")
    dact = _matmul(dy, w_down_f, tb=True, out_dtype=F32, tm=1024, tn=1408, tk=2048, name="mm_dact")
    dup, s_ffn_p = _ffn_act_bwd(dact, u_b, up, cw_ffn_p)
    s_ffn = _unpair_cols(s_ffn_p)
    dw_up = _matmul(
        h2, dup, ta=True, out_dtype=BF16, tm=1024, tn=tcp, tk=2048, name="mm_dw_up", out_n_perm=pair_perm, out_col_shards=True
    )
    dh2 = _matmul(
        dup, g_up, tb=True, out_dtype=F32, tm=1024, tn=1024, tk=tcp, name="mm_dh2", b_k_perm=pair_perm, b_col_shards=True
    )
    dx1, dmix, s_mid = _mid_bwd(dh2, dx2, x1, mix, g_pre_ffn, sc_f, g_post_mix, gt_m)

    dw_o = _matmul(cat, dmix, ta=True, out_dtype=BF16, tm=1024, tn=1024, tk=2048, name="mm_dw_o")
    send_b = [dw_o.reshape(N_CHIP, -1, d), dw_up, dw_down.reshape(N_CHIP, -1, d)]
    rs_b = _exchange_start(send_b, True, "rs_b_start")
    dmix = _behind(dmix, rs_b)
    dcat = _matmul(dmix, w_o_f, tb=True, out_dtype=F32, tm=1024, tn=1024, tk=2048, name="mm_dcat")
    dp_b, dp_c, dp_i, s_mix = _mixer_bwd(dcat, proj, cw_mix, conv_b_mix, lb, w_att)
    dob, stats = _attn_bwd_prep(cat, dcat, lse2, hh)
    dq_raw, dkv_k, dkv_v, dkr_h = _attn_bwd(q_c, k_c, kv_p, dob, stats, hh)
    dkv_p = jnp.concatenate([dkv_k, dkv_v], axis=1)
    dq_p = _dq_unrope(dq_raw, tabs, hh)
    dw_uq_p = _matmul(qn, dq_p, ta=True, out_dtype=BF16, tm=1024, tn=1024, tk=1024, name="mm_dw_uq")
    dqn = _matmul(dq_p, w_uq_p, tb=True, out_dtype=F32, tm=1024, tn=1024, tk=2048, name="mm_dqn")
    dw_ukv_p = _matmul(kvn, dkv_p, ta=True, out_dtype=BF16, tm=1024, tn=1024, tk=1024, name="mm_dw_ukv")
    dkvn = _matmul(dkv_p, w_ukv_p, tb=True, out_dtype=F32, tm=1024, tn=1024, tk=2048, name="mm_dkvn")
    dp_lat, s_lat = _latent_bwd(proj, dqn, dkvn, dkr_h, g_q, g_kv, tabs, lb)
    dproj = jnp.concatenate([dp_lat, dp_b, dp_c, dp_i], axis=1)
    dw_in_p = _matmul(h1, dproj, ta=True, out_dtype=BF16, tm=1024, tn=1536, tk=2048, name="mm_dw_in")

    dw_in_f = jnp.concatenate([dw_in_p[:, :lat], dw_in_p[:, lb:]], axis=1)
    uq3 = dw_uq_p.reshape(ql, 2, hh, LANES)
    dw_uq_f = jnp.concatenate([uq3[:, 0], uq3[:, 1, :, :ROPE]], axis=2).reshape(ql, hh * (NOPE + ROPE))
    ukv3 = dw_ukv_p.reshape(kl, 2, hh, LANES)
    dw_ukv_f = jnp.concatenate([ukv3[:, 0], ukv3[:, 1]], axis=2).reshape(kl, hh * (NOPE + VDIM))
    send_a = [_cols_to_shards(dw_in_f), _cols_to_shards(dw_uq_f), _cols_to_shards(dw_ukv_f)]
    rs_a = _exchange_start(send_a, True, "rs_a_start")
    dproj = _behind(dproj, rs_a)

    dh1 = _matmul(dproj, w_in_p, tb=True, out_dtype=F32, tm=512, tn=1024, tk=4608, name="mm_dh1")
    grad_x, s_first = _first_bwd(dh1, dx1, x0, g_pre_mix, sc_m)

    names = ["w_in", "w_uq", "w_ukv", "w_o", "w_up", "w_down"]
    sent_b, landed_b = _exchange_wait(rs_b, s_first, True, "rs_b_wait")
    sent_a, landed_a = _exchange_wait(rs_a, landed_b[0], True, "rs_a_wait")
    landed_a, s_first = lax.optimization_barrier((landed_a, s_first))
    part = [_sum_chips(l, a, "sum_chips_" + n) for l, a, n in zip(landed_a + landed_b, sent_a + sent_b, names)]

    dmod = jnp.concatenate([s_first[0:1], s_first[1:2], s_mid[3:4], s_mid[0:1], s_mid[1:2], s_fin[0:1]], axis=1)
    small = [
        dmod,
        s_first[2:3],
        s_mid[4:5],
        s_lat[0:1, :ql],
        s_lat[0:1, ql : ql + kl],
        s_mix[3:4],
        s_mid[2:3],
        s_fin[1:2],
        s_ffn[3:4],
        s_mix[0:3].reshape(1, -1),
        s_ffn[0:3].reshape(1, -1),
        s_fin[3:4, :LANES],
    ]
    sizes = [a.shape[1] for a in small]
    offs = [0]
    for n in sizes:
        offs.append(offs[-1] + n)
    pack_g = _pad_to(jnp.concatenate(small, axis=1).reshape(-1), SUBLANES * LANES * SUBLANES).reshape(SUBLANES, -1)
    gathered = _all_gather8(pack_g, "ag8_small_grads")
    tot = _sum_devices(gathered).reshape(-1)
    part_of = lambda k: tot[offs[k] : offs[k + 1]]
    dmod_all = gathered.reshape(N_DEV, -1)[:, : N_MOD * d]
    loss = part_of(11)[0]

    g_b_ada = part_of(0).reshape(1, -1)
    g_vecs = [part_of(k).reshape(1, -1) for k in range(1, 9)]
    g_cw_mix = lax.dynamic_slice(part_of(9).reshape(CONV_K, cwid), (0, chip * (cwid // N_CHIP)), (CONV_K, cwid // N_CHIP))
    g_cw_ffn = lax.dynamic_slice(part_of(10).reshape(CONV_K, f2), (0, chip * (f2 // N_CHIP)), (CONV_K, f2 // N_CHIP))

    swap = _swap_start(part, "swap_start")
    dm_cols = _behind(lax.dynamic_slice(dmod_all, (0, chip * nc_ada), (N_DEV, nc_ada)), swap)
    g_w_ada = _ada_grad(
        jnp.pad(c_act.T, ((0, 0), (0, LANES - N_DEV))), jnp.pad(dm_cols, ((0, LANES - N_DEV), (0, 0)))
    )
    big = {"w_ada": [a[None] for a in _adamw(w_ada[0], m_w_ada[0], v_w_ada[0], [g_w_ada], "adamw_w_ada")]}
    part, other = _swap_wait(swap, big["w_ada"][1], "swap_wait")

    big_w = [w_in, w_uq, w_ukv, w_o, w_up, w_down]
    big_m = [m_w_in, m_w_uq, m_w_ukv, m_w_o, m_w_up, m_w_down]
    big_v = [v_w_in, v_w_uq, v_w_ukv, v_w_o, v_w_up, v_w_down]
    for n, w_, m_, v_, p_, o_ in zip(names, big_w, big_m, big_v, part, other):
        big[n] = [a[None] for a in _adamw(w_[0], m_[0], v_[0], [p_, o_], "adamw_" + n)]

    sm_names = ["b_ada", "g_pre_mix", "g_post_mix", "g_q", "g_kv", "conv_b_mix", "g_pre_ffn", "g_post_ffn", "conv_b_ffn",
                "conv_w_mix", "conv_w_ffn"]
    sm_w = [b_ada, g_pre_mix, g_post_mix, g_q, g_kv, conv_b_mix, g_pre_ffn, g_post_ffn, conv_b_ffn, conv_w_mix, conv_w_ffn]
    sm_m = [m_b_ada, m_g_pre_mix, m_g_post_mix, m_g_q, m_g_kv, m_conv_b_mix, m_g_pre_ffn, m_g_post_ffn, m_conv_b_ffn,
            m_conv_w_mix, m_conv_w_ffn]
    sm_v = [v_b_ada, v_g_pre_mix, v_g_post_mix, v_g_q, v_g_kv, v_conv_b_mix, v_g_pre_ffn, v_g_post_ffn, v_conv_b_ffn,
            v_conv_w_mix, v_conv_w_ffn]
    sm_g = [g_b_ada] + g_vecs + [g_cw_mix, g_cw_ffn]
    flat = lambda arrs: jnp.concatenate([a.reshape(1, -1) for a in arrs], axis=1)
    sm_out = _adamw(flat(sm_w), flat(sm_m), flat(sm_v), [flat(sm_g)], "adamw_small")
    sm = {}
    off = 0
    for n, w_ in zip(sm_names, sm_w):
        sm[n] = [o[:, off : off + w_.size].reshape(w_.shape) for o in sm_out]
        off += w_.size

    order = ["w_ada", "b_ada", "g_pre_mix", "g_post_mix", "w_in", "g_q", "w_uq", "g_kv", "w_ukv", "conv_w_mix", "conv_b_mix",
             "w_o", "g_pre_ffn", "g_post_ffn", "w_up", "conv_w_ffn", "conv_b_ffn", "w_down"]
    res = {**big, **sm}
    outs = [loss + sum(anchors), grad_x.reshape(x.shape)]
    for k in range(4):
        outs += [res[n][k] for n in order]
    return tuple(outs)
```

```python
import math

import jax
import jax.numpy as jnp
from jax import lax
from jax.experimental import pallas as pl
from jax.experimental.pallas import tpu as pltpu

F32 = jnp.float32
BF16 = jnp.bfloat16
MESH = pl.DeviceIdType.MESH

N_DEV = 8
N_CHIP = 4
LANES = 128
SUBLANES = 8
VMEM_LIMIT = 56 * 2**20

NOPE = 128
ROPE = 64
VDIM = 128
HEAD_PAD = 128
ROPE_THETA = 10000.0
RMS_EPS = 1e-6
N_MOD = 6
CONV_K = 3
ATT_FWD_BLOCK, ATT_FWD_SUB = 2048, 256
ATT_BWD_BLOCK, ATT_BWD_SUB = 1024, 256
NEG = -1e30

ADAM_LR = 0.001
ADAM_B1 = 0.9
ADAM_B2 = 0.999
ADAM_EPS = 1e-08
ADAM_WD = 0.01
ADAM_STEP = 10


def _tile(n, pref, align):
    if n <= pref:
        return n
    t = (pref // align) * align
    while t >= align:
        if n % t == 0:
            return t
        t -= align
    return n


def _cp(*sem):
    return pltpu.CompilerParams(dimension_semantics=sem, vmem_limit_bytes=VMEM_LIMIT)


def _rsq(x):
    return lax.rsqrt(jnp.mean(x * x, axis=-1, keepdims=True) + RMS_EPS)


def _norm_bwd(dn, n, r):
    return r * (dn - n * jnp.mean(dn * n, axis=-1, keepdims=True))


def _colsum(a):
    return jnp.sum(a, axis=0, keepdims=True)


def _matmul(a, b, *, ta=False, tb=False, out_dtype, tm, tn, tk, name, b_n_perm=None, out_n_perm=None,
            b_col_shards=False, out_col_shards=False):
    assert not (b_col_shards and tb)
    if b_col_shards:
        b_rows, b_cols = b.shape[1], N_CHIP * b.shape[2]
    else:
        b_rows, b_cols = b.shape
    (k_a, m) = a.shape if ta else a.shape[::-1]
    (n, k_b) = (b_rows, b_cols) if tb else (b_cols, b_rows)
    assert k_a == k_b, (a.shape, b.shape, ta, tb)
    tm, tn, tk = _tile(m, tm, LANES), _tile(n, tn, LANES), _tile(k_a, tk, LANES)
    nk = k_a // tk
    same = lambda t: t
    bn, on = b_n_perm or same, out_n_perm or same
    a_spec = pl.BlockSpec((tk, tm), lambda i, j, k: (k, i)) if ta else pl.BlockSpec((tm, tk), lambda i, j, k: (i, k))
    if b_col_shards:
        per = (b_cols // N_CHIP) // tn
        b_spec = pl.BlockSpec((None, tk, tn), lambda i, j, k: (bn(j) // per, k, bn(j) % per))
    elif tb:
        b_spec = pl.BlockSpec((tn, tk), lambda i, j, k: (bn(j), k))
    else:
        b_spec = pl.BlockSpec((tk, tn), lambda i, j, k: (k, bn(j)))
    if out_col_shards:
        per_o = (n // N_CHIP) // tn
        out_shape = jax.ShapeDtypeStruct((N_CHIP, m, n // N_CHIP), out_dtype)
        out_spec = pl.BlockSpec((None, tm, tn), lambda i, j, k: (on(j) // per_o, i, on(j) % per_o))
    else:
        out_shape = jax.ShapeDtypeStruct((m, n), out_dtype)
        out_spec = pl.BlockSpec((tm, tn), lambda i, j, k: (i, on(j)))
    dims = (((0 if ta else 1,), (1 if tb else 0,)), ((), ()))

    def body(a_ref, b_ref, o_ref, *acc):
        p = lax.dot_general(a_ref[...].astype(BF16), b_ref[...].astype(BF16), dims, preferred_element_type=F32)
        _accumulate(p, o_ref, acc, nk)

    return pl.pallas_call(
        body,
        name=name,
        out_shape=out_shape,
        grid=(m // tm, n // tn, nk),
        in_specs=[a_spec, b_spec],
        out_specs=out_spec,
        scratch_shapes=[] if nk == 1 else [pltpu.VMEM((tm, tn), F32)],
        compiler_params=_cp("parallel", "parallel", "arbitrary"),
    )(a, b)


def _accumulate(p, o_ref, acc, nk):
    if nk == 1:
        o_ref[...] = p.astype(o_ref.dtype)
        return
    k = pl.program_id(2)

    @pl.when(k == 0)
    def _():
        acc[0][...] = p

    @pl.when(k > 0)
    def _():
        acc[0][...] += p

    @pl.when(k == nk - 1)
    def _():
        o_ref[...] = acc[0][...].astype(o_ref.dtype)


def _matmul_pair_k(a, b_shards, *, out_dtype, tm, tn, name):
    m, f2 = a.shape
    n = b_shards.shape[1]
    tc = _pair_tile(f2 // 2)
    nk = (f2 // 2) // tc
    per = (f2 // N_CHIP) // tc
    tm, tn = _tile(m, tm, LANES), _tile(n, tn, LANES)

    def body(a_ref, ba_ref, bg_ref, o_ref, *acc):
        av = a_ref[...]
        p = lax.dot_general(av[:, :tc], ba_ref[...], NT, preferred_element_type=F32)
        p = p + lax.dot_general(av[:, tc:], bg_ref[...], NT, preferred_element_type=F32)
        _accumulate(p, o_ref, acc, nk)

    def w_tile(first):
        return pl.BlockSpec((None, tn, tc), lambda i, j, k: ((first + k) // per, j, (first + k) % per))

    return pl.pallas_call(
        body,
        name=name,
        out_shape=jax.ShapeDtypeStruct((m, n), out_dtype),
        grid=(m // tm, n // tn, nk),
        in_specs=[pl.BlockSpec((tm, 2 * tc), lambda i, j, k: (i, k)), w_tile(0), w_tile(nk)],
        out_specs=pl.BlockSpec((tm, tn), lambda i, j, k: (i, j)),
        scratch_shapes=[] if nk == 1 else [pltpu.VMEM((tm, tn), F32)],
        compiler_params=_cp("parallel", "parallel", "arbitrary"),
    )(a, b_shards, b_shards)


def _rope_tables(pos_col, invf):
    s = pos_col.shape[0]
    ts = _tile(s, 1024, SUBLANES)
    half = ROPE // 2

    def body(p_ref, f_ref, c_ref, sa_ref, sb_ref):
        ang = p_ref[...] * f_ref[...]
        lane = lax.broadcasted_iota(jnp.int32, ang.shape, 1)
        cs, sn = jnp.cos(ang), jnp.sin(ang)
        c_ref[...] = jnp.where(lane < ROPE, cs, 0.0)
        sa_ref[...] = jnp.where((lane >= half) & (lane < ROPE), sn, 0.0)
        sb_ref[...] = jnp.where(lane < half, -sn, 0.0)

    tab = jax.ShapeDtypeStruct((s, LANES), F32)
    return pl.pallas_call(
        body,
        name="rope_tables",
        out_shape=(tab, tab, tab),
        grid=(s // ts,),
        in_specs=[pl.BlockSpec((ts, 1), lambda i: (i, 0)), pl.BlockSpec((1, LANES), lambda i: (0, 0))],
        out_specs=[pl.BlockSpec((ts, LANES), lambda i: (i, 0))] * 3,
        compiler_params=_cp("parallel"),
    )(pos_col, invf)


def _widen(t, w):
    return t if w == LANES else jnp.tile(t, (1, w // LANES))


def _rope(x, c, sa, sb):
    w = x.shape[1]
    c, sa, sb = _widen(c, w), _widen(sa, w), _widen(sb, w)
    return x * c + pltpu.roll(x, ROPE // 2, 1) * sa + pltpu.roll(x, w - ROPE // 2, 1) * sb


def _rope_t(d, c, sa, sb):
    w = d.shape[1]
    c, sa, sb = _widen(c, w), _widen(sa, w), _widen(sb, w)
    return d * c + pltpu.roll(d * sa, w - ROPE // 2, 1) + pltpu.roll(d * sb, ROPE // 2, 1)


def _ada_fwd(c_all, w, b):
    d, nc = w.shape
    tn = _tile(nc, 512, LANES)

    def body(c_ref, w_ref, b_ref, o_ref, ca_ref):
        cv = c_ref[...]
        ca = cv * jax.nn.sigmoid(cv)
        ca_ref[...] = ca
        o_ref[...] = jnp.dot(ca.astype(BF16), w_ref[...].astype(BF16), preferred_element_type=F32) + b_ref[...]

    return pl.pallas_call(
        body,
        name="ada_fwd",
        out_shape=(jax.ShapeDtypeStruct((N_DEV, nc), F32), jax.ShapeDtypeStruct((N_DEV, d), F32)),
        grid=(nc // tn,),
        in_specs=[
            pl.BlockSpec((N_DEV, d), lambda j: (0, 0)),
            pl.BlockSpec((d, tn), lambda j: (0, j)),
            pl.BlockSpec((1, tn), lambda j: (0, j)),
        ],
        out_specs=[pl.BlockSpec((N_DEV, tn), lambda j: (0, j)), pl.BlockSpec((N_DEV, d), lambda j: (0, 0))],
        compiler_params=_cp("arbitrary"),
    )(c_all, w, b)


def _rows(ts, d):
    return pl.BlockSpec((ts, d), lambda i: (i, 0))


def _vec(d):
    return pl.BlockSpec((1, d), lambda i: (0, 0))


def _sums(d):
    return pl.BlockSpec((SUBLANES, d), lambda i: (0, 0))


def _acc_rows(ref, i, rows):
    @pl.when(i == 0)
    def _():
        ref[...] = jnp.zeros(ref.shape, ref.dtype)

    for k, r in enumerate(rows):
        ref[k : k + 1, :] += r


def _pre_fwd(x, g, sc, sh):
    s, d = x.shape
    ts = _tile(s, 512, SUBLANES)

    def body(x_ref, g_ref, sc_ref, sh_ref, h_ref):
        xv = x_ref[...]
        h_ref[...] = (((xv * _rsq(xv)) * g_ref[...]) * (1.0 + sc_ref[...]) + sh_ref[...]).astype(BF16)

    return pl.pallas_call(
        body,
        name="pre_mix_fwd",
        out_shape=jax.ShapeDtypeStruct((s, d), BF16),
        grid=(s // ts,),
        in_specs=[_rows(ts, d), _vec(d), _vec(d), _vec(d)],
        out_specs=_rows(ts, d),
        compiler_params=_cp("parallel"),
    )(x, g, sc, sh)


def _mid_fwd(x0, mix, g_post, gt, g_pre, sc, sh):
    s, d = x0.shape
    ts = _tile(s, 256, SUBLANES)

    def body(x_ref, m_ref, gp_ref, gt_ref, g_ref, sc_ref, sh_ref, x1_ref, h_ref):
        mv = m_ref[...]
        x1 = x_ref[...] + gt_ref[...] * ((mv * _rsq(mv)) * gp_ref[...])
        x1_ref[...] = x1
        h_ref[...] = (((x1 * _rsq(x1)) * g_ref[...]) * (1.0 + sc_ref[...]) + sh_ref[...]).astype(BF16)

    return pl.pallas_call(
        body,
        name="mid_fwd",
        out_shape=(jax.ShapeDtypeStruct((s, d), F32), jax.ShapeDtypeStruct((s, d), BF16)),
        grid=(s // ts,),
        in_specs=[_rows(ts, d), _rows(ts, d)] + [_vec(d)] * 5,
        out_specs=[_rows(ts, d), _rows(ts, d)],
        compiler_params=_cp("parallel"),
    )(x0, mix, g_post, gt, g_pre, sc, sh)


def _final(x1, y, tgt, g_post, gt):
    s, d = x1.shape
    ts = _tile(s, 256, SUBLANES)
    ni = s // ts

    def body(x_ref, y_ref, t_ref, gp_ref, gt_ref, dx_ref, dy_ref, s_ref):
        i = pl.program_id(0)
        yv, gp, gt_v = y_ref[...], gp_ref[...], gt_ref[...]
        r = _rsq(yv)
        n = yv * r
        err = (x_ref[...] + gt_v * (n * gp)) - t_ref[...]
        dx = err * (1.0 / d)
        dx_ref[...] = dx
        dy_ref[...] = _norm_bwd(dx * (gt_v * gp), n, r).astype(BF16)
        _acc_rows(s_ref, i, [_colsum(dx * (n * gp)), _colsum(dx * gt_v * n), _colsum(err * err)])

        @pl.when(i == ni - 1)
        def _():
            tot = jnp.sum(s_ref[2:3, :], axis=1, keepdims=True) * (0.5 / d)
            s_ref[3:4, :] = jnp.broadcast_to(tot, (1, d))

    return pl.pallas_call(
        body,
        name="final_fwd_bwd",
        out_shape=(
            jax.ShapeDtypeStruct((s, d), F32),
            jax.ShapeDtypeStruct((s, d), BF16),
            jax.ShapeDtypeStruct((SUBLANES, d), F32),
        ),
        grid=(ni,),
        in_specs=[_rows(ts, d)] * 3 + [_vec(d)] * 2,
        out_specs=[_rows(ts, d), _rows(ts, d), _sums(d)],
        compiler_params=_cp("arbitrary"),
    )(x1, y, tgt, g_post, gt)


def _mid_bwd(dh2, dx2, x1, mix, g_pre, sc, g_post, gt):
    s, d = x1.shape
    ts = _tile(s, 256, SUBLANES)

    def body(dh_ref, dx2_ref, x_ref, m_ref, g_ref, sc_ref, gp_ref, gt_ref, dx1_ref, dm_ref, s_ref):
        i = pl.program_id(0)
        dh, xv, mv = dh_ref[...], x_ref[...], m_ref[...]
        g, sc_v, gp, gt_v = g_ref[...], sc_ref[...], gp_ref[...], gt_ref[...]
        r1 = _rsq(xv)
        n1 = xv * r1
        dx1 = dx2_ref[...] + _norm_bwd(dh * (g * (1.0 + sc_v)), n1, r1)
        dx1_ref[...] = dx1
        rm = _rsq(mv)
        nm = mv * rm
        dm_ref[...] = _norm_bwd(dx1 * (gt_v * gp), nm, rm).astype(BF16)
        _acc_rows(
            s_ref,
            i,
            [
                _colsum(dh),
                _colsum(dh * (n1 * g)),
                _colsum(dh * (1.0 + sc_v) * n1),
                _colsum(dx1 * (nm * gp)),
                _colsum(dx1 * gt_v * nm),
            ],
        )

    return pl.pallas_call(
        body,
        name="mid_bwd",
        out_shape=(
            jax.ShapeDtypeStruct((s, d), F32),
            jax.ShapeDtypeStruct((s, d), BF16),
            jax.ShapeDtypeStruct((SUBLANES, d), F32),
        ),
        grid=(s // ts,),
        in_specs=[_rows(ts, d)] * 4 + [_vec(d)] * 4,
        out_specs=[_rows(ts, d), _rows(ts, d), _sums(d)],
        compiler_params=_cp("arbitrary"),
    )(dh2, dx2, x1, mix, g_pre, sc, g_post, gt)


def _first_bwd(dh1, dx1, x0, g, sc):
    s, d = x0.shape
    ts = _tile(s, 256, SUBLANES)

    def body(dh_ref, dx1_ref, x_ref, g_ref, sc_ref, dx_ref, s_ref):
        i = pl.program_id(0)
        dh, xv, gv, sc_v = dh_ref[...], x_ref[...], g_ref[...], sc_ref[...]
        r = _rsq(xv)
        n = xv * r
        dx_ref[...] = dx1_ref[...] + _norm_bwd(dh * (gv * (1.0 + sc_v)), n, r)
        _acc_rows(s_ref, i, [_colsum(dh), _colsum(dh * (n * gv)), _colsum(dh * (1.0 + sc_v) * n)])

    return pl.pallas_call(
        body,
        name="first_bwd",
        out_shape=(jax.ShapeDtypeStruct((s, d), F32), jax.ShapeDtypeStruct((SUBLANES, d), F32)),
        grid=(s // ts,),
        in_specs=[_rows(ts, d)] * 3 + [_vec(d)] * 2,
        out_specs=[_rows(ts, d), _sums(d)],
        compiler_params=_cp("arbitrary"),
    )(dh1, dx1, x0, g, sc)


def _latent_fwd(proj, g_q, g_kv, tabs, lb):
    s = proj.shape[0]
    ql, kl = g_q.shape[1], g_kv.shape[1]
    ts = _tile(s, 512, SUBLANES)

    def body(p_ref, gq_ref, gk_ref, c_ref, sa_ref, sb_ref, q_ref, kv_ref, kr_ref):
        pv = p_ref[...]
        q, kv, kr = pv[:, :ql], pv[:, ql : ql + kl], pv[:, ql + kl : ql + kl + HEAD_PAD]
        q_ref[...] = ((q * _rsq(q)) * gq_ref[...]).astype(BF16)
        kv_ref[...] = ((kv * _rsq(kv)) * gk_ref[...]).astype(BF16)
        kr_ref[...] = _rope(kr, c_ref[...], sa_ref[...], sb_ref[...]).astype(BF16)

    return pl.pallas_call(
        body,
        name="latent_fwd",
        out_shape=(
            jax.ShapeDtypeStruct((s, ql), BF16),
            jax.ShapeDtypeStruct((s, kl), BF16),
            jax.ShapeDtypeStruct((s, HEAD_PAD), BF16),
        ),
        grid=(s // ts,),
        in_specs=[_rows(ts, lb), _vec(ql), _vec(kl)] + [_rows(ts, LANES)] * 3,
        out_specs=[_rows(ts, ql), _rows(ts, kl), _rows(ts, HEAD_PAD)],
        compiler_params=_cp("parallel"),
    )(proj, g_q, g_kv, *tabs)


def _latent_bwd(proj, dqn, dkvn, dkr_h, g_q, g_kv, tabs, lb):
    s = proj.shape[0]
    ql, kl = g_q.shape[1], g_kv.shape[1]
    hw = dkr_h.shape[1]
    ts = _tile(s, 256, SUBLANES)
    pad = lb - ql - kl - HEAD_PAD

    def body(p_ref, dq_ref, dkv_ref, dkr_ref, gq_ref, gk_ref, c_ref, sa_ref, sb_ref, o_ref, s_ref):
        i = pl.program_id(0)
        pv = p_ref[...]
        q, kv = pv[:, :ql], pv[:, ql : ql + kl]
        dqn_v, dkvn_v = dq_ref[...], dkv_ref[...]
        rq = _rsq(q)
        nq = q * rq
        rk = _rsq(kv)
        nk = kv * rk
        dkr = dkr_ref[:, :HEAD_PAD]
        for h in range(1, hw // HEAD_PAD):
            dkr = dkr + dkr_ref[:, h * HEAD_PAD : (h + 1) * HEAD_PAD]
        parts = [
            _norm_bwd(dqn_v * gq_ref[...], nq, rq).astype(BF16),
            _norm_bwd(dkvn_v * gk_ref[...], nk, rk).astype(BF16),
            _rope_t(dkr, c_ref[...], sa_ref[...], sb_ref[...]).astype(BF16),
        ]
        if pad:
            parts.append(jnp.zeros((ts, pad), BF16))
        o_ref[...] = jnp.concatenate(parts, axis=1)
        row = [_colsum(dqn_v * nq), _colsum(dkvn_v * nk), jnp.zeros((1, lb - ql - kl), F32)]
        _acc_rows(s_ref, i, [jnp.concatenate(row, axis=1)])

    return pl.pallas_call(
        body,
        name="latent_bwd",
        out_shape=(jax.ShapeDtypeStruct((s, lb), BF16), jax.ShapeDtypeStruct((SUBLANES, lb), F32)),
        grid=(s // ts,),
        in_specs=[_rows(ts, lb), _rows(ts, ql), _rows(ts, kl), _rows(ts, hw)]
        + [_vec(ql), _vec(kl)]
        + [_rows(ts, LANES)] * 3,
        out_specs=[_rows(ts, lb), _sums(lb)],
        compiler_params=_cp("arbitrary"),
    )(proj, dqn, dkvn, dkr_h, g_q, g_kv, *tabs)


def _conv3(ext, w, b):
    return (pltpu.roll(ext, 2, 0) * w[0:1] + pltpu.roll(ext, 1, 0) * w[1:2]) + ext * w[2:3] + b


def _conv3_t(du, w):
    n = du.shape[0]
    return du * w[2:3] + pltpu.roll(du, n - 1, 0) * w[1:2] + pltpu.roll(du, n - 2, 0) * w[0:1]


def _halo_maps(ts, s):
    r8, last = ts // SUBLANES, s // SUBLANES - 1
    prev = lambda i: jnp.maximum(i * r8 - 1, 0)
    nxt = lambda i: jnp.minimum((i + 1) * r8, last)
    return prev, nxt


def _mixer_fwd(cat, proj, cw, cb, lb, col0):
    s = proj.shape[0]
    cwid = cw.shape[1]
    ts = _tile(s, 512, SUBLANES)
    tc = _tile(cwid, 512, LANES)
    assert lb % tc == 0 and col0 % tc == 0
    nj, ob, oc = cwid // tc, lb // tc, col0 // tc
    prev, _ = _halo_maps(ts, s)

    def body(_, gb_ref, gc_ref, ci_ref, pgc_ref, pci_ref, w_ref, b_ref, o_ref):
        keep = jnp.where(pl.program_id(1) > 0, 1.0, 0.0)
        ext = jnp.concatenate([pgc_ref[...] * pci_ref[...] * keep, gc_ref[...] * ci_ref[...]], axis=0)
        o_ref[...] = (gb_ref[...] * _conv3(ext, w_ref[...], b_ref[...])[SUBLANES:]).astype(BF16)

    def col(k):
        return pl.BlockSpec((ts, tc), lambda j, i: (i, ob + k * nj + j))

    def halo(k):
        return pl.BlockSpec((SUBLANES, tc), lambda j, i: (prev(i), ob + k * nj + j))

    return pl.pallas_call(
        body,
        name="mixer_fwd",
        out_shape=jax.ShapeDtypeStruct(cat.shape, BF16),
        grid=(nj, s // ts),
        in_specs=[pl.BlockSpec(memory_space=pl.ANY), col(0), col(1), col(2), halo(1), halo(2)]
        + [pl.BlockSpec((CONV_K, tc), lambda j, i: (0, j)), pl.BlockSpec((1, tc), lambda j, i: (0, j))],
        out_specs=pl.BlockSpec((ts, tc), lambda j, i: (i, oc + j)),
        input_output_aliases={0: 0},
        compiler_params=_cp("parallel", "arbitrary"),
    )(cat, proj, proj, proj, proj, proj, cw, cb)


def _mixer_bwd(dcat, proj, cw, cb, lb, col0):
    s = proj.shape[0]
    cwid = cw.shape[1]
    ts = _tile(s, 256, SUBLANES)
    tc = _tile(cwid, 512, LANES)
    nj, ob, oc = cwid // tc, lb // tc, col0 // tc
    ni = s // ts
    prev, nxt = _halo_maps(ts, s)

    def body(d_ref, dn_ref, gb_ref, gbn_ref, gc_ref, gcp_ref, gcn_ref, ci_ref, cip_ref, cin_ref, w_ref, b_ref,
             dgb_ref, dgc_ref, dci_ref, s_ref):
        i = pl.program_id(1)
        keep_p = jnp.where(i > 0, 1.0, 0.0)
        keep_n = jnp.where(i < ni - 1, 1.0, 0.0)
        w = w_ref[...]
        gc = jnp.concatenate([gcp_ref[...], gc_ref[...], gcn_ref[...]], axis=0)
        ci = jnp.concatenate([cip_ref[...] * keep_p, ci_ref[...], cin_ref[...]], axis=0)
        u = gc * ci
        cv = _conv3(u, w, b_ref[...])[SUBLANES:]
        dco = jnp.concatenate([d_ref[...], dn_ref[...] * keep_n], axis=0)
        gb = jnp.concatenate([gb_ref[...], gbn_ref[...]], axis=0)
        dgb_ref[...] = (dco * cv)[:ts].astype(BF16)
        dcv = dco * gb
        du = _conv3_t(dcv, w)[:ts]
        dgc_ref[...] = (du * ci_ref[...]).astype(BF16)
        dci_ref[...] = (du * gc_ref[...]).astype(BF16)
        dt = dcv[:ts]
        u1, u2 = pltpu.roll(u, 1, 0), pltpu.roll(u, 2, 0)
        lo, hi = SUBLANES, SUBLANES + ts
        _acc_rows(s_ref, i, [_colsum(dt * u2[lo:hi]), _colsum(dt * u1[lo:hi]), _colsum(dt * u[lo:hi]), _colsum(dt)])

    def col(k):
        return pl.BlockSpec((ts, tc), lambda j, i: (i, ob + k * nj + j))

    def halo(k, which):
        return pl.BlockSpec((SUBLANES, tc), lambda j, i: (which(i), ob + k * nj + j))

    out_col = [pl.BlockSpec((ts, tc), lambda j, i: (i, j))] * 3
    grad = jax.ShapeDtypeStruct((s, cwid), BF16)
    return pl.pallas_call(
        body,
        name="mixer_bwd",
        out_shape=(grad, grad, grad, jax.ShapeDtypeStruct((SUBLANES, cwid), F32)),
        grid=(nj, ni),
        in_specs=[
            pl.BlockSpec((ts, tc), lambda j, i: (i, oc + j)),
            pl.BlockSpec((SUBLANES, tc), lambda j, i: (nxt(i), oc + j)),
            col(0), halo(0, nxt),
            col(1), halo(1, prev), halo(1, nxt),
            col(2), halo(2, prev), halo(2, nxt),
            pl.BlockSpec((CONV_K, tc), lambda j, i: (0, j)),
            pl.BlockSpec((1, tc), lambda j, i: (0, j)),
        ],
        out_specs=out_col + [pl.BlockSpec((SUBLANES, tc), lambda j, i: (0, j))],
        compiler_params=_cp("parallel", "arbitrary"),
    )(dcat, dcat, proj, proj, proj, proj, proj, proj, proj, proj, cw, cb)


def _pair_tile(f):
    return _tile(f, 1408, LANES)


def _pair_perm(f):
    nj = f // _pair_tile(f)
    return lambda p: (p % 2) * nj + p // 2


def _pair_cols(a):
    r, f2 = a.shape
    tc = _pair_tile(f2 // 2)
    return a.reshape(r, 2, f2 // (2 * tc), tc).transpose(0, 2, 1, 3).reshape(r, f2)


def _unpair_cols(a):
    r, f2 = a.shape
    tc = _pair_tile(f2 // 2)
    return a.reshape(r, f2 // (2 * tc), 2, tc).transpose(0, 2, 1, 3).reshape(r, f2)


def _ffn_act_fwd(up, cw, cb):
    s, f2 = up.shape
    f = f2 // 2
    ts = _tile(s, 256, SUBLANES)
    tc = _pair_tile(f)
    prev, _ = _halo_maps(ts, s)

    def body(u_ref, p_ref, w_ref, b_ref, o_ref):
        keep = jnp.where(pl.program_id(1) > 0, 1.0, 0.0)
        ext = jnp.concatenate([p_ref[...] * keep, u_ref[...]], axis=0)
        u = _conv3(ext, w_ref[...], b_ref[...])[SUBLANES:]
        a, g = u[:, :tc], u[:, tc:]
        o_ref[...] = ((g * jax.nn.sigmoid(g)) * a).astype(BF16)

    def pair(rows, which):
        return pl.BlockSpec((rows, 2 * tc), lambda j, i: (which(i), j))

    return pl.pallas_call(
        body,
        name="ffn_act_fwd",
        out_shape=jax.ShapeDtypeStruct((s, f), BF16),
        grid=(f // tc, s // ts),
        in_specs=[pair(ts, lambda i: i), pair(SUBLANES, prev), pair(CONV_K, lambda i: 0), pair(1, lambda i: 0)],
        out_specs=pl.BlockSpec((ts, tc), lambda j, i: (i, j)),
        compiler_params=_cp("parallel", "arbitrary"),
    )(up, up, cw, cb)


def _ffn_act_bwd(dact, up, cw, cb):
    s, f2 = up.shape
    f = f2 // 2
    ts = _tile(s, 128, SUBLANES)
    tc = _pair_tile(f)
    nj, ni = f // tc, s // ts
    prev, nxt = _halo_maps(ts, s)

    def body(d_ref, dn_ref, u_ref, up_ref, un_ref, w_ref, b_ref, dup_ref, s_ref):
        i = pl.program_id(1)
        keep_p = jnp.where(i > 0, 1.0, 0.0)
        keep_n = jnp.where(i < ni - 1, 1.0, 0.0)
        w = w_ref[...]
        ext = jnp.concatenate([up_ref[...] * keep_p, u_ref[...], un_ref[...]], axis=0)
        u = _conv3(ext, w, b_ref[...])[SUBLANES:]
        a, g = u[:, :tc], u[:, tc:]
        dact_v = jnp.concatenate([d_ref[...], dn_ref[...] * keep_n], axis=0)
        sg = jax.nn.sigmoid(g)
        du = jnp.concatenate([dact_v * (g * sg), dact_v * a * (sg * (1.0 + g * (1.0 - sg)))], axis=1)
        dup_ref[...] = _conv3_t(du, w)[:ts].astype(BF16)
        dt = du[:ts]
        lo, hi = SUBLANES, SUBLANES + ts
        e1, e2 = pltpu.roll(ext, 1, 0), pltpu.roll(ext, 2, 0)
        _acc_rows(s_ref, i, [_colsum(dt * e2[lo:hi]), _colsum(dt * e1[lo:hi]), _colsum(dt * ext[lo:hi]), _colsum(dt)])

    def pair(rows, which):
        return pl.BlockSpec((rows, 2 * tc), lambda j, i: (which(i), j))

    return pl.pallas_call(
        body,
        name="ffn_act_bwd",
        out_shape=(jax.ShapeDtypeStruct((s, f2), BF16), jax.ShapeDtypeStruct((SUBLANES, f2), F32)),
        grid=(nj, ni),
        in_specs=[
            pl.BlockSpec((ts, tc), lambda j, i: (i, j)),
            pl.BlockSpec((SUBLANES, tc), lambda j, i: (nxt(i), j)),
            pair(ts, lambda i: i), pair(SUBLANES, prev), pair(SUBLANES, nxt),
            pair(CONV_K, lambda i: 0), pair(1, lambda i: 0),
        ],
        out_specs=[pair(ts, lambda i: i), pair(SUBLANES, lambda i: 0)],
        compiler_params=_cp("parallel", "arbitrary"),
    )(dact, dact, up, up, up, cw, cb)


ATT_SCALE = 1.0 / math.sqrt(NOPE + ROPE)
LOG2E = math.log2(math.e)
ATT_C2 = ATT_SCALE * LOG2E
STAT_SPLIT = 64
NT = (((1,), (1,)), ((), ()))
TN = (((0,), (0,)), ((), ()))


def _head_cat(q, kv, kr, tabs, n_heads):
    s, w2 = q.shape
    w = w2 // 2
    ts = _tile(s, 512, SUBLANES)
    hd = NOPE + HEAD_PAD

    def body(q_ref, kv_ref, kr_ref, c_ref, sa_ref, sb_ref, qc_ref, kc_ref):
        qv = q_ref[...]
        qr = _rope(qv[:, w:], c_ref[...], sa_ref[...], sb_ref[...]).astype(BF16)
        krv = kr_ref[...]
        for h in range(n_heads):
            qc_ref[:, h * hd : h * hd + NOPE] = qv[:, h * NOPE : (h + 1) * NOPE].astype(BF16)
            qc_ref[:, h * hd + NOPE : (h + 1) * hd] = qr[:, h * HEAD_PAD : (h + 1) * HEAD_PAD]
            kc_ref[:, h * hd : h * hd + NOPE] = kv_ref[:, h * NOPE : (h + 1) * NOPE]
            kc_ref[:, h * hd + NOPE : (h + 1) * hd] = krv

    out = jax.ShapeDtypeStruct((s, n_heads * hd), BF16)
    return pl.pallas_call(
        body,
        name="head_cat",
        out_shape=(out, out),
        grid=(s // ts,),
        in_specs=[_rows(ts, w2), _rows(ts, w), _rows(ts, HEAD_PAD)] + [_rows(ts, LANES)] * 3,
        out_specs=[_rows(ts, n_heads * hd)] * 2,
        compiler_params=_cp("parallel"),
    )(q, kv, kr, *tabs)


def _attn_fwd(qc, kc, kv, n_heads, cat_cols):
    s = qc.shape[0]
    t = _tile(s, ATT_FWD_BLOCK, LANES)
    sub = _tile(t, ATT_FWD_SUB, LANES)
    hh = n_heads
    hd = NOPE + HEAD_PAD

    def body(q_ref, k_ref, v_ref, o_ref, lse_ref, m_s, l_s, acc_s):
        i = pl.program_id(1)
        m_s[...] = jnp.full(m_s.shape, NEG, F32)
        l_s[...] = jnp.zeros(l_s.shape, F32)
        acc_s[...] = jnp.zeros(acc_s.shape, F32)

        def chunk(k0, diag):
            m_all, l_all, acc_all = m_s[...], l_s[...], acc_s[...]
            new_m, new_l, new_acc = [], [], []

            def scores(r0):
                ncol = r0 + sub if diag else t
                return lax.dot_general(q_ref[pl.ds(r0, sub), :], k_ref[pl.ds(k0, ncol), :], NT, preferred_element_type=F32)

            sc_next = scores(0)
            for r0 in range(0, t, sub):
                ncol = r0 + sub if diag else t
                sc = sc_next
                if r0 + sub < t:
                    sc_next = scores(r0 + sub)
                if diag:
                    row = lax.broadcasted_iota(jnp.int32, sc.shape, 0) + r0
                    col = lax.broadcasted_iota(jnp.int32, sc.shape, 1)
                    sc = jnp.where(col <= row, sc, NEG)
                m_prev = m_all[r0 : r0 + sub]
                m_new = jnp.maximum(m_prev, jnp.max(sc, axis=1, keepdims=True))
                alpha = jnp.exp2((m_prev - m_new) * ATT_C2)
                p = jnp.exp2((sc - m_new) * ATT_C2)
                pv = jnp.dot(p.astype(BF16), v_ref[pl.ds(k0, ncol), :], preferred_element_type=F32)
                new_m.append(m_new)
                new_l.append(alpha * l_all[r0 : r0 + sub] + jnp.sum(p, axis=1, keepdims=True))
                new_acc.append(alpha * acc_all[r0 : r0 + sub] + pv)
            m_s[...] = jnp.concatenate(new_m, axis=0)
            l_s[...] = jnp.concatenate(new_l, axis=0)
            acc_s[...] = jnp.concatenate(new_acc, axis=0)

        def loop_body(k, carry):
            chunk(pl.multiple_of(k * t, t), False)
            return carry

        lax.fori_loop(0, i, loop_body, 0)
        chunk(pl.multiple_of(i * t, t), True)
        l = l_s[...]
        o_ref[...] = (acc_s[...] / l).astype(BF16)
        lse_ref[...] = jnp.broadcast_to(m_s[...] * ATT_C2 + jnp.log(l) * LOG2E, lse_ref.shape)

    return pl.pallas_call(
        body,
        name="attn_fwd",
        out_shape=(jax.ShapeDtypeStruct((s, cat_cols), BF16), jax.ShapeDtypeStruct((s, hh * LANES), F32)),
        grid=(hh, s // t),
        in_specs=[
            pl.BlockSpec((t, hd), lambda h, i: (i, h)),
            pl.BlockSpec((s, hd), lambda h, i: (0, h)),
            pl.BlockSpec((s, VDIM), lambda h, i: (0, hh + h)),
        ],
        out_specs=[pl.BlockSpec((t, VDIM), lambda h, i: (i, h)), pl.BlockSpec((t, LANES), lambda h, i: (i, h))],
        scratch_shapes=[pltpu.VMEM((t, 1), F32), pltpu.VMEM((t, 1), F32), pltpu.VMEM((t, VDIM), F32)],
        compiler_params=_cp("parallel", "parallel"),
    )(qc, kc, kv)


def _attn_bwd_prep(cat, dcat, lse2, n_heads):
    s, w = lse2.shape
    ts = _tile(s, 512, SUBLANES)

    def body(o_ref, do_ref, lse_ref, dob_ref, st_ref):
        do = do_ref[...]
        dob_ref[...] = do.astype(BF16)
        prod = do * o_ref[...].astype(F32)
        lane = lax.broadcasted_iota(jnp.int32, (ts, LANES), 1)
        for h in range(n_heads):
            cols = slice(h * LANES, (h + 1) * LANES)
            dsum = jnp.sum(prod[:, cols], axis=1, keepdims=True)
            st_ref[:, cols] = jnp.where(lane < STAT_SPLIT, lse_ref[:, cols], dsum)

    return pl.pallas_call(
        body,
        name="attn_bwd_prep",
        out_shape=(jax.ShapeDtypeStruct((s, w), BF16), jax.ShapeDtypeStruct((s, w), F32)),
        grid=(s // ts,),
        in_specs=[_rows(ts, w)] * 3,
        out_specs=[_rows(ts, w)] * 2,
        compiler_params=_cp("parallel"),
    )(cat, dcat, lse2)


def _attn_bwd(qc, kc, kv, dob, stats, n_heads):
    s = qc.shape[0]
    t = _tile(s, ATT_BWD_BLOCK, LANES)
    sub = _tile(t, ATT_BWD_SUB, LANES)
    nb = s // t
    hh = n_heads
    hd = NOPE + HEAD_PAD
    w = hh * LANES

    def body(q_ref, k_ref, v_ref, do_ref, st_ref, dq_ref, dkn_ref, dv_ref, dkr_ref, dk_s, dv_s):
        j = pl.program_id(1)

        @pl.when(j == 0)
        def _():
            dq_ref[...] = jnp.zeros(dq_ref.shape, F32)

        dk_s[...] = jnp.zeros(dk_s.shape, F32)
        dv_s[...] = jnp.zeros(dv_s.shape, F32)

        def pair(i0, diag):
            def width(r0):
                return r0 + sub if diag else t

            def products(r0):
                rows = pl.ds(i0 + r0, sub)
                sc_ = lax.dot_general(q_ref[rows, :], k_ref[0 : width(r0), :], NT, preferred_element_type=F32)
                dp_ = lax.dot_general(do_ref[rows, :], v_ref[0 : width(r0), :], NT, preferred_element_type=F32)
                return sc_, dp_

            nxt = products(0)
            for r0 in range(0, t, sub):
                ncol = width(r0)
                rows = pl.ds(i0 + r0, sub)
                kk = k_ref[0:ncol, :]
                qq, do, st = q_ref[rows, :], do_ref[rows, :], st_ref[rows, :]
                sc, dp = nxt
                if r0 + sub < t:
                    nxt = products(r0 + sub)
                if diag:
                    row = lax.broadcasted_iota(jnp.int32, sc.shape, 0) + r0
                    col = lax.broadcasted_iota(jnp.int32, sc.shape, 1)
                    sc = jnp.where(col <= row, sc, NEG)
                p = jnp.exp2(sc * ATT_C2 - st[:, 0:1])
                dv_s[0:ncol, :] += lax.dot_general(p.astype(BF16), do, TN, preferred_element_type=F32)
                ds = (p * (dp - st[:, STAT_SPLIT : STAT_SPLIT + 1]) * ATT_SCALE).astype(BF16)
                dk_s[0:ncol, :] += lax.dot_general(ds, qq, TN, preferred_element_type=F32)
                dq_ref[rows, :] += jnp.dot(ds, kk, preferred_element_type=F32)

        pair(pl.multiple_of(j * t, t), True)

        def loop_body(i, carry):
            pair(pl.multiple_of(i * t, t), False)
            return carry

        lax.fori_loop(j + 1, nb, loop_body, 0)
        dkn_ref[...] = dk_s[:, :NOPE].astype(BF16)
        dv_ref[...] = dv_s[...].astype(BF16)
        dkr_ref[...] = dk_s[:, NOPE:]

    whole = lambda width, off: pl.BlockSpec((s, width), lambda h, j: (0, off + h))
    blk = lambda width, off: pl.BlockSpec((t, width), lambda h, j: (j, off + h))
    return pl.pallas_call(
        body,
        name="attn_bwd",
        out_shape=(
            jax.ShapeDtypeStruct((s, hh * hd), F32),
            jax.ShapeDtypeStruct((s, w), BF16),
            jax.ShapeDtypeStruct((s, w), BF16),
            jax.ShapeDtypeStruct((s, w), F32),
        ),
        grid=(hh, nb),
        in_specs=[whole(hd, 0), blk(hd, 0), blk(VDIM, hh), whole(VDIM, 0), whole(LANES, 0)],
        out_specs=[whole(hd, 0), blk(NOPE, 0), blk(VDIM, 0), blk(HEAD_PAD, 0)],
        scratch_shapes=[pltpu.VMEM((t, hd), F32), pltpu.VMEM((t, VDIM), F32)],
        compiler_params=_cp("parallel", "arbitrary"),
    )(qc, kc, kv, dob, stats)


def _dq_unrope(dq, tabs, n_heads):
    s = dq.shape[0]
    hd = NOPE + HEAD_PAD
    w = n_heads * LANES
    ts = _tile(s, 512, SUBLANES)

    def body(d_ref, c_ref, sa_ref, sb_ref, o_ref):
        c, sa, sb = c_ref[...], sa_ref[...], sb_ref[...]
        for h in range(n_heads):
            o_ref[:, h * NOPE : (h + 1) * NOPE] = d_ref[:, h * hd : h * hd + NOPE].astype(BF16)
            rot = _rope_t(d_ref[:, h * hd + NOPE : (h + 1) * hd], c, sa, sb)
            o_ref[:, w + h * HEAD_PAD : w + (h + 1) * HEAD_PAD] = rot.astype(BF16)

    return pl.pallas_call(
        body,
        name="dq_unrope",
        out_shape=jax.ShapeDtypeStruct((s, 2 * w), BF16),
        grid=(s // ts,),
        in_specs=[_rows(ts, n_heads * hd)] + [_rows(ts, LANES)] * 3,
        out_specs=_rows(ts, 2 * w),
        compiler_params=_cp("parallel"),
    )(dq, *tabs)


def _adamw(w, m, v, grads, name):
    r, c = w.shape
    budget_rows = max(SUBLANES, (VMEM_LIMIT // 3) // (4 * c * 2 * (7 + len(grads))))
    tr = _tile(r, budget_rows, SUBLANES)
    ng = len(grads)
    c1 = 1.0 - ADAM_B1**ADAM_STEP
    c2 = 1.0 - ADAM_B2**ADAM_STEP

    def body(*refs):
        w_ref, m_ref, v_ref = refs[:3]
        g_ref, d_ref, nm_ref, nv_ref = refs[3 + ng :]
        g = refs[3][...]
        for extra in refs[4 : 3 + ng]:
            g = g + extra[...]
        mn = ADAM_B1 * m_ref[...] + (1.0 - ADAM_B1) * g
        vn = ADAM_B2 * v_ref[...] + (1.0 - ADAM_B2) * (g * g)
        g_ref[...] = g
        nm_ref[...] = mn
        nv_ref[...] = vn
        d_ref[...] = -ADAM_LR * ((mn / c1) / (jnp.sqrt(vn / c2) + ADAM_EPS) + ADAM_WD * w_ref[...])

    blk = pl.BlockSpec((tr, c), lambda i: (i, 0))
    out = jax.ShapeDtypeStruct((r, c), F32)
    return pl.pallas_call(
        body,
        name=name,
        out_shape=(out, out, out, out),
        grid=(r // tr,),
        in_specs=[blk] * (3 + ng),
        out_specs=[blk] * 4,
        compiler_params=_cp("parallel"),
    )(w, m, v, *grads)


def _ada_grad(ca_t, dm):
    d = ca_t.shape[0]
    nc = dm.shape[1]
    tn = _tile(nc, 512, LANES)

    def body(a_ref, b_ref, o_ref):
        o_ref[...] = jnp.dot(a_ref[...].astype(BF16), b_ref[...].astype(BF16), preferred_element_type=F32)

    return pl.pallas_call(
        body,
        name="ada_grad",
        out_shape=jax.ShapeDtypeStruct((d, nc), F32),
        grid=(nc // tn,),
        in_specs=[pl.BlockSpec((d, LANES), lambda j: (0, 0)), pl.BlockSpec((LANES, tn), lambda j: (0, j))],
        out_specs=pl.BlockSpec((d, tn), lambda j: (0, j)),
        compiler_params=_cp("parallel"),
    )(ca_t, dm)


def _sum_devices(g):
    n = g.shape[1]

    def body(g_ref, o_ref):
        acc = g_ref[0:SUBLANES, :]
        for dvc in range(1, N_DEV):
            acc = acc + g_ref[dvc * SUBLANES : (dvc + 1) * SUBLANES, :]
        o_ref[...] = acc

    return pl.pallas_call(
        body,
        name="sum_devices",
        out_shape=jax.ShapeDtypeStruct((SUBLANES, n), F32),
        in_specs=[pl.BlockSpec(memory_space=pltpu.VMEM)],
        out_specs=pl.BlockSpec(memory_space=pltpu.VMEM),
        compiler_params=pltpu.CompilerParams(vmem_limit_bytes=VMEM_LIMIT),
    )(g)


def _sum_chips(land, sent, name):
    _, r, c = land.shape
    tr = _tile(r, max(SUBLANES * 2, (VMEM_LIMIT // 4) // (c * 2 * (4 * N_CHIP + 4 * 2))), SUBLANES * 2)

    def body(l_ref, s_ref, o_ref):
        x, y, _ = _mesh_pos()
        me = 2 * x + y
        acc = jnp.where(me == 0, s_ref[0], l_ref[0]).astype(F32)
        for k in range(1, N_CHIP):
            acc = acc + jnp.where(me == k, s_ref[k], l_ref[k]).astype(F32)
        o_ref[...] = acc

    slots = pl.BlockSpec((N_CHIP, tr, c), lambda i: (0, i, 0))
    return pl.pallas_call(
        body,
        name=name,
        out_shape=jax.ShapeDtypeStruct((r, c), F32),
        grid=(r // tr,),
        in_specs=[slots, slots],
        out_specs=pl.BlockSpec((tr, c), lambda i: (i, 0)),
        compiler_params=_cp("parallel"),
    )(land, sent)


def _mesh_pos():
    return lax.axis_index("x"), lax.axis_index("y"), lax.axis_index("c")


def _other_chips(x, y):
    return [(1 - x, y), (x, 1 - y), (1 - x, 1 - y)]


def _all_gather8(x_shard, name):
    m_per, n = x_shard.shape

    def body(x_ref, out_ref, send_sems, recv_sems, local_sem):
        x, y, c = _mesh_pos()
        me, sibling = (x, y, c), (x, y, 1 - c)
        chips = _other_chips(x, y)

        def rows(px, py, pc):
            return out_ref.at[pl.ds((4 * px + 2 * py + pc) * m_per, m_per), :]

        def copy(k, block, to, src=None):
            return pltpu.make_async_remote_copy(
                src_ref=rows(*block) if src is None else src,
                dst_ref=rows(*block),
                send_sem=send_sems.at[k],
                recv_sem=recv_sems.at[k],
                device_id=to,
                device_id_type=MESH,
            )

        mine = pltpu.make_async_copy(x_ref, rows(*me), local_sem)
        mine.start()
        first = [copy(0, me, sibling, src=x_ref)]
        first += [copy(1 + j, me, (*chip, c), src=x_ref) for j, chip in enumerate(chips)]
        for cp in first:
            cp.start()
        passed = [copy(4 + j, (*chip, c), sibling) for j, chip in enumerate(chips)]
        for j, chip in enumerate(chips):
            copy(1 + j, (*chip, c), me).wait_recv()
            passed[j].start()
        copy(0, sibling, me).wait_recv()
        for j, chip in enumerate(chips):
            copy(4 + j, (*chip, 1 - c), me).wait_recv()
        for cp in first + passed:
            cp.wait_send()
        mine.wait()

    return pl.pallas_call(
        body,
        name=name,
        out_shape=jax.ShapeDtypeStruct((N_DEV * m_per, n), x_shard.dtype),
        in_specs=[pl.BlockSpec(memory_space=pltpu.VMEM)],
        out_specs=pl.BlockSpec(memory_space=pltpu.VMEM),
        scratch_shapes=[pltpu.SemaphoreType.DMA((7,)), pltpu.SemaphoreType.DMA((7,)), pltpu.SemaphoreType.DMA],
        compiler_params=pltpu.CompilerParams(vmem_limit_bytes=VMEM_LIMIT),
    )(x_shard)


HBM_SPEC = pl.BlockSpec(memory_space=pltpu.HBM)
SEM_SPEC = pl.BlockSpec(memory_space=pltpu.SEMAPHORE)
DATAFLOW = pltpu.SideEffectType.DATAFLOW_SIDE_EFFECTING


def _exchange_copies(ins, lands, send_sems, recv_sems, scatter):
    x, y, c = _mesh_pos()
    me = 2 * x + y
    sends, recvs = [], []
    for t in range(len(ins)):
        for r, (px, py) in enumerate(_other_chips(x, y)):
            peer = 2 * px + py

            def copy(src, dst, k=3 * t + r, to=(px, py, c)):
                return pltpu.make_async_remote_copy(
                    src_ref=src, dst_ref=dst, send_sem=send_sems.at[k], recv_sem=recv_sems.at[k], device_id=to, device_id_type=MESH
                )

            sends.append(copy(ins[t].at[peer] if scatter else ins[t], lands[t].at[me]))
            recvs.append(copy(ins[t].at[me] if scatter else ins[t], lands[t].at[peer]))
    return sends, recvs


def _exchange_start(arrs, scatter, name):
    nt = len(arrs)
    lands = [lax.empty(a.shape if scatter else (N_CHIP, *a.shape), a.dtype) for a in arrs]

    def body(*refs):
        ins, zones = refs[:nt], refs[nt : 2 * nt]
        send_sems, recv_sems, token = refs[2 * nt], refs[2 * nt + 1], refs[-1]
        sends, _ = _exchange_copies(ins, zones, send_sems, recv_sems, scatter)
        for cp in sends:
            cp.start()
        token[...] = jnp.zeros(token.shape, F32)

    bufs = list(arrs) + list(lands)
    return pl.pallas_call(
        body,
        name=name,
        out_shape=(
            pltpu.SemaphoreType.DMA((3 * nt,)),
            pltpu.SemaphoreType.DMA((3 * nt,)),
            *[pltpu.HBM(a.shape, a.dtype) for a in bufs],
            jax.ShapeDtypeStruct((SUBLANES, LANES), F32),
        ),
        in_specs=[HBM_SPEC] * (2 * nt),
        out_specs=(SEM_SPEC, SEM_SPEC, *[HBM_SPEC] * (2 * nt), pl.BlockSpec(memory_space=pltpu.VMEM)),
        input_output_aliases={k: 2 + k for k in range(2 * nt)},
        compiler_params=pltpu.CompilerParams(has_side_effects=DATAFLOW),
    )(*[pltpu.with_memory_space_constraint(a, pltpu.HBM) for a in bufs])


def _exchange_wait(state, after, scatter, name):
    send_sems, recv_sems, *bufs = state[:-1]
    nt = len(bufs) // 2

    def body(*refs):
        ins, zones = refs[:nt], refs[nt : 2 * nt]
        sends, recvs = _exchange_copies(ins, zones, refs[2 * nt], refs[2 * nt + 1], scatter)
        for cp in sends:
            cp.wait_send()
        for cp in recvs:
            cp.wait_recv()

    out = pl.pallas_call(
        body,
        name=name,
        out_shape=tuple(pltpu.HBM(a.shape, a.dtype) for a in bufs),
        in_specs=[HBM_SPEC] * (2 * nt) + [SEM_SPEC, SEM_SPEC, pl.BlockSpec(memory_space=pl.ANY)],
        out_specs=[HBM_SPEC] * (2 * nt),
        input_output_aliases={k: k for k in range(2 * nt)},
        compiler_params=pltpu.CompilerParams(has_side_effects=DATAFLOW),
    )(*bufs, send_sems, recv_sems, after)
    return list(out[:nt]), list(out[nt:])


def _swap_copies(ins, lands, send_sems, recv_sems):
    x, y, c = _mesh_pos()
    return [
        pltpu.make_async_remote_copy(
            src_ref=ins[t], dst_ref=lands[t], send_sem=send_sems.at[t], recv_sem=recv_sems.at[t],
            device_id=(x, y, 1 - c), device_id_type=MESH,
        )
        for t in range(len(ins))
    ]


def _swap_start(arrs, name):
    nt = len(arrs)
    lands = [lax.empty(a.shape, a.dtype) for a in arrs]

    def body(*refs):
        ins, zones = refs[:nt], refs[nt : 2 * nt]
        send_sems, recv_sems, token = refs[2 * nt], refs[2 * nt + 1], refs[-1]
        for cp in _swap_copies(ins, zones, send_sems, recv_sems):
            cp.start()
        token[...] = jnp.zeros(token.shape, F32)

    bufs = list(arrs) + lands
    return pl.pallas_call(
        body,
        name=name,
        out_shape=(
            pltpu.SemaphoreType.DMA((nt,)),
            pltpu.SemaphoreType.DMA((nt,)),
            *[pltpu.HBM(a.shape, a.dtype) for a in bufs],
            jax.ShapeDtypeStruct((SUBLANES, LANES), F32),
        ),
        in_specs=[HBM_SPEC] * (2 * nt),
        out_specs=(SEM_SPEC, SEM_SPEC, *[HBM_SPEC] * (2 * nt), pl.BlockSpec(memory_space=pltpu.VMEM)),
        input_output_aliases={k: 2 + k for k in range(2 * nt)},
        compiler_params=pltpu.CompilerParams(has_side_effects=DATAFLOW),
    )(*[pltpu.with_memory_space_constraint(a, pltpu.HBM) for a in bufs])


def _swap_wait(state, after, name):
    send_sems, recv_sems, *bufs = state[:-1]
    nt = len(bufs) // 2

    def body(*refs):
        cps = _swap_copies(refs[:nt], refs[nt : 2 * nt], refs[2 * nt], refs[2 * nt + 1])
        for cp in cps:
            cp.wait_send()
        for cp in cps:
            cp.wait_recv()

    out = pl.pallas_call(
        body,
        name=name,
        out_shape=tuple(pltpu.HBM(a.shape, a.dtype) for a in bufs),
        in_specs=[HBM_SPEC] * (2 * nt) + [SEM_SPEC, SEM_SPEC, pl.BlockSpec(memory_space=pl.ANY)],
        out_specs=[HBM_SPEC] * (2 * nt),
        input_output_aliases={k: k for k in range(2 * nt)},
        compiler_params=pltpu.CompilerParams(has_side_effects=DATAFLOW),
    )(*bufs, send_sems, recv_sems, after)
    return list(out[:nt]), list(out[nt:])


def _cols_from_shards(g):
    _, k, n = g.shape
    return jnp.transpose(g, (1, 0, 2)).reshape(k, N_CHIP * n)


def _cols_to_shards(a):
    k, n4 = a.shape
    return jnp.transpose(a.reshape(k, N_CHIP, n4 // N_CHIP), (1, 0, 2))


def _pad_to(vec, mult):
    n = vec.shape[0]
    return jnp.pad(vec, (0, (-n) % mult))


def kernel(x, c, positions, w_ada, b_ada, g_pre_mix, g_post_mix, w_in, g_q, w_uq, g_kv, w_ukv, conv_w_mix, conv_b_mix, w_o, g_pre_ffn, g_post_ffn, w_up, conv_w_ffn, conv_b_ffn, w_down, loss_target, m_w_ada, m_b_ada, m_g_pre_mix, m_g_post_mix, m_w_in, m_g_q, m_w_uq, m_g_kv, m_w_ukv, m_conv_w_mix, m_conv_b_mix, m_w_o, m_g_pre_ffn, m_g_post_ffn, m_w_up, m_conv_w_ffn, m_conv_b_ffn, m_w_down, v_w_ada, v_b_ada, v_g_pre_mix, v_g_post_mix, v_w_in, v_g_q, v_w_uq, v_g_kv, v_w_ukv, v_conv_w_mix, v_conv_b_mix, v_w_o, v_g_pre_ffn, v_g_post_ffn, v_w_up, v_conv_w_ffn, v_conv_b_ffn, v_w_down):
    xi, yi, ci = _mesh_pos()
    chip = 2 * xi + yi
    dev = 4 * xi + 2 * yi + ci

    s, d = x.shape[1], x.shape[2]
    ql, kl = g_q.shape[1], g_kv.shape[1]
    cwid = conv_b_mix.shape[1]
    f2 = conv_b_ffn.shape[1]
    hh = (w_uq.shape[2] * N_CHIP) // (NOPE + ROPE)
    w_att = hh * LANES
    nc_ada = w_ada.shape[2]
    lat = ql + kl + ROPE
    tc_mix = _tile(cwid, 512, LANES)
    lb = -(-(ql + kl + HEAD_PAD) // tc_mix) * tc_mix
    np_cols = lb + 3 * cwid
    assert cwid == hh * VDIM and w_att % tc_mix == 0

    x0 = x.reshape(s, d)
    tgt = loss_target.reshape(s, d)

    anchors = []

    def _behind(val, state):
        val, tok = lax.optimization_barrier((val, state[-1]))
        anchors.append(tok[0, 0])
        return val

    cwm_n, cwf_n = CONV_K * cwid // N_CHIP, CONV_K * f2 // N_CHIP
    pack_a = _pad_to(jnp.concatenate([c.reshape(-1), conv_w_mix.reshape(-1), conv_w_ffn.reshape(-1)]), SUBLANES * LANES)
    rows_a = _all_gather8(pack_a.reshape(SUBLANES, -1), "ag8_inputs").reshape(N_DEV, -1)
    c_all = rows_a[:, :d]
    south = rows_a[0::2]
    cw_mix = jnp.concatenate([south[j, d : d + cwm_n].reshape(CONV_K, -1) for j in range(N_CHIP)], axis=1)
    cw_ffn = jnp.concatenate([south[j, d + cwm_n : d + cwm_n + cwf_n].reshape(CONV_K, -1) for j in range(N_CHIP)], axis=1)

    b_cols = lax.dynamic_slice(b_ada, (0, chip * nc_ada), (1, nc_ada))
    mod_part, c_act = _ada_fwd(c_all, w_ada[0], b_cols)
    mod_rows = _all_gather8(mod_part, "ag8_mod")
    mod = jnp.concatenate(
        [lax.dynamic_slice_in_dim(mod_rows, 2 * N_DEV * j + dev, 1, axis=0) for j in range(N_CHIP)], axis=1
    )

    shards = [a[0].astype(BF16) for a in (w_in, w_uq, w_ukv, w_o, w_up, w_down)]
    first, mod = lax.optimization_barrier((shards[:3], mod))
    ag_a = _exchange_start(first, False, "ag_a_start")
    mod = _behind(mod, ag_a)
    sh_m, sc_m, gt_m, sh_f, sc_f, gt_f = [mod[:, k * d : (k + 1) * d] for k in range(N_MOD)]

    inv_freq = 1.0 / (ROPE_THETA ** (jnp.arange(0, ROPE, 2, dtype=F32) / ROPE))
    invf = jnp.concatenate([inv_freq, inv_freq, jnp.zeros((LANES - ROPE,), F32)]).reshape(1, LANES)
    tabs = _rope_tables(positions.astype(F32).reshape(s, 1), invf)
    h1 = _pre_fwd(x0, g_pre_mix, sc_m, sh_m)

    def with_own(landed, own):
        return [lax.dynamic_update_slice_in_dim(g, a[None], chip, axis=0) for g, a in zip(landed, own)]

    own_w, landed_w = _exchange_wait(ag_a, h1, False, "ag_a_wait")
    rest, landed_w = lax.optimization_barrier((shards[3:], landed_w))
    ag_b = _exchange_start(rest, False, "ag_b_start")
    h1 = _behind(h1, ag_b)
    g_in, g_uq, g_ukv = with_own(landed_w, own_w)
    full_in = _cols_from_shards(g_in)
    w_in_p = jnp.concatenate([full_in[:, :lat], jnp.zeros((d, lb - lat), BF16), full_in[:, lat:]], axis=1)
    full_uq = _cols_from_shards(g_uq).reshape(ql, hh, NOPE + ROPE)
    w_uq_p = jnp.concatenate(
        [
            full_uq[:, :, :NOPE].reshape(ql, w_att),
            jnp.pad(full_uq[:, :, NOPE:], ((0, 0), (0, 0), (0, HEAD_PAD - ROPE))).reshape(ql, w_att),
        ],
        axis=1,
    )
    full_ukv = _cols_from_shards(g_ukv).reshape(kl, hh, NOPE + VDIM)
    w_ukv_p = jnp.concatenate([full_ukv[:, :, :NOPE].reshape(kl, w_att), full_ukv[:, :, NOPE:].reshape(kl, w_att)], axis=1)

    proj = _matmul(h1, w_in_p, out_dtype=F32, tm=1024, tn=768, tk=2048, name="mm_proj")
    qn, kvn, kr = _latent_fwd(proj, g_q, g_kv, tabs, lb)
    q_f = _matmul(qn, w_uq_p, out_dtype=F32, tm=1024, tn=1024, tk=2048, name="mm_q")
    kv_p = _matmul(kvn, w_ukv_p, out_dtype=BF16, tm=1024, tn=1024, tk=2048, name="mm_kv")
    q_c, k_c = _head_cat(q_f, kv_p, kr, tabs, hh)
    cat, lse2 = _attn_fwd(q_c, k_c, kv_p, hh, w_att + cwid)
    cat = _mixer_fwd(cat, proj, cw_mix, conv_b_mix, lb, w_att)
    own_w, landed_w = _exchange_wait(ag_b, cat, False, "ag_b_wait")
    g_o, g_up, g_down = with_own(landed_w, own_w)
    w_o_f = g_o.reshape(-1, d)
    cw_ffn_p, cb_ffn_p = _pair_cols(cw_ffn), _pair_cols(conv_b_ffn)
    tcp, pair_perm = _pair_tile(f2 // 2), _pair_perm(f2 // 2)
    w_down_f = g_down.reshape(-1, d)
    mix = _matmul(cat, w_o_f, out_dtype=F32, tm=1024, tn=1024, tk=2048, name="mm_mix")

    x1, h2 = _mid_fwd(x0, mix, g_post_mix, gt_m, g_pre_ffn, sc_f, sh_f)
    up = _matmul(h2, g_up, out_dtype=F32, tm=1024, tn=tcp, tk=2048, name="mm_up", b_n_perm=pair_perm, b_col_shards=True)
    act = _ffn_act_fwd(up, cw_ffn_p, cb_ffn_p)
    y = _matmul(act, w_down_f, out_dtype=F32, tm=512, tn=1024, tk=5632, name="mm_down")
    dx2, dy, s_fin = _final(x1, y, tgt, g_post_ffn, gt_f)

    dw_down = _matmul(act, dy, ta=True, out_dtype=BF16, tm=1408, tn=1024, tk=2048, name="mm_dw_down")
    dact = _matmul(dy, w_down_f, tb=True, out_dtype=F32, tm=1024, tn=1408, tk=2048, name="mm_dact")
    dup, s_ffn_p = _ffn_act_bwd(dact, up, cw_ffn_p, cb_ffn_p)
    s_ffn = _unpair_cols(s_ffn_p)
    dw_up = _matmul(
        h2, dup, ta=True, out_dtype=BF16, tm=1024, tn=tcp, tk=2048, name="mm_dw_up", out_n_perm=pair_perm, out_col_shards=True
    )
    dh2 = _matmul_pair_k(dup, g_up, out_dtype=F32, tm=1024, tn=1024, name="mm_dh2")
    dx1, dmix, s_mid = _mid_bwd(dh2, dx2, x1, mix, g_pre_ffn, sc_f, g_post_mix, gt_m)

    dw_o = _matmul(cat, dmix, ta=True, out_dtype=BF16, tm=1024, tn=1024, tk=2048, name="mm_dw_o")
    send_b = [dw_o.reshape(N_CHIP, -1, d), dw_up, dw_down.reshape(N_CHIP, -1, d)]
    rs_b = _exchange_start(send_b, True, "rs_b_start")
    dmix = _behind(dmix, rs_b)
    dcat = _matmul(dmix, w_o_f, tb=True, out_dtype=F32, tm=1024, tn=1024, tk=2048, name="mm_dcat")
    dp_b, dp_c, dp_i, s_mix = _mixer_bwd(dcat, proj, cw_mix, conv_b_mix, lb, w_att)
    dob, stats = _attn_bwd_prep(cat, dcat, lse2, hh)
    dq_raw, dkv_k, dkv_v, dkr_h = _attn_bwd(q_c, k_c, kv_p, dob, stats, hh)
    dkv_p = jnp.concatenate([dkv_k, dkv_v], axis=1)
    dq_p = _dq_unrope(dq_raw, tabs, hh)
    dw_uq_p = _matmul(qn, dq_p, ta=True, out_dtype=BF16, tm=1024, tn=1024, tk=1024, name="mm_dw_uq")
    dqn = _matmul(dq_p, w_uq_p, tb=True, out_dtype=F32, tm=1024, tn=1024, tk=2048, name="mm_dqn")
    dw_ukv_p = _matmul(kvn, dkv_p, ta=True, out_dtype=BF16, tm=1024, tn=1024, tk=1024, name="mm_dw_ukv")
    dkvn = _matmul(dkv_p, w_ukv_p, tb=True, out_dtype=F32, tm=1024, tn=1024, tk=2048, name="mm_dkvn")
    dp_lat, s_lat = _latent_bwd(proj, dqn, dkvn, dkr_h, g_q, g_kv, tabs, lb)
    dproj = jnp.concatenate([dp_lat, dp_b, dp_c, dp_i], axis=1)
    dw_in_p = _matmul(h1, dproj, ta=True, out_dtype=BF16, tm=1024, tn=1536, tk=2048, name="mm_dw_in")

    dw_in_f = jnp.concatenate([dw_in_p[:, :lat], dw_in_p[:, lb:]], axis=1)
    uq3 = dw_uq_p.reshape(ql, 2, hh, LANES)
    dw_uq_f = jnp.concatenate([uq3[:, 0], uq3[:, 1, :, :ROPE]], axis=2).reshape(ql, hh * (NOPE + ROPE))
    ukv3 = dw_ukv_p.reshape(kl, 2, hh, LANES)
    dw_ukv_f = jnp.concatenate([ukv3[:, 0], ukv3[:, 1]], axis=2).reshape(kl, hh * (NOPE + VDIM))
    send_a = [_cols_to_shards(dw_in_f), _cols_to_shards(dw_uq_f), _cols_to_shards(dw_ukv_f)]
    rs_a = _exchange_start(send_a, True, "rs_a_start")
    dproj = _behind(dproj, rs_a)

    dh1 = _matmul(dproj, w_in_p, tb=True, out_dtype=F32, tm=512, tn=1024, tk=4608, name="mm_dh1")
    grad_x, s_first = _first_bwd(dh1, dx1, x0, g_pre_mix, sc_m)

    names = ["w_in", "w_uq", "w_ukv", "w_o", "w_up", "w_down"]
    sent_b, landed_b = _exchange_wait(rs_b, s_first, True, "rs_b_wait")
    sent_a, landed_a = _exchange_wait(rs_a, landed_b[0], True, "rs_a_wait")
    landed_a, s_first = lax.optimization_barrier((landed_a, s_first))
    part = [_sum_chips(l, a, "sum_chips_" + n) for l, a, n in zip(landed_a + landed_b, sent_a + sent_b, names)]

    dmod = jnp.concatenate([s_first[0:1], s_first[1:2], s_mid[3:4], s_mid[0:1], s_mid[1:2], s_fin[0:1]], axis=1)
    small = [
        dmod,
        s_first[2:3],
        s_mid[4:5],
        s_lat[0:1, :ql],
        s_lat[0:1, ql : ql + kl],
        s_mix[3:4],
        s_mid[2:3],
        s_fin[1:2],
        s_ffn[3:4],
        s_mix[0:3].reshape(1, -1),
        s_ffn[0:3].reshape(1, -1),
        s_fin[3:4, :LANES],
    ]
    sizes = [a.shape[1] for a in small]
    offs = [0]
    for n in sizes:
        offs.append(offs[-1] + n)
    pack_g = _pad_to(jnp.concatenate(small, axis=1).reshape(-1), SUBLANES * LANES * SUBLANES).reshape(SUBLANES, -1)
    gathered = _all_gather8(pack_g, "ag8_small_grads")
    tot = _sum_devices(gathered).reshape(-1)
    part_of = lambda k: tot[offs[k] : offs[k + 1]]
    dmod_all = gathered.reshape(N_DEV, -1)[:, : N_MOD * d]
    loss = part_of(11)[0]

    g_b_ada = part_of(0).reshape(1, -1)
    g_vecs = [part_of(k).reshape(1, -1) for k in range(1, 9)]
    g_cw_mix = lax.dynamic_slice(part_of(9).reshape(CONV_K, cwid), (0, chip * (cwid // N_CHIP)), (CONV_K, cwid // N_CHIP))
    g_cw_ffn = lax.dynamic_slice(part_of(10).reshape(CONV_K, f2), (0, chip * (f2 // N_CHIP)), (CONV_K, f2 // N_CHIP))

    swap = _swap_start(part, "swap_start")
    dm_cols = _behind(lax.dynamic_slice(dmod_all, (0, chip * nc_ada), (N_DEV, nc_ada)), swap)
    g_w_ada = _ada_grad(
        jnp.pad(c_act.T, ((0, 0), (0, LANES - N_DEV))), jnp.pad(dm_cols, ((0, LANES - N_DEV), (0, 0)))
    )
    big = {"w_ada": [a[None] for a in _adamw(w_ada[0], m_w_ada[0], v_w_ada[0], [g_w_ada], "adamw_w_ada")]}
    part, other = _swap_wait(swap, big["w_ada"][1], "swap_wait")

    big_w = [w_in, w_uq, w_ukv, w_o, w_up, w_down]
    big_m = [m_w_in, m_w_uq, m_w_ukv, m_w_o, m_w_up, m_w_down]
    big_v = [v_w_in, v_w_uq, v_w_ukv, v_w_o, v_w_up, v_w_down]
    for n, w_, m_, v_, p_, o_ in zip(names, big_w, big_m, big_v, part, other):
        big[n] = [a[None] for a in _adamw(w_[0], m_[0], v_[0], [p_, o_], "adamw_" + n)]

    sm_names = ["b_ada", "g_pre_mix", "g_post_mix", "g_q", "g_kv", "conv_b_mix", "g_pre_ffn", "g_post_ffn", "conv_b_ffn",
                "conv_w_mix", "conv_w_ffn"]
    sm_w = [b_ada, g_pre_mix, g_post_mix, g_q, g_kv, conv_b_mix, g_pre_ffn, g_post_ffn, conv_b_ffn, conv_w_mix, conv_w_ffn]
    sm_m = [m_b_ada, m_g_pre_mix, m_g_post_mix, m_g_q, m_g_kv, m_conv_b_mix, m_g_pre_ffn, m_g_post_ffn, m_conv_b_ffn,
            m_conv_w_mix, m_conv_w_ffn]
    sm_v = [v_b_ada, v_g_pre_mix, v_g_post_mix, v_g_q, v_g_kv, v_conv_b_mix, v_g_pre_ffn, v_g_post_ffn, v_conv_b_ffn,
            v_conv_w_mix, v_conv_w_ffn]
    sm_g = [g_b_ada] + g_vecs + [g_cw_mix, g_cw_ffn]
    flat = lambda arrs: jnp.concatenate([a.reshape(1, -1) for a in arrs], axis=1)
    sm_out = _adamw(flat(sm_w), flat(sm_m), flat(sm_v), [flat(sm_g)], "adamw_small")
    sm = {}
    off = 0
    for n, w_ in zip(sm_names, sm_w):
        sm[n] = [o[:, off : off + w_.size].reshape(w_.shape) for o in sm_out]
        off += w_.size

    order = ["w_ada", "b_ada", "g_pre_mix", "g_post_mix", "w_in", "g_q", "w_uq", "g_kv", "w_ukv", "conv_w_mix", "conv_b_mix",
             "w_o", "g_pre_ffn", "g_post_ffn", "w_up", "conv_w_ffn", "conv_b_ffn", "w_down"]
    res = {**big, **sm}
    outs = [loss + sum(anchors), grad_x.reshape(x.shape)]
    for k in range(4):
        outs += [res[n][k] for n in order]
    return tuple(outs)
```

```python
import math

import jax
import jax.numpy as jnp
from jax import lax
from jax.experimental import pallas as pl
from jax.experimental.pallas import tpu as pltpu

F32 = jnp.float32
BF16 = jnp.bfloat16
MESH = pl.DeviceIdType.MESH

N_DEV = 8
N_CHIP = 4
LANES = 128
SUBLANES = 8
VMEM_LIMIT = 56 * 2**20

NOPE = 128
ROPE = 64
VDIM = 128
HEAD_PAD = 128
ROPE_THETA = 10000.0
RMS_EPS = 1e-6
N_MOD = 6
CONV_K = 3
ATT_FWD_BLOCK, ATT_FWD_SUB = 2048, 256
ATT_BWD_BLOCK, ATT_BWD_SUB = 1024, 256
NEG = -1e30

ADAM_LR = 0.001
ADAM_B1 = 0.9
ADAM_B2 = 0.999
ADAM_EPS = 1e-08
ADAM_WD = 0.01
ADAM_STEP = 10


def _tile(n, pref, align):
    if n <= pref:
        return n
    t = (pref // align) * align
    while t >= align:
        if n % t == 0:
            return t
        t -= align
    return n


def _cp(*sem):
    return pltpu.CompilerParams(dimension_semantics=sem, vmem_limit_bytes=VMEM_LIMIT)


def _rsq(x):
    return lax.rsqrt(jnp.mean(x * x, axis=-1, keepdims=True) + RMS_EPS)


def _norm_bwd(dn, n, r):
    return r * (dn - n * jnp.mean(dn * n, axis=-1, keepdims=True))


def _colsum(a):
    return jnp.sum(a, axis=0, keepdims=True)


def _matmul(a, b, *, ta=False, tb=False, out_dtype, tm, tn, tk, name, b_n_perm=None, out_n_perm=None,
            b_col_shards=False, out_col_shards=False):
    assert not (b_col_shards and tb)
    if b_col_shards:
        b_rows, b_cols = b.shape[1], N_CHIP * b.shape[2]
    else:
        b_rows, b_cols = b.shape
    (k_a, m) = a.shape if ta else a.shape[::-1]
    (n, k_b) = (b_rows, b_cols) if tb else (b_cols, b_rows)
    assert k_a == k_b, (a.shape, b.shape, ta, tb)
    tm, tn, tk = _tile(m, tm, LANES), _tile(n, tn, LANES), _tile(k_a, tk, LANES)
    nk = k_a // tk
    same = lambda t: t
    bn, on = b_n_perm or same, out_n_perm or same
    a_spec = pl.BlockSpec((tk, tm), lambda i, j, k: (k, i)) if ta else pl.BlockSpec((tm, tk), lambda i, j, k: (i, k))
    if b_col_shards:
        per = (b_cols // N_CHIP) // tn
        b_spec = pl.BlockSpec((None, tk, tn), lambda i, j, k: (bn(j) // per, k, bn(j) % per))
    elif tb:
        b_spec = pl.BlockSpec((tn, tk), lambda i, j, k: (bn(j), k))
    else:
        b_spec = pl.BlockSpec((tk, tn), lambda i, j, k: (k, bn(j)))
    if out_col_shards:
        per_o = (n // N_CHIP) // tn
        out_shape = jax.ShapeDtypeStruct((N_CHIP, m, n // N_CHIP), out_dtype)
        out_spec = pl.BlockSpec((None, tm, tn), lambda i, j, k: (on(j) // per_o, i, on(j) % per_o))
    else:
        out_shape = jax.ShapeDtypeStruct((m, n), out_dtype)
        out_spec = pl.BlockSpec((tm, tn), lambda i, j, k: (i, on(j)))
    dims = (((0 if ta else 1,), (1 if tb else 0,)), ((), ()))

    def body(a_ref, b_ref, o_ref, *acc):
        p = lax.dot_general(a_ref[...].astype(BF16), b_ref[...].astype(BF16), dims, preferred_element_type=F32)
        _accumulate(p, o_ref, acc, nk)

    return pl.pallas_call(
        body,
        name=name,
        out_shape=out_shape,
        grid=(m // tm, n // tn, nk),
        in_specs=[a_spec, b_spec],
        out_specs=out_spec,
        scratch_shapes=[] if nk == 1 else [pltpu.VMEM((tm, tn), F32)],
        compiler_params=_cp("parallel", "parallel", "arbitrary"),
    )(a, b)


def _accumulate(p, o_ref, acc, nk):
    if nk == 1:
        o_ref[...] = p.astype(o_ref.dtype)
        return
    k = pl.program_id(2)

    @pl.when(k == 0)
    def _():
        acc[0][...] = p

    @pl.when(k > 0)
    def _():
        acc[0][...] += p

    @pl.when(k == nk - 1)
    def _():
        o_ref[...] = acc[0][...].astype(o_ref.dtype)


def _matmul_pair_k(a, b_shards, *, out_dtype, tm, tn, name):
    m, f2 = a.shape
    n = b_shards.shape[1]
    tc = _pair_tile(f2 // 2)
    nk = (f2 // 2) // tc
    per = (f2 // N_CHIP) // tc
    tm, tn = _tile(m, tm, LANES), _tile(n, tn, LANES)

    def body(a_ref, ba_ref, bg_ref, o_ref, *acc):
        av = a_ref[...]
        p = lax.dot_general(av[:, :tc], ba_ref[...], NT, preferred_element_type=F32)
        p = p + lax.dot_general(av[:, tc:], bg_ref[...], NT, preferred_element_type=F32)
        _accumulate(p, o_ref, acc, nk)

    def w_tile(first):
        return pl.BlockSpec((None, tn, tc), lambda i, j, k: ((first + k) // per, j, (first + k) % per))

    return pl.pallas_call(
        body,
        name=name,
        out_shape=jax.ShapeDtypeStruct((m, n), out_dtype),
        grid=(m // tm, n // tn, nk),
        in_specs=[pl.BlockSpec((tm, 2 * tc), lambda i, j, k: (i, k)), w_tile(0), w_tile(nk)],
        out_specs=pl.BlockSpec((tm, tn), lambda i, j, k: (i, j)),
        scratch_shapes=[] if nk == 1 else [pltpu.VMEM((tm, tn), F32)],
        compiler_params=_cp("parallel", "parallel", "arbitrary"),
    )(a, b_shards, b_shards)


def _rope_tables(pos_col, invf):
    s = pos_col.shape[0]
    ts = _tile(s, 1024, SUBLANES)
    half = ROPE // 2

    def body(p_ref, f_ref, c_ref, sa_ref, sb_ref):
        ang = p_ref[...] * f_ref[...]
        lane = lax.broadcasted_iota(jnp.int32, ang.shape, 1)
        cs, sn = jnp.cos(ang), jnp.sin(ang)
        c_ref[...] = jnp.where(lane < ROPE, cs, 0.0)
        sa_ref[...] = jnp.where((lane >= half) & (lane < ROPE), sn, 0.0)
        sb_ref[...] = jnp.where(lane < half, -sn, 0.0)

    tab = jax.ShapeDtypeStruct((s, LANES), F32)
    return pl.pallas_call(
        body,
        name="rope_tables",
        out_shape=(tab, tab, tab),
        grid=(s // ts,),
        in_specs=[pl.BlockSpec((ts, 1), lambda i: (i, 0)), pl.BlockSpec((1, LANES), lambda i: (0, 0))],
        out_specs=[pl.BlockSpec((ts, LANES), lambda i: (i, 0))] * 3,
        compiler_params=_cp("parallel"),
    )(pos_col, invf)


def _widen(t, w):
    return t if w == LANES else jnp.tile(t, (1, w // LANES))


def _rope(x, c, sa, sb):
    w = x.shape[1]
    c, sa, sb = _widen(c, w), _widen(sa, w), _widen(sb, w)
    return x * c + pltpu.roll(x, ROPE // 2, 1) * sa + pltpu.roll(x, w - ROPE // 2, 1) * sb


def _rope_t(d, c, sa, sb):
    w = d.shape[1]
    c, sa, sb = _widen(c, w), _widen(sa, w), _widen(sb, w)
    return d * c + pltpu.roll(d * sa, w - ROPE // 2, 1) + pltpu.roll(d * sb, ROPE // 2, 1)


def _ada_fwd(c_all, w, b):
    d, nc = w.shape
    tn = _tile(nc, 512, LANES)

    def body(c_ref, w_ref, b_ref, o_ref, ca_ref):
        cv = c_ref[...]
        ca = cv * jax.nn.sigmoid(cv)
        ca_ref[...] = ca
        o_ref[...] = jnp.dot(ca.astype(BF16), w_ref[...].astype(BF16), preferred_element_type=F32) + b_ref[...]

    return pl.pallas_call(
        body,
        name="ada_fwd",
        out_shape=(jax.ShapeDtypeStruct((N_DEV, nc), F32), jax.ShapeDtypeStruct((N_DEV, d), F32)),
        grid=(nc // tn,),
        in_specs=[
            pl.BlockSpec((N_DEV, d), lambda j: (0, 0)),
            pl.BlockSpec((d, tn), lambda j: (0, j)),
            pl.BlockSpec((1, tn), lambda j: (0, j)),
        ],
        out_specs=[pl.BlockSpec((N_DEV, tn), lambda j: (0, j)), pl.BlockSpec((N_DEV, d), lambda j: (0, 0))],
        compiler_params=_cp("arbitrary"),
    )(c_all, w, b)


def _rows(ts, d):
    return pl.BlockSpec((ts, d), lambda i: (i, 0))


def _vec(d):
    return pl.BlockSpec((1, d), lambda i: (0, 0))


def _sums(d):
    return pl.BlockSpec((SUBLANES, d), lambda i: (0, 0))


def _acc_rows(ref, i, rows):
    @pl.when(i == 0)
    def _():
        ref[...] = jnp.zeros(ref.shape, ref.dtype)

    for k, r in enumerate(rows):
        ref[k : k + 1, :] += r


def _pre_fwd(x, g, sc, sh):
    s, d = x.shape
    ts = _tile(s, 512, SUBLANES)

    def body(x_ref, g_ref, sc_ref, sh_ref, h_ref):
        xv = x_ref[...]
        h_ref[...] = (((xv * _rsq(xv)) * g_ref[...]) * (1.0 + sc_ref[...]) + sh_ref[...]).astype(BF16)

    return pl.pallas_call(
        body,
        name="pre_mix_fwd",
        out_shape=jax.ShapeDtypeStruct((s, d), BF16),
        grid=(s // ts,),
        in_specs=[_rows(ts, d), _vec(d), _vec(d), _vec(d)],
        out_specs=_rows(ts, d),
        compiler_params=_cp("parallel"),
    )(x, g, sc, sh)


def _mid_fwd(x0, mix, g_post, gt, g_pre, sc, sh):
    s, d = x0.shape
    ts = _tile(s, 256, SUBLANES)

    def body(x_ref, m_ref, gp_ref, gt_ref, g_ref, sc_ref, sh_ref, x1_ref, h_ref):
        mv = m_ref[...]
        x1 = x_ref[...] + gt_ref[...] * ((mv * _rsq(mv)) * gp_ref[...])
        x1_ref[...] = x1
        h_ref[...] = (((x1 * _rsq(x1)) * g_ref[...]) * (1.0 + sc_ref[...]) + sh_ref[...]).astype(BF16)

    return pl.pallas_call(
        body,
        name="mid_fwd",
        out_shape=(jax.ShapeDtypeStruct((s, d), F32), jax.ShapeDtypeStruct((s, d), BF16)),
        grid=(s // ts,),
        in_specs=[_rows(ts, d), _rows(ts, d)] + [_vec(d)] * 5,
        out_specs=[_rows(ts, d), _rows(ts, d)],
        compiler_params=_cp("parallel"),
    )(x0, mix, g_post, gt, g_pre, sc, sh)


def _final(x1, y, tgt, g_post, gt):
    s, d = x1.shape
    ts = _tile(s, 256, SUBLANES)
    ni = s // ts

    def body(x_ref, y_ref, t_ref, gp_ref, gt_ref, dx_ref, dy_ref, s_ref):
        i = pl.program_id(0)
        yv, gp, gt_v = y_ref[...], gp_ref[...], gt_ref[...]
        r = _rsq(yv)
        n = yv * r
        err = (x_ref[...] + gt_v * (n * gp)) - t_ref[...]
        dx = err * (1.0 / d)
        dx_ref[...] = dx
        dy_ref[...] = _norm_bwd(dx * (gt_v * gp), n, r).astype(BF16)
        _acc_rows(s_ref, i, [_colsum(dx * (n * gp)), _colsum(dx * gt_v * n), _colsum(err * err)])

        @pl.when(i == ni - 1)
        def _():
            tot = jnp.sum(s_ref[2:3, :], axis=1, keepdims=True) * (0.5 / d)
            s_ref[3:4, :] = jnp.broadcast_to(tot, (1, d))

    return pl.pallas_call(
        body,
        name="final_fwd_bwd",
        out_shape=(
            jax.ShapeDtypeStruct((s, d), F32),
            jax.ShapeDtypeStruct((s, d), BF16),
            jax.ShapeDtypeStruct((SUBLANES, d), F32),
        ),
        grid=(ni,),
        in_specs=[_rows(ts, d)] * 3 + [_vec(d)] * 2,
        out_specs=[_rows(ts, d), _rows(ts, d), _sums(d)],
        compiler_params=_cp("arbitrary"),
    )(x1, y, tgt, g_post, gt)


def _mid_bwd(dh2, dx2, x1, mix, g_pre, sc, g_post, gt):
    s, d = x1.shape
    ts = _tile(s, 256, SUBLANES)

    def body(dh_ref, dx2_ref, x_ref, m_ref, g_ref, sc_ref, gp_ref, gt_ref, dx1_ref, dm_ref, s_ref):
        i = pl.program_id(0)
        dh, xv, mv = dh_ref[...], x_ref[...], m_ref[...]
        g, sc_v, gp, gt_v = g_ref[...], sc_ref[...], gp_ref[...], gt_ref[...]
        r1 = _rsq(xv)
        n1 = xv * r1
        dx1 = dx2_ref[...] + _norm_bwd(dh * (g * (1.0 + sc_v)), n1, r1)
        dx1_ref[...] = dx1
        rm = _rsq(mv)
        nm = mv * rm
        dm_ref[...] = _norm_bwd(dx1 * (gt_v * gp), nm, rm).astype(BF16)
        _acc_rows(
            s_ref,
            i,
            [
                _colsum(dh),
                _colsum(dh * (n1 * g)),
                _colsum(dh * (1.0 + sc_v) * n1),
                _colsum(dx1 * (nm * gp)),
                _colsum(dx1 * gt_v * nm),
            ],
        )

    return pl.pallas_call(
        body,
        name="mid_bwd",
        out_shape=(
            jax.ShapeDtypeStruct((s, d), F32),
            jax.ShapeDtypeStruct((s, d), BF16),
            jax.ShapeDtypeStruct((SUBLANES, d), F32),
        ),
        grid=(s // ts,),
        in_specs=[_rows(ts, d)] * 4 + [_vec(d)] * 4,
        out_specs=[_rows(ts, d), _rows(ts, d), _sums(d)],
        compiler_params=_cp("arbitrary"),
    )(dh2, dx2, x1, mix, g_pre, sc, g_post, gt)


def _first_bwd(dh1, dx1, x0, g, sc):
    s, d = x0.shape
    ts = _tile(s, 256, SUBLANES)

    def body(dh_ref, dx1_ref, x_ref, g_ref, sc_ref, dx_ref, s_ref):
        i = pl.program_id(0)
        dh, xv, gv, sc_v = dh_ref[...], x_ref[...], g_ref[...], sc_ref[...]
        r = _rsq(xv)
        n = xv * r
        dx_ref[...] = dx1_ref[...] + _norm_bwd(dh * (gv * (1.0 + sc_v)), n, r)
        _acc_rows(s_ref, i, [_colsum(dh), _colsum(dh * (n * gv)), _colsum(dh * (1.0 + sc_v) * n)])

    return pl.pallas_call(
        body,
        name="first_bwd",
        out_shape=(jax.ShapeDtypeStruct((s, d), F32), jax.ShapeDtypeStruct((SUBLANES, d), F32)),
        grid=(s // ts,),
        in_specs=[_rows(ts, d)] * 3 + [_vec(d)] * 2,
        out_specs=[_rows(ts, d), _sums(d)],
        compiler_params=_cp("arbitrary"),
    )(dh1, dx1, x0, g, sc)


def _latent_fwd(proj, g_q, g_kv, tabs, lb):
    s = proj.shape[0]
    ql, kl = g_q.shape[1], g_kv.shape[1]
    ts = _tile(s, 512, SUBLANES)

    def body(p_ref, gq_ref, gk_ref, c_ref, sa_ref, sb_ref, q_ref, kv_ref, kr_ref):
        pv = p_ref[...]
        q, kv, kr = pv[:, :ql], pv[:, ql : ql + kl], pv[:, ql + kl : ql + kl + HEAD_PAD]
        q_ref[...] = ((q * _rsq(q)) * gq_ref[...]).astype(BF16)
        kv_ref[...] = ((kv * _rsq(kv)) * gk_ref[...]).astype(BF16)
        kr_ref[...] = _rope(kr, c_ref[...], sa_ref[...], sb_ref[...]).astype(BF16)

    return pl.pallas_call(
        body,
        name="latent_fwd",
        out_shape=(
            jax.ShapeDtypeStruct((s, ql), BF16),
            jax.ShapeDtypeStruct((s, kl), BF16),
            jax.ShapeDtypeStruct((s, HEAD_PAD), BF16),
        ),
        grid=(s // ts,),
        in_specs=[_rows(ts, lb), _vec(ql), _vec(kl)] + [_rows(ts, LANES)] * 3,
        out_specs=[_rows(ts, ql), _rows(ts, kl), _rows(ts, HEAD_PAD)],
        compiler_params=_cp("parallel"),
    )(proj, g_q, g_kv, *tabs)


def _latent_bwd(proj, dqn, dkvn, dkr_h, g_q, g_kv, tabs, lb):
    s = proj.shape[0]
    ql, kl = g_q.shape[1], g_kv.shape[1]
    hw = dkr_h.shape[1]
    ts = _tile(s, 256, SUBLANES)
    pad = lb - ql - kl - HEAD_PAD

    def body(p_ref, dq_ref, dkv_ref, dkr_ref, gq_ref, gk_ref, c_ref, sa_ref, sb_ref, o_ref, s_ref):
        i = pl.program_id(0)
        pv = p_ref[...]
        q, kv = pv[:, :ql], pv[:, ql : ql + kl]
        dqn_v, dkvn_v = dq_ref[...], dkv_ref[...]
        rq = _rsq(q)
        nq = q * rq
        rk = _rsq(kv)
        nk = kv * rk
        dkr = dkr_ref[:, :HEAD_PAD]
        for h in range(1, hw // HEAD_PAD):
            dkr = dkr + dkr_ref[:, h * HEAD_PAD : (h + 1) * HEAD_PAD]
        parts = [
            _norm_bwd(dqn_v * gq_ref[...], nq, rq).astype(BF16),
            _norm_bwd(dkvn_v * gk_ref[...], nk, rk).astype(BF16),
            _rope_t(dkr, c_ref[...], sa_ref[...], sb_ref[...]).astype(BF16),
        ]
        if pad:
            parts.append(jnp.zeros((ts, pad), BF16))
        o_ref[...] = jnp.concatenate(parts, axis=1)
        row = [_colsum(dqn_v * nq), _colsum(dkvn_v * nk), jnp.zeros((1, lb - ql - kl), F32)]
        _acc_rows(s_ref, i, [jnp.concatenate(row, axis=1)])

    return pl.pallas_call(
        body,
        name="latent_bwd",
        out_shape=(jax.ShapeDtypeStruct((s, lb), BF16), jax.ShapeDtypeStruct((SUBLANES, lb), F32)),
        grid=(s // ts,),
        in_specs=[_rows(ts, lb), _rows(ts, ql), _rows(ts, kl), _rows(ts, hw)]
        + [_vec(ql), _vec(kl)]
        + [_rows(ts, LANES)] * 3,
        out_specs=[_rows(ts, lb), _sums(lb)],
        compiler_params=_cp("arbitrary"),
    )(proj, dqn, dkvn, dkr_h, g_q, g_kv, *tabs)


def _conv3(ext, w, b):
    return (pltpu.roll(ext, 2, 0) * w[0:1] + pltpu.roll(ext, 1, 0) * w[1:2]) + ext * w[2:3] + b


def _conv3_t(du, w):
    n = du.shape[0]
    return du * w[2:3] + pltpu.roll(du, n - 1, 0) * w[1:2] + pltpu.roll(du, n - 2, 0) * w[0:1]


def _halo_maps(ts, s):
    r8, last = ts // SUBLANES, s // SUBLANES - 1
    prev = lambda i: jnp.maximum(i * r8 - 1, 0)
    nxt = lambda i: jnp.minimum((i + 1) * r8, last)
    return prev, nxt


def _mixer_fwd(cat, proj, cw, cb, lb, col0):
    s = proj.shape[0]
    cwid = cw.shape[1]
    ts = _tile(s, 512, SUBLANES)
    tc = _tile(cwid, 512, LANES)
    assert lb % tc == 0 and col0 % tc == 0
    nj, ob, oc = cwid // tc, lb // tc, col0 // tc
    prev, _ = _halo_maps(ts, s)

    def body(_, gb_ref, gc_ref, ci_ref, pgc_ref, pci_ref, w_ref, b_ref, o_ref):
        keep = jnp.where(pl.program_id(1) > 0, 1.0, 0.0)
        ext = jnp.concatenate([pgc_ref[...] * pci_ref[...] * keep, gc_ref[...] * ci_ref[...]], axis=0)
        o_ref[...] = (gb_ref[...] * _conv3(ext, w_ref[...], b_ref[...])[SUBLANES:]).astype(BF16)

    def col(k):
        return pl.BlockSpec((ts, tc), lambda j, i: (i, ob + k * nj + j))

    def halo(k):
        return pl.BlockSpec((SUBLANES, tc), lambda j, i: (prev(i), ob + k * nj + j))

    return pl.pallas_call(
        body,
        name="mixer_fwd",
        out_shape=jax.ShapeDtypeStruct(cat.shape, BF16),
        grid=(nj, s // ts),
        in_specs=[pl.BlockSpec(memory_space=pl.ANY), col(0), col(1), col(2), halo(1), halo(2)]
        + [pl.BlockSpec((CONV_K, tc), lambda j, i: (0, j)), pl.BlockSpec((1, tc), lambda j, i: (0, j))],
        out_specs=pl.BlockSpec((ts, tc), lambda j, i: (i, oc + j)),
        input_output_aliases={0: 0},
        compiler_params=_cp("parallel", "arbitrary"),
    )(cat, proj, proj, proj, proj, proj, cw, cb)


def _mixer_bwd(dcat, proj, cw, cb, lb, col0):
    s = proj.shape[0]
    cwid = cw.shape[1]
    ts = _tile(s, 256, SUBLANES)
    tc = _tile(cwid, 512, LANES)
    nj, ob, oc = cwid // tc, lb // tc, col0 // tc
    ni = s // ts
    prev, nxt = _halo_maps(ts, s)

    def body(d_ref, dn_ref, gb_ref, gbn_ref, gc_ref, gcp_ref, gcn_ref, ci_ref, cip_ref, cin_ref, w_ref, b_ref,
             dgb_ref, dgc_ref, dci_ref, s_ref):
        i = pl.program_id(1)
        keep_p = jnp.where(i > 0, 1.0, 0.0)
        keep_n = jnp.where(i < ni - 1, 1.0, 0.0)
        w = w_ref[...]
        gc = jnp.concatenate([gcp_ref[...], gc_ref[...], gcn_ref[...]], axis=0)
        ci = jnp.concatenate([cip_ref[...] * keep_p, ci_ref[...], cin_ref[...]], axis=0)
        u = gc * ci
        cv = _conv3(u, w, b_ref[...])[SUBLANES:]
        dco = jnp.concatenate([d_ref[...], dn_ref[...] * keep_n], axis=0)
        gb = jnp.concatenate([gb_ref[...], gbn_ref[...]], axis=0)
        dgb_ref[...] = (dco * cv)[:ts].astype(BF16)
        dcv = dco * gb
        du = _conv3_t(dcv, w)[:ts]
        dgc_ref[...] = (du * ci_ref[...]).astype(BF16)
        dci_ref[...] = (du * gc_ref[...]).astype(BF16)
        dt = dcv[:ts]
        u1, u2 = pltpu.roll(u, 1, 0), pltpu.roll(u, 2, 0)
        lo, hi = SUBLANES, SUBLANES + ts
        _acc_rows(s_ref, i, [_colsum(dt * u2[lo:hi]), _colsum(dt * u1[lo:hi]), _colsum(dt * u[lo:hi]), _colsum(dt)])

    def col(k):
        return pl.BlockSpec((ts, tc), lambda j, i: (i, ob + k * nj + j))

    def halo(k, which):
        return pl.BlockSpec((SUBLANES, tc), lambda j, i: (which(i), ob + k * nj + j))

    out_col = [pl.BlockSpec((ts, tc), lambda j, i: (i, j))] * 3
    grad = jax.ShapeDtypeStruct((s, cwid), BF16)
    return pl.pallas_call(
        body,
        name="mixer_bwd",
        out_shape=(grad, grad, grad, jax.ShapeDtypeStruct((SUBLANES, cwid), F32)),
        grid=(nj, ni),
        in_specs=[
            pl.BlockSpec((ts, tc), lambda j, i: (i, oc + j)),
            pl.BlockSpec((SUBLANES, tc), lambda j, i: (nxt(i), oc + j)),
            col(0), halo(0, nxt),
            col(1), halo(1, prev), halo(1, nxt),
            col(2), halo(2, prev), halo(2, nxt),
            pl.BlockSpec((CONV_K, tc), lambda j, i: (0, j)),
            pl.BlockSpec((1, tc), lambda j, i: (0, j)),
        ],
        out_specs=out_col + [pl.BlockSpec((SUBLANES, tc), lambda j, i: (0, j))],
        compiler_params=_cp("parallel", "arbitrary"),
    )(dcat, dcat, proj, proj, proj, proj, proj, proj, proj, proj, cw, cb)


def _pair_tile(f):
    return _tile(f, 1408, LANES)


def _pair_perm(f):
    nj = f // _pair_tile(f)
    return lambda p: (p % 2) * nj + p // 2


def _pair_cols(a):
    r, f2 = a.shape
    tc = _pair_tile(f2 // 2)
    return a.reshape(r, 2, f2 // (2 * tc), tc).transpose(0, 2, 1, 3).reshape(r, f2)


def _unpair_cols(a):
    r, f2 = a.shape
    tc = _pair_tile(f2 // 2)
    return a.reshape(r, f2 // (2 * tc), 2, tc).transpose(0, 2, 1, 3).reshape(r, f2)


def _ffn_act_fwd(up, cw, cb):
    s, f2 = up.shape
    f = f2 // 2
    ts = _tile(s, 512, SUBLANES)
    tc = _pair_tile(f)
    prev, _ = _halo_maps(ts, s)

    def body(u_ref, p_ref, w_ref, b_ref, o_ref):
        keep = jnp.where(pl.program_id(1) > 0, 1.0, 0.0)
        ext = jnp.concatenate([p_ref[...] * keep, u_ref[...]], axis=0)
        u = _conv3(ext, w_ref[...], b_ref[...])[SUBLANES:]
        a, g = u[:, :tc], u[:, tc:]
        o_ref[...] = ((g * jax.nn.sigmoid(g)) * a).astype(BF16)

    def pair(rows, which):
        return pl.BlockSpec((rows, 2 * tc), lambda j, i: (which(i), j))

    return pl.pallas_call(
        body,
        name="ffn_act_fwd",
        out_shape=jax.ShapeDtypeStruct((s, f), BF16),
        grid=(f // tc, s // ts),
        in_specs=[pair(ts, lambda i: i), pair(SUBLANES, prev), pair(CONV_K, lambda i: 0), pair(1, lambda i: 0)],
        out_specs=pl.BlockSpec((ts, tc), lambda j, i: (i, j)),
        compiler_params=_cp("parallel", "arbitrary"),
    )(up, up, cw, cb)


def _ffn_act_bwd(dact, up, cw, cb):
    s, f2 = up.shape
    f = f2 // 2
    ts = _tile(s, 256, SUBLANES)
    tc = _pair_tile(f)
    nj, ni = f // tc, s // ts
    prev, nxt = _halo_maps(ts, s)

    def body(d_ref, dn_ref, u_ref, up_ref, un_ref, w_ref, b_ref, dup_ref, s_ref):
        i = pl.program_id(1)
        keep_p = jnp.where(i > 0, 1.0, 0.0)
        keep_n = jnp.where(i < ni - 1, 1.0, 0.0)
        w = w_ref[...]
        ext = jnp.concatenate([up_ref[...] * keep_p, u_ref[...], un_ref[...]], axis=0)
        u = _conv3(ext, w, b_ref[...])[SUBLANES:]
        a, g = u[:, :tc], u[:, tc:]
        dact_v = jnp.concatenate([d_ref[...], dn_ref[...] * keep_n], axis=0)
        sg = jax.nn.sigmoid(g)
        du = jnp.concatenate([dact_v * (g * sg), dact_v * a * (sg * (1.0 + g * (1.0 - sg)))], axis=1)
        dup_ref[...] = _conv3_t(du, w)[:ts].astype(BF16)
        dt = du[:ts]
        lo, hi = SUBLANES, SUBLANES + ts
        e1, e2 = pltpu.roll(ext, 1, 0), pltpu.roll(ext, 2, 0)
        _acc_rows(s_ref, i, [_colsum(dt * e2[lo:hi]), _colsum(dt * e1[lo:hi]), _colsum(dt * ext[lo:hi]), _colsum(dt)])

    def pair(rows, which):
        return pl.BlockSpec((rows, 2 * tc), lambda j, i: (which(i), j))

    return pl.pallas_call(
        body,
        name="ffn_act_bwd",
        out_shape=(jax.ShapeDtypeStruct((s, f2), BF16), jax.ShapeDtypeStruct((SUBLANES, f2), F32)),
        grid=(nj, ni),
        in_specs=[
            pl.BlockSpec((ts, tc), lambda j, i: (i, j)),
            pl.BlockSpec((SUBLANES, tc), lambda j, i: (nxt(i), j)),
            pair(ts, lambda i: i), pair(SUBLANES, prev), pair(SUBLANES, nxt),
            pair(CONV_K, lambda i: 0), pair(1, lambda i: 0),
        ],
        out_specs=[pair(ts, lambda i: i), pair(SUBLANES, lambda i: 0)],
        compiler_params=_cp("parallel", "arbitrary"),
    )(dact, dact, up, up, up, cw, cb)


ATT_SCALE = 1.0 / math.sqrt(NOPE + ROPE)
LOG2E = math.log2(math.e)
ATT_C2 = ATT_SCALE * LOG2E
STAT_SPLIT = 64
NT = (((1,), (1,)), ((), ()))
TN = (((0,), (0,)), ((), ()))


def _head_cat(q, kv, kr, tabs, n_heads):
    s, w2 = q.shape
    w = w2 // 2
    ts = _tile(s, 512, SUBLANES)
    hd = NOPE + HEAD_PAD

    def body(q_ref, kv_ref, kr_ref, c_ref, sa_ref, sb_ref, qc_ref, kc_ref):
        qv = q_ref[...]
        qr = _rope(qv[:, w:], c_ref[...], sa_ref[...], sb_ref[...]).astype(BF16)
        krv = kr_ref[...]
        for h in range(n_heads):
            qc_ref[:, h * hd : h * hd + NOPE] = qv[:, h * NOPE : (h + 1) * NOPE].astype(BF16)
            qc_ref[:, h * hd + NOPE : (h + 1) * hd] = qr[:, h * HEAD_PAD : (h + 1) * HEAD_PAD]
            kc_ref[:, h * hd : h * hd + NOPE] = kv_ref[:, h * NOPE : (h + 1) * NOPE]
            kc_ref[:, h * hd + NOPE : (h + 1) * hd] = krv

    out = jax.ShapeDtypeStruct((s, n_heads * hd), BF16)
    return pl.pallas_call(
        body,
        name="head_cat",
        out_shape=(out, out),
        grid=(s // ts,),
        in_specs=[_rows(ts, w2), _rows(ts, w), _rows(ts, HEAD_PAD)] + [_rows(ts, LANES)] * 3,
        out_specs=[_rows(ts, n_heads * hd)] * 2,
        compiler_params=_cp("parallel"),
    )(q, kv, kr, *tabs)


def _attn_fwd(qc, kc, kv, n_heads, cat_cols):
    s = qc.shape[0]
    t = _tile(s, ATT_FWD_BLOCK, LANES)
    sub = _tile(t, ATT_FWD_SUB, LANES)
    hh = n_heads
    hd = NOPE + HEAD_PAD

    def body(q_ref, k_ref, v_ref, o_ref, lse_ref, m_s, l_s, acc_s):
        i = pl.program_id(1)
        m_s[...] = jnp.full(m_s.shape, NEG, F32)
        l_s[...] = jnp.zeros(l_s.shape, F32)
        acc_s[...] = jnp.zeros(acc_s.shape, F32)

        def chunk(k0, diag):
            m_all, l_all, acc_all = m_s[...], l_s[...], acc_s[...]
            new_m, new_l, new_acc = [], [], []

            def scores(r0):
                ncol = r0 + sub if diag else t
                return lax.dot_general(q_ref[pl.ds(r0, sub), :], k_ref[pl.ds(k0, ncol), :], NT, preferred_element_type=F32)

            sc_next = scores(0)
            for r0 in range(0, t, sub):
                ncol = r0 + sub if diag else t
                sc = sc_next
                if r0 + sub < t:
                    sc_next = scores(r0 + sub)
                if diag:
                    row = lax.broadcasted_iota(jnp.int32, sc.shape, 0) + r0
                    col = lax.broadcasted_iota(jnp.int32, sc.shape, 1)
                    sc = jnp.where(col <= row, sc, NEG)
                m_prev = m_all[r0 : r0 + sub]
                m_new = jnp.maximum(m_prev, jnp.max(sc, axis=1, keepdims=True))
                alpha = jnp.exp2((m_prev - m_new) * ATT_C2)
                p = jnp.exp2((sc - m_new) * ATT_C2)
                pv = jnp.dot(p.astype(BF16), v_ref[pl.ds(k0, ncol), :], preferred_element_type=F32)
                new_m.append(m_new)
                new_l.append(alpha * l_all[r0 : r0 + sub] + jnp.sum(p, axis=1, keepdims=True))
                new_acc.append(alpha * acc_all[r0 : r0 + sub] + pv)
            m_s[...] = jnp.concatenate(new_m, axis=0)
            l_s[...] = jnp.concatenate(new_l, axis=0)
            acc_s[...] = jnp.concatenate(new_acc, axis=0)

        def loop_body(k, carry):
            chunk(pl.multiple_of(k * t, t), False)
            return carry

        lax.fori_loop(0, i, loop_body, 0)
        chunk(pl.multiple_of(i * t, t), True)
        l = l_s[...]
        o_ref[...] = (acc_s[...] / l).astype(BF16)
        lse_ref[...] = jnp.broadcast_to(m_s[...] * ATT_C2 + jnp.log(l) * LOG2E, lse_ref.shape)

    return pl.pallas_call(
        body,
        name="attn_fwd",
        out_shape=(jax.ShapeDtypeStruct((s, cat_cols), BF16), jax.ShapeDtypeStruct((s, hh * LANES), F32)),
        grid=(hh, s // t),
        in_specs=[
            pl.BlockSpec((t, hd), lambda h, i: (i, h)),
            pl.BlockSpec((s, hd), lambda h, i: (0, h)),
            pl.BlockSpec((s, VDIM), lambda h, i: (0, hh + h)),
        ],
        out_specs=[pl.BlockSpec((t, VDIM), lambda h, i: (i, h)), pl.BlockSpec((t, LANES), lambda h, i: (i, h))],
        scratch_shapes=[pltpu.VMEM((t, 1), F32), pltpu.VMEM((t, 1), F32), pltpu.VMEM((t, VDIM), F32)],
        compiler_params=_cp("parallel", "parallel"),
    )(qc, kc, kv)


def _attn_bwd_prep(cat, dcat, lse2, n_heads):
    s, w = lse2.shape
    ts = _tile(s, 512, SUBLANES)

    def body(o_ref, do_ref, lse_ref, dob_ref, st_ref):
        do = do_ref[...]
        dob_ref[...] = do.astype(BF16)
        prod = do * o_ref[...].astype(F32)
        lane = lax.broadcasted_iota(jnp.int32, (ts, LANES), 1)
        for h in range(n_heads):
            cols = slice(h * LANES, (h + 1) * LANES)
            dsum = jnp.sum(prod[:, cols], axis=1, keepdims=True)
            st_ref[:, cols] = jnp.where(lane < STAT_SPLIT, lse_ref[:, cols], dsum)

    return pl.pallas_call(
        body,
        name="attn_bwd_prep",
        out_shape=(jax.ShapeDtypeStruct((s, w), BF16), jax.ShapeDtypeStruct((s, w), F32)),
        grid=(s // ts,),
        in_specs=[_rows(ts, w)] * 3,
        out_specs=[_rows(ts, w)] * 2,
        compiler_params=_cp("parallel"),
    )(cat, dcat, lse2)


def _attn_bwd(qc, kc, kv, dob, stats, n_heads):
    s = qc.shape[0]
    t = _tile(s, ATT_BWD_BLOCK, LANES)
    sub = _tile(t, ATT_BWD_SUB, LANES)
    nb = s // t
    hh = n_heads
    hd = NOPE + HEAD_PAD
    w = hh * LANES

    def body(q_ref, k_ref, v_ref, do_ref, st_ref, dq_ref, dkn_ref, dv_ref, dkr_ref, dk_s, dv_s):
        j = pl.program_id(1)

        @pl.when(j == 0)
        def _():
            dq_ref[...] = jnp.zeros(dq_ref.shape, F32)

        dk_s[...] = jnp.zeros(dk_s.shape, F32)
        dv_s[...] = jnp.zeros(dv_s.shape, F32)

        def pair(i0, diag):
            def width(r0):
                return r0 + sub if diag else t

            def products(r0):
                rows = pl.ds(i0 + r0, sub)
                sc_ = lax.dot_general(q_ref[rows, :], k_ref[0 : width(r0), :], NT, preferred_element_type=F32)
                dp_ = lax.dot_general(do_ref[rows, :], v_ref[0 : width(r0), :], NT, preferred_element_type=F32)
                return sc_, dp_

            nxt = products(0)
            for r0 in range(0, t, sub):
                ncol = width(r0)
                rows = pl.ds(i0 + r0, sub)
                kk = k_ref[0:ncol, :]
                qq, do, st = q_ref[rows, :], do_ref[rows, :], st_ref[rows, :]
                sc, dp = nxt
                if r0 + sub < t:
                    nxt = products(r0 + sub)
                if diag:
                    row = lax.broadcasted_iota(jnp.int32, sc.shape, 0) + r0
                    col = lax.broadcasted_iota(jnp.int32, sc.shape, 1)
                    sc = jnp.where(col <= row, sc, NEG)
                p = jnp.exp2(sc * ATT_C2 - st[:, 0:1])
                dv_s[0:ncol, :] += lax.dot_general(p.astype(BF16), do, TN, preferred_element_type=F32)
                ds = (p * (dp - st[:, STAT_SPLIT : STAT_SPLIT + 1]) * ATT_SCALE).astype(BF16)
                dk_s[0:ncol, :] += lax.dot_general(ds, qq, TN, preferred_element_type=F32)
                dq_ref[rows, :] += jnp.dot(ds, kk, preferred_element_type=F32)

        pair(pl.multiple_of(j * t, t), True)

        def loop_body(i, carry):
            pair(pl.multiple_of(i * t, t), False)
            return carry

        lax.fori_loop(j + 1, nb, loop_body, 0)
        dkn_ref[...] = dk_s[:, :NOPE].astype(BF16)
        dv_ref[...] = dv_s[...].astype(BF16)
        dkr_ref[...] = dk_s[:, NOPE:]

    whole = lambda width, off: pl.BlockSpec((s, width), lambda h, j: (0, off + h))
    blk = lambda width, off: pl.BlockSpec((t, width), lambda h, j: (j, off + h))
    return pl.pallas_call(
        body,
        name="attn_bwd",
        out_shape=(
            jax.ShapeDtypeStruct((s, hh * hd), F32),
            jax.ShapeDtypeStruct((s, w), BF16),
            jax.ShapeDtypeStruct((s, w), BF16),
            jax.ShapeDtypeStruct((s, w), F32),
        ),
        grid=(hh, nb),
        in_specs=[whole(hd, 0), blk(hd, 0), blk(VDIM, hh), whole(VDIM, 0), whole(LANES, 0)],
        out_specs=[whole(hd, 0), blk(NOPE, 0), blk(VDIM, 0), blk(HEAD_PAD, 0)],
        scratch_shapes=[pltpu.VMEM((t, hd), F32), pltpu.VMEM((t, VDIM), F32)],
        compiler_params=_cp("parallel", "arbitrary"),
    )(qc, kc, kv, dob, stats)


def _dq_unrope(dq, tabs, n_heads):
    s = dq.shape[0]
    hd = NOPE + HEAD_PAD
    w = n_heads * LANES
    ts = _tile(s, 512, SUBLANES)

    def body(d_ref, c_ref, sa_ref, sb_ref, o_ref):
        c, sa, sb = c_ref[...], sa_ref[...], sb_ref[...]
        for h in range(n_heads):
            o_ref[:, h * NOPE : (h + 1) * NOPE] = d_ref[:, h * hd : h * hd + NOPE].astype(BF16)
            rot = _rope_t(d_ref[:, h * hd + NOPE : (h + 1) * hd], c, sa, sb)
            o_ref[:, w + h * HEAD_PAD : w + (h + 1) * HEAD_PAD] = rot.astype(BF16)

    return pl.pallas_call(
        body,
        name="dq_unrope",
        out_shape=jax.ShapeDtypeStruct((s, 2 * w), BF16),
        grid=(s // ts,),
        in_specs=[_rows(ts, n_heads * hd)] + [_rows(ts, LANES)] * 3,
        out_specs=_rows(ts, 2 * w),
        compiler_params=_cp("parallel"),
    )(dq, *tabs)


def _adamw(w, m, v, grads, name):
    r, c = w.shape
    budget_rows = max(SUBLANES, (VMEM_LIMIT // 3) // (4 * c * 2 * (7 + len(grads))))
    tr = _tile(r, budget_rows, SUBLANES)
    ng = len(grads)
    c1 = 1.0 - ADAM_B1**ADAM_STEP
    c2 = 1.0 - ADAM_B2**ADAM_STEP

    def body(*refs):
        w_ref, m_ref, v_ref = refs[:3]
        g_ref, d_ref, nm_ref, nv_ref = refs[3 + ng :]
        g = refs[3][...]
        for extra in refs[4 : 3 + ng]:
            g = g + extra[...]
        mn = ADAM_B1 * m_ref[...] + (1.0 - ADAM_B1) * g
        vn = ADAM_B2 * v_ref[...] + (1.0 - ADAM_B2) * (g * g)
        g_ref[...] = g
        nm_ref[...] = mn
        nv_ref[...] = vn
        d_ref[...] = -ADAM_LR * ((mn / c1) / (jnp.sqrt(vn / c2) + ADAM_EPS) + ADAM_WD * w_ref[...])

    blk = pl.BlockSpec((tr, c), lambda i: (i, 0))
    out = jax.ShapeDtypeStruct((r, c), F32)
    return pl.pallas_call(
        body,
        name=name,
        out_shape=(out, out, out, out),
        grid=(r // tr,),
        in_specs=[blk] * (3 + ng),
        out_specs=[blk] * 4,
        compiler_params=_cp("parallel"),
    )(w, m, v, *grads)


def _ada_grad(ca_t, dm):
    d = ca_t.shape[0]
    nc = dm.shape[1]
    tn = _tile(nc, 512, LANES)

    def body(a_ref, b_ref, o_ref):
        o_ref[...] = jnp.dot(a_ref[...].astype(BF16), b_ref[...].astype(BF16), preferred_element_type=F32)

    return pl.pallas_call(
        body,
        name="ada_grad",
        out_shape=jax.ShapeDtypeStruct((d, nc), F32),
        grid=(nc // tn,),
        in_specs=[pl.BlockSpec((d, LANES), lambda j: (0, 0)), pl.BlockSpec((LANES, tn), lambda j: (0, j))],
        out_specs=pl.BlockSpec((d, tn), lambda j: (0, j)),
        compiler_params=_cp("parallel"),
    )(ca_t, dm)


def _sum_devices(g):
    n = g.shape[1]

    def body(g_ref, o_ref):
        acc = g_ref[0:SUBLANES, :]
        for dvc in range(1, N_DEV):
            acc = acc + g_ref[dvc * SUBLANES : (dvc + 1) * SUBLANES, :]
        o_ref[...] = acc

    return pl.pallas_call(
        body,
        name="sum_devices",
        out_shape=jax.ShapeDtypeStruct((SUBLANES, n), F32),
        in_specs=[pl.BlockSpec(memory_space=pltpu.VMEM)],
        out_specs=pl.BlockSpec(memory_space=pltpu.VMEM),
        compiler_params=pltpu.CompilerParams(vmem_limit_bytes=VMEM_LIMIT),
    )(g)


def _sum_chips(land, sent, name):
    _, r, c = land.shape
    tr = _tile(r, max(SUBLANES * 2, (VMEM_LIMIT // 4) // (c * 2 * (4 * N_CHIP + 4 * 2))), SUBLANES * 2)

    def body(l_ref, s_ref, o_ref):
        x, y, _ = _mesh_pos()
        me = 2 * x + y
        acc = jnp.where(me == 0, s_ref[0], l_ref[0]).astype(F32)
        for k in range(1, N_CHIP):
            acc = acc + jnp.where(me == k, s_ref[k], l_ref[k]).astype(F32)
        o_ref[...] = acc

    slots = pl.BlockSpec((N_CHIP, tr, c), lambda i: (0, i, 0))
    return pl.pallas_call(
        body,
        name=name,
        out_shape=jax.ShapeDtypeStruct((r, c), F32),
        grid=(r // tr,),
        in_specs=[slots, slots],
        out_specs=pl.BlockSpec((tr, c), lambda i: (i, 0)),
        compiler_params=_cp("parallel"),
    )(land, sent)


def _mesh_pos():
    return lax.axis_index("x"), lax.axis_index("y"), lax.axis_index("c")


def _other_chips(x, y):
    return [(1 - x, y), (x, 1 - y), (1 - x, 1 - y)]


def _all_gather8(x_shard, name):
    m_per, n = x_shard.shape

    def body(x_ref, out_ref, send_sems, recv_sems, local_sem):
        x, y, c = _mesh_pos()
        me, sibling = (x, y, c), (x, y, 1 - c)
        chips = _other_chips(x, y)

        def rows(px, py, pc):
            return out_ref.at[pl.ds((4 * px + 2 * py + pc) * m_per, m_per), :]

        def copy(k, block, to, src=None):
            return pltpu.make_async_remote_copy(
                src_ref=rows(*block) if src is None else src,
                dst_ref=rows(*block),
                send_sem=send_sems.at[k],
                recv_sem=recv_sems.at[k],
                device_id=to,
                device_id_type=MESH,
            )

        mine = pltpu.make_async_copy(x_ref, rows(*me), local_sem)
        mine.start()
        first = [copy(0, me, sibling, src=x_ref)]
        first += [copy(1 + j, me, (*chip, c), src=x_ref) for j, chip in enumerate(chips)]
        for cp in first:
            cp.start()
        passed = [copy(4 + j, (*chip, c), sibling) for j, chip in enumerate(chips)]
        for j, chip in enumerate(chips):
            copy(1 + j, (*chip, c), me).wait_recv()
            passed[j].start()
        copy(0, sibling, me).wait_recv()
        for j, chip in enumerate(chips):
            copy(4 + j, (*chip, 1 - c), me).wait_recv()
        for cp in first + passed:
            cp.wait_send()
        mine.wait()

    return pl.pallas_call(
        body,
        name=name,
        out_shape=jax.ShapeDtypeStruct((N_DEV * m_per, n), x_shard.dtype),
        in_specs=[pl.BlockSpec(memory_space=pltpu.VMEM)],
        out_specs=pl.BlockSpec(memory_space=pltpu.VMEM),
        scratch_shapes=[pltpu.SemaphoreType.DMA((7,)), pltpu.SemaphoreType.DMA((7,)), pltpu.SemaphoreType.DMA],
        compiler_params=pltpu.CompilerParams(vmem_limit_bytes=VMEM_LIMIT),
    )(x_shard)


HBM_SPEC = pl.BlockSpec(memory_space=pltpu.HBM)
SEM_SPEC = pl.BlockSpec(memory_space=pltpu.SEMAPHORE)
DATAFLOW = pltpu.SideEffectType.DATAFLOW_SIDE_EFFECTING


def _exchange_copies(ins, lands, send_sems, recv_sems, scatter):
    x, y, c = _mesh_pos()
    me = 2 * x + y
    sends, recvs = [], []
    for t in range(len(ins)):
        for r, (px, py) in enumerate(_other_chips(x, y)):
            peer = 2 * px + py

            def copy(src, dst, k=3 * t + r, to=(px, py, c)):
                return pltpu.make_async_remote_copy(
                    src_ref=src, dst_ref=dst, send_sem=send_sems.at[k], recv_sem=recv_sems.at[k], device_id=to, device_id_type=MESH
                )

            sends.append(copy(ins[t].at[peer] if scatter else ins[t], lands[t].at[me]))
            recvs.append(copy(ins[t].at[me] if scatter else ins[t], lands[t].at[peer]))
    return sends, recvs


def _exchange_start(arrs, scatter, name):
    nt = len(arrs)
    lands = [lax.empty(a.shape if scatter else (N_CHIP, *a.shape), a.dtype) for a in arrs]

    def body(*refs):
        ins, zones = refs[:nt], refs[nt : 2 * nt]
        send_sems, recv_sems, token = refs[2 * nt], refs[2 * nt + 1], refs[-1]
        sends, _ = _exchange_copies(ins, zones, send_sems, recv_sems, scatter)
        for cp in sends:
            cp.start()
        token[...] = jnp.zeros(token.shape, F32)

    bufs = list(arrs) + list(lands)
    return pl.pallas_call(
        body,
        name=name,
        out_shape=(
            pltpu.SemaphoreType.DMA((3 * nt,)),
            pltpu.SemaphoreType.DMA((3 * nt,)),
            *[pltpu.HBM(a.shape, a.dtype) for a in bufs],
            jax.ShapeDtypeStruct((SUBLANES, LANES), F32),
        ),
        in_specs=[HBM_SPEC] * (2 * nt),
        out_specs=(SEM_SPEC, SEM_SPEC, *[HBM_SPEC] * (2 * nt), pl.BlockSpec(memory_space=pltpu.VMEM)),
        input_output_aliases={k: 2 + k for k in range(2 * nt)},
        compiler_params=pltpu.CompilerParams(has_side_effects=DATAFLOW),
    )(*[pltpu.with_memory_space_constraint(a, pltpu.HBM) for a in bufs])


def _exchange_wait(state, after, scatter, name):
    send_sems, recv_sems, *bufs = state[:-1]
    nt = len(bufs) // 2

    def body(*refs):
        ins, zones = refs[:nt], refs[nt : 2 * nt]
        sends, recvs = _exchange_copies(ins, zones, refs[2 * nt], refs[2 * nt + 1], scatter)
        for cp in sends:
            cp.wait_send()
        for cp in recvs:
            cp.wait_recv()

    out = pl.pallas_call(
        body,
        name=name,
        out_shape=tuple(pltpu.HBM(a.shape, a.dtype) for a in bufs),
        in_specs=[HBM_SPEC] * (2 * nt) + [SEM_SPEC, SEM_SPEC, pl.BlockSpec(memory_space=pl.ANY)],
        out_specs=[HBM_SPEC] * (2 * nt),
        input_output_aliases={k: k for k in range(2 * nt)},
        compiler_params=pltpu.CompilerParams(has_side_effects=DATAFLOW),
    )(*bufs, send_sems, recv_sems, after)
    return list(out[:nt]), list(out[nt:])


def _swap_copies(ins, lands, send_sems, recv_sems):
    x, y, c = _mesh_pos()
    return [
        pltpu.make_async_remote_copy(
            src_ref=ins[t], dst_ref=lands[t], send_sem=send_sems.at[t], recv_sem=recv_sems.at[t],
            device_id=(x, y, 1 - c), device_id_type=MESH,
        )
        for t in range(len(ins))
    ]


def _swap_start(arrs, name):
    nt = len(arrs)
    lands = [lax.empty(a.shape, a.dtype) for a in arrs]

    def body(*refs):
        ins, zones = refs[:nt], refs[nt : 2 * nt]
        send_sems, recv_sems, token = refs[2 * nt], refs[2 * nt + 1], refs[-1]
        for cp in _swap_copies(ins, zones, send_sems, recv_sems):
            cp.start()
        token[...] = jnp.zeros(token.shape, F32)

    bufs = list(arrs) + lands
    return pl.pallas_call(
        body,
        name=name,
        out_shape=(
            pltpu.SemaphoreType.DMA((nt,)),
            pltpu.SemaphoreType.DMA((nt,)),
            *[pltpu.HBM(a.shape, a.dtype) for a in bufs],
            jax.ShapeDtypeStruct((SUBLANES, LANES), F32),
        ),
        in_specs=[HBM_SPEC] * (2 * nt),
        out_specs=(SEM_SPEC, SEM_SPEC, *[HBM_SPEC] * (2 * nt), pl.BlockSpec(memory_space=pltpu.VMEM)),
        input_output_aliases={k: 2 + k for k in range(2 * nt)},
        compiler_params=pltpu.CompilerParams(has_side_effects=DATAFLOW),
    )(*[pltpu.with_memory_space_constraint(a, pltpu.HBM) for a in bufs])


def _swap_wait(state, after, name):
    send_sems, recv_sems, *bufs = state[:-1]
    nt = len(bufs) // 2

    def body(*refs):
        cps = _swap_copies(refs[:nt], refs[nt : 2 * nt], refs[2 * nt], refs[2 * nt + 1])
        for cp in cps:
            cp.wait_send()
        for cp in cps:
            cp.wait_recv()

    out = pl.pallas_call(
        body,
        name=name,
        out_shape=tuple(pltpu.HBM(a.shape, a.dtype) for a in bufs),
        in_specs=[HBM_SPEC] * (2 * nt) + [SEM_SPEC, SEM_SPEC, pl.BlockSpec(memory_space=pl.ANY)],
        out_specs=[HBM_SPEC] * (2 * nt),
        input_output_aliases={k: k for k in range(2 * nt)},
        compiler_params=pltpu.CompilerParams(has_side_effects=DATAFLOW),
    )(*bufs, send_sems, recv_sems, after)
    return list(out[:nt]), list(out[nt:])


def _cols_from_shards(g):
    _, k, n = g.shape
    return jnp.transpose(g, (1, 0, 2)).reshape(k, N_CHIP * n)


def _cols_to_shards(a):
    k, n4 = a.shape
    return jnp.transpose(a.reshape(k, N_CHIP, n4 // N_CHIP), (1, 0, 2))


def _pad_to(vec, mult):
    n = vec.shape[0]
    return jnp.pad(vec, (0, (-n) % mult))


def kernel(x, c, positions, w_ada, b_ada, g_pre_mix, g_post_mix, w_in, g_q, w_uq, g_kv, w_ukv, conv_w_mix, conv_b_mix, w_o, g_pre_ffn, g_post_ffn, w_up, conv_w_ffn, conv_b_ffn, w_down, loss_target, m_w_ada, m_b_ada, m_g_pre_mix, m_g_post_mix, m_w_in, m_g_q, m_w_uq, m_g_kv, m_w_ukv, m_conv_w_mix, m_conv_b_mix, m_w_o, m_g_pre_ffn, m_g_post_ffn, m_w_up, m_conv_w_ffn, m_conv_b_ffn, m_w_down, v_w_ada, v_b_ada, v_g_pre_mix, v_g_post_mix, v_w_in, v_g_q, v_w_uq, v_g_kv, v_w_ukv, v_conv_w_mix, v_conv_b_mix, v_w_o, v_g_pre_ffn, v_g_post_ffn, v_w_up, v_conv_w_ffn, v_conv_b_ffn, v_w_down):
    xi, yi, ci = _mesh_pos()
    chip = 2 * xi + yi
    dev = 4 * xi + 2 * yi + ci

    s, d = x.shape[1], x.shape[2]
    ql, kl = g_q.shape[1], g_kv.shape[1]
    cwid = conv_b_mix.shape[1]
    f2 = conv_b_ffn.shape[1]
    hh = (w_uq.shape[2] * N_CHIP) // (NOPE + ROPE)
    w_att = hh * LANES
    nc_ada = w_ada.shape[2]
    lat = ql + kl + ROPE
    tc_mix = _tile(cwid, 512, LANES)
    lb = -(-(ql + kl + HEAD_PAD) // tc_mix) * tc_mix
    np_cols = lb + 3 * cwid
    assert cwid == hh * VDIM and w_att % tc_mix == 0

    x0 = x.reshape(s, d)
    tgt = loss_target.reshape(s, d)

    anchors = []

    def _behind(val, state):
        val, tok = lax.optimization_barrier((val, state[-1]))
        anchors.append(tok[0, 0])
        return val

    cwm_n, cwf_n = CONV_K * cwid // N_CHIP, CONV_K * f2 // N_CHIP
    pack_a = _pad_to(jnp.concatenate([c.reshape(-1), conv_w_mix.reshape(-1), conv_w_ffn.reshape(-1)]), SUBLANES * LANES)
    rows_a = _all_gather8(pack_a.reshape(SUBLANES, -1), "ag8_inputs").reshape(N_DEV, -1)
    c_all = rows_a[:, :d]
    south = rows_a[0::2]
    cw_mix = jnp.concatenate([south[j, d : d + cwm_n].reshape(CONV_K, -1) for j in range(N_CHIP)], axis=1)
    cw_ffn = jnp.concatenate([south[j, d + cwm_n : d + cwm_n + cwf_n].reshape(CONV_K, -1) for j in range(N_CHIP)], axis=1)

    b_cols = lax.dynamic_slice(b_ada, (0, chip * nc_ada), (1, nc_ada))
    mod_part, c_act = _ada_fwd(c_all, w_ada[0], b_cols)
    mod_rows = _all_gather8(mod_part, "ag8_mod")
    mod = jnp.concatenate(
        [lax.dynamic_slice_in_dim(mod_rows, 2 * N_DEV * j + dev, 1, axis=0) for j in range(N_CHIP)], axis=1
    )

    shards = [a[0].astype(BF16) for a in (w_in, w_uq, w_ukv, w_o, w_up, w_down)]
    first, mod = lax.optimization_barrier((shards[:3], mod))
    ag_a = _exchange_start(first, False, "ag_a_start")
    mod = _behind(mod, ag_a)
    sh_m, sc_m, gt_m, sh_f, sc_f, gt_f = [mod[:, k * d : (k + 1) * d] for k in range(N_MOD)]

    inv_freq = 1.0 / (ROPE_THETA ** (jnp.arange(0, ROPE, 2, dtype=F32) / ROPE))
    invf = jnp.concatenate([inv_freq, inv_freq, jnp.zeros((LANES - ROPE,), F32)]).reshape(1, LANES)
    tabs = _rope_tables(positions.astype(F32).reshape(s, 1), invf)
    h1 = _pre_fwd(x0, g_pre_mix, sc_m, sh_m)

    def with_own(landed, own):
        return [lax.dynamic_update_slice_in_dim(g, a[None], chip, axis=0) for g, a in zip(landed, own)]

    own_w, landed_w = _exchange_wait(ag_a, h1, False, "ag_a_wait")
    rest, landed_w = lax.optimization_barrier((shards[3:], landed_w))
    ag_b = _exchange_start(rest, False, "ag_b_start")
    h1 = _behind(h1, ag_b)
    g_in, g_uq, g_ukv = with_own(landed_w, own_w)
    full_in = _cols_from_shards(g_in)
    w_in_p = jnp.concatenate([full_in[:, :lat], jnp.zeros((d, lb - lat), BF16), full_in[:, lat:]], axis=1)
    full_uq = _cols_from_shards(g_uq).reshape(ql, hh, NOPE + ROPE)
    w_uq_p = jnp.concatenate(
        [
            full_uq[:, :, :NOPE].reshape(ql, w_att),
            jnp.pad(full_uq[:, :, NOPE:], ((0, 0), (0, 0), (0, HEAD_PAD - ROPE))).reshape(ql, w_att),
        ],
        axis=1,
    )
    full_ukv = _cols_from_shards(g_ukv).reshape(kl, hh, NOPE + VDIM)
    w_ukv_p = jnp.concatenate([full_ukv[:, :, :NOPE].reshape(kl, w_att), full_ukv[:, :, NOPE:].reshape(kl, w_att)], axis=1)

    proj = _matmul(h1, w_in_p, out_dtype=F32, tm=1024, tn=768, tk=2048, name="mm_proj")
    qn, kvn, kr = _latent_fwd(proj, g_q, g_kv, tabs, lb)
    q_f = _matmul(qn, w_uq_p, out_dtype=F32, tm=1024, tn=1024, tk=2048, name="mm_q")
    kv_p = _matmul(kvn, w_ukv_p, out_dtype=BF16, tm=1024, tn=1024, tk=2048, name="mm_kv")
    q_c, k_c = _head_cat(q_f, kv_p, kr, tabs, hh)
    cat, lse2 = _attn_fwd(q_c, k_c, kv_p, hh, w_att + cwid)
    cat = _mixer_fwd(cat, proj, cw_mix, conv_b_mix, lb, w_att)
    own_w, landed_w = _exchange_wait(ag_b, cat, False, "ag_b_wait")
    g_o, g_up, g_down = with_own(landed_w, own_w)
    w_o_f = g_o.reshape(-1, d)
    cw_ffn_p, cb_ffn_p = _pair_cols(cw_ffn), _pair_cols(conv_b_ffn)
    tcp, pair_perm = _pair_tile(f2 // 2), _pair_perm(f2 // 2)
    w_down_f = g_down.reshape(-1, d)
    mix = _matmul(cat, w_o_f, out_dtype=F32, tm=1024, tn=1024, tk=2048, name="mm_mix")

    x1, h2 = _mid_fwd(x0, mix, g_post_mix, gt_m, g_pre_ffn, sc_f, sh_f)
    up = _matmul(h2, g_up, out_dtype=F32, tm=1024, tn=tcp, tk=2048, name="mm_up", b_n_perm=pair_perm, b_col_shards=True)
    act = _ffn_act_fwd(up, cw_ffn_p, cb_ffn_p)
    y = _matmul(act, w_down_f, out_dtype=F32, tm=512, tn=1024, tk=5632, name="mm_down")
    dx2, dy, s_fin = _final(x1, y, tgt, g_post_ffn, gt_f)

    dw_down = _matmul(act, dy, ta=True, out_dtype=BF16, tm=1408, tn=1024, tk=2048, name="mm_dw_down")
    dact = _matmul(dy, w_down_f, tb=True, out_dtype=F32, tm=1024, tn=1408, tk=2048, name="mm_dact")
    dup, s_ffn_p = _ffn_act_bwd(dact, up, cw_ffn_p, cb_ffn_p)
    s_ffn = _unpair_cols(s_ffn_p)
    dw_up = _matmul(
        h2, dup, ta=True, out_dtype=BF16, tm=1024, tn=tcp, tk=2048, name="mm_dw_up", out_n_perm=pair_perm, out_col_shards=True
    )
    dh2 = _matmul_pair_k(dup, g_up, out_dtype=F32, tm=1024, tn=1024, name="mm_dh2")
    dx1, dmix, s_mid = _mid_bwd(dh2, dx2, x1, mix, g_pre_ffn, sc_f, g_post_mix, gt_m)

    dw_o = _matmul(cat, dmix, ta=True, out_dtype=BF16, tm=1024, tn=1024, tk=2048, name="mm_dw_o")
    send_b = [dw_o.reshape(N_CHIP, -1, d), dw_up, dw_down.reshape(N_CHIP, -1, d)]
    rs_b = _exchange_start(send_b, True, "rs_b_start")
    dmix = _behind(dmix, rs_b)
    dcat = _matmul(dmix, w_o_f, tb=True, out_dtype=F32, tm=1024, tn=1024, tk=2048, name="mm_dcat")
    dp_b, dp_c, dp_i, s_mix = _mixer_bwd(dcat, proj, cw_mix, conv_b_mix, lb, w_att)
    dob, stats = _attn_bwd_prep(cat, dcat, lse2, hh)
    dq_raw, dkv_k, dkv_v, dkr_h = _attn_bwd(q_c, k_c, kv_p, dob, stats, hh)
    dkv_p = jnp.concatenate([dkv_k, dkv_v], axis=1)
    dq_p = _dq_unrope(dq_raw, tabs, hh)
    dw_uq_p = _matmul(qn, dq_p, ta=True, out_dtype=BF16, tm=1024, tn=1024, tk=1024, name="mm_dw_uq")
    dqn = _matmul(dq_p, w_uq_p, tb=True, out_dtype=F32, tm=1024, tn=1024, tk=2048, name="mm_dqn")
    dw_ukv_p = _matmul(kvn, dkv_p, ta=True, out_dtype=BF16, tm=1024, tn=1024, tk=1024, name="mm_dw_ukv")
    dkvn = _matmul(dkv_p, w_ukv_p, tb=True, out_dtype=F32, tm=1024, tn=1024, tk=2048, name="mm_dkvn")
    dp_lat, s_lat = _latent_bwd(proj, dqn, dkvn, dkr_h, g_q, g_kv, tabs, lb)
    dproj = jnp.concatenate([dp_lat, dp_b, dp_c, dp_i], axis=1)
    dw_in_p = _matmul(h1, dproj, ta=True, out_dtype=BF16, tm=1024, tn=1536, tk=2048, name="mm_dw_in")

    dw_in_f = jnp.concatenate([dw_in_p[:, :lat], dw_in_p[:, lb:]], axis=1)
    uq3 = dw_uq_p.reshape(ql, 2, hh, LANES)
    dw_uq_f = jnp.concatenate([uq3[:, 0], uq3[:, 1, :, :ROPE]], axis=2).reshape(ql, hh * (NOPE + ROPE))
    ukv3 = dw_ukv_p.reshape(kl, 2, hh, LANES)
    dw_ukv_f = jnp.concatenate([ukv3[:, 0], ukv3[:, 1]], axis=2).reshape(kl, hh * (NOPE + VDIM))
    send_a = [_cols_to_shards(dw_in_f), _cols_to_shards(dw_uq_f), _cols_to_shards(dw_ukv_f)]
    rs_a = _exchange_start(send_a, True, "rs_a_start")
    dproj = _behind(dproj, rs_a)

    dh1 = _matmul(dproj, w_in_p, tb=True, out_dtype=F32, tm=512, tn=1024, tk=4608, name="mm_dh1")
    grad_x, s_first = _first_bwd(dh1, dx1, x0, g_pre_mix, sc_m)

    names = ["w_in", "w_uq", "w_ukv", "w_o", "w_up", "w_down"]
    sent_b, landed_b = _exchange_wait(rs_b, s_first, True, "rs_b_wait")
    sent_a, landed_a = _exchange_wait(rs_a, landed_b[0], True, "rs_a_wait")
    landed_a, s_first = lax.optimization_barrier((landed_a, s_first))
    part = [_sum_chips(l, a, "sum_chips_" + n) for l, a, n in zip(landed_a + landed_b, sent_a + sent_b, names)]

    dmod = jnp.concatenate([s_first[0:1], s_first[1:2], s_mid[3:4], s_mid[0:1], s_mid[1:2], s_fin[0:1]], axis=1)
    small = [
        dmod,
        s_first[2:3],
        s_mid[4:5],
        s_lat[0:1, :ql],
        s_lat[0:1, ql : ql + kl],
        s_mix[3:4],
        s_mid[2:3],
        s_fin[1:2],
        s_ffn[3:4],
        s_mix[0:3].reshape(1, -1),
        s_ffn[0:3].reshape(1, -1),
        s_fin[3:4, :LANES],
    ]
    sizes = [a.shape[1] for a in small]
    offs = [0]
    for n in sizes:
        offs.append(offs[-1] + n)
    pack_g = _pad_to(jnp.concatenate(small, axis=1).reshape(-1), SUBLANES * LANES * SUBLANES).reshape(SUBLANES, -1)
    gathered = _all_gather8(pack_g, "ag8_small_grads")
    tot = _sum_devices(gathered).reshape(-1)
    part_of = lambda k: tot[offs[k] : offs[k + 1]]
    dmod_all = gathered.reshape(N_DEV, -1)[:, : N_MOD * d]
    loss = part_of(11)[0]

    g_b_ada = part_of(0).reshape(1, -1)
    g_vecs = [part_of(k).reshape(1, -1) for k in range(1, 9)]
    g_cw_mix = lax.dynamic_slice(part_of(9).reshape(CONV_K, cwid), (0, chip * (cwid // N_CHIP)), (CONV_K, cwid // N_CHIP))
    g_cw_ffn = lax.dynamic_slice(part_of(10).reshape(CONV_K, f2), (0, chip * (f2 // N_CHIP)), (CONV_K, f2 // N_CHIP))

    swap = _swap_start(part, "swap_start")
    dm_cols = _behind(lax.dynamic_slice(dmod_all, (0, chip * nc_ada), (N_DEV, nc_ada)), swap)
    g_w_ada = _ada_grad(
        jnp.pad(c_act.T, ((0, 0), (0, LANES - N_DEV))), jnp.pad(dm_cols, ((0, LANES - N_DEV), (0, 0)))
    )
    big = {"w_ada": [a[None] for a in _adamw(w_ada[0], m_w_ada[0], v_w_ada[0], [g_w_ada], "adamw_w_ada")]}
    part, other = _swap_wait(swap, big["w_ada"][1], "swap_wait")

    big_w = [w_in, w_uq, w_ukv, w_o, w_up, w_down]
    big_m = [m_w_in, m_w_uq, m_w_ukv, m_w_o, m_w_up, m_w_down]
    big_v = [v_w_in, v_w_uq, v_w_ukv, v_w_o, v_w_up, v_w_down]
    for n, w_, m_, v_, p_, o_ in zip(names, big_w, big_m, big_v, part, other):
        big[n] = [a[None] for a in _adamw(w_[0], m_[0], v_[0], [p_, o_], "adamw_" + n)]

    sm_names = ["b_ada", "g_pre_mix", "g_post_mix", "g_q", "g_kv", "conv_b_mix", "g_pre_ffn", "g_post_ffn", "conv_b_ffn",
                "conv_w_mix", "conv_w_ffn"]
    sm_w = [b_ada, g_pre_mix, g_post_mix, g_q, g_kv, conv_b_mix, g_pre_ffn, g_post_ffn, conv_b_ffn, conv_w_mix, conv_w_ffn]
    sm_m = [m_b_ada, m_g_pre_mix, m_g_post_mix, m_g_q, m_g_kv, m_conv_b_mix, m_g_pre_ffn, m_g_post_ffn, m_conv_b_ffn,
            m_conv_w_mix, m_conv_w_ffn]
    sm_v = [v_b_ada, v_g_pre_mix, v_g_post_mix, v_g_q, v_g_kv, v_conv_b_mix, v_g_pre_ffn, v_g_post_ffn, v_conv_b_ffn,
            v_conv_w_mix, v_conv_w_ffn]
    sm_g = [g_b_ada] + g_vecs + [g_cw_mix, g_cw_ffn]
    flat = lambda arrs: jnp.concatenate([a.reshape(1, -1) for a in arrs], axis=1)
    sm_out = _adamw(flat(sm_w), flat(sm_m), flat(sm_v), [flat(sm_g)], "adamw_small")
    sm = {}
    off = 0
    for n, w_ in zip(sm_names, sm_w):
        sm[n] = [o[:, off : off + w_.size].reshape(w_.shape) for o in sm_out]
        off += w_.size

    order = ["w_ada", "b_ada", "g_pre_mix", "g_post_mix", "w_in", "g_q", "w_uq", "g_kv", "w_ukv", "conv_w_mix", "conv_b_mix",
             "w_o", "g_pre_ffn", "g_post_ffn", "w_up", "conv_w_ffn", "conv_b_ffn", "w_down"]
    res = {**big, **sm}
    outs = [loss + sum(anchors), grad_x.reshape(x.shape)]
    for k in range(4):
        outs += [res[n][k] for n in order]
    return tuple(outs)
```

```python
import math

import jax
import jax.numpy as jnp
from jax import lax
from jax.experimental import pallas as pl
from jax.experimental.pallas import tpu as pltpu

F32 = jnp.float32
BF16 = jnp.bfloat16
MESH = pl.DeviceIdType.MESH

N_DEV = 8
N_CHIP = 4
LANES = 128
SUBLANES = 8
VMEM_LIMIT = 56 * 2**20

NOPE = 128
ROPE = 64
VDIM = 128
HEAD_PAD = 128
ROPE_THETA = 10000.0
RMS_EPS = 1e-6
N_MOD = 6
CONV_K = 3
ATT_FWD_BLOCK, ATT_FWD_SUB = 2048, 256
ATT_BWD_BLOCK, ATT_BWD_SUB = 1024, 256
NEG = -1e30

ADAM_LR = 0.001
ADAM_B1 = 0.9
ADAM_B2 = 0.999
ADAM_EPS = 1e-08
ADAM_WD = 0.01
ADAM_STEP = 10


def _tile(n, pref, align):
    if n <= pref:
        return n
    t = (pref // align) * align
    while t >= align:
        if n % t == 0:
            return t
        t -= align
    return n


def _cp(*sem):
    return pltpu.CompilerParams(dimension_semantics=sem, vmem_limit_bytes=VMEM_LIMIT)


def _rsq(x):
    return lax.rsqrt(jnp.mean(x * x, axis=-1, keepdims=True) + RMS_EPS)


def _norm_bwd(dn, n, r):
    return r * (dn - n * jnp.mean(dn * n, axis=-1, keepdims=True))


def _colsum(a):
    return jnp.sum(a, axis=0, keepdims=True)


def _matmul(a, b, *, ta=False, tb=False, out_dtype, tm, tn, tk, name, b_n_perm=None, out_n_perm=None,
            b_col_shards=False, out_col_shards=False):
    assert not (b_col_shards and tb)
    if b_col_shards:
        b_rows, b_cols = b.shape[1], N_CHIP * b.shape[2]
    else:
        b_rows, b_cols = b.shape
    (k_a, m) = a.shape if ta else a.shape[::-1]
    (n, k_b) = (b_rows, b_cols) if tb else (b_cols, b_rows)
    assert k_a == k_b, (a.shape, b.shape, ta, tb)
    tm, tn, tk = _tile(m, tm, LANES), _tile(n, tn, LANES), _tile(k_a, tk, LANES)
    nk = k_a // tk
    same = lambda t: t
    bn, on = b_n_perm or same, out_n_perm or same
    a_spec = pl.BlockSpec((tk, tm), lambda i, j, k: (k, i)) if ta else pl.BlockSpec((tm, tk), lambda i, j, k: (i, k))
    if b_col_shards:
        per = (b_cols // N_CHIP) // tn
        b_spec = pl.BlockSpec((None, tk, tn), lambda i, j, k: (bn(j) // per, k, bn(j) % per))
    elif tb:
        b_spec = pl.BlockSpec((tn, tk), lambda i, j, k: (bn(j), k))
    else:
        b_spec = pl.BlockSpec((tk, tn), lambda i, j, k: (k, bn(j)))
    if out_col_shards:
        per_o = (n // N_CHIP) // tn
        out_shape = jax.ShapeDtypeStruct((N_CHIP, m, n // N_CHIP), out_dtype)
        out_spec = pl.BlockSpec((None, tm, tn), lambda i, j, k: (on(j) // per_o, i, on(j) % per_o))
    else:
        out_shape = jax.ShapeDtypeStruct((m, n), out_dtype)
        out_spec = pl.BlockSpec((tm, tn), lambda i, j, k: (i, on(j)))
    dims = (((0 if ta else 1,), (1 if tb else 0,)), ((), ()))

    def body(a_ref, b_ref, o_ref, *acc):
        p = lax.dot_general(a_ref[...].astype(BF16), b_ref[...].astype(BF16), dims, preferred_element_type=F32)
        _accumulate(p, o_ref, acc, nk)

    return pl.pallas_call(
        body,
        name=name,
        out_shape=out_shape,
        grid=(m // tm, n // tn, nk),
        in_specs=[a_spec, b_spec],
        out_specs=out_spec,
        scratch_shapes=[] if nk == 1 else [pltpu.VMEM((tm, tn), F32)],
        compiler_params=_cp("parallel", "parallel", "arbitrary"),
    )(a, b)


def _accumulate(p, o_ref, acc, nk):
    if nk == 1:
        o_ref[...] = p.astype(o_ref.dtype)
        return
    k = pl.program_id(2)

    @pl.when(k == 0)
    def _():
        acc[0][...] = p

    @pl.when(k > 0)
    def _():
        acc[0][...] += p

    @pl.when(k == nk - 1)
    def _():
        o_ref[...] = acc[0][...].astype(o_ref.dtype)


def _matmul_pair_k(a, b_shards, *, out_dtype, tm, tn, name):
    m, f2 = a.shape
    n = b_shards.shape[1]
    tc = _pair_tile(f2 // 2)
    nk = (f2 // 2) // tc
    per = (f2 // N_CHIP) // tc
    tm, tn = _tile(m, tm, LANES), _tile(n, tn, LANES)

    def body(a_ref, ba_ref, bg_ref, o_ref, *acc):
        av = a_ref[...]
        p = lax.dot_general(av[:, :tc], ba_ref[...], NT, preferred_element_type=F32)
        p = p + lax.dot_general(av[:, tc:], bg_ref[...], NT, preferred_element_type=F32)
        _accumulate(p, o_ref, acc, nk)

    def w_tile(first):
        return pl.BlockSpec((None, tn, tc), lambda i, j, k: ((first + k) // per, j, (first + k) % per))

    return pl.pallas_call(
        body,
        name=name,
        out_shape=jax.ShapeDtypeStruct((m, n), out_dtype),
        grid=(m // tm, n // tn, nk),
        in_specs=[pl.BlockSpec((tm, 2 * tc), lambda i, j, k: (i, k)), w_tile(0), w_tile(nk)],
        out_specs=pl.BlockSpec((tm, tn), lambda i, j, k: (i, j)),
        scratch_shapes=[] if nk == 1 else [pltpu.VMEM((tm, tn), F32)],
        compiler_params=_cp("parallel", "parallel", "arbitrary"),
    )(a, b_shards, b_shards)


def _rope_tables(pos_col, invf):
    s = pos_col.shape[0]
    ts = _tile(s, 1024, SUBLANES)
    half = ROPE // 2

    def body(p_ref, f_ref, c_ref, sa_ref, sb_ref):
        ang = p_ref[...] * f_ref[...]
        lane = lax.broadcasted_iota(jnp.int32, ang.shape, 1)
        cs, sn = jnp.cos(ang), jnp.sin(ang)
        c_ref[...] = jnp.where(lane < ROPE, cs, 0.0)
        sa_ref[...] = jnp.where((lane >= half) & (lane < ROPE), sn, 0.0)
        sb_ref[...] = jnp.where(lane < half, -sn, 0.0)

    tab = jax.ShapeDtypeStruct((s, LANES), F32)
    return pl.pallas_call(
        body,
        name="rope_tables",
        out_shape=(tab, tab, tab),
        grid=(s // ts,),
        in_specs=[pl.BlockSpec((ts, 1), lambda i: (i, 0)), pl.BlockSpec((1, LANES), lambda i: (0, 0))],
        out_specs=[pl.BlockSpec((ts, LANES), lambda i: (i, 0))] * 3,
        compiler_params=_cp("parallel"),
    )(pos_col, invf)


def _widen(t, w):
    return t if w == LANES else jnp.tile(t, (1, w // LANES))


def _rope(x, c, sa, sb):
    w = x.shape[1]
    c, sa, sb = _widen(c, w), _widen(sa, w), _widen(sb, w)
    return x * c + pltpu.roll(x, ROPE // 2, 1) * sa + pltpu.roll(x, w - ROPE // 2, 1) * sb


def _rope_t(d, c, sa, sb):
    w = d.shape[1]
    c, sa, sb = _widen(c, w), _widen(sa, w), _widen(sb, w)
    return d * c + pltpu.roll(d * sa, w - ROPE // 2, 1) + pltpu.roll(d * sb, ROPE // 2, 1)


def _ada_fwd(c_all, w, b):
    d, nc = w.shape
    tn = _tile(nc, 512, LANES)

    def body(c_ref, w_ref, b_ref, o_ref, ca_ref):
        cv = c_ref[...]
        ca = cv * jax.nn.sigmoid(cv)
        ca_ref[...] = ca
        o_ref[...] = jnp.dot(ca.astype(BF16), w_ref[...].astype(BF16), preferred_element_type=F32) + b_ref[...]

    return pl.pallas_call(
        body,
        name="ada_fwd",
        out_shape=(jax.ShapeDtypeStruct((N_DEV, nc), F32), jax.ShapeDtypeStruct((N_DEV, d), F32)),
        grid=(nc // tn,),
        in_specs=[
            pl.BlockSpec((N_DEV, d), lambda j: (0, 0)),
            pl.BlockSpec((d, tn), lambda j: (0, j)),
            pl.BlockSpec((1, tn), lambda j: (0, j)),
        ],
        out_specs=[pl.BlockSpec((N_DEV, tn), lambda j: (0, j)), pl.BlockSpec((N_DEV, d), lambda j: (0, 0))],
        compiler_params=_cp("arbitrary"),
    )(c_all, w, b)


def _rows(ts, d):
    return pl.BlockSpec((ts, d), lambda i: (i, 0))


def _vec(d):
    return pl.BlockSpec((1, d), lambda i: (0, 0))


def _sums(d):
    return pl.BlockSpec((SUBLANES, d), lambda i: (0, 0))


def _acc_rows(ref, i, rows):
    @pl.when(i == 0)
    def _():
        ref[...] = jnp.zeros(ref.shape, ref.dtype)

    for k, r in enumerate(rows):
        ref[k : k + 1, :] += r


def _pre_fwd(x, g, sc, sh):
    s, d = x.shape
    ts = _tile(s, 512, SUBLANES)

    def body(x_ref, g_ref, sc_ref, sh_ref, h_ref):
        xv = x_ref[...]
        h_ref[...] = (((xv * _rsq(xv)) * g_ref[...]) * (1.0 + sc_ref[...]) + sh_ref[...]).astype(BF16)

    return pl.pallas_call(
        body,
        name="pre_mix_fwd",
        out_shape=jax.ShapeDtypeStruct((s, d), BF16),
        grid=(s // ts,),
        in_specs=[_rows(ts, d), _vec(d), _vec(d), _vec(d)],
        out_specs=_rows(ts, d),
        compiler_params=_cp("parallel"),
    )(x, g, sc, sh)


def _mid_fwd(x0, mix, g_post, gt, g_pre, sc, sh):
    s, d = x0.shape
    ts = _tile(s, 256, SUBLANES)

    def body(x_ref, m_ref, gp_ref, gt_ref, g_ref, sc_ref, sh_ref, x1_ref, h_ref):
        mv = m_ref[...]
        x1 = x_ref[...] + gt_ref[...] * ((mv * _rsq(mv)) * gp_ref[...])
        x1_ref[...] = x1
        h_ref[...] = (((x1 * _rsq(x1)) * g_ref[...]) * (1.0 + sc_ref[...]) + sh_ref[...]).astype(BF16)

    return pl.pallas_call(
        body,
        name="mid_fwd",
        out_shape=(jax.ShapeDtypeStruct((s, d), F32), jax.ShapeDtypeStruct((s, d), BF16)),
        grid=(s // ts,),
        in_specs=[_rows(ts, d), _rows(ts, d)] + [_vec(d)] * 5,
        out_specs=[_rows(ts, d), _rows(ts, d)],
        compiler_params=_cp("parallel"),
    )(x0, mix, g_post, gt, g_pre, sc, sh)


def _final(x1, y, tgt, g_post, gt):
    s, d = x1.shape
    ts = _tile(s, 256, SUBLANES)
    ni = s // ts

    def body(x_ref, y_ref, t_ref, gp_ref, gt_ref, dx_ref, dy_ref, s_ref):
        i = pl.program_id(0)
        yv, gp, gt_v = y_ref[...], gp_ref[...], gt_ref[...]
        r = _rsq(yv)
        n = yv * r
        err = (x_ref[...] + gt_v * (n * gp)) - t_ref[...]
        dx = err * (1.0 / d)
        dx_ref[...] = dx
        dy_ref[...] = _norm_bwd(dx * (gt_v * gp), n, r).astype(BF16)
        _acc_rows(s_ref, i, [_colsum(dx * (n * gp)), _colsum(dx * gt_v * n), _colsum(err * err)])

        @pl.when(i == ni - 1)
        def _():
            tot = jnp.sum(s_ref[2:3, :], axis=1, keepdims=True) * (0.5 / d)
            s_ref[3:4, :] = jnp.broadcast_to(tot, (1, d))

    return pl.pallas_call(
        body,
        name="final_fwd_bwd",
        out_shape=(
            jax.ShapeDtypeStruct((s, d), F32),
            jax.ShapeDtypeStruct((s, d), BF16),
            jax.ShapeDtypeStruct((SUBLANES, d), F32),
        ),
        grid=(ni,),
        in_specs=[_rows(ts, d)] * 3 + [_vec(d)] * 2,
        out_specs=[_rows(ts, d), _rows(ts, d), _sums(d)],
        compiler_params=_cp("arbitrary"),
    )(x1, y, tgt, g_post, gt)


def _mid_bwd(dh2, dx2, x1, mix, g_pre, sc, g_post, gt):
    s, d = x1.shape
    ts = _tile(s, 256, SUBLANES)

    def body(dh_ref, dx2_ref, x_ref, m_ref, g_ref, sc_ref, gp_ref, gt_ref, dx1_ref, dm_ref, s_ref):
        i = pl.program_id(0)
        dh, xv, mv = dh_ref[...], x_ref[...], m_ref[...]
        g, sc_v, gp, gt_v = g_ref[...], sc_ref[...], gp_ref[...], gt_ref[...]
        r1 = _rsq(xv)
        n1 = xv * r1
        dx1 = dx2_ref[...] + _norm_bwd(dh * (g * (1.0 + sc_v)), n1, r1)
        dx1_ref[...] = dx1
        rm = _rsq(mv)
        nm = mv * rm
        dm_ref[...] = _norm_bwd(dx1 * (gt_v * gp), nm, rm).astype(BF16)
        _acc_rows(
            s_ref,
            i,
            [
                _colsum(dh),
                _colsum(dh * (n1 * g)),
                _colsum(dh * (1.0 + sc_v) * n1),
                _colsum(dx1 * (nm * gp)),
                _colsum(dx1 * gt_v * nm),
            ],
        )

    return pl.pallas_call(
        body,
        name="mid_bwd",
        out_shape=(
            jax.ShapeDtypeStruct((s, d), F32),
            jax.ShapeDtypeStruct((s, d), BF16),
            jax.ShapeDtypeStruct((SUBLANES, d), F32),
        ),
        grid=(s // ts,),
        in_specs=[_rows(ts, d)] * 4 + [_vec(d)] * 4,
        out_specs=[_rows(ts, d), _rows(ts, d), _sums(d)],
        compiler_params=_cp("arbitrary"),
    )(dh2, dx2, x1, mix, g_pre, sc, g_post, gt)


def _first_bwd(dh1, dx1, x0, g, sc):
    s, d = x0.shape
    ts = _tile(s, 256, SUBLANES)

    def body(dh_ref, dx1_ref, x_ref, g_ref, sc_ref, dx_ref, s_ref):
        i = pl.program_id(0)
        dh, xv, gv, sc_v = dh_ref[...], x_ref[...], g_ref[...], sc_ref[...]
        r = _rsq(xv)
        n = xv * r
        dx_ref[...] = dx1_ref[...] + _norm_bwd(dh * (gv * (1.0 + sc_v)), n, r)
        _acc_rows(s_ref, i, [_colsum(dh), _colsum(dh * (n * gv)), _colsum(dh * (1.0 + sc_v) * n)])

    return pl.pallas_call(
        body,
        name="first_bwd",
        out_shape=(jax.ShapeDtypeStruct((s, d), F32), jax.ShapeDtypeStruct((SUBLANES, d), F32)),
        grid=(s // ts,),
        in_specs=[_rows(ts, d)] * 3 + [_vec(d)] * 2,
        out_specs=[_rows(ts, d), _sums(d)],
        compiler_params=_cp("arbitrary"),
    )(dh1, dx1, x0, g, sc)


def _latent_fwd(proj, g_q, g_kv, tabs, lb):
    s = proj.shape[0]
    ql, kl = g_q.shape[1], g_kv.shape[1]
    ts = _tile(s, 512, SUBLANES)

    def body(p_ref, gq_ref, gk_ref, c_ref, sa_ref, sb_ref, q_ref, kv_ref, kr_ref):
        pv = p_ref[...]
        q, kv, kr = pv[:, :ql], pv[:, ql : ql + kl], pv[:, ql + kl : ql + kl + HEAD_PAD]
        q_ref[...] = ((q * _rsq(q)) * gq_ref[...]).astype(BF16)
        kv_ref[...] = ((kv * _rsq(kv)) * gk_ref[...]).astype(BF16)
        kr_ref[...] = _rope(kr, c_ref[...], sa_ref[...], sb_ref[...]).astype(BF16)

    return pl.pallas_call(
        body,
        name="latent_fwd",
        out_shape=(
            jax.ShapeDtypeStruct((s, ql), BF16),
            jax.ShapeDtypeStruct((s, kl), BF16),
            jax.ShapeDtypeStruct((s, HEAD_PAD), BF16),
        ),
        grid=(s // ts,),
        in_specs=[_rows(ts, lb), _vec(ql), _vec(kl)] + [_rows(ts, LANES)] * 3,
        out_specs=[_rows(ts, ql), _rows(ts, kl), _rows(ts, HEAD_PAD)],
        compiler_params=_cp("parallel"),
    )(proj, g_q, g_kv, *tabs)


def _latent_bwd(proj, dqn, dkvn, dkr_h, g_q, g_kv, tabs, lb):
    s = proj.shape[0]
    ql, kl = g_q.shape[1], g_kv.shape[1]
    hw = dkr_h.shape[1]
    ts = _tile(s, 256, SUBLANES)
    pad = lb - ql - kl - HEAD_PAD

    def body(p_ref, dq_ref, dkv_ref, dkr_ref, gq_ref, gk_ref, c_ref, sa_ref, sb_ref, o_ref, s_ref):
        i = pl.program_id(0)
        pv = p_ref[...]
        q, kv = pv[:, :ql], pv[:, ql : ql + kl]
        dqn_v, dkvn_v = dq_ref[...], dkv_ref[...]
        rq = _rsq(q)
        nq = q * rq
        rk = _rsq(kv)
        nk = kv * rk
        dkr = dkr_ref[:, :HEAD_PAD]
        for h in range(1, hw // HEAD_PAD):
            dkr = dkr + dkr_ref[:, h * HEAD_PAD : (h + 1) * HEAD_PAD]
        parts = [
            _norm_bwd(dqn_v * gq_ref[...], nq, rq).astype(BF16),
            _norm_bwd(dkvn_v * gk_ref[...], nk, rk).astype(BF16),
            _rope_t(dkr, c_ref[...], sa_ref[...], sb_ref[...]).astype(BF16),
        ]
        if pad:
            parts.append(jnp.zeros((ts, pad), BF16))
        o_ref[...] = jnp.concatenate(parts, axis=1)
        row = [_colsum(dqn_v * nq), _colsum(dkvn_v * nk), jnp.zeros((1, lb - ql - kl), F32)]
        _acc_rows(s_ref, i, [jnp.concatenate(row, axis=1)])

    return pl.pallas_call(
        body,
        name="latent_bwd",
        out_shape=(jax.ShapeDtypeStruct((s, lb), BF16), jax.ShapeDtypeStruct((SUBLANES, lb), F32)),
        grid=(s // ts,),
        in_specs=[_rows(ts, lb), _rows(ts, ql), _rows(ts, kl), _rows(ts, hw)]
        + [_vec(ql), _vec(kl)]
        + [_rows(ts, LANES)] * 3,
        out_specs=[_rows(ts, lb), _sums(lb)],
        compiler_params=_cp("arbitrary"),
    )(proj, dqn, dkvn, dkr_h, g_q, g_kv, *tabs)


def _conv3(ext, w, b):
    return (pltpu.roll(ext, 2, 0) * w[0:1] + pltpu.roll(ext, 1, 0) * w[1:2]) + ext * w[2:3] + b


def _conv3_t(du, w):
    n = du.shape[0]
    return du * w[2:3] + pltpu.roll(du, n - 1, 0) * w[1:2] + pltpu.roll(du, n - 2, 0) * w[0:1]


def _halo_maps(ts, s):
    r8, last = ts // SUBLANES, s // SUBLANES - 1
    prev = lambda i: jnp.maximum(i * r8 - 1, 0)
    nxt = lambda i: jnp.minimum((i + 1) * r8, last)
    return prev, nxt


def _mixer_fwd(cat, proj, cw, cb, lb, col0):
    s = proj.shape[0]
    cwid = cw.shape[1]
    ts = _tile(s, 512, SUBLANES)
    tc = _tile(cwid, 512, LANES)
    assert lb % tc == 0 and col0 % tc == 0
    nj, ob, oc = cwid // tc, lb // tc, col0 // tc
    prev, _ = _halo_maps(ts, s)

    def body(_, gb_ref, gc_ref, ci_ref, pgc_ref, pci_ref, w_ref, b_ref, o_ref):
        keep = jnp.where(pl.program_id(1) > 0, 1.0, 0.0)
        ext = jnp.concatenate([pgc_ref[...] * pci_ref[...] * keep, gc_ref[...] * ci_ref[...]], axis=0)
        o_ref[...] = (gb_ref[...] * _conv3(ext, w_ref[...], b_ref[...])[SUBLANES:]).astype(BF16)

    def col(k):
        return pl.BlockSpec((ts, tc), lambda j, i: (i, ob + k * nj + j))

    def halo(k):
        return pl.BlockSpec((SUBLANES, tc), lambda j, i: (prev(i), ob + k * nj + j))

    return pl.pallas_call(
        body,
        name="mixer_fwd",
        out_shape=jax.ShapeDtypeStruct(cat.shape, BF16),
        grid=(nj, s // ts),
        in_specs=[pl.BlockSpec(memory_space=pl.ANY), col(0), col(1), col(2), halo(1), halo(2)]
        + [pl.BlockSpec((CONV_K, tc), lambda j, i: (0, j)), pl.BlockSpec((1, tc), lambda j, i: (0, j))],
        out_specs=pl.BlockSpec((ts, tc), lambda j, i: (i, oc + j)),
        input_output_aliases={0: 0},
        compiler_params=_cp("parallel", "arbitrary"),
    )(cat, proj, proj, proj, proj, proj, cw, cb)


def _mixer_bwd(dcat, proj, cw, cb, lb, col0):
    s = proj.shape[0]
    cwid = cw.shape[1]
    ts = _tile(s, 256, SUBLANES)
    tc = _tile(cwid, 512, LANES)
    nj, ob, oc = cwid // tc, lb // tc, col0 // tc
    ni = s // ts
    prev, nxt = _halo_maps(ts, s)

    def body(d_ref, dn_ref, gb_ref, gbn_ref, gc_ref, gcp_ref, gcn_ref, ci_ref, cip_ref, cin_ref, w_ref, b_ref,
             dgb_ref, dgc_ref, dci_ref, s_ref):
        i = pl.program_id(1)
        keep_p = jnp.where(i > 0, 1.0, 0.0)
        keep_n = jnp.where(i < ni - 1, 1.0, 0.0)
        w = w_ref[...]
        gc = jnp.concatenate([gcp_ref[...], gc_ref[...], gcn_ref[...]], axis=0)
        ci = jnp.concatenate([cip_ref[...] * keep_p, ci_ref[...], cin_ref[...]], axis=0)
        u = gc * ci
        cv = _conv3(u, w, b_ref[...])[SUBLANES:]
        dco = jnp.concatenate([d_ref[...], dn_ref[...] * keep_n], axis=0)
        gb = jnp.concatenate([gb_ref[...], gbn_ref[...]], axis=0)
        dgb_ref[...] = (dco * cv)[:ts].astype(BF16)
        dcv = dco * gb
        du = _conv3_t(dcv, w)[:ts]
        dgc_ref[...] = (du * ci_ref[...]).astype(BF16)
        dci_ref[...] = (du * gc_ref[...]).astype(BF16)
        dt = dcv[:ts]
        u1, u2 = pltpu.roll(u, 1, 0), pltpu.roll(u, 2, 0)
        lo, hi = SUBLANES, SUBLANES + ts
        _acc_rows(s_ref, i, [_colsum(dt * u2[lo:hi]), _colsum(dt * u1[lo:hi]), _colsum(dt * u[lo:hi]), _colsum(dt)])

    def col(k):
        return pl.BlockSpec((ts, tc), lambda j, i: (i, ob + k * nj + j))

    def halo(k, which):
        return pl.BlockSpec((SUBLANES, tc), lambda j, i: (which(i), ob + k * nj + j))

    out_col = [pl.BlockSpec((ts, tc), lambda j, i: (i, j))] * 3
    grad = jax.ShapeDtypeStruct((s, cwid), BF16)
    return pl.pallas_call(
        body,
        name="mixer_bwd",
        out_shape=(grad, grad, grad, jax.ShapeDtypeStruct((SUBLANES, cwid), F32)),
        grid=(nj, ni),
        in_specs=[
            pl.BlockSpec((ts, tc), lambda j, i: (i, oc + j)),
            pl.BlockSpec((SUBLANES, tc), lambda j, i: (nxt(i), oc + j)),
            col(0), halo(0, nxt),
            col(1), halo(1, prev), halo(1, nxt),
            col(2), halo(2, prev), halo(2, nxt),
            pl.BlockSpec((CONV_K, tc), lambda j, i: (0, j)),
            pl.BlockSpec((1, tc), lambda j, i: (0, j)),
        ],
        out_specs=out_col + [pl.BlockSpec((SUBLANES, tc), lambda j, i: (0, j))],
        compiler_params=_cp("parallel", "arbitrary"),
    )(dcat, dcat, proj, proj, proj, proj, proj, proj, proj, proj, cw, cb)


def _pair_tile(f):
    return _tile(f, 1408, LANES)


def _pair_perm(f):
    nj = f // _pair_tile(f)
    return lambda p: (p % 2) * nj + p // 2


def _pair_cols(a):
    r, f2 = a.shape
    tc = _pair_tile(f2 // 2)
    return a.reshape(r, 2, f2 // (2 * tc), tc).transpose(0, 2, 1, 3).reshape(r, f2)


def _unpair_cols(a):
    r, f2 = a.shape
    tc = _pair_tile(f2 // 2)
    return a.reshape(r, f2 // (2 * tc), 2, tc).transpose(0, 2, 1, 3).reshape(r, f2)


def _ffn_act_fwd(up, cw, cb):
    s, f2 = up.shape
    f = f2 // 2
    ts = _tile(s, 512, SUBLANES)
    tc = _pair_tile(f)
    prev, _ = _halo_maps(ts, s)

    def body(u_ref, p_ref, w_ref, b_ref, o_ref):
        keep = jnp.where(pl.program_id(1) > 0, 1.0, 0.0)
        ext = jnp.concatenate([p_ref[...] * keep, u_ref[...]], axis=0)
        u = _conv3(ext, w_ref[...], b_ref[...])[SUBLANES:]
        a, g = u[:, :tc], u[:, tc:]
        o_ref[...] = ((g * jax.nn.sigmoid(g)) * a).astype(BF16)

    def pair(rows, which):
        return pl.BlockSpec((rows, 2 * tc), lambda j, i: (which(i), j))

    return pl.pallas_call(
        body,
        name="ffn_act_fwd",
        out_shape=jax.ShapeDtypeStruct((s, f), BF16),
        grid=(f // tc, s // ts),
        in_specs=[pair(ts, lambda i: i), pair(SUBLANES, prev), pair(CONV_K, lambda i: 0), pair(1, lambda i: 0)],
        out_specs=pl.BlockSpec((ts, tc), lambda j, i: (i, j)),
        compiler_params=_cp("parallel", "arbitrary"),
    )(up, up, cw, cb)


def _ffn_act_bwd(dact, up, cw, cb):
    s, f2 = up.shape
    f = f2 // 2
    ts = _tile(s, 256, SUBLANES)
    tc = _pair_tile(f)
    nj, ni = f // tc, s // ts
    prev, nxt = _halo_maps(ts, s)

    def body(d_ref, dn_ref, u_ref, up_ref, un_ref, w_ref, b_ref, dup_ref, s_ref):
        i = pl.program_id(1)
        keep_p = jnp.where(i > 0, 1.0, 0.0)
        keep_n = jnp.where(i < ni - 1, 1.0, 0.0)
        w = w_ref[...]
        ext = jnp.concatenate([up_ref[...] * keep_p, u_ref[...], un_ref[...]], axis=0)
        u = _conv3(ext, w, b_ref[...])[SUBLANES:]
        a, g = u[:, :tc], u[:, tc:]
        dact_v = jnp.concatenate([d_ref[...], dn_ref[...] * keep_n], axis=0)
        sg = jax.nn.sigmoid(g)
        du = jnp.concatenate([dact_v * (g * sg), dact_v * a * (sg * (1.0 + g * (1.0 - sg)))], axis=1)
        dup_ref[...] = _conv3_t(du, w)[:ts].astype(BF16)
        dt = du[:ts]
        lo, hi = SUBLANES, SUBLANES + ts
        e1, e2 = pltpu.roll(ext, 1, 0), pltpu.roll(ext, 2, 0)
        _acc_rows(s_ref, i, [_colsum(dt * e2[lo:hi]), _colsum(dt * e1[lo:hi]), _colsum(dt * ext[lo:hi]), _colsum(dt)])

    def pair(rows, which):
        return pl.BlockSpec((rows, 2 * tc), lambda j, i: (which(i), j))

    return pl.pallas_call(
        body,
        name="ffn_act_bwd",
        out_shape=(jax.ShapeDtypeStruct((s, f2), BF16), jax.ShapeDtypeStruct((SUBLANES, f2), F32)),
        grid=(nj, ni),
        in_specs=[
            pl.BlockSpec((ts, tc), lambda j, i: (i, j)),
            pl.BlockSpec((SUBLANES, tc), lambda j, i: (nxt(i), j)),
            pair(ts, lambda i: i), pair(SUBLANES, prev), pair(SUBLANES, nxt),
            pair(CONV_K, lambda i: 0), pair(1, lambda i: 0),
        ],
        out_specs=[pair(ts, lambda i: i), pair(SUBLANES, lambda i: 0)],
        compiler_params=_cp("parallel", "arbitrary"),
    )(dact, dact, up, up, up, cw, cb)


ATT_SCALE = 1.0 / math.sqrt(NOPE + ROPE)
LOG2E = math.log2(math.e)
ATT_C2 = ATT_SCALE * LOG2E
STAT_SPLIT = 64
NT = (((1,), (1,)), ((), ()))
TN = (((0,), (0,)), ((), ()))


def _head_cat(q, kv, kr, tabs, n_heads):
    s, w2 = q.shape
    w = w2 // 2
    ts = _tile(s, 512, SUBLANES)
    hd = NOPE + HEAD_PAD

    def body(q_ref, kv_ref, kr_ref, c_ref, sa_ref, sb_ref, qc_ref, kc_ref):
        qv = q_ref[...]
        qr = _rope(qv[:, w:], c_ref[...], sa_ref[...], sb_ref[...]).astype(BF16)
        krv = kr_ref[...]
        for h in range(n_heads):
            qc_ref[:, h * hd : h * hd + NOPE] = qv[:, h * NOPE : (h + 1) * NOPE].astype(BF16)
            qc_ref[:, h * hd + NOPE : (h + 1) * hd] = qr[:, h * HEAD_PAD : (h + 1) * HEAD_PAD]
            kc_ref[:, h * hd : h * hd + NOPE] = kv_ref[:, h * NOPE : (h + 1) * NOPE]
            kc_ref[:, h * hd + NOPE : (h + 1) * hd] = krv

    out = jax.ShapeDtypeStruct((s, n_heads * hd), BF16)
    return pl.pallas_call(
        body,
        name="head_cat",
        out_shape=(out, out),
        grid=(s // ts,),
        in_specs=[_rows(ts, w2), _rows(ts, w), _rows(ts, HEAD_PAD)] + [_rows(ts, LANES)] * 3,
        out_specs=[_rows(ts, n_heads * hd)] * 2,
        compiler_params=_cp("parallel"),
    )(q, kv, kr, *tabs)


def _attn_fwd(qc, kc, kv, n_heads, cat_cols):
    s = qc.shape[0]
    t = _tile(s, ATT_FWD_BLOCK, LANES)
    sub = _tile(t, ATT_FWD_SUB, LANES)
    hh = n_heads
    hd = NOPE + HEAD_PAD

    def body(q_ref, k_ref, v_ref, o_ref, lse_ref, m_s, l_s, acc_s):
        i = pl.program_id(1)
        m_s[...] = jnp.full(m_s.shape, NEG, F32)
        l_s[...] = jnp.zeros(l_s.shape, F32)
        acc_s[...] = jnp.zeros(acc_s.shape, F32)

        def chunk(k0, diag):
            m_all, l_all, acc_all = m_s[...], l_s[...], acc_s[...]
            new_m, new_l, new_acc = [], [], []

            def scores(r0):
                ncol = r0 + sub if diag else t
                return lax.dot_general(q_ref[pl.ds(r0, sub), :], k_ref[pl.ds(k0, ncol), :], NT, preferred_element_type=F32)

            sc_next = scores(0)
            for r0 in range(0, t, sub):
                ncol = r0 + sub if diag else t
                sc = sc_next
                if r0 + sub < t:
                    sc_next = scores(r0 + sub)
                if diag:
                    row = lax.broadcasted_iota(jnp.int32, sc.shape, 0) + r0
                    col = lax.broadcasted_iota(jnp.int32, sc.shape, 1)
                    sc = jnp.where(col <= row, sc, NEG)
                m_prev = m_all[r0 : r0 + sub]
                m_new = jnp.maximum(m_prev, jnp.max(sc, axis=1, keepdims=True))
                alpha = jnp.exp2((m_prev - m_new) * ATT_C2)
                p = jnp.exp2((sc - m_new) * ATT_C2)
                pv = jnp.dot(p.astype(BF16), v_ref[pl.ds(k0, ncol), :], preferred_element_type=F32)
                new_m.append(m_new)
                new_l.append(alpha * l_all[r0 : r0 + sub] + jnp.sum(p, axis=1, keepdims=True))
                new_acc.append(alpha * acc_all[r0 : r0 + sub] + pv)
            m_s[...] = jnp.concatenate(new_m, axis=0)
            l_s[...] = jnp.concatenate(new_l, axis=0)
            acc_s[...] = jnp.concatenate(new_acc, axis=0)

        def loop_body(k, carry):
            chunk(pl.multiple_of(k * t, t), False)
            return carry

        lax.fori_loop(0, i, loop_body, 0)
        chunk(pl.multiple_of(i * t, t), True)
        l = l_s[...]
        o_ref[...] = (acc_s[...] / l).astype(BF16)
        lse_ref[...] = jnp.broadcast_to(m_s[...] * ATT_C2 + jnp.log(l) * LOG2E, lse_ref.shape)

    return pl.pallas_call(
        body,
        name="attn_fwd",
        out_shape=(jax.ShapeDtypeStruct((s, cat_cols), BF16), jax.ShapeDtypeStruct((s, hh * LANES), F32)),
        grid=(hh, s // t),
        in_specs=[
            pl.BlockSpec((t, hd), lambda h, i: (i, h)),
            pl.BlockSpec((s, hd), lambda h, i: (0, h)),
            pl.BlockSpec((s, VDIM), lambda h, i: (0, hh + h)),
        ],
        out_specs=[pl.BlockSpec((t, VDIM), lambda h, i: (i, h)), pl.BlockSpec((t, LANES), lambda h, i: (i, h))],
        scratch_shapes=[pltpu.VMEM((t, 1), F32), pltpu.VMEM((t, 1), F32), pltpu.VMEM((t, VDIM), F32)],
        compiler_params=_cp("parallel", "parallel"),
    )(qc, kc, kv)


def _attn_bwd_prep(cat, dcat, lse2, n_heads):
    s, w = lse2.shape
    ts = _tile(s, 512, SUBLANES)

    def body(o_ref, do_ref, lse_ref, dob_ref, st_ref):
        do = do_ref[...]
        dob_ref[...] = do.astype(BF16)
        prod = do * o_ref[...].astype(F32)
        lane = lax.broadcasted_iota(jnp.int32, (ts, LANES), 1)
        for h in range(n_heads):
            cols = slice(h * LANES, (h + 1) * LANES)
            dsum = jnp.sum(prod[:, cols], axis=1, keepdims=True)
            st_ref[:, cols] = jnp.where(lane < STAT_SPLIT, lse_ref[:, cols], dsum)

    return pl.pallas_call(
        body,
        name="attn_bwd_prep",
        out_shape=(jax.ShapeDtypeStruct((s, w), BF16), jax.ShapeDtypeStruct((s, w), F32)),
        grid=(s // ts,),
        in_specs=[_rows(ts, w)] * 3,
        out_specs=[_rows(ts, w)] * 2,
        compiler_params=_cp("parallel"),
    )(cat, dcat, lse2)


def _attn_bwd(qc, kc, kv, dob, stats, n_heads):
    s = qc.shape[0]
    t = _tile(s, ATT_BWD_BLOCK, LANES)
    sub = _tile(t, ATT_BWD_SUB, LANES)
    nb = s // t
    hh = n_heads
    hd = NOPE + HEAD_PAD
    w = hh * LANES

    def body(q_ref, k_ref, v_ref, do_ref, st_ref, dq_ref, dkn_ref, dv_ref, dkr_ref, dk_s, dv_s):
        j = pl.program_id(1)

        @pl.when(j == 0)
        def _():
            dq_ref[...] = jnp.zeros(dq_ref.shape, F32)

        dk_s[...] = jnp.zeros(dk_s.shape, F32)
        dv_s[...] = jnp.zeros(dv_s.shape, F32)

        def pair(i0, diag):
            def width(r0):
                return r0 + sub if diag else t

            def products(r0):
                rows = pl.ds(i0 + r0, sub)
                sc_ = lax.dot_general(q_ref[rows, :], k_ref[0 : width(r0), :], NT, preferred_element_type=F32)
                dp_ = lax.dot_general(do_ref[rows, :], v_ref[0 : width(r0), :], NT, preferred_element_type=F32)
                return sc_, dp_

            nxt = products(0)
            for r0 in range(0, t, sub):
                ncol = width(r0)
                rows = pl.ds(i0 + r0, sub)
                kk = k_ref[0:ncol, :]
                qq, do, st = q_ref[rows, :], do_ref[rows, :], st_ref[rows, :]
                sc, dp = nxt
                if r0 + sub < t:
                    nxt = products(r0 + sub)
                if diag:
                    row = lax.broadcasted_iota(jnp.int32, sc.shape, 0) + r0
                    col = lax.broadcasted_iota(jnp.int32, sc.shape, 1)
                    sc = jnp.where(col <= row, sc, NEG)
                p = jnp.exp2(sc * ATT_C2 - st[:, 0:1])
                dv_s[0:ncol, :] += lax.dot_general(p.astype(BF16), do, TN, preferred_element_type=F32)
                ds = (p * (dp - st[:, STAT_SPLIT : STAT_SPLIT + 1]) * ATT_SCALE).astype(BF16)
                dk_s[0:ncol, :] += lax.dot_general(ds, qq, TN, preferred_element_type=F32)
                dq_ref[rows, :] += jnp.dot(ds, kk, preferred_element_type=F32)

        pair(pl.multiple_of(j * t, t), True)

        def loop_body(i, carry):
            pair(pl.multiple_of(i * t, t), False)
            return carry

        lax.fori_loop(j + 1, nb, loop_body, 0)
        dkn_ref[...] = dk_s[:, :NOPE].astype(BF16)
        dv_ref[...] = dv_s[...].astype(BF16)
        dkr_ref[...] = dk_s[:, NOPE:]

    whole = lambda width, off: pl.BlockSpec((s, width), lambda h, j: (0, off + h))
    blk = lambda width, off: pl.BlockSpec((t, width), lambda h, j: (j, off + h))
    return pl.pallas_call(
        body,
        name="attn_bwd",
        out_shape=(
            jax.ShapeDtypeStruct((s, hh * hd), F32),
            jax.ShapeDtypeStruct((s, w), BF16),
            jax.ShapeDtypeStruct((s, w), BF16),
            jax.ShapeDtypeStruct((s, w), F32),
        ),
        grid=(hh, nb),
        in_specs=[whole(hd, 0), blk(hd, 0), blk(VDIM, hh), whole(VDIM, 0), whole(LANES, 0)],
        out_specs=[whole(hd, 0), blk(NOPE, 0), blk(VDIM, 0), blk(HEAD_PAD, 0)],
        scratch_shapes=[pltpu.VMEM((t, hd), F32), pltpu.VMEM((t, VDIM), F32)],
        compiler_params=_cp("parallel", "arbitrary"),
    )(qc, kc, kv, dob, stats)


def _dq_unrope(dq, tabs, n_heads):
    s = dq.shape[0]
    hd = NOPE + HEAD_PAD
    w = n_heads * LANES
    ts = _tile(s, 512, SUBLANES)

    def body(d_ref, c_ref, sa_ref, sb_ref, o_ref):
        c, sa, sb = c_ref[...], sa_ref[...], sb_ref[...]
        for h in range(n_heads):
            o_ref[:, h * NOPE : (h + 1) * NOPE] = d_ref[:, h * hd : h * hd + NOPE].astype(BF16)
            rot = _rope_t(d_ref[:, h * hd + NOPE : (h + 1) * hd], c, sa, sb)
            o_ref[:, w + h * HEAD_PAD : w + (h + 1) * HEAD_PAD] = rot.astype(BF16)

    return pl.pallas_call(
        body,
        name="dq_unrope",
        out_shape=jax.ShapeDtypeStruct((s, 2 * w), BF16),
        grid=(s // ts,),
        in_specs=[_rows(ts, n_heads * hd)] + [_rows(ts, LANES)] * 3,
        out_specs=_rows(ts, 2 * w),
        compiler_params=_cp("parallel"),
    )(dq, *tabs)


def _adamw(w, m, v, grads, name):
    r, c = w.shape
    budget_rows = max(SUBLANES, (VMEM_LIMIT // 3) // (4 * c * 2 * (7 + len(grads))))
    tr = _tile(r, budget_rows, SUBLANES)
    ng = len(grads)
    c1 = 1.0 - ADAM_B1**ADAM_STEP
    c2 = 1.0 - ADAM_B2**ADAM_STEP

    def body(*refs):
        w_ref, m_ref, v_ref = refs[:3]
        g_ref, d_ref, nm_ref, nv_ref = refs[3 + ng :]
        g = refs[3][...]
        for extra in refs[4 : 3 + ng]:
            g = g + extra[...]
        mn = ADAM_B1 * m_ref[...] + (1.0 - ADAM_B1) * g
        vn = ADAM_B2 * v_ref[...] + (1.0 - ADAM_B2) * (g * g)
        g_ref[...] = g
        nm_ref[...] = mn
        nv_ref[...] = vn
        d_ref[...] = -ADAM_LR * ((mn / c1) / (jnp.sqrt(vn / c2) + ADAM_EPS) + ADAM_WD * w_ref[...])

    blk = pl.BlockSpec((tr, c), lambda i: (i, 0))
    out = jax.ShapeDtypeStruct((r, c), F32)
    return pl.pallas_call(
        body,
        name=name,
        out_shape=(out, out, out, out),
        grid=(r // tr,),
        in_specs=[blk] * (3 + ng),
        out_specs=[blk] * 4,
        compiler_params=_cp("parallel"),
    )(w, m, v, *grads)


def _ada_grad(ca_t, dm):
    d = ca_t.shape[0]
    nc = dm.shape[1]
    tn = _tile(nc, 512, LANES)

    def body(a_ref, b_ref, o_ref):
        o_ref[...] = jnp.dot(a_ref[...].astype(BF16), b_ref[...].astype(BF16), preferred_element_type=F32)

    return pl.pallas_call(
        body,
        name="ada_grad",
        out_shape=jax.ShapeDtypeStruct((d, nc), F32),
        grid=(nc // tn,),
        in_specs=[pl.BlockSpec((d, LANES), lambda j: (0, 0)), pl.BlockSpec((LANES, tn), lambda j: (0, j))],
        out_specs=pl.BlockSpec((d, tn), lambda j: (0, j)),
        compiler_params=_cp("parallel"),
    )(ca_t, dm)


def _sum_devices(g):
    n = g.shape[1]

    def body(g_ref, o_ref):
        acc = g_ref[0:SUBLANES, :]
        for dvc in range(1, N_DEV):
            acc = acc + g_ref[dvc * SUBLANES : (dvc + 1) * SUBLANES, :]
        o_ref[...] = acc

    return pl.pallas_call(
        body,
        name="sum_devices",
        out_shape=jax.ShapeDtypeStruct((SUBLANES, n), F32),
        in_specs=[pl.BlockSpec(memory_space=pltpu.VMEM)],
        out_specs=pl.BlockSpec(memory_space=pltpu.VMEM),
        compiler_params=pltpu.CompilerParams(vmem_limit_bytes=VMEM_LIMIT),
    )(g)


def _sum_chips(land, sent, name):
    _, r, c = land.shape
    tr = _tile(r, max(SUBLANES * 2, (VMEM_LIMIT // 4) // (c * 2 * (4 * N_CHIP + 4 * 2))), SUBLANES * 2)

    def body(l_ref, s_ref, o_ref):
        x, y, _ = _mesh_pos()
        me = 2 * x + y
        acc = jnp.where(me == 0, s_ref[0], l_ref[0]).astype(F32)
        for k in range(1, N_CHIP):
            acc = acc + jnp.where(me == k, s_ref[k], l_ref[k]).astype(F32)
        o_ref[...] = acc

    slots = pl.BlockSpec((N_CHIP, tr, c), lambda i: (0, i, 0))
    return pl.pallas_call(
        body,
        name=name,
        out_shape=jax.ShapeDtypeStruct((r, c), F32),
        grid=(r // tr,),
        in_specs=[slots, slots],
        out_specs=pl.BlockSpec((tr, c), lambda i: (i, 0)),
        compiler_params=_cp("parallel"),
    )(land, sent)


def _mesh_pos():
    return lax.axis_index("x"), lax.axis_index("y"), lax.axis_index("c")


def _other_chips(x, y):
    return [(1 - x, y), (x, 1 - y), (1 - x, 1 - y)]


def _all_gather8(x_shard, name):
    m_per, n = x_shard.shape

    def body(x_ref, out_ref, send_sems, recv_sems, local_sem):
        x, y, c = _mesh_pos()
        me, sibling = (x, y, c), (x, y, 1 - c)
        chips = _other_chips(x, y)

        def rows(px, py, pc):
            return out_ref.at[pl.ds((4 * px + 2 * py + pc) * m_per, m_per), :]

        def copy(k, block, to, src=None):
            return pltpu.make_async_remote_copy(
                src_ref=rows(*block) if src is None else src,
                dst_ref=rows(*block),
                send_sem=send_sems.at[k],
                recv_sem=recv_sems.at[k],
                device_id=to,
                device_id_type=MESH,
            )

        mine = pltpu.make_async_copy(x_ref, rows(*me), local_sem)
        mine.start()
        first = [copy(0, me, sibling, src=x_ref)]
        first += [copy(1 + j, me, (*chip, c), src=x_ref) for j, chip in enumerate(chips)]
        for cp in first:
            cp.start()
        passed = [copy(4 + j, (*chip, c), sibling) for j, chip in enumerate(chips)]
        for j, chip in enumerate(chips):
            copy(1 + j, (*chip, c), me).wait_recv()
            passed[j].start()
        copy(0, sibling, me).wait_recv()
        for j, chip in enumerate(chips):
            copy(4 + j, (*chip, 1 - c), me).wait_recv()
        for cp in first + passed:
            cp.wait_send()
        mine.wait()

    return pl.pallas_call(
        body,
        name=name,
        out_shape=jax.ShapeDtypeStruct((N_DEV * m_per, n), x_shard.dtype),
        in_specs=[pl.BlockSpec(memory_space=pltpu.VMEM)],
        out_specs=pl.BlockSpec(memory_space=pltpu.VMEM),
        scratch_shapes=[pltpu.SemaphoreType.DMA((7,)), pltpu.SemaphoreType.DMA((7,)), pltpu.SemaphoreType.DMA],
        compiler_params=pltpu.CompilerParams(vmem_limit_bytes=VMEM_LIMIT),
    )(x_shard)


HBM_SPEC = pl.BlockSpec(memory_space=pltpu.HBM)
SEM_SPEC = pl.BlockSpec(memory_space=pltpu.SEMAPHORE)
DATAFLOW = pltpu.SideEffectType.DATAFLOW_SIDE_EFFECTING


def _half_rows(n_rows, c):
    return pl.ds(c * (n_rows // 2), n_rows // 2)


def _exchange_copies(ins, lands, send_sems, recv_sems, scatter, halves=False):
    x, y, c = _mesh_pos()
    me = 2 * x + y
    sends, recvs = [], []
    for t in range(len(ins)):
        rows = _half_rows(ins[t].shape[0], c) if halves else slice(None)
        for r, (px, py) in enumerate(_other_chips(x, y)):
            peer = 2 * px + py

            def copy(src, dst, k=3 * t + r, to=(px, py, c)):
                return pltpu.make_async_remote_copy(
                    src_ref=src, dst_ref=dst, send_sem=send_sems.at[k], recv_sem=recv_sems.at[k], device_id=to, device_id_type=MESH
                )

            if scatter:
                sends.append(copy(ins[t].at[peer], lands[t].at[me]))
                recvs.append(copy(ins[t].at[me], lands[t].at[peer]))
            else:
                sends.append(copy(ins[t].at[rows], lands[t].at[me, rows]))
                recvs.append(copy(ins[t].at[rows], lands[t].at[peer, rows]))
    return sends, recvs


def _sibling_fill(lands, name):
    nt = len(lands)

    def body(*refs):
        outs, send_sems, recv_sems = refs[nt : 2 * nt], refs[2 * nt], refs[2 * nt + 1]
        x, y, c = _mesh_pos()
        sends, recvs = [], []
        for t in range(nt):
            mine, theirs = _half_rows(outs[t].shape[1], c), _half_rows(outs[t].shape[1], 1 - c)
            for r, (px, py) in enumerate(_other_chips(x, y)):
                slot = 2 * px + py

                def copy(rows, k=3 * t + r, zone=outs[t], slot=slot):
                    part = zone.at[slot, rows]
                    return pltpu.make_async_remote_copy(
                        src_ref=part, dst_ref=part, send_sem=send_sems.at[k], recv_sem=recv_sems.at[k],
                        device_id=(x, y, 1 - c), device_id_type=MESH,
                    )

                sends.append(copy(mine))
                recvs.append(copy(theirs))
        for cp in sends:
            cp.start()
        for cp in recvs:
            cp.wait_recv()
        for cp in sends:
            cp.wait_send()

    return pl.pallas_call(
        body,
        name=name,
        out_shape=tuple(jax.ShapeDtypeStruct(a.shape, a.dtype) for a in lands),
        in_specs=[pl.BlockSpec(memory_space=pl.ANY)] * nt,
        out_specs=[pl.BlockSpec(memory_space=pl.ANY)] * nt,
        input_output_aliases={t: t for t in range(nt)},
        scratch_shapes=[pltpu.SemaphoreType.DMA((3 * nt,)), pltpu.SemaphoreType.DMA((3 * nt,))],
    )(*lands)


def _exchange_start(arrs, scatter, name, halves=False):
    nt = len(arrs)
    lands = [lax.empty(a.shape if scatter else (N_CHIP, *a.shape), a.dtype) for a in arrs]

    def body(*refs):
        ins, zones = refs[:nt], refs[nt : 2 * nt]
        send_sems, recv_sems, token = refs[2 * nt], refs[2 * nt + 1], refs[-1]
        sends, _ = _exchange_copies(ins, zones, send_sems, recv_sems, scatter, halves)
        for cp in sends:
            cp.start()
        token[...] = jnp.zeros(token.shape, F32)

    bufs = list(arrs) + list(lands)
    return pl.pallas_call(
        body,
        name=name,
        out_shape=(
            pltpu.SemaphoreType.DMA((3 * nt,)),
            pltpu.SemaphoreType.DMA((3 * nt,)),
            *[pltpu.HBM(a.shape, a.dtype) for a in bufs],
            jax.ShapeDtypeStruct((SUBLANES, LANES), F32),
        ),
        in_specs=[HBM_SPEC] * (2 * nt),
        out_specs=(SEM_SPEC, SEM_SPEC, *[HBM_SPEC] * (2 * nt), pl.BlockSpec(memory_space=pltpu.VMEM)),
        input_output_aliases={k: 2 + k for k in range(2 * nt)},
        compiler_params=pltpu.CompilerParams(has_side_effects=DATAFLOW),
    )(*[pltpu.with_memory_space_constraint(a, pltpu.HBM) for a in bufs])


def _exchange_wait(state, after, scatter, name, halves=False):
    send_sems, recv_sems, *bufs = state[:-1]
    nt = len(bufs) // 2

    def body(*refs):
        ins, zones = refs[:nt], refs[nt : 2 * nt]
        sends, recvs = _exchange_copies(ins, zones, refs[2 * nt], refs[2 * nt + 1], scatter, halves)
        for cp in sends:
            cp.wait_send()
        for cp in recvs:
            cp.wait_recv()

    out = pl.pallas_call(
        body,
        name=name,
        out_shape=tuple(pltpu.HBM(a.shape, a.dtype) for a in bufs),
        in_specs=[HBM_SPEC] * (2 * nt) + [SEM_SPEC, SEM_SPEC, pl.BlockSpec(memory_space=pl.ANY)],
        out_specs=[HBM_SPEC] * (2 * nt),
        input_output_aliases={k: k for k in range(2 * nt)},
        compiler_params=pltpu.CompilerParams(has_side_effects=DATAFLOW),
    )(*bufs, send_sems, recv_sems, after)
    return list(out[:nt]), list(out[nt:])


def _swap_copies(ins, lands, send_sems, recv_sems):
    x, y, c = _mesh_pos()
    return [
        pltpu.make_async_remote_copy(
            src_ref=ins[t], dst_ref=lands[t], send_sem=send_sems.at[t], recv_sem=recv_sems.at[t],
            device_id=(x, y, 1 - c), device_id_type=MESH,
        )
        for t in range(len(ins))
    ]


def _swap_start(arrs, name):
    nt = len(arrs)
    lands = [lax.empty(a.shape, a.dtype) for a in arrs]

    def body(*refs):
        ins, zones = refs[:nt], refs[nt : 2 * nt]
        send_sems, recv_sems, token = refs[2 * nt], refs[2 * nt + 1], refs[-1]
        for cp in _swap_copies(ins, zones, send_sems, recv_sems):
            cp.start()
        token[...] = jnp.zeros(token.shape, F32)

    bufs = list(arrs) + lands
    return pl.pallas_call(
        body,
        name=name,
        out_shape=(
            pltpu.SemaphoreType.DMA((nt,)),
            pltpu.SemaphoreType.DMA((nt,)),
            *[pltpu.HBM(a.shape, a.dtype) for a in bufs],
            jax.ShapeDtypeStruct((SUBLANES, LANES), F32),
        ),
        in_specs=[HBM_SPEC] * (2 * nt),
        out_specs=(SEM_SPEC, SEM_SPEC, *[HBM_SPEC] * (2 * nt), pl.BlockSpec(memory_space=pltpu.VMEM)),
        input_output_aliases={k: 2 + k for k in range(2 * nt)},
        compiler_params=pltpu.CompilerParams(has_side_effects=DATAFLOW),
    )(*[pltpu.with_memory_space_constraint(a, pltpu.HBM) for a in bufs])


def _swap_wait(state, after, name):
    send_sems, recv_sems, *bufs = state[:-1]
    nt = len(bufs) // 2

    def body(*refs):
        cps = _swap_copies(refs[:nt], refs[nt : 2 * nt], refs[2 * nt], refs[2 * nt + 1])
        for cp in cps:
            cp.wait_send()
        for cp in cps:
            cp.wait_recv()

    out = pl.pallas_call(
        body,
        name=name,
        out_shape=tuple(pltpu.HBM(a.shape, a.dtype) for a in bufs),
        in_specs=[HBM_SPEC] * (2 * nt) + [SEM_SPEC, SEM_SPEC, pl.BlockSpec(memory_space=pl.ANY)],
        out_specs=[HBM_SPEC] * (2 * nt),
        input_output_aliases={k: k for k in range(2 * nt)},
        compiler_params=pltpu.CompilerParams(has_side_effects=DATAFLOW),
    )(*bufs, send_sems, recv_sems, after)
    return list(out[:nt]), list(out[nt:])


def _cols_from_shards(g):
    _, k, n = g.shape
    return jnp.transpose(g, (1, 0, 2)).reshape(k, N_CHIP * n)


def _cols_to_shards(a):
    k, n4 = a.shape
    return jnp.transpose(a.reshape(k, N_CHIP, n4 // N_CHIP), (1, 0, 2))


def _pad_to(vec, mult):
    n = vec.shape[0]
    return jnp.pad(vec, (0, (-n) % mult))


def kernel(x, c, positions, w_ada, b_ada, g_pre_mix, g_post_mix, w_in, g_q, w_uq, g_kv, w_ukv, conv_w_mix, conv_b_mix, w_o, g_pre_ffn, g_post_ffn, w_up, conv_w_ffn, conv_b_ffn, w_down, loss_target, m_w_ada, m_b_ada, m_g_pre_mix, m_g_post_mix, m_w_in, m_g_q, m_w_uq, m_g_kv, m_w_ukv, m_conv_w_mix, m_conv_b_mix, m_w_o, m_g_pre_ffn, m_g_post_ffn, m_w_up, m_conv_w_ffn, m_conv_b_ffn, m_w_down, v_w_ada, v_b_ada, v_g_pre_mix, v_g_post_mix, v_w_in, v_g_q, v_w_uq, v_g_kv, v_w_ukv, v_conv_w_mix, v_conv_b_mix, v_w_o, v_g_pre_ffn, v_g_post_ffn, v_w_up, v_conv_w_ffn, v_conv_b_ffn, v_w_down):
    xi, yi, ci = _mesh_pos()
    chip = 2 * xi + yi
    dev = 4 * xi + 2 * yi + ci

    s, d = x.shape[1], x.shape[2]
    ql, kl = g_q.shape[1], g_kv.shape[1]
    cwid = conv_b_mix.shape[1]
    f2 = conv_b_ffn.shape[1]
    hh = (w_uq.shape[2] * N_CHIP) // (NOPE + ROPE)
    w_att = hh * LANES
    nc_ada = w_ada.shape[2]
    lat = ql + kl + ROPE
    tc_mix = _tile(cwid, 512, LANES)
    lb = -(-(ql + kl + HEAD_PAD) // tc_mix) * tc_mix
    np_cols = lb + 3 * cwid
    assert cwid == hh * VDIM and w_att % tc_mix == 0

    x0 = x.reshape(s, d)
    tgt = loss_target.reshape(s, d)

    anchors = []

    def _behind(val, state):
        val, tok = lax.optimization_barrier((val, state[-1]))
        anchors.append(tok[0, 0])
        return val

    cwm_n, cwf_n = CONV_K * cwid // N_CHIP, CONV_K * f2 // N_CHIP
    pack_a = _pad_to(jnp.concatenate([c.reshape(-1), conv_w_mix.reshape(-1), conv_w_ffn.reshape(-1)]), SUBLANES * LANES)
    rows_a = _all_gather8(pack_a.reshape(SUBLANES, -1), "ag8_inputs").reshape(N_DEV, -1)
    c_all = rows_a[:, :d]
    south = rows_a[0::2]
    cw_mix = jnp.concatenate([south[j, d : d + cwm_n].reshape(CONV_K, -1) for j in range(N_CHIP)], axis=1)
    cw_ffn = jnp.concatenate([south[j, d + cwm_n : d + cwm_n + cwf_n].reshape(CONV_K, -1) for j in range(N_CHIP)], axis=1)

    b_cols = lax.dynamic_slice(b_ada, (0, chip * nc_ada), (1, nc_ada))
    mod_part, c_act = _ada_fwd(c_all, w_ada[0], b_cols)
    mod_rows = _all_gather8(mod_part, "ag8_mod")
    mod = jnp.concatenate(
        [lax.dynamic_slice_in_dim(mod_rows, 2 * N_DEV * j + dev, 1, axis=0) for j in range(N_CHIP)], axis=1
    )

    shards = [a[0].astype(BF16) for a in (w_in, w_uq, w_ukv, w_o, w_up, w_down)]
    first, mod = lax.optimization_barrier((shards[:3], mod))
    ag_a = _exchange_start(first, False, "ag_a_start", halves=True)
    mod = _behind(mod, ag_a)
    sh_m, sc_m, gt_m, sh_f, sc_f, gt_f = [mod[:, k * d : (k + 1) * d] for k in range(N_MOD)]

    inv_freq = 1.0 / (ROPE_THETA ** (jnp.arange(0, ROPE, 2, dtype=F32) / ROPE))
    invf = jnp.concatenate([inv_freq, inv_freq, jnp.zeros((LANES - ROPE,), F32)]).reshape(1, LANES)
    tabs = _rope_tables(positions.astype(F32).reshape(s, 1), invf)
    h1 = _pre_fwd(x0, g_pre_mix, sc_m, sh_m)

    def with_own(landed, own):
        return [lax.dynamic_update_slice_in_dim(g, a[None], chip, axis=0) for g, a in zip(landed, own)]

    own_w, landed_w = _exchange_wait(ag_a, h1, False, "ag_a_wait", halves=True)
    landed_w = list(_sibling_fill(landed_w, "ag_a_fill"))
    rest, landed_w = lax.optimization_barrier((shards[3:], landed_w))
    ag_b = _exchange_start(rest, False, "ag_b_start")
    h1 = _behind(h1, ag_b)
    g_in, g_uq, g_ukv = with_own(landed_w, own_w)
    full_in = _cols_from_shards(g_in)
    w_in_p = jnp.concatenate([full_in[:, :lat], jnp.zeros((d, lb - lat), BF16), full_in[:, lat:]], axis=1)
    full_uq = _cols_from_shards(g_uq).reshape(ql, hh, NOPE + ROPE)
    w_uq_p = jnp.concatenate(
        [
            full_uq[:, :, :NOPE].reshape(ql, w_att),
            jnp.pad(full_uq[:, :, NOPE:], ((0, 0), (0, 0), (0, HEAD_PAD - ROPE))).reshape(ql, w_att),
        ],
        axis=1,
    )
    full_ukv = _cols_from_shards(g_ukv).reshape(kl, hh, NOPE + VDIM)
    w_ukv_p = jnp.concatenate([full_ukv[:, :, :NOPE].reshape(kl, w_att), full_ukv[:, :, NOPE:].reshape(kl, w_att)], axis=1)

    proj = _matmul(h1, w_in_p, out_dtype=F32, tm=1024, tn=768, tk=2048, name="mm_proj")
    qn, kvn, kr = _latent_fwd(proj, g_q, g_kv, tabs, lb)
    q_f = _matmul(qn, w_uq_p, out_dtype=F32, tm=1024, tn=1024, tk=2048, name="mm_q")
    kv_p = _matmul(kvn, w_ukv_p, out_dtype=BF16, tm=1024, tn=1024, tk=2048, name="mm_kv")
    q_c, k_c = _head_cat(q_f, kv_p, kr, tabs, hh)
    cat, lse2 = _attn_fwd(q_c, k_c, kv_p, hh, w_att + cwid)
    cat = _mixer_fwd(cat, proj, cw_mix, conv_b_mix, lb, w_att)
    own_w, landed_w = _exchange_wait(ag_b, cat, False, "ag_b_wait")
    g_o, g_up, g_down = with_own(landed_w, own_w)
    w_o_f = g_o.reshape(-1, d)
    cw_ffn_p, cb_ffn_p = _pair_cols(cw_ffn), _pair_cols(conv_b_ffn)
    tcp, pair_perm = _pair_tile(f2 // 2), _pair_perm(f2 // 2)
    w_down_f = g_down.reshape(-1, d)
    mix = _matmul(cat, w_o_f, out_dtype=F32, tm=1024, tn=1024, tk=2048, name="mm_mix")

    x1, h2 = _mid_fwd(x0, mix, g_post_mix, gt_m, g_pre_ffn, sc_f, sh_f)
    up = _matmul(h2, g_up, out_dtype=F32, tm=1024, tn=tcp, tk=2048, name="mm_up", b_n_perm=pair_perm, b_col_shards=True)
    act = _ffn_act_fwd(up, cw_ffn_p, cb_ffn_p)
    y = _matmul(act, w_down_f, out_dtype=F32, tm=512, tn=1024, tk=5632, name="mm_down")
    dx2, dy, s_fin = _final(x1, y, tgt, g_post_ffn, gt_f)

    dw_down = _matmul(act, dy, ta=True, out_dtype=BF16, tm=1408, tn=1024, tk=2048, name="mm_dw_down")
    dact = _matmul(dy, w_down_f, tb=True, out_dtype=F32, tm=1024, tn=1408, tk=2048, name="mm_dact")
    dup, s_ffn_p = _ffn_act_bwd(dact, up, cw_ffn_p, cb_ffn_p)
    s_ffn = _unpair_cols(s_ffn_p)
    dw_up = _matmul(
        h2, dup, ta=True, out_dtype=BF16, tm=1024, tn=tcp, tk=2048, name="mm_dw_up", out_n_perm=pair_perm, out_col_shards=True
    )
    dh2 = _matmul_pair_k(dup, g_up, out_dtype=F32, tm=1024, tn=1024, name="mm_dh2")
    dx1, dmix, s_mid = _mid_bwd(dh2, dx2, x1, mix, g_pre_ffn, sc_f, g_post_mix, gt_m)

    dw_o = _matmul(cat, dmix, ta=True, out_dtype=BF16, tm=1024, tn=1024, tk=2048, name="mm_dw_o")
    send_b = [dw_o.reshape(N_CHIP, -1, d), dw_up, dw_down.reshape(N_CHIP, -1, d)]
    rs_b = _exchange_start(send_b, True, "rs_b_start")
    dmix = _behind(dmix, rs_b)
    dcat = _matmul(dmix, w_o_f, tb=True, out_dtype=F32, tm=1024, tn=1024, tk=2048, name="mm_dcat")
    dp_b, dp_c, dp_i, s_mix = _mixer_bwd(dcat, proj, cw_mix, conv_b_mix, lb, w_att)
    dob, stats = _attn_bwd_prep(cat, dcat, lse2, hh)
    dq_raw, dkv_k, dkv_v, dkr_h = _attn_bwd(q_c, k_c, kv_p, dob, stats, hh)
    dkv_p = jnp.concatenate([dkv_k, dkv_v], axis=1)
    dq_p = _dq_unrope(dq_raw, tabs, hh)
    dw_uq_p = _matmul(qn, dq_p, ta=True, out_dtype=BF16, tm=1024, tn=1024, tk=1024, name="mm_dw_uq")
    dqn = _matmul(dq_p, w_uq_p, tb=True, out_dtype=F32, tm=1024, tn=1024, tk=2048, name="mm_dqn")
    dw_ukv_p = _matmul(kvn, dkv_p, ta=True, out_dtype=BF16, tm=1024, tn=1024, tk=1024, name="mm_dw_ukv")
    dkvn = _matmul(dkv_p, w_ukv_p, tb=True, out_dtype=F32, tm=1024, tn=1024, tk=2048, name="mm_dkvn")
    dp_lat, s_lat = _latent_bwd(proj, dqn, dkvn, dkr_h, g_q, g_kv, tabs, lb)
    dproj = jnp.concatenate([dp_lat, dp_b, dp_c, dp_i], axis=1)
    dw_in_p = _matmul(h1, dproj, ta=True, out_dtype=BF16, tm=1024, tn=1536, tk=2048, name="mm_dw_in")

    dw_in_f = jnp.concatenate([dw_in_p[:, :lat], dw_in_p[:, lb:]], axis=1)
    uq3 = dw_uq_p.reshape(ql, 2, hh, LANES)
    dw_uq_f = jnp.concatenate([uq3[:, 0], uq3[:, 1, :, :ROPE]], axis=2).reshape(ql, hh * (NOPE + ROPE))
    ukv3 = dw_ukv_p.reshape(kl, 2, hh, LANES)
    dw_ukv_f = jnp.concatenate([ukv3[:, 0], ukv3[:, 1]], axis=2).reshape(kl, hh * (NOPE + VDIM))
    send_a = [_cols_to_shards(dw_in_f), _cols_to_shards(dw_uq_f), _cols_to_shards(dw_ukv_f)]
    rs_a = _exchange_start(send_a, True, "rs_a_start")
    dproj = _behind(dproj, rs_a)

    dh1 = _matmul(dproj, w_in_p, tb=True, out_dtype=F32, tm=512, tn=1024, tk=4608, name="mm_dh1")
    grad_x, s_first = _first_bwd(dh1, dx1, x0, g_pre_mix, sc_m)

    names = ["w_in", "w_uq", "w_ukv", "w_o", "w_up", "w_down"]
    sent_b, landed_b = _exchange_wait(rs_b, s_first, True, "rs_b_wait")
    sent_a, landed_a = _exchange_wait(rs_a, landed_b[0], True, "rs_a_wait")
    landed_a, s_first = lax.optimization_barrier((landed_a, s_first))
    part = [_sum_chips(l, a, "sum_chips_" + n) for l, a, n in zip(landed_a + landed_b, sent_a + sent_b, names)]

    dmod = jnp.concatenate([s_first[0:1], s_first[1:2], s_mid[3:4], s_mid[0:1], s_mid[1:2], s_fin[0:1]], axis=1)
    small = [
        dmod,
        s_first[2:3],
        s_mid[4:5],
        s_lat[0:1, :ql],
        s_lat[0:1, ql : ql + kl],
        s_mix[3:4],
        s_mid[2:3],
        s_fin[1:2],
        s_ffn[3:4],
        s_mix[0:3].reshape(1, -1),
        s_ffn[0:3].reshape(1, -1),
        s_fin[3:4, :LANES],
    ]
    sizes = [a.shape[1] for a in small]
    offs = [0]
    for n in sizes:
        offs.append(offs[-1] + n)
    pack_g = _pad_to(jnp.concatenate(small, axis=1).reshape(-1), SUBLANES * LANES * SUBLANES).reshape(SUBLANES, -1)
    gathered = _all_gather8(pack_g, "ag8_small_grads")
    tot = _sum_devices(gathered).reshape(-1)
    part_of = lambda k: tot[offs[k] : offs[k + 1]]
    dmod_all = gathered.reshape(N_DEV, -1)[:, : N_MOD * d]
    loss = part_of(11)[0]

    g_b_ada = part_of(0).reshape(1, -1)
    g_vecs = [part_of(k).reshape(1, -1) for k in range(1, 9)]
    g_cw_mix = lax.dynamic_slice(part_of(9).reshape(CONV_K, cwid), (0, chip * (cwid // N_CHIP)), (CONV_K, cwid // N_CHIP))
    g_cw_ffn = lax.dynamic_slice(part_of(10).reshape(CONV_K, f2), (0, chip * (f2 // N_CHIP)), (CONV_K, f2 // N_CHIP))

    swap = _swap_start(part, "swap_start")
    dm_cols = _behind(lax.dynamic_slice(dmod_all, (0, chip * nc_ada), (N_DEV, nc_ada)), swap)
    g_w_ada = _ada_grad(
        jnp.pad(c_act.T, ((0, 0), (0, LANES - N_DEV))), jnp.pad(dm_cols, ((0, LANES - N_DEV), (0, 0)))
    )
    big = {"w_ada": [a[None] for a in _adamw(w_ada[0], m_w_ada[0], v_w_ada[0], [g_w_ada], "adamw_w_ada")]}
    part, other = _swap_wait(swap, big["w_ada"][1], "swap_wait")

    big_w = [w_in, w_uq, w_ukv, w_o, w_up, w_down]
    big_m = [m_w_in, m_w_uq, m_w_ukv, m_w_o, m_w_up, m_w_down]
    big_v = [v_w_in, v_w_uq, v_w_ukv, v_w_o, v_w_up, v_w_down]
    for n, w_, m_, v_, p_, o_ in zip(names, big_w, big_m, big_v, part, other):
        big[n] = [a[None] for a in _adamw(w_[0], m_[0], v_[0], [p_, o_], "adamw_" + n)]

    sm_names = ["b_ada", "g_pre_mix", "g_post_mix", "g_q", "g_kv", "conv_b_mix", "g_pre_ffn", "g_post_ffn", "conv_b_ffn",
                "conv_w_mix", "conv_w_ffn"]
    sm_w = [b_ada, g_pre_mix, g_post_mix, g_q, g_kv, conv_b_mix, g_pre_ffn, g_post_ffn, conv_b_ffn, conv_w_mix, conv_w_ffn]
    sm_m = [m_b_ada, m_g_pre_mix, m_g_post_mix, m_g_q, m_g_kv, m_conv_b_mix, m_g_pre_ffn, m_g_post_ffn, m_conv_b_ffn,
            m_conv_w_mix, m_conv_w_ffn]
    sm_v = [v_b_ada, v_g_pre_mix, v_g_post_mix, v_g_q, v_g_kv, v_conv_b_mix, v_g_pre_ffn, v_g_post_ffn, v_conv_b_ffn,
            v_conv_w_mix, v_conv_w_ffn]
    sm_g = [g_b_ada] + g_vecs + [g_cw_mix, g_cw_ffn]
    flat = lambda arrs: jnp.concatenate([a.reshape(1, -1) for a in arrs], axis=1)
    sm_out = _adamw(flat(sm_w), flat(sm_m), flat(sm_v), [flat(sm_g)], "adamw_small")
    sm = {}
    off = 0
    for n, w_ in zip(sm_names, sm_w):
        sm[n] = [o[:, off : off + w_.size].reshape(w_.shape) for o in sm_out]
        off += w_.size

    order = ["w_ada", "b_ada", "g_pre_mix", "g_post_mix", "w_in", "g_q", "w_uq", "g_kv", "w_ukv", "conv_w_mix", "conv_b_mix",
             "w_o", "g_pre_ffn", "g_post_ffn", "w_up", "conv_w_ffn", "conv_b_ffn", "w_down"]
    res = {**big, **sm}
    outs = [loss + sum(anchors), grad_x.reshape(x.shape)]
    for k in range(4):
        outs += [res[n][k] for n in order]
    return tuple(outs)
```

```python
import math

import jax
import jax.numpy as jnp
from jax import lax
from jax.experimental import pallas as pl
from jax.experimental.pallas import tpu as pltpu

F32 = jnp.float32
BF16 = jnp.bfloat16
MESH = pl.DeviceIdType.MESH

N_DEV = 8
N_CHIP = 4
LANES = 128
SUBLANES = 8
VMEM_LIMIT = 56 * 2**20

NOPE = 128
ROPE = 64
VDIM = 128
HEAD_PAD = 128
ROPE_THETA = 10000.0
RMS_EPS = 1e-6
N_MOD = 6
CONV_K = 3
ATT_FWD_BLOCK, ATT_FWD_SUB = 2048, 256
ATT_BWD_BLOCK, ATT_BWD_SUB = 1024, 256
NEG = -1e30

ADAM_LR = 0.001
ADAM_B1 = 0.9
ADAM_B2 = 0.999
ADAM_EPS = 1e-08
ADAM_WD = 0.01
ADAM_STEP = 10


def _tile(n, pref, align):
    if n <= pref:
        return n
    t = (pref // align) * align
    while t >= align:
        if n % t == 0:
            return t
        t -= align
    return n


def _cp(*sem):
    return pltpu.CompilerParams(dimension_semantics=sem, vmem_limit_bytes=VMEM_LIMIT)


def _rsq(x):
    return lax.rsqrt(jnp.mean(x * x, axis=-1, keepdims=True) + RMS_EPS)


def _norm_bwd(dn, n, r):
    return r * (dn - n * jnp.mean(dn * n, axis=-1, keepdims=True))


def _colsum(a):
    return jnp.sum(a, axis=0, keepdims=True)


def _matmul(a, b, *, ta=False, tb=False, out_dtype, tm, tn, tk, name, b_n_perm=None, out_n_perm=None,
            b_col_shards=False, out_col_shards=False):
    assert not (b_col_shards and tb)
    if b_col_shards:
        b_rows, b_cols = b.shape[1], N_CHIP * b.shape[2]
    else:
        b_rows, b_cols = b.shape
    (k_a, m) = a.shape if ta else a.shape[::-1]
    (n, k_b) = (b_rows, b_cols) if tb else (b_cols, b_rows)
    assert k_a == k_b, (a.shape, b.shape, ta, tb)
    tm, tn, tk = _tile(m, tm, LANES), _tile(n, tn, LANES), _tile(k_a, tk, LANES)
    nk = k_a // tk
    same = lambda t: t
    bn, on = b_n_perm or same, out_n_perm or same
    a_spec = pl.BlockSpec((tk, tm), lambda i, j, k: (k, i)) if ta else pl.BlockSpec((tm, tk), lambda i, j, k: (i, k))
    if b_col_shards:
        per = (b_cols // N_CHIP) // tn
        b_spec = pl.BlockSpec((None, tk, tn), lambda i, j, k: (bn(j) // per, k, bn(j) % per))
    elif tb:
        b_spec = pl.BlockSpec((tn, tk), lambda i, j, k: (bn(j), k))
    else:
        b_spec = pl.BlockSpec((tk, tn), lambda i, j, k: (k, bn(j)))
    if out_col_shards:
        per_o = (n // N_CHIP) // tn
        out_shape = jax.ShapeDtypeStruct((N_CHIP, m, n // N_CHIP), out_dtype)
        out_spec = pl.BlockSpec((None, tm, tn), lambda i, j, k: (on(j) // per_o, i, on(j) % per_o))
    else:
        out_shape = jax.ShapeDtypeStruct((m, n), out_dtype)
        out_spec = pl.BlockSpec((tm, tn), lambda i, j, k: (i, on(j)))
    dims = (((0 if ta else 1,), (1 if tb else 0,)), ((), ()))

    def body(a_ref, b_ref, o_ref, *acc):
        p = lax.dot_general(a_ref[...].astype(BF16), b_ref[...].astype(BF16), dims, preferred_element_type=F32)
        _accumulate(p, o_ref, acc, nk)

    return pl.pallas_call(
        body,
        name=name,
        out_shape=out_shape,
        grid=(m // tm, n // tn, nk),
        in_specs=[a_spec, b_spec],
        out_specs=out_spec,
        scratch_shapes=[] if nk == 1 else [pltpu.VMEM((tm, tn), F32)],
        compiler_params=_cp("parallel", "parallel", "arbitrary"),
    )(a, b)


def _accumulate(p, o_ref, acc, nk):
    if nk == 1:
        o_ref[...] = p.astype(o_ref.dtype)
        return
    k = pl.program_id(2)

    @pl.when(k == 0)
    def _():
        acc[0][...] = p

    @pl.when(k > 0)
    def _():
        acc[0][...] += p

    @pl.when(k == nk - 1)
    def _():
        o_ref[...] = acc[0][...].astype(o_ref.dtype)


def _matmul_pair_k(a, b_shards, *, out_dtype, tm, tn, name):
    m, f2 = a.shape
    n = b_shards.shape[1]
    tc = _pair_tile(f2 // 2)
    nk = (f2 // 2) // tc
    per = (f2 // N_CHIP) // tc
    tm, tn = _tile(m, tm, LANES), _tile(n, tn, LANES)

    def body(a_ref, ba_ref, bg_ref, o_ref, *acc):
        av = a_ref[...]
        p = lax.dot_general(av[:, :tc], ba_ref[...], NT, preferred_element_type=F32)
        p = p + lax.dot_general(av[:, tc:], bg_ref[...], NT, preferred_element_type=F32)
        _accumulate(p, o_ref, acc, nk)

    def w_tile(first):
        return pl.BlockSpec((None, tn, tc), lambda i, j, k: ((first + k) // per, j, (first + k) % per))

    return pl.pallas_call(
        body,
        name=name,
        out_shape=jax.ShapeDtypeStruct((m, n), out_dtype),
        grid=(m // tm, n // tn, nk),
        in_specs=[pl.BlockSpec((tm, 2 * tc), lambda i, j, k: (i, k)), w_tile(0), w_tile(nk)],
        out_specs=pl.BlockSpec((tm, tn), lambda i, j, k: (i, j)),
        scratch_shapes=[] if nk == 1 else [pltpu.VMEM((tm, tn), F32)],
        compiler_params=_cp("parallel", "parallel", "arbitrary"),
    )(a, b_shards, b_shards)


def _rope_tables(pos_col, invf):
    s = pos_col.shape[0]
    ts = _tile(s, 1024, SUBLANES)
    half = ROPE // 2

    def body(p_ref, f_ref, c_ref, sa_ref, sb_ref):
        ang = p_ref[...] * f_ref[...]
        lane = lax.broadcasted_iota(jnp.int32, ang.shape, 1)
        cs, sn = jnp.cos(ang), jnp.sin(ang)
        c_ref[...] = jnp.where(lane < ROPE, cs, 0.0)
        sa_ref[...] = jnp.where((lane >= half) & (lane < ROPE), sn, 0.0)
        sb_ref[...] = jnp.where(lane < half, -sn, 0.0)

    tab = jax.ShapeDtypeStruct((s, LANES), F32)
    return pl.pallas_call(
        body,
        name="rope_tables",
        out_shape=(tab, tab, tab),
        grid=(s // ts,),
        in_specs=[pl.BlockSpec((ts, 1), lambda i: (i, 0)), pl.BlockSpec((1, LANES), lambda i: (0, 0))],
        out_specs=[pl.BlockSpec((ts, LANES), lambda i: (i, 0))] * 3,
        compiler_params=_cp("parallel"),
    )(pos_col, invf)


def _widen(t, w):
    return t if w == LANES else jnp.tile(t, (1, w // LANES))


def _rope(x, c, sa, sb):
    w = x.shape[1]
    c, sa, sb = _widen(c, w), _widen(sa, w), _widen(sb, w)
    return x * c + pltpu.roll(x, ROPE // 2, 1) * sa + pltpu.roll(x, w - ROPE // 2, 1) * sb


def _rope_t(d, c, sa, sb):
    w = d.shape[1]
    c, sa, sb = _widen(c, w), _widen(sa, w), _widen(sb, w)
    return d * c + pltpu.roll(d * sa, w - ROPE // 2, 1) + pltpu.roll(d * sb, ROPE // 2, 1)


def _ada_fwd(c_all, w, b):
    d, nc = w.shape
    tn = _tile(nc, 512, LANES)

    def body(c_ref, w_ref, b_ref, o_ref, ca_ref):
        cv = c_ref[...]
        ca = cv * jax.nn.sigmoid(cv)
        ca_ref[...] = ca
        o_ref[...] = jnp.dot(ca.astype(BF16), w_ref[...].astype(BF16), preferred_element_type=F32) + b_ref[...]

    return pl.pallas_call(
        body,
        name="ada_fwd",
        out_shape=(jax.ShapeDtypeStruct((N_DEV, nc), F32), jax.ShapeDtypeStruct((N_DEV, d), F32)),
        grid=(nc // tn,),
        in_specs=[
            pl.BlockSpec((N_DEV, d), lambda j: (0, 0)),
            pl.BlockSpec((d, tn), lambda j: (0, j)),
            pl.BlockSpec((1, tn), lambda j: (0, j)),
        ],
        out_specs=[pl.BlockSpec((N_DEV, tn), lambda j: (0, j)), pl.BlockSpec((N_DEV, d), lambda j: (0, 0))],
        compiler_params=_cp("arbitrary"),
    )(c_all, w, b)


def _rows(ts, d):
    return pl.BlockSpec((ts, d), lambda i: (i, 0))


def _vec(d):
    return pl.BlockSpec((1, d), lambda i: (0, 0))


def _sums(d):
    return pl.BlockSpec((SUBLANES, d), lambda i: (0, 0))


def _acc_rows(ref, i, rows):
    @pl.when(i == 0)
    def _():
        ref[...] = jnp.zeros(ref.shape, ref.dtype)

    for k, r in enumerate(rows):
        ref[k : k + 1, :] += r


def _pre_fwd(x, g, sc, sh):
    s, d = x.shape
    ts = _tile(s, 512, SUBLANES)

    def body(x_ref, g_ref, sc_ref, sh_ref, h_ref):
        xv = x_ref[...]
        h_ref[...] = (((xv * _rsq(xv)) * g_ref[...]) * (1.0 + sc_ref[...]) + sh_ref[...]).astype(BF16)

    return pl.pallas_call(
        body,
        name="pre_mix_fwd",
        out_shape=jax.ShapeDtypeStruct((s, d), BF16),
        grid=(s // ts,),
        in_specs=[_rows(ts, d), _vec(d), _vec(d), _vec(d)],
        out_specs=_rows(ts, d),
        compiler_params=_cp("parallel"),
    )(x, g, sc, sh)


def _mid_fwd(x0, mix, g_post, gt, g_pre, sc, sh):
    s, d = x0.shape
    ts = _tile(s, 256, SUBLANES)

    def body(x_ref, m_ref, gp_ref, gt_ref, g_ref, sc_ref, sh_ref, x1_ref, h_ref):
        mv = m_ref[...]
        x1 = x_ref[...] + gt_ref[...] * ((mv * _rsq(mv)) * gp_ref[...])
        x1_ref[...] = x1
        h_ref[...] = (((x1 * _rsq(x1)) * g_ref[...]) * (1.0 + sc_ref[...]) + sh_ref[...]).astype(BF16)

    return pl.pallas_call(
        body,
        name="mid_fwd",
        out_shape=(jax.ShapeDtypeStruct((s, d), F32), jax.ShapeDtypeStruct((s, d), BF16)),
        grid=(s // ts,),
        in_specs=[_rows(ts, d), _rows(ts, d)] + [_vec(d)] * 5,
        out_specs=[_rows(ts, d), _rows(ts, d)],
        compiler_params=_cp("parallel"),
    )(x0, mix, g_post, gt, g_pre, sc, sh)


def _final(x1, y, tgt, g_post, gt):
    s, d = x1.shape
    ts = _tile(s, 256, SUBLANES)
    ni = s // ts

    def body(x_ref, y_ref, t_ref, gp_ref, gt_ref, dx_ref, dy_ref, s_ref):
        i = pl.program_id(0)
        yv, gp, gt_v = y_ref[...], gp_ref[...], gt_ref[...]
        r = _rsq(yv)
        n = yv * r
        err = (x_ref[...] + gt_v * (n * gp)) - t_ref[...]
        dx = err * (1.0 / d)
        dx_ref[...] = dx
        dy_ref[...] = _norm_bwd(dx * (gt_v * gp), n, r).astype(BF16)
        _acc_rows(s_ref, i, [_colsum(dx * (n * gp)), _colsum(dx * gt_v * n), _colsum(err * err)])

        @pl.when(i == ni - 1)
        def _():
            tot = jnp.sum(s_ref[2:3, :], axis=1, keepdims=True) * (0.5 / d)
            s_ref[3:4, :] = jnp.broadcast_to(tot, (1, d))

    return pl.pallas_call(
        body,
        name="final_fwd_bwd",
        out_shape=(
            jax.ShapeDtypeStruct((s, d), F32),
            jax.ShapeDtypeStruct((s, d), BF16),
            jax.ShapeDtypeStruct((SUBLANES, d), F32),
        ),
        grid=(ni,),
        in_specs=[_rows(ts, d)] * 3 + [_vec(d)] * 2,
        out_specs=[_rows(ts, d), _rows(ts, d), _sums(d)],
        compiler_params=_cp("arbitrary"),
    )(x1, y, tgt, g_post, gt)


def _mid_bwd(dh2, dx2, x1, mix, g_pre, sc, g_post, gt):
    s, d = x1.shape
    ts = _tile(s, 256, SUBLANES)

    def body(dh_ref, dx2_ref, x_ref, m_ref, g_ref, sc_ref, gp_ref, gt_ref, dx1_ref, dm_ref, s_ref):
        i = pl.program_id(0)
        dh, xv, mv = dh_ref[...], x_ref[...], m_ref[...]
        g, sc_v, gp, gt_v = g_ref[...], sc_ref[...], gp_ref[...], gt_ref[...]
        r1 = _rsq(xv)
        n1 = xv * r1
        dx1 = dx2_ref[...] + _norm_bwd(dh * (g * (1.0 + sc_v)), n1, r1)
        dx1_ref[...] = dx1
        rm = _rsq(mv)
        nm = mv * rm
        dm_ref[...] = _norm_bwd(dx1 * (gt_v * gp), nm, rm).astype(BF16)
        _acc_rows(
            s_ref,
            i,
            [
                _colsum(dh),
                _colsum(dh * (n1 * g)),
                _colsum(dh * (1.0 + sc_v) * n1),
                _colsum(dx1 * (nm * gp)),
                _colsum(dx1 * gt_v * nm),
            ],
        )

    return pl.pallas_call(
        body,
        name="mid_bwd",
        out_shape=(
            jax.ShapeDtypeStruct((s, d), F32),
            jax.ShapeDtypeStruct((s, d), BF16),
            jax.ShapeDtypeStruct((SUBLANES, d), F32),
        ),
        grid=(s // ts,),
        in_specs=[_rows(ts, d)] * 4 + [_vec(d)] * 4,
        out_specs=[_rows(ts, d), _rows(ts, d), _sums(d)],
        compiler_params=_cp("arbitrary"),
    )(dh2, dx2, x1, mix, g_pre, sc, g_post, gt)


def _first_bwd(dh1, dx1, x0, g, sc):
    s, d = x0.shape
    ts = _tile(s, 256, SUBLANES)

    def body(dh_ref, dx1_ref, x_ref, g_ref, sc_ref, dx_ref, s_ref):
        i = pl.program_id(0)
        dh, xv, gv, sc_v = dh_ref[...], x_ref[...], g_ref[...], sc_ref[...]
        r = _rsq(xv)
        n = xv * r
        dx_ref[...] = dx1_ref[...] + _norm_bwd(dh * (gv * (1.0 + sc_v)), n, r)
        _acc_rows(s_ref, i, [_colsum(dh), _colsum(dh * (n * gv)), _colsum(dh * (1.0 + sc_v) * n)])

    return pl.pallas_call(
        body,
        name="first_bwd",
        out_shape=(jax.ShapeDtypeStruct((s, d), F32), jax.ShapeDtypeStruct((SUBLANES, d), F32)),
        grid=(s // ts,),
        in_specs=[_rows(ts, d)] * 3 + [_vec(d)] * 2,
        out_specs=[_rows(ts, d), _sums(d)],
        compiler_params=_cp("arbitrary"),
    )(dh1, dx1, x0, g, sc)


def _latent_fwd(proj, g_q, g_kv, tabs, lb):
    s = proj.shape[0]
    ql, kl = g_q.shape[1], g_kv.shape[1]
    ts = _tile(s, 512, SUBLANES)

    def body(p_ref, gq_ref, gk_ref, c_ref, sa_ref, sb_ref, q_ref, kv_ref, kr_ref):
        pv = p_ref[...]
        q, kv, kr = pv[:, :ql], pv[:, ql : ql + kl], pv[:, ql + kl : ql + kl + HEAD_PAD]
        q_ref[...] = ((q * _rsq(q)) * gq_ref[...]).astype(BF16)
        kv_ref[...] = ((kv * _rsq(kv)) * gk_ref[...]).astype(BF16)
        kr_ref[...] = _rope(kr, c_ref[...], sa_ref[...], sb_ref[...]).astype(BF16)

    return pl.pallas_call(
        body,
        name="latent_fwd",
        out_shape=(
            jax.ShapeDtypeStruct((s, ql), BF16),
            jax.ShapeDtypeStruct((s, kl), BF16),
            jax.ShapeDtypeStruct((s, HEAD_PAD), BF16),
        ),
        grid=(s // ts,),
        in_specs=[_rows(ts, lb), _vec(ql), _vec(kl)] + [_rows(ts, LANES)] * 3,
        out_specs=[_rows(ts, ql), _rows(ts, kl), _rows(ts, HEAD_PAD)],
        compiler_params=_cp("parallel"),
    )(proj, g_q, g_kv, *tabs)


def _latent_bwd(dproj, proj, dqn, dkvn, dkr_h, g_q, g_kv, tabs, lb):
    s = proj.shape[0]
    ql, kl = g_q.shape[1], g_kv.shape[1]
    hw = dkr_h.shape[1]
    ts = _tile(s, 256, SUBLANES)
    pad = lb - ql - kl - HEAD_PAD

    def body(_, p_ref, dq_ref, dkv_ref, dkr_ref, gq_ref, gk_ref, c_ref, sa_ref, sb_ref, o_ref, s_ref):
        i = pl.program_id(0)
        pv = p_ref[...]
        q, kv = pv[:, :ql], pv[:, ql : ql + kl]
        dqn_v, dkvn_v = dq_ref[...], dkv_ref[...]
        rq = _rsq(q)
        nq = q * rq
        rk = _rsq(kv)
        nk = kv * rk
        dkr = dkr_ref[:, :HEAD_PAD]
        for h in range(1, hw // HEAD_PAD):
            dkr = dkr + dkr_ref[:, h * HEAD_PAD : (h + 1) * HEAD_PAD]
        parts = [
            _norm_bwd(dqn_v * gq_ref[...], nq, rq).astype(BF16),
            _norm_bwd(dkvn_v * gk_ref[...], nk, rk).astype(BF16),
            _rope_t(dkr, c_ref[...], sa_ref[...], sb_ref[...]).astype(BF16),
        ]
        if pad:
            parts.append(jnp.zeros((ts, pad), BF16))
        o_ref[...] = jnp.concatenate(parts, axis=1)
        row = [_colsum(dqn_v * nq), _colsum(dkvn_v * nk), jnp.zeros((1, lb - ql - kl), F32)]
        _acc_rows(s_ref, i, [jnp.concatenate(row, axis=1)])

    return pl.pallas_call(
        body,
        name="latent_bwd",
        out_shape=(jax.ShapeDtypeStruct(dproj.shape, BF16), jax.ShapeDtypeStruct((SUBLANES, lb), F32)),
        grid=(s // ts,),
        in_specs=[pl.BlockSpec(memory_space=pl.ANY), _rows(ts, lb), _rows(ts, ql), _rows(ts, kl), _rows(ts, hw)]
        + [_vec(ql), _vec(kl)]
        + [_rows(ts, LANES)] * 3,
        out_specs=[_rows(ts, lb), _sums(lb)],
        input_output_aliases={0: 0},
        compiler_params=_cp("arbitrary"),
    )(dproj, proj, dqn, dkvn, dkr_h, g_q, g_kv, *tabs)


def _conv3(ext, w, b):
    return (pltpu.roll(ext, 2, 0) * w[0:1] + pltpu.roll(ext, 1, 0) * w[1:2]) + ext * w[2:3] + b


def _conv3_t(du, w):
    n = du.shape[0]
    return du * w[2:3] + pltpu.roll(du, n - 1, 0) * w[1:2] + pltpu.roll(du, n - 2, 0) * w[0:1]


def _halo_maps(ts, s):
    r8, last = ts // SUBLANES, s // SUBLANES - 1
    prev = lambda i: jnp.maximum(i * r8 - 1, 0)
    nxt = lambda i: jnp.minimum((i + 1) * r8, last)
    return prev, nxt


def _gate_tile(cwid):
    return _tile(cwid, 512, LANES)


def _mixer_fwd(cat, proj, cw, cb, lb, col0):
    s = proj.shape[0]
    cwid = cw.shape[1]
    ts = _tile(s, 512, SUBLANES)
    tc = _gate_tile(cwid)
    assert lb % (3 * tc) == 0 and col0 % tc == 0
    t0, oc = lb // (3 * tc), col0 // tc
    prev, _ = _halo_maps(ts, s)

    def body(_, g_ref, p_ref, w_ref, b_ref, o_ref):
        keep = jnp.where(pl.program_id(1) > 0, 1.0, 0.0)
        gv, pv = g_ref[...], p_ref[...]
        ext = jnp.concatenate([pv[:, tc : 2 * tc] * pv[:, 2 * tc :] * keep, gv[:, tc : 2 * tc] * gv[:, 2 * tc :]], axis=0)
        o_ref[...] = (gv[:, :tc] * _conv3(ext, w_ref[...], b_ref[...])[SUBLANES:]).astype(BF16)

    return pl.pallas_call(
        body,
        name="mixer_fwd",
        out_shape=jax.ShapeDtypeStruct(cat.shape, BF16),
        grid=(cwid // tc, s // ts),
        in_specs=[
            pl.BlockSpec(memory_space=pl.ANY),
            pl.BlockSpec((ts, 3 * tc), lambda j, i: (i, t0 + j)),
            pl.BlockSpec((SUBLANES, 3 * tc), lambda j, i: (prev(i), t0 + j)),
            pl.BlockSpec((CONV_K, tc), lambda j, i: (0, j)),
            pl.BlockSpec((1, tc), lambda j, i: (0, j)),
        ],
        out_specs=pl.BlockSpec((ts, tc), lambda j, i: (i, oc + j)),
        input_output_aliases={0: 0},
        compiler_params=_cp("parallel", "arbitrary"),
    )(cat, proj, proj, cw, cb)


def _mixer_bwd(dcat, proj, cw, cb, lb, col0):
    s, np_cols = proj.shape
    cwid = cw.shape[1]
    ts = _tile(s, 256, SUBLANES)
    tc = _gate_tile(cwid)
    t0, oc = lb // (3 * tc), col0 // tc
    ni = s // ts
    prev, nxt = _halo_maps(ts, s)

    def body(d_ref, dn_ref, g_ref, gp_ref, gn_ref, w_ref, b_ref, dg_ref, s_ref):
        i = pl.program_id(1)
        keep_p = jnp.where(i > 0, 1.0, 0.0)
        keep_n = jnp.where(i < ni - 1, 1.0, 0.0)
        w = w_ref[...]
        gv, gp, gn = g_ref[...], gp_ref[...], gn_ref[...]
        gc = jnp.concatenate([gp[:, tc : 2 * tc], gv[:, tc : 2 * tc], gn[:, tc : 2 * tc]], axis=0)
        ci = jnp.concatenate([gp[:, 2 * tc :] * keep_p, gv[:, 2 * tc :], gn[:, 2 * tc :]], axis=0)
        u = gc * ci
        cv = _conv3(u, w, b_ref[...])[SUBLANES:]
        dco = jnp.concatenate([d_ref[...], dn_ref[...] * keep_n], axis=0)
        gb = jnp.concatenate([gv[:, :tc], gn[:, :tc]], axis=0)
        dcv = dco * gb
        du = _conv3_t(dcv, w)[:ts]
        dg_ref[:, :tc] = (dco * cv)[:ts].astype(BF16)
        dg_ref[:, tc : 2 * tc] = (du * gv[:, 2 * tc :]).astype(BF16)
        dg_ref[:, 2 * tc :] = (du * gv[:, tc : 2 * tc]).astype(BF16)
        dt = dcv[:ts]
        u1, u2 = pltpu.roll(u, 1, 0), pltpu.roll(u, 2, 0)
        lo, hi = SUBLANES, SUBLANES + ts
        _acc_rows(s_ref, i, [_colsum(dt * u2[lo:hi]), _colsum(dt * u1[lo:hi]), _colsum(dt * u[lo:hi]), _colsum(dt)])

    def triple(rows, which):
        return pl.BlockSpec((rows, 3 * tc), lambda j, i: (which(i), t0 + j))

    return pl.pallas_call(
        body,
        name="mixer_bwd",
        out_shape=(jax.ShapeDtypeStruct((s, np_cols), BF16), jax.ShapeDtypeStruct((SUBLANES, cwid), F32)),
        grid=(cwid // tc, ni),
        in_specs=[
            pl.BlockSpec((ts, tc), lambda j, i: (i, oc + j)),
            pl.BlockSpec((SUBLANES, tc), lambda j, i: (nxt(i), oc + j)),
            triple(ts, lambda i: i), triple(SUBLANES, prev), triple(SUBLANES, nxt),
            pl.BlockSpec((CONV_K, tc), lambda j, i: (0, j)),
            pl.BlockSpec((1, tc), lambda j, i: (0, j)),
        ],
        out_specs=[triple(ts, lambda i: i), pl.BlockSpec((SUBLANES, tc), lambda j, i: (0, j))],
        compiler_params=_cp("parallel", "arbitrary"),
    )(dcat, dcat, proj, proj, proj, cw, cb)


def _pair_tile(f):
    return _tile(f, 1408, LANES)


def _pair_perm(f):
    nj = f // _pair_tile(f)
    return lambda p: (p % 2) * nj + p // 2


def _pair_cols(a):
    r, f2 = a.shape
    tc = _pair_tile(f2 // 2)
    return a.reshape(r, 2, f2 // (2 * tc), tc).transpose(0, 2, 1, 3).reshape(r, f2)


def _unpair_cols(a):
    r, f2 = a.shape
    tc = _pair_tile(f2 // 2)
    return a.reshape(r, f2 // (2 * tc), 2, tc).transpose(0, 2, 1, 3).reshape(r, f2)


def _ffn_act_fwd(up, cw, cb):
    s, f2 = up.shape
    f = f2 // 2
    ts = _tile(s, 512, SUBLANES)
    tc = _pair_tile(f)
    prev, _ = _halo_maps(ts, s)

    def body(u_ref, p_ref, w_ref, b_ref, o_ref):
        keep = jnp.where(pl.program_id(1) > 0, 1.0, 0.0)
        ext = jnp.concatenate([p_ref[...] * keep, u_ref[...]], axis=0)
        u = _conv3(ext, w_ref[...], b_ref[...])[SUBLANES:]
        a, g = u[:, :tc], u[:, tc:]
        o_ref[...] = ((g * jax.nn.sigmoid(g)) * a).astype(BF16)

    def pair(rows, which):
        return pl.BlockSpec((rows, 2 * tc), lambda j, i: (which(i), j))

    return pl.pallas_call(
        body,
        name="ffn_act_fwd",
        out_shape=jax.ShapeDtypeStruct((s, f), BF16),
        grid=(f // tc, s // ts),
        in_specs=[pair(ts, lambda i: i), pair(SUBLANES, prev), pair(CONV_K, lambda i: 0), pair(1, lambda i: 0)],
        out_specs=pl.BlockSpec((ts, tc), lambda j, i: (i, j)),
        compiler_params=_cp("parallel", "arbitrary"),
    )(up, up, cw, cb)


def _ffn_act_bwd(dact, up, cw, cb):
    s, f2 = up.shape
    f = f2 // 2
    ts = _tile(s, 256, SUBLANES)
    tc = _pair_tile(f)
    nj, ni = f // tc, s // ts
    prev, nxt = _halo_maps(ts, s)

    def body(d_ref, dn_ref, u_ref, up_ref, un_ref, w_ref, b_ref, dup_ref, s_ref):
        i = pl.program_id(1)
        keep_p = jnp.where(i > 0, 1.0, 0.0)
        keep_n = jnp.where(i < ni - 1, 1.0, 0.0)
        w = w_ref[...]
        ext = jnp.concatenate([up_ref[...] * keep_p, u_ref[...], un_ref[...]], axis=0)
        u = _conv3(ext, w, b_ref[...])[SUBLANES:]
        a, g = u[:, :tc], u[:, tc:]
        dact_v = jnp.concatenate([d_ref[...], dn_ref[...] * keep_n], axis=0)
        sg = jax.nn.sigmoid(g)
        du = jnp.concatenate([dact_v * (g * sg), dact_v * a * (sg * (1.0 + g * (1.0 - sg)))], axis=1)
        dup_ref[...] = _conv3_t(du, w)[:ts].astype(BF16)
        dt = du[:ts]
        lo, hi = SUBLANES, SUBLANES + ts
        e1, e2 = pltpu.roll(ext, 1, 0), pltpu.roll(ext, 2, 0)
        _acc_rows(s_ref, i, [_colsum(dt * e2[lo:hi]), _colsum(dt * e1[lo:hi]), _colsum(dt * ext[lo:hi]), _colsum(dt)])

    def pair(rows, which):
        return pl.BlockSpec((rows, 2 * tc), lambda j, i: (which(i), j))

    return pl.pallas_call(
        body,
        name="ffn_act_bwd",
        out_shape=(jax.ShapeDtypeStruct((s, f2), BF16), jax.ShapeDtypeStruct((SUBLANES, f2), F32)),
        grid=(nj, ni),
        in_specs=[
            pl.BlockSpec((ts, tc), lambda j, i: (i, j)),
            pl.BlockSpec((SUBLANES, tc), lambda j, i: (nxt(i), j)),
            pair(ts, lambda i: i), pair(SUBLANES, prev), pair(SUBLANES, nxt),
            pair(CONV_K, lambda i: 0), pair(1, lambda i: 0),
        ],
        out_specs=[pair(ts, lambda i: i), pair(SUBLANES, lambda i: 0)],
        compiler_params=_cp("parallel", "arbitrary"),
    )(dact, dact, up, up, up, cw, cb)


ATT_SCALE = 1.0 / math.sqrt(NOPE + ROPE)
LOG2E = math.log2(math.e)
ATT_C2 = ATT_SCALE * LOG2E
STAT_SPLIT = 64
NT = (((1,), (1,)), ((), ()))
TN = (((0,), (0,)), ((), ()))


def _head_cat(q, kv, kr, tabs, n_heads):
    s, w2 = q.shape
    w = w2 // 2
    ts = _tile(s, 512, SUBLANES)
    hd = NOPE + HEAD_PAD

    def body(q_ref, kv_ref, kr_ref, c_ref, sa_ref, sb_ref, qc_ref, kc_ref):
        qv = q_ref[...]
        qr = _rope(qv[:, w:], c_ref[...], sa_ref[...], sb_ref[...]).astype(BF16)
        krv = kr_ref[...]
        for h in range(n_heads):
            qc_ref[:, h * hd : h * hd + NOPE] = qv[:, h * NOPE : (h + 1) * NOPE].astype(BF16)
            qc_ref[:, h * hd + NOPE : (h + 1) * hd] = qr[:, h * HEAD_PAD : (h + 1) * HEAD_PAD]
            kc_ref[:, h * hd : h * hd + NOPE] = kv_ref[:, h * NOPE : (h + 1) * NOPE]
            kc_ref[:, h * hd + NOPE : (h + 1) * hd] = krv

    out = jax.ShapeDtypeStruct((s, n_heads * hd), BF16)
    return pl.pallas_call(
        body,
        name="head_cat",
        out_shape=(out, out),
        grid=(s // ts,),
        in_specs=[_rows(ts, w2), _rows(ts, w), _rows(ts, HEAD_PAD)] + [_rows(ts, LANES)] * 3,
        out_specs=[_rows(ts, n_heads * hd)] * 2,
        compiler_params=_cp("parallel"),
    )(q, kv, kr, *tabs)


def _attn_fwd(qc, kc, kv, n_heads, cat_cols):
    s = qc.shape[0]
    t = _tile(s, ATT_FWD_BLOCK, LANES)
    sub = _tile(t, ATT_FWD_SUB, LANES)
    hh = n_heads
    hd = NOPE + HEAD_PAD

    def body(q_ref, k_ref, v_ref, o_ref, lse_ref, m_s, l_s, acc_s):
        i = pl.program_id(1)
        m_s[...] = jnp.full(m_s.shape, NEG, F32)
        l_s[...] = jnp.zeros(l_s.shape, F32)
        acc_s[...] = jnp.zeros(acc_s.shape, F32)

        def chunk(k0, diag):
            m_all, l_all, acc_all = m_s[...], l_s[...], acc_s[...]
            new_m, new_l, new_acc = [], [], []

            def scores(r0):
                ncol = r0 + sub if diag else t
                return lax.dot_general(q_ref[pl.ds(r0, sub), :], k_ref[pl.ds(k0, ncol), :], NT, preferred_element_type=F32)

            sc_next = scores(0)
            for r0 in range(0, t, sub):
                ncol = r0 + sub if diag else t
                sc = sc_next
                if r0 + sub < t:
                    sc_next = scores(r0 + sub)
                if diag:
                    row = lax.broadcasted_iota(jnp.int32, sc.shape, 0) + r0
                    col = lax.broadcasted_iota(jnp.int32, sc.shape, 1)
                    sc = jnp.where(col <= row, sc, NEG)
                m_prev = m_all[r0 : r0 + sub]
                m_new = jnp.maximum(m_prev, jnp.max(sc, axis=1, keepdims=True))
                alpha = jnp.exp2((m_prev - m_new) * ATT_C2)
                p = jnp.exp2((sc - m_new) * ATT_C2)
                pv = jnp.dot(p.astype(BF16), v_ref[pl.ds(k0, ncol), :], preferred_element_type=F32)
                new_m.append(m_new)
                new_l.append(alpha * l_all[r0 : r0 + sub] + jnp.sum(p, axis=1, keepdims=True))
                new_acc.append(alpha * acc_all[r0 : r0 + sub] + pv)
            m_s[...] = jnp.concatenate(new_m, axis=0)
            l_s[...] = jnp.concatenate(new_l, axis=0)
            acc_s[...] = jnp.concatenate(new_acc, axis=0)

        def loop_body(k, carry):
            chunk(pl.multiple_of(k * t, t), False)
            return carry

        lax.fori_loop(0, i, loop_body, 0)
        chunk(pl.multiple_of(i * t, t), True)
        l = l_s[...]
        o_ref[...] = (acc_s[...] / l).astype(BF16)
        lse_ref[...] = jnp.broadcast_to(m_s[...] * ATT_C2 + jnp.log(l) * LOG2E, lse_ref.shape)

    return pl.pallas_call(
        body,
        name="attn_fwd",
        out_shape=(jax.ShapeDtypeStruct((s, cat_cols), BF16), jax.ShapeDtypeStruct((s, hh * LANES), F32)),
        grid=(hh, s // t),
        in_specs=[
            pl.BlockSpec((t, hd), lambda h, i: (i, h)),
            pl.BlockSpec((s, hd), lambda h, i: (0, h)),
            pl.BlockSpec((s, VDIM), lambda h, i: (0, hh + h)),
        ],
        out_specs=[pl.BlockSpec((t, VDIM), lambda h, i: (i, h)), pl.BlockSpec((t, LANES), lambda h, i: (i, h))],
        scratch_shapes=[pltpu.VMEM((t, 1), F32), pltpu.VMEM((t, 1), F32), pltpu.VMEM((t, VDIM), F32)],
        compiler_params=_cp("parallel", "parallel"),
    )(qc, kc, kv)


def _attn_bwd_prep(cat, dcat, lse2, n_heads):
    s, w = lse2.shape
    ts = _tile(s, 512, SUBLANES)

    def body(o_ref, do_ref, lse_ref, dob_ref, st_ref):
        do = do_ref[...]
        dob_ref[...] = do.astype(BF16)
        prod = do * o_ref[...].astype(F32)
        lane = lax.broadcasted_iota(jnp.int32, (ts, LANES), 1)
        for h in range(n_heads):
            cols = slice(h * LANES, (h + 1) * LANES)
            dsum = jnp.sum(prod[:, cols], axis=1, keepdims=True)
            st_ref[:, cols] = jnp.where(lane < STAT_SPLIT, lse_ref[:, cols], dsum)

    return pl.pallas_call(
        body,
        name="attn_bwd_prep",
        out_shape=(jax.ShapeDtypeStruct((s, w), BF16), jax.ShapeDtypeStruct((s, w), F32)),
        grid=(s // ts,),
        in_specs=[_rows(ts, w)] * 3,
        out_specs=[_rows(ts, w)] * 2,
        compiler_params=_cp("parallel"),
    )(cat, dcat, lse2)


def _attn_bwd(qc, kc, kv, dob, stats, n_heads):
    s = qc.shape[0]
    t = _tile(s, ATT_BWD_BLOCK, LANES)
    sub = _tile(t, ATT_BWD_SUB, LANES)
    nb = s // t
    hh = n_heads
    hd = NOPE + HEAD_PAD
    w = hh * LANES

    def body(q_ref, k_ref, v_ref, do_ref, st_ref, dq_ref, dkn_ref, dv_ref, dkr_ref, dk_s, dv_s):
        j = pl.program_id(1)

        @pl.when(j == 0)
        def _():
            dq_ref[...] = jnp.zeros(dq_ref.shape, F32)

        dk_s[...] = jnp.zeros(dk_s.shape, F32)
        dv_s[...] = jnp.zeros(dv_s.shape, F32)

        def pair(i0, diag):
            def width(r0):
                return r0 + sub if diag else t

            def products(r0):
                rows = pl.ds(i0 + r0, sub)
                sc_ = lax.dot_general(q_ref[rows, :], k_ref[0 : width(r0), :], NT, preferred_element_type=F32)
                dp_ = lax.dot_general(do_ref[rows, :], v_ref[0 : width(r0), :], NT, preferred_element_type=F32)
                return sc_, dp_

            nxt = products(0)
            for r0 in range(0, t, sub):
                ncol = width(r0)
                rows = pl.ds(i0 + r0, sub)
                kk = k_ref[0:ncol, :]
                qq, do, st = q_ref[rows, :], do_ref[rows, :], st_ref[rows, :]
                sc, dp = nxt
                if r0 + sub < t:
                    nxt = products(r0 + sub)
                if diag:
                    row = lax.broadcasted_iota(jnp.int32, sc.shape, 0) + r0
                    col = lax.broadcasted_iota(jnp.int32, sc.shape, 1)
                    sc = jnp.where(col <= row, sc, NEG)
                p = jnp.exp2(sc * ATT_C2 - st[:, 0:1])
                dv_s[0:ncol, :] += lax.dot_general(p.astype(BF16), do, TN, preferred_element_type=F32)
                ds = (p * (dp - st[:, STAT_SPLIT : STAT_SPLIT + 1]) * ATT_SCALE).astype(BF16)
                dk_s[0:ncol, :] += lax.dot_general(ds, qq, TN, preferred_element_type=F32)
                dq_ref[rows, :] += jnp.dot(ds, kk, preferred_element_type=F32)

        pair(pl.multiple_of(j * t, t), True)

        def loop_body(i, carry):
            pair(pl.multiple_of(i * t, t), False)
            return carry

        lax.fori_loop(j + 1, nb, loop_body, 0)
        dkn_ref[...] = dk_s[:, :NOPE].astype(BF16)
        dv_ref[...] = dv_s[...].astype(BF16)
        dkr_ref[...] = dk_s[:, NOPE:]

    whole = lambda width, off: pl.BlockSpec((s, width), lambda h, j: (0, off + h))
    blk = lambda width, off: pl.BlockSpec((t, width), lambda h, j: (j, off + h))
    return pl.pallas_call(
        body,
        name="attn_bwd",
        out_shape=(
            jax.ShapeDtypeStruct((s, hh * hd), F32),
            jax.ShapeDtypeStruct((s, w), BF16),
            jax.ShapeDtypeStruct((s, w), BF16),
            jax.ShapeDtypeStruct((s, w), F32),
        ),
        grid=(hh, nb),
        in_specs=[whole(hd, 0), blk(hd, 0), blk(VDIM, hh), whole(VDIM, 0), whole(LANES, 0)],
        out_specs=[whole(hd, 0), blk(NOPE, 0), blk(VDIM, 0), blk(HEAD_PAD, 0)],
        scratch_shapes=[pltpu.VMEM((t, hd), F32), pltpu.VMEM((t, VDIM), F32)],
        compiler_params=_cp("parallel", "arbitrary"),
    )(qc, kc, kv, dob, stats)


def _dq_unrope(dq, tabs, n_heads):
    s = dq.shape[0]
    hd = NOPE + HEAD_PAD
    w = n_heads * LANES
    ts = _tile(s, 512, SUBLANES)

    def body(d_ref, c_ref, sa_ref, sb_ref, o_ref):
        c, sa, sb = c_ref[...], sa_ref[...], sb_ref[...]
        for h in range(n_heads):
            o_ref[:, h * NOPE : (h + 1) * NOPE] = d_ref[:, h * hd : h * hd + NOPE].astype(BF16)
            rot = _rope_t(d_ref[:, h * hd + NOPE : (h + 1) * hd], c, sa, sb)
            o_ref[:, w + h * HEAD_PAD : w + (h + 1) * HEAD_PAD] = rot.astype(BF16)

    return pl.pallas_call(
        body,
        name="dq_unrope",
        out_shape=jax.ShapeDtypeStruct((s, 2 * w), BF16),
        grid=(s // ts,),
        in_specs=[_rows(ts, n_heads * hd)] + [_rows(ts, LANES)] * 3,
        out_specs=_rows(ts, 2 * w),
        compiler_params=_cp("parallel"),
    )(dq, *tabs)


def _adamw(w, m, v, grads, name):
    r, c = w.shape
    budget_rows = max(SUBLANES, (VMEM_LIMIT // 3) // (4 * c * 2 * (7 + len(grads))))
    tr = _tile(r, budget_rows, SUBLANES)
    ng = len(grads)
    c1 = 1.0 - ADAM_B1**ADAM_STEP
    c2 = 1.0 - ADAM_B2**ADAM_STEP

    def body(*refs):
        w_ref, m_ref, v_ref = refs[:3]
        g_ref, d_ref, nm_ref, nv_ref = refs[3 + ng :]
        g = refs[3][...]
        for extra in refs[4 : 3 + ng]:
            g = g + extra[...]
        mn = ADAM_B1 * m_ref[...] + (1.0 - ADAM_B1) * g
        vn = ADAM_B2 * v_ref[...] + (1.0 - ADAM_B2) * (g * g)
        g_ref[...] = g
        nm_ref[...] = mn
        nv_ref[...] = vn
        d_ref[...] = -ADAM_LR * ((mn / c1) / (jnp.sqrt(vn / c2) + ADAM_EPS) + ADAM_WD * w_ref[...])

    blk = pl.BlockSpec((tr, c), lambda i: (i, 0))
    out = jax.ShapeDtypeStruct((r, c), F32)
    return pl.pallas_call(
        body,
        name=name,
        out_shape=(out, out, out, out),
        grid=(r // tr,),
        in_specs=[blk] * (3 + ng),
        out_specs=[blk] * 4,
        compiler_params=_cp("parallel"),
    )(w, m, v, *grads)


def _ada_grad(ca_t, dm):
    d = ca_t.shape[0]
    nc = dm.shape[1]
    tn = _tile(nc, 512, LANES)

    def body(a_ref, b_ref, o_ref):
        o_ref[...] = jnp.dot(a_ref[...].astype(BF16), b_ref[...].astype(BF16), preferred_element_type=F32)

    return pl.pallas_call(
        body,
        name="ada_grad",
        out_shape=jax.ShapeDtypeStruct((d, nc), F32),
        grid=(nc // tn,),
        in_specs=[pl.BlockSpec((d, LANES), lambda j: (0, 0)), pl.BlockSpec((LANES, tn), lambda j: (0, j))],
        out_specs=pl.BlockSpec((d, tn), lambda j: (0, j)),
        compiler_params=_cp("parallel"),
    )(ca_t, dm)


def _sum_devices(g):
    n = g.shape[1]

    def body(g_ref, o_ref):
        acc = g_ref[0:SUBLANES, :]
        for dvc in range(1, N_DEV):
            acc = acc + g_ref[dvc * SUBLANES : (dvc + 1) * SUBLANES, :]
        o_ref[...] = acc

    return pl.pallas_call(
        body,
        name="sum_devices",
        out_shape=jax.ShapeDtypeStruct((SUBLANES, n), F32),
        in_specs=[pl.BlockSpec(memory_space=pltpu.VMEM)],
        out_specs=pl.BlockSpec(memory_space=pltpu.VMEM),
        compiler_params=pltpu.CompilerParams(vmem_limit_bytes=VMEM_LIMIT),
    )(g)


def _sum_chips(land, sent, name):
    _, r, c = land.shape
    tr = _tile(r, max(SUBLANES * 2, (VMEM_LIMIT // 4) // (c * 2 * (4 * N_CHIP + 4 * 2))), SUBLANES * 2)

    def body(l_ref, s_ref, o_ref):
        x, y, _ = _mesh_pos()
        me = 2 * x + y
        acc = jnp.where(me == 0, s_ref[0], l_ref[0]).astype(F32)
        for k in range(1, N_CHIP):
            acc = acc + jnp.where(me == k, s_ref[k], l_ref[k]).astype(F32)
        o_ref[...] = acc

    slots = pl.BlockSpec((N_CHIP, tr, c), lambda i: (0, i, 0))
    return pl.pallas_call(
        body,
        name=name,
        out_shape=jax.ShapeDtypeStruct((r, c), F32),
        grid=(r // tr,),
        in_specs=[slots, slots],
        out_specs=pl.BlockSpec((tr, c), lambda i: (i, 0)),
        compiler_params=_cp("parallel"),
    )(land, sent)


def _mesh_pos():
    return lax.axis_index("x"), lax.axis_index("y"), lax.axis_index("c")


def _other_chips(x, y):
    return [(1 - x, y), (x, 1 - y), (1 - x, 1 - y)]


def _all_gather8(x_shard, name):
    m_per, n = x_shard.shape

    def body(x_ref, out_ref, send_sems, recv_sems, local_sem):
        x, y, c = _mesh_pos()
        me, sibling = (x, y, c), (x, y, 1 - c)
        chips = _other_chips(x, y)

        def rows(px, py, pc):
            return out_ref.at[pl.ds((4 * px + 2 * py + pc) * m_per, m_per), :]

        def copy(k, block, to, src=None):
            return pltpu.make_async_remote_copy(
                src_ref=rows(*block) if src is None else src,
                dst_ref=rows(*block),
                send_sem=send_sems.at[k],
                recv_sem=recv_sems.at[k],
                device_id=to,
                device_id_type=MESH,
            )

        mine = pltpu.make_async_copy(x_ref, rows(*me), local_sem)
        mine.start()
        first = [copy(0, me, sibling, src=x_ref)]
        first += [copy(1 + j, me, (*chip, c), src=x_ref) for j, chip in enumerate(chips)]
        for cp in first:
            cp.start()
        passed = [copy(4 + j, (*chip, c), sibling) for j, chip in enumerate(chips)]
        for j, chip in enumerate(chips):
            copy(1 + j, (*chip, c), me).wait_recv()
            passed[j].start()
        copy(0, sibling, me).wait_recv()
        for j, chip in enumerate(chips):
            copy(4 + j, (*chip, 1 - c), me).wait_recv()
        for cp in first + passed:
            cp.wait_send()
        mine.wait()

    return pl.pallas_call(
        body,
        name=name,
        out_shape=jax.ShapeDtypeStruct((N_DEV * m_per, n), x_shard.dtype),
        in_specs=[pl.BlockSpec(memory_space=pltpu.VMEM)],
        out_specs=pl.BlockSpec(memory_space=pltpu.VMEM),
        scratch_shapes=[pltpu.SemaphoreType.DMA((7,)), pltpu.SemaphoreType.DMA((7,)), pltpu.SemaphoreType.DMA],
        compiler_params=pltpu.CompilerParams(vmem_limit_bytes=VMEM_LIMIT),
    )(x_shard)


HBM_SPEC = pl.BlockSpec(memory_space=pltpu.HBM)
SEM_SPEC = pl.BlockSpec(memory_space=pltpu.SEMAPHORE)
DATAFLOW = pltpu.SideEffectType.DATAFLOW_SIDE_EFFECTING


def _half_rows(n_rows, c):
    return pl.ds(c * (n_rows // 2), n_rows // 2)


def _exchange_copies(ins, lands, send_sems, recv_sems, scatter, halves=False):
    x, y, c = _mesh_pos()
    me = 2 * x + y
    sends, recvs = [], []
    for t in range(len(ins)):
        rows = _half_rows(ins[t].shape[0], c) if halves else slice(None)
        for r, (px, py) in enumerate(_other_chips(x, y)):
            peer = 2 * px + py

            def copy(src, dst, k=3 * t + r, to=(px, py, c)):
                return pltpu.make_async_remote_copy(
                    src_ref=src, dst_ref=dst, send_sem=send_sems.at[k], recv_sem=recv_sems.at[k], device_id=to, device_id_type=MESH
                )

            if scatter:
                sends.append(copy(ins[t].at[peer], lands[t].at[me]))
                recvs.append(copy(ins[t].at[me], lands[t].at[peer]))
            else:
                sends.append(copy(ins[t].at[rows], lands[t].at[me, rows]))
                recvs.append(copy(ins[t].at[rows], lands[t].at[peer, rows]))
    return sends, recvs


def _sibling_fill(lands, name):
    nt = len(lands)

    def body(*refs):
        outs, send_sems, recv_sems = refs[nt : 2 * nt], refs[2 * nt], refs[2 * nt + 1]
        x, y, c = _mesh_pos()
        sends, recvs = [], []
        for t in range(nt):
            mine, theirs = _half_rows(outs[t].shape[1], c), _half_rows(outs[t].shape[1], 1 - c)
            for r, (px, py) in enumerate(_other_chips(x, y)):
                slot = 2 * px + py

                def copy(rows, k=3 * t + r, zone=outs[t], slot=slot):
                    part = zone.at[slot, rows]
                    return pltpu.make_async_remote_copy(
                        src_ref=part, dst_ref=part, send_sem=send_sems.at[k], recv_sem=recv_sems.at[k],
                        device_id=(x, y, 1 - c), device_id_type=MESH,
                    )

                sends.append(copy(mine))
                recvs.append(copy(theirs))
        for cp in sends:
            cp.start()
        for cp in recvs:
            cp.wait_recv()
        for cp in sends:
            cp.wait_send()

    return pl.pallas_call(
        body,
        name=name,
        out_shape=tuple(jax.ShapeDtypeStruct(a.shape, a.dtype) for a in lands),
        in_specs=[pl.BlockSpec(memory_space=pl.ANY)] * nt,
        out_specs=[pl.BlockSpec(memory_space=pl.ANY)] * nt,
        input_output_aliases={t: t for t in range(nt)},
        scratch_shapes=[pltpu.SemaphoreType.DMA((3 * nt,)), pltpu.SemaphoreType.DMA((3 * nt,))],
    )(*lands)


def _exchange_start(arrs, scatter, name, halves=False):
    nt = len(arrs)
    lands = [lax.empty(a.shape if scatter else (N_CHIP, *a.shape), a.dtype) for a in arrs]

    def body(*refs):
        ins, zones = refs[:nt], refs[nt : 2 * nt]
        send_sems, recv_sems, token = refs[2 * nt], refs[2 * nt + 1], refs[-1]
        sends, _ = _exchange_copies(ins, zones, send_sems, recv_sems, scatter, halves)
        for cp in sends:
            cp.start()
        token[...] = jnp.zeros(token.shape, F32)

    bufs = list(arrs) + list(lands)
    return pl.pallas_call(
        body,
        name=name,
        out_shape=(
            pltpu.SemaphoreType.DMA((3 * nt,)),
            pltpu.SemaphoreType.DMA((3 * nt,)),
            *[pltpu.HBM(a.shape, a.dtype) for a in bufs],
            jax.ShapeDtypeStruct((SUBLANES, LANES), F32),
        ),
        in_specs=[HBM_SPEC] * (2 * nt),
        out_specs=(SEM_SPEC, SEM_SPEC, *[HBM_SPEC] * (2 * nt), pl.BlockSpec(memory_space=pltpu.VMEM)),
        input_output_aliases={k: 2 + k for k in range(2 * nt)},
        compiler_params=pltpu.CompilerParams(has_side_effects=DATAFLOW),
    )(*[pltpu.with_memory_space_constraint(a, pltpu.HBM) for a in bufs])


def _exchange_wait(state, after, scatter, name, halves=False):
    send_sems, recv_sems, *bufs = state[:-1]
    nt = len(bufs) // 2
    afters = list(after) if isinstance(after, (list, tuple)) else [after]

    def body(*refs):
        ins, zones = refs[:nt], refs[nt : 2 * nt]
        sends, recvs = _exchange_copies(ins, zones, refs[2 * nt], refs[2 * nt + 1], scatter, halves)
        for cp in sends:
            cp.wait_send()
        for cp in recvs:
            cp.wait_recv()

    out = pl.pallas_call(
        body,
        name=name,
        out_shape=tuple(pltpu.HBM(a.shape, a.dtype) for a in bufs),
        in_specs=[HBM_SPEC] * (2 * nt) + [SEM_SPEC, SEM_SPEC] + [pl.BlockSpec(memory_space=pl.ANY)] * len(afters),
        out_specs=[HBM_SPEC] * (2 * nt),
        input_output_aliases={k: k for k in range(2 * nt)},
        compiler_params=pltpu.CompilerParams(has_side_effects=DATAFLOW),
    )(*bufs, send_sems, recv_sems, *afters)
    return list(out[:nt]), list(out[nt:])


def _swap_copies(ins, lands, send_sems, recv_sems):
    x, y, c = _mesh_pos()
    return [
        pltpu.make_async_remote_copy(
            src_ref=ins[t], dst_ref=lands[t], send_sem=send_sems.at[t], recv_sem=recv_sems.at[t],
            device_id=(x, y, 1 - c), device_id_type=MESH,
        )
        for t in range(len(ins))
    ]


def _swap_start(arrs, name):
    nt = len(arrs)
    lands = [lax.empty(a.shape, a.dtype) for a in arrs]

    def body(*refs):
        ins, zones = refs[:nt], refs[nt : 2 * nt]
        send_sems, recv_sems, token = refs[2 * nt], refs[2 * nt + 1], refs[-1]
        for cp in _swap_copies(ins, zones, send_sems, recv_sems):
            cp.start()
        token[...] = jnp.zeros(token.shape, F32)

    bufs = list(arrs) + lands
    return pl.pallas_call(
        body,
        name=name,
        out_shape=(
            pltpu.SemaphoreType.DMA((nt,)),
            pltpu.SemaphoreType.DMA((nt,)),
            *[pltpu.HBM(a.shape, a.dtype) for a in bufs],
            jax.ShapeDtypeStruct((SUBLANES, LANES), F32),
        ),
        in_specs=[HBM_SPEC] * (2 * nt),
        out_specs=(SEM_SPEC, SEM_SPEC, *[HBM_SPEC] * (2 * nt), pl.BlockSpec(memory_space=pltpu.VMEM)),
        input_output_aliases={k: 2 + k for k in range(2 * nt)},
        compiler_params=pltpu.CompilerParams(has_side_effects=DATAFLOW),
    )(*[pltpu.with_memory_space_constraint(a, pltpu.HBM) for a in bufs])


def _swap_wait(state, after, name):
    send_sems, recv_sems, *bufs = state[:-1]
    nt = len(bufs) // 2

    def body(*refs):
        cps = _swap_copies(refs[:nt], refs[nt : 2 * nt], refs[2 * nt], refs[2 * nt + 1])
        for cp in cps:
            cp.wait_send()
        for cp in cps:
            cp.wait_recv()

    out = pl.pallas_call(
        body,
        name=name,
        out_shape=tuple(pltpu.HBM(a.shape, a.dtype) for a in bufs),
        in_specs=[HBM_SPEC] * (2 * nt) + [SEM_SPEC, SEM_SPEC, pl.BlockSpec(memory_space=pl.ANY)],
        out_specs=[HBM_SPEC] * (2 * nt),
        input_output_aliases={k: k for k in range(2 * nt)},
        compiler_params=pltpu.CompilerParams(has_side_effects=DATAFLOW),
    )(*bufs, send_sems, recv_sems, after)
    return list(out[:nt]), list(out[nt:])


def _cols_from_shards(g):
    _, k, n = g.shape
    return jnp.transpose(g, (1, 0, 2)).reshape(k, N_CHIP * n)


def _cols_to_shards(a):
    k, n4 = a.shape
    return jnp.transpose(a.reshape(k, N_CHIP, n4 // N_CHIP), (1, 0, 2))


def _pad_to(vec, mult):
    n = vec.shape[0]
    return jnp.pad(vec, (0, (-n) % mult))


def kernel(x, c, positions, w_ada, b_ada, g_pre_mix, g_post_mix, w_in, g_q, w_uq, g_kv, w_ukv, conv_w_mix, conv_b_mix, w_o, g_pre_ffn, g_post_ffn, w_up, conv_w_ffn, conv_b_ffn, w_down, loss_target, m_w_ada, m_b_ada, m_g_pre_mix, m_g_post_mix, m_w_in, m_g_q, m_w_uq, m_g_kv, m_w_ukv, m_conv_w_mix, m_conv_b_mix, m_w_o, m_g_pre_ffn, m_g_post_ffn, m_w_up, m_conv_w_ffn, m_conv_b_ffn, m_w_down, v_w_ada, v_b_ada, v_g_pre_mix, v_g_post_mix, v_w_in, v_g_q, v_w_uq, v_g_kv, v_w_ukv, v_conv_w_mix, v_conv_b_mix, v_w_o, v_g_pre_ffn, v_g_post_ffn, v_w_up, v_conv_w_ffn, v_conv_b_ffn, v_w_down):
    xi, yi, ci = _mesh_pos()
    chip = 2 * xi + yi
    dev = 4 * xi + 2 * yi + ci

    s, d = x.shape[1], x.shape[2]
    ql, kl = g_q.shape[1], g_kv.shape[1]
    cwid = conv_b_mix.shape[1]
    f2 = conv_b_ffn.shape[1]
    hh = (w_uq.shape[2] * N_CHIP) // (NOPE + ROPE)
    w_att = hh * LANES
    nc_ada = w_ada.shape[2]
    lat = ql + kl + ROPE
    tc_mix = _gate_tile(cwid)
    lb = -(-(ql + kl + HEAD_PAD) // (3 * tc_mix)) * (3 * tc_mix)
    np_cols = lb + 3 * cwid
    assert cwid == hh * VDIM and w_att % tc_mix == 0

    x0 = x.reshape(s, d)
    tgt = loss_target.reshape(s, d)

    anchors = []

    def _behind(val, state):
        val, tok = lax.optimization_barrier((val, state[-1]))
        anchors.append(tok[0, 0])
        return val

    cwm_n, cwf_n = CONV_K * cwid // N_CHIP, CONV_K * f2 // N_CHIP
    pack_a = _pad_to(jnp.concatenate([c.reshape(-1), conv_w_mix.reshape(-1), conv_w_ffn.reshape(-1)]), SUBLANES * LANES)
    rows_a = _all_gather8(pack_a.reshape(SUBLANES, -1), "ag8_inputs").reshape(N_DEV, -1)
    c_all = rows_a[:, :d]
    south = rows_a[0::2]
    cw_mix = jnp.concatenate([south[j, d : d + cwm_n].reshape(CONV_K, -1) for j in range(N_CHIP)], axis=1)
    cw_ffn = jnp.concatenate([south[j, d + cwm_n : d + cwm_n + cwf_n].reshape(CONV_K, -1) for j in range(N_CHIP)], axis=1)

    b_cols = lax.dynamic_slice(b_ada, (0, chip * nc_ada), (1, nc_ada))
    mod_part, c_act = _ada_fwd(c_all, w_ada[0], b_cols)
    mod_rows = _all_gather8(mod_part, "ag8_mod")
    mod = jnp.concatenate(
        [lax.dynamic_slice_in_dim(mod_rows, 2 * N_DEV * j + dev, 1, axis=0) for j in range(N_CHIP)], axis=1
    )

    shards = [a[0].astype(BF16) for a in (w_in, w_uq, w_ukv, w_o, w_up, w_down)]
    first, mod = lax.optimization_barrier((shards[:3], mod))
    ag_a = _exchange_start(first, False, "ag_a_start", halves=True)
    mod = _behind(mod, ag_a)
    sh_m, sc_m, gt_m, sh_f, sc_f, gt_f = [mod[:, k * d : (k + 1) * d] for k in range(N_MOD)]

    inv_freq = 1.0 / (ROPE_THETA ** (jnp.arange(0, ROPE, 2, dtype=F32) / ROPE))
    invf = jnp.concatenate([inv_freq, inv_freq, jnp.zeros((LANES - ROPE,), F32)]).reshape(1, LANES)
    tabs = _rope_tables(positions.astype(F32).reshape(s, 1), invf)
    h1 = _pre_fwd(x0, g_pre_mix, sc_m, sh_m)

    def with_own(landed, own):
        return [lax.dynamic_update_slice_in_dim(g, a[None], chip, axis=0) for g, a in zip(landed, own)]

    own_w, landed_w = _exchange_wait(ag_a, [h1, tabs[0]], False, "ag_a_wait", halves=True)
    landed_w = list(_sibling_fill(landed_w, "ag_a_fill"))
    rest, landed_w = lax.optimization_barrier((shards[3:], landed_w))
    ag_b = _exchange_start(rest, False, "ag_b_start")
    h1 = _behind(h1, ag_b)
    g_in, g_uq, g_ukv = with_own(landed_w, own_w)
    full_in = _cols_from_shards(g_in)
    gate_cols = [(lat + k * cwid + j * tc_mix, tc_mix) for j in range(cwid // tc_mix) for k in range(3)]
    w_in_p = jnp.concatenate(
        [full_in[:, :lat], jnp.zeros((d, lb - lat), BF16)] + [full_in[:, o : o + n] for o, n in gate_cols], axis=1
    )
    full_uq = _cols_from_shards(g_uq).reshape(ql, hh, NOPE + ROPE)
    w_uq_p = jnp.concatenate(
        [
            full_uq[:, :, :NOPE].reshape(ql, w_att),
            jnp.pad(full_uq[:, :, NOPE:], ((0, 0), (0, 0), (0, HEAD_PAD - ROPE))).reshape(ql, w_att),
        ],
        axis=1,
    )
    full_ukv = _cols_from_shards(g_ukv).reshape(kl, hh, NOPE + VDIM)
    w_ukv_p = jnp.concatenate([full_ukv[:, :, :NOPE].reshape(kl, w_att), full_ukv[:, :, NOPE:].reshape(kl, w_att)], axis=1)

    proj = _matmul(h1, w_in_p, out_dtype=F32, tm=1024, tn=768, tk=2048, name="mm_proj")
    qn, kvn, kr = _latent_fwd(proj, g_q, g_kv, tabs, lb)
    q_f = _matmul(qn, w_uq_p, out_dtype=F32, tm=1024, tn=1024, tk=2048, name="mm_q")
    kv_p = _matmul(kvn, w_ukv_p, out_dtype=BF16, tm=1024, tn=1024, tk=2048, name="mm_kv")
    q_c, k_c = _head_cat(q_f, kv_p, kr, tabs, hh)
    cat, lse2 = _attn_fwd(q_c, k_c, kv_p, hh, w_att + cwid)
    cat = _mixer_fwd(cat, proj, cw_mix, conv_b_mix, lb, w_att)
    own_w, landed_w = _exchange_wait(ag_b, cat, False, "ag_b_wait")
    g_o, g_up, g_down = with_own(landed_w, own_w)
    w_o_f = g_o.reshape(-1, d)
    cw_ffn_p, cb_ffn_p = _pair_cols(cw_ffn), _pair_cols(conv_b_ffn)
    tcp, pair_perm = _pair_tile(f2 // 2), _pair_perm(f2 // 2)
    w_down_f = g_down.reshape(-1, d)
    mix = _matmul(cat, w_o_f, out_dtype=F32, tm=1024, tn=1024, tk=2048, name="mm_mix")

    x1, h2 = _mid_fwd(x0, mix, g_post_mix, gt_m, g_pre_ffn, sc_f, sh_f)
    up = _matmul(h2, g_up, out_dtype=F32, tm=1024, tn=tcp, tk=2048, name="mm_up", b_n_perm=pair_perm, b_col_shards=True)
    act = _ffn_act_fwd(up, cw_ffn_p, cb_ffn_p)
    y = _matmul(act, w_down_f, out_dtype=F32, tm=512, tn=1024, tk=5632, name="mm_down")
    dx2, dy, s_fin = _final(x1, y, tgt, g_post_ffn, gt_f)

    dw_down = _matmul(act, dy, ta=True, out_dtype=BF16, tm=1408, tn=1024, tk=2048, name="mm_dw_down")
    dact = _matmul(dy, w_down_f, tb=True, out_dtype=F32, tm=1024, tn=1408, tk=2048, name="mm_dact")
    dup, s_ffn_p = _ffn_act_bwd(dact, up, cw_ffn_p, cb_ffn_p)
    s_ffn = _unpair_cols(s_ffn_p)
    dw_up = _matmul(
        h2, dup, ta=True, out_dtype=BF16, tm=1024, tn=tcp, tk=2048, name="mm_dw_up", out_n_perm=pair_perm, out_col_shards=True
    )
    dh2 = _matmul_pair_k(dup, g_up, out_dtype=F32, tm=1024, tn=1024, name="mm_dh2")
    dx1, dmix, s_mid = _mid_bwd(dh2, dx2, x1, mix, g_pre_ffn, sc_f, g_post_mix, gt_m)

    dw_o = _matmul(cat, dmix, ta=True, out_dtype=BF16, tm=1024, tn=1024, tk=2048, name="mm_dw_o")
    send_b = [dw_o.reshape(N_CHIP, -1, d), dw_up, dw_down.reshape(N_CHIP, -1, d)]
    rs_b = _exchange_start(send_b, True, "rs_b_start")
    dmix = _behind(dmix, rs_b)
    dcat = _matmul(dmix, w_o_f, tb=True, out_dtype=F32, tm=1024, tn=1024, tk=2048, name="mm_dcat")
    dproj, s_mix = _mixer_bwd(dcat, proj, cw_mix, conv_b_mix, lb, w_att)
    dob, stats = _attn_bwd_prep(cat, dcat, lse2, hh)
    dq_raw, dkv_k, dkv_v, dkr_h = _attn_bwd(q_c, k_c, kv_p, dob, stats, hh)
    dkv_p = jnp.concatenate([dkv_k, dkv_v], axis=1)
    dq_p = _dq_unrope(dq_raw, tabs, hh)
    dw_uq_p = _matmul(qn, dq_p, ta=True, out_dtype=BF16, tm=1024, tn=1024, tk=1024, name="mm_dw_uq")
    dqn = _matmul(dq_p, w_uq_p, tb=True, out_dtype=F32, tm=1024, tn=1024, tk=2048, name="mm_dqn")
    dw_ukv_p = _matmul(kvn, dkv_p, ta=True, out_dtype=BF16, tm=1024, tn=1024, tk=1024, name="mm_dw_ukv")
    dkvn = _matmul(dkv_p, w_ukv_p, tb=True, out_dtype=F32, tm=1024, tn=1024, tk=2048, name="mm_dkvn")
    dproj, s_lat = _latent_bwd(dproj, proj, dqn, dkvn, dkr_h, g_q, g_kv, tabs, lb)
    dw_in_p = _matmul(h1, dproj, ta=True, out_dtype=BF16, tm=1024, tn=1536, tk=2048, name="mm_dw_in")

    n_trip = cwid // tc_mix
    ungate = [lb + (3 * j + k) * tc_mix for k in range(3) for j in range(n_trip)]
    dw_in_f = jnp.concatenate([dw_in_p[:, :lat]] + [dw_in_p[:, o : o + tc_mix] for o in ungate], axis=1)
    uq3 = dw_uq_p.reshape(ql, 2, hh, LANES)
    dw_uq_f = jnp.concatenate([uq3[:, 0], uq3[:, 1, :, :ROPE]], axis=2).reshape(ql, hh * (NOPE + ROPE))
    ukv3 = dw_ukv_p.reshape(kl, 2, hh, LANES)
    dw_ukv_f = jnp.concatenate([ukv3[:, 0], ukv3[:, 1]], axis=2).reshape(kl, hh * (NOPE + VDIM))
    send_a = [_cols_to_shards(dw_in_f), _cols_to_shards(dw_uq_f), _cols_to_shards(dw_ukv_f)]
    rs_a = _exchange_start(send_a, True, "rs_a_start")
    dproj = _behind(dproj, rs_a)

    dh1 = _matmul(dproj, w_in_p, tb=True, out_dtype=F32, tm=512, tn=1024, tk=4608, name="mm_dh1")
    grad_x, s_first = _first_bwd(dh1, dx1, x0, g_pre_mix, sc_m)

    names = ["w_in", "w_uq", "w_ukv", "w_o", "w_up", "w_down"]
    sent_b, landed_b = _exchange_wait(rs_b, s_first, True, "rs_b_wait")
    sent_a, landed_a = _exchange_wait(rs_a, landed_b[0], True, "rs_a_wait")
    landed_a, s_first = lax.optimization_barrier((landed_a, s_first))
    part = [_sum_chips(l, a, "sum_chips_" + n) for l, a, n in zip(landed_a + landed_b, sent_a + sent_b, names)]

    dmod = jnp.concatenate([s_first[0:1], s_first[1:2], s_mid[3:4], s_mid[0:1], s_mid[1:2], s_fin[0:1]], axis=1)
    small = [
        dmod,
        s_first[2:3],
        s_mid[4:5],
        s_lat[0:1, :ql],
        s_lat[0:1, ql : ql + kl],
        s_mix[3:4],
        s_mid[2:3],
        s_fin[1:2],
        s_ffn[3:4],
        s_mix[0:3].reshape(1, -1),
        s_ffn[0:3].reshape(1, -1),
        s_fin[3:4, :LANES],
    ]
    sizes = [a.shape[1] for a in small]
    offs = [0]
    for n in sizes:
        offs.append(offs[-1] + n)
    pack_g = _pad_to(jnp.concatenate(small, axis=1).reshape(-1), SUBLANES * LANES * SUBLANES).reshape(SUBLANES, -1)
    gathered = _all_gather8(pack_g, "ag8_small_grads")
    tot = _sum_devices(gathered).reshape(-1)
    part_of = lambda k: tot[offs[k] : offs[k + 1]]
    dmod_all = gathered.reshape(N_DEV, -1)[:, : N_MOD * d]
    loss = part_of(11)[0]

    g_b_ada = part_of(0).reshape(1, -1)
    g_vecs = [part_of(k).reshape(1, -1) for k in range(1, 9)]
    g_cw_mix = lax.dynamic_slice(part_of(9).reshape(CONV_K, cwid), (0, chip * (cwid // N_CHIP)), (CONV_K, cwid // N_CHIP))
    g_cw_ffn = lax.dynamic_slice(part_of(10).reshape(CONV_K, f2), (0, chip * (f2 // N_CHIP)), (CONV_K, f2 // N_CHIP))

    swap = _swap_start(part, "swap_start")
    dm_cols = _behind(lax.dynamic_slice(dmod_all, (0, chip * nc_ada), (N_DEV, nc_ada)), swap)
    g_w_ada = _ada_grad(
        jnp.pad(c_act.T, ((0, 0), (0, LANES - N_DEV))), jnp.pad(dm_cols, ((0, LANES - N_DEV), (0, 0)))
    )
    big = {"w_ada": [a[None] for a in _adamw(w_ada[0], m_w_ada[0], v_w_ada[0], [g_w_ada], "adamw_w_ada")]}
    part, other = _swap_wait(swap, big["w_ada"][1], "swap_wait")

    big_w = [w_in, w_uq, w_ukv, w_o, w_up, w_down]
    big_m = [m_w_in, m_w_uq, m_w_ukv, m_w_o, m_w_up, m_w_down]
    big_v = [v_w_in, v_w_uq, v_w_ukv, v_w_o, v_w_up, v_w_down]
    for n, w_, m_, v_, p_, o_ in zip(names, big_w, big_m, big_v, part, other):
        big[n] = [a[None] for a in _adamw(w_[0], m_[0], v_[0], [p_, o_], "adamw_" + n)]

    sm_names = ["b_ada", "g_pre_mix", "g_post_mix", "g_q", "g_kv", "conv_b_mix", "g_pre_ffn", "g_post_ffn", "conv_b_ffn",
                "conv_w_mix", "conv_w_ffn"]
    sm_w = [b_ada, g_pre_mix, g_post_mix, g_q, g_kv, conv_b_mix, g_pre_ffn, g_post_ffn, conv_b_ffn, conv_w_mix, conv_w_ffn]
    sm_m = [m_b_ada, m_g_pre_mix, m_g_post_mix, m_g_q, m_g_kv, m_conv_b_mix, m_g_pre_ffn, m_g_post_ffn, m_conv_b_ffn,
            m_conv_w_mix, m_conv_w_ffn]
    sm_v = [v_b_ada, v_g_pre_mix, v_g_post_mix, v_g_q, v_g_kv, v_conv_b_mix, v_g_pre_ffn, v_g_post_ffn, v_conv_b_ffn,
            v_conv_w_mix, v_conv_w_ffn]
    sm_g = [g_b_ada] + g_vecs + [g_cw_mix, g_cw_ffn]
    flat = lambda arrs: jnp.concatenate([a.reshape(1, -1) for a in arrs], axis=1)
    sm_out = _adamw(flat(sm_w), flat(sm_m), flat(sm_v), [flat(sm_g)], "adamw_small")
    sm = {}
    off = 0
    for n, w_ in zip(sm_names, sm_w):
        sm[n] = [o[:, off : off + w_.size].reshape(w_.shape) for o in sm_out]
        off += w_.size

    order = ["w_ada", "b_ada", "g_pre_mix", "g_post_mix", "w_in", "g_q", "w_uq", "g_kv", "w_ukv", "conv_w_mix", "conv_b_mix",
             "w_o", "g_pre_ffn", "g_post_ffn", "w_up", "conv_w_ffn", "conv_b_ffn", "w_down"]
    res = {**big, **sm}
    outs = [loss + sum(anchors), grad_x.reshape(x.shape)]
    for k in range(4):
        outs += [res[n][k] for n in order]
    return tuple(outs)
```

```python
import math

import jax
import jax.numpy as jnp
from jax import lax
from jax.experimental import pallas as pl
from jax.experimental.pallas import tpu as pltpu

F32 = jnp.float32
BF16 = jnp.bfloat16
MESH = pl.DeviceIdType.MESH

N_DEV = 8
N_CHIP = 4
LANES = 128
SUBLANES = 8
VMEM_LIMIT = 56 * 2**20

NOPE = 128
ROPE = 64
VDIM = 128
HEAD_PAD = 128
ROPE_THETA = 10000.0
RMS_EPS = 1e-6
N_MOD = 6
CONV_K = 3
ATT_FWD_BLOCK, ATT_FWD_SUB = 2048, 256
ATT_BWD_BLOCK, ATT_BWD_SUB = 1024, 256
NEG = -1e30

ADAM_LR = 0.001
ADAM_B1 = 0.9
ADAM_B2 = 0.999
ADAM_EPS = 1e-08
ADAM_WD = 0.01
ADAM_STEP = 10


def _tile(n, pref, align):
    if n <= pref:
        return n
    t = (pref // align) * align
    while t >= align:
        if n % t == 0:
            return t
        t -= align
    return n


def _cp(*sem):
    return pltpu.CompilerParams(dimension_semantics=sem, vmem_limit_bytes=VMEM_LIMIT)


def _rsq(x):
    return lax.rsqrt(jnp.mean(x * x, axis=-1, keepdims=True) + RMS_EPS)


def _norm_bwd(dn, n, r):
    return r * (dn - n * jnp.mean(dn * n, axis=-1, keepdims=True))


def _colsum(a):
    return jnp.sum(a, axis=0, keepdims=True)


def _matmul(a, b, *, ta=False, tb=False, out_dtype, tm, tn, tk, name, b_n_perm=None, out_n_perm=None,
            b_col_shards=False, out_col_shards=False):
    assert not (b_col_shards and tb)
    if b_col_shards:
        b_rows, b_cols = b.shape[1], N_CHIP * b.shape[2]
    else:
        b_rows, b_cols = b.shape
    (k_a, m) = a.shape if ta else a.shape[::-1]
    (n, k_b) = (b_rows, b_cols) if tb else (b_cols, b_rows)
    assert k_a == k_b, (a.shape, b.shape, ta, tb)
    tm, tn, tk = _tile(m, tm, LANES), _tile(n, tn, LANES), _tile(k_a, tk, LANES)
    nk = k_a // tk
    same = lambda t: t
    bn, on = b_n_perm or same, out_n_perm or same
    a_spec = pl.BlockSpec((tk, tm), lambda i, j, k: (k, i)) if ta else pl.BlockSpec((tm, tk), lambda i, j, k: (i, k))
    if b_col_shards:
        per = (b_cols // N_CHIP) // tn
        b_spec = pl.BlockSpec((None, tk, tn), lambda i, j, k: (bn(j) // per, k, bn(j) % per))
    elif tb:
        b_spec = pl.BlockSpec((tn, tk), lambda i, j, k: (bn(j), k))
    else:
        b_spec = pl.BlockSpec((tk, tn), lambda i, j, k: (k, bn(j)))
    if out_col_shards:
        per_o = (n // N_CHIP) // tn
        out_shape = jax.ShapeDtypeStruct((N_CHIP, m, n // N_CHIP), out_dtype)
        out_spec = pl.BlockSpec((None, tm, tn), lambda i, j, k: (on(j) // per_o, i, on(j) % per_o))
    else:
        out_shape = jax.ShapeDtypeStruct((m, n), out_dtype)
        out_spec = pl.BlockSpec((tm, tn), lambda i, j, k: (i, on(j)))
    dims = (((0 if ta else 1,), (1 if tb else 0,)), ((), ()))

    def body(a_ref, b_ref, o_ref, *acc):
        p = lax.dot_general(a_ref[...].astype(BF16), b_ref[...].astype(BF16), dims, preferred_element_type=F32)
        _accumulate(p, o_ref, acc, nk)

    return pl.pallas_call(
        body,
        name=name,
        out_shape=out_shape,
        grid=(m // tm, n // tn, nk),
        in_specs=[a_spec, b_spec],
        out_specs=out_spec,
        scratch_shapes=[] if nk == 1 else [pltpu.VMEM((tm, tn), F32)],
        compiler_params=_cp("parallel", "parallel", "arbitrary"),
    )(a, b)


def _accumulate(p, o_ref, acc, nk):
    if nk == 1:
        o_ref[...] = p.astype(o_ref.dtype)
        return
    k = pl.program_id(2)

    @pl.when(k == 0)
    def _():
        acc[0][...] = p

    @pl.when(k > 0)
    def _():
        acc[0][...] += p

    @pl.when(k == nk - 1)
    def _():
        o_ref[...] = acc[0][...].astype(o_ref.dtype)


def _matmul_pair_k(a, b_shards, *, out_dtype, tm, tn, name):
    m, f2 = a.shape
    n = b_shards.shape[1]
    tc = _pair_tile(f2 // 2)
    nk = (f2 // 2) // tc
    per = (f2 // N_CHIP) // tc
    tm, tn = _tile(m, tm, LANES), _tile(n, tn, LANES)

    def body(a_ref, ba_ref, bg_ref, o_ref, *acc):
        av = a_ref[...]
        p = lax.dot_general(av[:, :tc], ba_ref[...], NT, preferred_element_type=F32)
        p = p + lax.dot_general(av[:, tc:], bg_ref[...], NT, preferred_element_type=F32)
        _accumulate(p, o_ref, acc, nk)

    def w_tile(first):
        return pl.BlockSpec((None, tn, tc), lambda i, j, k: ((first + k) // per, j, (first + k) % per))

    return pl.pallas_call(
        body,
        name=name,
        out_shape=jax.ShapeDtypeStruct((m, n), out_dtype),
        grid=(m // tm, n // tn, nk),
        in_specs=[pl.BlockSpec((tm, 2 * tc), lambda i, j, k: (i, k)), w_tile(0), w_tile(nk)],
        out_specs=pl.BlockSpec((tm, tn), lambda i, j, k: (i, j)),
        scratch_shapes=[] if nk == 1 else [pltpu.VMEM((tm, tn), F32)],
        compiler_params=_cp("parallel", "parallel", "arbitrary"),
    )(a, b_shards, b_shards)


def _rope_tables(pos_col, invf):
    s = pos_col.shape[0]
    ts = _tile(s, 1024, SUBLANES)
    half = ROPE // 2

    def body(p_ref, f_ref, c_ref, sa_ref, sb_ref):
        ang = p_ref[...] * f_ref[...]
        lane = lax.broadcasted_iota(jnp.int32, ang.shape, 1)
        cs, sn = jnp.cos(ang), jnp.sin(ang)
        c_ref[...] = jnp.where(lane < ROPE, cs, 0.0)
        sa_ref[...] = jnp.where((lane >= half) & (lane < ROPE), sn, 0.0)
        sb_ref[...] = jnp.where(lane < half, -sn, 0.0)

    tab = jax.ShapeDtypeStruct((s, LANES), F32)
    return pl.pallas_call(
        body,
        name="rope_tables",
        out_shape=(tab, tab, tab),
        grid=(s // ts,),
        in_specs=[pl.BlockSpec((ts, 1), lambda i: (i, 0)), pl.BlockSpec((1, LANES), lambda i: (0, 0))],
        out_specs=[pl.BlockSpec((ts, LANES), lambda i: (i, 0))] * 3,
        compiler_params=_cp("parallel"),
    )(pos_col, invf)


def _widen(t, w):
    return t if w == LANES else jnp.tile(t, (1, w // LANES))


def _rope(x, c, sa, sb):
    w = x.shape[1]
    c, sa, sb = _widen(c, w), _widen(sa, w), _widen(sb, w)
    return x * c + pltpu.roll(x, ROPE // 2, 1) * sa + pltpu.roll(x, w - ROPE // 2, 1) * sb


def _rope_t(d, c, sa, sb):
    w = d.shape[1]
    c, sa, sb = _widen(c, w), _widen(sa, w), _widen(sb, w)
    return d * c + pltpu.roll(d * sa, w - ROPE // 2, 1) + pltpu.roll(d * sb, ROPE // 2, 1)


def _ada_fwd(c_all, w, b):
    d, nc = w.shape
    tn = _tile(nc, 512, LANES)

    def body(c_ref, w_ref, b_ref, o_ref, ca_ref):
        cv = c_ref[...]
        ca = cv * jax.nn.sigmoid(cv)
        ca_ref[...] = ca
        o_ref[...] = jnp.dot(ca.astype(BF16), w_ref[...].astype(BF16), preferred_element_type=F32) + b_ref[...]

    return pl.pallas_call(
        body,
        name="ada_fwd",
        out_shape=(jax.ShapeDtypeStruct((N_DEV, nc), F32), jax.ShapeDtypeStruct((N_DEV, d), F32)),
        grid=(nc // tn,),
        in_specs=[
            pl.BlockSpec((N_DEV, d), lambda j: (0, 0)),
            pl.BlockSpec((d, tn), lambda j: (0, j)),
            pl.BlockSpec((1, tn), lambda j: (0, j)),
        ],
        out_specs=[pl.BlockSpec((N_DEV, tn), lambda j: (0, j)), pl.BlockSpec((N_DEV, d), lambda j: (0, 0))],
        compiler_params=_cp("arbitrary"),
    )(c_all, w, b)


def _rows(ts, d):
    return pl.BlockSpec((ts, d), lambda i: (i, 0))


def _vec(d):
    return pl.BlockSpec((1, d), lambda i: (0, 0))


def _sums(d):
    return pl.BlockSpec((SUBLANES, d), lambda i: (0, 0))


def _acc_rows(ref, i, rows):
    @pl.when(i == 0)
    def _():
        ref[...] = jnp.zeros(ref.shape, ref.dtype)

    for k, r in enumerate(rows):
        ref[k : k + 1, :] += r


def _pre_fwd(x, g, sc, sh):
    s, d = x.shape
    ts = _tile(s, 512, SUBLANES)

    def body(x_ref, g_ref, sc_ref, sh_ref, h_ref):
        xv = x_ref[...]
        h_ref[...] = (((xv * _rsq(xv)) * g_ref[...]) * (1.0 + sc_ref[...]) + sh_ref[...]).astype(BF16)

    return pl.pallas_call(
        body,
        name="pre_mix_fwd",
        out_shape=jax.ShapeDtypeStruct((s, d), BF16),
        grid=(s // ts,),
        in_specs=[_rows(ts, d), _vec(d), _vec(d), _vec(d)],
        out_specs=_rows(ts, d),
        compiler_params=_cp("parallel"),
    )(x, g, sc, sh)


def _mid_fwd(x0, mix, g_post, gt, g_pre, sc, sh):
    s, d = x0.shape
    ts = _tile(s, 256, SUBLANES)

    def body(x_ref, m_ref, gp_ref, gt_ref, g_ref, sc_ref, sh_ref, x1_ref, h_ref):
        mv = m_ref[...]
        x1 = x_ref[...] + gt_ref[...] * ((mv * _rsq(mv)) * gp_ref[...])
        x1_ref[...] = x1
        h_ref[...] = (((x1 * _rsq(x1)) * g_ref[...]) * (1.0 + sc_ref[...]) + sh_ref[...]).astype(BF16)

    return pl.pallas_call(
        body,
        name="mid_fwd",
        out_shape=(jax.ShapeDtypeStruct((s, d), F32), jax.ShapeDtypeStruct((s, d), BF16)),
        grid=(s // ts,),
        in_specs=[_rows(ts, d), _rows(ts, d)] + [_vec(d)] * 5,
        out_specs=[_rows(ts, d), _rows(ts, d)],
        compiler_params=_cp("parallel"),
    )(x0, mix, g_post, gt, g_pre, sc, sh)


def _final(x1, y, tgt, g_post, gt):
    s, d = x1.shape
    ts = _tile(s, 256, SUBLANES)
    ni = s // ts

    def body(x_ref, y_ref, t_ref, gp_ref, gt_ref, dx_ref, dy_ref, s_ref):
        i = pl.program_id(0)
        yv, gp, gt_v = y_ref[...], gp_ref[...], gt_ref[...]
        r = _rsq(yv)
        n = yv * r
        err = (x_ref[...] + gt_v * (n * gp)) - t_ref[...]
        dx = err * (1.0 / d)
        dx_ref[...] = dx
        dy_ref[...] = _norm_bwd(dx * (gt_v * gp), n, r).astype(BF16)
        _acc_rows(s_ref, i, [_colsum(dx * (n * gp)), _colsum(dx * gt_v * n), _colsum(err * err)])

        @pl.when(i == ni - 1)
        def _():
            tot = jnp.sum(s_ref[2:3, :], axis=1, keepdims=True) * (0.5 / d)
            s_ref[3:4, :] = jnp.broadcast_to(tot, (1, d))

    return pl.pallas_call(
        body,
        name="final_fwd_bwd",
        out_shape=(
            jax.ShapeDtypeStruct((s, d), F32),
            jax.ShapeDtypeStruct((s, d), BF16),
            jax.ShapeDtypeStruct((SUBLANES, d), F32),
        ),
        grid=(ni,),
        in_specs=[_rows(ts, d)] * 3 + [_vec(d)] * 2,
        out_specs=[_rows(ts, d), _rows(ts, d), _sums(d)],
        compiler_params=_cp("arbitrary"),
    )(x1, y, tgt, g_post, gt)


def _mid_bwd(dh2, dx2, x1, mix, g_pre, sc, g_post, gt):
    s, d = x1.shape
    ts = _tile(s, 256, SUBLANES)

    def body(dh_ref, dx2_ref, x_ref, m_ref, g_ref, sc_ref, gp_ref, gt_ref, dx1_ref, dm_ref, s_ref):
        i = pl.program_id(0)
        dh, xv, mv = dh_ref[...], x_ref[...], m_ref[...]
        g, sc_v, gp, gt_v = g_ref[...], sc_ref[...], gp_ref[...], gt_ref[...]
        r1 = _rsq(xv)
        n1 = xv * r1
        dx1 = dx2_ref[...] + _norm_bwd(dh * (g * (1.0 + sc_v)), n1, r1)
        dx1_ref[...] = dx1
        rm = _rsq(mv)
        nm = mv * rm
        dm_ref[...] = _norm_bwd(dx1 * (gt_v * gp), nm, rm).astype(BF16)
        _acc_rows(
            s_ref,
            i,
            [
                _colsum(dh),
                _colsum(dh * (n1 * g)),
                _colsum(dh * (1.0 + sc_v) * n1),
                _colsum(dx1 * (nm * gp)),
                _colsum(dx1 * gt_v * nm),
            ],
        )

    return pl.pallas_call(
        body,
        name="mid_bwd",
        out_shape=(
            jax.ShapeDtypeStruct((s, d), F32),
            jax.ShapeDtypeStruct((s, d), BF16),
            jax.ShapeDtypeStruct((SUBLANES, d), F32),
        ),
        grid=(s // ts,),
        in_specs=[_rows(ts, d)] * 4 + [_vec(d)] * 4,
        out_specs=[_rows(ts, d), _rows(ts, d), _sums(d)],
        compiler_params=_cp("arbitrary"),
    )(dh2, dx2, x1, mix, g_pre, sc, g_post, gt)


def _first_bwd(dh1, dx1, x0, g, sc):
    s, d = x0.shape
    ts = _tile(s, 256, SUBLANES)

    def body(dh_ref, dx1_ref, x_ref, g_ref, sc_ref, dx_ref, s_ref):
        i = pl.program_id(0)
        dh, xv, gv, sc_v = dh_ref[...], x_ref[...], g_ref[...], sc_ref[...]
        r = _rsq(xv)
        n = xv * r
        dx_ref[...] = dx1_ref[...] + _norm_bwd(dh * (gv * (1.0 + sc_v)), n, r)
        _acc_rows(s_ref, i, [_colsum(dh), _colsum(dh * (n * gv)), _colsum(dh * (1.0 + sc_v) * n)])

    return pl.pallas_call(
        body,
        name="first_bwd",
        out_shape=(jax.ShapeDtypeStruct((s, d), F32), jax.ShapeDtypeStruct((SUBLANES, d), F32)),
        grid=(s // ts,),
        in_specs=[_rows(ts, d)] * 3 + [_vec(d)] * 2,
        out_specs=[_rows(ts, d), _sums(d)],
        compiler_params=_cp("arbitrary"),
    )(dh1, dx1, x0, g, sc)


def _latent_fwd(proj, g_q, g_kv, tabs, lb):
    s = proj.shape[0]
    ql, kl = g_q.shape[1], g_kv.shape[1]
    ts = _tile(s, 512, SUBLANES)

    def body(p_ref, gq_ref, gk_ref, c_ref, sa_ref, sb_ref, q_ref, kv_ref, kr_ref):
        pv = p_ref[...]
        q, kv, kr = pv[:, :ql], pv[:, ql : ql + kl], pv[:, ql + kl : ql + kl + HEAD_PAD]
        q_ref[...] = ((q * _rsq(q)) * gq_ref[...]).astype(BF16)
        kv_ref[...] = ((kv * _rsq(kv)) * gk_ref[...]).astype(BF16)
        kr_ref[...] = _rope(kr, c_ref[...], sa_ref[...], sb_ref[...]).astype(BF16)

    return pl.pallas_call(
        body,
        name="latent_fwd",
        out_shape=(
            jax.ShapeDtypeStruct((s, ql), BF16),
            jax.ShapeDtypeStruct((s, kl), BF16),
            jax.ShapeDtypeStruct((s, HEAD_PAD), BF16),
        ),
        grid=(s // ts,),
        in_specs=[_rows(ts, lb), _vec(ql), _vec(kl)] + [_rows(ts, LANES)] * 3,
        out_specs=[_rows(ts, ql), _rows(ts, kl), _rows(ts, HEAD_PAD)],
        compiler_params=_cp("parallel"),
    )(proj, g_q, g_kv, *tabs)


def _latent_bwd(dproj, proj, dqn, dkvn, dkr_h, g_q, g_kv, tabs, lb):
    s = proj.shape[0]
    ql, kl = g_q.shape[1], g_kv.shape[1]
    hw = dkr_h.shape[1]
    ts = _tile(s, 256, SUBLANES)
    pad = lb - ql - kl - HEAD_PAD

    def body(_, p_ref, dq_ref, dkv_ref, dkr_ref, gq_ref, gk_ref, c_ref, sa_ref, sb_ref, o_ref, s_ref):
        i = pl.program_id(0)
        pv = p_ref[...]
        q, kv = pv[:, :ql], pv[:, ql : ql + kl]
        dqn_v, dkvn_v = dq_ref[...], dkv_ref[...]
        rq = _rsq(q)
        nq = q * rq
        rk = _rsq(kv)
        nk = kv * rk
        dkr = dkr_ref[:, :HEAD_PAD]
        for h in range(1, hw // HEAD_PAD):
            dkr = dkr + dkr_ref[:, h * HEAD_PAD : (h + 1) * HEAD_PAD]
        parts = [
            _norm_bwd(dqn_v * gq_ref[...], nq, rq).astype(BF16),
            _norm_bwd(dkvn_v * gk_ref[...], nk, rk).astype(BF16),
            _rope_t(dkr, c_ref[...], sa_ref[...], sb_ref[...]).astype(BF16),
        ]
        if pad:
            parts.append(jnp.zeros((ts, pad), BF16))
        o_ref[...] = jnp.concatenate(parts, axis=1)
        row = [_colsum(dqn_v * nq), _colsum(dkvn_v * nk), jnp.zeros((1, lb - ql - kl), F32)]
        _acc_rows(s_ref, i, [jnp.concatenate(row, axis=1)])

    return pl.pallas_call(
        body,
        name="latent_bwd",
        out_shape=(jax.ShapeDtypeStruct(dproj.shape, BF16), jax.ShapeDtypeStruct((SUBLANES, lb), F32)),
        grid=(s // ts,),
        in_specs=[pl.BlockSpec(memory_space=pl.ANY), _rows(ts, lb), _rows(ts, ql), _rows(ts, kl), _rows(ts, hw)]
        + [_vec(ql), _vec(kl)]
        + [_rows(ts, LANES)] * 3,
        out_specs=[_rows(ts, lb), _sums(lb)],
        input_output_aliases={0: 0},
        compiler_params=_cp("arbitrary"),
    )(dproj, proj, dqn, dkvn, dkr_h, g_q, g_kv, *tabs)


def _conv3(ext, w, b):
    return (pltpu.roll(ext, 2, 0) * w[0:1] + pltpu.roll(ext, 1, 0) * w[1:2]) + ext * w[2:3] + b


def _conv3_t(du, w):
    n = du.shape[0]
    return du * w[2:3] + pltpu.roll(du, n - 1, 0) * w[1:2] + pltpu.roll(du, n - 2, 0) * w[0:1]


def _halo_maps(ts, s):
    r8, last = ts // SUBLANES, s // SUBLANES - 1
    prev = lambda i: jnp.maximum(i * r8 - 1, 0)
    nxt = lambda i: jnp.minimum((i + 1) * r8, last)
    return prev, nxt


def _gate_tile(cwid):
    return _tile(cwid, 512, LANES)


def _mixer_fwd(cat, proj, cw, cb, lb, col0):
    s = proj.shape[0]
    cwid = cw.shape[1]
    ts = _tile(s, 512, SUBLANES)
    tc = _gate_tile(cwid)
    assert lb % (3 * tc) == 0 and col0 % tc == 0
    t0, oc = lb // (3 * tc), col0 // tc
    prev, _ = _halo_maps(ts, s)

    def body(_, g_ref, p_ref, w_ref, b_ref, o_ref):
        keep = jnp.where(pl.program_id(1) > 0, 1.0, 0.0)
        gv, pv = g_ref[...], p_ref[...]
        ext = jnp.concatenate([pv[:, tc : 2 * tc] * pv[:, 2 * tc :] * keep, gv[:, tc : 2 * tc] * gv[:, 2 * tc :]], axis=0)
        o_ref[...] = (gv[:, :tc] * _conv3(ext, w_ref[...], b_ref[...])[SUBLANES:]).astype(BF16)

    return pl.pallas_call(
        body,
        name="mixer_fwd",
        out_shape=jax.ShapeDtypeStruct(cat.shape, BF16),
        grid=(cwid // tc, s // ts),
        in_specs=[
            pl.BlockSpec(memory_space=pl.ANY),
            pl.BlockSpec((ts, 3 * tc), lambda j, i: (i, t0 + j)),
            pl.BlockSpec((SUBLANES, 3 * tc), lambda j, i: (prev(i), t0 + j)),
            pl.BlockSpec((CONV_K, tc), lambda j, i: (0, j)),
            pl.BlockSpec((1, tc), lambda j, i: (0, j)),
        ],
        out_specs=pl.BlockSpec((ts, tc), lambda j, i: (i, oc + j)),
        input_output_aliases={0: 0},
        compiler_params=_cp("parallel", "arbitrary"),
    )(cat, proj, proj, cw, cb)


def _mixer_bwd(dcat, proj, cw, cb, lb, col0):
    s, np_cols = proj.shape
    cwid = cw.shape[1]
    ts = _tile(s, 256, SUBLANES)
    tc = _gate_tile(cwid)
    t0, oc = lb // (3 * tc), col0 // tc
    ni = s // ts
    prev, nxt = _halo_maps(ts, s)

    def body(d_ref, dn_ref, g_ref, gp_ref, gn_ref, w_ref, b_ref, dg_ref, s_ref):
        i = pl.program_id(1)
        keep_p = jnp.where(i > 0, 1.0, 0.0)
        keep_n = jnp.where(i < ni - 1, 1.0, 0.0)
        w = w_ref[...]
        gv, gp, gn = g_ref[...], gp_ref[...], gn_ref[...]
        gc = jnp.concatenate([gp[:, tc : 2 * tc], gv[:, tc : 2 * tc], gn[:, tc : 2 * tc]], axis=0)
        ci = jnp.concatenate([gp[:, 2 * tc :] * keep_p, gv[:, 2 * tc :], gn[:, 2 * tc :]], axis=0)
        u = gc * ci
        cv = _conv3(u, w, b_ref[...])[SUBLANES:]
        dco = jnp.concatenate([d_ref[...], dn_ref[...] * keep_n], axis=0)
        gb = jnp.concatenate([gv[:, :tc], gn[:, :tc]], axis=0)
        dcv = dco * gb
        du = _conv3_t(dcv, w)[:ts]
        dg_ref[:, :tc] = (dco * cv)[:ts].astype(BF16)
        dg_ref[:, tc : 2 * tc] = (du * gv[:, 2 * tc :]).astype(BF16)
        dg_ref[:, 2 * tc :] = (du * gv[:, tc : 2 * tc]).astype(BF16)
        dt = dcv[:ts]
        u1, u2 = pltpu.roll(u, 1, 0), pltpu.roll(u, 2, 0)
        lo, hi = SUBLANES, SUBLANES + ts
        _acc_rows(s_ref, i, [_colsum(dt * u2[lo:hi]), _colsum(dt * u1[lo:hi]), _colsum(dt * u[lo:hi]), _colsum(dt)])

    def triple(rows, which):
        return pl.BlockSpec((rows, 3 * tc), lambda j, i: (which(i), t0 + j))

    return pl.pallas_call(
        body,
        name="mixer_bwd",
        out_shape=(jax.ShapeDtypeStruct((s, np_cols), BF16), jax.ShapeDtypeStruct((SUBLANES, cwid), F32)),
        grid=(cwid // tc, ni),
        in_specs=[
            pl.BlockSpec((ts, tc), lambda j, i: (i, oc + j)),
            pl.BlockSpec((SUBLANES, tc), lambda j, i: (nxt(i), oc + j)),
            triple(ts, lambda i: i), triple(SUBLANES, prev), triple(SUBLANES, nxt),
            pl.BlockSpec((CONV_K, tc), lambda j, i: (0, j)),
            pl.BlockSpec((1, tc), lambda j, i: (0, j)),
        ],
        out_specs=[triple(ts, lambda i: i), pl.BlockSpec((SUBLANES, tc), lambda j, i: (0, j))],
        compiler_params=_cp("parallel", "arbitrary"),
    )(dcat, dcat, proj, proj, proj, cw, cb)


def _pair_tile(f):
    return _tile(f, 1408, LANES)


def _pair_perm(f):
    nj = f // _pair_tile(f)
    return lambda p: (p % 2) * nj + p // 2


def _pair_cols(a):
    r, f2 = a.shape
    tc = _pair_tile(f2 // 2)
    return a.reshape(r, 2, f2 // (2 * tc), tc).transpose(0, 2, 1, 3).reshape(r, f2)


def _unpair_cols(a):
    r, f2 = a.shape
    tc = _pair_tile(f2 // 2)
    return a.reshape(r, f2 // (2 * tc), 2, tc).transpose(0, 2, 1, 3).reshape(r, f2)


def _ffn_act_fwd(up, cw, cb):
    s, f2 = up.shape
    f = f2 // 2
    ts = _tile(s, 512, SUBLANES)
    tc = _pair_tile(f)
    prev, _ = _halo_maps(ts, s)

    def body(u_ref, p_ref, w_ref, b_ref, o_ref):
        keep = jnp.where(pl.program_id(1) > 0, 1.0, 0.0)
        ext = jnp.concatenate([p_ref[...] * keep, u_ref[...]], axis=0)
        u = _conv3(ext, w_ref[...], b_ref[...])[SUBLANES:]
        a, g = u[:, :tc], u[:, tc:]
        o_ref[...] = ((g * jax.nn.sigmoid(g)) * a).astype(BF16)

    def pair(rows, which):
        return pl.BlockSpec((rows, 2 * tc), lambda j, i: (which(i), j))

    return pl.pallas_call(
        body,
        name="ffn_act_fwd",
        out_shape=jax.ShapeDtypeStruct((s, f), BF16),
        grid=(f // tc, s // ts),
        in_specs=[pair(ts, lambda i: i), pair(SUBLANES, prev), pair(CONV_K, lambda i: 0), pair(1, lambda i: 0)],
        out_specs=pl.BlockSpec((ts, tc), lambda j, i: (i, j)),
        compiler_params=_cp("parallel", "arbitrary"),
    )(up, up, cw, cb)


def _ffn_act_bwd(dact, up, cw, cb):
    s, f2 = up.shape
    f = f2 // 2
    ts = _tile(s, 256, SUBLANES)
    tc = _pair_tile(f)
    nj, ni = f // tc, s // ts
    prev, nxt = _halo_maps(ts, s)

    def body(d_ref, dn_ref, u_ref, up_ref, un_ref, w_ref, b_ref, dup_ref, s_ref):
        i = pl.program_id(1)
        keep_p = jnp.where(i > 0, 1.0, 0.0)
        keep_n = jnp.where(i < ni - 1, 1.0, 0.0)
        w = w_ref[...]
        ext = jnp.concatenate([up_ref[...] * keep_p, u_ref[...], un_ref[...]], axis=0)
        u = _conv3(ext, w, b_ref[...])[SUBLANES:]
        a, g = u[:, :tc], u[:, tc:]
        dact_v = jnp.concatenate([d_ref[...], dn_ref[...] * keep_n], axis=0)
        sg = jax.nn.sigmoid(g)
        du = jnp.concatenate([dact_v * (g * sg), dact_v * a * (sg * (1.0 + g * (1.0 - sg)))], axis=1)
        dup_ref[...] = _conv3_t(du, w)[:ts].astype(BF16)
        dt = du[:ts]
        lo, hi = SUBLANES, SUBLANES + ts
        e1, e2 = pltpu.roll(ext, 1, 0), pltpu.roll(ext, 2, 0)
        _acc_rows(s_ref, i, [_colsum(dt * e2[lo:hi]), _colsum(dt * e1[lo:hi]), _colsum(dt * ext[lo:hi]), _colsum(dt)])

    def pair(rows, which):
        return pl.BlockSpec((rows, 2 * tc), lambda j, i: (which(i), j))

    return pl.pallas_call(
        body,
        name="ffn_act_bwd",
        out_shape=(jax.ShapeDtypeStruct((s, f2), BF16), jax.ShapeDtypeStruct((SUBLANES, f2), F32)),
        grid=(nj, ni),
        in_specs=[
            pl.BlockSpec((ts, tc), lambda j, i: (i, j)),
            pl.BlockSpec((SUBLANES, tc), lambda j, i: (nxt(i), j)),
            pair(ts, lambda i: i), pair(SUBLANES, prev), pair(SUBLANES, nxt),
            pair(CONV_K, lambda i: 0), pair(1, lambda i: 0),
        ],
        out_specs=[pair(ts, lambda i: i), pair(SUBLANES, lambda i: 0)],
        compiler_params=_cp("parallel", "arbitrary"),
    )(dact, dact, up, up, up, cw, cb)


ATT_SCALE = 1.0 / math.sqrt(NOPE + ROPE)
LOG2E = math.log2(math.e)
ATT_C2 = ATT_SCALE * LOG2E
STAT_SPLIT = 64
NT = (((1,), (1,)), ((), ()))
TN = (((0,), (0,)), ((), ()))


def _head_cat(q, kv, kr, tabs, n_heads):
    s, w2 = q.shape
    w = w2 // 2
    ts = _tile(s, 512, SUBLANES)
    hd = NOPE + HEAD_PAD

    def body(q_ref, kv_ref, kr_ref, c_ref, sa_ref, sb_ref, qc_ref, kc_ref):
        qv = q_ref[...]
        qr = _rope(qv[:, w:], c_ref[...], sa_ref[...], sb_ref[...]).astype(BF16)
        krv = kr_ref[...]
        for h in range(n_heads):
            qc_ref[:, h * hd : h * hd + NOPE] = qv[:, h * NOPE : (h + 1) * NOPE].astype(BF16)
            qc_ref[:, h * hd + NOPE : (h + 1) * hd] = qr[:, h * HEAD_PAD : (h + 1) * HEAD_PAD]
            kc_ref[:, h * hd : h * hd + NOPE] = kv_ref[:, h * NOPE : (h + 1) * NOPE]
            kc_ref[:, h * hd + NOPE : (h + 1) * hd] = krv

    out = jax.ShapeDtypeStruct((s, n_heads * hd), BF16)
    return pl.pallas_call(
        body,
        name="head_cat",
        out_shape=(out, out),
        grid=(s // ts,),
        in_specs=[_rows(ts, w2), _rows(ts, w), _rows(ts, HEAD_PAD)] + [_rows(ts, LANES)] * 3,
        out_specs=[_rows(ts, n_heads * hd)] * 2,
        compiler_params=_cp("parallel"),
    )(q, kv, kr, *tabs)


def _attn_fwd(qc, kc, kv, n_heads, cat_cols):
    s = qc.shape[0]
    t = _tile(s, ATT_FWD_BLOCK, LANES)
    sub = _tile(t, ATT_FWD_SUB, LANES)
    hh = n_heads
    hd = NOPE + HEAD_PAD

    def body(q_ref, k_ref, v_ref, o_ref, lse_ref, m_s, l_s, acc_s):
        i = pl.program_id(1)
        m_s[...] = jnp.full(m_s.shape, NEG, F32)
        l_s[...] = jnp.zeros(l_s.shape, F32)
        acc_s[...] = jnp.zeros(acc_s.shape, F32)

        def chunk(k0, diag):
            m_all, l_all, acc_all = m_s[...], l_s[...], acc_s[...]
            new_m, new_l, new_acc = [], [], []

            def scores(r0):
                ncol = r0 + sub if diag else t
                return lax.dot_general(q_ref[pl.ds(r0, sub), :], k_ref[pl.ds(k0, ncol), :], NT, preferred_element_type=F32)

            sc_next = scores(0)
            for r0 in range(0, t, sub):
                ncol = r0 + sub if diag else t
                sc = sc_next
                if r0 + sub < t:
                    sc_next = scores(r0 + sub)
                if diag:
                    row = lax.broadcasted_iota(jnp.int32, sc.shape, 0) + r0
                    col = lax.broadcasted_iota(jnp.int32, sc.shape, 1)
                    sc = jnp.where(col <= row, sc, NEG)
                m_prev = m_all[r0 : r0 + sub]
                m_new = jnp.maximum(m_prev, jnp.max(sc, axis=1, keepdims=True))
                alpha = jnp.exp2((m_prev - m_new) * ATT_C2)
                p = jnp.exp2((sc - m_new) * ATT_C2)
                pv = jnp.dot(p.astype(BF16), v_ref[pl.ds(k0, ncol), :], preferred_element_type=F32)
                new_m.append(m_new)
                new_l.append(alpha * l_all[r0 : r0 + sub] + jnp.sum(p, axis=1, keepdims=True))
                new_acc.append(alpha * acc_all[r0 : r0 + sub] + pv)
            m_s[...] = jnp.concatenate(new_m, axis=0)
            l_s[...] = jnp.concatenate(new_l, axis=0)
            acc_s[...] = jnp.concatenate(new_acc, axis=0)

        def loop_body(k, carry):
            chunk(pl.multiple_of(k * t, t), False)
            return carry

        lax.fori_loop(0, i, loop_body, 0)
        chunk(pl.multiple_of(i * t, t), True)
        l = l_s[...]
        o_ref[...] = (acc_s[...] / l).astype(BF16)
        lse_ref[...] = jnp.broadcast_to(m_s[...] * ATT_C2 + jnp.log(l) * LOG2E, lse_ref.shape)

    return pl.pallas_call(
        body,
        name="attn_fwd",
        out_shape=(jax.ShapeDtypeStruct((s, cat_cols), BF16), jax.ShapeDtypeStruct((s, hh * LANES), F32)),
        grid=(hh, s // t),
        in_specs=[
            pl.BlockSpec((t, hd), lambda h, i: (i, h)),
            pl.BlockSpec((s, hd), lambda h, i: (0, h)),
            pl.BlockSpec((s, VDIM), lambda h, i: (0, hh + h)),
        ],
        out_specs=[pl.BlockSpec((t, VDIM), lambda h, i: (i, h)), pl.BlockSpec((t, LANES), lambda h, i: (i, h))],
        scratch_shapes=[pltpu.VMEM((t, 1), F32), pltpu.VMEM((t, 1), F32), pltpu.VMEM((t, VDIM), F32)],
        compiler_params=_cp("parallel", "parallel"),
    )(qc, kc, kv)


def _attn_bwd_prep(cat, dcat, lse2, n_heads):
    s, w = lse2.shape
    ts = _tile(s, 512, SUBLANES)

    def body(o_ref, do_ref, lse_ref, dob_ref, st_ref):
        do = do_ref[...]
        dob_ref[...] = do.astype(BF16)
        prod = do * o_ref[...].astype(F32)
        lane = lax.broadcasted_iota(jnp.int32, (ts, LANES), 1)
        for h in range(n_heads):
            cols = slice(h * LANES, (h + 1) * LANES)
            dsum = jnp.sum(prod[:, cols], axis=1, keepdims=True)
            st_ref[:, cols] = jnp.where(lane < STAT_SPLIT, lse_ref[:, cols], dsum)

    return pl.pallas_call(
        body,
        name="attn_bwd_prep",
        out_shape=(jax.ShapeDtypeStruct((s, w), BF16), jax.ShapeDtypeStruct((s, w), F32)),
        grid=(s // ts,),
        in_specs=[_rows(ts, w)] * 3,
        out_specs=[_rows(ts, w)] * 2,
        compiler_params=_cp("parallel"),
    )(cat, dcat, lse2)


def _attn_bwd(qc, kc, kv, dob, stats, n_heads):
    s = qc.shape[0]
    t = _tile(s, ATT_BWD_BLOCK, LANES)
    sub = _tile(t, ATT_BWD_SUB, LANES)
    nb = s // t
    hh = n_heads
    hd = NOPE + HEAD_PAD
    w = hh * LANES

    def body(q_ref, k_ref, v_ref, do_ref, st_ref, dq_ref, dkn_ref, dv_ref, dkr_ref, dk_s, dv_s):
        j = pl.program_id(1)

        @pl.when(j == 0)
        def _():
            dq_ref[...] = jnp.zeros(dq_ref.shape, F32)

        dk_s[...] = jnp.zeros(dk_s.shape, F32)
        dv_s[...] = jnp.zeros(dv_s.shape, F32)

        def pair(i0, diag):
            def width(r0):
                return r0 + sub if diag else t

            def products(r0):
                rows = pl.ds(i0 + r0, sub)
                sc_ = lax.dot_general(q_ref[rows, :], k_ref[0 : width(r0), :], NT, preferred_element_type=F32)
                dp_ = lax.dot_general(do_ref[rows, :], v_ref[0 : width(r0), :], NT, preferred_element_type=F32)
                return sc_, dp_

            nxt = products(0)
            for r0 in range(0, t, sub):
                ncol = width(r0)
                rows = pl.ds(i0 + r0, sub)
                kk = k_ref[0:ncol, :]
                qq, do, st = q_ref[rows, :], do_ref[rows, :], st_ref[rows, :]
                sc, dp = nxt
                if r0 + sub < t:
                    nxt = products(r0 + sub)
                if diag:
                    row = lax.broadcasted_iota(jnp.int32, sc.shape, 0) + r0
                    col = lax.broadcasted_iota(jnp.int32, sc.shape, 1)
                    sc = jnp.where(col <= row, sc, NEG)
                p = jnp.exp2(sc * ATT_C2 - st[:, 0:1])
                dv_s[0:ncol, :] += lax.dot_general(p.astype(BF16), do, TN, preferred_element_type=F32)
                ds = (p * (dp - st[:, STAT_SPLIT : STAT_SPLIT + 1]) * ATT_SCALE).astype(BF16)
                dk_s[0:ncol, :] += lax.dot_general(ds, qq, TN, preferred_element_type=F32)
                dq_ref[rows, :] += jnp.dot(ds, kk, preferred_element_type=F32)

        pair(pl.multiple_of(j * t, t), True)

        def loop_body(i, carry):
            pair(pl.multiple_of(i * t, t), False)
            return carry

        lax.fori_loop(j + 1, nb, loop_body, 0)
        dkn_ref[...] = dk_s[:, :NOPE].astype(BF16)
        dv_ref[...] = dv_s[...].astype(BF16)
        dkr_ref[...] = dk_s[:, NOPE:]

    whole = lambda width, off: pl.BlockSpec((s, width), lambda h, j: (0, off + h))
    blk = lambda width, off: pl.BlockSpec((t, width), lambda h, j: (j, off + h))
    return pl.pallas_call(
        body,
        name="attn_bwd",
        out_shape=(
            jax.ShapeDtypeStruct((s, hh * hd), F32),
            jax.ShapeDtypeStruct((s, w), BF16),
            jax.ShapeDtypeStruct((s, w), BF16),
            jax.ShapeDtypeStruct((s, w), F32),
        ),
        grid=(hh, nb),
        in_specs=[whole(hd, 0), blk(hd, 0), blk(VDIM, hh), whole(VDIM, 0), whole(LANES, 0)],
        out_specs=[whole(hd, 0), blk(NOPE, 0), blk(VDIM, 0), blk(HEAD_PAD, 0)],
        scratch_shapes=[pltpu.VMEM((t, hd), F32), pltpu.VMEM((t, VDIM), F32)],
        compiler_params=_cp("parallel", "arbitrary"),
    )(qc, kc, kv, dob, stats)


def _dq_unrope(dq, tabs, n_heads):
    s = dq.shape[0]
    hd = NOPE + HEAD_PAD
    w = n_heads * LANES
    ts = _tile(s, 512, SUBLANES)

    def body(d_ref, c_ref, sa_ref, sb_ref, o_ref):
        c, sa, sb = c_ref[...], sa_ref[...], sb_ref[...]
        for h in range(n_heads):
            o_ref[:, h * NOPE : (h + 1) * NOPE] = d_ref[:, h * hd : h * hd + NOPE].astype(BF16)
            rot = _rope_t(d_ref[:, h * hd + NOPE : (h + 1) * hd], c, sa, sb)
            o_ref[:, w + h * HEAD_PAD : w + (h + 1) * HEAD_PAD] = rot.astype(BF16)

    return pl.pallas_call(
        body,
        name="dq_unrope",
        out_shape=jax.ShapeDtypeStruct((s, 2 * w), BF16),
        grid=(s // ts,),
        in_specs=[_rows(ts, n_heads * hd)] + [_rows(ts, LANES)] * 3,
        out_specs=_rows(ts, 2 * w),
        compiler_params=_cp("parallel"),
    )(dq, *tabs)


def _adamw(w, m, v, grads, name):
    r, c = w.shape
    budget_rows = max(SUBLANES, (VMEM_LIMIT // 3) // (4 * c * 2 * (7 + len(grads))))
    tr = _tile(r, budget_rows, SUBLANES)
    ng = len(grads)
    c1 = 1.0 - ADAM_B1**ADAM_STEP
    c2 = 1.0 - ADAM_B2**ADAM_STEP

    def body(*refs):
        w_ref, m_ref, v_ref = refs[:3]
        g_ref, d_ref, nm_ref, nv_ref = refs[3 + ng :]
        g = refs[3][...]
        for extra in refs[4 : 3 + ng]:
            g = g + extra[...]
        mn = ADAM_B1 * m_ref[...] + (1.0 - ADAM_B1) * g
        vn = ADAM_B2 * v_ref[...] + (1.0 - ADAM_B2) * (g * g)
        g_ref[...] = g
        nm_ref[...] = mn
        nv_ref[...] = vn
        d_ref[...] = -ADAM_LR * ((mn / c1) / (jnp.sqrt(vn / c2) + ADAM_EPS) + ADAM_WD * w_ref[...])

    blk = pl.BlockSpec((tr, c), lambda i: (i, 0))
    out = jax.ShapeDtypeStruct((r, c), F32)
    return pl.pallas_call(
        body,
        name=name,
        out_shape=(out, out, out, out),
        grid=(r // tr,),
        in_specs=[blk] * (3 + ng),
        out_specs=[blk] * 4,
        compiler_params=_cp("parallel"),
    )(w, m, v, *grads)


def _ada_grad(ca_t, dm):
    d = ca_t.shape[0]
    nc = dm.shape[1]
    tn = _tile(nc, 512, LANES)

    def body(a_ref, b_ref, o_ref):
        o_ref[...] = jnp.dot(a_ref[...].astype(BF16), b_ref[...].astype(BF16), preferred_element_type=F32)

    return pl.pallas_call(
        body,
        name="ada_grad",
        out_shape=jax.ShapeDtypeStruct((d, nc), F32),
        grid=(nc // tn,),
        in_specs=[pl.BlockSpec((d, LANES), lambda j: (0, 0)), pl.BlockSpec((LANES, tn), lambda j: (0, j))],
        out_specs=pl.BlockSpec((d, tn), lambda j: (0, j)),
        compiler_params=_cp("parallel"),
    )(ca_t, dm)


def _sum_devices(g):
    n = g.shape[1]

    def body(g_ref, o_ref):
        acc = g_ref[0:SUBLANES, :]
        for dvc in range(1, N_DEV):
            acc = acc + g_ref[dvc * SUBLANES : (dvc + 1) * SUBLANES, :]
        o_ref[...] = acc

    return pl.pallas_call(
        body,
        name="sum_devices",
        out_shape=jax.ShapeDtypeStruct((SUBLANES, n), F32),
        in_specs=[pl.BlockSpec(memory_space=pltpu.VMEM)],
        out_specs=pl.BlockSpec(memory_space=pltpu.VMEM),
        compiler_params=pltpu.CompilerParams(vmem_limit_bytes=VMEM_LIMIT),
    )(g)


def _sum_chips(land, sent, name):
    _, r, c = land.shape
    tr = _tile(r, max(SUBLANES * 2, (VMEM_LIMIT // 4) // (c * 2 * (4 * N_CHIP + 4 * 2))), SUBLANES * 2)

    def body(l_ref, s_ref, o_ref):
        x, y, _ = _mesh_pos()
        me = 2 * x + y
        acc = jnp.where(me == 0, s_ref[0], l_ref[0]).astype(F32)
        for k in range(1, N_CHIP):
            acc = acc + jnp.where(me == k, s_ref[k], l_ref[k]).astype(F32)
        o_ref[...] = acc

    slots = pl.BlockSpec((N_CHIP, tr, c), lambda i: (0, i, 0))
    return pl.pallas_call(
        body,
        name=name,
        out_shape=jax.ShapeDtypeStruct((r, c), F32),
        grid=(r // tr,),
        in_specs=[slots, slots],
        out_specs=pl.BlockSpec((tr, c), lambda i: (i, 0)),
        compiler_params=_cp("parallel"),
    )(land, sent)


def _mesh_pos():
    return lax.axis_index("x"), lax.axis_index("y"), lax.axis_index("c")


def _other_chips(x, y):
    return [(1 - x, y), (x, 1 - y), (1 - x, 1 - y)]


def _all_gather8(x_shard, name):
    m_per, n = x_shard.shape

    def body(x_ref, out_ref, send_sems, recv_sems, local_sem):
        x, y, c = _mesh_pos()
        me, sibling = (x, y, c), (x, y, 1 - c)
        chips = _other_chips(x, y)

        def rows(px, py, pc):
            return out_ref.at[pl.ds((4 * px + 2 * py + pc) * m_per, m_per), :]

        def copy(k, block, to, src=None):
            return pltpu.make_async_remote_copy(
                src_ref=rows(*block) if src is None else src,
                dst_ref=rows(*block),
                send_sem=send_sems.at[k],
                recv_sem=recv_sems.at[k],
                device_id=to,
                device_id_type=MESH,
            )

        mine = pltpu.make_async_copy(x_ref, rows(*me), local_sem)
        mine.start()
        first = [copy(0, me, sibling, src=x_ref)]
        first += [copy(1 + j, me, (*chip, c), src=x_ref) for j, chip in enumerate(chips)]
        for cp in first:
            cp.start()
        passed = [copy(4 + j, (*chip, c), sibling) for j, chip in enumerate(chips)]
        for j, chip in enumerate(chips):
            copy(1 + j, (*chip, c), me).wait_recv()
            passed[j].start()
        copy(0, sibling, me).wait_recv()
        for j, chip in enumerate(chips):
            copy(4 + j, (*chip, 1 - c), me).wait_recv()
        for cp in first + passed:
            cp.wait_send()
        mine.wait()

    return pl.pallas_call(
        body,
        name=name,
        out_shape=jax.ShapeDtypeStruct((N_DEV * m_per, n), x_shard.dtype),
        in_specs=[pl.BlockSpec(memory_space=pltpu.VMEM)],
        out_specs=pl.BlockSpec(memory_space=pltpu.VMEM),
        scratch_shapes=[pltpu.SemaphoreType.DMA((7,)), pltpu.SemaphoreType.DMA((7,)), pltpu.SemaphoreType.DMA],
        compiler_params=pltpu.CompilerParams(vmem_limit_bytes=VMEM_LIMIT),
    )(x_shard)


HBM_SPEC = pl.BlockSpec(memory_space=pltpu.HBM)
SEM_SPEC = pl.BlockSpec(memory_space=pltpu.SEMAPHORE)
DATAFLOW = pltpu.SideEffectType.DATAFLOW_SIDE_EFFECTING


def _half_rows(n_rows, c):
    return pl.ds(c * (n_rows // 2), n_rows // 2)


def _exchange_copies(ins, lands, send_sems, recv_sems, scatter, halves=False):
    x, y, c = _mesh_pos()
    me = 2 * x + y
    sends, recvs = [], []
    for t in range(len(ins)):
        rows = _half_rows(ins[t].shape[0], c) if halves else slice(None)
        for r, (px, py) in enumerate(_other_chips(x, y)):
            peer = 2 * px + py

            def copy(src, dst, k=3 * t + r, to=(px, py, c)):
                return pltpu.make_async_remote_copy(
                    src_ref=src, dst_ref=dst, send_sem=send_sems.at[k], recv_sem=recv_sems.at[k], device_id=to, device_id_type=MESH
                )

            if scatter:
                sends.append(copy(ins[t].at[peer], lands[t].at[me]))
                recvs.append(copy(ins[t].at[me], lands[t].at[peer]))
            else:
                sends.append(copy(ins[t].at[rows], lands[t].at[me, rows]))
                recvs.append(copy(ins[t].at[rows], lands[t].at[peer, rows]))
    return sends, recvs


def _sibling_fill(lands, name):
    nt = len(lands)

    def body(*refs):
        outs, send_sems, recv_sems = refs[nt : 2 * nt], refs[2 * nt], refs[2 * nt + 1]
        x, y, c = _mesh_pos()
        sends, recvs = [], []
        for t in range(nt):
            mine, theirs = _half_rows(outs[t].shape[1], c), _half_rows(outs[t].shape[1], 1 - c)
            for r, (px, py) in enumerate(_other_chips(x, y)):
                slot = 2 * px + py

                def copy(rows, k=3 * t + r, zone=outs[t], slot=slot):
                    part = zone.at[slot, rows]
                    return pltpu.make_async_remote_copy(
                        src_ref=part, dst_ref=part, send_sem=send_sems.at[k], recv_sem=recv_sems.at[k],
                        device_id=(x, y, 1 - c), device_id_type=MESH,
                    )

                sends.append(copy(mine))
                recvs.append(copy(theirs))
        for cp in sends:
            cp.start()
        for cp in recvs:
            cp.wait_recv()
        for cp in sends:
            cp.wait_send()

    return pl.pallas_call(
        body,
        name=name,
        out_shape=tuple(jax.ShapeDtypeStruct(a.shape, a.dtype) for a in lands),
        in_specs=[pl.BlockSpec(memory_space=pl.ANY)] * nt,
        out_specs=[pl.BlockSpec(memory_space=pl.ANY)] * nt,
        input_output_aliases={t: t for t in range(nt)},
        scratch_shapes=[pltpu.SemaphoreType.DMA((3 * nt,)), pltpu.SemaphoreType.DMA((3 * nt,))],
    )(*lands)


def _exchange_start(arrs, scatter, name, halves=False):
    nt = len(arrs)
    lands = [lax.empty(a.shape if scatter else (N_CHIP, *a.shape), a.dtype) for a in arrs]

    def body(*refs):
        ins, zones = refs[:nt], refs[nt : 2 * nt]
        send_sems, recv_sems, token = refs[2 * nt], refs[2 * nt + 1], refs[-1]
        sends, _ = _exchange_copies(ins, zones, send_sems, recv_sems, scatter, halves)
        for cp in sends:
            cp.start()
        token[...] = jnp.zeros(token.shape, F32)

    bufs = list(arrs) + list(lands)
    return pl.pallas_call(
        body,
        name=name,
        out_shape=(
            pltpu.SemaphoreType.DMA((3 * nt,)),
            pltpu.SemaphoreType.DMA((3 * nt,)),
            *[pltpu.HBM(a.shape, a.dtype) for a in bufs],
            jax.ShapeDtypeStruct((SUBLANES, LANES), F32),
        ),
        in_specs=[HBM_SPEC] * (2 * nt),
        out_specs=(SEM_SPEC, SEM_SPEC, *[HBM_SPEC] * (2 * nt), pl.BlockSpec(memory_space=pltpu.VMEM)),
        input_output_aliases={k: 2 + k for k in range(2 * nt)},
        compiler_params=pltpu.CompilerParams(has_side_effects=DATAFLOW),
    )(*[pltpu.with_memory_space_constraint(a, pltpu.HBM) for a in bufs])


def _exchange_wait(state, after, scatter, name, halves=False):
    send_sems, recv_sems, *bufs = state[:-1]
    nt = len(bufs) // 2
    afters = list(after) if isinstance(after, (list, tuple)) else [after]

    def body(*refs):
        ins, zones = refs[:nt], refs[nt : 2 * nt]
        sends, recvs = _exchange_copies(ins, zones, refs[2 * nt], refs[2 * nt + 1], scatter, halves)
        for cp in sends:
            cp.wait_send()
        for cp in recvs:
            cp.wait_recv()

    out = pl.pallas_call(
        body,
        name=name,
        out_shape=tuple(pltpu.HBM(a.shape, a.dtype) for a in bufs),
        in_specs=[HBM_SPEC] * (2 * nt) + [SEM_SPEC, SEM_SPEC] + [pl.BlockSpec(memory_space=pl.ANY)] * len(afters),
        out_specs=[HBM_SPEC] * (2 * nt),
        input_output_aliases={k: k for k in range(2 * nt)},
        compiler_params=pltpu.CompilerParams(has_side_effects=DATAFLOW),
    )(*bufs, send_sems, recv_sems, *afters)
    return list(out[:nt]), list(out[nt:])


def _swap_copies(ins, lands, send_sems, recv_sems):
    x, y, c = _mesh_pos()
    return [
        pltpu.make_async_remote_copy(
            src_ref=ins[t], dst_ref=lands[t], send_sem=send_sems.at[t], recv_sem=recv_sems.at[t],
            device_id=(x, y, 1 - c), device_id_type=MESH,
        )
        for t in range(len(ins))
    ]


def _swap_start(arrs, name):
    nt = len(arrs)
    lands = [lax.empty(a.shape, a.dtype) for a in arrs]

    def body(*refs):
        ins, zones = refs[:nt], refs[nt : 2 * nt]
        send_sems, recv_sems, token = refs[2 * nt], refs[2 * nt + 1], refs[-1]
        for cp in _swap_copies(ins, zones, send_sems, recv_sems):
            cp.start()
        token[...] = jnp.zeros(token.shape, F32)

    bufs = list(arrs) + lands
    return pl.pallas_call(
        body,
        name=name,
        out_shape=(
            pltpu.SemaphoreType.DMA((nt,)),
            pltpu.SemaphoreType.DMA((nt,)),
            *[pltpu.HBM(a.shape, a.dtype) for a in bufs],
            jax.ShapeDtypeStruct((SUBLANES, LANES), F32),
        ),
        in_specs=[HBM_SPEC] * (2 * nt),
        out_specs=(SEM_SPEC, SEM_SPEC, *[HBM_SPEC] * (2 * nt), pl.BlockSpec(memory_space=pltpu.VMEM)),
        input_output_aliases={k: 2 + k for k in range(2 * nt)},
        compiler_params=pltpu.CompilerParams(has_side_effects=DATAFLOW),
    )(*[pltpu.with_memory_space_constraint(a, pltpu.HBM) for a in bufs])


def _swap_wait(state, after, name):
    send_sems, recv_sems, *bufs = state[:-1]
    nt = len(bufs) // 2

    def body(*refs):
        cps = _swap_copies(refs[:nt], refs[nt : 2 * nt], refs[2 * nt], refs[2 * nt + 1])
        for cp in cps:
            cp.wait_send()
        for cp in cps:
            cp.wait_recv()

    out = pl.pallas_call(
        body,
        name=name,
        out_shape=tuple(pltpu.HBM(a.shape, a.dtype) for a in bufs),
        in_specs=[HBM_SPEC] * (2 * nt) + [SEM_SPEC, SEM_SPEC, pl.BlockSpec(memory_space=pl.ANY)],
        out_specs=[HBM_SPEC] * (2 * nt),
        input_output_aliases={k: k for k in range(2 * nt)},
        compiler_params=pltpu.CompilerParams(has_side_effects=DATAFLOW),
    )(*bufs, send_sems, recv_sems, after)
    return list(out[:nt]), list(out[nt:])


def _cols_from_shards(g):
    _, k, n = g.shape
    return jnp.transpose(g, (1, 0, 2)).reshape(k, N_CHIP * n)


def _cols_to_shards(a):
    k, n4 = a.shape
    return jnp.transpose(a.reshape(k, N_CHIP, n4 // N_CHIP), (1, 0, 2))


def _pad_to(vec, mult):
    n = vec.shape[0]
    return jnp.pad(vec, (0, (-n) % mult))


def kernel(x, c, positions, w_ada, b_ada, g_pre_mix, g_post_mix, w_in, g_q, w_uq, g_kv, w_ukv, conv_w_mix, conv_b_mix, w_o, g_pre_ffn, g_post_ffn, w_up, conv_w_ffn, conv_b_ffn, w_down, loss_target, m_w_ada, m_b_ada, m_g_pre_mix, m_g_post_mix, m_w_in, m_g_q, m_w_uq, m_g_kv, m_w_ukv, m_conv_w_mix, m_conv_b_mix, m_w_o, m_g_pre_ffn, m_g_post_ffn, m_w_up, m_conv_w_ffn, m_conv_b_ffn, m_w_down, v_w_ada, v_b_ada, v_g_pre_mix, v_g_post_mix, v_w_in, v_g_q, v_w_uq, v_g_kv, v_w_ukv, v_conv_w_mix, v_conv_b_mix, v_w_o, v_g_pre_ffn, v_g_post_ffn, v_w_up, v_conv_w_ffn, v_conv_b_ffn, v_w_down):
    xi, yi, ci = _mesh_pos()
    chip = 2 * xi + yi
    dev = 4 * xi + 2 * yi + ci

    s, d = x.shape[1], x.shape[2]
    ql, kl = g_q.shape[1], g_kv.shape[1]
    cwid = conv_b_mix.shape[1]
    f2 = conv_b_ffn.shape[1]
    hh = (w_uq.shape[2] * N_CHIP) // (NOPE + ROPE)
    w_att = hh * LANES
    nc_ada = w_ada.shape[2]
    lat = ql + kl + ROPE
    tc_mix = _gate_tile(cwid)
    lb = -(-(ql + kl + HEAD_PAD) // (3 * tc_mix)) * (3 * tc_mix)
    np_cols = lb + 3 * cwid
    assert cwid == hh * VDIM and w_att % tc_mix == 0

    x0 = x.reshape(s, d)
    tgt = loss_target.reshape(s, d)

    anchors = []

    def _behind(val, state):
        val, tok = lax.optimization_barrier((val, state[-1]))
        anchors.append(tok[0, 0])
        return val

    cwm_n, cwf_n = CONV_K * cwid // N_CHIP, CONV_K * f2 // N_CHIP
    pack_a = _pad_to(jnp.concatenate([c.reshape(-1), conv_w_mix.reshape(-1), conv_w_ffn.reshape(-1)]), SUBLANES * LANES)
    rows_a = _all_gather8(pack_a.reshape(SUBLANES, -1), "ag8_inputs").reshape(N_DEV, -1)
    c_all = rows_a[:, :d]
    south = rows_a[0::2]
    cw_mix = jnp.concatenate([south[j, d : d + cwm_n].reshape(CONV_K, -1) for j in range(N_CHIP)], axis=1)
    cw_ffn = jnp.concatenate([south[j, d + cwm_n : d + cwm_n + cwf_n].reshape(CONV_K, -1) for j in range(N_CHIP)], axis=1)

    b_cols = lax.dynamic_slice(b_ada, (0, chip * nc_ada), (1, nc_ada))
    mod_part, c_act = _ada_fwd(c_all, w_ada[0], b_cols)
    mod_rows = _all_gather8(mod_part, "ag8_mod")
    mod = jnp.concatenate(
        [lax.dynamic_slice_in_dim(mod_rows, 2 * N_DEV * j + dev, 1, axis=0) for j in range(N_CHIP)], axis=1
    )

    shards = [a[0].astype(BF16) for a in (w_in, w_uq, w_ukv, w_o, w_up, w_down)]
    first, mod = lax.optimization_barrier((shards[:3], mod))
    ag_a = _exchange_start(first, False, "ag_a_start", halves=True)
    mod = _behind(mod, ag_a)
    sh_m, sc_m, gt_m, sh_f, sc_f, gt_f = [mod[:, k * d : (k + 1) * d] for k in range(N_MOD)]

    inv_freq = 1.0 / (ROPE_THETA ** (jnp.arange(0, ROPE, 2, dtype=F32) / ROPE))
    invf = jnp.concatenate([inv_freq, inv_freq, jnp.zeros((LANES - ROPE,), F32)]).reshape(1, LANES)
    tabs = _rope_tables(positions.astype(F32).reshape(s, 1), invf)
    h1 = _pre_fwd(x0, g_pre_mix, sc_m, sh_m)

    def with_own(landed, own):
        return [lax.dynamic_update_slice_in_dim(g, a[None], chip, axis=0) for g, a in zip(landed, own)]

    own_w, landed_w = _exchange_wait(ag_a, [h1, tabs[0]], False, "ag_a_wait", halves=True)
    landed_w = list(_sibling_fill(landed_w, "ag_a_fill"))
    rest, landed_w = lax.optimization_barrier((shards[3:], landed_w))
    ag_b = _exchange_start(rest, False, "ag_b_start")
    h1 = _behind(h1, ag_b)
    g_in, g_uq, g_ukv = with_own(landed_w, own_w)
    full_in = _cols_from_shards(g_in)
    gate_cols = [(lat + k * cwid + j * tc_mix, tc_mix) for j in range(cwid // tc_mix) for k in range(3)]
    w_in_p = jnp.concatenate(
        [full_in[:, :lat], jnp.zeros((d, lb - lat), BF16)] + [full_in[:, o : o + n] for o, n in gate_cols], axis=1
    )
    full_uq = _cols_from_shards(g_uq).reshape(ql, hh, NOPE + ROPE)
    w_uq_p = jnp.concatenate(
        [
            full_uq[:, :, :NOPE].reshape(ql, w_att),
            jnp.pad(full_uq[:, :, NOPE:], ((0, 0), (0, 0), (0, HEAD_PAD - ROPE))).reshape(ql, w_att),
        ],
        axis=1,
    )
    full_ukv = _cols_from_shards(g_ukv).reshape(kl, hh, NOPE + VDIM)
    w_ukv_p = jnp.concatenate([full_ukv[:, :, :NOPE].reshape(kl, w_att), full_ukv[:, :, NOPE:].reshape(kl, w_att)], axis=1)

    proj = _matmul(h1, w_in_p, out_dtype=F32, tm=1024, tn=768, tk=2048, name="mm_proj")
    qn, kvn, kr = _latent_fwd(proj, g_q, g_kv, tabs, lb)
    q_f = _matmul(qn, w_uq_p, out_dtype=F32, tm=1024, tn=1024, tk=2048, name="mm_q")
    kv_p = _matmul(kvn, w_ukv_p, out_dtype=BF16, tm=1024, tn=1024, tk=2048, name="mm_kv")
    q_c, k_c = _head_cat(q_f, kv_p, kr, tabs, hh)
    cat, lse2 = _attn_fwd(q_c, k_c, kv_p, hh, w_att + cwid)
    cat = _mixer_fwd(cat, proj, cw_mix, conv_b_mix, lb, w_att)
    own_w, landed_w = _exchange_wait(ag_b, cat, False, "ag_b_wait")
    g_o, g_up, g_down = with_own(landed_w, own_w)
    w_o_f = g_o.reshape(-1, d)
    cw_ffn_p, cb_ffn_p = _pair_cols(cw_ffn), _pair_cols(conv_b_ffn)
    tcp, pair_perm = _pair_tile(f2 // 2), _pair_perm(f2 // 2)
    w_down_f = g_down.reshape(-1, d)
    mix = _matmul(cat, w_o_f, out_dtype=F32, tm=1024, tn=1024, tk=2048, name="mm_mix")

    x1, h2 = _mid_fwd(x0, mix, g_post_mix, gt_m, g_pre_ffn, sc_f, sh_f)
    up = _matmul(h2, g_up, out_dtype=F32, tm=1024, tn=tcp, tk=2048, name="mm_up", b_n_perm=pair_perm, b_col_shards=True)
    act = _ffn_act_fwd(up, cw_ffn_p, cb_ffn_p)
    y = _matmul(act, w_down_f, out_dtype=F32, tm=512, tn=1024, tk=5632, name="mm_down")
    dx2, dy, s_fin = _final(x1, y, tgt, g_post_ffn, gt_f)

    dw_down = _matmul(act, dy, ta=True, out_dtype=BF16, tm=512, tn=2048, tk=4096, name="mm_dw_down")
    dact = _matmul(dy, w_down_f, tb=True, out_dtype=F32, tm=1024, tn=1408, tk=2048, name="mm_dact")
    dup, s_ffn_p = _ffn_act_bwd(dact, up, cw_ffn_p, cb_ffn_p)
    s_ffn = _unpair_cols(s_ffn_p)
    dw_up = _matmul(
        h2, dup, ta=True, out_dtype=BF16, tm=512, tn=tcp, tk=4096, name="mm_dw_up", out_n_perm=pair_perm, out_col_shards=True
    )
    dh2 = _matmul_pair_k(dup, g_up, out_dtype=F32, tm=1024, tn=1024, name="mm_dh2")
    dx1, dmix, s_mid = _mid_bwd(dh2, dx2, x1, mix, g_pre_ffn, sc_f, g_post_mix, gt_m)

    dw_o = _matmul(cat, dmix, ta=True, out_dtype=BF16, tm=512, tn=1024, tk=4096, name="mm_dw_o")
    send_b = [dw_o.reshape(N_CHIP, -1, d), dw_up, dw_down.reshape(N_CHIP, -1, d)]
    rs_b = _exchange_start(send_b, True, "rs_b_start")
    dmix = _behind(dmix, rs_b)
    dcat = _matmul(dmix, w_o_f, tb=True, out_dtype=F32, tm=1024, tn=1024, tk=2048, name="mm_dcat")
    dproj, s_mix = _mixer_bwd(dcat, proj, cw_mix, conv_b_mix, lb, w_att)
    dob, stats = _attn_bwd_prep(cat, dcat, lse2, hh)
    dq_raw, dkv_k, dkv_v, dkr_h = _attn_bwd(q_c, k_c, kv_p, dob, stats, hh)
    dkv_p = jnp.concatenate([dkv_k, dkv_v], axis=1)
    dq_p = _dq_unrope(dq_raw, tabs, hh)
    dw_uq_p = _matmul(qn, dq_p, ta=True, out_dtype=BF16, tm=1024, tn=1024, tk=1024, name="mm_dw_uq")
    dqn = _matmul(dq_p, w_uq_p, tb=True, out_dtype=F32, tm=1024, tn=1024, tk=2048, name="mm_dqn")
    dw_ukv_p = _matmul(kvn, dkv_p, ta=True, out_dtype=BF16, tm=1024, tn=1024, tk=1024, name="mm_dw_ukv")
    dkvn = _matmul(dkv_p, w_ukv_p, tb=True, out_dtype=F32, tm=1024, tn=1024, tk=2048, name="mm_dkvn")
    dproj, s_lat = _latent_bwd(dproj, proj, dqn, dkvn, dkr_h, g_q, g_kv, tabs, lb)
    dw_in_p = _matmul(h1, dproj, ta=True, out_dtype=BF16, tm=512, tn=1536, tk=4096, name="mm_dw_in")

    n_trip = cwid // tc_mix
    ungate = [lb + (3 * j + k) * tc_mix for k in range(3) for j in range(n_trip)]
    dw_in_f = jnp.concatenate([dw_in_p[:, :lat]] + [dw_in_p[:, o : o + tc_mix] for o in ungate], axis=1)
    uq3 = dw_uq_p.reshape(ql, 2, hh, LANES)
    dw_uq_f = jnp.concatenate([uq3[:, 0], uq3[:, 1, :, :ROPE]], axis=2).reshape(ql, hh * (NOPE + ROPE))
    ukv3 = dw_ukv_p.reshape(kl, 2, hh, LANES)
    dw_ukv_f = jnp.concatenate([ukv3[:, 0], ukv3[:, 1]], axis=2).reshape(kl, hh * (NOPE + VDIM))
    send_a = [_cols_to_shards(dw_in_f), _cols_to_shards(dw_uq_f), _cols_to_shards(dw_ukv_f)]
    rs_a = _exchange_start(send_a, True, "rs_a_start")
    dproj = _behind(dproj, rs_a)

    dh1 = _matmul(dproj, w_in_p, tb=True, out_dtype=F32, tm=512, tn=1024, tk=4608, name="mm_dh1")
    grad_x, s_first = _first_bwd(dh1, dx1, x0, g_pre_mix, sc_m)

    names = ["w_in", "w_uq", "w_ukv", "w_o", "w_up", "w_down"]
    sent_b, landed_b = _exchange_wait(rs_b, s_first, True, "rs_b_wait")
    sent_a, landed_a = _exchange_wait(rs_a, landed_b[0], True, "rs_a_wait")
    landed_a, s_first = lax.optimization_barrier((landed_a, s_first))
    part = [_sum_chips(l, a, "sum_chips_" + n) for l, a, n in zip(landed_a + landed_b, sent_a + sent_b, names)]

    dmod = jnp.concatenate([s_first[0:1], s_first[1:2], s_mid[3:4], s_mid[0:1], s_mid[1:2], s_fin[0:1]], axis=1)
    small = [
        dmod,
        s_first[2:3],
        s_mid[4:5],
        s_lat[0:1, :ql],
        s_lat[0:1, ql : ql + kl],
        s_mix[3:4],
        s_mid[2:3],
        s_fin[1:2],
        s_ffn[3:4],
        s_mix[0:3].reshape(1, -1),
        s_ffn[0:3].reshape(1, -1),
        s_fin[3:4, :LANES],
    ]
    sizes = [a.shape[1] for a in small]
    offs = [0]
    for n in sizes:
        offs.append(offs[-1] + n)
    pack_g = _pad_to(jnp.concatenate(small, axis=1).reshape(-1), SUBLANES * LANES * SUBLANES).reshape(SUBLANES, -1)
    gathered = _all_gather8(pack_g, "ag8_small_grads")
    tot = _sum_devices(gathered).reshape(-1)
    part_of = lambda k: tot[offs[k] : offs[k + 1]]
    dmod_all = gathered.reshape(N_DEV, -1)[:, : N_MOD * d]
    loss = part_of(11)[0]

    g_b_ada = part_of(0).reshape(1, -1)
    g_vecs = [part_of(k).reshape(1, -1) for k in range(1, 9)]
    g_cw_mix = lax.dynamic_slice(part_of(9).reshape(CONV_K, cwid), (0, chip * (cwid // N_CHIP)), (CONV_K, cwid // N_CHIP))
    g_cw_ffn = lax.dynamic_slice(part_of(10).reshape(CONV_K, f2), (0, chip * (f2 // N_CHIP)), (CONV_K, f2 // N_CHIP))

    swap = _swap_start(part, "swap_start")
    dm_cols = _behind(lax.dynamic_slice(dmod_all, (0, chip * nc_ada), (N_DEV, nc_ada)), swap)
    g_w_ada = _ada_grad(
        jnp.pad(c_act.T, ((0, 0), (0, LANES - N_DEV))), jnp.pad(dm_cols, ((0, LANES - N_DEV), (0, 0)))
    )
    big = {"w_ada": [a[None] for a in _adamw(w_ada[0], m_w_ada[0], v_w_ada[0], [g_w_ada], "adamw_w_ada")]}
    part, other = _swap_wait(swap, big["w_ada"][1], "swap_wait")

    big_w = [w_in, w_uq, w_ukv, w_o, w_up, w_down]
    big_m = [m_w_in, m_w_uq, m_w_ukv, m_w_o, m_w_up, m_w_down]
    big_v = [v_w_in, v_w_uq, v_w_ukv, v_w_o, v_w_up, v_w_down]
    for n, w_, m_, v_, p_, o_ in zip(names, big_w, big_m, big_v, part, other):
        big[n] = [a[None] for a in _adamw(w_[0], m_[0], v_[0], [p_, o_], "adamw_" + n)]

    sm_names = ["b_ada", "g_pre_mix", "g_post_mix", "g_q", "g_kv", "conv_b_mix", "g_pre_ffn", "g_post_ffn", "conv_b_ffn",
                "conv_w_mix", "conv_w_ffn"]
    sm_w = [b_ada, g_pre_mix, g_post_mix, g_q, g_kv, conv_b_mix, g_pre_ffn, g_post_ffn, conv_b_ffn, conv_w_mix, conv_w_ffn]
    sm_m = [m_b_ada, m_g_pre_mix, m_g_post_mix, m_g_q, m_g_kv, m_conv_b_mix, m_g_pre_ffn, m_g_post_ffn, m_conv_b_ffn,
            m_conv_w_mix, m_conv_w_ffn]
    sm_v = [v_b_ada, v_g_pre_mix, v_g_post_mix, v_g_q, v_g_kv, v_conv_b_mix, v_g_pre_ffn, v_g_post_ffn, v_conv_b_ffn,
            v_conv_w_mix, v_conv_w_ffn]
    sm_g = [g_b_ada] + g_vecs + [g_cw_mix, g_cw_ffn]
    flat = lambda arrs: jnp.concatenate([a.reshape(1, -1) for a in arrs], axis=1)
    sm_out = _adamw(flat(sm_w), flat(sm_m), flat(sm_v), [flat(sm_g)], "adamw_small")
    sm = {}
    off = 0
    for n, w_ in zip(sm_names, sm_w):
        sm[n] = [o[:, off : off + w_.size].reshape(w_.shape) for o in sm_out]
        off += w_.size

    order = ["w_ada", "b_ada", "g_pre_mix", "g_post_mix", "w_in", "g_q", "w_uq", "g_kv", "w_ukv", "conv_w_mix", "conv_b_mix",
             "w_o", "g_pre_ffn", "g_post_ffn", "w_up", "conv_w_ffn", "conv_b_ffn", "w_down"]
    res = {**big, **sm}
    outs = [loss + sum(anchors), grad_x.reshape(x.shape)]
    for k in range(4):
        outs += [res[n][k] for n in order]
    return tuple(outs)
```

```python
import math

import jax
import jax.numpy as jnp
from jax import lax
from jax.experimental import pallas as pl
from jax.experimental.pallas import tpu as pltpu

F32 = jnp.float32
BF16 = jnp.bfloat16
MESH = pl.DeviceIdType.MESH

N_DEV = 8
N_CHIP = 4
LANES = 128
SUBLANES = 8
VMEM_LIMIT = 56 * 2**20

NOPE = 128
ROPE = 64
VDIM = 128
HEAD_PAD = 128
ROPE_THETA = 10000.0
RMS_EPS = 1e-6
N_MOD = 6
CONV_K = 3
ATT_FWD_BLOCK, ATT_FWD_SUB = 2048, 256
ATT_BWD_BLOCK, ATT_BWD_SUB = 1024, 256
NEG = -1e30

ADAM_LR = 0.001
ADAM_B1 = 0.9
ADAM_B2 = 0.999
ADAM_EPS = 1e-08
ADAM_WD = 0.01
ADAM_STEP = 10


def _tile(n, pref, align):
    if n <= pref:
        return n
    t = (pref // align) * align
    while t >= align:
        if n % t == 0:
            return t
        t -= align
    return n


def _cp(*sem):
    return pltpu.CompilerParams(dimension_semantics=sem, vmem_limit_bytes=VMEM_LIMIT)


def _rsq(x):
    return lax.rsqrt(jnp.mean(x * x, axis=-1, keepdims=True) + RMS_EPS)


def _norm_bwd(dn, n, r):
    return r * (dn - n * jnp.mean(dn * n, axis=-1, keepdims=True))


def _colsum(a):
    return jnp.sum(a, axis=0, keepdims=True)


def _matmul(a, b, *, ta=False, tb=False, out_dtype, tm, tn, tk, name, b_n_perm=None, out_n_perm=None,
            b_col_shards=False, out_col_shards=False):
    assert not (b_col_shards and tb)
    if b_col_shards:
        b_rows, b_cols = b.shape[1], N_CHIP * b.shape[2]
    else:
        b_rows, b_cols = b.shape
    (k_a, m) = a.shape if ta else a.shape[::-1]
    (n, k_b) = (b_rows, b_cols) if tb else (b_cols, b_rows)
    assert k_a == k_b, (a.shape, b.shape, ta, tb)
    tm, tn, tk = _tile(m, tm, LANES), _tile(n, tn, LANES), _tile(k_a, tk, LANES)
    nk = k_a // tk
    same = lambda t: t
    bn, on = b_n_perm or same, out_n_perm or same
    a_spec = pl.BlockSpec((tk, tm), lambda i, j, k: (k, i)) if ta else pl.BlockSpec((tm, tk), lambda i, j, k: (i, k))
    if b_col_shards:
        per = (b_cols // N_CHIP) // tn
        b_spec = pl.BlockSpec((None, tk, tn), lambda i, j, k: (bn(j) // per, k, bn(j) % per))
    elif tb:
        b_spec = pl.BlockSpec((tn, tk), lambda i, j, k: (bn(j), k))
    else:
        b_spec = pl.BlockSpec((tk, tn), lambda i, j, k: (k, bn(j)))
    if out_col_shards:
        per_o = (n // N_CHIP) // tn
        out_shape = jax.ShapeDtypeStruct((N_CHIP, m, n // N_CHIP), out_dtype)
        out_spec = pl.BlockSpec((None, tm, tn), lambda i, j, k: (on(j) // per_o, i, on(j) % per_o))
    else:
        out_shape = jax.ShapeDtypeStruct((m, n), out_dtype)
        out_spec = pl.BlockSpec((tm, tn), lambda i, j, k: (i, on(j)))
    dims = (((0 if ta else 1,), (1 if tb else 0,)), ((), ()))

    def body(a_ref, b_ref, o_ref, *acc):
        p = lax.dot_general(a_ref[...].astype(BF16), b_ref[...].astype(BF16), dims, preferred_element_type=F32)
        _accumulate(p, o_ref, acc, nk)

    return pl.pallas_call(
        body,
        name=name,
        out_shape=out_shape,
        grid=(m // tm, n // tn, nk),
        in_specs=[a_spec, b_spec],
        out_specs=out_spec,
        scratch_shapes=[] if nk == 1 else [pltpu.VMEM((tm, tn), F32)],
        compiler_params=_cp("parallel", "parallel", "arbitrary"),
    )(a, b)


def _accumulate(p, o_ref, acc, nk):
    if nk == 1:
        o_ref[...] = p.astype(o_ref.dtype)
        return
    k = pl.program_id(2)

    @pl.when(k == 0)
    def _():
        acc[0][...] = p

    @pl.when(k > 0)
    def _():
        acc[0][...] += p

    @pl.when(k == nk - 1)
    def _():
        o_ref[...] = acc[0][...].astype(o_ref.dtype)


def _matmul_pair_k(a, b_shards, *, out_dtype, tm, tn, pairs, name):
    m, f2 = a.shape
    n = b_shards.shape[1]
    tc = _pair_tile(f2 // 2)
    nj = (f2 // 2) // tc
    nk = nj // pairs
    per = (f2 // N_CHIP) // tc
    tm, tn = _tile(m, tm, LANES), _tile(n, tn, LANES)

    def body(a_ref, *refs):
        w_refs, (o_ref, *acc) = refs[: 2 * pairs], refs[2 * pairs :]
        av = a_ref[...]
        p = None
        for q in range(2 * pairs):
            part = lax.dot_general(av[:, q * tc : (q + 1) * tc], w_refs[q][...], NT, preferred_element_type=F32)
            p = part if p is None else p + part
        _accumulate(p, o_ref, acc, nk)

    def w_tile(first, q):
        return pl.BlockSpec((None, tn, tc), lambda i, j, k: ((first + pairs * k + q) // per, j, (first + pairs * k + q) % per))

    w_specs = [w_tile(first, q) for q in range(pairs) for first in (0, nj)]
    return pl.pallas_call(
        body,
        name=name,
        out_shape=jax.ShapeDtypeStruct((m, n), out_dtype),
        grid=(m // tm, n // tn, nk),
        in_specs=[pl.BlockSpec((tm, 2 * tc * pairs), lambda i, j, k: (i, k))] + w_specs,
        out_specs=pl.BlockSpec((tm, tn), lambda i, j, k: (i, j)),
        scratch_shapes=[] if nk == 1 else [pltpu.VMEM((tm, tn), F32)],
        compiler_params=_cp("parallel", "parallel", "arbitrary"),
    )(a, *[b_shards] * (2 * pairs))


def _rope_tables(pos_col, invf):
    s = pos_col.shape[0]
    ts = _tile(s, 1024, SUBLANES)
    half = ROPE // 2

    def body(p_ref, f_ref, c_ref, sa_ref, sb_ref):
        ang = p_ref[...] * f_ref[...]
        lane = lax.broadcasted_iota(jnp.int32, ang.shape, 1)
        cs, sn = jnp.cos(ang), jnp.sin(ang)
        c_ref[...] = jnp.where(lane < ROPE, cs, 0.0)
        sa_ref[...] = jnp.where((lane >= half) & (lane < ROPE), sn, 0.0)
        sb_ref[...] = jnp.where(lane < half, -sn, 0.0)

    tab = jax.ShapeDtypeStruct((s, LANES), F32)
    return pl.pallas_call(
        body,
        name="rope_tables",
        out_shape=(tab, tab, tab),
        grid=(s // ts,),
        in_specs=[pl.BlockSpec((ts, 1), lambda i: (i, 0)), pl.BlockSpec((1, LANES), lambda i: (0, 0))],
        out_specs=[pl.BlockSpec((ts, LANES), lambda i: (i, 0))] * 3,
        compiler_params=_cp("parallel"),
    )(pos_col, invf)


def _widen(t, w):
    return t if w == LANES else jnp.tile(t, (1, w // LANES))


def _rope(x, c, sa, sb):
    w = x.shape[1]
    c, sa, sb = _widen(c, w), _widen(sa, w), _widen(sb, w)
    return x * c + pltpu.roll(x, ROPE // 2, 1) * sa + pltpu.roll(x, w - ROPE // 2, 1) * sb


def _rope_t(d, c, sa, sb):
    w = d.shape[1]
    c, sa, sb = _widen(c, w), _widen(sa, w), _widen(sb, w)
    return d * c + pltpu.roll(d * sa, w - ROPE // 2, 1) + pltpu.roll(d * sb, ROPE // 2, 1)


def _ada_fwd(c_all, w, b):
    d, nc = w.shape
    tn = _tile(nc, 512, LANES)

    def body(c_ref, w_ref, b_ref, o_ref, ca_ref):
        cv = c_ref[...]
        ca = cv * jax.nn.sigmoid(cv)
        ca_ref[...] = ca
        o_ref[...] = jnp.dot(ca.astype(BF16), w_ref[...].astype(BF16), preferred_element_type=F32) + b_ref[...]

    return pl.pallas_call(
        body,
        name="ada_fwd",
        out_shape=(jax.ShapeDtypeStruct((N_DEV, nc), F32), jax.ShapeDtypeStruct((N_DEV, d), F32)),
        grid=(nc // tn,),
        in_specs=[
            pl.BlockSpec((N_DEV, d), lambda j: (0, 0)),
            pl.BlockSpec((d, tn), lambda j: (0, j)),
            pl.BlockSpec((1, tn), lambda j: (0, j)),
        ],
        out_specs=[pl.BlockSpec((N_DEV, tn), lambda j: (0, j)), pl.BlockSpec((N_DEV, d), lambda j: (0, 0))],
        compiler_params=_cp("arbitrary"),
    )(c_all, w, b)


def _rows(ts, d):
    return pl.BlockSpec((ts, d), lambda i: (i, 0))


def _vec(d):
    return pl.BlockSpec((1, d), lambda i: (0, 0))


def _sums(d):
    return pl.BlockSpec((SUBLANES, d), lambda i: (0, 0))


def _acc_rows(ref, i, rows):
    @pl.when(i == 0)
    def _():
        ref[...] = jnp.zeros(ref.shape, ref.dtype)

    for k, r in enumerate(rows):
        ref[k : k + 1, :] += r


def _pre_fwd(x, g, sc, sh):
    s, d = x.shape
    ts = _tile(s, 512, SUBLANES)

    def body(x_ref, g_ref, sc_ref, sh_ref, h_ref):
        xv = x_ref[...]
        h_ref[...] = (((xv * _rsq(xv)) * g_ref[...]) * (1.0 + sc_ref[...]) + sh_ref[...]).astype(BF16)

    return pl.pallas_call(
        body,
        name="pre_mix_fwd",
        out_shape=jax.ShapeDtypeStruct((s, d), BF16),
        grid=(s // ts,),
        in_specs=[_rows(ts, d), _vec(d), _vec(d), _vec(d)],
        out_specs=_rows(ts, d),
        compiler_params=_cp("parallel"),
    )(x, g, sc, sh)


def _mid_fwd(x0, mix, g_post, gt, g_pre, sc, sh):
    s, d = x0.shape
    ts = _tile(s, 256, SUBLANES)

    def body(x_ref, m_ref, gp_ref, gt_ref, g_ref, sc_ref, sh_ref, x1_ref, h_ref):
        mv = m_ref[...]
        x1 = x_ref[...] + gt_ref[...] * ((mv * _rsq(mv)) * gp_ref[...])
        x1_ref[...] = x1
        h_ref[...] = (((x1 * _rsq(x1)) * g_ref[...]) * (1.0 + sc_ref[...]) + sh_ref[...]).astype(BF16)

    return pl.pallas_call(
        body,
        name="mid_fwd",
        out_shape=(jax.ShapeDtypeStruct((s, d), F32), jax.ShapeDtypeStruct((s, d), BF16)),
        grid=(s // ts,),
        in_specs=[_rows(ts, d), _rows(ts, d)] + [_vec(d)] * 5,
        out_specs=[_rows(ts, d), _rows(ts, d)],
        compiler_params=_cp("parallel"),
    )(x0, mix, g_post, gt, g_pre, sc, sh)


def _final(x1, y, tgt, g_post, gt):
    s, d = x1.shape
    ts = _tile(s, 256, SUBLANES)
    ni = s // ts

    def body(x_ref, y_ref, t_ref, gp_ref, gt_ref, dx_ref, dy_ref, s_ref):
        i = pl.program_id(0)
        yv, gp, gt_v = y_ref[...], gp_ref[...], gt_ref[...]
        r = _rsq(yv)
        n = yv * r
        err = (x_ref[...] + gt_v * (n * gp)) - t_ref[...]
        dx = err * (1.0 / d)
        dx_ref[...] = dx
        dy_ref[...] = _norm_bwd(dx * (gt_v * gp), n, r).astype(BF16)
        _acc_rows(s_ref, i, [_colsum(dx * (n * gp)), _colsum(dx * gt_v * n), _colsum(err * err)])

        @pl.when(i == ni - 1)
        def _():
            tot = jnp.sum(s_ref[2:3, :], axis=1, keepdims=True) * (0.5 / d)
            s_ref[3:4, :] = jnp.broadcast_to(tot, (1, d))

    return pl.pallas_call(
        body,
        name="final_fwd_bwd",
        out_shape=(
            jax.ShapeDtypeStruct((s, d), F32),
            jax.ShapeDtypeStruct((s, d), BF16),
            jax.ShapeDtypeStruct((SUBLANES, d), F32),
        ),
        grid=(ni,),
        in_specs=[_rows(ts, d)] * 3 + [_vec(d)] * 2,
        out_specs=[_rows(ts, d), _rows(ts, d), _sums(d)],
        compiler_params=_cp("arbitrary"),
    )(x1, y, tgt, g_post, gt)


def _mid_bwd(dh2, dx2, x1, mix, g_pre, sc, g_post, gt):
    s, d = x1.shape
    ts = _tile(s, 256, SUBLANES)

    def body(dh_ref, dx2_ref, x_ref, m_ref, g_ref, sc_ref, gp_ref, gt_ref, dx1_ref, dm_ref, s_ref):
        i = pl.program_id(0)
        dh, xv, mv = dh_ref[...], x_ref[...], m_ref[...]
        g, sc_v, gp, gt_v = g_ref[...], sc_ref[...], gp_ref[...], gt_ref[...]
        r1 = _rsq(xv)
        n1 = xv * r1
        dx1 = dx2_ref[...] + _norm_bwd(dh * (g * (1.0 + sc_v)), n1, r1)
        dx1_ref[...] = dx1
        rm = _rsq(mv)
        nm = mv * rm
        dm_ref[...] = _norm_bwd(dx1 * (gt_v * gp), nm, rm).astype(BF16)
        _acc_rows(
            s_ref,
            i,
            [
                _colsum(dh),
                _colsum(dh * (n1 * g)),
                _colsum(dh * (1.0 + sc_v) * n1),
                _colsum(dx1 * (nm * gp)),
                _colsum(dx1 * gt_v * nm),
            ],
        )

    return pl.pallas_call(
        body,
        name="mid_bwd",
        out_shape=(
            jax.ShapeDtypeStruct((s, d), F32),
            jax.ShapeDtypeStruct((s, d), BF16),
            jax.ShapeDtypeStruct((SUBLANES, d), F32),
        ),
        grid=(s // ts,),
        in_specs=[_rows(ts, d)] * 4 + [_vec(d)] * 4,
        out_specs=[_rows(ts, d), _rows(ts, d), _sums(d)],
        compiler_params=_cp("arbitrary"),
    )(dh2, dx2, x1, mix, g_pre, sc, g_post, gt)


def _first_bwd(dh1, dx1, x0, g, sc):
    s, d = x0.shape
    ts = _tile(s, 256, SUBLANES)

    def body(dh_ref, dx1_ref, x_ref, g_ref, sc_ref, dx_ref, s_ref):
        i = pl.program_id(0)
        dh, xv, gv, sc_v = dh_ref[...], x_ref[...], g_ref[...], sc_ref[...]
        r = _rsq(xv)
        n = xv * r
        dx_ref[...] = dx1_ref[...] + _norm_bwd(dh * (gv * (1.0 + sc_v)), n, r)
        _acc_rows(s_ref, i, [_colsum(dh), _colsum(dh * (n * gv)), _colsum(dh * (1.0 + sc_v) * n)])

    return pl.pallas_call(
        body,
        name="first_bwd",
        out_shape=(jax.ShapeDtypeStruct((s, d), F32), jax.ShapeDtypeStruct((SUBLANES, d), F32)),
        grid=(s // ts,),
        in_specs=[_rows(ts, d)] * 3 + [_vec(d)] * 2,
        out_specs=[_rows(ts, d), _sums(d)],
        compiler_params=_cp("arbitrary"),
    )(dh1, dx1, x0, g, sc)


def _latent_fwd(proj, g_q, g_kv, tabs, lb):
    s = proj.shape[0]
    ql, kl = g_q.shape[1], g_kv.shape[1]
    ts = _tile(s, 512, SUBLANES)

    def body(p_ref, gq_ref, gk_ref, c_ref, sa_ref, sb_ref, q_ref, kv_ref, kr_ref):
        pv = p_ref[...]
        q, kv, kr = pv[:, :ql], pv[:, ql : ql + kl], pv[:, ql + kl : ql + kl + HEAD_PAD]
        q_ref[...] = ((q * _rsq(q)) * gq_ref[...]).astype(BF16)
        kv_ref[...] = ((kv * _rsq(kv)) * gk_ref[...]).astype(BF16)
        kr_ref[...] = _rope(kr, c_ref[...], sa_ref[...], sb_ref[...]).astype(BF16)

    return pl.pallas_call(
        body,
        name="latent_fwd",
        out_shape=(
            jax.ShapeDtypeStruct((s, ql), BF16),
            jax.ShapeDtypeStruct((s, kl), BF16),
            jax.ShapeDtypeStruct((s, HEAD_PAD), BF16),
        ),
        grid=(s // ts,),
        in_specs=[_rows(ts, lb), _vec(ql), _vec(kl)] + [_rows(ts, LANES)] * 3,
        out_specs=[_rows(ts, ql), _rows(ts, kl), _rows(ts, HEAD_PAD)],
        compiler_params=_cp("parallel"),
    )(proj, g_q, g_kv, *tabs)


def _latent_bwd(dproj, proj, dqn, dkvn, dkr_h, g_q, g_kv, tabs, lb):
    s = proj.shape[0]
    ql, kl = g_q.shape[1], g_kv.shape[1]
    hw = dkr_h.shape[1]
    ts = _tile(s, 256, SUBLANES)
    pad = lb - ql - kl - HEAD_PAD

    def body(_, p_ref, dq_ref, dkv_ref, dkr_ref, gq_ref, gk_ref, c_ref, sa_ref, sb_ref, o_ref, s_ref):
        i = pl.program_id(0)
        pv = p_ref[...]
        q, kv = pv[:, :ql], pv[:, ql : ql + kl]
        dqn_v, dkvn_v = dq_ref[...], dkv_ref[...]
        rq = _rsq(q)
        nq = q * rq
        rk = _rsq(kv)
        nk = kv * rk
        dkr = dkr_ref[:, :HEAD_PAD]
        for h in range(1, hw // HEAD_PAD):
            dkr = dkr + dkr_ref[:, h * HEAD_PAD : (h + 1) * HEAD_PAD]
        parts = [
            _norm_bwd(dqn_v * gq_ref[...], nq, rq).astype(BF16),
            _norm_bwd(dkvn_v * gk_ref[...], nk, rk).astype(BF16),
            _rope_t(dkr, c_ref[...], sa_ref[...], sb_ref[...]).astype(BF16),
        ]
        if pad:
            parts.append(jnp.zeros((ts, pad), BF16))
        o_ref[...] = jnp.concatenate(parts, axis=1)
        row = [_colsum(dqn_v * nq), _colsum(dkvn_v * nk), jnp.zeros((1, lb - ql - kl), F32)]
        _acc_rows(s_ref, i, [jnp.concatenate(row, axis=1)])

    return pl.pallas_call(
        body,
        name="latent_bwd",
        out_shape=(jax.ShapeDtypeStruct(dproj.shape, BF16), jax.ShapeDtypeStruct((SUBLANES, lb), F32)),
        grid=(s // ts,),
        in_specs=[pl.BlockSpec(memory_space=pl.ANY), _rows(ts, lb), _rows(ts, ql), _rows(ts, kl), _rows(ts, hw)]
        + [_vec(ql), _vec(kl)]
        + [_rows(ts, LANES)] * 3,
        out_specs=[_rows(ts, lb), _sums(lb)],
        input_output_aliases={0: 0},
        compiler_params=_cp("arbitrary"),
    )(dproj, proj, dqn, dkvn, dkr_h, g_q, g_kv, *tabs)


def _conv3(ext, w, b):
    return (pltpu.roll(ext, 2, 0) * w[0:1] + pltpu.roll(ext, 1, 0) * w[1:2]) + ext * w[2:3] + b


def _conv3_t(du, w):
    n = du.shape[0]
    return du * w[2:3] + pltpu.roll(du, n - 1, 0) * w[1:2] + pltpu.roll(du, n - 2, 0) * w[0:1]


def _halo_maps(ts, s):
    r8, last = ts // SUBLANES, s // SUBLANES - 1
    prev = lambda i: jnp.maximum(i * r8 - 1, 0)
    nxt = lambda i: jnp.minimum((i + 1) * r8, last)
    return prev, nxt


def _gate_tile(cwid):
    return _tile(cwid, 512, LANES)


def _mixer_fwd(cat, proj, cw, cb, lb, col0):
    s = proj.shape[0]
    cwid = cw.shape[1]
    ts = _tile(s, 512, SUBLANES)
    tc = _gate_tile(cwid)
    assert lb % (3 * tc) == 0 and col0 % tc == 0
    t0, oc = lb // (3 * tc), col0 // tc
    prev, _ = _halo_maps(ts, s)

    def body(_, g_ref, p_ref, w_ref, b_ref, o_ref):
        keep = jnp.where(pl.program_id(1) > 0, 1.0, 0.0)
        gv, pv = g_ref[...], p_ref[...]
        ext = jnp.concatenate([pv[:, tc : 2 * tc] * pv[:, 2 * tc :] * keep, gv[:, tc : 2 * tc] * gv[:, 2 * tc :]], axis=0)
        o_ref[...] = (gv[:, :tc] * _conv3(ext, w_ref[...], b_ref[...])[SUBLANES:]).astype(BF16)

    return pl.pallas_call(
        body,
        name="mixer_fwd",
        out_shape=jax.ShapeDtypeStruct(cat.shape, BF16),
        grid=(cwid // tc, s // ts),
        in_specs=[
            pl.BlockSpec(memory_space=pl.ANY),
            pl.BlockSpec((ts, 3 * tc), lambda j, i: (i, t0 + j)),
            pl.BlockSpec((SUBLANES, 3 * tc), lambda j, i: (prev(i), t0 + j)),
            pl.BlockSpec((CONV_K, tc), lambda j, i: (0, j)),
            pl.BlockSpec((1, tc), lambda j, i: (0, j)),
        ],
        out_specs=pl.BlockSpec((ts, tc), lambda j, i: (i, oc + j)),
        input_output_aliases={0: 0},
        compiler_params=_cp("parallel", "arbitrary"),
    )(cat, proj, proj, cw, cb)


def _mixer_bwd(dcat, proj, cw, cb, lb, col0):
    s, np_cols = proj.shape
    cwid = cw.shape[1]
    ts = _tile(s, 256, SUBLANES)
    tc = _gate_tile(cwid)
    t0, oc = lb // (3 * tc), col0 // tc
    ni = s // ts
    prev, nxt = _halo_maps(ts, s)

    def body(d_ref, dn_ref, g_ref, gp_ref, gn_ref, w_ref, b_ref, dg_ref, s_ref):
        i = pl.program_id(1)
        keep_p = jnp.where(i > 0, 1.0, 0.0)
        keep_n = jnp.where(i < ni - 1, 1.0, 0.0)
        w = w_ref[...]
        gv, gp, gn = g_ref[...], gp_ref[...], gn_ref[...]
        gc = jnp.concatenate([gp[:, tc : 2 * tc], gv[:, tc : 2 * tc], gn[:, tc : 2 * tc]], axis=0)
        ci = jnp.concatenate([gp[:, 2 * tc :] * keep_p, gv[:, 2 * tc :], gn[:, 2 * tc :]], axis=0)
        u = gc * ci
        cv = _conv3(u, w, b_ref[...])[SUBLANES:]
        dco = jnp.concatenate([d_ref[...], dn_ref[...] * keep_n], axis=0)
        gb = jnp.concatenate([gv[:, :tc], gn[:, :tc]], axis=0)
        dcv = dco * gb
        du = _conv3_t(dcv, w)[:ts]
        dg_ref[:, :tc] = (dco * cv)[:ts].astype(BF16)
        dg_ref[:, tc : 2 * tc] = (du * gv[:, 2 * tc :]).astype(BF16)
        dg_ref[:, 2 * tc :] = (du * gv[:, tc : 2 * tc]).astype(BF16)
        dt = dcv[:ts]
        u1, u2 = pltpu.roll(u, 1, 0), pltpu.roll(u, 2, 0)
        lo, hi = SUBLANES, SUBLANES + ts
        _acc_rows(s_ref, i, [_colsum(dt * u2[lo:hi]), _colsum(dt * u1[lo:hi]), _colsum(dt * u[lo:hi]), _colsum(dt)])

    def triple(rows, which):
        return pl.BlockSpec((rows, 3 * tc), lambda j, i: (which(i), t0 + j))

    return pl.pallas_call(
        body,
        name="mixer_bwd",
        out_shape=(jax.ShapeDtypeStruct((s, np_cols), BF16), jax.ShapeDtypeStruct((SUBLANES, cwid), F32)),
        grid=(cwid // tc, ni),
        in_specs=[
            pl.BlockSpec((ts, tc), lambda j, i: (i, oc + j)),
            pl.BlockSpec((SUBLANES, tc), lambda j, i: (nxt(i), oc + j)),
            triple(ts, lambda i: i), triple(SUBLANES, prev), triple(SUBLANES, nxt),
            pl.BlockSpec((CONV_K, tc), lambda j, i: (0, j)),
            pl.BlockSpec((1, tc), lambda j, i: (0, j)),
        ],
        out_specs=[triple(ts, lambda i: i), pl.BlockSpec((SUBLANES, tc), lambda j, i: (0, j))],
        compiler_params=_cp("parallel", "arbitrary"),
    )(dcat, dcat, proj, proj, proj, cw, cb)


def _pair_tile(f):
    return _tile(f, 1408, LANES)


def _pair_perm(f):
    nj = f // _pair_tile(f)
    return lambda p: (p % 2) * nj + p // 2


def _pair_cols(a):
    r, f2 = a.shape
    tc = _pair_tile(f2 // 2)
    return a.reshape(r, 2, f2 // (2 * tc), tc).transpose(0, 2, 1, 3).reshape(r, f2)


def _unpair_cols(a):
    r, f2 = a.shape
    tc = _pair_tile(f2 // 2)
    return a.reshape(r, f2 // (2 * tc), 2, tc).transpose(0, 2, 1, 3).reshape(r, f2)


def _ffn_act_fwd(up, cw, cb):
    s, f2 = up.shape
    f = f2 // 2
    ts = _tile(s, 512, SUBLANES)
    tc = _pair_tile(f)
    prev, _ = _halo_maps(ts, s)

    def body(u_ref, p_ref, w_ref, b_ref, o_ref):
        keep = jnp.where(pl.program_id(1) > 0, 1.0, 0.0)
        ext = jnp.concatenate([p_ref[...] * keep, u_ref[...]], axis=0)
        u = _conv3(ext, w_ref[...], b_ref[...])[SUBLANES:]
        a, g = u[:, :tc], u[:, tc:]
        o_ref[...] = ((g * jax.nn.sigmoid(g)) * a).astype(BF16)

    def pair(rows, which):
        return pl.BlockSpec((rows, 2 * tc), lambda j, i: (which(i), j))

    return pl.pallas_call(
        body,
        name="ffn_act_fwd",
        out_shape=jax.ShapeDtypeStruct((s, f), BF16),
        grid=(f // tc, s // ts),
        in_specs=[pair(ts, lambda i: i), pair(SUBLANES, prev), pair(CONV_K, lambda i: 0), pair(1, lambda i: 0)],
        out_specs=pl.BlockSpec((ts, tc), lambda j, i: (i, j)),
        compiler_params=_cp("parallel", "arbitrary"),
    )(up, up, cw, cb)


def _ffn_act_bwd(dact, up, cw, cb):
    s, f2 = up.shape
    f = f2 // 2
    ts = _tile(s, 256, SUBLANES)
    tc = _pair_tile(f)
    nj, ni = f // tc, s // ts
    prev, nxt = _halo_maps(ts, s)

    def body(d_ref, dn_ref, u_ref, up_ref, un_ref, w_ref, b_ref, dup_ref, s_ref):
        i = pl.program_id(1)
        keep_p = jnp.where(i > 0, 1.0, 0.0)
        keep_n = jnp.where(i < ni - 1, 1.0, 0.0)
        w = w_ref[...]
        ext = jnp.concatenate([up_ref[...] * keep_p, u_ref[...], un_ref[...]], axis=0)
        u = _conv3(ext, w, b_ref[...])[SUBLANES:]
        a, g = u[:, :tc], u[:, tc:]
        dact_v = jnp.concatenate([d_ref[...], dn_ref[...] * keep_n], axis=0)
        sg = jax.nn.sigmoid(g)
        du = jnp.concatenate([dact_v * (g * sg), dact_v * a * (sg * (1.0 + g * (1.0 - sg)))], axis=1)
        dup_ref[...] = _conv3_t(du, w)[:ts].astype(BF16)
        dt = du[:ts]
        lo, hi = SUBLANES, SUBLANES + ts
        e1, e2 = pltpu.roll(ext, 1, 0), pltpu.roll(ext, 2, 0)
        _acc_rows(s_ref, i, [_colsum(dt * e2[lo:hi]), _colsum(dt * e1[lo:hi]), _colsum(dt * ext[lo:hi]), _colsum(dt)])

    def pair(rows, which):
        return pl.BlockSpec((rows, 2 * tc), lambda j, i: (which(i), j))

    return pl.pallas_call(
        body,
        name="ffn_act_bwd",
        out_shape=(jax.ShapeDtypeStruct((s, f2), BF16), jax.ShapeDtypeStruct((SUBLANES, f2), F32)),
        grid=(nj, ni),
        in_specs=[
            pl.BlockSpec((ts, tc), lambda j, i: (i, j)),
            pl.BlockSpec((SUBLANES, tc), lambda j, i: (nxt(i), j)),
            pair(ts, lambda i: i), pair(SUBLANES, prev), pair(SUBLANES, nxt),
            pair(CONV_K, lambda i: 0), pair(1, lambda i: 0),
        ],
        out_specs=[pair(ts, lambda i: i), pair(SUBLANES, lambda i: 0)],
        compiler_params=_cp("parallel", "arbitrary"),
    )(dact, dact, up, up, up, cw, cb)


ATT_SCALE = 1.0 / math.sqrt(NOPE + ROPE)
LOG2E = math.log2(math.e)
ATT_C2 = ATT_SCALE * LOG2E
STAT_SPLIT = 64
NT = (((1,), (1,)), ((), ()))
TN = (((0,), (0,)), ((), ()))


def _head_cat(q, kv, kr, tabs, n_heads):
    s, w2 = q.shape
    w = w2 // 2
    ts = _tile(s, 512, SUBLANES)
    hd = NOPE + HEAD_PAD

    def body(q_ref, kv_ref, kr_ref, c_ref, sa_ref, sb_ref, qc_ref, kc_ref):
        qv = q_ref[...]
        qr = _rope(qv[:, w:], c_ref[...], sa_ref[...], sb_ref[...]).astype(BF16)
        krv = kr_ref[...]
        for h in range(n_heads):
            qc_ref[:, h * hd : h * hd + NOPE] = qv[:, h * NOPE : (h + 1) * NOPE].astype(BF16)
            qc_ref[:, h * hd + NOPE : (h + 1) * hd] = qr[:, h * HEAD_PAD : (h + 1) * HEAD_PAD]
            kc_ref[:, h * hd : h * hd + NOPE] = kv_ref[:, h * NOPE : (h + 1) * NOPE]
            kc_ref[:, h * hd + NOPE : (h + 1) * hd] = krv

    out = jax.ShapeDtypeStruct((s, n_heads * hd), BF16)
    return pl.pallas_call(
        body,
        name="head_cat",
        out_shape=(out, out),
        grid=(s // ts,),
        in_specs=[_rows(ts, w2), _rows(ts, w), _rows(ts, HEAD_PAD)] + [_rows(ts, LANES)] * 3,
        out_specs=[_rows(ts, n_heads * hd)] * 2,
        compiler_params=_cp("parallel"),
    )(q, kv, kr, *tabs)


def _attn_fwd(qc, kc, kv, n_heads, cat_cols):
    s = qc.shape[0]
    t = _tile(s, ATT_FWD_BLOCK, LANES)
    sub = _tile(t, ATT_FWD_SUB, LANES)
    hh = n_heads
    hd = NOPE + HEAD_PAD

    def body(q_ref, k_ref, v_ref, o_ref, lse_ref, m_s, l_s, acc_s):
        i = pl.program_id(1)
        m_s[...] = jnp.full(m_s.shape, NEG, F32)
        l_s[...] = jnp.zeros(l_s.shape, F32)
        acc_s[...] = jnp.zeros(acc_s.shape, F32)

        def chunk(k0, diag):
            m_all, l_all, acc_all = m_s[...], l_s[...], acc_s[...]
            new_m, new_l, new_acc = [], [], []

            def scores(r0):
                ncol = r0 + sub if diag else t
                return lax.dot_general(q_ref[pl.ds(r0, sub), :], k_ref[pl.ds(k0, ncol), :], NT, preferred_element_type=F32)

            sc_next = scores(0)
            for r0 in range(0, t, sub):
                ncol = r0 + sub if diag else t
                sc = sc_next
                if r0 + sub < t:
                    sc_next = scores(r0 + sub)
                if diag:
                    row = lax.broadcasted_iota(jnp.int32, sc.shape, 0) + r0
                    col = lax.broadcasted_iota(jnp.int32, sc.shape, 1)
                    sc = jnp.where(col <= row, sc, NEG)
                m_prev = m_all[r0 : r0 + sub]
                m_new = jnp.maximum(m_prev, jnp.max(sc, axis=1, keepdims=True))
                alpha = jnp.exp2((m_prev - m_new) * ATT_C2)
                p = jnp.exp2((sc - m_new) * ATT_C2)
                pv = jnp.dot(p.astype(BF16), v_ref[pl.ds(k0, ncol), :], preferred_element_type=F32)
                new_m.append(m_new)
                new_l.append(alpha * l_all[r0 : r0 + sub] + jnp.sum(p, axis=1, keepdims=True))
                new_acc.append(alpha * acc_all[r0 : r0 + sub] + pv)
            m_s[...] = jnp.concatenate(new_m, axis=0)
            l_s[...] = jnp.concatenate(new_l, axis=0)
            acc_s[...] = jnp.concatenate(new_acc, axis=0)

        def loop_body(k, carry):
            chunk(pl.multiple_of(k * t, t), False)
            return carry

        lax.fori_loop(0, i, loop_body, 0)
        chunk(pl.multiple_of(i * t, t), True)
        l = l_s[...]
        o_ref[...] = (acc_s[...] / l).astype(BF16)
        lse_ref[...] = jnp.broadcast_to(m_s[...] * ATT_C2 + jnp.log(l) * LOG2E, lse_ref.shape)

    return pl.pallas_call(
        body,
        name="attn_fwd",
        out_shape=(jax.ShapeDtypeStruct((s, cat_cols), BF16), jax.ShapeDtypeStruct((s, hh * LANES), F32)),
        grid=(hh, s // t),
        in_specs=[
            pl.BlockSpec((t, hd), lambda h, i: (i, h)),
            pl.BlockSpec((s, hd), lambda h, i: (0, h)),
            pl.BlockSpec((s, VDIM), lambda h, i: (0, hh + h)),
        ],
        out_specs=[pl.BlockSpec((t, VDIM), lambda h, i: (i, h)), pl.BlockSpec((t, LANES), lambda h, i: (i, h))],
        scratch_shapes=[pltpu.VMEM((t, 1), F32), pltpu.VMEM((t, 1), F32), pltpu.VMEM((t, VDIM), F32)],
        compiler_params=_cp("parallel", "parallel"),
    )(qc, kc, kv)


def _attn_bwd_prep(cat, dcat, lse2, n_heads):
    s, w = lse2.shape
    ts = _tile(s, 512, SUBLANES)

    def body(o_ref, do_ref, lse_ref, dob_ref, st_ref):
        do = do_ref[...]
        dob_ref[...] = do.astype(BF16)
        prod = do * o_ref[...].astype(F32)
        lane = lax.broadcasted_iota(jnp.int32, (ts, LANES), 1)
        for h in range(n_heads):
            cols = slice(h * LANES, (h + 1) * LANES)
            dsum = jnp.sum(prod[:, cols], axis=1, keepdims=True)
            st_ref[:, cols] = jnp.where(lane < STAT_SPLIT, lse_ref[:, cols], dsum)

    return pl.pallas_call(
        body,
        name="attn_bwd_prep",
        out_shape=(jax.ShapeDtypeStruct((s, w), BF16), jax.ShapeDtypeStruct((s, w), F32)),
        grid=(s // ts,),
        in_specs=[_rows(ts, w)] * 3,
        out_specs=[_rows(ts, w)] * 2,
        compiler_params=_cp("parallel"),
    )(cat, dcat, lse2)


def _attn_bwd(qc, kc, kv, dob, stats, n_heads):
    s = qc.shape[0]
    t = _tile(s, ATT_BWD_BLOCK, LANES)
    sub = _tile(t, ATT_BWD_SUB, LANES)
    nb = s // t
    hh = n_heads
    hd = NOPE + HEAD_PAD
    w = hh * LANES

    def body(q_ref, k_ref, v_ref, do_ref, st_ref, dq_ref, dkn_ref, dv_ref, dkr_ref, dk_s, dv_s):
        j = pl.program_id(1)

        @pl.when(j == 0)
        def _():
            dq_ref[...] = jnp.zeros(dq_ref.shape, F32)

        dk_s[...] = jnp.zeros(dk_s.shape, F32)
        dv_s[...] = jnp.zeros(dv_s.shape, F32)

        def pair(i0, diag):
            def width(r0):
                return r0 + sub if diag else t

            def products(r0):
                rows = pl.ds(i0 + r0, sub)
                sc_ = lax.dot_general(q_ref[rows, :], k_ref[0 : width(r0), :], NT, preferred_element_type=F32)
                dp_ = lax.dot_general(do_ref[rows, :], v_ref[0 : width(r0), :], NT, preferred_element_type=F32)
                return sc_, dp_

            nxt = products(0)
            for r0 in range(0, t, sub):
                ncol = width(r0)
                rows = pl.ds(i0 + r0, sub)
                kk = k_ref[0:ncol, :]
                qq, do, st = q_ref[rows, :], do_ref[rows, :], st_ref[rows, :]
                sc, dp = nxt
                if r0 + sub < t:
                    nxt = products(r0 + sub)
                if diag:
                    row = lax.broadcasted_iota(jnp.int32, sc.shape, 0) + r0
                    col = lax.broadcasted_iota(jnp.int32, sc.shape, 1)
                    sc = jnp.where(col <= row, sc, NEG)
                p = jnp.exp2(sc * ATT_C2 - st[:, 0:1])
                dv_s[0:ncol, :] += lax.dot_general(p.astype(BF16), do, TN, preferred_element_type=F32)
                ds = (p * (dp - st[:, STAT_SPLIT : STAT_SPLIT + 1]) * ATT_SCALE).astype(BF16)
                dk_s[0:ncol, :] += lax.dot_general(ds, qq, TN, preferred_element_type=F32)
                dq_ref[rows, :] += jnp.dot(ds, kk, preferred_element_type=F32)

        pair(pl.multiple_of(j * t, t), True)

        def loop_body(i, carry):
            pair(pl.multiple_of(i * t, t), False)
            return carry

        lax.fori_loop(j + 1, nb, loop_body, 0)
        dkn_ref[...] = dk_s[:, :NOPE].astype(BF16)
        dv_ref[...] = dv_s[...].astype(BF16)
        dkr_ref[...] = dk_s[:, NOPE:]

    whole = lambda width, off: pl.BlockSpec((s, width), lambda h, j: (0, off + h))
    blk = lambda width, off: pl.BlockSpec((t, width), lambda h, j: (j, off + h))
    return pl.pallas_call(
        body,
        name="attn_bwd",
        out_shape=(
            jax.ShapeDtypeStruct((s, hh * hd), F32),
            jax.ShapeDtypeStruct((s, w), BF16),
            jax.ShapeDtypeStruct((s, w), BF16),
            jax.ShapeDtypeStruct((s, w), F32),
        ),
        grid=(hh, nb),
        in_specs=[whole(hd, 0), blk(hd, 0), blk(VDIM, hh), whole(VDIM, 0), whole(LANES, 0)],
        out_specs=[whole(hd, 0), blk(NOPE, 0), blk(VDIM, 0), blk(HEAD_PAD, 0)],
        scratch_shapes=[pltpu.VMEM((t, hd), F32), pltpu.VMEM((t, VDIM), F32)],
        compiler_params=_cp("parallel", "arbitrary"),
    )(qc, kc, kv, dob, stats)


def _dq_unrope(dq, tabs, n_heads):
    s = dq.shape[0]
    hd = NOPE + HEAD_PAD
    w = n_heads * LANES
    ts = _tile(s, 512, SUBLANES)

    def body(d_ref, c_ref, sa_ref, sb_ref, o_ref):
        c, sa, sb = c_ref[...], sa_ref[...], sb_ref[...]
        for h in range(n_heads):
            o_ref[:, h * NOPE : (h + 1) * NOPE] = d_ref[:, h * hd : h * hd + NOPE].astype(BF16)
            rot = _rope_t(d_ref[:, h * hd + NOPE : (h + 1) * hd], c, sa, sb)
            o_ref[:, w + h * HEAD_PAD : w + (h + 1) * HEAD_PAD] = rot.astype(BF16)

    return pl.pallas_call(
        body,
        name="dq_unrope",
        out_shape=jax.ShapeDtypeStruct((s, 2 * w), BF16),
        grid=(s // ts,),
        in_specs=[_rows(ts, n_heads * hd)] + [_rows(ts, LANES)] * 3,
        out_specs=_rows(ts, 2 * w),
        compiler_params=_cp("parallel"),
    )(dq, *tabs)


def _adamw(w, m, v, grads, name):
    r, c = w.shape
    budget_rows = max(SUBLANES, (VMEM_LIMIT // 3) // (4 * c * 2 * (7 + len(grads))))
    tr = _tile(r, budget_rows, SUBLANES)
    ng = len(grads)
    c1 = 1.0 - ADAM_B1**ADAM_STEP
    c2 = 1.0 - ADAM_B2**ADAM_STEP

    def body(*refs):
        w_ref, m_ref, v_ref = refs[:3]
        g_ref, d_ref, nm_ref, nv_ref = refs[3 + ng :]
        g = refs[3][...]
        for extra in refs[4 : 3 + ng]:
            g = g + extra[...]
        mn = ADAM_B1 * m_ref[...] + (1.0 - ADAM_B1) * g
        vn = ADAM_B2 * v_ref[...] + (1.0 - ADAM_B2) * (g * g)
        g_ref[...] = g
        nm_ref[...] = mn
        nv_ref[...] = vn
        d_ref[...] = -ADAM_LR * ((mn / c1) / (jnp.sqrt(vn / c2) + ADAM_EPS) + ADAM_WD * w_ref[...])

    blk = pl.BlockSpec((tr, c), lambda i: (i, 0))
    out = jax.ShapeDtypeStruct((r, c), F32)
    return pl.pallas_call(
        body,
        name=name,
        out_shape=(out, out, out, out),
        grid=(r // tr,),
        in_specs=[blk] * (3 + ng),
        out_specs=[blk] * 4,
        compiler_params=_cp("parallel"),
    )(w, m, v, *grads)


def _ada_grad(ca_t, dm):
    d = ca_t.shape[0]
    nc = dm.shape[1]
    tn = _tile(nc, 512, LANES)

    def body(a_ref, b_ref, o_ref):
        o_ref[...] = jnp.dot(a_ref[...].astype(BF16), b_ref[...].astype(BF16), preferred_element_type=F32)

    return pl.pallas_call(
        body,
        name="ada_grad",
        out_shape=jax.ShapeDtypeStruct((d, nc), F32),
        grid=(nc // tn,),
        in_specs=[pl.BlockSpec((d, LANES), lambda j: (0, 0)), pl.BlockSpec((LANES, tn), lambda j: (0, j))],
        out_specs=pl.BlockSpec((d, tn), lambda j: (0, j)),
        compiler_params=_cp("parallel"),
    )(ca_t, dm)


def _sum_devices(g):
    n = g.shape[1]

    def body(g_ref, o_ref):
        acc = g_ref[0:SUBLANES, :]
        for dvc in range(1, N_DEV):
            acc = acc + g_ref[dvc * SUBLANES : (dvc + 1) * SUBLANES, :]
        o_ref[...] = acc

    return pl.pallas_call(
        body,
        name="sum_devices",
        out_shape=jax.ShapeDtypeStruct((SUBLANES, n), F32),
        in_specs=[pl.BlockSpec(memory_space=pltpu.VMEM)],
        out_specs=pl.BlockSpec(memory_space=pltpu.VMEM),
        compiler_params=pltpu.CompilerParams(vmem_limit_bytes=VMEM_LIMIT),
    )(g)


def _sum_chips(land, sent, name):
    _, r, c = land.shape
    tr = _tile(r, max(SUBLANES * 2, (VMEM_LIMIT // 4) // (c * 2 * (4 * N_CHIP + 4 * 2))), SUBLANES * 2)

    def body(l_ref, s_ref, o_ref):
        x, y, _ = _mesh_pos()
        me = 2 * x + y
        acc = jnp.where(me == 0, s_ref[0], l_ref[0]).astype(F32)
        for k in range(1, N_CHIP):
            acc = acc + jnp.where(me == k, s_ref[k], l_ref[k]).astype(F32)
        o_ref[...] = acc

    slots = pl.BlockSpec((N_CHIP, tr, c), lambda i: (0, i, 0))
    return pl.pallas_call(
        body,
        name=name,
        out_shape=jax.ShapeDtypeStruct((r, c), F32),
        grid=(r // tr,),
        in_specs=[slots, slots],
        out_specs=pl.BlockSpec((tr, c), lambda i: (i, 0)),
        compiler_params=_cp("parallel"),
    )(land, sent)


def _mesh_pos():
    return lax.axis_index("x"), lax.axis_index("y"), lax.axis_index("c")


def _other_chips(x, y):
    return [(1 - x, y), (x, 1 - y), (1 - x, 1 - y)]


def _all_gather8(x_shard, name):
    m_per, n = x_shard.shape

    def body(x_ref, out_ref, send_sems, recv_sems, local_sem):
        x, y, c = _mesh_pos()
        me, sibling = (x, y, c), (x, y, 1 - c)
        chips = _other_chips(x, y)

        def rows(px, py, pc):
            return out_ref.at[pl.ds((4 * px + 2 * py + pc) * m_per, m_per), :]

        def copy(k, block, to, src=None):
            return pltpu.make_async_remote_copy(
                src_ref=rows(*block) if src is None else src,
                dst_ref=rows(*block),
                send_sem=send_sems.at[k],
                recv_sem=recv_sems.at[k],
                device_id=to,
                device_id_type=MESH,
            )

        mine = pltpu.make_async_copy(x_ref, rows(*me), local_sem)
        mine.start()
        first = [copy(0, me, sibling, src=x_ref)]
        first += [copy(1 + j, me, (*chip, c), src=x_ref) for j, chip in enumerate(chips)]
        for cp in first:
            cp.start()
        passed = [copy(4 + j, (*chip, c), sibling) for j, chip in enumerate(chips)]
        for j, chip in enumerate(chips):
            copy(1 + j, (*chip, c), me).wait_recv()
            passed[j].start()
        copy(0, sibling, me).wait_recv()
        for j, chip in enumerate(chips):
            copy(4 + j, (*chip, 1 - c), me).wait_recv()
        for cp in first + passed:
            cp.wait_send()
        mine.wait()

    return pl.pallas_call(
        body,
        name=name,
        out_shape=jax.ShapeDtypeStruct((N_DEV * m_per, n), x_shard.dtype),
        in_specs=[pl.BlockSpec(memory_space=pltpu.VMEM)],
        out_specs=pl.BlockSpec(memory_space=pltpu.VMEM),
        scratch_shapes=[pltpu.SemaphoreType.DMA((7,)), pltpu.SemaphoreType.DMA((7,)), pltpu.SemaphoreType.DMA],
        compiler_params=pltpu.CompilerParams(vmem_limit_bytes=VMEM_LIMIT),
    )(x_shard)


HBM_SPEC = pl.BlockSpec(memory_space=pltpu.HBM)
SEM_SPEC = pl.BlockSpec(memory_space=pltpu.SEMAPHORE)
DATAFLOW = pltpu.SideEffectType.DATAFLOW_SIDE_EFFECTING


def _half_rows(n_rows, c):
    return pl.ds(c * (n_rows // 2), n_rows // 2)


def _exchange_copies(ins, lands, send_sems, recv_sems, scatter, halves=False):
    x, y, c = _mesh_pos()
    me = 2 * x + y
    sends, recvs = [], []
    for t in range(len(ins)):
        rows = _half_rows(ins[t].shape[0], c) if halves else slice(None)
        for r, (px, py) in enumerate(_other_chips(x, y)):
            peer = 2 * px + py

            def copy(src, dst, k=3 * t + r, to=(px, py, c)):
                return pltpu.make_async_remote_copy(
                    src_ref=src, dst_ref=dst, send_sem=send_sems.at[k], recv_sem=recv_sems.at[k], device_id=to, device_id_type=MESH
                )

            if scatter:
                sends.append(copy(ins[t].at[peer], lands[t].at[me]))
                recvs.append(copy(ins[t].at[me], lands[t].at[peer]))
            else:
                sends.append(copy(ins[t].at[rows], lands[t].at[me, rows]))
                recvs.append(copy(ins[t].at[rows], lands[t].at[peer, rows]))
    return sends, recvs


def _sibling_fill(lands, name):
    nt = len(lands)

    def body(*refs):
        outs, send_sems, recv_sems = refs[nt : 2 * nt], refs[2 * nt], refs[2 * nt + 1]
        x, y, c = _mesh_pos()
        sends, recvs = [], []
        for t in range(nt):
            mine, theirs = _half_rows(outs[t].shape[1], c), _half_rows(outs[t].shape[1], 1 - c)
            for r, (px, py) in enumerate(_other_chips(x, y)):
                slot = 2 * px + py

                def copy(rows, k=3 * t + r, zone=outs[t], slot=slot):
                    part = zone.at[slot, rows]
                    return pltpu.make_async_remote_copy(
                        src_ref=part, dst_ref=part, send_sem=send_sems.at[k], recv_sem=recv_sems.at[k],
                        device_id=(x, y, 1 - c), device_id_type=MESH,
                    )

                sends.append(copy(mine))
                recvs.append(copy(theirs))
        for cp in sends:
            cp.start()
        for cp in recvs:
            cp.wait_recv()
        for cp in sends:
            cp.wait_send()

    return pl.pallas_call(
        body,
        name=name,
        out_shape=tuple(jax.ShapeDtypeStruct(a.shape, a.dtype) for a in lands),
        in_specs=[pl.BlockSpec(memory_space=pl.ANY)] * nt,
        out_specs=[pl.BlockSpec(memory_space=pl.ANY)] * nt,
        input_output_aliases={t: t for t in range(nt)},
        scratch_shapes=[pltpu.SemaphoreType.DMA((3 * nt,)), pltpu.SemaphoreType.DMA((3 * nt,))],
    )(*lands)


def _exchange_start(arrs, scatter, name, halves=False):
    nt = len(arrs)
    lands = [lax.empty(a.shape if scatter else (N_CHIP, *a.shape), a.dtype) for a in arrs]

    def body(*refs):
        ins, zones = refs[:nt], refs[nt : 2 * nt]
        send_sems, recv_sems, token = refs[2 * nt], refs[2 * nt + 1], refs[-1]
        sends, _ = _exchange_copies(ins, zones, send_sems, recv_sems, scatter, halves)
        for cp in sends:
            cp.start()
        token[...] = jnp.zeros(token.shape, F32)

    bufs = list(arrs) + list(lands)
    return pl.pallas_call(
        body,
        name=name,
        out_shape=(
            pltpu.SemaphoreType.DMA((3 * nt,)),
            pltpu.SemaphoreType.DMA((3 * nt,)),
            *[pltpu.HBM(a.shape, a.dtype) for a in bufs],
            jax.ShapeDtypeStruct((SUBLANES, LANES), F32),
        ),
        in_specs=[HBM_SPEC] * (2 * nt),
        out_specs=(SEM_SPEC, SEM_SPEC, *[HBM_SPEC] * (2 * nt), pl.BlockSpec(memory_space=pltpu.VMEM)),
        input_output_aliases={k: 2 + k for k in range(2 * nt)},
        compiler_params=pltpu.CompilerParams(has_side_effects=DATAFLOW),
    )(*[pltpu.with_memory_space_constraint(a, pltpu.HBM) for a in bufs])


def _exchange_wait(state, after, scatter, name, halves=False):
    send_sems, recv_sems, *bufs = state[:-1]
    nt = len(bufs) // 2
    afters = list(after) if isinstance(after, (list, tuple)) else [after]

    def body(*refs):
        ins, zones = refs[:nt], refs[nt : 2 * nt]
        sends, recvs = _exchange_copies(ins, zones, refs[2 * nt], refs[2 * nt + 1], scatter, halves)
        for cp in sends:
            cp.wait_send()
        for cp in recvs:
            cp.wait_recv()

    out = pl.pallas_call(
        body,
        name=name,
        out_shape=tuple(pltpu.HBM(a.shape, a.dtype) for a in bufs),
        in_specs=[HBM_SPEC] * (2 * nt) + [SEM_SPEC, SEM_SPEC] + [pl.BlockSpec(memory_space=pl.ANY)] * len(afters),
        out_specs=[HBM_SPEC] * (2 * nt),
        input_output_aliases={k: k for k in range(2 * nt)},
        compiler_params=pltpu.CompilerParams(has_side_effects=DATAFLOW),
    )(*bufs, send_sems, recv_sems, *afters)
    return list(out[:nt]), list(out[nt:])


def _swap_copies(ins, lands, send_sems, recv_sems):
    x, y, c = _mesh_pos()
    return [
        pltpu.make_async_remote_copy(
            src_ref=ins[t], dst_ref=lands[t], send_sem=send_sems.at[t], recv_sem=recv_sems.at[t],
            device_id=(x, y, 1 - c), device_id_type=MESH,
        )
        for t in range(len(ins))
    ]


def _swap_start(arrs, name):
    nt = len(arrs)
    lands = [lax.empty(a.shape, a.dtype) for a in arrs]

    def body(*refs):
        ins, zones = refs[:nt], refs[nt : 2 * nt]
        send_sems, recv_sems, token = refs[2 * nt], refs[2 * nt + 1], refs[-1]
        for cp in _swap_copies(ins, zones, send_sems, recv_sems):
            cp.start()
        token[...] = jnp.zeros(token.shape, F32)

    bufs = list(arrs) + lands
    return pl.pallas_call(
        body,
        name=name,
        out_shape=(
            pltpu.SemaphoreType.DMA((nt,)),
            pltpu.SemaphoreType.DMA((nt,)),
            *[pltpu.HBM(a.shape, a.dtype) for a in bufs],
            jax.ShapeDtypeStruct((SUBLANES, LANES), F32),
        ),
        in_specs=[HBM_SPEC] * (2 * nt),
        out_specs=(SEM_SPEC, SEM_SPEC, *[HBM_SPEC] * (2 * nt), pl.BlockSpec(memory_space=pltpu.VMEM)),
        input_output_aliases={k: 2 + k for k in range(2 * nt)},
        compiler_params=pltpu.CompilerParams(has_side_effects=DATAFLOW),
    )(*[pltpu.with_memory_space_constraint(a, pltpu.HBM) for a in bufs])


def _swap_wait(state, after, name):
    send_sems, recv_sems, *bufs = state[:-1]
    nt = len(bufs) // 2

    def body(*refs):
        cps = _swap_copies(refs[:nt], refs[nt : 2 * nt], refs[2 * nt], refs[2 * nt + 1])
        for cp in cps:
            cp.wait_send()
        for cp in cps:
            cp.wait_recv()

    out = pl.pallas_call(
        body,
        name=name,
        out_shape=tuple(pltpu.HBM(a.shape, a.dtype) for a in bufs),
        in_specs=[HBM_SPEC] * (2 * nt) + [SEM_SPEC, SEM_SPEC, pl.BlockSpec(memory_space=pl.ANY)],
        out_specs=[HBM_SPEC] * (2 * nt),
        input_output_aliases={k: k for k in range(2 * nt)},
        compiler_params=pltpu.CompilerParams(has_side_effects=DATAFLOW),
    )(*bufs, send_sems, recv_sems, after)
    return list(out[:nt]), list(out[nt:])


def _cols_from_shards(g):
    _, k, n = g.shape
    return jnp.transpose(g, (1, 0, 2)).reshape(k, N_CHIP * n)


def _cols_to_shards(a):
    k, n4 = a.shape
    return jnp.transpose(a.reshape(k, N_CHIP, n4 // N_CHIP), (1, 0, 2))


def _pad_to(vec, mult):
    n = vec.shape[0]
    return jnp.pad(vec, (0, (-n) % mult))


def kernel(x, c, positions, w_ada, b_ada, g_pre_mix, g_post_mix, w_in, g_q, w_uq, g_kv, w_ukv, conv_w_mix, conv_b_mix, w_o, g_pre_ffn, g_post_ffn, w_up, conv_w_ffn, conv_b_ffn, w_down, loss_target, m_w_ada, m_b_ada, m_g_pre_mix, m_g_post_mix, m_w_in, m_g_q, m_w_uq, m_g_kv, m_w_ukv, m_conv_w_mix, m_conv_b_mix, m_w_o, m_g_pre_ffn, m_g_post_ffn, m_w_up, m_conv_w_ffn, m_conv_b_ffn, m_w_down, v_w_ada, v_b_ada, v_g_pre_mix, v_g_post_mix, v_w_in, v_g_q, v_w_uq, v_g_kv, v_w_ukv, v_conv_w_mix, v_conv_b_mix, v_w_o, v_g_pre_ffn, v_g_post_ffn, v_w_up, v_conv_w_ffn, v_conv_b_ffn, v_w_down):
    xi, yi, ci = _mesh_pos()
    chip = 2 * xi + yi
    dev = 4 * xi + 2 * yi + ci

    s, d = x.shape[1], x.shape[2]
    ql, kl = g_q.shape[1], g_kv.shape[1]
    cwid = conv_b_mix.shape[1]
    f2 = conv_b_ffn.shape[1]
    hh = (w_uq.shape[2] * N_CHIP) // (NOPE + ROPE)
    w_att = hh * LANES
    nc_ada = w_ada.shape[2]
    lat = ql + kl + ROPE
    tc_mix = _gate_tile(cwid)
    lb = -(-(ql + kl + HEAD_PAD) // (3 * tc_mix)) * (3 * tc_mix)
    np_cols = lb + 3 * cwid
    assert cwid == hh * VDIM and w_att % tc_mix == 0

    x0 = x.reshape(s, d)
    tgt = loss_target.reshape(s, d)

    anchors = []

    def _behind(val, state):
        val, tok = lax.optimization_barrier((val, state[-1]))
        anchors.append(tok[0, 0])
        return val

    cwm_n, cwf_n = CONV_K * cwid // N_CHIP, CONV_K * f2 // N_CHIP
    pack_a = _pad_to(jnp.concatenate([c.reshape(-1), conv_w_mix.reshape(-1), conv_w_ffn.reshape(-1)]), SUBLANES * LANES)
    rows_a = _all_gather8(pack_a.reshape(SUBLANES, -1), "ag8_inputs").reshape(N_DEV, -1)
    c_all = rows_a[:, :d]
    south = rows_a[0::2]
    cw_mix = jnp.concatenate([south[j, d : d + cwm_n].reshape(CONV_K, -1) for j in range(N_CHIP)], axis=1)
    cw_ffn = jnp.concatenate([south[j, d + cwm_n : d + cwm_n + cwf_n].reshape(CONV_K, -1) for j in range(N_CHIP)], axis=1)

    b_cols = lax.dynamic_slice(b_ada, (0, chip * nc_ada), (1, nc_ada))
    mod_part, c_act = _ada_fwd(c_all, w_ada[0], b_cols)
    mod_rows = _all_gather8(mod_part, "ag8_mod")
    mod = jnp.concatenate(
        [lax.dynamic_slice_in_dim(mod_rows, 2 * N_DEV * j + dev, 1, axis=0) for j in range(N_CHIP)], axis=1
    )

    shards = [a[0].astype(BF16) for a in (w_in, w_uq, w_ukv, w_o, w_up, w_down)]
    first, mod = lax.optimization_barrier((shards[:3], mod))
    ag_a = _exchange_start(first, False, "ag_a_start", halves=True)
    mod = _behind(mod, ag_a)
    sh_m, sc_m, gt_m, sh_f, sc_f, gt_f = [mod[:, k * d : (k + 1) * d] for k in range(N_MOD)]

    inv_freq = 1.0 / (ROPE_THETA ** (jnp.arange(0, ROPE, 2, dtype=F32) / ROPE))
    invf = jnp.concatenate([inv_freq, inv_freq, jnp.zeros((LANES - ROPE,), F32)]).reshape(1, LANES)
    tabs = _rope_tables(positions.astype(F32).reshape(s, 1), invf)
    h1 = _pre_fwd(x0, g_pre_mix, sc_m, sh_m)

    def with_own(landed, own):
        return [lax.dynamic_update_slice_in_dim(g, a[None], chip, axis=0) for g, a in zip(landed, own)]

    own_w, landed_w = _exchange_wait(ag_a, [h1, tabs[0]], False, "ag_a_wait", halves=True)
    landed_w = list(_sibling_fill(landed_w, "ag_a_fill"))
    rest, landed_w = lax.optimization_barrier((shards[3:], landed_w))
    ag_b = _exchange_start(rest, False, "ag_b_start")
    h1 = _behind(h1, ag_b)
    g_in, g_uq, g_ukv = with_own(landed_w, own_w)
    full_in = _cols_from_shards(g_in)
    gate_cols = [(lat + k * cwid + j * tc_mix, tc_mix) for j in range(cwid // tc_mix) for k in range(3)]
    w_in_p = jnp.concatenate(
        [full_in[:, :lat], jnp.zeros((d, lb - lat), BF16)] + [full_in[:, o : o + n] for o, n in gate_cols], axis=1
    )
    full_uq = _cols_from_shards(g_uq).reshape(ql, hh, NOPE + ROPE)
    w_uq_p = jnp.concatenate(
        [
            full_uq[:, :, :NOPE].reshape(ql, w_att),
            jnp.pad(full_uq[:, :, NOPE:], ((0, 0), (0, 0), (0, HEAD_PAD - ROPE))).reshape(ql, w_att),
        ],
        axis=1,
    )
    full_ukv = _cols_from_shards(g_ukv).reshape(kl, hh, NOPE + VDIM)
    w_ukv_p = jnp.concatenate([full_ukv[:, :, :NOPE].reshape(kl, w_att), full_ukv[:, :, NOPE:].reshape(kl, w_att)], axis=1)

    proj = _matmul(h1, w_in_p, out_dtype=F32, tm=1024, tn=768, tk=2048, name="mm_proj")
    qn, kvn, kr = _latent_fwd(proj, g_q, g_kv, tabs, lb)
    q_f = _matmul(qn, w_uq_p, out_dtype=F32, tm=1024, tn=1024, tk=2048, name="mm_q")
    kv_p = _matmul(kvn, w_ukv_p, out_dtype=BF16, tm=1024, tn=1024, tk=2048, name="mm_kv")
    q_c, k_c = _head_cat(q_f, kv_p, kr, tabs, hh)
    cat, lse2 = _attn_fwd(q_c, k_c, kv_p, hh, w_att + cwid)
    cat = _mixer_fwd(cat, proj, cw_mix, conv_b_mix, lb, w_att)
    own_w, landed_w = _exchange_wait(ag_b, cat, False, "ag_b_wait")
    g_o, g_up, g_down = with_own(landed_w, own_w)
    w_o_f = g_o.reshape(-1, d)
    cw_ffn_p, cb_ffn_p = _pair_cols(cw_ffn), _pair_cols(conv_b_ffn)
    tcp, pair_perm = _pair_tile(f2 // 2), _pair_perm(f2 // 2)
    w_down_f = g_down.reshape(-1, d)
    mix = _matmul(cat, w_o_f, out_dtype=F32, tm=1024, tn=1024, tk=2048, name="mm_mix")

    x1, h2 = _mid_fwd(x0, mix, g_post_mix, gt_m, g_pre_ffn, sc_f, sh_f)
    up = _matmul(h2, g_up, out_dtype=F32, tm=1024, tn=tcp, tk=2048, name="mm_up", b_n_perm=pair_perm, b_col_shards=True)
    act = _ffn_act_fwd(up, cw_ffn_p, cb_ffn_p)
    y = _matmul(act, w_down_f, out_dtype=F32, tm=512, tn=1024, tk=5632, name="mm_down")
    dx2, dy, s_fin = _final(x1, y, tgt, g_post_ffn, gt_f)

    dw_down = _matmul(act, dy, ta=True, out_dtype=BF16, tm=512, tn=2048, tk=4096, name="mm_dw_down")
    dact = _matmul(dy, w_down_f, tb=True, out_dtype=F32, tm=1024, tn=1408, tk=2048, name="mm_dact")
    dup, s_ffn_p = _ffn_act_bwd(dact, up, cw_ffn_p, cb_ffn_p)
    s_ffn = _unpair_cols(s_ffn_p)
    dw_up = _matmul(
        h2, dup, ta=True, out_dtype=BF16, tm=512, tn=tcp, tk=4096, name="mm_dw_up", out_n_perm=pair_perm, out_col_shards=True
    )
    dh2 = _matmul_pair_k(dup, g_up, out_dtype=F32, tm=512, tn=1024, pairs=2, name="mm_dh2")
    dx1, dmix, s_mid = _mid_bwd(dh2, dx2, x1, mix, g_pre_ffn, sc_f, g_post_mix, gt_m)

    dw_o = _matmul(cat, dmix, ta=True, out_dtype=BF16, tm=512, tn=1024, tk=4096, name="mm_dw_o")
    send_b = [dw_o.reshape(N_CHIP, -1, d), dw_up, dw_down.reshape(N_CHIP, -1, d)]
    rs_b = _exchange_start(send_b, True, "rs_b_start")
    dmix = _behind(dmix, rs_b)
    dcat = _matmul(dmix, w_o_f, tb=True, out_dtype=F32, tm=1024, tn=1024, tk=2048, name="mm_dcat")
    dproj, s_mix = _mixer_bwd(dcat, proj, cw_mix, conv_b_mix, lb, w_att)
    dob, stats = _attn_bwd_prep(cat, dcat, lse2, hh)
    dq_raw, dkv_k, dkv_v, dkr_h = _attn_bwd(q_c, k_c, kv_p, dob, stats, hh)
    dkv_p = jnp.concatenate([dkv_k, dkv_v], axis=1)
    dq_p = _dq_unrope(dq_raw, tabs, hh)
    dw_uq_p = _matmul(qn, dq_p, ta=True, out_dtype=BF16, tm=1024, tn=1024, tk=1024, name="mm_dw_uq")
    dqn = _matmul(dq_p, w_uq_p, tb=True, out_dtype=F32, tm=1024, tn=1024, tk=2048, name="mm_dqn")
    dw_ukv_p = _matmul(kvn, dkv_p, ta=True, out_dtype=BF16, tm=1024, tn=1024, tk=1024, name="mm_dw_ukv")
    dkvn = _matmul(dkv_p, w_ukv_p, tb=True, out_dtype=F32, tm=1024, tn=1024, tk=2048, name="mm_dkvn")
    dproj, s_lat = _latent_bwd(dproj, proj, dqn, dkvn, dkr_h, g_q, g_kv, tabs, lb)
    dw_in_p = _matmul(h1, dproj, ta=True, out_dtype=BF16, tm=512, tn=1536, tk=4096, name="mm_dw_in")

    n_trip = cwid // tc_mix
    ungate = [lb + (3 * j + k) * tc_mix for k in range(3) for j in range(n_trip)]
    dw_in_f = jnp.concatenate([dw_in_p[:, :lat]] + [dw_in_p[:, o : o + tc_mix] for o in ungate], axis=1)
    uq3 = dw_uq_p.reshape(ql, 2, hh, LANES)
    dw_uq_f = jnp.concatenate([uq3[:, 0], uq3[:, 1, :, :ROPE]], axis=2).reshape(ql, hh * (NOPE + ROPE))
    ukv3 = dw_ukv_p.reshape(kl, 2, hh, LANES)
    dw_ukv_f = jnp.concatenate([ukv3[:, 0], ukv3[:, 1]], axis=2).reshape(kl, hh * (NOPE + VDIM))
    send_a = [_cols_to_shards(dw_in_f), _cols_to_shards(dw_uq_f), _cols_to_shards(dw_ukv_f)]
    rs_a = _exchange_start(send_a, True, "rs_a_start")
    dproj = _behind(dproj, rs_a)

    dh1 = _matmul(dproj, w_in_p, tb=True, out_dtype=F32, tm=512, tn=1024, tk=4608, name="mm_dh1")
    grad_x, s_first = _first_bwd(dh1, dx1, x0, g_pre_mix, sc_m)

    names = ["w_in", "w_uq", "w_ukv", "w_o", "w_up", "w_down"]
    sent_b, landed_b = _exchange_wait(rs_b, s_first, True, "rs_b_wait")
    sent_a, landed_a = _exchange_wait(rs_a, landed_b[0], True, "rs_a_wait")
    landed_a, s_first = lax.optimization_barrier((landed_a, s_first))
    part = [_sum_chips(l, a, "sum_chips_" + n) for l, a, n in zip(landed_a + landed_b, sent_a + sent_b, names)]

    dmod = jnp.concatenate([s_first[0:1], s_first[1:2], s_mid[3:4], s_mid[0:1], s_mid[1:2], s_fin[0:1]], axis=1)
    small = [
        dmod,
        s_first[2:3],
        s_mid[4:5],
        s_lat[0:1, :ql],
        s_lat[0:1, ql : ql + kl],
        s_mix[3:4],
        s_mid[2:3],
        s_fin[1:2],
        s_ffn[3:4],
        s_mix[0:3].reshape(1, -1),
        s_ffn[0:3].reshape(1, -1),
        s_fin[3:4, :LANES],
    ]
    sizes = [a.shape[1] for a in small]
    offs = [0]
    for n in sizes:
        offs.append(offs[-1] + n)
    pack_g = _pad_to(jnp.concatenate(small, axis=1).reshape(-1), SUBLANES * LANES * SUBLANES).reshape(SUBLANES, -1)
    gathered = _all_gather8(pack_g, "ag8_small_grads")
    tot = _sum_devices(gathered).reshape(-1)
    part_of = lambda k: tot[offs[k] : offs[k + 1]]
    dmod_all = gathered.reshape(N_DEV, -1)[:, : N_MOD * d]
    loss = part_of(11)[0]

    g_b_ada = part_of(0).reshape(1, -1)
    g_vecs = [part_of(k).reshape(1, -1) for k in range(1, 9)]
    g_cw_mix = lax.dynamic_slice(part_of(9).reshape(CONV_K, cwid), (0, chip * (cwid // N_CHIP)), (CONV_K, cwid // N_CHIP))
    g_cw_ffn = lax.dynamic_slice(part_of(10).reshape(CONV_K, f2), (0, chip * (f2 // N_CHIP)), (CONV_K, f2 // N_CHIP))

    swap = _swap_start(part, "swap_start")
    dm_cols = _behind(lax.dynamic_slice(dmod_all, (0, chip * nc_ada), (N_DEV, nc_ada)), swap)
    g_w_ada = _ada_grad(
        jnp.pad(c_act.T, ((0, 0), (0, LANES - N_DEV))), jnp.pad(dm_cols, ((0, LANES - N_DEV), (0, 0)))
    )
    big = {"w_ada": [a[None] for a in _adamw(w_ada[0], m_w_ada[0], v_w_ada[0], [g_w_ada], "adamw_w_ada")]}
    part, other = _swap_wait(swap, big["w_ada"][1], "swap_wait")

    big_w = [w_in, w_uq, w_ukv, w_o, w_up, w_down]
    big_m = [m_w_in, m_w_uq, m_w_ukv, m_w_o, m_w_up, m_w_down]
    big_v = [v_w_in, v_w_uq, v_w_ukv, v_w_o, v_w_up, v_w_down]
    for n, w_, m_, v_, p_, o_ in zip(names, big_w, big_m, big_v, part, other):
        big[n] = [a[None] for a in _adamw(w_[0], m_[0], v_[0], [p_, o_], "adamw_" + n)]

    sm_names = ["b_ada", "g_pre_mix", "g_post_mix", "g_q", "g_kv", "conv_b_mix", "g_pre_ffn", "g_post_ffn", "conv_b_ffn",
                "conv_w_mix", "conv_w_ffn"]
    sm_w = [b_ada, g_pre_mix, g_post_mix, g_q, g_kv, conv_b_mix, g_pre_ffn, g_post_ffn, conv_b_ffn, conv_w_mix, conv_w_ffn]
    sm_m = [m_b_ada, m_g_pre_mix, m_g_post_mix, m_g_q, m_g_kv, m_conv_b_mix, m_g_pre_ffn, m_g_post_ffn, m_conv_b_ffn,
            m_conv_w_mix, m_conv_w_ffn]
    sm_v = [v_b_ada, v_g_pre_mix, v_g_post_mix, v_g_q, v_g_kv, v_conv_b_mix, v_g_pre_ffn, v_g_post_ffn, v_conv_b_ffn,
            v_conv_w_mix, v_conv_w_ffn]
    sm_g = [g_b_ada] + g_vecs + [g_cw_mix, g_cw_ffn]
    flat = lambda arrs: jnp.concatenate([a.reshape(1, -1) for a in arrs], axis=1)
    sm_out = _adamw(flat(sm_w), flat(sm_m), flat(sm_v), [flat(sm_g)], "adamw_small")
    sm = {}
    off = 0
    for n, w_ in zip(sm_names, sm_w):
        sm[n] = [o[:, off : off + w_.size].reshape(w_.shape) for o in sm_out]
        off += w_.size

    order = ["w_ada", "b_ada", "g_pre_mix", "g_post_mix", "w_in", "g_q", "w_uq", "g_kv", "w_ukv", "conv_w_mix", "conv_b_mix",
             "w_o", "g_pre_ffn", "g_post_ffn", "w_up", "conv_w_ffn", "conv_b_ffn", "w_down"]
    res = {**big, **sm}
    outs = [loss + sum(anchors), grad_x.reshape(x.shape)]
    for k in range(4):
        outs += [res[n][k] for n in order]
    return tuple(outs)
```

```python
import math

import jax
import jax.numpy as jnp
from jax import lax
from jax.experimental import pallas as pl
from jax.experimental.pallas import tpu as pltpu

F32 = jnp.float32
BF16 = jnp.bfloat16
MESH = pl.DeviceIdType.MESH

N_DEV = 8
N_CHIP = 4
LANES = 128
SUBLANES = 8
VMEM_LIMIT = 56 * 2**20

NOPE = 128
ROPE = 64
VDIM = 128
HEAD_PAD = 128
ROPE_THETA = 10000.0
RMS_EPS = 1e-6
N_MOD = 6
CONV_K = 3
ATT_FWD_BLOCK, ATT_FWD_SUB = 2048, 256
ATT_BWD_BLOCK, ATT_BWD_SUB = 1024, 256
NEG = -1e30

ADAM_LR = 0.001
ADAM_B1 = 0.9
ADAM_B2 = 0.999
ADAM_EPS = 1e-08
ADAM_WD = 0.01
ADAM_STEP = 10


def _tile(n, pref, align):
    if n <= pref:
        return n
    t = (pref // align) * align
    while t >= align:
        if n % t == 0:
            return t
        t -= align
    return n


def _cp(*sem):
    return pltpu.CompilerParams(dimension_semantics=sem, vmem_limit_bytes=VMEM_LIMIT)


def _rsq(x):
    return lax.rsqrt(jnp.mean(x * x, axis=-1, keepdims=True) + RMS_EPS)


def _norm_bwd(dn, n, r):
    return r * (dn - n * jnp.mean(dn * n, axis=-1, keepdims=True))


def _colsum(a):
    return jnp.sum(a, axis=0, keepdims=True)


def _matmul(a, b, *, ta=False, tb=False, out_dtype, tm, tn, tk, name, b_n_perm=None, out_n_perm=None,
            b_col_shards=False, out_col_shards=False):
    assert not (b_col_shards and tb)
    if b_col_shards:
        b_rows, b_cols = b.shape[1], N_CHIP * b.shape[2]
    else:
        b_rows, b_cols = b.shape
    (k_a, m) = a.shape if ta else a.shape[::-1]
    (n, k_b) = (b_rows, b_cols) if tb else (b_cols, b_rows)
    assert k_a == k_b, (a.shape, b.shape, ta, tb)
    tm, tn, tk = _tile(m, tm, LANES), _tile(n, tn, LANES), _tile(k_a, tk, LANES)
    nk = k_a // tk
    same = lambda t: t
    bn, on = b_n_perm or same, out_n_perm or same
    a_spec = pl.BlockSpec((tk, tm), lambda i, j, k: (k, i)) if ta else pl.BlockSpec((tm, tk), lambda i, j, k: (i, k))
    if b_col_shards:
        per = (b_cols // N_CHIP) // tn
        b_spec = pl.BlockSpec((None, tk, tn), lambda i, j, k: (bn(j) // per, k, bn(j) % per))
    elif tb:
        b_spec = pl.BlockSpec((tn, tk), lambda i, j, k: (bn(j), k))
    else:
        b_spec = pl.BlockSpec((tk, tn), lambda i, j, k: (k, bn(j)))
    if out_col_shards:
        per_o = (n // N_CHIP) // tn
        out_shape = jax.ShapeDtypeStruct((N_CHIP, m, n // N_CHIP), out_dtype)
        out_spec = pl.BlockSpec((None, tm, tn), lambda i, j, k: (on(j) // per_o, i, on(j) % per_o))
    else:
        out_shape = jax.ShapeDtypeStruct((m, n), out_dtype)
        out_spec = pl.BlockSpec((tm, tn), lambda i, j, k: (i, on(j)))
    dims = (((0 if ta else 1,), (1 if tb else 0,)), ((), ()))

    def body(a_ref, b_ref, o_ref, *acc):
        p = lax.dot_general(a_ref[...].astype(BF16), b_ref[...].astype(BF16), dims, preferred_element_type=F32)
        _accumulate(p, o_ref, acc, nk)

    return pl.pallas_call(
        body,
        name=name,
        out_shape=out_shape,
        grid=(m // tm, n // tn, nk),
        in_specs=[a_spec, b_spec],
        out_specs=out_spec,
        scratch_shapes=[] if nk == 1 else [pltpu.VMEM((tm, tn), F32)],
        compiler_params=_cp("parallel", "parallel", "arbitrary"),
    )(a, b)


def _accumulate(p, o_ref, acc, nk):
    if nk == 1:
        o_ref[...] = p.astype(o_ref.dtype)
        return
    k = pl.program_id(2)

    @pl.when(k == 0)
    def _():
        acc[0][...] = p

    @pl.when(k > 0)
    def _():
        acc[0][...] += p

    @pl.when(k == nk - 1)
    def _():
        o_ref[...] = acc[0][...].astype(o_ref.dtype)


def _matmul_pair_k(a, b_shards, *, out_dtype, tm, tn, pairs, name):
    m, f2 = a.shape
    n = b_shards.shape[1]
    tc = _pair_tile(f2 // 2)
    nj = (f2 // 2) // tc
    nk = nj // pairs
    per = (f2 // N_CHIP) // tc
    tm, tn = _tile(m, tm, LANES), _tile(n, tn, LANES)

    def body(a_ref, *refs):
        w_refs, (o_ref, *acc) = refs[: 2 * pairs], refs[2 * pairs :]
        av = a_ref[...]
        p = None
        for q in range(2 * pairs):
            part = lax.dot_general(av[:, q * tc : (q + 1) * tc], w_refs[q][...], NT, preferred_element_type=F32)
            p = part if p is None else p + part
        _accumulate(p, o_ref, acc, nk)

    def w_tile(first, q):
        return pl.BlockSpec((None, tn, tc), lambda i, j, k: ((first + pairs * k + q) // per, j, (first + pairs * k + q) % per))

    w_specs = [w_tile(first, q) for q in range(pairs) for first in (0, nj)]
    return pl.pallas_call(
        body,
        name=name,
        out_shape=jax.ShapeDtypeStruct((m, n), out_dtype),
        grid=(m // tm, n // tn, nk),
        in_specs=[pl.BlockSpec((tm, 2 * tc * pairs), lambda i, j, k: (i, k))] + w_specs,
        out_specs=pl.BlockSpec((tm, tn), lambda i, j, k: (i, j)),
        scratch_shapes=[] if nk == 1 else [pltpu.VMEM((tm, tn), F32)],
        compiler_params=_cp("parallel", "parallel", "arbitrary"),
    )(a, *[b_shards] * (2 * pairs))


def _rope_tables(pos_col, invf):
    s = pos_col.shape[0]
    ts = _tile(s, 1024, SUBLANES)
    half = ROPE // 2

    def body(p_ref, f_ref, c_ref, sa_ref, sb_ref):
        ang = p_ref[...] * f_ref[...]
        lane = lax.broadcasted_iota(jnp.int32, ang.shape, 1)
        cs, sn = jnp.cos(ang), jnp.sin(ang)
        c_ref[...] = jnp.where(lane < ROPE, cs, 0.0)
        sa_ref[...] = jnp.where((lane >= half) & (lane < ROPE), sn, 0.0)
        sb_ref[...] = jnp.where(lane < half, -sn, 0.0)

    tab = jax.ShapeDtypeStruct((s, LANES), F32)
    return pl.pallas_call(
        body,
        name="rope_tables",
        out_shape=(tab, tab, tab),
        grid=(s // ts,),
        in_specs=[pl.BlockSpec((ts, 1), lambda i: (i, 0)), pl.BlockSpec((1, LANES), lambda i: (0, 0))],
        out_specs=[pl.BlockSpec((ts, LANES), lambda i: (i, 0))] * 3,
        compiler_params=_cp("parallel"),
    )(pos_col, invf)


def _widen(t, w):
    return t if w == LANES else jnp.tile(t, (1, w // LANES))


def _rope(x, c, sa, sb):
    w = x.shape[1]
    c, sa, sb = _widen(c, w), _widen(sa, w), _widen(sb, w)
    return x * c + pltpu.roll(x, ROPE // 2, 1) * sa + pltpu.roll(x, w - ROPE // 2, 1) * sb


def _rope_t(d, c, sa, sb):
    w = d.shape[1]
    c, sa, sb = _widen(c, w), _widen(sa, w), _widen(sb, w)
    return d * c + pltpu.roll(d * sa, w - ROPE // 2, 1) + pltpu.roll(d * sb, ROPE // 2, 1)


def _ada_fwd(c_all, w, b):
    d, nc = w.shape
    tn = _tile(nc, 512, LANES)

    def body(c_ref, w_ref, b_ref, o_ref, ca_ref):
        cv = c_ref[...]
        ca = cv * jax.nn.sigmoid(cv)
        ca_ref[...] = ca
        o_ref[...] = jnp.dot(ca.astype(BF16), w_ref[...].astype(BF16), preferred_element_type=F32) + b_ref[...]

    return pl.pallas_call(
        body,
        name="ada_fwd",
        out_shape=(jax.ShapeDtypeStruct((N_DEV, nc), F32), jax.ShapeDtypeStruct((N_DEV, d), F32)),
        grid=(nc // tn,),
        in_specs=[
            pl.BlockSpec((N_DEV, d), lambda j: (0, 0)),
            pl.BlockSpec((d, tn), lambda j: (0, j)),
            pl.BlockSpec((1, tn), lambda j: (0, j)),
        ],
        out_specs=[pl.BlockSpec((N_DEV, tn), lambda j: (0, j)), pl.BlockSpec((N_DEV, d), lambda j: (0, 0))],
        compiler_params=_cp("arbitrary"),
    )(c_all, w, b)


def _rows(ts, d):
    return pl.BlockSpec((ts, d), lambda i: (i, 0))


def _vec(d):
    return pl.BlockSpec((1, d), lambda i: (0, 0))


def _sums(d):
    return pl.BlockSpec((SUBLANES, d), lambda i: (0, 0))


def _acc_rows(ref, i, rows):
    @pl.when(i == 0)
    def _():
        ref[...] = jnp.zeros(ref.shape, ref.dtype)

    for k, r in enumerate(rows):
        ref[k : k + 1, :] += r


def _pre_fwd(x, g, sc, sh):
    s, d = x.shape
    ts = _tile(s, 512, SUBLANES)

    def body(x_ref, g_ref, sc_ref, sh_ref, h_ref):
        xv = x_ref[...]
        h_ref[...] = (((xv * _rsq(xv)) * g_ref[...]) * (1.0 + sc_ref[...]) + sh_ref[...]).astype(BF16)

    return pl.pallas_call(
        body,
        name="pre_mix_fwd",
        out_shape=jax.ShapeDtypeStruct((s, d), BF16),
        grid=(s // ts,),
        in_specs=[_rows(ts, d), _vec(d), _vec(d), _vec(d)],
        out_specs=_rows(ts, d),
        compiler_params=_cp("parallel"),
    )(x, g, sc, sh)


def _mid_fwd(x0, mix, g_post, gt, g_pre, sc, sh):
    s, d = x0.shape
    ts = _tile(s, 512, SUBLANES)

    def body(x_ref, m_ref, gp_ref, gt_ref, g_ref, sc_ref, sh_ref, x1_ref, h_ref):
        mv = m_ref[...]
        x1 = x_ref[...] + gt_ref[...] * ((mv * _rsq(mv)) * gp_ref[...])
        x1_ref[...] = x1
        h_ref[...] = (((x1 * _rsq(x1)) * g_ref[...]) * (1.0 + sc_ref[...]) + sh_ref[...]).astype(BF16)

    return pl.pallas_call(
        body,
        name="mid_fwd",
        out_shape=(jax.ShapeDtypeStruct((s, d), F32), jax.ShapeDtypeStruct((s, d), BF16)),
        grid=(s // ts,),
        in_specs=[_rows(ts, d), _rows(ts, d)] + [_vec(d)] * 5,
        out_specs=[_rows(ts, d), _rows(ts, d)],
        compiler_params=_cp("parallel"),
    )(x0, mix, g_post, gt, g_pre, sc, sh)


def _final(x1, y, tgt, g_post, gt):
    s, d = x1.shape
    ts = _tile(s, 256, SUBLANES)
    ni = s // ts

    def body(x_ref, y_ref, t_ref, gp_ref, gt_ref, dx_ref, dy_ref, s_ref):
        i = pl.program_id(0)
        yv, gp, gt_v = y_ref[...], gp_ref[...], gt_ref[...]
        r = _rsq(yv)
        n = yv * r
        err = (x_ref[...] + gt_v * (n * gp)) - t_ref[...]
        dx = err * (1.0 / d)
        dx_ref[...] = dx
        dy_ref[...] = _norm_bwd(dx * (gt_v * gp), n, r).astype(BF16)
        _acc_rows(s_ref, i, [_colsum(dx * (n * gp)), _colsum(dx * gt_v * n), _colsum(err * err)])

        @pl.when(i == ni - 1)
        def _():
            tot = jnp.sum(s_ref[2:3, :], axis=1, keepdims=True) * (0.5 / d)
            s_ref[3:4, :] = jnp.broadcast_to(tot, (1, d))

    return pl.pallas_call(
        body,
        name="final_fwd_bwd",
        out_shape=(
            jax.ShapeDtypeStruct((s, d), F32),
            jax.ShapeDtypeStruct((s, d), BF16),
            jax.ShapeDtypeStruct((SUBLANES, d), F32),
        ),
        grid=(ni,),
        in_specs=[_rows(ts, d)] * 3 + [_vec(d)] * 2,
        out_specs=[_rows(ts, d), _rows(ts, d), _sums(d)],
        compiler_params=_cp("arbitrary"),
    )(x1, y, tgt, g_post, gt)


def _mid_bwd(dh2, dx2, x1, mix, g_pre, sc, g_post, gt):
    s, d = x1.shape
    ts = _tile(s, 256, SUBLANES)

    def body(dh_ref, dx2_ref, x_ref, m_ref, g_ref, sc_ref, gp_ref, gt_ref, dx1_ref, dm_ref, s_ref):
        i = pl.program_id(0)
        dh, xv, mv = dh_ref[...], x_ref[...], m_ref[...]
        g, sc_v, gp, gt_v = g_ref[...], sc_ref[...], gp_ref[...], gt_ref[...]
        r1 = _rsq(xv)
        n1 = xv * r1
        dx1 = dx2_ref[...] + _norm_bwd(dh * (g * (1.0 + sc_v)), n1, r1)
        dx1_ref[...] = dx1
        rm = _rsq(mv)
        nm = mv * rm
        dm_ref[...] = _norm_bwd(dx1 * (gt_v * gp), nm, rm).astype(BF16)
        _acc_rows(
            s_ref,
            i,
            [
                _colsum(dh),
                _colsum(dh * (n1 * g)),
                _colsum(dh * (1.0 + sc_v) * n1),
                _colsum(dx1 * (nm * gp)),
                _colsum(dx1 * gt_v * nm),
            ],
        )

    return pl.pallas_call(
        body,
        name="mid_bwd",
        out_shape=(
            jax.ShapeDtypeStruct((s, d), F32),
            jax.ShapeDtypeStruct((s, d), BF16),
            jax.ShapeDtypeStruct((SUBLANES, d), F32),
        ),
        grid=(s // ts,),
        in_specs=[_rows(ts, d)] * 4 + [_vec(d)] * 4,
        out_specs=[_rows(ts, d), _rows(ts, d), _sums(d)],
        compiler_params=_cp("arbitrary"),
    )(dh2, dx2, x1, mix, g_pre, sc, g_post, gt)


def _first_bwd(dh1, dx1, x0, g, sc):
    s, d = x0.shape
    ts = _tile(s, 256, SUBLANES)

    def body(dh_ref, dx1_ref, x_ref, g_ref, sc_ref, dx_ref, s_ref):
        i = pl.program_id(0)
        dh, xv, gv, sc_v = dh_ref[...], x_ref[...], g_ref[...], sc_ref[...]
        r = _rsq(xv)
        n = xv * r
        dx_ref[...] = dx1_ref[...] + _norm_bwd(dh * (gv * (1.0 + sc_v)), n, r)
        _acc_rows(s_ref, i, [_colsum(dh), _colsum(dh * (n * gv)), _colsum(dh * (1.0 + sc_v) * n)])

    return pl.pallas_call(
        body,
        name="first_bwd",
        out_shape=(jax.ShapeDtypeStruct((s, d), F32), jax.ShapeDtypeStruct((SUBLANES, d), F32)),
        grid=(s // ts,),
        in_specs=[_rows(ts, d)] * 3 + [_vec(d)] * 2,
        out_specs=[_rows(ts, d), _sums(d)],
        compiler_params=_cp("arbitrary"),
    )(dh1, dx1, x0, g, sc)


def _latent_fwd(proj, g_q, g_kv, tabs, lb):
    s = proj.shape[0]
    ql, kl = g_q.shape[1], g_kv.shape[1]
    ts = _tile(s, 512, SUBLANES)

    def body(p_ref, gq_ref, gk_ref, c_ref, sa_ref, sb_ref, q_ref, kv_ref, kr_ref):
        pv = p_ref[...]
        q, kv, kr = pv[:, :ql], pv[:, ql : ql + kl], pv[:, ql + kl : ql + kl + HEAD_PAD]
        q_ref[...] = ((q * _rsq(q)) * gq_ref[...]).astype(BF16)
        kv_ref[...] = ((kv * _rsq(kv)) * gk_ref[...]).astype(BF16)
        kr_ref[...] = _rope(kr, c_ref[...], sa_ref[...], sb_ref[...]).astype(BF16)

    return pl.pallas_call(
        body,
        name="latent_fwd",
        out_shape=(
            jax.ShapeDtypeStruct((s, ql), BF16),
            jax.ShapeDtypeStruct((s, kl), BF16),
            jax.ShapeDtypeStruct((s, HEAD_PAD), BF16),
        ),
        grid=(s // ts,),
        in_specs=[_rows(ts, lb), _vec(ql), _vec(kl)] + [_rows(ts, LANES)] * 3,
        out_specs=[_rows(ts, ql), _rows(ts, kl), _rows(ts, HEAD_PAD)],
        compiler_params=_cp("parallel"),
    )(proj, g_q, g_kv, *tabs)


def _latent_bwd(dproj, proj, dqn, dkvn, dkr_h, g_q, g_kv, tabs, lb):
    s = proj.shape[0]
    ql, kl = g_q.shape[1], g_kv.shape[1]
    hw = dkr_h.shape[1]
    ts = _tile(s, 512, SUBLANES)
    pad = lb - ql - kl - HEAD_PAD

    def body(_, p_ref, dq_ref, dkv_ref, dkr_ref, gq_ref, gk_ref, c_ref, sa_ref, sb_ref, o_ref, s_ref):
        i = pl.program_id(0)
        pv = p_ref[...]
        q, kv = pv[:, :ql], pv[:, ql : ql + kl]
        dqn_v, dkvn_v = dq_ref[...], dkv_ref[...]
        rq = _rsq(q)
        nq = q * rq
        rk = _rsq(kv)
        nk = kv * rk
        dkr = dkr_ref[:, :HEAD_PAD]
        for h in range(1, hw // HEAD_PAD):
            dkr = dkr + dkr_ref[:, h * HEAD_PAD : (h + 1) * HEAD_PAD]
        parts = [
            _norm_bwd(dqn_v * gq_ref[...], nq, rq).astype(BF16),
            _norm_bwd(dkvn_v * gk_ref[...], nk, rk).astype(BF16),
            _rope_t(dkr, c_ref[...], sa_ref[...], sb_ref[...]).astype(BF16),
        ]
        if pad:
            parts.append(jnp.zeros((ts, pad), BF16))
        o_ref[...] = jnp.concatenate(parts, axis=1)
        row = [_colsum(dqn_v * nq), _colsum(dkvn_v * nk), jnp.zeros((1, lb - ql - kl), F32)]
        _acc_rows(s_ref, i, [jnp.concatenate(row, axis=1)])

    return pl.pallas_call(
        body,
        name="latent_bwd",
        out_shape=(jax.ShapeDtypeStruct(dproj.shape, BF16), jax.ShapeDtypeStruct((SUBLANES, lb), F32)),
        grid=(s // ts,),
        in_specs=[pl.BlockSpec(memory_space=pl.ANY), _rows(ts, lb), _rows(ts, ql), _rows(ts, kl), _rows(ts, hw)]
        + [_vec(ql), _vec(kl)]
        + [_rows(ts, LANES)] * 3,
        out_specs=[_rows(ts, lb), _sums(lb)],
        input_output_aliases={0: 0},
        compiler_params=_cp("arbitrary"),
    )(dproj, proj, dqn, dkvn, dkr_h, g_q, g_kv, *tabs)


def _conv3(ext, w, b):
    return (pltpu.roll(ext, 2, 0) * w[0:1] + pltpu.roll(ext, 1, 0) * w[1:2]) + ext * w[2:3] + b


def _conv3_t(du, w):
    n = du.shape[0]
    return du * w[2:3] + pltpu.roll(du, n - 1, 0) * w[1:2] + pltpu.roll(du, n - 2, 0) * w[0:1]


def _halo_maps(ts, s):
    r8, last = ts // SUBLANES, s // SUBLANES - 1
    prev = lambda i: jnp.maximum(i * r8 - 1, 0)
    nxt = lambda i: jnp.minimum((i + 1) * r8, last)
    return prev, nxt


def _gate_tile(cwid):
    return _tile(cwid, 512, LANES)


def _mixer_fwd(cat, proj, cw, cb, lb, col0):
    s = proj.shape[0]
    cwid = cw.shape[1]
    ts = _tile(s, 512, SUBLANES)
    tc = _gate_tile(cwid)
    assert lb % (3 * tc) == 0 and col0 % tc == 0
    t0, oc = lb // (3 * tc), col0 // tc
    prev, _ = _halo_maps(ts, s)

    def body(_, g_ref, p_ref, w_ref, b_ref, o_ref):
        keep = jnp.where(pl.program_id(1) > 0, 1.0, 0.0)
        gv, pv = g_ref[...], p_ref[...]
        ext = jnp.concatenate([pv[:, tc : 2 * tc] * pv[:, 2 * tc :] * keep, gv[:, tc : 2 * tc] * gv[:, 2 * tc :]], axis=0)
        o_ref[...] = (gv[:, :tc] * _conv3(ext, w_ref[...], b_ref[...])[SUBLANES:]).astype(BF16)

    return pl.pallas_call(
        body,
        name="mixer_fwd",
        out_shape=jax.ShapeDtypeStruct(cat.shape, BF16),
        grid=(cwid // tc, s // ts),
        in_specs=[
            pl.BlockSpec(memory_space=pl.ANY),
            pl.BlockSpec((ts, 3 * tc), lambda j, i: (i, t0 + j)),
            pl.BlockSpec((SUBLANES, 3 * tc), lambda j, i: (prev(i), t0 + j)),
            pl.BlockSpec((CONV_K, tc), lambda j, i: (0, j)),
            pl.BlockSpec((1, tc), lambda j, i: (0, j)),
        ],
        out_specs=pl.BlockSpec((ts, tc), lambda j, i: (i, oc + j)),
        input_output_aliases={0: 0},
        compiler_params=_cp("parallel", "arbitrary"),
    )(cat, proj, proj, cw, cb)


def _mixer_bwd(dcat, proj, cw, cb, lb, col0):
    s, np_cols = proj.shape
    cwid = cw.shape[1]
    ts = _tile(s, 512, SUBLANES)
    tc = _gate_tile(cwid)
    t0, oc = lb // (3 * tc), col0 // tc
    ni = s // ts
    prev, nxt = _halo_maps(ts, s)

    def body(d_ref, dn_ref, g_ref, gp_ref, gn_ref, w_ref, b_ref, dg_ref, s_ref):
        i = pl.program_id(1)
        keep_p = jnp.where(i > 0, 1.0, 0.0)
        keep_n = jnp.where(i < ni - 1, 1.0, 0.0)
        w = w_ref[...]
        gv, gp, gn = g_ref[...], gp_ref[...], gn_ref[...]
        gc = jnp.concatenate([gp[:, tc : 2 * tc], gv[:, tc : 2 * tc], gn[:, tc : 2 * tc]], axis=0)
        ci = jnp.concatenate([gp[:, 2 * tc :] * keep_p, gv[:, 2 * tc :], gn[:, 2 * tc :]], axis=0)
        u = gc * ci
        cv = _conv3(u, w, b_ref[...])[SUBLANES:]
        dco = jnp.concatenate([d_ref[...], dn_ref[...] * keep_n], axis=0)
        gb = jnp.concatenate([gv[:, :tc], gn[:, :tc]], axis=0)
        dcv = dco * gb
        du = _conv3_t(dcv, w)[:ts]
        dg_ref[:, :tc] = (dco * cv)[:ts].astype(BF16)
        dg_ref[:, tc : 2 * tc] = (du * gv[:, 2 * tc :]).astype(BF16)
        dg_ref[:, 2 * tc :] = (du * gv[:, tc : 2 * tc]).astype(BF16)
        dt = dcv[:ts]
        u1, u2 = pltpu.roll(u, 1, 0), pltpu.roll(u, 2, 0)
        lo, hi = SUBLANES, SUBLANES + ts
        _acc_rows(s_ref, i, [_colsum(dt * u2[lo:hi]), _colsum(dt * u1[lo:hi]), _colsum(dt * u[lo:hi]), _colsum(dt)])

    def triple(rows, which):
        return pl.BlockSpec((rows, 3 * tc), lambda j, i: (which(i), t0 + j))

    return pl.pallas_call(
        body,
        name="mixer_bwd",
        out_shape=(jax.ShapeDtypeStruct((s, np_cols), BF16), jax.ShapeDtypeStruct((SUBLANES, cwid), F32)),
        grid=(cwid // tc, ni),
        in_specs=[
            pl.BlockSpec((ts, tc), lambda j, i: (i, oc + j)),
            pl.BlockSpec((SUBLANES, tc), lambda j, i: (nxt(i), oc + j)),
            triple(ts, lambda i: i), triple(SUBLANES, prev), triple(SUBLANES, nxt),
            pl.BlockSpec((CONV_K, tc), lambda j, i: (0, j)),
            pl.BlockSpec((1, tc), lambda j, i: (0, j)),
        ],
        out_specs=[triple(ts, lambda i: i), pl.BlockSpec((SUBLANES, tc), lambda j, i: (0, j))],
        compiler_params=_cp("parallel", "arbitrary"),
    )(dcat, dcat, proj, proj, proj, cw, cb)


def _pair_tile(f):
    return _tile(f, 1408, LANES)


def _pair_perm(f):
    nj = f // _pair_tile(f)
    return lambda p: (p % 2) * nj + p // 2


def _pair_cols(a):
    r, f2 = a.shape
    tc = _pair_tile(f2 // 2)
    return a.reshape(r, 2, f2 // (2 * tc), tc).transpose(0, 2, 1, 3).reshape(r, f2)


def _unpair_cols(a):
    r, f2 = a.shape
    tc = _pair_tile(f2 // 2)
    return a.reshape(r, f2 // (2 * tc), 2, tc).transpose(0, 2, 1, 3).reshape(r, f2)


def _ffn_act_fwd(up, cw, cb):
    s, f2 = up.shape
    f = f2 // 2
    ts = _tile(s, 512, SUBLANES)
    tc = _pair_tile(f)
    prev, _ = _halo_maps(ts, s)

    def body(u_ref, p_ref, w_ref, b_ref, o_ref):
        keep = jnp.where(pl.program_id(1) > 0, 1.0, 0.0)
        ext = jnp.concatenate([p_ref[...] * keep, u_ref[...]], axis=0)
        u = _conv3(ext, w_ref[...], b_ref[...])[SUBLANES:]
        a, g = u[:, :tc], u[:, tc:]
        o_ref[...] = ((g * jax.nn.sigmoid(g)) * a).astype(BF16)

    def pair(rows, which):
        return pl.BlockSpec((rows, 2 * tc), lambda j, i: (which(i), j))

    return pl.pallas_call(
        body,
        name="ffn_act_fwd",
        out_shape=jax.ShapeDtypeStruct((s, f), BF16),
        grid=(f // tc, s // ts),
        in_specs=[pair(ts, lambda i: i), pair(SUBLANES, prev), pair(CONV_K, lambda i: 0), pair(1, lambda i: 0)],
        out_specs=pl.BlockSpec((ts, tc), lambda j, i: (i, j)),
        compiler_params=_cp("parallel", "arbitrary"),
    )(up, up, cw, cb)


def _ffn_act_bwd(dact, up, cw, cb):
    s, f2 = up.shape
    f = f2 // 2
    ts = _tile(s, 256, SUBLANES)
    tc = _pair_tile(f)
    nj, ni = f // tc, s // ts
    prev, nxt = _halo_maps(ts, s)

    def body(d_ref, dn_ref, u_ref, up_ref, un_ref, w_ref, b_ref, dup_ref, s_ref):
        i = pl.program_id(1)
        keep_p = jnp.where(i > 0, 1.0, 0.0)
        keep_n = jnp.where(i < ni - 1, 1.0, 0.0)
        w = w_ref[...]
        ext = jnp.concatenate([up_ref[...] * keep_p, u_ref[...], un_ref[...]], axis=0)
        u = _conv3(ext, w, b_ref[...])[SUBLANES:]
        a, g = u[:, :tc], u[:, tc:]
        dact_v = jnp.concatenate([d_ref[...], dn_ref[...] * keep_n], axis=0)
        sg = jax.nn.sigmoid(g)
        du = jnp.concatenate([dact_v * (g * sg), dact_v * a * (sg * (1.0 + g * (1.0 - sg)))], axis=1)
        dup_ref[...] = _conv3_t(du, w)[:ts].astype(BF16)
        dt = du[:ts]
        lo, hi = SUBLANES, SUBLANES + ts
        e1, e2 = pltpu.roll(ext, 1, 0), pltpu.roll(ext, 2, 0)
        _acc_rows(s_ref, i, [_colsum(dt * e2[lo:hi]), _colsum(dt * e1[lo:hi]), _colsum(dt * ext[lo:hi]), _colsum(dt)])

    def pair(rows, which):
        return pl.BlockSpec((rows, 2 * tc), lambda j, i: (which(i), j))

    return pl.pallas_call(
        body,
        name="ffn_act_bwd",
        out_shape=(jax.ShapeDtypeStruct((s, f2), BF16), jax.ShapeDtypeStruct((SUBLANES, f2), F32)),
        grid=(nj, ni),
        in_specs=[
            pl.BlockSpec((ts, tc), lambda j, i: (i, j)),
            pl.BlockSpec((SUBLANES, tc), lambda j, i: (nxt(i), j)),
            pair(ts, lambda i: i), pair(SUBLANES, prev), pair(SUBLANES, nxt),
            pair(CONV_K, lambda i: 0), pair(1, lambda i: 0),
        ],
        out_specs=[pair(ts, lambda i: i), pair(SUBLANES, lambda i: 0)],
        compiler_params=_cp("parallel", "arbitrary"),
    )(dact, dact, up, up, up, cw, cb)


ATT_SCALE = 1.0 / math.sqrt(NOPE + ROPE)
LOG2E = math.log2(math.e)
ATT_C2 = ATT_SCALE * LOG2E
STAT_SPLIT = 64
NT = (((1,), (1,)), ((), ()))
TN = (((0,), (0,)), ((), ()))


def _head_cat(q, kv, kr, tabs, n_heads):
    s, w2 = q.shape
    w = w2 // 2
    ts = _tile(s, 512, SUBLANES)
    hd = NOPE + HEAD_PAD

    def body(q_ref, kv_ref, kr_ref, c_ref, sa_ref, sb_ref, qc_ref, kc_ref):
        qv = q_ref[...]
        qr = _rope(qv[:, w:], c_ref[...], sa_ref[...], sb_ref[...]).astype(BF16)
        krv = kr_ref[...]
        for h in range(n_heads):
            qc_ref[:, h * hd : h * hd + NOPE] = qv[:, h * NOPE : (h + 1) * NOPE].astype(BF16)
            qc_ref[:, h * hd + NOPE : (h + 1) * hd] = qr[:, h * HEAD_PAD : (h + 1) * HEAD_PAD]
            kc_ref[:, h * hd : h * hd + NOPE] = kv_ref[:, h * NOPE : (h + 1) * NOPE]
            kc_ref[:, h * hd + NOPE : (h + 1) * hd] = krv

    out = jax.ShapeDtypeStruct((s, n_heads * hd), BF16)
    return pl.pallas_call(
        body,
        name="head_cat",
        out_shape=(out, out),
        grid=(s // ts,),
        in_specs=[_rows(ts, w2), _rows(ts, w), _rows(ts, HEAD_PAD)] + [_rows(ts, LANES)] * 3,
        out_specs=[_rows(ts, n_heads * hd)] * 2,
        compiler_params=_cp("parallel"),
    )(q, kv, kr, *tabs)


def _attn_fwd(qc, kc, kv, n_heads, cat_cols):
    s = qc.shape[0]
    t = _tile(s, ATT_FWD_BLOCK, LANES)
    sub = _tile(t, ATT_FWD_SUB, LANES)
    hh = n_heads
    hd = NOPE + HEAD_PAD

    def body(q_ref, k_ref, v_ref, o_ref, lse_ref, m_s, l_s, acc_s):
        i = pl.program_id(1)
        m_s[...] = jnp.full(m_s.shape, NEG, F32)
        l_s[...] = jnp.zeros(l_s.shape, F32)
        acc_s[...] = jnp.zeros(acc_s.shape, F32)

        def chunk(k0, diag):
            m_all, l_all, acc_all = m_s[...], l_s[...], acc_s[...]
            new_m, new_l, new_acc = [], [], []

            def scores(r0):
                ncol = r0 + sub if diag else t
                return lax.dot_general(q_ref[pl.ds(r0, sub), :], k_ref[pl.ds(k0, ncol), :], NT, preferred_element_type=F32)

            sc_next = scores(0)
            for r0 in range(0, t, sub):
                ncol = r0 + sub if diag else t
                sc = sc_next
                if r0 + sub < t:
                    sc_next = scores(r0 + sub)
                if diag:
                    row = lax.broadcasted_iota(jnp.int32, sc.shape, 0) + r0
                    col = lax.broadcasted_iota(jnp.int32, sc.shape, 1)
                    sc = jnp.where(col <= row, sc, NEG)
                m_prev = m_all[r0 : r0 + sub]
                m_new = jnp.maximum(m_prev, jnp.max(sc, axis=1, keepdims=True))
                alpha = jnp.exp2((m_prev - m_new) * ATT_C2)
                p = jnp.exp2((sc - m_new) * ATT_C2)
                pv = jnp.dot(p.astype(BF16), v_ref[pl.ds(k0, ncol), :], preferred_element_type=F32)
                new_m.append(m_new)
                new_l.append(alpha * l_all[r0 : r0 + sub] + jnp.sum(p, axis=1, keepdims=True))
                new_acc.append(alpha * acc_all[r0 : r0 + sub] + pv)
            m_s[...] = jnp.concatenate(new_m, axis=0)
            l_s[...] = jnp.concatenate(new_l, axis=0)
            acc_s[...] = jnp.concatenate(new_acc, axis=0)

        def loop_body(k, carry):
            chunk(pl.multiple_of(k * t, t), False)
            return carry

        lax.fori_loop(0, i, loop_body, 0)
        chunk(pl.multiple_of(i * t, t), True)
        l = l_s[...]
        o_ref[...] = (acc_s[...] / l).astype(BF16)
        lse_ref[...] = jnp.broadcast_to(m_s[...] * ATT_C2 + jnp.log(l) * LOG2E, lse_ref.shape)

    return pl.pallas_call(
        body,
        name="attn_fwd",
        out_shape=(jax.ShapeDtypeStruct((s, cat_cols), BF16), jax.ShapeDtypeStruct((s, hh * LANES), F32)),
        grid=(hh, s // t),
        in_specs=[
            pl.BlockSpec((t, hd), lambda h, i: (i, h)),
            pl.BlockSpec((s, hd), lambda h, i: (0, h)),
            pl.BlockSpec((s, VDIM), lambda h, i: (0, hh + h)),
        ],
        out_specs=[pl.BlockSpec((t, VDIM), lambda h, i: (i, h)), pl.BlockSpec((t, LANES), lambda h, i: (i, h))],
        scratch_shapes=[pltpu.VMEM((t, 1), F32), pltpu.VMEM((t, 1), F32), pltpu.VMEM((t, VDIM), F32)],
        compiler_params=_cp("parallel", "parallel"),
    )(qc, kc, kv)


def _attn_bwd_prep(cat, dcat, lse2, n_heads):
    s, w = lse2.shape
    ts = _tile(s, 512, SUBLANES)

    def body(o_ref, do_ref, lse_ref, dob_ref, st_ref):
        do = do_ref[...]
        dob_ref[...] = do.astype(BF16)
        prod = do * o_ref[...].astype(F32)
        lane = lax.broadcasted_iota(jnp.int32, (ts, LANES), 1)
        for h in range(n_heads):
            cols = slice(h * LANES, (h + 1) * LANES)
            dsum = jnp.sum(prod[:, cols], axis=1, keepdims=True)
            st_ref[:, cols] = jnp.where(lane < STAT_SPLIT, lse_ref[:, cols], dsum)

    return pl.pallas_call(
        body,
        name="attn_bwd_prep",
        out_shape=(jax.ShapeDtypeStruct((s, w), BF16), jax.ShapeDtypeStruct((s, w), F32)),
        grid=(s // ts,),
        in_specs=[_rows(ts, w)] * 3,
        out_specs=[_rows(ts, w)] * 2,
        compiler_params=_cp("parallel"),
    )(cat, dcat, lse2)


def _attn_bwd(qc, kc, kv, dob, stats, n_heads):
    s = qc.shape[0]
    t = _tile(s, ATT_BWD_BLOCK, LANES)
    sub = _tile(t, ATT_BWD_SUB, LANES)
    nb = s // t
    hh = n_heads
    hd = NOPE + HEAD_PAD
    w = hh * LANES

    def body(q_ref, k_ref, v_ref, do_ref, st_ref, dq_ref, dkn_ref, dv_ref, dkr_ref, dk_s, dv_s):
        j = pl.program_id(1)

        @pl.when(j == 0)
        def _():
            dq_ref[...] = jnp.zeros(dq_ref.shape, F32)

        dk_s[...] = jnp.zeros(dk_s.shape, F32)
        dv_s[...] = jnp.zeros(dv_s.shape, F32)

        def pair(i0, diag):
            def width(r0):
                return r0 + sub if diag else t

            def products(r0):
                rows = pl.ds(i0 + r0, sub)
                sc_ = lax.dot_general(q_ref[rows, :], k_ref[0 : width(r0), :], NT, preferred_element_type=F32)
                dp_ = lax.dot_general(do_ref[rows, :], v_ref[0 : width(r0), :], NT, preferred_element_type=F32)
                return sc_, dp_

            nxt = products(0)
            for r0 in range(0, t, sub):
                ncol = width(r0)
                rows = pl.ds(i0 + r0, sub)
                kk = k_ref[0:ncol, :]
                qq, do, st = q_ref[rows, :], do_ref[rows, :], st_ref[rows, :]
                sc, dp = nxt
                if r0 + sub < t:
                    nxt = products(r0 + sub)
                if diag:
                    row = lax.broadcasted_iota(jnp.int32, sc.shape, 0) + r0
                    col = lax.broadcasted_iota(jnp.int32, sc.shape, 1)
                    sc = jnp.where(col <= row, sc, NEG)
                p = jnp.exp2(sc * ATT_C2 - st[:, 0:1])
                dv_s[0:ncol, :] += lax.dot_general(p.astype(BF16), do, TN, preferred_element_type=F32)
                ds = (p * (dp - st[:, STAT_SPLIT : STAT_SPLIT + 1]) * ATT_SCALE).astype(BF16)
                dk_s[0:ncol, :] += lax.dot_general(ds, qq, TN, preferred_element_type=F32)
                dq_ref[rows, :] += jnp.dot(ds, kk, preferred_element_type=F32)

        pair(pl.multiple_of(j * t, t), True)

        def loop_body(i, carry):
            pair(pl.multiple_of(i * t, t), False)
            return carry

        lax.fori_loop(j + 1, nb, loop_body, 0)
        dkn_ref[...] = dk_s[:, :NOPE].astype(BF16)
        dv_ref[...] = dv_s[...].astype(BF16)
        dkr_ref[...] = dk_s[:, NOPE:]

    whole = lambda width, off: pl.BlockSpec((s, width), lambda h, j: (0, off + h))
    blk = lambda width, off: pl.BlockSpec((t, width), lambda h, j: (j, off + h))
    return pl.pallas_call(
        body,
        name="attn_bwd",
        out_shape=(
            jax.ShapeDtypeStruct((s, hh * hd), F32),
            jax.ShapeDtypeStruct((s, w), BF16),
            jax.ShapeDtypeStruct((s, w), BF16),
            jax.ShapeDtypeStruct((s, w), F32),
        ),
        grid=(hh, nb),
        in_specs=[whole(hd, 0), blk(hd, 0), blk(VDIM, hh), whole(VDIM, 0), whole(LANES, 0)],
        out_specs=[whole(hd, 0), blk(NOPE, 0), blk(VDIM, 0), blk(HEAD_PAD, 0)],
        scratch_shapes=[pltpu.VMEM((t, hd), F32), pltpu.VMEM((t, VDIM), F32)],
        compiler_params=_cp("parallel", "arbitrary"),
    )(qc, kc, kv, dob, stats)


def _dq_unrope(dq, tabs, n_heads):
    s = dq.shape[0]
    hd = NOPE + HEAD_PAD
    w = n_heads * LANES
    ts = _tile(s, 512, SUBLANES)

    def body(d_ref, c_ref, sa_ref, sb_ref, o_ref):
        c, sa, sb = c_ref[...], sa_ref[...], sb_ref[...]
        for h in range(n_heads):
            o_ref[:, h * NOPE : (h + 1) * NOPE] = d_ref[:, h * hd : h * hd + NOPE].astype(BF16)
            rot = _rope_t(d_ref[:, h * hd + NOPE : (h + 1) * hd], c, sa, sb)
            o_ref[:, w + h * HEAD_PAD : w + (h + 1) * HEAD_PAD] = rot.astype(BF16)

    return pl.pallas_call(
        body,
        name="dq_unrope",
        out_shape=jax.ShapeDtypeStruct((s, 2 * w), BF16),
        grid=(s // ts,),
        in_specs=[_rows(ts, n_heads * hd)] + [_rows(ts, LANES)] * 3,
        out_specs=_rows(ts, 2 * w),
        compiler_params=_cp("parallel"),
    )(dq, *tabs)


def _adamw(w, m, v, grads, name):
    r, c = w.shape
    budget_rows = max(SUBLANES, (VMEM_LIMIT // 3) // (4 * c * 2 * (7 + len(grads))))
    tr = _tile(r, budget_rows, SUBLANES)
    ng = len(grads)
    c1 = 1.0 - ADAM_B1**ADAM_STEP
    c2 = 1.0 - ADAM_B2**ADAM_STEP

    def body(*refs):
        w_ref, m_ref, v_ref = refs[:3]
        g_ref, d_ref, nm_ref, nv_ref = refs[3 + ng :]
        g = refs[3][...]
        for extra in refs[4 : 3 + ng]:
            g = g + extra[...]
        mn = ADAM_B1 * m_ref[...] + (1.0 - ADAM_B1) * g
        vn = ADAM_B2 * v_ref[...] + (1.0 - ADAM_B2) * (g * g)
        g_ref[...] = g
        nm_ref[...] = mn
        nv_ref[...] = vn
        d_ref[...] = -ADAM_LR * ((mn / c1) / (jnp.sqrt(vn / c2) + ADAM_EPS) + ADAM_WD * w_ref[...])

    blk = pl.BlockSpec((tr, c), lambda i: (i, 0))
    out = jax.ShapeDtypeStruct((r, c), F32)
    return pl.pallas_call(
        body,
        name=name,
        out_shape=(out, out, out, out),
        grid=(r // tr,),
        in_specs=[blk] * (3 + ng),
        out_specs=[blk] * 4,
        compiler_params=_cp("parallel"),
    )(w, m, v, *grads)


def _ada_grad(ca_t, dm):
    d = ca_t.shape[0]
    nc = dm.shape[1]
    tn = _tile(nc, 512, LANES)

    def body(a_ref, b_ref, o_ref):
        o_ref[...] = jnp.dot(a_ref[...].astype(BF16), b_ref[...].astype(BF16), preferred_element_type=F32)

    return pl.pallas_call(
        body,
        name="ada_grad",
        out_shape=jax.ShapeDtypeStruct((d, nc), F32),
        grid=(nc // tn,),
        in_specs=[pl.BlockSpec((d, LANES), lambda j: (0, 0)), pl.BlockSpec((LANES, tn), lambda j: (0, j))],
        out_specs=pl.BlockSpec((d, tn), lambda j: (0, j)),
        compiler_params=_cp("parallel"),
    )(ca_t, dm)


def _sum_devices(g):
    n = g.shape[1]

    def body(g_ref, o_ref):
        acc = g_ref[0:SUBLANES, :]
        for dvc in range(1, N_DEV):
            acc = acc + g_ref[dvc * SUBLANES : (dvc + 1) * SUBLANES, :]
        o_ref[...] = acc

    return pl.pallas_call(
        body,
        name="sum_devices",
        out_shape=jax.ShapeDtypeStruct((SUBLANES, n), F32),
        in_specs=[pl.BlockSpec(memory_space=pltpu.VMEM)],
        out_specs=pl.BlockSpec(memory_space=pltpu.VMEM),
        compiler_params=pltpu.CompilerParams(vmem_limit_bytes=VMEM_LIMIT),
    )(g)


def _sum_chips(land, sent, name):
    _, r, c = land.shape
    tr = _tile(r, max(SUBLANES * 2, (VMEM_LIMIT // 4) // (c * 2 * (4 * N_CHIP + 4 * 2))), SUBLANES * 2)

    def body(l_ref, s_ref, o_ref):
        x, y, _ = _mesh_pos()
        me = 2 * x + y
        acc = jnp.where(me == 0, s_ref[0], l_ref[0]).astype(F32)
        for k in range(1, N_CHIP):
            acc = acc + jnp.where(me == k, s_ref[k], l_ref[k]).astype(F32)
        o_ref[...] = acc

    slots = pl.BlockSpec((N_CHIP, tr, c), lambda i: (0, i, 0))
    return pl.pallas_call(
        body,
        name=name,
        out_shape=jax.ShapeDtypeStruct((r, c), F32),
        grid=(r // tr,),
        in_specs=[slots, slots],
        out_specs=pl.BlockSpec((tr, c), lambda i: (i, 0)),
        compiler_params=_cp("parallel"),
    )(land, sent)


def _mesh_pos():
    return lax.axis_index("x"), lax.axis_index("y"), lax.axis_index("c")


def _other_chips(x, y):
    return [(1 - x, y), (x, 1 - y), (1 - x, 1 - y)]


def _all_gather8(x_shard, name):
    m_per, n = x_shard.shape

    def body(x_ref, out_ref, send_sems, recv_sems, local_sem):
        x, y, c = _mesh_pos()
        me, sibling = (x, y, c), (x, y, 1 - c)
        chips = _other_chips(x, y)

        def rows(px, py, pc):
            return out_ref.at[pl.ds((4 * px + 2 * py + pc) * m_per, m_per), :]

        def copy(k, block, to, src=None):
            return pltpu.make_async_remote_copy(
                src_ref=rows(*block) if src is None else src,
                dst_ref=rows(*block),
                send_sem=send_sems.at[k],
                recv_sem=recv_sems.at[k],
                device_id=to,
                device_id_type=MESH,
            )

        mine = pltpu.make_async_copy(x_ref, rows(*me), local_sem)
        mine.start()
        first = [copy(0, me, sibling, src=x_ref)]
        first += [copy(1 + j, me, (*chip, c), src=x_ref) for j, chip in enumerate(chips)]
        for cp in first:
            cp.start()
        passed = [copy(4 + j, (*chip, c), sibling) for j, chip in enumerate(chips)]
        for j, chip in enumerate(chips):
            copy(1 + j, (*chip, c), me).wait_recv()
            passed[j].start()
        copy(0, sibling, me).wait_recv()
        for j, chip in enumerate(chips):
            copy(4 + j, (*chip, 1 - c), me).wait_recv()
        for cp in first + passed:
            cp.wait_send()
        mine.wait()

    return pl.pallas_call(
        body,
        name=name,
        out_shape=jax.ShapeDtypeStruct((N_DEV * m_per, n), x_shard.dtype),
        in_specs=[pl.BlockSpec(memory_space=pltpu.VMEM)],
        out_specs=pl.BlockSpec(memory_space=pltpu.VMEM),
        scratch_shapes=[pltpu.SemaphoreType.DMA((7,)), pltpu.SemaphoreType.DMA((7,)), pltpu.SemaphoreType.DMA],
        compiler_params=pltpu.CompilerParams(vmem_limit_bytes=VMEM_LIMIT),
    )(x_shard)


HBM_SPEC = pl.BlockSpec(memory_space=pltpu.HBM)
SEM_SPEC = pl.BlockSpec(memory_space=pltpu.SEMAPHORE)
DATAFLOW = pltpu.SideEffectType.DATAFLOW_SIDE_EFFECTING


def _half_rows(n_rows, c):
    return pl.ds(c * (n_rows // 2), n_rows // 2)


def _exchange_copies(ins, lands, send_sems, recv_sems, scatter, halves=False):
    x, y, c = _mesh_pos()
    me = 2 * x + y
    sends, recvs = [], []
    for t in range(len(ins)):
        rows = _half_rows(ins[t].shape[0], c) if halves else slice(None)
        for r, (px, py) in enumerate(_other_chips(x, y)):
            peer = 2 * px + py

            def copy(src, dst, k=3 * t + r, to=(px, py, c)):
                return pltpu.make_async_remote_copy(
                    src_ref=src, dst_ref=dst, send_sem=send_sems.at[k], recv_sem=recv_sems.at[k], device_id=to, device_id_type=MESH
                )

            if scatter:
                sends.append(copy(ins[t].at[peer], lands[t].at[me]))
                recvs.append(copy(ins[t].at[me], lands[t].at[peer]))
            else:
                sends.append(copy(ins[t].at[rows], lands[t].at[me, rows]))
                recvs.append(copy(ins[t].at[rows], lands[t].at[peer, rows]))
    return sends, recvs


def _sibling_fill(lands, name):
    nt = len(lands)

    def body(*refs):
        outs, send_sems, recv_sems = refs[nt : 2 * nt], refs[2 * nt], refs[2 * nt + 1]
        x, y, c = _mesh_pos()
        sends, recvs = [], []
        for t in range(nt):
            mine, theirs = _half_rows(outs[t].shape[1], c), _half_rows(outs[t].shape[1], 1 - c)
            for r, (px, py) in enumerate(_other_chips(x, y)):
                slot = 2 * px + py

                def copy(rows, k=3 * t + r, zone=outs[t], slot=slot):
                    part = zone.at[slot, rows]
                    return pltpu.make_async_remote_copy(
                        src_ref=part, dst_ref=part, send_sem=send_sems.at[k], recv_sem=recv_sems.at[k],
                        device_id=(x, y, 1 - c), device_id_type=MESH,
                    )

                sends.append(copy(mine))
                recvs.append(copy(theirs))
        for cp in sends:
            cp.start()
        for cp in recvs:
            cp.wait_recv()
        for cp in sends:
            cp.wait_send()

    return pl.pallas_call(
        body,
        name=name,
        out_shape=tuple(jax.ShapeDtypeStruct(a.shape, a.dtype) for a in lands),
        in_specs=[pl.BlockSpec(memory_space=pl.ANY)] * nt,
        out_specs=[pl.BlockSpec(memory_space=pl.ANY)] * nt,
        input_output_aliases={t: t for t in range(nt)},
        scratch_shapes=[pltpu.SemaphoreType.DMA((3 * nt,)), pltpu.SemaphoreType.DMA((3 * nt,))],
    )(*lands)


def _exchange_start(arrs, scatter, name, halves=False):
    nt = len(arrs)
    lands = [lax.empty(a.shape if scatter else (N_CHIP, *a.shape), a.dtype) for a in arrs]

    def body(*refs):
        ins, zones = refs[:nt], refs[nt : 2 * nt]
        send_sems, recv_sems, token = refs[2 * nt], refs[2 * nt + 1], refs[-1]
        sends, _ = _exchange_copies(ins, zones, send_sems, recv_sems, scatter, halves)
        for cp in sends:
            cp.start()
        token[...] = jnp.zeros(token.shape, F32)

    bufs = list(arrs) + list(lands)
    return pl.pallas_call(
        body,
        name=name,
        out_shape=(
            pltpu.SemaphoreType.DMA((3 * nt,)),
            pltpu.SemaphoreType.DMA((3 * nt,)),
            *[pltpu.HBM(a.shape, a.dtype) for a in bufs],
            jax.ShapeDtypeStruct((SUBLANES, LANES), F32),
        ),
        in_specs=[HBM_SPEC] * (2 * nt),
        out_specs=(SEM_SPEC, SEM_SPEC, *[HBM_SPEC] * (2 * nt), pl.BlockSpec(memory_space=pltpu.VMEM)),
        input_output_aliases={k: 2 + k for k in range(2 * nt)},
        compiler_params=pltpu.CompilerParams(has_side_effects=DATAFLOW),
    )(*[pltpu.with_memory_space_constraint(a, pltpu.HBM) for a in bufs])


def _exchange_wait(state, after, scatter, name, halves=False):
    send_sems, recv_sems, *bufs = state[:-1]
    nt = len(bufs) // 2
    afters = list(after) if isinstance(after, (list, tuple)) else [after]

    def body(*refs):
        ins, zones = refs[:nt], refs[nt : 2 * nt]
        sends, recvs = _exchange_copies(ins, zones, refs[2 * nt], refs[2 * nt + 1], scatter, halves)
        for cp in sends:
            cp.wait_send()
        for cp in recvs:
            cp.wait_recv()

    out = pl.pallas_call(
        body,
        name=name,
        out_shape=tuple(pltpu.HBM(a.shape, a.dtype) for a in bufs),
        in_specs=[HBM_SPEC] * (2 * nt) + [SEM_SPEC, SEM_SPEC] + [pl.BlockSpec(memory_space=pl.ANY)] * len(afters),
        out_specs=[HBM_SPEC] * (2 * nt),
        input_output_aliases={k: k for k in range(2 * nt)},
        compiler_params=pltpu.CompilerParams(has_side_effects=DATAFLOW),
    )(*bufs, send_sems, recv_sems, *afters)
    return list(out[:nt]), list(out[nt:])


def _swap_copies(ins, lands, send_sems, recv_sems):
    x, y, c = _mesh_pos()
    return [
        pltpu.make_async_remote_copy(
            src_ref=ins[t], dst_ref=lands[t], send_sem=send_sems.at[t], recv_sem=recv_sems.at[t],
            device_id=(x, y, 1 - c), device_id_type=MESH,
        )
        for t in range(len(ins))
    ]


def _swap_start(arrs, name):
    nt = len(arrs)
    lands = [lax.empty(a.shape, a.dtype) for a in arrs]

    def body(*refs):
        ins, zones = refs[:nt], refs[nt : 2 * nt]
        send_sems, recv_sems, token = refs[2 * nt], refs[2 * nt + 1], refs[-1]
        for cp in _swap_copies(ins, zones, send_sems, recv_sems):
            cp.start()
        token[...] = jnp.zeros(token.shape, F32)

    bufs = list(arrs) + lands
    return pl.pallas_call(
        body,
        name=name,
        out_shape=(
            pltpu.SemaphoreType.DMA((nt,)),
            pltpu.SemaphoreType.DMA((nt,)),
            *[pltpu.HBM(a.shape, a.dtype) for a in bufs],
            jax.ShapeDtypeStruct((SUBLANES, LANES), F32),
        ),
        in_specs=[HBM_SPEC] * (2 * nt),
        out_specs=(SEM_SPEC, SEM_SPEC, *[HBM_SPEC] * (2 * nt), pl.BlockSpec(memory_space=pltpu.VMEM)),
        input_output_aliases={k: 2 + k for k in range(2 * nt)},
        compiler_params=pltpu.CompilerParams(has_side_effects=DATAFLOW),
    )(*[pltpu.with_memory_space_constraint(a, pltpu.HBM) for a in bufs])


def _swap_wait(state, after, name):
    send_sems, recv_sems, *bufs = state[:-1]
    nt = len(bufs) // 2

    def body(*refs):
        cps = _swap_copies(refs[:nt], refs[nt : 2 * nt], refs[2 * nt], refs[2 * nt + 1])
        for cp in cps:
            cp.wait_send()
        for cp in cps:
            cp.wait_recv()

    out = pl.pallas_call(
        body,
        name=name,
        out_shape=tuple(pltpu.HBM(a.shape, a.dtype) for a in bufs),
        in_specs=[HBM_SPEC] * (2 * nt) + [SEM_SPEC, SEM_SPEC, pl.BlockSpec(memory_space=pl.ANY)],
        out_specs=[HBM_SPEC] * (2 * nt),
        input_output_aliases={k: k for k in range(2 * nt)},
        compiler_params=pltpu.CompilerParams(has_side_effects=DATAFLOW),
    )(*bufs, send_sems, recv_sems, after)
    return list(out[:nt]), list(out[nt:])


def _cols_from_shards(g):
    _, k, n = g.shape
    return jnp.transpose(g, (1, 0, 2)).reshape(k, N_CHIP * n)


def _cols_to_shards(a):
    k, n4 = a.shape
    return jnp.transpose(a.reshape(k, N_CHIP, n4 // N_CHIP), (1, 0, 2))


def _pad_to(vec, mult):
    n = vec.shape[0]
    return jnp.pad(vec, (0, (-n) % mult))


def kernel(x, c, positions, w_ada, b_ada, g_pre_mix, g_post_mix, w_in, g_q, w_uq, g_kv, w_ukv, conv_w_mix, conv_b_mix, w_o, g_pre_ffn, g_post_ffn, w_up, conv_w_ffn, conv_b_ffn, w_down, loss_target, m_w_ada, m_b_ada, m_g_pre_mix, m_g_post_mix, m_w_in, m_g_q, m_w_uq, m_g_kv, m_w_ukv, m_conv_w_mix, m_conv_b_mix, m_w_o, m_g_pre_ffn, m_g_post_ffn, m_w_up, m_conv_w_ffn, m_conv_b_ffn, m_w_down, v_w_ada, v_b_ada, v_g_pre_mix, v_g_post_mix, v_w_in, v_g_q, v_w_uq, v_g_kv, v_w_ukv, v_conv_w_mix, v_conv_b_mix, v_w_o, v_g_pre_ffn, v_g_post_ffn, v_w_up, v_conv_w_ffn, v_conv_b_ffn, v_w_down):
    xi, yi, ci = _mesh_pos()
    chip = 2 * xi + yi
    dev = 4 * xi + 2 * yi + ci

    s, d = x.shape[1], x.shape[2]
    ql, kl = g_q.shape[1], g_kv.shape[1]
    cwid = conv_b_mix.shape[1]
    f2 = conv_b_ffn.shape[1]
    hh = (w_uq.shape[2] * N_CHIP) // (NOPE + ROPE)
    w_att = hh * LANES
    nc_ada = w_ada.shape[2]
    lat = ql + kl + ROPE
    tc_mix = _gate_tile(cwid)
    lb = -(-(ql + kl + HEAD_PAD) // (3 * tc_mix)) * (3 * tc_mix)
    np_cols = lb + 3 * cwid
    assert cwid == hh * VDIM and w_att % tc_mix == 0

    x0 = x.reshape(s, d)
    tgt = loss_target.reshape(s, d)

    anchors = []

    def _behind(val, state):
        val, tok = lax.optimization_barrier((val, state[-1]))
        anchors.append(tok[0, 0])
        return val

    cwm_n, cwf_n = CONV_K * cwid // N_CHIP, CONV_K * f2 // N_CHIP
    pack_a = _pad_to(jnp.concatenate([c.reshape(-1), conv_w_mix.reshape(-1), conv_w_ffn.reshape(-1)]), SUBLANES * LANES)
    rows_a = _all_gather8(pack_a.reshape(SUBLANES, -1), "ag8_inputs").reshape(N_DEV, -1)
    c_all = rows_a[:, :d]
    south = rows_a[0::2]
    cw_mix = jnp.concatenate([south[j, d : d + cwm_n].reshape(CONV_K, -1) for j in range(N_CHIP)], axis=1)
    cw_ffn = jnp.concatenate([south[j, d + cwm_n : d + cwm_n + cwf_n].reshape(CONV_K, -1) for j in range(N_CHIP)], axis=1)

    b_cols = lax.dynamic_slice(b_ada, (0, chip * nc_ada), (1, nc_ada))
    mod_part, c_act = _ada_fwd(c_all, w_ada[0], b_cols)
    mod_rows = _all_gather8(mod_part, "ag8_mod")
    mod = jnp.concatenate(
        [lax.dynamic_slice_in_dim(mod_rows, 2 * N_DEV * j + dev, 1, axis=0) for j in range(N_CHIP)], axis=1
    )

    shards = [a[0].astype(BF16) for a in (w_in, w_uq, w_ukv, w_o, w_up, w_down)]
    first, mod = lax.optimization_barrier((shards[:3], mod))
    ag_a = _exchange_start(first, False, "ag_a_start", halves=True)
    mod = _behind(mod, ag_a)
    sh_m, sc_m, gt_m, sh_f, sc_f, gt_f = [mod[:, k * d : (k + 1) * d] for k in range(N_MOD)]

    inv_freq = 1.0 / (ROPE_THETA ** (jnp.arange(0, ROPE, 2, dtype=F32) / ROPE))
    invf = jnp.concatenate([inv_freq, inv_freq, jnp.zeros((LANES - ROPE,), F32)]).reshape(1, LANES)
    tabs = _rope_tables(positions.astype(F32).reshape(s, 1), invf)
    h1 = _pre_fwd(x0, g_pre_mix, sc_m, sh_m)

    def with_own(landed, own):
        return [lax.dynamic_update_slice_in_dim(g, a[None], chip, axis=0) for g, a in zip(landed, own)]

    own_w, landed_w = _exchange_wait(ag_a, [h1, tabs[0]], False, "ag_a_wait", halves=True)
    landed_w = list(_sibling_fill(landed_w, "ag_a_fill"))
    rest, landed_w = lax.optimization_barrier((shards[3:], landed_w))
    ag_b = _exchange_start(rest, False, "ag_b_start")
    h1 = _behind(h1, ag_b)
    g_in, g_uq, g_ukv = with_own(landed_w, own_w)
    full_in = _cols_from_shards(g_in)
    gate_cols = [(lat + k * cwid + j * tc_mix, tc_mix) for j in range(cwid // tc_mix) for k in range(3)]
    w_in_p = jnp.concatenate(
        [full_in[:, :lat], jnp.zeros((d, lb - lat), BF16)] + [full_in[:, o : o + n] for o, n in gate_cols], axis=1
    )
    full_uq = _cols_from_shards(g_uq).reshape(ql, hh, NOPE + ROPE)
    w_uq_p = jnp.concatenate(
        [
            full_uq[:, :, :NOPE].reshape(ql, w_att),
            jnp.pad(full_uq[:, :, NOPE:], ((0, 0), (0, 0), (0, HEAD_PAD - ROPE))).reshape(ql, w_att),
        ],
        axis=1,
    )
    full_ukv = _cols_from_shards(g_ukv).reshape(kl, hh, NOPE + VDIM)
    w_ukv_p = jnp.concatenate([full_ukv[:, :, :NOPE].reshape(kl, w_att), full_ukv[:, :, NOPE:].reshape(kl, w_att)], axis=1)

    proj = _matmul(h1, w_in_p, out_dtype=F32, tm=1024, tn=768, tk=2048, name="mm_proj")
    qn, kvn, kr = _latent_fwd(proj, g_q, g_kv, tabs, lb)
    q_f = _matmul(qn, w_uq_p, out_dtype=F32, tm=1024, tn=1024, tk=2048, name="mm_q")
    kv_p = _matmul(kvn, w_ukv_p, out_dtype=BF16, tm=1024, tn=1024, tk=2048, name="mm_kv")
    q_c, k_c = _head_cat(q_f, kv_p, kr, tabs, hh)
    cat, lse2 = _attn_fwd(q_c, k_c, kv_p, hh, w_att + cwid)
    cat = _mixer_fwd(cat, proj, cw_mix, conv_b_mix, lb, w_att)
    own_w, landed_w = _exchange_wait(ag_b, cat, False, "ag_b_wait")
    g_o, g_up, g_down = with_own(landed_w, own_w)
    w_o_f = g_o.reshape(-1, d)
    cw_ffn_p, cb_ffn_p = _pair_cols(cw_ffn), _pair_cols(conv_b_ffn)
    tcp, pair_perm = _pair_tile(f2 // 2), _pair_perm(f2 // 2)
    w_down_f = g_down.reshape(-1, d)
    mix = _matmul(cat, w_o_f, out_dtype=F32, tm=1024, tn=1024, tk=2048, name="mm_mix")

    x1, h2 = _mid_fwd(x0, mix, g_post_mix, gt_m, g_pre_ffn, sc_f, sh_f)
    up = _matmul(h2, g_up, out_dtype=F32, tm=1024, tn=tcp, tk=2048, name="mm_up", b_n_perm=pair_perm, b_col_shards=True)
    act = _ffn_act_fwd(up, cw_ffn_p, cb_ffn_p)
    y = _matmul(act, w_down_f, out_dtype=F32, tm=512, tn=1024, tk=5632, name="mm_down")
    dx2, dy, s_fin = _final(x1, y, tgt, g_post_ffn, gt_f)

    dw_down = _matmul(act, dy, ta=True, out_dtype=BF16, tm=512, tn=2048, tk=4096, name="mm_dw_down")
    dact = _matmul(dy, w_down_f, tb=True, out_dtype=F32, tm=1024, tn=1408, tk=2048, name="mm_dact")
    dup, s_ffn_p = _ffn_act_bwd(dact, up, cw_ffn_p, cb_ffn_p)
    s_ffn = _unpair_cols(s_ffn_p)
    dw_up = _matmul(
        h2, dup, ta=True, out_dtype=BF16, tm=512, tn=tcp, tk=4096, name="mm_dw_up", out_n_perm=pair_perm, out_col_shards=True
    )
    dh2 = _matmul_pair_k(dup, g_up, out_dtype=F32, tm=512, tn=1024, pairs=2, name="mm_dh2")
    dx1, dmix, s_mid = _mid_bwd(dh2, dx2, x1, mix, g_pre_ffn, sc_f, g_post_mix, gt_m)

    dw_o = _matmul(cat, dmix, ta=True, out_dtype=BF16, tm=512, tn=1024, tk=4096, name="mm_dw_o")
    send_b = [dw_o.reshape(N_CHIP, -1, d), dw_up, dw_down.reshape(N_CHIP, -1, d)]
    rs_b = _exchange_start(send_b, True, "rs_b_start")
    dmix = _behind(dmix, rs_b)
    dcat = _matmul(dmix, w_o_f, tb=True, out_dtype=F32, tm=1024, tn=1024, tk=2048, name="mm_dcat")
    dproj, s_mix = _mixer_bwd(dcat, proj, cw_mix, conv_b_mix, lb, w_att)
    dob, stats = _attn_bwd_prep(cat, dcat, lse2, hh)
    dq_raw, dkv_k, dkv_v, dkr_h = _attn_bwd(q_c, k_c, kv_p, dob, stats, hh)
    dkv_p = jnp.concatenate([dkv_k, dkv_v], axis=1)
    dq_p = _dq_unrope(dq_raw, tabs, hh)
    dw_uq_p = _matmul(qn, dq_p, ta=True, out_dtype=BF16, tm=1024, tn=1024, tk=1024, name="mm_dw_uq")
    dqn = _matmul(dq_p, w_uq_p, tb=True, out_dtype=F32, tm=1024, tn=1024, tk=2048, name="mm_dqn")
    dw_ukv_p = _matmul(kvn, dkv_p, ta=True, out_dtype=BF16, tm=1024, tn=1024, tk=1024, name="mm_dw_ukv")
    dkvn = _matmul(dkv_p, w_ukv_p, tb=True, out_dtype=F32, tm=1024, tn=1024, tk=2048, name="mm_dkvn")
    dproj, s_lat = _latent_bwd(dproj, proj, dqn, dkvn, dkr_h, g_q, g_kv, tabs, lb)
    dw_in_p = _matmul(h1, dproj, ta=True, out_dtype=BF16, tm=512, tn=1536, tk=4096, name="mm_dw_in")

    n_trip = cwid // tc_mix
    ungate = [lb + (3 * j + k) * tc_mix for k in range(3) for j in range(n_trip)]
    dw_in_f = jnp.concatenate([dw_in_p[:, :lat]] + [dw_in_p[:, o : o + tc_mix] for o in ungate], axis=1)
    uq3 = dw_uq_p.reshape(ql, 2, hh, LANES)
    dw_uq_f = jnp.concatenate([uq3[:, 0], uq3[:, 1, :, :ROPE]], axis=2).reshape(ql, hh * (NOPE + ROPE))
    ukv3 = dw_ukv_p.reshape(kl, 2, hh, LANES)
    dw_ukv_f = jnp.concatenate([ukv3[:, 0], ukv3[:, 1]], axis=2).reshape(kl, hh * (NOPE + VDIM))
    send_a = [_cols_to_shards(dw_in_f), _cols_to_shards(dw_uq_f), _cols_to_shards(dw_ukv_f)]
    rs_a = _exchange_start(send_a, True, "rs_a_start")
    dproj = _behind(dproj, rs_a)

    dh1 = _matmul(dproj, w_in_p, tb=True, out_dtype=F32, tm=512, tn=1024, tk=4608, name="mm_dh1")
    grad_x, s_first = _first_bwd(dh1, dx1, x0, g_pre_mix, sc_m)

    names = ["w_in", "w_uq", "w_ukv", "w_o", "w_up", "w_down"]
    sent_b, landed_b = _exchange_wait(rs_b, s_first, True, "rs_b_wait")
    sent_a, landed_a = _exchange_wait(rs_a, landed_b[0], True, "rs_a_wait")
    landed_a, s_first = lax.optimization_barrier((landed_a, s_first))
    part = [_sum_chips(l, a, "sum_chips_" + n) for l, a, n in zip(landed_a + landed_b, sent_a + sent_b, names)]

    dmod = jnp.concatenate([s_first[0:1], s_first[1:2], s_mid[3:4], s_mid[0:1], s_mid[1:2], s_fin[0:1]], axis=1)
    small = [
        dmod,
        s_first[2:3],
        s_mid[4:5],
        s_lat[0:1, :ql],
        s_lat[0:1, ql : ql + kl],
        s_mix[3:4],
        s_mid[2:3],
        s_fin[1:2],
        s_ffn[3:4],
        s_mix[0:3].reshape(1, -1),
        s_ffn[0:3].reshape(1, -1),
        s_fin[3:4, :LANES],
    ]
    sizes = [a.shape[1] for a in small]
    offs = [0]
    for n in sizes:
        offs.append(offs[-1] + n)
    pack_g = _pad_to(jnp.concatenate(small, axis=1).reshape(-1), SUBLANES * LANES * SUBLANES).reshape(SUBLANES, -1)
    gathered = _all_gather8(pack_g, "ag8_small_grads")
    tot = _sum_devices(gathered).reshape(-1)
    part_of = lambda k: tot[offs[k] : offs[k + 1]]
    dmod_all = gathered.reshape(N_DEV, -1)[:, : N_MOD * d]
    loss = part_of(11)[0]

    g_b_ada = part_of(0).reshape(1, -1)
    g_vecs = [part_of(k).reshape(1, -1) for k in range(1, 9)]
    g_cw_mix = lax.dynamic_slice(part_of(9).reshape(CONV_K, cwid), (0, chip * (cwid // N_CHIP)), (CONV_K, cwid // N_CHIP))
    g_cw_ffn = lax.dynamic_slice(part_of(10).reshape(CONV_K, f2), (0, chip * (f2 // N_CHIP)), (CONV_K, f2 // N_CHIP))

    swap = _swap_start(part, "swap_start")
    dm_cols = _behind(lax.dynamic_slice(dmod_all, (0, chip * nc_ada), (N_DEV, nc_ada)), swap)
    g_w_ada = _ada_grad(
        jnp.pad(c_act.T, ((0, 0), (0, LANES - N_DEV))), jnp.pad(dm_cols, ((0, LANES - N_DEV), (0, 0)))
    )
    big = {"w_ada": [a[None] for a in _adamw(w_ada[0], m_w_ada[0], v_w_ada[0], [g_w_ada], "adamw_w_ada")]}
    part, other = _swap_wait(swap, big["w_ada"][1], "swap_wait")

    big_w = [w_in, w_uq, w_ukv, w_o, w_up, w_down]
    big_m = [m_w_in, m_w_uq, m_w_ukv, m_w_o, m_w_up, m_w_down]
    big_v = [v_w_in, v_w_uq, v_w_ukv, v_w_o, v_w_up, v_w_down]
    for n, w_, m_, v_, p_, o_ in zip(names, big_w, big_m, big_v, part, other):
        big[n] = [a[None] for a in _adamw(w_[0], m_[0], v_[0], [p_, o_], "adamw_" + n)]

    sm_names = ["b_ada", "g_pre_mix", "g_post_mix", "g_q", "g_kv", "conv_b_mix", "g_pre_ffn", "g_post_ffn", "conv_b_ffn",
                "conv_w_mix", "conv_w_ffn"]
    sm_w = [b_ada, g_pre_mix, g_post_mix, g_q, g_kv, conv_b_mix, g_pre_ffn, g_post_ffn, conv_b_ffn, conv_w_mix, conv_w_ffn]
    sm_m = [m_b_ada, m_g_pre_mix, m_g_post_mix, m_g_q, m_g_kv, m_conv_b_mix, m_g_pre_ffn, m_g_post_ffn, m_conv_b_ffn,
            m_conv_w_mix, m_conv_w_ffn]
    sm_v = [v_b_ada, v_g_pre_mix, v_g_post_mix, v_g_q, v_g_kv, v_conv_b_mix, v_g_pre_ffn, v_g_post_ffn, v_conv_b_ffn,
            v_conv_w_mix, v_conv_w_ffn]
    sm_g = [g_b_ada] + g_vecs + [g_cw_mix, g_cw_ffn]
    flat = lambda arrs: jnp.concatenate([a.reshape(1, -1) for a in arrs], axis=1)
    sm_out = _adamw(flat(sm_w), flat(sm_m), flat(sm_v), [flat(sm_g)], "adamw_small")
    sm = {}
    off = 0
    for n, w_ in zip(sm_names, sm_w):
        sm[n] = [o[:, off : off + w_.size].reshape(w_.shape) for o in sm_out]
        off += w_.size

    order = ["w_ada", "b_ada", "g_pre_mix", "g_post_mix", "w_in", "g_q", "w_uq", "g_kv", "w_ukv", "conv_w_mix", "conv_b_mix",
             "w_o", "g_pre_ffn", "g_post_ffn", "w_up", "conv_w_ffn", "conv_b_ffn", "w_down"]
    res = {**big, **sm}
    outs = [loss + sum(anchors), grad_x.reshape(x.shape)]
    for k in range(4):
        outs += [res[n][k] for n in order]
    return tuple(outs)
```

```python
import math

import jax
import jax.numpy as jnp
from jax import lax
from jax.experimental import pallas as pl
from jax.experimental.pallas import tpu as pltpu

F32 = jnp.float32
BF16 = jnp.bfloat16
MESH = pl.DeviceIdType.MESH

N_DEV = 8
N_CHIP = 4
LANES = 128
SUBLANES = 8
VMEM_LIMIT = 56 * 2**20

NOPE = 128
ROPE = 64
VDIM = 128
HEAD_PAD = 128
ROPE_THETA = 10000.0
RMS_EPS = 1e-6
N_MOD = 6
CONV_K = 3
ATT_FWD_BLOCK, ATT_FWD_SUB = 2048, 256
ATT_BWD_BLOCK, ATT_BWD_SUB = 1024, 256
NEG = -1e30

ADAM_LR = 0.001
ADAM_B1 = 0.9
ADAM_B2 = 0.999
ADAM_EPS = 1e-08
ADAM_WD = 0.01
ADAM_STEP = 10


def _tile(n, pref, align):
    if n <= pref:
        return n
    t = (pref // align) * align
    while t >= align:
        if n % t == 0:
            return t
        t -= align
    return n


def _cp(*sem):
    return pltpu.CompilerParams(dimension_semantics=sem, vmem_limit_bytes=VMEM_LIMIT)


def _rsq(x):
    return lax.rsqrt(jnp.mean(x * x, axis=-1, keepdims=True) + RMS_EPS)


def _norm_bwd(dn, n, r):
    return r * (dn - n * jnp.mean(dn * n, axis=-1, keepdims=True))


def _colsum(a):
    return jnp.sum(a, axis=0, keepdims=True)


def _matmul(a, b, *, ta=False, tb=False, out_dtype, tm, tn, tk, name, b_n_perm=None, out_n_perm=None,
            b_col_shards=False, out_col_shards=False):
    assert not (b_col_shards and tb)
    if b_col_shards:
        b_rows, b_cols = b.shape[1], N_CHIP * b.shape[2]
    else:
        b_rows, b_cols = b.shape
    (k_a, m) = a.shape if ta else a.shape[::-1]
    (n, k_b) = (b_rows, b_cols) if tb else (b_cols, b_rows)
    assert k_a == k_b, (a.shape, b.shape, ta, tb)
    tm, tn, tk = _tile(m, tm, LANES), _tile(n, tn, LANES), _tile(k_a, tk, LANES)
    nk = k_a // tk
    same = lambda t: t
    bn, on = b_n_perm or same, out_n_perm or same
    a_spec = pl.BlockSpec((tk, tm), lambda i, j, k: (k, i)) if ta else pl.BlockSpec((tm, tk), lambda i, j, k: (i, k))
    if b_col_shards:
        per = (b_cols // N_CHIP) // tn
        b_spec = pl.BlockSpec((None, tk, tn), lambda i, j, k: (bn(j) // per, k, bn(j) % per))
    elif tb:
        b_spec = pl.BlockSpec((tn, tk), lambda i, j, k: (bn(j), k))
    else:
        b_spec = pl.BlockSpec((tk, tn), lambda i, j, k: (k, bn(j)))
    if out_col_shards:
        per_o = (n // N_CHIP) // tn
        out_shape = jax.ShapeDtypeStruct((N_CHIP, m, n // N_CHIP), out_dtype)
        out_spec = pl.BlockSpec((None, tm, tn), lambda i, j, k: (on(j) // per_o, i, on(j) % per_o))
    else:
        out_shape = jax.ShapeDtypeStruct((m, n), out_dtype)
        out_spec = pl.BlockSpec((tm, tn), lambda i, j, k: (i, on(j)))
    dims = (((0 if ta else 1,), (1 if tb else 0,)), ((), ()))

    def body(a_ref, b_ref, o_ref, *acc):
        p = lax.dot_general(a_ref[...].astype(BF16), b_ref[...].astype(BF16), dims, preferred_element_type=F32)
        _accumulate(p, o_ref, acc, nk)

    return pl.pallas_call(
        body,
        name=name,
        out_shape=out_shape,
        grid=(m // tm, n // tn, nk),
        in_specs=[a_spec, b_spec],
        out_specs=out_spec,
        scratch_shapes=[] if nk == 1 else [pltpu.VMEM((tm, tn), F32)],
        compiler_params=_cp("parallel", "parallel", "arbitrary"),
    )(a, b)


def _accumulate(p, o_ref, acc, nk):
    if nk == 1:
        o_ref[...] = p.astype(o_ref.dtype)
        return
    k = pl.program_id(2)

    @pl.when(k == 0)
    def _():
        acc[0][...] = p

    @pl.when(k > 0)
    def _():
        acc[0][...] += p

    @pl.when(k == nk - 1)
    def _():
        o_ref[...] = acc[0][...].astype(o_ref.dtype)


def _matmul_pair_k(a, b_shards, *, out_dtype, tm, tn, pairs, name):
    m, f2 = a.shape
    n = b_shards.shape[1]
    tc = _pair_tile(f2 // 2)
    nj = (f2 // 2) // tc
    nk = nj // pairs
    per = (f2 // N_CHIP) // tc
    tm, tn = _tile(m, tm, LANES), _tile(n, tn, LANES)

    def body(a_ref, *refs):
        w_refs, (o_ref, *acc) = refs[: 2 * pairs], refs[2 * pairs :]
        av = a_ref[...]
        p = None
        for q in range(2 * pairs):
            part = lax.dot_general(av[:, q * tc : (q + 1) * tc], w_refs[q][...], NT, preferred_element_type=F32)
            p = part if p is None else p + part
        _accumulate(p, o_ref, acc, nk)

    def w_tile(first, q):
        return pl.BlockSpec((None, tn, tc), lambda i, j, k: ((first + pairs * k + q) // per, j, (first + pairs * k + q) % per))

    w_specs = [w_tile(first, q) for q in range(pairs) for first in (0, nj)]
    return pl.pallas_call(
        body,
        name=name,
        out_shape=jax.ShapeDtypeStruct((m, n), out_dtype),
        grid=(m // tm, n // tn, nk),
        in_specs=[pl.BlockSpec((tm, 2 * tc * pairs), lambda i, j, k: (i, k))] + w_specs,
        out_specs=pl.BlockSpec((tm, tn), lambda i, j, k: (i, j)),
        scratch_shapes=[] if nk == 1 else [pltpu.VMEM((tm, tn), F32)],
        compiler_params=_cp("parallel", "parallel", "arbitrary"),
    )(a, *[b_shards] * (2 * pairs))


def _rope_tables(pos_col, invf):
    s = pos_col.shape[0]
    ts = _tile(s, 1024, SUBLANES)
    half = ROPE // 2

    def body(p_ref, f_ref, c_ref, sa_ref, sb_ref):
        ang = p_ref[...] * f_ref[...]
        lane = lax.broadcasted_iota(jnp.int32, ang.shape, 1)
        cs, sn = jnp.cos(ang), jnp.sin(ang)
        c_ref[...] = jnp.where(lane < ROPE, cs, 0.0)
        sa_ref[...] = jnp.where((lane >= half) & (lane < ROPE), sn, 0.0)
        sb_ref[...] = jnp.where(lane < half, -sn, 0.0)

    tab = jax.ShapeDtypeStruct((s, LANES), F32)
    return pl.pallas_call(
        body,
        name="rope_tables",
        out_shape=(tab, tab, tab),
        grid=(s // ts,),
        in_specs=[pl.BlockSpec((ts, 1), lambda i: (i, 0)), pl.BlockSpec((1, LANES), lambda i: (0, 0))],
        out_specs=[pl.BlockSpec((ts, LANES), lambda i: (i, 0))] * 3,
        compiler_params=_cp("parallel"),
    )(pos_col, invf)


def _widen(t, w):
    return t if w == LANES else jnp.tile(t, (1, w // LANES))


def _rope(x, c, sa, sb):
    w = x.shape[1]
    c, sa, sb = _widen(c, w), _widen(sa, w), _widen(sb, w)
    return x * c + pltpu.roll(x, ROPE // 2, 1) * sa + pltpu.roll(x, w - ROPE // 2, 1) * sb


def _rope_t(d, c, sa, sb):
    w = d.shape[1]
    c, sa, sb = _widen(c, w), _widen(sa, w), _widen(sb, w)
    return d * c + pltpu.roll(d * sa, w - ROPE // 2, 1) + pltpu.roll(d * sb, ROPE // 2, 1)


def _ada_fwd(c_all, w, b):
    d, nc = w.shape
    tn = _tile(nc, 512, LANES)

    def body(c_ref, w_ref, b_ref, o_ref, ca_ref):
        cv = c_ref[...]
        ca = cv * jax.nn.sigmoid(cv)
        ca_ref[...] = ca
        o_ref[...] = jnp.dot(ca.astype(BF16), w_ref[...].astype(BF16), preferred_element_type=F32) + b_ref[...]

    return pl.pallas_call(
        body,
        name="ada_fwd",
        out_shape=(jax.ShapeDtypeStruct((N_DEV, nc), F32), jax.ShapeDtypeStruct((N_DEV, d), F32)),
        grid=(nc // tn,),
        in_specs=[
            pl.BlockSpec((N_DEV, d), lambda j: (0, 0)),
            pl.BlockSpec((d, tn), lambda j: (0, j)),
            pl.BlockSpec((1, tn), lambda j: (0, j)),
        ],
        out_specs=[pl.BlockSpec((N_DEV, tn), lambda j: (0, j)), pl.BlockSpec((N_DEV, d), lambda j: (0, 0))],
        compiler_params=_cp("arbitrary"),
    )(c_all, w, b)


def _rows(ts, d):
    return pl.BlockSpec((ts, d), lambda i: (i, 0))


def _vec(d):
    return pl.BlockSpec((1, d), lambda i: (0, 0))


def _sums(d):
    return pl.BlockSpec((SUBLANES, d), lambda i: (0, 0))


def _acc_rows(ref, i, rows):
    @pl.when(i == 0)
    def _():
        ref[...] = jnp.zeros(ref.shape, ref.dtype)

    for k, r in enumerate(rows):
        ref[k : k + 1, :] += r


def _pre_fwd(x, g, sc, sh):
    s, d = x.shape
    ts = _tile(s, 512, SUBLANES)

    def body(x_ref, g_ref, sc_ref, sh_ref, h_ref):
        xv = x_ref[...]
        h_ref[...] = (((xv * _rsq(xv)) * g_ref[...]) * (1.0 + sc_ref[...]) + sh_ref[...]).astype(BF16)

    return pl.pallas_call(
        body,
        name="pre_mix_fwd",
        out_shape=jax.ShapeDtypeStruct((s, d), BF16),
        grid=(s // ts,),
        in_specs=[_rows(ts, d), _vec(d), _vec(d), _vec(d)],
        out_specs=_rows(ts, d),
        compiler_params=_cp("parallel"),
    )(x, g, sc, sh)


def _mid_fwd(x0, mix, g_post, gt, g_pre, sc, sh):
    s, d = x0.shape
    ts = _tile(s, 512, SUBLANES)

    def body(x_ref, m_ref, gp_ref, gt_ref, g_ref, sc_ref, sh_ref, x1_ref, h_ref):
        mv = m_ref[...]
        x1 = x_ref[...] + gt_ref[...] * ((mv * _rsq(mv)) * gp_ref[...])
        x1_ref[...] = x1
        h_ref[...] = (((x1 * _rsq(x1)) * g_ref[...]) * (1.0 + sc_ref[...]) + sh_ref[...]).astype(BF16)

    return pl.pallas_call(
        body,
        name="mid_fwd",
        out_shape=(jax.ShapeDtypeStruct((s, d), F32), jax.ShapeDtypeStruct((s, d), BF16)),
        grid=(s // ts,),
        in_specs=[_rows(ts, d), _rows(ts, d)] + [_vec(d)] * 5,
        out_specs=[_rows(ts, d), _rows(ts, d)],
        compiler_params=_cp("parallel"),
    )(x0, mix, g_post, gt, g_pre, sc, sh)


def _final(x1, y, tgt, g_post, gt):
    s, d = x1.shape
    ts = _tile(s, 256, SUBLANES)
    ni = s // ts

    def body(x_ref, y_ref, t_ref, gp_ref, gt_ref, dx_ref, dy_ref, s_ref):
        i = pl.program_id(0)
        yv, gp, gt_v = y_ref[...], gp_ref[...], gt_ref[...]
        r = _rsq(yv)
        n = yv * r
        err = (x_ref[...] + gt_v * (n * gp)) - t_ref[...]
        dx = err * (1.0 / d)
        dx_ref[...] = dx
        dy_ref[...] = _norm_bwd(dx * (gt_v * gp), n, r).astype(BF16)
        _acc_rows(s_ref, i, [_colsum(dx * (n * gp)), _colsum(dx * gt_v * n), _colsum(err * err)])

        @pl.when(i == ni - 1)
        def _():
            tot = jnp.sum(s_ref[2:3, :], axis=1, keepdims=True) * (0.5 / d)
            s_ref[3:4, :] = jnp.broadcast_to(tot, (1, d))

    return pl.pallas_call(
        body,
        name="final_fwd_bwd",
        out_shape=(
            jax.ShapeDtypeStruct((s, d), F32),
            jax.ShapeDtypeStruct((s, d), BF16),
            jax.ShapeDtypeStruct((SUBLANES, d), F32),
        ),
        grid=(ni,),
        in_specs=[_rows(ts, d)] * 3 + [_vec(d)] * 2,
        out_specs=[_rows(ts, d), _rows(ts, d), _sums(d)],
        compiler_params=_cp("arbitrary"),
    )(x1, y, tgt, g_post, gt)


def _mid_bwd(dh2, dx2, x1, mix, g_pre, sc, g_post, gt):
    s, d = x1.shape
    ts = _tile(s, 256, SUBLANES)

    def body(dh_ref, dx2_ref, x_ref, m_ref, g_ref, sc_ref, gp_ref, gt_ref, dx1_ref, dm_ref, s_ref):
        i = pl.program_id(0)
        dh, xv, mv = dh_ref[...], x_ref[...], m_ref[...]
        g, sc_v, gp, gt_v = g_ref[...], sc_ref[...], gp_ref[...], gt_ref[...]
        r1 = _rsq(xv)
        n1 = xv * r1
        dx1 = dx2_ref[...] + _norm_bwd(dh * (g * (1.0 + sc_v)), n1, r1)
        dx1_ref[...] = dx1
        rm = _rsq(mv)
        nm = mv * rm
        dm_ref[...] = _norm_bwd(dx1 * (gt_v * gp), nm, rm).astype(BF16)
        _acc_rows(
            s_ref,
            i,
            [
                _colsum(dh),
                _colsum(dh * (n1 * g)),
                _colsum(dh * (1.0 + sc_v) * n1),
                _colsum(dx1 * (nm * gp)),
                _colsum(dx1 * gt_v * nm),
            ],
        )

    return pl.pallas_call(
        body,
        name="mid_bwd",
        out_shape=(
            jax.ShapeDtypeStruct((s, d), F32),
            jax.ShapeDtypeStruct((s, d), BF16),
            jax.ShapeDtypeStruct((SUBLANES, d), F32),
        ),
        grid=(s // ts,),
        in_specs=[_rows(ts, d)] * 4 + [_vec(d)] * 4,
        out_specs=[_rows(ts, d), _rows(ts, d), _sums(d)],
        compiler_params=_cp("arbitrary"),
    )(dh2, dx2, x1, mix, g_pre, sc, g_post, gt)


def _first_bwd(dh1, dx1, x0, g, sc):
    s, d = x0.shape
    ts = _tile(s, 512, SUBLANES)

    def body(dh_ref, dx1_ref, x_ref, g_ref, sc_ref, dx_ref, s_ref):
        i = pl.program_id(0)
        dh, xv, gv, sc_v = dh_ref[...], x_ref[...], g_ref[...], sc_ref[...]
        r = _rsq(xv)
        n = xv * r
        dx_ref[...] = dx1_ref[...] + _norm_bwd(dh * (gv * (1.0 + sc_v)), n, r)
        _acc_rows(s_ref, i, [_colsum(dh), _colsum(dh * (n * gv)), _colsum(dh * (1.0 + sc_v) * n)])

    return pl.pallas_call(
        body,
        name="first_bwd",
        out_shape=(jax.ShapeDtypeStruct((s, d), F32), jax.ShapeDtypeStruct((SUBLANES, d), F32)),
        grid=(s // ts,),
        in_specs=[_rows(ts, d)] * 3 + [_vec(d)] * 2,
        out_specs=[_rows(ts, d), _sums(d)],
        compiler_params=_cp("arbitrary"),
    )(dh1, dx1, x0, g, sc)


def _latent_fwd(proj, g_q, g_kv, tabs, lb):
    s = proj.shape[0]
    ql, kl = g_q.shape[1], g_kv.shape[1]
    ts = _tile(s, 512, SUBLANES)

    def body(p_ref, gq_ref, gk_ref, c_ref, sa_ref, sb_ref, q_ref, kv_ref, kr_ref):
        pv = p_ref[...]
        q, kv, kr = pv[:, :ql], pv[:, ql : ql + kl], pv[:, ql + kl : ql + kl + HEAD_PAD]
        q_ref[...] = ((q * _rsq(q)) * gq_ref[...]).astype(BF16)
        kv_ref[...] = ((kv * _rsq(kv)) * gk_ref[...]).astype(BF16)
        kr_ref[...] = _rope(kr, c_ref[...], sa_ref[...], sb_ref[...]).astype(BF16)

    return pl.pallas_call(
        body,
        name="latent_fwd",
        out_shape=(
            jax.ShapeDtypeStruct((s, ql), BF16),
            jax.ShapeDtypeStruct((s, kl), BF16),
            jax.ShapeDtypeStruct((s, HEAD_PAD), BF16),
        ),
        grid=(s // ts,),
        in_specs=[_rows(ts, lb), _vec(ql), _vec(kl)] + [_rows(ts, LANES)] * 3,
        out_specs=[_rows(ts, ql), _rows(ts, kl), _rows(ts, HEAD_PAD)],
        compiler_params=_cp("parallel"),
    )(proj, g_q, g_kv, *tabs)


def _latent_bwd(dproj, proj, dqn, dkvn, dkr_h, g_q, g_kv, tabs, lb):
    s = proj.shape[0]
    ql, kl = g_q.shape[1], g_kv.shape[1]
    hw = dkr_h.shape[1]
    ts = _tile(s, 512, SUBLANES)
    pad = lb - ql - kl - HEAD_PAD

    def body(_, p_ref, dq_ref, dkv_ref, dkr_ref, gq_ref, gk_ref, c_ref, sa_ref, sb_ref, o_ref, s_ref):
        i = pl.program_id(0)
        pv = p_ref[...]
        q, kv = pv[:, :ql], pv[:, ql : ql + kl]
        dqn_v, dkvn_v = dq_ref[...], dkv_ref[...]
        rq = _rsq(q)
        nq = q * rq
        rk = _rsq(kv)
        nk = kv * rk
        dkr = dkr_ref[:, :HEAD_PAD]
        for h in range(1, hw // HEAD_PAD):
            dkr = dkr + dkr_ref[:, h * HEAD_PAD : (h + 1) * HEAD_PAD]
        parts = [
            _norm_bwd(dqn_v * gq_ref[...], nq, rq).astype(BF16),
            _norm_bwd(dkvn_v * gk_ref[...], nk, rk).astype(BF16),
            _rope_t(dkr, c_ref[...], sa_ref[...], sb_ref[...]).astype(BF16),
        ]
        if pad:
            parts.append(jnp.zeros((ts, pad), BF16))
        o_ref[...] = jnp.concatenate(parts, axis=1)
        row = [_colsum(dqn_v * nq), _colsum(dkvn_v * nk), jnp.zeros((1, lb - ql - kl), F32)]
        _acc_rows(s_ref, i, [jnp.concatenate(row, axis=1)])

    return pl.pallas_call(
        body,
        name="latent_bwd",
        out_shape=(jax.ShapeDtypeStruct(dproj.shape, BF16), jax.ShapeDtypeStruct((SUBLANES, lb), F32)),
        grid=(s // ts,),
        in_specs=[pl.BlockSpec(memory_space=pl.ANY), _rows(ts, lb), _rows(ts, ql), _rows(ts, kl), _rows(ts, hw)]
        + [_vec(ql), _vec(kl)]
        + [_rows(ts, LANES)] * 3,
        out_specs=[_rows(ts, lb), _sums(lb)],
        input_output_aliases={0: 0},
        compiler_params=_cp("arbitrary"),
    )(dproj, proj, dqn, dkvn, dkr_h, g_q, g_kv, *tabs)


def _conv3(ext, w, b):
    return (pltpu.roll(ext, 2, 0) * w[0:1] + pltpu.roll(ext, 1, 0) * w[1:2]) + ext * w[2:3] + b


def _conv3_t(du, w):
    n = du.shape[0]
    return du * w[2:3] + pltpu.roll(du, n - 1, 0) * w[1:2] + pltpu.roll(du, n - 2, 0) * w[0:1]


def _halo_maps(ts, s):
    r8, last = ts // SUBLANES, s // SUBLANES - 1
    prev = lambda i: jnp.maximum(i * r8 - 1, 0)
    nxt = lambda i: jnp.minimum((i + 1) * r8, last)
    return prev, nxt


def _gate_tile(cwid):
    return _tile(cwid, 512, LANES)


def _mixer_fwd(cat, proj, cw, cb, lb, col0):
    s = proj.shape[0]
    cwid = cw.shape[1]
    ts = _tile(s, 512, SUBLANES)
    tc = _gate_tile(cwid)
    assert lb % (3 * tc) == 0 and col0 % tc == 0
    t0, oc = lb // (3 * tc), col0 // tc
    prev, _ = _halo_maps(ts, s)

    def body(_, g_ref, p_ref, w_ref, b_ref, o_ref):
        keep = jnp.where(pl.program_id(1) > 0, 1.0, 0.0)
        gv, pv = g_ref[...], p_ref[...]
        ext = jnp.concatenate([pv[:, tc : 2 * tc] * pv[:, 2 * tc :] * keep, gv[:, tc : 2 * tc] * gv[:, 2 * tc :]], axis=0)
        o_ref[...] = (gv[:, :tc] * _conv3(ext, w_ref[...], b_ref[...])[SUBLANES:]).astype(BF16)

    return pl.pallas_call(
        body,
        name="mixer_fwd",
        out_shape=jax.ShapeDtypeStruct(cat.shape, BF16),
        grid=(cwid // tc, s // ts),
        in_specs=[
            pl.BlockSpec(memory_space=pl.ANY),
            pl.BlockSpec((ts, 3 * tc), lambda j, i: (i, t0 + j)),
            pl.BlockSpec((SUBLANES, 3 * tc), lambda j, i: (prev(i), t0 + j)),
            pl.BlockSpec((CONV_K, tc), lambda j, i: (0, j)),
            pl.BlockSpec((1, tc), lambda j, i: (0, j)),
        ],
        out_specs=pl.BlockSpec((ts, tc), lambda j, i: (i, oc + j)),
        input_output_aliases={0: 0},
        compiler_params=_cp("parallel", "arbitrary"),
    )(cat, proj, proj, cw, cb)


def _mixer_bwd(dcat, proj, cw, cb, lb, col0):
    s, np_cols = proj.shape
    cwid = cw.shape[1]
    ts = _tile(s, 512, SUBLANES)
    tc = _gate_tile(cwid)
    t0, oc = lb // (3 * tc), col0 // tc
    ni = s // ts
    prev, nxt = _halo_maps(ts, s)

    def body(d_ref, dn_ref, g_ref, gp_ref, gn_ref, w_ref, b_ref, dg_ref, s_ref):
        i = pl.program_id(1)
        keep_p = jnp.where(i > 0, 1.0, 0.0)
        keep_n = jnp.where(i < ni - 1, 1.0, 0.0)
        w = w_ref[...]
        gv, gp, gn = g_ref[...], gp_ref[...], gn_ref[...]
        gc = jnp.concatenate([gp[:, tc : 2 * tc], gv[:, tc : 2 * tc], gn[:, tc : 2 * tc]], axis=0)
        ci = jnp.concatenate([gp[:, 2 * tc :] * keep_p, gv[:, 2 * tc :], gn[:, 2 * tc :]], axis=0)
        u = gc * ci
        cv = _conv3(u, w, b_ref[...])[SUBLANES:]
        dco = jnp.concatenate([d_ref[...], dn_ref[...] * keep_n], axis=0)
        gb = jnp.concatenate([gv[:, :tc], gn[:, :tc]], axis=0)
        dcv = dco * gb
        du = _conv3_t(dcv, w)[:ts]
        dg_ref[:, :tc] = (dco * cv)[:ts].astype(BF16)
        dg_ref[:, tc : 2 * tc] = (du * gv[:, 2 * tc :]).astype(BF16)
        dg_ref[:, 2 * tc :] = (du * gv[:, tc : 2 * tc]).astype(BF16)
        dt = dcv[:ts]
        u1, u2 = pltpu.roll(u, 1, 0), pltpu.roll(u, 2, 0)
        lo, hi = SUBLANES, SUBLANES + ts
        _acc_rows(s_ref, i, [_colsum(dt * u2[lo:hi]), _colsum(dt * u1[lo:hi]), _colsum(dt * u[lo:hi]), _colsum(dt)])

    def triple(rows, which):
        return pl.BlockSpec((rows, 3 * tc), lambda j, i: (which(i), t0 + j))

    return pl.pallas_call(
        body,
        name="mixer_bwd",
        out_shape=(jax.ShapeDtypeStruct((s, np_cols), BF16), jax.ShapeDtypeStruct((SUBLANES, cwid), F32)),
        grid=(cwid // tc, ni),
        in_specs=[
            pl.BlockSpec((ts, tc), lambda j, i: (i, oc + j)),
            pl.BlockSpec((SUBLANES, tc), lambda j, i: (nxt(i), oc + j)),
            triple(ts, lambda i: i), triple(SUBLANES, prev), triple(SUBLANES, nxt),
            pl.BlockSpec((CONV_K, tc), lambda j, i: (0, j)),
            pl.BlockSpec((1, tc), lambda j, i: (0, j)),
        ],
        out_specs=[triple(ts, lambda i: i), pl.BlockSpec((SUBLANES, tc), lambda j, i: (0, j))],
        compiler_params=_cp("parallel", "arbitrary"),
    )(dcat, dcat, proj, proj, proj, cw, cb)


def _pair_tile(f):
    return _tile(f, 1408, LANES)


def _pair_perm(f):
    nj = f // _pair_tile(f)
    return lambda p: (p % 2) * nj + p // 2


def _pair_cols(a):
    r, f2 = a.shape
    tc = _pair_tile(f2 // 2)
    return a.reshape(r, 2, f2 // (2 * tc), tc).transpose(0, 2, 1, 3).reshape(r, f2)


def _unpair_cols(a):
    r, f2 = a.shape
    tc = _pair_tile(f2 // 2)
    return a.reshape(r, f2 // (2 * tc), 2, tc).transpose(0, 2, 1, 3).reshape(r, f2)


def _ffn_act_fwd(up, cw, cb):
    s, f2 = up.shape
    f = f2 // 2
    ts = _tile(s, 512, SUBLANES)
    tc = _pair_tile(f)
    prev, _ = _halo_maps(ts, s)

    def body(u_ref, p_ref, w_ref, b_ref, o_ref):
        keep = jnp.where(pl.program_id(1) > 0, 1.0, 0.0)
        ext = jnp.concatenate([p_ref[...] * keep, u_ref[...]], axis=0)
        u = _conv3(ext, w_ref[...], b_ref[...])[SUBLANES:]
        a, g = u[:, :tc], u[:, tc:]
        o_ref[...] = ((g * jax.nn.sigmoid(g)) * a).astype(BF16)

    def pair(rows, which):
        return pl.BlockSpec((rows, 2 * tc), lambda j, i: (which(i), j))

    return pl.pallas_call(
        body,
        name="ffn_act_fwd",
        out_shape=jax.ShapeDtypeStruct((s, f), BF16),
        grid=(f // tc, s // ts),
        in_specs=[pair(ts, lambda i: i), pair(SUBLANES, prev), pair(CONV_K, lambda i: 0), pair(1, lambda i: 0)],
        out_specs=pl.BlockSpec((ts, tc), lambda j, i: (i, j)),
        compiler_params=_cp("parallel", "arbitrary"),
    )(up, up, cw, cb)


def _ffn_act_bwd(dact, up, cw, cb):
    s, f2 = up.shape
    f = f2 // 2
    ts = _tile(s, 256, SUBLANES)
    tc = _pair_tile(f)
    nj, ni = f // tc, s // ts
    prev, nxt = _halo_maps(ts, s)

    def body(d_ref, dn_ref, u_ref, up_ref, un_ref, w_ref, b_ref, dup_ref, s_ref):
        i = pl.program_id(1)
        keep_p = jnp.where(i > 0, 1.0, 0.0)
        keep_n = jnp.where(i < ni - 1, 1.0, 0.0)
        w = w_ref[...]
        ext = jnp.concatenate([up_ref[...] * keep_p, u_ref[...], un_ref[...]], axis=0)
        u = _conv3(ext, w, b_ref[...])[SUBLANES:]
        a, g = u[:, :tc], u[:, tc:]
        dact_v = jnp.concatenate([d_ref[...], dn_ref[...] * keep_n], axis=0)
        sg = jax.nn.sigmoid(g)
        du = jnp.concatenate([dact_v * (g * sg), dact_v * a * (sg * (1.0 + g * (1.0 - sg)))], axis=1)
        dup_ref[...] = _conv3_t(du, w)[:ts].astype(BF16)
        dt = du[:ts]
        lo, hi = SUBLANES, SUBLANES + ts
        e1, e2 = pltpu.roll(ext, 1, 0), pltpu.roll(ext, 2, 0)
        _acc_rows(s_ref, i, [_colsum(dt * e2[lo:hi]), _colsum(dt * e1[lo:hi]), _colsum(dt * ext[lo:hi]), _colsum(dt)])

    def pair(rows, which):
        return pl.BlockSpec((rows, 2 * tc), lambda j, i: (which(i), j))

    return pl.pallas_call(
        body,
        name="ffn_act_bwd",
        out_shape=(jax.ShapeDtypeStruct((s, f2), BF16), jax.ShapeDtypeStruct((SUBLANES, f2), F32)),
        grid=(nj, ni),
        in_specs=[
            pl.BlockSpec((ts, tc), lambda j, i: (i, j)),
            pl.BlockSpec((SUBLANES, tc), lambda j, i: (nxt(i), j)),
            pair(ts, lambda i: i), pair(SUBLANES, prev), pair(SUBLANES, nxt),
            pair(CONV_K, lambda i: 0), pair(1, lambda i: 0),
        ],
        out_specs=[pair(ts, lambda i: i), pair(SUBLANES, lambda i: 0)],
        compiler_params=_cp("parallel", "arbitrary"),
    )(dact, dact, up, up, up, cw, cb)


ATT_SCALE = 1.0 / math.sqrt(NOPE + ROPE)
LOG2E = math.log2(math.e)
ATT_C2 = ATT_SCALE * LOG2E
STAT_SPLIT = 64
NT = (((1,), (1,)), ((), ()))
TN = (((0,), (0,)), ((), ()))


def _head_cat(q, kv, kr, tabs, n_heads):
    s, w2 = q.shape
    w = w2 // 2
    ts = _tile(s, 512, SUBLANES)
    hd = NOPE + HEAD_PAD

    def body(q_ref, kv_ref, kr_ref, c_ref, sa_ref, sb_ref, qc_ref, kc_ref):
        qv = q_ref[...]
        qr = _rope(qv[:, w:], c_ref[...], sa_ref[...], sb_ref[...]).astype(BF16)
        krv = kr_ref[...]
        for h in range(n_heads):
            qc_ref[:, h * hd : h * hd + NOPE] = qv[:, h * NOPE : (h + 1) * NOPE].astype(BF16)
            qc_ref[:, h * hd + NOPE : (h + 1) * hd] = qr[:, h * HEAD_PAD : (h + 1) * HEAD_PAD]
            kc_ref[:, h * hd : h * hd + NOPE] = kv_ref[:, h * NOPE : (h + 1) * NOPE]
            kc_ref[:, h * hd + NOPE : (h + 1) * hd] = krv

    out = jax.ShapeDtypeStruct((s, n_heads * hd), BF16)
    return pl.pallas_call(
        body,
        name="head_cat",
        out_shape=(out, out),
        grid=(s // ts,),
        in_specs=[_rows(ts, w2), _rows(ts, w), _rows(ts, HEAD_PAD)] + [_rows(ts, LANES)] * 3,
        out_specs=[_rows(ts, n_heads * hd)] * 2,
        compiler_params=_cp("parallel"),
    )(q, kv, kr, *tabs)


def _attn_fwd(qc, kc, kv, n_heads, cat_cols):
    s = qc.shape[0]
    t = _tile(s, ATT_FWD_BLOCK, LANES)
    sub = _tile(t, ATT_FWD_SUB, LANES)
    hh = n_heads
    hd = NOPE + HEAD_PAD

    def body(q_ref, k_ref, v_ref, o_ref, lse_ref, m_s, l_s, acc_s):
        i = pl.program_id(1)
        m_s[...] = jnp.full(m_s.shape, NEG, F32)
        l_s[...] = jnp.zeros(l_s.shape, F32)
        acc_s[...] = jnp.zeros(acc_s.shape, F32)

        def chunk(k0, diag):
            m_all, l_all, acc_all = m_s[...], l_s[...], acc_s[...]
            new_m, new_l, new_acc = [], [], []

            def scores(r0):
                ncol = r0 + sub if diag else t
                return lax.dot_general(q_ref[pl.ds(r0, sub), :], k_ref[pl.ds(k0, ncol), :], NT, preferred_element_type=F32)

            sc_next = scores(0)
            for r0 in range(0, t, sub):
                ncol = r0 + sub if diag else t
                sc = sc_next
                if r0 + sub < t:
                    sc_next = scores(r0 + sub)
                if diag:
                    row = lax.broadcasted_iota(jnp.int32, sc.shape, 0) + r0
                    col = lax.broadcasted_iota(jnp.int32, sc.shape, 1)
                    sc = jnp.where(col <= row, sc, NEG)
                m_prev = m_all[r0 : r0 + sub]
                m_new = jnp.maximum(m_prev, jnp.max(sc, axis=1, keepdims=True))
                alpha = jnp.exp2((m_prev - m_new) * ATT_C2)
                p = jnp.exp2((sc - m_new) * ATT_C2)
                pv = jnp.dot(p.astype(BF16), v_ref[pl.ds(k0, ncol), :], preferred_element_type=F32)
                new_m.append(m_new)
                new_l.append(alpha * l_all[r0 : r0 + sub] + jnp.sum(p, axis=1, keepdims=True))
                new_acc.append(alpha * acc_all[r0 : r0 + sub] + pv)
            m_s[...] = jnp.concatenate(new_m, axis=0)
            l_s[...] = jnp.concatenate(new_l, axis=0)
            acc_s[...] = jnp.concatenate(new_acc, axis=0)

        def loop_body(k, carry):
            chunk(pl.multiple_of(k * t, t), False)
            return carry

        lax.fori_loop(0, i, loop_body, 0)
        chunk(pl.multiple_of(i * t, t), True)
        l = l_s[...]
        o_ref[...] = (acc_s[...] / l).astype(BF16)
        lse_ref[...] = jnp.broadcast_to(m_s[...] * ATT_C2 + jnp.log(l) * LOG2E, lse_ref.shape)

    return pl.pallas_call(
        body,
        name="attn_fwd",
        out_shape=(jax.ShapeDtypeStruct((s, cat_cols), BF16), jax.ShapeDtypeStruct((s, hh * LANES), F32)),
        grid=(hh, s // t),
        in_specs=[
            pl.BlockSpec((t, hd), lambda h, i: (i, h)),
            pl.BlockSpec((s, hd), lambda h, i: (0, h)),
            pl.BlockSpec((s, VDIM), lambda h, i: (0, hh + h)),
        ],
        out_specs=[pl.BlockSpec((t, VDIM), lambda h, i: (i, h)), pl.BlockSpec((t, LANES), lambda h, i: (i, h))],
        scratch_shapes=[pltpu.VMEM((t, 1), F32), pltpu.VMEM((t, 1), F32), pltpu.VMEM((t, VDIM), F32)],
        compiler_params=_cp("parallel", "parallel"),
    )(qc, kc, kv)


def _attn_bwd_prep(cat, dcat, lse2, n_heads):
    s, w = lse2.shape
    ts = _tile(s, 512, SUBLANES)

    def body(o_ref, do_ref, lse_ref, dob_ref, st_ref):
        do = do_ref[...]
        dob_ref[...] = do.astype(BF16)
        prod = do * o_ref[...].astype(F32)
        lane = lax.broadcasted_iota(jnp.int32, (ts, LANES), 1)
        for h in range(n_heads):
            cols = slice(h * LANES, (h + 1) * LANES)
            dsum = jnp.sum(prod[:, cols], axis=1, keepdims=True)
            st_ref[:, cols] = jnp.where(lane < STAT_SPLIT, lse_ref[:, cols], dsum)

    return pl.pallas_call(
        body,
        name="attn_bwd_prep",
        out_shape=(jax.ShapeDtypeStruct((s, w), BF16), jax.ShapeDtypeStruct((s, w), F32)),
        grid=(s // ts,),
        in_specs=[_rows(ts, w)] * 3,
        out_specs=[_rows(ts, w)] * 2,
        compiler_params=_cp("parallel"),
    )(cat, dcat, lse2)


def _attn_bwd(qc, kc, kv, dob, stats, n_heads):
    s = qc.shape[0]
    t = _tile(s, ATT_BWD_BLOCK, LANES)
    sub = _tile(t, ATT_BWD_SUB, LANES)
    nb = s // t
    hh = n_heads
    hd = NOPE + HEAD_PAD
    w = hh * LANES

    def body(q_ref, k_ref, v_ref, do_ref, st_ref, dq_ref, dkn_ref, dv_ref, dkr_ref, dk_s, dv_s):
        j = pl.program_id(1)

        @pl.when(j == 0)
        def _():
            dq_ref[...] = jnp.zeros(dq_ref.shape, F32)

        dk_s[...] = jnp.zeros(dk_s.shape, F32)
        dv_s[...] = jnp.zeros(dv_s.shape, F32)

        def pair(i0, diag):
            def width(r0):
                return r0 + sub if diag else t

            def products(r0):
                rows = pl.ds(i0 + r0, sub)
                sc_ = lax.dot_general(q_ref[rows, :], k_ref[0 : width(r0), :], NT, preferred_element_type=F32)
                dp_ = lax.dot_general(do_ref[rows, :], v_ref[0 : width(r0), :], NT, preferred_element_type=F32)
                return sc_, dp_

            nxt = products(0)
            for r0 in range(0, t, sub):
                ncol = width(r0)
                rows = pl.ds(i0 + r0, sub)
                kk = k_ref[0:ncol, :]
                qq, do, st = q_ref[rows, :], do_ref[rows, :], st_ref[rows, :]
                sc, dp = nxt
                if r0 + sub < t:
                    nxt = products(r0 + sub)
                if diag:
                    row = lax.broadcasted_iota(jnp.int32, sc.shape, 0) + r0
                    col = lax.broadcasted_iota(jnp.int32, sc.shape, 1)
                    sc = jnp.where(col <= row, sc, NEG)
                p = jnp.exp2(sc * ATT_C2 - st[:, 0:1])
                dv_s[0:ncol, :] += lax.dot_general(p.astype(BF16), do, TN, preferred_element_type=F32)
                ds = (p * (dp - st[:, STAT_SPLIT : STAT_SPLIT + 1]) * ATT_SCALE).astype(BF16)
                dk_s[0:ncol, :] += lax.dot_general(ds, qq, TN, preferred_element_type=F32)
                dq_ref[rows, :] += jnp.dot(ds, kk, preferred_element_type=F32)

        pair(pl.multiple_of(j * t, t), True)

        def loop_body(i, carry):
            pair(pl.multiple_of(i * t, t), False)
            return carry

        lax.fori_loop(j + 1, nb, loop_body, 0)
        dkn_ref[...] = dk_s[:, :NOPE].astype(BF16)
        dv_ref[...] = dv_s[...].astype(BF16)
        dkr_ref[...] = dk_s[:, NOPE:]

    whole = lambda width, off: pl.BlockSpec((s, width), lambda h, j: (0, off + h))
    blk = lambda width, off: pl.BlockSpec((t, width), lambda h, j: (j, off + h))
    return pl.pallas_call(
        body,
        name="attn_bwd",
        out_shape=(
            jax.ShapeDtypeStruct((s, hh * hd), F32),
            jax.ShapeDtypeStruct((s, w), BF16),
            jax.ShapeDtypeStruct((s, w), BF16),
            jax.ShapeDtypeStruct((s, w), F32),
        ),
        grid=(hh, nb),
        in_specs=[whole(hd, 0), blk(hd, 0), blk(VDIM, hh), whole(VDIM, 0), whole(LANES, 0)],
        out_specs=[whole(hd, 0), blk(NOPE, 0), blk(VDIM, 0), blk(HEAD_PAD, 0)],
        scratch_shapes=[pltpu.VMEM((t, hd), F32), pltpu.VMEM((t, VDIM), F32)],
        compiler_params=_cp("parallel", "arbitrary"),
    )(qc, kc, kv, dob, stats)


def _dq_unrope(dq, tabs, n_heads):
    s = dq.shape[0]
    hd = NOPE + HEAD_PAD
    w = n_heads * LANES
    ts = _tile(s, 512, SUBLANES)

    def body(d_ref, c_ref, sa_ref, sb_ref, o_ref):
        c, sa, sb = c_ref[...], sa_ref[...], sb_ref[...]
        for h in range(n_heads):
            o_ref[:, h * NOPE : (h + 1) * NOPE] = d_ref[:, h * hd : h * hd + NOPE].astype(BF16)
            rot = _rope_t(d_ref[:, h * hd + NOPE : (h + 1) * hd], c, sa, sb)
            o_ref[:, w + h * HEAD_PAD : w + (h + 1) * HEAD_PAD] = rot.astype(BF16)

    return pl.pallas_call(
        body,
        name="dq_unrope",
        out_shape=jax.ShapeDtypeStruct((s, 2 * w), BF16),
        grid=(s // ts,),
        in_specs=[_rows(ts, n_heads * hd)] + [_rows(ts, LANES)] * 3,
        out_specs=_rows(ts, 2 * w),
        compiler_params=_cp("parallel"),
    )(dq, *tabs)


def _adamw(w, m, v, grads, name):
    r, c = w.shape
    budget_rows = max(SUBLANES, (VMEM_LIMIT // 3) // (4 * c * 2 * (7 + len(grads))))
    tr = _tile(r, budget_rows, SUBLANES)
    ng = len(grads)
    c1 = 1.0 - ADAM_B1**ADAM_STEP
    c2 = 1.0 - ADAM_B2**ADAM_STEP

    def body(*refs):
        w_ref, m_ref, v_ref = refs[:3]
        g_ref, d_ref, nm_ref, nv_ref = refs[3 + ng :]
        g = refs[3][...]
        for extra in refs[4 : 3 + ng]:
            g = g + extra[...]
        mn = ADAM_B1 * m_ref[...] + (1.0 - ADAM_B1) * g
        vn = ADAM_B2 * v_ref[...] + (1.0 - ADAM_B2) * (g * g)
        g_ref[...] = g
        nm_ref[...] = mn
        nv_ref[...] = vn
        d_ref[...] = -ADAM_LR * ((mn / c1) / (jnp.sqrt(vn / c2) + ADAM_EPS) + ADAM_WD * w_ref[...])

    blk = pl.BlockSpec((tr, c), lambda i: (i, 0))
    out = jax.ShapeDtypeStruct((r, c), F32)
    return pl.pallas_call(
        body,
        name=name,
        out_shape=(out, out, out, out),
        grid=(r // tr,),
        in_specs=[blk] * (3 + ng),
        out_specs=[blk] * 4,
        compiler_params=_cp("parallel"),
    )(w, m, v, *grads)


def _ada_grad(ca_t, dm):
    d = ca_t.shape[0]
    nc = dm.shape[1]
    tn = _tile(nc, 512, LANES)

    def body(a_ref, b_ref, o_ref):
        o_ref[...] = jnp.dot(a_ref[...].astype(BF16), b_ref[...].astype(BF16), preferred_element_type=F32)

    return pl.pallas_call(
        body,
        name="ada_grad",
        out_shape=jax.ShapeDtypeStruct((d, nc), F32),
        grid=(nc // tn,),
        in_specs=[pl.BlockSpec((d, LANES), lambda j: (0, 0)), pl.BlockSpec((LANES, tn), lambda j: (0, j))],
        out_specs=pl.BlockSpec((d, tn), lambda j: (0, j)),
        compiler_params=_cp("parallel"),
    )(ca_t, dm)


def _sum_devices(g):
    n = g.shape[1]

    def body(g_ref, o_ref):
        acc = g_ref[0:SUBLANES, :]
        for dvc in range(1, N_DEV):
            acc = acc + g_ref[dvc * SUBLANES : (dvc + 1) * SUBLANES, :]
        o_ref[...] = acc

    return pl.pallas_call(
        body,
        name="sum_devices",
        out_shape=jax.ShapeDtypeStruct((SUBLANES, n), F32),
        in_specs=[pl.BlockSpec(memory_space=pltpu.VMEM)],
        out_specs=pl.BlockSpec(memory_space=pltpu.VMEM),
        compiler_params=pltpu.CompilerParams(vmem_limit_bytes=VMEM_LIMIT),
    )(g)


def _sum_chips(land, sent, name):
    _, r, c = land.shape
    tr = _tile(r, max(SUBLANES * 2, (VMEM_LIMIT // 4) // (c * 2 * (4 * N_CHIP + 4 * 2))), SUBLANES * 2)

    def body(l_ref, s_ref, o_ref):
        x, y, _ = _mesh_pos()
        me = 2 * x + y
        acc = jnp.where(me == 0, s_ref[0], l_ref[0]).astype(F32)
        for k in range(1, N_CHIP):
            acc = acc + jnp.where(me == k, s_ref[k], l_ref[k]).astype(F32)
        o_ref[...] = acc

    slots = pl.BlockSpec((N_CHIP, tr, c), lambda i: (0, i, 0))
    return pl.pallas_call(
        body,
        name=name,
        out_shape=jax.ShapeDtypeStruct((r, c), F32),
        grid=(r // tr,),
        in_specs=[slots, slots],
        out_specs=pl.BlockSpec((tr, c), lambda i: (i, 0)),
        compiler_params=_cp("parallel"),
    )(land, sent)


def _mesh_pos():
    return lax.axis_index("x"), lax.axis_index("y"), lax.axis_index("c")


def _other_chips(x, y):
    return [(1 - x, y), (x, 1 - y), (1 - x, 1 - y)]


def _all_gather8(x_shard, name):
    m_per, n = x_shard.shape

    def body(x_ref, out_ref, send_sems, recv_sems, local_sem):
        x, y, c = _mesh_pos()
        me, sibling = (x, y, c), (x, y, 1 - c)
        chips = _other_chips(x, y)

        def rows(px, py, pc):
            return out_ref.at[pl.ds((4 * px + 2 * py + pc) * m_per, m_per), :]

        def copy(k, block, to, src=None):
            return pltpu.make_async_remote_copy(
                src_ref=rows(*block) if src is None else src,
                dst_ref=rows(*block),
                send_sem=send_sems.at[k],
                recv_sem=recv_sems.at[k],
                device_id=to,
                device_id_type=MESH,
            )

        mine = pltpu.make_async_copy(x_ref, rows(*me), local_sem)
        mine.start()
        first = [copy(0, me, sibling, src=x_ref)]
        first += [copy(1 + j, me, (*chip, c), src=x_ref) for j, chip in enumerate(chips)]
        for cp in first:
            cp.start()
        passed = [copy(4 + j, (*chip, c), sibling) for j, chip in enumerate(chips)]
        for j, chip in enumerate(chips):
            copy(1 + j, (*chip, c), me).wait_recv()
            passed[j].start()
        copy(0, sibling, me).wait_recv()
        for j, chip in enumerate(chips):
            copy(4 + j, (*chip, 1 - c), me).wait_recv()
        for cp in first + passed:
            cp.wait_send()
        mine.wait()

    return pl.pallas_call(
        body,
        name=name,
        out_shape=jax.ShapeDtypeStruct((N_DEV * m_per, n), x_shard.dtype),
        in_specs=[pl.BlockSpec(memory_space=pltpu.VMEM)],
        out_specs=pl.BlockSpec(memory_space=pltpu.VMEM),
        scratch_shapes=[pltpu.SemaphoreType.DMA((7,)), pltpu.SemaphoreType.DMA((7,)), pltpu.SemaphoreType.DMA],
        compiler_params=pltpu.CompilerParams(vmem_limit_bytes=VMEM_LIMIT),
    )(x_shard)


HBM_SPEC = pl.BlockSpec(memory_space=pltpu.HBM)
SEM_SPEC = pl.BlockSpec(memory_space=pltpu.SEMAPHORE)
DATAFLOW = pltpu.SideEffectType.DATAFLOW_SIDE_EFFECTING


def _half_rows(n_rows, c):
    return pl.ds(c * (n_rows // 2), n_rows // 2)


def _exchange_copies(ins, lands, send_sems, recv_sems, scatter, halves=False):
    x, y, c = _mesh_pos()
    me = 2 * x + y
    sends, recvs = [], []
    for t in range(len(ins)):
        rows = _half_rows(ins[t].shape[0], c) if halves else slice(None)
        for r, (px, py) in enumerate(_other_chips(x, y)):
            peer = 2 * px + py

            def copy(src, dst, k=3 * t + r, to=(px, py, c)):
                return pltpu.make_async_remote_copy(
                    src_ref=src, dst_ref=dst, send_sem=send_sems.at[k], recv_sem=recv_sems.at[k], device_id=to, device_id_type=MESH
                )

            if scatter:
                sends.append(copy(ins[t].at[peer], lands[t].at[me]))
                recvs.append(copy(ins[t].at[me], lands[t].at[peer]))
            else:
                sends.append(copy(ins[t].at[rows], lands[t].at[me, rows]))
                recvs.append(copy(ins[t].at[rows], lands[t].at[peer, rows]))
    return sends, recvs


def _sibling_fill(lands, name):
    nt = len(lands)

    def body(*refs):
        outs, send_sems, recv_sems = refs[nt : 2 * nt], refs[2 * nt], refs[2 * nt + 1]
        x, y, c = _mesh_pos()
        sends, recvs = [], []
        for t in range(nt):
            mine, theirs = _half_rows(outs[t].shape[1], c), _half_rows(outs[t].shape[1], 1 - c)
            for r, (px, py) in enumerate(_other_chips(x, y)):
                slot = 2 * px + py

                def copy(rows, k=3 * t + r, zone=outs[t], slot=slot):
                    part = zone.at[slot, rows]
                    return pltpu.make_async_remote_copy(
                        src_ref=part, dst_ref=part, send_sem=send_sems.at[k], recv_sem=recv_sems.at[k],
                        device_id=(x, y, 1 - c), device_id_type=MESH,
                    )

                sends.append(copy(mine))
                recvs.append(copy(theirs))
        for cp in sends:
            cp.start()
        for cp in recvs:
            cp.wait_recv()
        for cp in sends:
            cp.wait_send()

    return pl.pallas_call(
        body,
        name=name,
        out_shape=tuple(jax.ShapeDtypeStruct(a.shape, a.dtype) for a in lands),
        in_specs=[pl.BlockSpec(memory_space=pl.ANY)] * nt,
        out_specs=[pl.BlockSpec(memory_space=pl.ANY)] * nt,
        input_output_aliases={t: t for t in range(nt)},
        scratch_shapes=[pltpu.SemaphoreType.DMA((3 * nt,)), pltpu.SemaphoreType.DMA((3 * nt,))],
    )(*lands)


def _exchange_start(arrs, scatter, name, halves=False):
    nt = len(arrs)
    lands = [lax.empty(a.shape if scatter else (N_CHIP, *a.shape), a.dtype) for a in arrs]

    def body(*refs):
        ins, zones = refs[:nt], refs[nt : 2 * nt]
        send_sems, recv_sems, token = refs[2 * nt], refs[2 * nt + 1], refs[-1]
        sends, _ = _exchange_copies(ins, zones, send_sems, recv_sems, scatter, halves)
        for cp in sends:
            cp.start()
        token[...] = jnp.zeros(token.shape, F32)

    bufs = list(arrs) + list(lands)
    return pl.pallas_call(
        body,
        name=name,
        out_shape=(
            pltpu.SemaphoreType.DMA((3 * nt,)),
            pltpu.SemaphoreType.DMA((3 * nt,)),
            *[pltpu.HBM(a.shape, a.dtype) for a in bufs],
            jax.ShapeDtypeStruct((SUBLANES, LANES), F32),
        ),
        in_specs=[HBM_SPEC] * (2 * nt),
        out_specs=(SEM_SPEC, SEM_SPEC, *[HBM_SPEC] * (2 * nt), pl.BlockSpec(memory_space=pltpu.VMEM)),
        input_output_aliases={k: 2 + k for k in range(2 * nt)},
        compiler_params=pltpu.CompilerParams(has_side_effects=DATAFLOW),
    )(*[pltpu.with_memory_space_constraint(a, pltpu.HBM) for a in bufs])


def _exchange_wait(state, after, scatter, name, halves=False):
    send_sems, recv_sems, *bufs = state[:-1]
    nt = len(bufs) // 2
    afters = list(after) if isinstance(after, (list, tuple)) else [after]

    def body(*refs):
        ins, zones = refs[:nt], refs[nt : 2 * nt]
        sends, recvs = _exchange_copies(ins, zones, refs[2 * nt], refs[2 * nt + 1], scatter, halves)
        for cp in sends:
            cp.wait_send()
        for cp in recvs:
            cp.wait_recv()

    out = pl.pallas_call(
        body,
        name=name,
        out_shape=tuple(pltpu.HBM(a.shape, a.dtype) for a in bufs),
        in_specs=[HBM_SPEC] * (2 * nt) + [SEM_SPEC, SEM_SPEC] + [pl.BlockSpec(memory_space=pl.ANY)] * len(afters),
        out_specs=[HBM_SPEC] * (2 * nt),
        input_output_aliases={k: k for k in range(2 * nt)},
        compiler_params=pltpu.CompilerParams(has_side_effects=DATAFLOW),
    )(*bufs, send_sems, recv_sems, *afters)
    return list(out[:nt]), list(out[nt:])


def _swap_copies(ins, lands, send_sems, recv_sems):
    x, y, c = _mesh_pos()
    return [
        pltpu.make_async_remote_copy(
            src_ref=ins[t], dst_ref=lands[t], send_sem=send_sems.at[t], recv_sem=recv_sems.at[t],
            device_id=(x, y, 1 - c), device_id_type=MESH,
        )
        for t in range(len(ins))
    ]


def _swap_start(arrs, name):
    nt = len(arrs)
    lands = [lax.empty(a.shape, a.dtype) for a in arrs]

    def body(*refs):
        ins, zones = refs[:nt], refs[nt : 2 * nt]
        send_sems, recv_sems, token = refs[2 * nt], refs[2 * nt + 1], refs[-1]
        for cp in _swap_copies(ins, zones, send_sems, recv_sems):
            cp.start()
        token[...] = jnp.zeros(token.shape, F32)

    bufs = list(arrs) + lands
    return pl.pallas_call(
        body,
        name=name,
        out_shape=(
            pltpu.SemaphoreType.DMA((nt,)),
            pltpu.SemaphoreType.DMA((nt,)),
            *[pltpu.HBM(a.shape, a.dtype) for a in bufs],
            jax.ShapeDtypeStruct((SUBLANES, LANES), F32),
        ),
        in_specs=[HBM_SPEC] * (2 * nt),
        out_specs=(SEM_SPEC, SEM_SPEC, *[HBM_SPEC] * (2 * nt), pl.BlockSpec(memory_space=pltpu.VMEM)),
        input_output_aliases={k: 2 + k for k in range(2 * nt)},
        compiler_params=pltpu.CompilerParams(has_side_effects=DATAFLOW),
    )(*[pltpu.with_memory_space_constraint(a, pltpu.HBM) for a in bufs])


def _swap_wait(state, after, name):
    send_sems, recv_sems, *bufs = state[:-1]
    nt = len(bufs) // 2

    def body(*refs):
        cps = _swap_copies(refs[:nt], refs[nt : 2 * nt], refs[2 * nt], refs[2 * nt + 1])
        for cp in cps:
            cp.wait_send()
        for cp in cps:
            cp.wait_recv()

    out = pl.pallas_call(
        body,
        name=name,
        out_shape=tuple(pltpu.HBM(a.shape, a.dtype) for a in bufs),
        in_specs=[HBM_SPEC] * (2 * nt) + [SEM_SPEC, SEM_SPEC, pl.BlockSpec(memory_space=pl.ANY)],
        out_specs=[HBM_SPEC] * (2 * nt),
        input_output_aliases={k: k for k in range(2 * nt)},
        compiler_params=pltpu.CompilerParams(has_side_effects=DATAFLOW),
    )(*bufs, send_sems, recv_sems, after)
    return list(out[:nt]), list(out[nt:])


def _cols_from_shards(g):
    _, k, n = g.shape
    return jnp.transpose(g, (1, 0, 2)).reshape(k, N_CHIP * n)


def _cols_to_shards(a):
    k, n4 = a.shape
    return jnp.transpose(a.reshape(k, N_CHIP, n4 // N_CHIP), (1, 0, 2))


def _pad_to(vec, mult):
    n = vec.shape[0]
    return jnp.pad(vec, (0, (-n) % mult))


def kernel(x, c, positions, w_ada, b_ada, g_pre_mix, g_post_mix, w_in, g_q, w_uq, g_kv, w_ukv, conv_w_mix, conv_b_mix, w_o, g_pre_ffn, g_post_ffn, w_up, conv_w_ffn, conv_b_ffn, w_down, loss_target, m_w_ada, m_b_ada, m_g_pre_mix, m_g_post_mix, m_w_in, m_g_q, m_w_uq, m_g_kv, m_w_ukv, m_conv_w_mix, m_conv_b_mix, m_w_o, m_g_pre_ffn, m_g_post_ffn, m_w_up, m_conv_w_ffn, m_conv_b_ffn, m_w_down, v_w_ada, v_b_ada, v_g_pre_mix, v_g_post_mix, v_w_in, v_g_q, v_w_uq, v_g_kv, v_w_ukv, v_conv_w_mix, v_conv_b_mix, v_w_o, v_g_pre_ffn, v_g_post_ffn, v_w_up, v_conv_w_ffn, v_conv_b_ffn, v_w_down):
    xi, yi, ci = _mesh_pos()
    chip = 2 * xi + yi
    dev = 4 * xi + 2 * yi + ci

    s, d = x.shape[1], x.shape[2]
    ql, kl = g_q.shape[1], g_kv.shape[1]
    cwid = conv_b_mix.shape[1]
    f2 = conv_b_ffn.shape[1]
    hh = (w_uq.shape[2] * N_CHIP) // (NOPE + ROPE)
    w_att = hh * LANES
    nc_ada = w_ada.shape[2]
    lat = ql + kl + ROPE
    tc_mix = _gate_tile(cwid)
    lb = -(-(ql + kl + HEAD_PAD) // (3 * tc_mix)) * (3 * tc_mix)
    np_cols = lb + 3 * cwid
    assert cwid == hh * VDIM and w_att % tc_mix == 0

    x0 = x.reshape(s, d)
    tgt = loss_target.reshape(s, d)

    anchors = []

    def _behind(val, state):
        val, tok = lax.optimization_barrier((val, state[-1]))
        anchors.append(tok[0, 0])
        return val

    cwm_n, cwf_n = CONV_K * cwid // N_CHIP, CONV_K * f2 // N_CHIP
    pack_a = _pad_to(jnp.concatenate([c.reshape(-1), conv_w_mix.reshape(-1), conv_w_ffn.reshape(-1)]), SUBLANES * LANES)
    rows_a = _all_gather8(pack_a.reshape(SUBLANES, -1), "ag8_inputs").reshape(N_DEV, -1)
    c_all = rows_a[:, :d]
    south = rows_a[0::2]
    cw_mix = jnp.concatenate([south[j, d : d + cwm_n].reshape(CONV_K, -1) for j in range(N_CHIP)], axis=1)
    cw_ffn = jnp.concatenate([south[j, d + cwm_n : d + cwm_n + cwf_n].reshape(CONV_K, -1) for j in range(N_CHIP)], axis=1)

    b_cols = lax.dynamic_slice(b_ada, (0, chip * nc_ada), (1, nc_ada))
    mod_part, c_act = _ada_fwd(c_all, w_ada[0], b_cols)
    mod_rows = _all_gather8(mod_part, "ag8_mod")
    mod = jnp.concatenate(
        [lax.dynamic_slice_in_dim(mod_rows, 2 * N_DEV * j + dev, 1, axis=0) for j in range(N_CHIP)], axis=1
    )

    shards = [a[0].astype(BF16) for a in (w_in, w_uq, w_ukv, w_o, w_up, w_down)]
    first, mod = lax.optimization_barrier((shards[:3], mod))
    ag_a = _exchange_start(first, False, "ag_a_start", halves=True)
    mod = _behind(mod, ag_a)
    sh_m, sc_m, gt_m, sh_f, sc_f, gt_f = [mod[:, k * d : (k + 1) * d] for k in range(N_MOD)]

    inv_freq = 1.0 / (ROPE_THETA ** (jnp.arange(0, ROPE, 2, dtype=F32) / ROPE))
    invf = jnp.concatenate([inv_freq, inv_freq, jnp.zeros((LANES - ROPE,), F32)]).reshape(1, LANES)
    tabs = _rope_tables(positions.astype(F32).reshape(s, 1), invf)
    h1 = _pre_fwd(x0, g_pre_mix, sc_m, sh_m)

    def with_own(landed, own):
        return [lax.dynamic_update_slice_in_dim(g, a[None], chip, axis=0) for g, a in zip(landed, own)]

    own_w, landed_w = _exchange_wait(ag_a, [h1, tabs[0]], False, "ag_a_wait", halves=True)
    landed_w = list(_sibling_fill(landed_w, "ag_a_fill"))
    rest, landed_w = lax.optimization_barrier((shards[3:], landed_w))
    ag_b = _exchange_start(rest, False, "ag_b_start")
    h1 = _behind(h1, ag_b)
    g_in, g_uq, g_ukv = with_own(landed_w, own_w)
    full_in = _cols_from_shards(g_in)
    gate_cols = [(lat + k * cwid + j * tc_mix, tc_mix) for j in range(cwid // tc_mix) for k in range(3)]
    w_in_p = jnp.concatenate(
        [full_in[:, :lat], jnp.zeros((d, lb - lat), BF16)] + [full_in[:, o : o + n] for o, n in gate_cols], axis=1
    )
    full_uq = _cols_from_shards(g_uq).reshape(ql, hh, NOPE + ROPE)
    w_uq_p = jnp.concatenate(
        [
            full_uq[:, :, :NOPE].reshape(ql, w_att),
            jnp.pad(full_uq[:, :, NOPE:], ((0, 0), (0, 0), (0, HEAD_PAD - ROPE))).reshape(ql, w_att),
        ],
        axis=1,
    )
    full_ukv = _cols_from_shards(g_ukv).reshape(kl, hh, NOPE + VDIM)
    w_ukv_p = jnp.concatenate([full_ukv[:, :, :NOPE].reshape(kl, w_att), full_ukv[:, :, NOPE:].reshape(kl, w_att)], axis=1)

    proj = _matmul(h1, w_in_p, out_dtype=F32, tm=1024, tn=768, tk=2048, name="mm_proj")
    qn, kvn, kr = _latent_fwd(proj, g_q, g_kv, tabs, lb)
    q_f = _matmul(qn, w_uq_p, out_dtype=F32, tm=1024, tn=1024, tk=2048, name="mm_q")
    kv_p = _matmul(kvn, w_ukv_p, out_dtype=BF16, tm=1024, tn=1024, tk=2048, name="mm_kv")
    q_c, k_c = _head_cat(q_f, kv_p, kr, tabs, hh)
    cat, lse2 = _attn_fwd(q_c, k_c, kv_p, hh, w_att + cwid)
    cat = _mixer_fwd(cat, proj, cw_mix, conv_b_mix, lb, w_att)
    own_w, landed_w = _exchange_wait(ag_b, cat, False, "ag_b_wait")
    g_o, g_up, g_down = with_own(landed_w, own_w)
    w_o_f = g_o.reshape(-1, d)
    cw_ffn_p, cb_ffn_p = _pair_cols(cw_ffn), _pair_cols(conv_b_ffn)
    tcp, pair_perm = _pair_tile(f2 // 2), _pair_perm(f2 // 2)
    w_down_f = g_down.reshape(-1, d)
    mix = _matmul(cat, w_o_f, out_dtype=F32, tm=1024, tn=1024, tk=2048, name="mm_mix")

    x1, h2 = _mid_fwd(x0, mix, g_post_mix, gt_m, g_pre_ffn, sc_f, sh_f)
    up = _matmul(h2, g_up, out_dtype=F32, tm=1024, tn=tcp, tk=2048, name="mm_up", b_n_perm=pair_perm, b_col_shards=True)
    act = _ffn_act_fwd(up, cw_ffn_p, cb_ffn_p)
    y = _matmul(act, w_down_f, out_dtype=F32, tm=512, tn=1024, tk=5632, name="mm_down")
    dx2, dy, s_fin = _final(x1, y, tgt, g_post_ffn, gt_f)

    dw_down = _matmul(act, dy, ta=True, out_dtype=BF16, tm=512, tn=2048, tk=4096, name="mm_dw_down")
    dact = _matmul(dy, w_down_f, tb=True, out_dtype=F32, tm=1024, tn=1408, tk=2048, name="mm_dact")
    dup, s_ffn_p = _ffn_act_bwd(dact, up, cw_ffn_p, cb_ffn_p)
    s_ffn = _unpair_cols(s_ffn_p)
    dw_up = _matmul(
        h2, dup, ta=True, out_dtype=BF16, tm=512, tn=tcp, tk=4096, name="mm_dw_up", out_n_perm=pair_perm, out_col_shards=True
    )
    dh2 = _matmul_pair_k(dup, g_up, out_dtype=F32, tm=512, tn=1024, pairs=2, name="mm_dh2")
    dx1, dmix, s_mid = _mid_bwd(dh2, dx2, x1, mix, g_pre_ffn, sc_f, g_post_mix, gt_m)

    dw_o = _matmul(cat, dmix, ta=True, out_dtype=BF16, tm=512, tn=1024, tk=4096, name="mm_dw_o")
    send_b = [dw_o.reshape(N_CHIP, -1, d), dw_up, dw_down.reshape(N_CHIP, -1, d)]
    rs_b = _exchange_start(send_b, True, "rs_b_start")
    dmix = _behind(dmix, rs_b)
    dcat = _matmul(dmix, w_o_f, tb=True, out_dtype=F32, tm=1024, tn=1024, tk=2048, name="mm_dcat")
    dproj, s_mix = _mixer_bwd(dcat, proj, cw_mix, conv_b_mix, lb, w_att)
    dob, stats = _attn_bwd_prep(cat, dcat, lse2, hh)
    dq_raw, dkv_k, dkv_v, dkr_h = _attn_bwd(q_c, k_c, kv_p, dob, stats, hh)
    dkv_p = jnp.concatenate([dkv_k, dkv_v], axis=1)
    dq_p = _dq_unrope(dq_raw, tabs, hh)
    dw_uq_p = _matmul(qn, dq_p, ta=True, out_dtype=BF16, tm=1024, tn=1024, tk=1024, name="mm_dw_uq")
    dqn = _matmul(dq_p, w_uq_p, tb=True, out_dtype=F32, tm=1024, tn=1024, tk=2048, name="mm_dqn")
    dw_ukv_p = _matmul(kvn, dkv_p, ta=True, out_dtype=BF16, tm=1024, tn=1024, tk=1024, name="mm_dw_ukv")
    dkvn = _matmul(dkv_p, w_ukv_p, tb=True, out_dtype=F32, tm=1024, tn=1024, tk=2048, name="mm_dkvn")
    dproj, s_lat = _latent_bwd(dproj, proj, dqn, dkvn, dkr_h, g_q, g_kv, tabs, lb)
    dw_in_p = _matmul(h1, dproj, ta=True, out_dtype=BF16, tm=512, tn=1536, tk=4096, name="mm_dw_in")

    n_trip = cwid // tc_mix
    ungate = [lb + (3 * j + k) * tc_mix for k in range(3) for j in range(n_trip)]
    dw_in_f = jnp.concatenate([dw_in_p[:, :lat]] + [dw_in_p[:, o : o + tc_mix] for o in ungate], axis=1)
    uq3 = dw_uq_p.reshape(ql, 2, hh, LANES)
    dw_uq_f = jnp.concatenate([uq3[:, 0], uq3[:, 1, :, :ROPE]], axis=2).reshape(ql, hh * (NOPE + ROPE))
    ukv3 = dw_ukv_p.reshape(kl, 2, hh, LANES)
    dw_ukv_f = jnp.concatenate([ukv3[:, 0], ukv3[:, 1]], axis=2).reshape(kl, hh * (NOPE + VDIM))
    send_a = [_cols_to_shards(dw_in_f), _cols_to_shards(dw_uq_f), _cols_to_shards(dw_ukv_f)]
    rs_a = _exchange_start(send_a, True, "rs_a_start")
    dproj = _behind(dproj, rs_a)

    dh1 = _matmul(dproj, w_in_p, tb=True, out_dtype=F32, tm=512, tn=1024, tk=4608, name="mm_dh1")
    grad_x, s_first = _first_bwd(dh1, dx1, x0, g_pre_mix, sc_m)

    names = ["w_in", "w_uq", "w_ukv", "w_o", "w_up", "w_down"]
    sent_b, landed_b = _exchange_wait(rs_b, s_first, True, "rs_b_wait")
    sent_a, landed_a = _exchange_wait(rs_a, landed_b[0], True, "rs_a_wait")
    landed_a, s_first = lax.optimization_barrier((landed_a, s_first))
    part = [_sum_chips(l, a, "sum_chips_" + n) for l, a, n in zip(landed_a + landed_b, sent_a + sent_b, names)]

    dmod = jnp.concatenate([s_first[0:1], s_first[1:2], s_mid[3:4], s_mid[0:1], s_mid[1:2], s_fin[0:1]], axis=1)
    small = [
        dmod,
        s_first[2:3],
        s_mid[4:5],
        s_lat[0:1, :ql],
        s_lat[0:1, ql : ql + kl],
        s_mix[3:4],
        s_mid[2:3],
        s_fin[1:2],
        s_ffn[3:4],
        s_mix[0:3].reshape(1, -1),
        s_ffn[0:3].reshape(1, -1),
        s_fin[3:4, :LANES],
    ]
    sizes = [a.shape[1] for a in small]
    offs = [0]
    for n in sizes:
        offs.append(offs[-1] + n)
    pack_g = _pad_to(jnp.concatenate(small, axis=1).reshape(-1), SUBLANES * LANES * SUBLANES).reshape(SUBLANES, -1)
    gathered = _all_gather8(pack_g, "ag8_small_grads")
    tot = _sum_devices(gathered).reshape(-1)
    part_of = lambda k: tot[offs[k] : offs[k + 1]]
    dmod_all = gathered.reshape(N_DEV, -1)[:, : N_MOD * d]
    loss = part_of(11)[0]

    g_b_ada = part_of(0).reshape(1, -1)
    g_vecs = [part_of(k).reshape(1, -1) for k in range(1, 9)]
    g_cw_mix = lax.dynamic_slice(part_of(9).reshape(CONV_K, cwid), (0, chip * (cwid // N_CHIP)), (CONV_K, cwid // N_CHIP))
    g_cw_ffn = lax.dynamic_slice(part_of(10).reshape(CONV_K, f2), (0, chip * (f2 // N_CHIP)), (CONV_K, f2 // N_CHIP))

    swap = _swap_start(part, "swap_start")
    dm_cols = _behind(lax.dynamic_slice(dmod_all, (0, chip * nc_ada), (N_DEV, nc_ada)), swap)
    g_w_ada = _ada_grad(
        jnp.pad(c_act.T, ((0, 0), (0, LANES - N_DEV))), jnp.pad(dm_cols, ((0, LANES - N_DEV), (0, 0)))
    )
    big = {"w_ada": [a[None] for a in _adamw(w_ada[0], m_w_ada[0], v_w_ada[0], [g_w_ada], "adamw_w_ada")]}
    part, other = _swap_wait(swap, big["w_ada"][1], "swap_wait")

    big_w = [w_in, w_uq, w_ukv, w_o, w_up, w_down]
    big_m = [m_w_in, m_w_uq, m_w_ukv, m_w_o, m_w_up, m_w_down]
    big_v = [v_w_in, v_w_uq, v_w_ukv, v_w_o, v_w_up, v_w_down]
    for n, w_, m_, v_, p_, o_ in zip(names, big_w, big_m, big_v, part, other):
        big[n] = [a[None] for a in _adamw(w_[0], m_[0], v_[0], [p_, o_], "adamw_" + n)]

    sm_names = ["b_ada", "g_pre_mix", "g_post_mix", "g_q", "g_kv", "conv_b_mix", "g_pre_ffn", "g_post_ffn", "conv_b_ffn",
                "conv_w_mix", "conv_w_ffn"]
    sm_w = [b_ada, g_pre_mix, g_post_mix, g_q, g_kv, conv_b_mix, g_pre_ffn, g_post_ffn, conv_b_ffn, conv_w_mix, conv_w_ffn]
    sm_m = [m_b_ada, m_g_pre_mix, m_g_post_mix, m_g_q, m_g_kv, m_conv_b_mix, m_g_pre_ffn, m_g_post_ffn, m_conv_b_ffn,
            m_conv_w_mix, m_conv_w_ffn]
    sm_v = [v_b_ada, v_g_pre_mix, v_g_post_mix, v_g_q, v_g_kv, v_conv_b_mix, v_g_pre_ffn, v_g_post_ffn, v_conv_b_ffn,
            v_conv_w_mix, v_conv_w_ffn]
    sm_g = [g_b_ada] + g_vecs + [g_cw_mix, g_cw_ffn]
    flat = lambda arrs: jnp.concatenate([a.reshape(1, -1) for a in arrs], axis=1)
    sm_out = _adamw(flat(sm_w), flat(sm_m), flat(sm_v), [flat(sm_g)], "adamw_small")
    sm = {}
    off = 0
    for n, w_ in zip(sm_names, sm_w):
        sm[n] = [o[:, off : off + w_.size].reshape(w_.shape) for o in sm_out]
        off += w_.size

    order = ["w_ada", "b_ada", "g_pre_mix", "g_post_mix", "w_in", "g_q", "w_uq", "g_kv", "w_ukv", "conv_w_mix", "conv_b_mix",
             "w_o", "g_pre_ffn", "g_post_ffn", "w_up", "conv_w_ffn", "conv_b_ffn", "w_down"]
    res = {**big, **sm}
    outs = [loss + sum(anchors), grad_x.reshape(x.shape)]
    for k in range(4):
        outs += [res[n][k] for n in order]
    return tuple(outs)
```

```python
import math

import jax
import jax.numpy as jnp
from jax import lax
from jax.experimental import pallas as pl
from jax.experimental.pallas import tpu as pltpu

F32 = jnp.float32
BF16 = jnp.bfloat16
MESH = pl.DeviceIdType.MESH

N_DEV = 8
N_CHIP = 4
LANES = 128
SUBLANES = 8
VMEM_LIMIT = 56 * 2**20

NOPE = 128
ROPE = 64
VDIM = 128
HEAD_PAD = 128
ROPE_THETA = 10000.0
RMS_EPS = 1e-6
N_MOD = 6
CONV_K = 3
ATT_FWD_BLOCK, ATT_FWD_SUB = 2048, 256
ATT_BWD_BLOCK, ATT_BWD_SUB = 1024, 256
NEG = -1e30

ADAM_LR = 0.001
ADAM_B1 = 0.9
ADAM_B2 = 0.999
ADAM_EPS = 1e-08
ADAM_WD = 0.01
ADAM_STEP = 10


def _tile(n, pref, align):
    if n <= pref:
        return n
    t = (pref // align) * align
    while t >= align:
        if n % t == 0:
            return t
        t -= align
    return n


def _cp(*sem):
    return pltpu.CompilerParams(dimension_semantics=sem, vmem_limit_bytes=VMEM_LIMIT)


def _rsq(x):
    return lax.rsqrt(jnp.mean(x * x, axis=-1, keepdims=True) + RMS_EPS)


def _norm_bwd(dn, n, r):
    return r * (dn - n * jnp.mean(dn * n, axis=-1, keepdims=True))


def _colsum(a):
    return jnp.sum(a, axis=0, keepdims=True)


def _matmul(a, b, *, ta=False, tb=False, out_dtype, tm, tn, tk, name, b_n_perm=None, out_n_perm=None,
            b_col_shards=False, out_col_shards=False, n_major=False):
    assert not (b_col_shards and tb)
    if b_col_shards:
        b_rows, b_cols = b.shape[1], N_CHIP * b.shape[2]
    else:
        b_rows, b_cols = b.shape
    (k_a, m) = a.shape if ta else a.shape[::-1]
    (n, k_b) = (b_rows, b_cols) if tb else (b_cols, b_rows)
    assert k_a == k_b, (a.shape, b.shape, ta, tb)
    tm, tn, tk = _tile(m, tm, LANES), _tile(n, tn, LANES), _tile(k_a, tk, LANES)
    nk = k_a // tk
    same = lambda t: t
    bn, on = b_n_perm or same, out_n_perm or same
    a_spec = pl.BlockSpec((tk, tm), lambda i, j, k: (k, i)) if ta else pl.BlockSpec((tm, tk), lambda i, j, k: (i, k))
    if b_col_shards:
        per = (b_cols // N_CHIP) // tn
        b_spec = pl.BlockSpec((None, tk, tn), lambda i, j, k: (bn(j) // per, k, bn(j) % per))
    elif tb:
        b_spec = pl.BlockSpec((tn, tk), lambda i, j, k: (bn(j), k))
    else:
        b_spec = pl.BlockSpec((tk, tn), lambda i, j, k: (k, bn(j)))
    if out_col_shards:
        per_o = (n // N_CHIP) // tn
        out_shape = jax.ShapeDtypeStruct((N_CHIP, m, n // N_CHIP), out_dtype)
        out_spec = pl.BlockSpec((None, tm, tn), lambda i, j, k: (on(j) // per_o, i, on(j) % per_o))
    else:
        out_shape = jax.ShapeDtypeStruct((m, n), out_dtype)
        out_spec = pl.BlockSpec((tm, tn), lambda i, j, k: (i, on(j)))
    dims = (((0 if ta else 1,), (1 if tb else 0,)), ((), ()))

    def body(a_ref, b_ref, o_ref, *acc):
        p = lax.dot_general(a_ref[...].astype(BF16), b_ref[...].astype(BF16), dims, preferred_element_type=F32)
        _accumulate(p, o_ref, acc, nk)

    grid = (m // tm, n // tn, nk)
    if n_major:
        flip = lambda spec: pl.BlockSpec(spec.block_shape, lambda j, i, k, f=spec.index_map: f(i, j, k))
        a_spec, b_spec, out_spec = flip(a_spec), flip(b_spec), flip(out_spec)
        grid = (n // tn, m // tm, nk)

    return pl.pallas_call(
        body,
        name=name,
        out_shape=out_shape,
        grid=grid,
        in_specs=[a_spec, b_spec],
        out_specs=out_spec,
        scratch_shapes=[] if nk == 1 else [pltpu.VMEM((tm, tn), F32)],
        compiler_params=_cp("parallel", "parallel", "arbitrary"),
    )(a, b)


def _accumulate(p, o_ref, acc, nk):
    if nk == 1:
        o_ref[...] = p.astype(o_ref.dtype)
        return
    k = pl.program_id(2)

    @pl.when(k == 0)
    def _():
        acc[0][...] = p

    @pl.when(k > 0)
    def _():
        acc[0][...] += p

    @pl.when(k == nk - 1)
    def _():
        o_ref[...] = acc[0][...].astype(o_ref.dtype)


def _matmul_pair_k(a, b_shards, *, out_dtype, tm, tn, pairs, name):
    m, f2 = a.shape
    n = b_shards.shape[1]
    tc = _pair_tile(f2 // 2)
    nj = (f2 // 2) // tc
    nk = nj // pairs
    per = (f2 // N_CHIP) // tc
    tm, tn = _tile(m, tm, LANES), _tile(n, tn, LANES)

    def body(a_ref, *refs):
        w_refs, (o_ref, *acc) = refs[: 2 * pairs], refs[2 * pairs :]
        av = a_ref[...]
        p = None
        for q in range(2 * pairs):
            part = lax.dot_general(av[:, q * tc : (q + 1) * tc], w_refs[q][...], NT, preferred_element_type=F32)
            p = part if p is None else p + part
        _accumulate(p, o_ref, acc, nk)

    def w_tile(first, q):
        return pl.BlockSpec((None, tn, tc), lambda i, j, k: ((first + pairs * k + q) // per, j, (first + pairs * k + q) % per))

    w_specs = [w_tile(first, q) for q in range(pairs) for first in (0, nj)]
    return pl.pallas_call(
        body,
        name=name,
        out_shape=jax.ShapeDtypeStruct((m, n), out_dtype),
        grid=(m // tm, n // tn, nk),
        in_specs=[pl.BlockSpec((tm, 2 * tc * pairs), lambda i, j, k: (i, k))] + w_specs,
        out_specs=pl.BlockSpec((tm, tn), lambda i, j, k: (i, j)),
        scratch_shapes=[] if nk == 1 else [pltpu.VMEM((tm, tn), F32)],
        compiler_params=_cp("parallel", "parallel", "arbitrary"),
    )(a, *[b_shards] * (2 * pairs))


def _rope_tables(pos_col, invf):
    s = pos_col.shape[0]
    ts = _tile(s, 1024, SUBLANES)
    half = ROPE // 2

    def body(p_ref, f_ref, c_ref, sa_ref, sb_ref):
        ang = p_ref[...] * f_ref[...]
        lane = lax.broadcasted_iota(jnp.int32, ang.shape, 1)
        cs, sn = jnp.cos(ang), jnp.sin(ang)
        c_ref[...] = jnp.where(lane < ROPE, cs, 0.0)
        sa_ref[...] = jnp.where((lane >= half) & (lane < ROPE), sn, 0.0)
        sb_ref[...] = jnp.where(lane < half, -sn, 0.0)

    tab = jax.ShapeDtypeStruct((s, LANES), F32)
    return pl.pallas_call(
        body,
        name="rope_tables",
        out_shape=(tab, tab, tab),
        grid=(s // ts,),
        in_specs=[pl.BlockSpec((ts, 1), lambda i: (i, 0)), pl.BlockSpec((1, LANES), lambda i: (0, 0))],
        out_specs=[pl.BlockSpec((ts, LANES), lambda i: (i, 0))] * 3,
        compiler_params=_cp("parallel"),
    )(pos_col, invf)


def _widen(t, w):
    return t if w == LANES else jnp.tile(t, (1, w // LANES))


def _rope(x, c, sa, sb):
    w = x.shape[1]
    c, sa, sb = _widen(c, w), _widen(sa, w), _widen(sb, w)
    return x * c + pltpu.roll(x, ROPE // 2, 1) * sa + pltpu.roll(x, w - ROPE // 2, 1) * sb


def _rope_t(d, c, sa, sb):
    w = d.shape[1]
    c, sa, sb = _widen(c, w), _widen(sa, w), _widen(sb, w)
    return d * c + pltpu.roll(d * sa, w - ROPE // 2, 1) + pltpu.roll(d * sb, ROPE // 2, 1)


def _ada_fwd(c_all, w, b):
    d, nc = w.shape
    tn = _tile(nc, 512, LANES)

    def body(c_ref, w_ref, b_ref, o_ref, ca_ref):
        cv = c_ref[...]
        ca = cv * jax.nn.sigmoid(cv)
        ca_ref[...] = ca
        o_ref[...] = jnp.dot(ca.astype(BF16), w_ref[...].astype(BF16), preferred_element_type=F32) + b_ref[...]

    return pl.pallas_call(
        body,
        name="ada_fwd",
        out_shape=(jax.ShapeDtypeStruct((N_DEV, nc), F32), jax.ShapeDtypeStruct((N_DEV, d), F32)),
        grid=(nc // tn,),
        in_specs=[
            pl.BlockSpec((N_DEV, d), lambda j: (0, 0)),
            pl.BlockSpec((d, tn), lambda j: (0, j)),
            pl.BlockSpec((1, tn), lambda j: (0, j)),
        ],
        out_specs=[pl.BlockSpec((N_DEV, tn), lambda j: (0, j)), pl.BlockSpec((N_DEV, d), lambda j: (0, 0))],
        compiler_params=_cp("arbitrary"),
    )(c_all, w, b)


def _rows(ts, d):
    return pl.BlockSpec((ts, d), lambda i: (i, 0))


def _vec(d):
    return pl.BlockSpec((1, d), lambda i: (0, 0))


def _sums(d):
    return pl.BlockSpec((SUBLANES, d), lambda i: (0, 0))


def _acc_rows(ref, i, rows):
    @pl.when(i == 0)
    def _():
        ref[...] = jnp.zeros(ref.shape, ref.dtype)

    for k, r in enumerate(rows):
        ref[k : k + 1, :] += r


def _pre_fwd(x, g, sc, sh):
    s, d = x.shape
    ts = _tile(s, 512, SUBLANES)

    def body(x_ref, g_ref, sc_ref, sh_ref, h_ref):
        xv = x_ref[...]
        h_ref[...] = (((xv * _rsq(xv)) * g_ref[...]) * (1.0 + sc_ref[...]) + sh_ref[...]).astype(BF16)

    return pl.pallas_call(
        body,
        name="pre_mix_fwd",
        out_shape=jax.ShapeDtypeStruct((s, d), BF16),
        grid=(s // ts,),
        in_specs=[_rows(ts, d), _vec(d), _vec(d), _vec(d)],
        out_specs=_rows(ts, d),
        compiler_params=_cp("parallel"),
    )(x, g, sc, sh)


def _mid_fwd(x0, mix, g_post, gt, g_pre, sc, sh):
    s, d = x0.shape
    ts = _tile(s, 512, SUBLANES)

    def body(x_ref, m_ref, gp_ref, gt_ref, g_ref, sc_ref, sh_ref, x1_ref, h_ref):
        mv = m_ref[...]
        x1 = x_ref[...] + gt_ref[...] * ((mv * _rsq(mv)) * gp_ref[...])
        x1_ref[...] = x1
        h_ref[...] = (((x1 * _rsq(x1)) * g_ref[...]) * (1.0 + sc_ref[...]) + sh_ref[...]).astype(BF16)

    return pl.pallas_call(
        body,
        name="mid_fwd",
        out_shape=(jax.ShapeDtypeStruct((s, d), F32), jax.ShapeDtypeStruct((s, d), BF16)),
        grid=(s // ts,),
        in_specs=[_rows(ts, d), _rows(ts, d)] + [_vec(d)] * 5,
        out_specs=[_rows(ts, d), _rows(ts, d)],
        compiler_params=_cp("parallel"),
    )(x0, mix, g_post, gt, g_pre, sc, sh)


def _final(x1, y, tgt, g_post, gt):
    s, d = x1.shape
    ts = _tile(s, 256, SUBLANES)
    ni = s // ts

    def body(x_ref, y_ref, t_ref, gp_ref, gt_ref, dx_ref, dy_ref, s_ref):
        i = pl.program_id(0)
        yv, gp, gt_v = y_ref[...], gp_ref[...], gt_ref[...]
        r = _rsq(yv)
        n = yv * r
        err = (x_ref[...] + gt_v * (n * gp)) - t_ref[...]
        dx = err * (1.0 / d)
        dx_ref[...] = dx
        dy_ref[...] = _norm_bwd(dx * (gt_v * gp), n, r).astype(BF16)
        _acc_rows(s_ref, i, [_colsum(dx * (n * gp)), _colsum(dx * gt_v * n), _colsum(err * err)])

        @pl.when(i == ni - 1)
        def _():
            tot = jnp.sum(s_ref[2:3, :], axis=1, keepdims=True) * (0.5 / d)
            s_ref[3:4, :] = jnp.broadcast_to(tot, (1, d))

    return pl.pallas_call(
        body,
        name="final_fwd_bwd",
        out_shape=(
            jax.ShapeDtypeStruct((s, d), F32),
            jax.ShapeDtypeStruct((s, d), BF16),
            jax.ShapeDtypeStruct((SUBLANES, d), F32),
        ),
        grid=(ni,),
        in_specs=[_rows(ts, d)] * 3 + [_vec(d)] * 2,
        out_specs=[_rows(ts, d), _rows(ts, d), _sums(d)],
        compiler_params=_cp("arbitrary"),
    )(x1, y, tgt, g_post, gt)


def _mid_bwd(dh2, dx2, x1, mix, g_pre, sc, g_post, gt):
    s, d = x1.shape
    ts = _tile(s, 256, SUBLANES)

    def body(dh_ref, dx2_ref, x_ref, m_ref, g_ref, sc_ref, gp_ref, gt_ref, dx1_ref, dm_ref, s_ref):
        i = pl.program_id(0)
        dh, xv, mv = dh_ref[...], x_ref[...], m_ref[...]
        g, sc_v, gp, gt_v = g_ref[...], sc_ref[...], gp_ref[...], gt_ref[...]
        r1 = _rsq(xv)
        n1 = xv * r1
        dx1 = dx2_ref[...] + _norm_bwd(dh * (g * (1.0 + sc_v)), n1, r1)
        dx1_ref[...] = dx1
        rm = _rsq(mv)
        nm = mv * rm
        dm_ref[...] = _norm_bwd(dx1 * (gt_v * gp), nm, rm).astype(BF16)
        _acc_rows(
            s_ref,
            i,
            [
                _colsum(dh),
                _colsum(dh * (n1 * g)),
                _colsum(dh * (1.0 + sc_v) * n1),
                _colsum(dx1 * (nm * gp)),
                _colsum(dx1 * gt_v * nm),
            ],
        )

    return pl.pallas_call(
        body,
        name="mid_bwd",
        out_shape=(
            jax.ShapeDtypeStruct((s, d), F32),
            jax.ShapeDtypeStruct((s, d), BF16),
            jax.ShapeDtypeStruct((SUBLANES, d), F32),
        ),
        grid=(s // ts,),
        in_specs=[_rows(ts, d)] * 4 + [_vec(d)] * 4,
        out_specs=[_rows(ts, d), _rows(ts, d), _sums(d)],
        compiler_params=_cp("arbitrary"),
    )(dh2, dx2, x1, mix, g_pre, sc, g_post, gt)


def _first_bwd(dh1, dx1, x0, g, sc):
    s, d = x0.shape
    ts = _tile(s, 256, SUBLANES)

    def body(dh_ref, dx1_ref, x_ref, g_ref, sc_ref, dx_ref, s_ref):
        i = pl.program_id(0)
        dh, xv, gv, sc_v = dh_ref[...], x_ref[...], g_ref[...], sc_ref[...]
        r = _rsq(xv)
        n = xv * r
        dx_ref[...] = dx1_ref[...] + _norm_bwd(dh * (gv * (1.0 + sc_v)), n, r)
        _acc_rows(s_ref, i, [_colsum(dh), _colsum(dh * (n * gv)), _colsum(dh * (1.0 + sc_v) * n)])

    return pl.pallas_call(
        body,
        name="first_bwd",
        out_shape=(jax.ShapeDtypeStruct((s, d), F32), jax.ShapeDtypeStruct((SUBLANES, d), F32)),
        grid=(s // ts,),
        in_specs=[_rows(ts, d)] * 3 + [_vec(d)] * 2,
        out_specs=[_rows(ts, d), _sums(d)],
        compiler_params=_cp("arbitrary"),
    )(dh1, dx1, x0, g, sc)


def _latent_fwd(proj, g_q, g_kv, tabs, lb):
    s = proj.shape[0]
    ql, kl = g_q.shape[1], g_kv.shape[1]
    ts = _tile(s, 512, SUBLANES)

    def body(p_ref, gq_ref, gk_ref, c_ref, sa_ref, sb_ref, q_ref, kv_ref, kr_ref):
        pv = p_ref[...]
        q, kv, kr = pv[:, :ql], pv[:, ql : ql + kl], pv[:, ql + kl : ql + kl + HEAD_PAD]
        q_ref[...] = ((q * _rsq(q)) * gq_ref[...]).astype(BF16)
        kv_ref[...] = ((kv * _rsq(kv)) * gk_ref[...]).astype(BF16)
        kr_ref[...] = _rope(kr, c_ref[...], sa_ref[...], sb_ref[...]).astype(BF16)

    return pl.pallas_call(
        body,
        name="latent_fwd",
        out_shape=(
            jax.ShapeDtypeStruct((s, ql), BF16),
            jax.ShapeDtypeStruct((s, kl), BF16),
            jax.ShapeDtypeStruct((s, HEAD_PAD), BF16),
        ),
        grid=(s // ts,),
        in_specs=[_rows(ts, lb), _vec(ql), _vec(kl)] + [_rows(ts, LANES)] * 3,
        out_specs=[_rows(ts, ql), _rows(ts, kl), _rows(ts, HEAD_PAD)],
        compiler_params=_cp("parallel"),
    )(proj, g_q, g_kv, *tabs)


def _latent_bwd(dproj, proj, dqn, dkvn, dkr_h, g_q, g_kv, tabs, lb):
    s = proj.shape[0]
    ql, kl = g_q.shape[1], g_kv.shape[1]
    hw = dkr_h.shape[1]
    ts = _tile(s, 512, SUBLANES)
    pad = lb - ql - kl - HEAD_PAD

    def body(_, p_ref, dq_ref, dkv_ref, dkr_ref, gq_ref, gk_ref, c_ref, sa_ref, sb_ref, o_ref, s_ref):
        i = pl.program_id(0)
        pv = p_ref[...]
        q, kv = pv[:, :ql], pv[:, ql : ql + kl]
        dqn_v, dkvn_v = dq_ref[...], dkv_ref[...]
        rq = _rsq(q)
        nq = q * rq
        rk = _rsq(kv)
        nk = kv * rk
        dkr = dkr_ref[:, :HEAD_PAD]
        for h in range(1, hw // HEAD_PAD):
            dkr = dkr + dkr_ref[:, h * HEAD_PAD : (h + 1) * HEAD_PAD]
        parts = [
            _norm_bwd(dqn_v * gq_ref[...], nq, rq).astype(BF16),
            _norm_bwd(dkvn_v * gk_ref[...], nk, rk).astype(BF16),
            _rope_t(dkr, c_ref[...], sa_ref[...], sb_ref[...]).astype(BF16),
        ]
        if pad:
            parts.append(jnp.zeros((ts, pad), BF16))
        o_ref[...] = jnp.concatenate(parts, axis=1)
        row = [_colsum(dqn_v * nq), _colsum(dkvn_v * nk), jnp.zeros((1, lb - ql - kl), F32)]
        _acc_rows(s_ref, i, [jnp.concatenate(row, axis=1)])

    return pl.pallas_call(
        body,
        name="latent_bwd",
        out_shape=(jax.ShapeDtypeStruct(dproj.shape, BF16), jax.ShapeDtypeStruct((SUBLANES, lb), F32)),
        grid=(s // ts,),
        in_specs=[pl.BlockSpec(memory_space=pl.ANY), _rows(ts, lb), _rows(ts, ql), _rows(ts, kl), _rows(ts, hw)]
        + [_vec(ql), _vec(kl)]
        + [_rows(ts, LANES)] * 3,
        out_specs=[_rows(ts, lb), _sums(lb)],
        input_output_aliases={0: 0},
        compiler_params=_cp("arbitrary"),
    )(dproj, proj, dqn, dkvn, dkr_h, g_q, g_kv, *tabs)


def _conv3(ext, w, b):
    return (pltpu.roll(ext, 2, 0) * w[0:1] + pltpu.roll(ext, 1, 0) * w[1:2]) + ext * w[2:3] + b


def _conv3_t(du, w):
    n = du.shape[0]
    return du * w[2:3] + pltpu.roll(du, n - 1, 0) * w[1:2] + pltpu.roll(du, n - 2, 0) * w[0:1]


def _halo_maps(ts, s):
    r8, last = ts // SUBLANES, s // SUBLANES - 1
    prev = lambda i: jnp.maximum(i * r8 - 1, 0)
    nxt = lambda i: jnp.minimum((i + 1) * r8, last)
    return prev, nxt


def _gate_tile(cwid):
    return _tile(cwid, 512, LANES)


def _mixer_fwd(cat, proj, cw, cb, lb, col0):
    s = proj.shape[0]
    cwid = cw.shape[1]
    ts = _tile(s, 512, SUBLANES)
    tc = _gate_tile(cwid)
    assert lb % (3 * tc) == 0 and col0 % tc == 0
    t0, oc = lb // (3 * tc), col0 // tc
    prev, _ = _halo_maps(ts, s)

    def body(_, g_ref, p_ref, w_ref, b_ref, o_ref):
        keep = jnp.where(pl.program_id(1) > 0, 1.0, 0.0)
        gv, pv = g_ref[...], p_ref[...]
        ext = jnp.concatenate([pv[:, tc : 2 * tc] * pv[:, 2 * tc :] * keep, gv[:, tc : 2 * tc] * gv[:, 2 * tc :]], axis=0)
        o_ref[...] = (gv[:, :tc] * _conv3(ext, w_ref[...], b_ref[...])[SUBLANES:]).astype(BF16)

    return pl.pallas_call(
        body,
        name="mixer_fwd",
        out_shape=jax.ShapeDtypeStruct(cat.shape, BF16),
        grid=(cwid // tc, s // ts),
        in_specs=[
            pl.BlockSpec(memory_space=pl.ANY),
            pl.BlockSpec((ts, 3 * tc), lambda j, i: (i, t0 + j)),
            pl.BlockSpec((SUBLANES, 3 * tc), lambda j, i: (prev(i), t0 + j)),
            pl.BlockSpec((CONV_K, tc), lambda j, i: (0, j)),
            pl.BlockSpec((1, tc), lambda j, i: (0, j)),
        ],
        out_specs=pl.BlockSpec((ts, tc), lambda j, i: (i, oc + j)),
        input_output_aliases={0: 0},
        compiler_params=_cp("parallel", "arbitrary"),
    )(cat, proj, proj, cw, cb)


def _mixer_bwd(dcat, proj, cw, cb, lb, col0):
    s, np_cols = proj.shape
    cwid = cw.shape[1]
    ts = _tile(s, 512, SUBLANES)
    tc = _gate_tile(cwid)
    t0, oc = lb // (3 * tc), col0 // tc
    ni = s // ts
    prev, nxt = _halo_maps(ts, s)

    def body(d_ref, dn_ref, g_ref, gp_ref, gn_ref, w_ref, b_ref, dg_ref, s_ref):
        i = pl.program_id(1)
        keep_p = jnp.where(i > 0, 1.0, 0.0)
        keep_n = jnp.where(i < ni - 1, 1.0, 0.0)
        w = w_ref[...]
        gv, gp, gn = g_ref[...], gp_ref[...], gn_ref[...]
        gc = jnp.concatenate([gp[:, tc : 2 * tc], gv[:, tc : 2 * tc], gn[:, tc : 2 * tc]], axis=0)
        ci = jnp.concatenate([gp[:, 2 * tc :] * keep_p, gv[:, 2 * tc :], gn[:, 2 * tc :]], axis=0)
        u = gc * ci
        cv = _conv3(u, w, b_ref[...])[SUBLANES:]
        dco = jnp.concatenate([d_ref[...], dn_ref[...] * keep_n], axis=0)
        gb = jnp.concatenate([gv[:, :tc], gn[:, :tc]], axis=0)
        dcv = dco * gb
        du = _conv3_t(dcv, w)[:ts]
        dg_ref[:, :tc] = (dco * cv)[:ts].astype(BF16)
        dg_ref[:, tc : 2 * tc] = (du * gv[:, 2 * tc :]).astype(BF16)
        dg_ref[:, 2 * tc :] = (du * gv[:, tc : 2 * tc]).astype(BF16)
        dt = dcv[:ts]
        u1, u2 = pltpu.roll(u, 1, 0), pltpu.roll(u, 2, 0)
        lo, hi = SUBLANES, SUBLANES + ts
        _acc_rows(s_ref, i, [_colsum(dt * u2[lo:hi]), _colsum(dt * u1[lo:hi]), _colsum(dt * u[lo:hi]), _colsum(dt)])

    def triple(rows, which):
        return pl.BlockSpec((rows, 3 * tc), lambda j, i: (which(i), t0 + j))

    return pl.pallas_call(
        body,
        name="mixer_bwd",
        out_shape=(jax.ShapeDtypeStruct((s, np_cols), BF16), jax.ShapeDtypeStruct((SUBLANES, cwid), F32)),
        grid=(cwid // tc, ni),
        in_specs=[
            pl.BlockSpec((ts, tc), lambda j, i: (i, oc + j)),
            pl.BlockSpec((SUBLANES, tc), lambda j, i: (nxt(i), oc + j)),
            triple(ts, lambda i: i), triple(SUBLANES, prev), triple(SUBLANES, nxt),
            pl.BlockSpec((CONV_K, tc), lambda j, i: (0, j)),
            pl.BlockSpec((1, tc), lambda j, i: (0, j)),
        ],
        out_specs=[triple(ts, lambda i: i), pl.BlockSpec((SUBLANES, tc), lambda j, i: (0, j))],
        compiler_params=_cp("parallel", "arbitrary"),
    )(dcat, dcat, proj, proj, proj, cw, cb)


def _pair_tile(f):
    return _tile(f, 1408, LANES)


def _pair_perm(f):
    nj = f // _pair_tile(f)
    return lambda p: (p % 2) * nj + p // 2


def _pair_cols(a):
    r, f2 = a.shape
    tc = _pair_tile(f2 // 2)
    return a.reshape(r, 2, f2 // (2 * tc), tc).transpose(0, 2, 1, 3).reshape(r, f2)


def _unpair_cols(a):
    r, f2 = a.shape
    tc = _pair_tile(f2 // 2)
    return a.reshape(r, f2 // (2 * tc), 2, tc).transpose(0, 2, 1, 3).reshape(r, f2)


def _ffn_act_fwd(up, cw, cb):
    s, f2 = up.shape
    f = f2 // 2
    ts = _tile(s, 512, SUBLANES)
    tc = _pair_tile(f)
    prev, _ = _halo_maps(ts, s)

    def body(u_ref, p_ref, w_ref, b_ref, o_ref):
        keep = jnp.where(pl.program_id(1) > 0, 1.0, 0.0)
        ext = jnp.concatenate([p_ref[...] * keep, u_ref[...]], axis=0)
        u = _conv3(ext, w_ref[...], b_ref[...])[SUBLANES:]
        a, g = u[:, :tc], u[:, tc:]
        o_ref[...] = ((g * jax.nn.sigmoid(g)) * a).astype(BF16)

    def pair(rows, which):
        return pl.BlockSpec((rows, 2 * tc), lambda j, i: (which(i), j))

    return pl.pallas_call(
        body,
        name="ffn_act_fwd",
        out_shape=jax.ShapeDtypeStruct((s, f), BF16),
        grid=(f // tc, s // ts),
        in_specs=[pair(ts, lambda i: i), pair(SUBLANES, prev), pair(CONV_K, lambda i: 0), pair(1, lambda i: 0)],
        out_specs=pl.BlockSpec((ts, tc), lambda j, i: (i, j)),
        compiler_params=_cp("parallel", "arbitrary"),
    )(up, up, cw, cb)


def _ffn_act_bwd(dact, up, cw, cb):
    s, f2 = up.shape
    f = f2 // 2
    ts = _tile(s, 256, SUBLANES)
    tc = _pair_tile(f)
    nj, ni = f // tc, s // ts
    prev, nxt = _halo_maps(ts, s)

    def body(d_ref, dn_ref, u_ref, up_ref, un_ref, w_ref, b_ref, dup_ref, s_ref):
        i = pl.program_id(1)
        keep_p = jnp.where(i > 0, 1.0, 0.0)
        keep_n = jnp.where(i < ni - 1, 1.0, 0.0)
        w = w_ref[...]
        ext = jnp.concatenate([up_ref[...] * keep_p, u_ref[...], un_ref[...]], axis=0)
        u = _conv3(ext, w, b_ref[...])[SUBLANES:]
        a, g = u[:, :tc], u[:, tc:]
        dact_v = jnp.concatenate([d_ref[...], dn_ref[...] * keep_n], axis=0)
        sg = jax.nn.sigmoid(g)
        du = jnp.concatenate([dact_v * (g * sg), dact_v * a * (sg * (1.0 + g * (1.0 - sg)))], axis=1)
        dup_ref[...] = _conv3_t(du, w)[:ts].astype(BF16)
        dt = du[:ts]
        lo, hi = SUBLANES, SUBLANES + ts
        e1, e2 = pltpu.roll(ext, 1, 0), pltpu.roll(ext, 2, 0)
        _acc_rows(s_ref, i, [_colsum(dt * e2[lo:hi]), _colsum(dt * e1[lo:hi]), _colsum(dt * ext[lo:hi]), _colsum(dt)])

    def pair(rows, which):
        return pl.BlockSpec((rows, 2 * tc), lambda j, i: (which(i), j))

    return pl.pallas_call(
        body,
        name="ffn_act_bwd",
        out_shape=(jax.ShapeDtypeStruct((s, f2), BF16), jax.ShapeDtypeStruct((SUBLANES, f2), F32)),
        grid=(nj, ni),
        in_specs=[
            pl.BlockSpec((ts, tc), lambda j, i: (i, j)),
            pl.BlockSpec((SUBLANES, tc), lambda j, i: (nxt(i), j)),
            pair(ts, lambda i: i), pair(SUBLANES, prev), pair(SUBLANES, nxt),
            pair(CONV_K, lambda i: 0), pair(1, lambda i: 0),
        ],
        out_specs=[pair(ts, lambda i: i), pair(SUBLANES, lambda i: 0)],
        compiler_params=_cp("parallel", "arbitrary"),
    )(dact, dact, up, up, up, cw, cb)


ATT_SCALE = 1.0 / math.sqrt(NOPE + ROPE)
LOG2E = math.log2(math.e)
ATT_C2 = ATT_SCALE * LOG2E
STAT_SPLIT = 64
NT = (((1,), (1,)), ((), ()))
TN = (((0,), (0,)), ((), ()))


def _head_cat(q, kv, kr, tabs, n_heads):
    s, w2 = q.shape
    w = w2 // 2
    ts = _tile(s, 512, SUBLANES)
    hd = NOPE + HEAD_PAD

    def body(q_ref, kv_ref, kr_ref, c_ref, sa_ref, sb_ref, qc_ref, kc_ref):
        qv = q_ref[...]
        qr = _rope(qv[:, w:], c_ref[...], sa_ref[...], sb_ref[...]).astype(BF16)
        krv = kr_ref[...]
        for h in range(n_heads):
            qc_ref[:, h * hd : h * hd + NOPE] = qv[:, h * NOPE : (h + 1) * NOPE].astype(BF16)
            qc_ref[:, h * hd + NOPE : (h + 1) * hd] = qr[:, h * HEAD_PAD : (h + 1) * HEAD_PAD]
            kc_ref[:, h * hd : h * hd + NOPE] = kv_ref[:, h * NOPE : (h + 1) * NOPE]
            kc_ref[:, h * hd + NOPE : (h + 1) * hd] = krv

    out = jax.ShapeDtypeStruct((s, n_heads * hd), BF16)
    return pl.pallas_call(
        body,
        name="head_cat",
        out_shape=(out, out),
        grid=(s // ts,),
        in_specs=[_rows(ts, w2), _rows(ts, w), _rows(ts, HEAD_PAD)] + [_rows(ts, LANES)] * 3,
        out_specs=[_rows(ts, n_heads * hd)] * 2,
        compiler_params=_cp("parallel"),
    )(q, kv, kr, *tabs)


def _attn_fwd(qc, kc, kv, n_heads, cat_cols):
    s = qc.shape[0]
    t = _tile(s, ATT_FWD_BLOCK, LANES)
    sub = _tile(t, ATT_FWD_SUB, LANES)
    hh = n_heads
    hd = NOPE + HEAD_PAD

    def body(q_ref, k_ref, v_ref, o_ref, lse_ref, m_s, l_s, acc_s):
        i = pl.program_id(1)
        m_s[...] = jnp.full(m_s.shape, NEG, F32)
        l_s[...] = jnp.zeros(l_s.shape, F32)
        acc_s[...] = jnp.zeros(acc_s.shape, F32)

        def chunk(k0, diag):
            m_all, l_all, acc_all = m_s[...], l_s[...], acc_s[...]
            new_m, new_l, new_acc = [], [], []

            def scores(r0):
                ncol = r0 + sub if diag else t
                return lax.dot_general(q_ref[pl.ds(r0, sub), :], k_ref[pl.ds(k0, ncol), :], NT, preferred_element_type=F32)

            sc_next = scores(0)
            for r0 in range(0, t, sub):
                ncol = r0 + sub if diag else t
                sc = sc_next
                if r0 + sub < t:
                    sc_next = scores(r0 + sub)
                if diag:
                    row = lax.broadcasted_iota(jnp.int32, sc.shape, 0) + r0
                    col = lax.broadcasted_iota(jnp.int32, sc.shape, 1)
                    sc = jnp.where(col <= row, sc, NEG)
                m_prev = m_all[r0 : r0 + sub]
                m_new = jnp.maximum(m_prev, jnp.max(sc, axis=1, keepdims=True))
                alpha = jnp.exp2((m_prev - m_new) * ATT_C2)
                p = jnp.exp2((sc - m_new) * ATT_C2)
                pv = jnp.dot(p.astype(BF16), v_ref[pl.ds(k0, ncol), :], preferred_element_type=F32)
                new_m.append(m_new)
                new_l.append(alpha * l_all[r0 : r0 + sub] + jnp.sum(p, axis=1, keepdims=True))
                new_acc.append(alpha * acc_all[r0 : r0 + sub] + pv)
            m_s[...] = jnp.concatenate(new_m, axis=0)
            l_s[...] = jnp.concatenate(new_l, axis=0)
            acc_s[...] = jnp.concatenate(new_acc, axis=0)

        def loop_body(k, carry):
            chunk(pl.multiple_of(k * t, t), False)
            return carry

        lax.fori_loop(0, i, loop_body, 0)
        chunk(pl.multiple_of(i * t, t), True)
        l = l_s[...]
        o_ref[...] = (acc_s[...] / l).astype(BF16)
        lse_ref[...] = jnp.broadcast_to(m_s[...] * ATT_C2 + jnp.log(l) * LOG2E, lse_ref.shape)

    return pl.pallas_call(
        body,
        name="attn_fwd",
        out_shape=(jax.ShapeDtypeStruct((s, cat_cols), BF16), jax.ShapeDtypeStruct((s, hh * LANES), F32)),
        grid=(hh, s // t),
        in_specs=[
            pl.BlockSpec((t, hd), lambda h, i: (i, h)),
            pl.BlockSpec((s, hd), lambda h, i: (0, h)),
            pl.BlockSpec((s, VDIM), lambda h, i: (0, hh + h)),
        ],
        out_specs=[pl.BlockSpec((t, VDIM), lambda h, i: (i, h)), pl.BlockSpec((t, LANES), lambda h, i: (i, h))],
        scratch_shapes=[pltpu.VMEM((t, 1), F32), pltpu.VMEM((t, 1), F32), pltpu.VMEM((t, VDIM), F32)],
        compiler_params=_cp("parallel", "parallel"),
    )(qc, kc, kv)


def _attn_bwd_prep(cat, dcat, lse2, n_heads):
    s, w = lse2.shape
    ts = _tile(s, 512, SUBLANES)

    def body(o_ref, do_ref, lse_ref, dob_ref, st_ref):
        do = do_ref[...]
        dob_ref[...] = do.astype(BF16)
        prod = do * o_ref[...].astype(F32)
        lane = lax.broadcasted_iota(jnp.int32, (ts, LANES), 1)
        for h in range(n_heads):
            cols = slice(h * LANES, (h + 1) * LANES)
            dsum = jnp.sum(prod[:, cols], axis=1, keepdims=True)
            st_ref[:, cols] = jnp.where(lane < STAT_SPLIT, lse_ref[:, cols], dsum)

    return pl.pallas_call(
        body,
        name="attn_bwd_prep",
        out_shape=(jax.ShapeDtypeStruct((s, w), BF16), jax.ShapeDtypeStruct((s, w), F32)),
        grid=(s // ts,),
        in_specs=[_rows(ts, w)] * 3,
        out_specs=[_rows(ts, w)] * 2,
        compiler_params=_cp("parallel"),
    )(cat, dcat, lse2)


def _attn_bwd(qc, kc, kv, dob, stats, n_heads):
    s = qc.shape[0]
    t = _tile(s, ATT_BWD_BLOCK, LANES)
    sub = _tile(t, ATT_BWD_SUB, LANES)
    nb = s // t
    hh = n_heads
    hd = NOPE + HEAD_PAD
    w = hh * LANES

    def body(q_ref, k_ref, v_ref, do_ref, st_ref, dq_ref, dkn_ref, dv_ref, dkr_ref, dk_s, dv_s):
        j = pl.program_id(1)

        @pl.when(j == 0)
        def _():
            dq_ref[...] = jnp.zeros(dq_ref.shape, F32)

        dk_s[...] = jnp.zeros(dk_s.shape, F32)
        dv_s[...] = jnp.zeros(dv_s.shape, F32)

        def pair(i0, diag):
            def width(r0):
                return r0 + sub if diag else t

            def products(r0):
                rows = pl.ds(i0 + r0, sub)
                sc_ = lax.dot_general(q_ref[rows, :], k_ref[0 : width(r0), :], NT, preferred_element_type=F32)
                dp_ = lax.dot_general(do_ref[rows, :], v_ref[0 : width(r0), :], NT, preferred_element_type=F32)
                return sc_, dp_

            nxt = products(0)
            for r0 in range(0, t, sub):
                ncol = width(r0)
                rows = pl.ds(i0 + r0, sub)
                kk = k_ref[0:ncol, :]
                qq, do, st = q_ref[rows, :], do_ref[rows, :], st_ref[rows, :]
                sc, dp = nxt
                if r0 + sub < t:
                    nxt = products(r0 + sub)
                if diag:
                    row = lax.broadcasted_iota(jnp.int32, sc.shape, 0) + r0
                    col = lax.broadcasted_iota(jnp.int32, sc.shape, 1)
                    sc = jnp.where(col <= row, sc, NEG)
                p = jnp.exp2(sc * ATT_C2 - st[:, 0:1])
                dv_s[0:ncol, :] += lax.dot_general(p.astype(BF16), do, TN, preferred_element_type=F32)
                ds = (p * (dp - st[:, STAT_SPLIT : STAT_SPLIT + 1]) * ATT_SCALE).astype(BF16)
                dk_s[0:ncol, :] += lax.dot_general(ds, qq, TN, preferred_element_type=F32)
                dq_ref[rows, :] += jnp.dot(ds, kk, preferred_element_type=F32)

        pair(pl.multiple_of(j * t, t), True)

        def loop_body(i, carry):
            pair(pl.multiple_of(i * t, t), False)
            return carry

        lax.fori_loop(j + 1, nb, loop_body, 0)
        dkn_ref[...] = dk_s[:, :NOPE].astype(BF16)
        dv_ref[...] = dv_s[...].astype(BF16)
        dkr_ref[...] = dk_s[:, NOPE:]

    whole = lambda width, off: pl.BlockSpec((s, width), lambda h, j: (0, off + h))
    blk = lambda width, off: pl.BlockSpec((t, width), lambda h, j: (j, off + h))
    return pl.pallas_call(
        body,
        name="attn_bwd",
        out_shape=(
            jax.ShapeDtypeStruct((s, hh * hd), F32),
            jax.ShapeDtypeStruct((s, w), BF16),
            jax.ShapeDtypeStruct((s, w), BF16),
            jax.ShapeDtypeStruct((s, w), F32),
        ),
        grid=(hh, nb),
        in_specs=[whole(hd, 0), blk(hd, 0), blk(VDIM, hh), whole(VDIM, 0), whole(LANES, 0)],
        out_specs=[whole(hd, 0), blk(NOPE, 0), blk(VDIM, 0), blk(HEAD_PAD, 0)],
        scratch_shapes=[pltpu.VMEM((t, hd), F32), pltpu.VMEM((t, VDIM), F32)],
        compiler_params=_cp("parallel", "arbitrary"),
    )(qc, kc, kv, dob, stats)


def _dq_unrope(dq, tabs, n_heads):
    s = dq.shape[0]
    hd = NOPE + HEAD_PAD
    w = n_heads * LANES
    ts = _tile(s, 512, SUBLANES)

    def body(d_ref, c_ref, sa_ref, sb_ref, o_ref):
        c, sa, sb = c_ref[...], sa_ref[...], sb_ref[...]
        for h in range(n_heads):
            o_ref[:, h * NOPE : (h + 1) * NOPE] = d_ref[:, h * hd : h * hd + NOPE].astype(BF16)
            rot = _rope_t(d_ref[:, h * hd + NOPE : (h + 1) * hd], c, sa, sb)
            o_ref[:, w + h * HEAD_PAD : w + (h + 1) * HEAD_PAD] = rot.astype(BF16)

    return pl.pallas_call(
        body,
        name="dq_unrope",
        out_shape=jax.ShapeDtypeStruct((s, 2 * w), BF16),
        grid=(s // ts,),
        in_specs=[_rows(ts, n_heads * hd)] + [_rows(ts, LANES)] * 3,
        out_specs=_rows(ts, 2 * w),
        compiler_params=_cp("parallel"),
    )(dq, *tabs)


def _adamw(w, m, v, grads, name):
    r, c = w.shape
    budget_rows = max(SUBLANES, (VMEM_LIMIT // 3) // (4 * c * 2 * (7 + len(grads))))
    tr = _tile(r, budget_rows, SUBLANES)
    ng = len(grads)
    c1 = 1.0 - ADAM_B1**ADAM_STEP
    c2 = 1.0 - ADAM_B2**ADAM_STEP

    def body(*refs):
        w_ref, m_ref, v_ref = refs[:3]
        g_ref, d_ref, nm_ref, nv_ref = refs[3 + ng :]
        g = refs[3][...]
        for extra in refs[4 : 3 + ng]:
            g = g + extra[...]
        mn = ADAM_B1 * m_ref[...] + (1.0 - ADAM_B1) * g
        vn = ADAM_B2 * v_ref[...] + (1.0 - ADAM_B2) * (g * g)
        g_ref[...] = g
        nm_ref[...] = mn
        nv_ref[...] = vn
        d_ref[...] = -ADAM_LR * ((mn / c1) / (jnp.sqrt(vn / c2) + ADAM_EPS) + ADAM_WD * w_ref[...])

    blk = pl.BlockSpec((tr, c), lambda i: (i, 0))
    out = jax.ShapeDtypeStruct((r, c), F32)
    return pl.pallas_call(
        body,
        name=name,
        out_shape=(out, out, out, out),
        grid=(r // tr,),
        in_specs=[blk] * (3 + ng),
        out_specs=[blk] * 4,
        compiler_params=_cp("parallel"),
    )(w, m, v, *grads)


def _ada_grad(ca_t, dm):
    d = ca_t.shape[0]
    nc = dm.shape[1]
    tn = _tile(nc, 512, LANES)

    def body(a_ref, b_ref, o_ref):
        o_ref[...] = jnp.dot(a_ref[...].astype(BF16), b_ref[...].astype(BF16), preferred_element_type=F32)

    return pl.pallas_call(
        body,
        name="ada_grad",
        out_shape=jax.ShapeDtypeStruct((d, nc), F32),
        grid=(nc // tn,),
        in_specs=[pl.BlockSpec((d, LANES), lambda j: (0, 0)), pl.BlockSpec((LANES, tn), lambda j: (0, j))],
        out_specs=pl.BlockSpec((d, tn), lambda j: (0, j)),
        compiler_params=_cp("parallel"),
    )(ca_t, dm)


def _sum_devices(g):
    n = g.shape[1]

    def body(g_ref, o_ref):
        acc = g_ref[0:SUBLANES, :]
        for dvc in range(1, N_DEV):
            acc = acc + g_ref[dvc * SUBLANES : (dvc + 1) * SUBLANES, :]
        o_ref[...] = acc

    return pl.pallas_call(
        body,
        name="sum_devices",
        out_shape=jax.ShapeDtypeStruct((SUBLANES, n), F32),
        in_specs=[pl.BlockSpec(memory_space=pltpu.VMEM)],
        out_specs=pl.BlockSpec(memory_space=pltpu.VMEM),
        compiler_params=pltpu.CompilerParams(vmem_limit_bytes=VMEM_LIMIT),
    )(g)


def _sum_chips(land, sent, name):
    _, r, c = land.shape
    tr = _tile(r, max(SUBLANES * 2, (VMEM_LIMIT // 4) // (c * 2 * (4 * N_CHIP + 4 * 2))), SUBLANES * 2)

    def body(l_ref, s_ref, o_ref):
        x, y, _ = _mesh_pos()
        me = 2 * x + y
        acc = jnp.where(me == 0, s_ref[0], l_ref[0]).astype(F32)
        for k in range(1, N_CHIP):
            acc = acc + jnp.where(me == k, s_ref[k], l_ref[k]).astype(F32)
        o_ref[...] = acc

    slots = pl.BlockSpec((N_CHIP, tr, c), lambda i: (0, i, 0))
    return pl.pallas_call(
        body,
        name=name,
        out_shape=jax.ShapeDtypeStruct((r, c), F32),
        grid=(r // tr,),
        in_specs=[slots, slots],
        out_specs=pl.BlockSpec((tr, c), lambda i: (i, 0)),
        compiler_params=_cp("parallel"),
    )(land, sent)


def _mesh_pos():
    return lax.axis_index("x"), lax.axis_index("y"), lax.axis_index("c")


def _other_chips(x, y):
    return [(1 - x, y), (x, 1 - y), (1 - x, 1 - y)]


def _all_gather8(x_shard, name):
    m_per, n = x_shard.shape

    def body(x_ref, out_ref, send_sems, recv_sems, local_sem):
        x, y, c = _mesh_pos()
        me, sibling = (x, y, c), (x, y, 1 - c)
        chips = _other_chips(x, y)

        def rows(px, py, pc):
            return out_ref.at[pl.ds((4 * px + 2 * py + pc) * m_per, m_per), :]

        def copy(k, block, to, src=None):
            return pltpu.make_async_remote_copy(
                src_ref=rows(*block) if src is None else src,
                dst_ref=rows(*block),
                send_sem=send_sems.at[k],
                recv_sem=recv_sems.at[k],
                device_id=to,
                device_id_type=MESH,
            )

        mine = pltpu.make_async_copy(x_ref, rows(*me), local_sem)
        mine.start()
        first = [copy(0, me, sibling, src=x_ref)]
        first += [copy(1 + j, me, (*chip, c), src=x_ref) for j, chip in enumerate(chips)]
        for cp in first:
            cp.start()
        passed = [copy(4 + j, (*chip, c), sibling) for j, chip in enumerate(chips)]
        for j, chip in enumerate(chips):
            copy(1 + j, (*chip, c), me).wait_recv()
            passed[j].start()
        copy(0, sibling, me).wait_recv()
        for j, chip in enumerate(chips):
            copy(4 + j, (*chip, 1 - c), me).wait_recv()
        for cp in first + passed:
            cp.wait_send()
        mine.wait()

    return pl.pallas_call(
        body,
        name=name,
        out_shape=jax.ShapeDtypeStruct((N_DEV * m_per, n), x_shard.dtype),
        in_specs=[pl.BlockSpec(memory_space=pltpu.VMEM)],
        out_specs=pl.BlockSpec(memory_space=pltpu.VMEM),
        scratch_shapes=[pltpu.SemaphoreType.DMA((7,)), pltpu.SemaphoreType.DMA((7,)), pltpu.SemaphoreType.DMA],
        compiler_params=pltpu.CompilerParams(vmem_limit_bytes=VMEM_LIMIT),
    )(x_shard)


HBM_SPEC = pl.BlockSpec(memory_space=pltpu.HBM)
SEM_SPEC = pl.BlockSpec(memory_space=pltpu.SEMAPHORE)
DATAFLOW = pltpu.SideEffectType.DATAFLOW_SIDE_EFFECTING


def _half_rows(n_rows, c):
    return pl.ds(c * (n_rows // 2), n_rows // 2)


def _exchange_copies(ins, lands, send_sems, recv_sems, scatter, halves=False):
    x, y, c = _mesh_pos()
    me = 2 * x + y
    sends, recvs = [], []
    for t in range(len(ins)):
        rows = _half_rows(ins[t].shape[0], c) if halves else slice(None)
        for r, (px, py) in enumerate(_other_chips(x, y)):
            peer = 2 * px + py

            def copy(src, dst, k=3 * t + r, to=(px, py, c)):
                return pltpu.make_async_remote_copy(
                    src_ref=src, dst_ref=dst, send_sem=send_sems.at[k], recv_sem=recv_sems.at[k], device_id=to, device_id_type=MESH
                )

            if scatter:
                sends.append(copy(ins[t].at[peer], lands[t].at[me]))
                recvs.append(copy(ins[t].at[me], lands[t].at[peer]))
            else:
                sends.append(copy(ins[t].at[rows], lands[t].at[me, rows]))
                recvs.append(copy(ins[t].at[rows], lands[t].at[peer, rows]))
    return sends, recvs


def _sibling_fill(lands, name):
    nt = len(lands)

    def body(*refs):
        outs, send_sems, recv_sems = refs[nt : 2 * nt], refs[2 * nt], refs[2 * nt + 1]
        x, y, c = _mesh_pos()
        sends, recvs = [], []
        for t in range(nt):
            mine, theirs = _half_rows(outs[t].shape[1], c), _half_rows(outs[t].shape[1], 1 - c)
            for r, (px, py) in enumerate(_other_chips(x, y)):
                slot = 2 * px + py

                def copy(rows, k=3 * t + r, zone=outs[t], slot=slot):
                    part = zone.at[slot, rows]
                    return pltpu.make_async_remote_copy(
                        src_ref=part, dst_ref=part, send_sem=send_sems.at[k], recv_sem=recv_sems.at[k],
                        device_id=(x, y, 1 - c), device_id_type=MESH,
                    )

                sends.append(copy(mine))
                recvs.append(copy(theirs))
        for cp in sends:
            cp.start()
        for cp in recvs:
            cp.wait_recv()
        for cp in sends:
            cp.wait_send()

    return pl.pallas_call(
        body,
        name=name,
        out_shape=tuple(jax.ShapeDtypeStruct(a.shape, a.dtype) for a in lands),
        in_specs=[pl.BlockSpec(memory_space=pl.ANY)] * nt,
        out_specs=[pl.BlockSpec(memory_space=pl.ANY)] * nt,
        input_output_aliases={t: t for t in range(nt)},
        scratch_shapes=[pltpu.SemaphoreType.DMA((3 * nt,)), pltpu.SemaphoreType.DMA((3 * nt,))],
    )(*lands)


def _exchange_start(arrs, scatter, name, halves=False):
    nt = len(arrs)
    lands = [lax.empty(a.shape if scatter else (N_CHIP, *a.shape), a.dtype) for a in arrs]

    def body(*refs):
        ins, zones = refs[:nt], refs[nt : 2 * nt]
        send_sems, recv_sems, token = refs[2 * nt], refs[2 * nt + 1], refs[-1]
        sends, _ = _exchange_copies(ins, zones, send_sems, recv_sems, scatter, halves)
        for cp in sends:
            cp.start()
        token[...] = jnp.zeros(token.shape, F32)

    bufs = list(arrs) + list(lands)
    return pl.pallas_call(
        body,
        name=name,
        out_shape=(
            pltpu.SemaphoreType.DMA((3 * nt,)),
            pltpu.SemaphoreType.DMA((3 * nt,)),
            *[pltpu.HBM(a.shape, a.dtype) for a in bufs],
            jax.ShapeDtypeStruct((SUBLANES, LANES), F32),
        ),
        in_specs=[HBM_SPEC] * (2 * nt),
        out_specs=(SEM_SPEC, SEM_SPEC, *[HBM_SPEC] * (2 * nt), pl.BlockSpec(memory_space=pltpu.VMEM)),
        input_output_aliases={k: 2 + k for k in range(2 * nt)},
        compiler_params=pltpu.CompilerParams(has_side_effects=DATAFLOW),
    )(*[pltpu.with_memory_space_constraint(a, pltpu.HBM) for a in bufs])


def _exchange_wait(state, after, scatter, name, halves=False):
    send_sems, recv_sems, *bufs = state[:-1]
    nt = len(bufs) // 2
    afters = list(after) if isinstance(after, (list, tuple)) else [after]

    def body(*refs):
        ins, zones = refs[:nt], refs[nt : 2 * nt]
        sends, recvs = _exchange_copies(ins, zones, refs[2 * nt], refs[2 * nt + 1], scatter, halves)
        for cp in sends:
            cp.wait_send()
        for cp in recvs:
            cp.wait_recv()

    out = pl.pallas_call(
        body,
        name=name,
        out_shape=tuple(pltpu.HBM(a.shape, a.dtype) for a in bufs),
        in_specs=[HBM_SPEC] * (2 * nt) + [SEM_SPEC, SEM_SPEC] + [pl.BlockSpec(memory_space=pl.ANY)] * len(afters),
        out_specs=[HBM_SPEC] * (2 * nt),
        input_output_aliases={k: k for k in range(2 * nt)},
        compiler_params=pltpu.CompilerParams(has_side_effects=DATAFLOW),
    )(*bufs, send_sems, recv_sems, *afters)
    return list(out[:nt]), list(out[nt:])


def _swap_copies(ins, lands, send_sems, recv_sems):
    x, y, c = _mesh_pos()
    return [
        pltpu.make_async_remote_copy(
            src_ref=ins[t], dst_ref=lands[t], send_sem=send_sems.at[t], recv_sem=recv_sems.at[t],
            device_id=(x, y, 1 - c), device_id_type=MESH,
        )
        for t in range(len(ins))
    ]


def _swap_start(arrs, name):
    nt = len(arrs)
    lands = [lax.empty(a.shape, a.dtype) for a in arrs]

    def body(*refs):
        ins, zones = refs[:nt], refs[nt : 2 * nt]
        send_sems, recv_sems, token = refs[2 * nt], refs[2 * nt + 1], refs[-1]
        for cp in _swap_copies(ins, zones, send_sems, recv_sems):
            cp.start()
        token[...] = jnp.zeros(token.shape, F32)

    bufs = list(arrs) + lands
    return pl.pallas_call(
        body,
        name=name,
        out_shape=(
            pltpu.SemaphoreType.DMA((nt,)),
            pltpu.SemaphoreType.DMA((nt,)),
            *[pltpu.HBM(a.shape, a.dtype) for a in bufs],
            jax.ShapeDtypeStruct((SUBLANES, LANES), F32),
        ),
        in_specs=[HBM_SPEC] * (2 * nt),
        out_specs=(SEM_SPEC, SEM_SPEC, *[HBM_SPEC] * (2 * nt), pl.BlockSpec(memory_space=pltpu.VMEM)),
        input_output_aliases={k: 2 + k for k in range(2 * nt)},
        compiler_params=pltpu.CompilerParams(has_side_effects=DATAFLOW),
    )(*[pltpu.with_memory_space_constraint(a, pltpu.HBM) for a in bufs])


def _swap_wait(state, after, name):
    send_sems, recv_sems, *bufs = state[:-1]
    nt = len(bufs) // 2

    def body(*refs):
        cps = _swap_copies(refs[:nt], refs[nt : 2 * nt], refs[2 * nt], refs[2 * nt + 1])
        for cp in cps:
            cp.wait_send()
        for cp in cps:
            cp.wait_recv()

    out = pl.pallas_call(
        body,
        name=name,
        out_shape=tuple(pltpu.HBM(a.shape, a.dtype) for a in bufs),
        in_specs=[HBM_SPEC] * (2 * nt) + [SEM_SPEC, SEM_SPEC, pl.BlockSpec(memory_space=pl.ANY)],
        out_specs=[HBM_SPEC] * (2 * nt),
        input_output_aliases={k: k for k in range(2 * nt)},
        compiler_params=pltpu.CompilerParams(has_side_effects=DATAFLOW),
    )(*bufs, send_sems, recv_sems, after)
    return list(out[:nt]), list(out[nt:])


def _cols_from_shards(g):
    _, k, n = g.shape
    return jnp.transpose(g, (1, 0, 2)).reshape(k, N_CHIP * n)


def _cols_to_shards(a):
    k, n4 = a.shape
    return jnp.transpose(a.reshape(k, N_CHIP, n4 // N_CHIP), (1, 0, 2))


def _pad_to(vec, mult):
    n = vec.shape[0]
    return jnp.pad(vec, (0, (-n) % mult))


def kernel(x, c, positions, w_ada, b_ada, g_pre_mix, g_post_mix, w_in, g_q, w_uq, g_kv, w_ukv, conv_w_mix, conv_b_mix, w_o, g_pre_ffn, g_post_ffn, w_up, conv_w_ffn, conv_b_ffn, w_down, loss_target, m_w_ada, m_b_ada, m_g_pre_mix, m_g_post_mix, m_w_in, m_g_q, m_w_uq, m_g_kv, m_w_ukv, m_conv_w_mix, m_conv_b_mix, m_w_o, m_g_pre_ffn, m_g_post_ffn, m_w_up, m_conv_w_ffn, m_conv_b_ffn, m_w_down, v_w_ada, v_b_ada, v_g_pre_mix, v_g_post_mix, v_w_in, v_g_q, v_w_uq, v_g_kv, v_w_ukv, v_conv_w_mix, v_conv_b_mix, v_w_o, v_g_pre_ffn, v_g_post_ffn, v_w_up, v_conv_w_ffn, v_conv_b_ffn, v_w_down):
    xi, yi, ci = _mesh_pos()
    chip = 2 * xi + yi
    dev = 4 * xi + 2 * yi + ci

    s, d = x.shape[1], x.shape[2]
    ql, kl = g_q.shape[1], g_kv.shape[1]
    cwid = conv_b_mix.shape[1]
    f2 = conv_b_ffn.shape[1]
    hh = (w_uq.shape[2] * N_CHIP) // (NOPE + ROPE)
    w_att = hh * LANES
    nc_ada = w_ada.shape[2]
    lat = ql + kl + ROPE
    tc_mix = _gate_tile(cwid)
    lb = -(-(ql + kl + HEAD_PAD) // (3 * tc_mix)) * (3 * tc_mix)
    np_cols = lb + 3 * cwid
    assert cwid == hh * VDIM and w_att % tc_mix == 0

    x0 = x.reshape(s, d)
    tgt = loss_target.reshape(s, d)

    anchors = []

    def _behind(val, state):
        val, tok = lax.optimization_barrier((val, state[-1]))
        anchors.append(tok[0, 0])
        return val

    cwm_n, cwf_n = CONV_K * cwid // N_CHIP, CONV_K * f2 // N_CHIP
    pack_a = _pad_to(jnp.concatenate([c.reshape(-1), conv_w_mix.reshape(-1), conv_w_ffn.reshape(-1)]), SUBLANES * LANES)
    rows_a = _all_gather8(pack_a.reshape(SUBLANES, -1), "ag8_inputs").reshape(N_DEV, -1)
    c_all = rows_a[:, :d]
    south = rows_a[0::2]
    cw_mix = jnp.concatenate([south[j, d : d + cwm_n].reshape(CONV_K, -1) for j in range(N_CHIP)], axis=1)
    cw_ffn = jnp.concatenate([south[j, d + cwm_n : d + cwm_n + cwf_n].reshape(CONV_K, -1) for j in range(N_CHIP)], axis=1)

    b_cols = lax.dynamic_slice(b_ada, (0, chip * nc_ada), (1, nc_ada))
    mod_part, c_act = _ada_fwd(c_all, w_ada[0], b_cols)
    mod_rows = _all_gather8(mod_part, "ag8_mod")
    mod = jnp.concatenate(
        [lax.dynamic_slice_in_dim(mod_rows, 2 * N_DEV * j + dev, 1, axis=0) for j in range(N_CHIP)], axis=1
    )

    shards = [a[0].astype(BF16) for a in (w_in, w_uq, w_ukv, w_o, w_up, w_down)]
    first, mod = lax.optimization_barrier((shards[:3], mod))
    ag_a = _exchange_start(first, False, "ag_a_start", halves=True)
    mod = _behind(mod, ag_a)
    sh_m, sc_m, gt_m, sh_f, sc_f, gt_f = [mod[:, k * d : (k + 1) * d] for k in range(N_MOD)]

    inv_freq = 1.0 / (ROPE_THETA ** (jnp.arange(0, ROPE, 2, dtype=F32) / ROPE))
    invf = jnp.concatenate([inv_freq, inv_freq, jnp.zeros((LANES - ROPE,), F32)]).reshape(1, LANES)
    tabs = _rope_tables(positions.astype(F32).reshape(s, 1), invf)
    h1 = _pre_fwd(x0, g_pre_mix, sc_m, sh_m)

    def with_own(landed, own):
        return [lax.dynamic_update_slice_in_dim(g, a[None], chip, axis=0) for g, a in zip(landed, own)]

    own_w, landed_w = _exchange_wait(ag_a, [h1, tabs[0]], False, "ag_a_wait", halves=True)
    landed_w = list(_sibling_fill(landed_w, "ag_a_fill"))
    rest, landed_w = lax.optimization_barrier((shards[3:], landed_w))
    ag_b = _exchange_start(rest, False, "ag_b_start")
    h1 = _behind(h1, ag_b)
    g_in, g_uq, g_ukv = with_own(landed_w, own_w)
    full_in = _cols_from_shards(g_in)
    gate_cols = [(lat + k * cwid + j * tc_mix, tc_mix) for j in range(cwid // tc_mix) for k in range(3)]
    w_in_p = jnp.concatenate(
        [full_in[:, :lat], jnp.zeros((d, lb - lat), BF16)] + [full_in[:, o : o + n] for o, n in gate_cols], axis=1
    )
    full_uq = _cols_from_shards(g_uq).reshape(ql, hh, NOPE + ROPE)
    w_uq_p = jnp.concatenate(
        [
            full_uq[:, :, :NOPE].reshape(ql, w_att),
            jnp.pad(full_uq[:, :, NOPE:], ((0, 0), (0, 0), (0, HEAD_PAD - ROPE))).reshape(ql, w_att),
        ],
        axis=1,
    )
    full_ukv = _cols_from_shards(g_ukv).reshape(kl, hh, NOPE + VDIM)
    w_ukv_p = jnp.concatenate([full_ukv[:, :, :NOPE].reshape(kl, w_att), full_ukv[:, :, NOPE:].reshape(kl, w_att)], axis=1)

    proj = _matmul(h1, w_in_p, out_dtype=F32, tm=1024, tn=768, tk=2048, name="mm_proj", n_major=True)
    qn, kvn, kr = _latent_fwd(proj, g_q, g_kv, tabs, lb)
    q_f = _matmul(qn, w_uq_p, out_dtype=F32, tm=1024, tn=1024, tk=2048, name="mm_q")
    kv_p = _matmul(kvn, w_ukv_p, out_dtype=BF16, tm=1024, tn=1024, tk=2048, name="mm_kv")
    q_c, k_c = _head_cat(q_f, kv_p, kr, tabs, hh)
    cat, lse2 = _attn_fwd(q_c, k_c, kv_p, hh, w_att + cwid)
    cat = _mixer_fwd(cat, proj, cw_mix, conv_b_mix, lb, w_att)
    own_w, landed_w = _exchange_wait(ag_b, cat, False, "ag_b_wait")
    g_o, g_up, g_down = with_own(landed_w, own_w)
    w_o_f = g_o.reshape(-1, d)
    cw_ffn_p, cb_ffn_p = _pair_cols(cw_ffn), _pair_cols(conv_b_ffn)
    tcp, pair_perm = _pair_tile(f2 // 2), _pair_perm(f2 // 2)
    w_down_f = g_down.reshape(-1, d)
    mix = _matmul(cat, w_o_f, out_dtype=F32, tm=1024, tn=1024, tk=2048, name="mm_mix")

    x1, h2 = _mid_fwd(x0, mix, g_post_mix, gt_m, g_pre_ffn, sc_f, sh_f)
    up = _matmul(
        h2, g_up, out_dtype=F32, tm=1024, tn=tcp, tk=2048, name="mm_up", b_n_perm=pair_perm, b_col_shards=True, n_major=True
    )
    act = _ffn_act_fwd(up, cw_ffn_p, cb_ffn_p)
    y = _matmul(act, w_down_f, out_dtype=F32, tm=512, tn=1024, tk=5632, name="mm_down", n_major=True)
    dx2, dy, s_fin = _final(x1, y, tgt, g_post_ffn, gt_f)

    dw_down = _matmul(act, dy, ta=True, out_dtype=BF16, tm=512, tn=2048, tk=4096, name="mm_dw_down")
    dact = _matmul(dy, w_down_f, tb=True, out_dtype=F32, tm=1024, tn=1408, tk=2048, name="mm_dact", n_major=True)
    dup, s_ffn_p = _ffn_act_bwd(dact, up, cw_ffn_p, cb_ffn_p)
    s_ffn = _unpair_cols(s_ffn_p)
    dw_up = _matmul(
        h2, dup, ta=True, out_dtype=BF16, tm=512, tn=tcp, tk=4096, name="mm_dw_up", out_n_perm=pair_perm, out_col_shards=True
    )
    dh2 = _matmul_pair_k(dup, g_up, out_dtype=F32, tm=512, tn=1024, pairs=2, name="mm_dh2")
    dx1, dmix, s_mid = _mid_bwd(dh2, dx2, x1, mix, g_pre_ffn, sc_f, g_post_mix, gt_m)

    dw_o = _matmul(cat, dmix, ta=True, out_dtype=BF16, tm=512, tn=1024, tk=4096, name="mm_dw_o")
    send_b = [dw_o.reshape(N_CHIP, -1, d), dw_up, dw_down.reshape(N_CHIP, -1, d)]
    rs_b = _exchange_start(send_b, True, "rs_b_start")
    dmix = _behind(dmix, rs_b)
    dcat = _matmul(dmix, w_o_f, tb=True, out_dtype=F32, tm=1024, tn=1024, tk=2048, name="mm_dcat")
    dproj, s_mix = _mixer_bwd(dcat, proj, cw_mix, conv_b_mix, lb, w_att)
    dob, stats = _attn_bwd_prep(cat, dcat, lse2, hh)
    dq_raw, dkv_k, dkv_v, dkr_h = _attn_bwd(q_c, k_c, kv_p, dob, stats, hh)
    dkv_p = jnp.concatenate([dkv_k, dkv_v], axis=1)
    dq_p = _dq_unrope(dq_raw, tabs, hh)
    dw_uq_p = _matmul(qn, dq_p, ta=True, out_dtype=BF16, tm=1024, tn=1024, tk=1024, name="mm_dw_uq")
    dqn = _matmul(dq_p, w_uq_p, tb=True, out_dtype=F32, tm=1024, tn=1024, tk=2048, name="mm_dqn")
    dw_ukv_p = _matmul(kvn, dkv_p, ta=True, out_dtype=BF16, tm=1024, tn=1024, tk=1024, name="mm_dw_ukv")
    dkvn = _matmul(dkv_p, w_ukv_p, tb=True, out_dtype=F32, tm=1024, tn=1024, tk=2048, name="mm_dkvn")
    dproj, s_lat = _latent_bwd(dproj, proj, dqn, dkvn, dkr_h, g_q, g_kv, tabs, lb)
    dw_in_p = _matmul(h1, dproj, ta=True, out_dtype=BF16, tm=512, tn=1536, tk=4096, name="mm_dw_in")

    n_trip = cwid // tc_mix
    ungate = [lb + (3 * j + k) * tc_mix for k in range(3) for j in range(n_trip)]
    dw_in_f = jnp.concatenate([dw_in_p[:, :lat]] + [dw_in_p[:, o : o + tc_mix] for o in ungate], axis=1)
    uq3 = dw_uq_p.reshape(ql, 2, hh, LANES)
    dw_uq_f = jnp.concatenate([uq3[:, 0], uq3[:, 1, :, :ROPE]], axis=2).reshape(ql, hh * (NOPE + ROPE))
    ukv3 = dw_ukv_p.reshape(kl, 2, hh, LANES)
    dw_ukv_f = jnp.concatenate([ukv3[:, 0], ukv3[:, 1]], axis=2).reshape(kl, hh * (NOPE + VDIM))
    send_a = [_cols_to_shards(dw_in_f), _cols_to_shards(dw_uq_f), _cols_to_shards(dw_ukv_f)]
    rs_a = _exchange_start(send_a, True, "rs_a_start")
    dproj = _behind(dproj, rs_a)

    dh1 = _matmul(dproj, w_in_p, tb=True, out_dtype=F32, tm=512, tn=1024, tk=4608, name="mm_dh1", n_major=True)
    grad_x, s_first = _first_bwd(dh1, dx1, x0, g_pre_mix, sc_m)

    names = ["w_in", "w_uq", "w_ukv", "w_o", "w_up", "w_down"]
    sent_b, landed_b = _exchange_wait(rs_b, s_first, True, "rs_b_wait")
    sent_a, landed_a = _exchange_wait(rs_a, landed_b[0], True, "rs_a_wait")
    landed_a, s_first = lax.optimization_barrier((landed_a, s_first))
    part = [_sum_chips(l, a, "sum_chips_" + n) for l, a, n in zip(landed_a + landed_b, sent_a + sent_b, names)]

    dmod = jnp.concatenate([s_first[0:1], s_first[1:2], s_mid[3:4], s_mid[0:1], s_mid[1:2], s_fin[0:1]], axis=1)
    small = [
        dmod,
        s_first[2:3],
        s_mid[4:5],
        s_lat[0:1, :ql],
        s_lat[0:1, ql : ql + kl],
        s_mix[3:4],
        s_mid[2:3],
        s_fin[1:2],
        s_ffn[3:4],
        s_mix[0:3].reshape(1, -1),
        s_ffn[0:3].reshape(1, -1),
        s_fin[3:4, :LANES],
    ]
    sizes = [a.shape[1] for a in small]
    offs = [0]
    for n in sizes:
        offs.append(offs[-1] + n)
    pack_g = _pad_to(jnp.concatenate(small, axis=1).reshape(-1), SUBLANES * LANES * SUBLANES).reshape(SUBLANES, -1)
    gathered = _all_gather8(pack_g, "ag8_small_grads")
    tot = _sum_devices(gathered).reshape(-1)
    part_of = lambda k: tot[offs[k] : offs[k + 1]]
    dmod_all = gathered.reshape(N_DEV, -1)[:, : N_MOD * d]
    loss = part_of(11)[0]

    g_b_ada = part_of(0).reshape(1, -1)
    g_vecs = [part_of(k).reshape(1, -1) for k in range(1, 9)]
    g_cw_mix = lax.dynamic_slice(part_of(9).reshape(CONV_K, cwid), (0, chip * (cwid // N_CHIP)), (CONV_K, cwid // N_CHIP))
    g_cw_ffn = lax.dynamic_slice(part_of(10).reshape(CONV_K, f2), (0, chip * (f2 // N_CHIP)), (CONV_K, f2 // N_CHIP))

    swap = _swap_start(part, "swap_start")
    dm_cols = _behind(lax.dynamic_slice(dmod_all, (0, chip * nc_ada), (N_DEV, nc_ada)), swap)
    g_w_ada = _ada_grad(
        jnp.pad(c_act.T, ((0, 0), (0, LANES - N_DEV))), jnp.pad(dm_cols, ((0, LANES - N_DEV), (0, 0)))
    )
    big = {"w_ada": [a[None] for a in _adamw(w_ada[0], m_w_ada[0], v_w_ada[0], [g_w_ada], "adamw_w_ada")]}
    part, other = _swap_wait(swap, big["w_ada"][1], "swap_wait")

    big_w = [w_in, w_uq, w_ukv, w_o, w_up, w_down]
    big_m = [m_w_in, m_w_uq, m_w_ukv, m_w_o, m_w_up, m_w_down]
    big_v = [v_w_in, v_w_uq, v_w_ukv, v_w_o, v_w_up, v_w_down]
    for n, w_, m_, v_, p_, o_ in zip(names, big_w, big_m, big_v, part, other):
        big[n] = [a[None] for a in _adamw(w_[0], m_[0], v_[0], [p_, o_], "adamw_" + n)]

    sm_names = ["b_ada", "g_pre_mix", "g_post_mix", "g_q", "g_kv", "conv_b_mix", "g_pre_ffn", "g_post_ffn", "conv_b_ffn",
                "conv_w_mix", "conv_w_ffn"]
    sm_w = [b_ada, g_pre_mix, g_post_mix, g_q, g_kv, conv_b_mix, g_pre_ffn, g_post_ffn, conv_b_ffn, conv_w_mix, conv_w_ffn]
    sm_m = [m_b_ada, m_g_pre_mix, m_g_post_mix, m_g_q, m_g_kv, m_conv_b_mix, m_g_pre_ffn, m_g_post_ffn, m_conv_b_ffn,
            m_conv_w_mix, m_conv_w_ffn]
    sm_v = [v_b_ada, v_g_pre_mix, v_g_post_mix, v_g_q, v_g_kv, v_conv_b_mix, v_g_pre_ffn, v_g_post_ffn, v_conv_b_ffn,
            v_conv_w_mix, v_conv_w_ffn]
    sm_g = [g_b_ada] + g_vecs + [g_cw_mix, g_cw_ffn]
    flat = lambda arrs: jnp.concatenate([a.reshape(1, -1) for a in arrs], axis=1)
    sm_out = _adamw(flat(sm_w), flat(sm_m), flat(sm_v), [flat(sm_g)], "adamw_small")
    sm = {}
    off = 0
    for n, w_ in zip(sm_names, sm_w):
        sm[n] = [o[:, off : off + w_.size].reshape(w_.shape) for o in sm_out]
        off += w_.size

    order = ["w_ada", "b_ada", "g_pre_mix", "g_post_mix", "w_in", "g_q", "w_uq", "g_kv", "w_ukv", "conv_w_mix", "conv_b_mix",
             "w_o", "g_pre_ffn", "g_post_ffn", "w_up", "conv_w_ffn", "conv_b_ffn", "w_down"]
    res = {**big, **sm}
    outs = [loss + sum(anchors), grad_x.reshape(x.shape)]
    for k in range(4):
        outs += [res[n][k] for n in order]
    return tuple(outs)
```
